```python
import math
import jax, jax.numpy as jnp
from jax import lax
import numpy as np

D_MODEL = 1024
BATCH = 8
SEQ = 4096
DEPTH = 4

MLA_HEADS = 8
MLA_Q_LORA = 256
MLA_KV_LORA = 128
MLA_NOPE_DIM = 64
MLA_ROPE_DIM = 32
MLA_V_DIM = 64
ROPE_THETA = 10000.0
Q_BLOCK = 128
SC_DIM = 256
SC_WIDTH = 3
SSD_HEADS = 4
SSD_HEAD_DIM = 64
SSD_GROUPS = 2
SSD_STATE = 128
SSD_CONV_WIDTH = 4
SSD_CHUNK = 128
FFN_DIM = 2816
FFN_CONV_WIDTH = 3
NORM_EPS = 1e-6

MLA_QK_DIM = MLA_NOPE_DIM + MLA_ROPE_DIM
MLA_OUT = MLA_HEADS * MLA_V_DIM
SSD_DIM = SSD_HEADS * SSD_HEAD_DIM
SSD_CONV_DIM = SSD_DIM + 2 * SSD_GROUPS * SSD_STATE
SSD_IN = SSD_DIM + SSD_CONV_DIM + SSD_HEADS
IN_WIDTHS = (MLA_Q_LORA, MLA_KV_LORA, MLA_ROPE_DIM, SC_DIM, SC_DIM, SC_DIM, SSD_IN)
IN_SPLITS = tuple(int(v) for v in np.cumsum(IN_WIDTHS)[:-1])
D_IN = sum(IN_WIDTHS)
D_MIX = MLA_OUT + SC_DIM + SSD_DIM

kernel_name = "hybrid_mla_shortconv_ssd_convffn"


def rms_norm(x, w):
    xf = x.astype(jnp.float32)
    y = xf * lax.rsqrt(jnp.mean(xf * xf, axis=-1, keepdims=True) + NORM_EPS)
    return (y * w.astype(jnp.float32)).astype(x.dtype)


def causal_dwconv(u, w):
    width = w.shape[0]
    s = u.shape[1]
    up = jnp.pad(u, ((0, 0), (width - 1, 0), (0, 0)))
    out = up[:, 0:s] * w[0]
    for i in range(1, width):
        out = out + up[:, i:i + s] * w[i]
    return out


def rope(x, cos, sin):
    x1, x2 = jnp.split(x, 2, axis=-1)
    return jnp.concatenate([x1 * cos - x2 * sin, x2 * cos + x1 * sin], axis=-1).astype(x.dtype)


def mla_mixer(c_q, c_kv, k_rope, cos, sin, q_norm, w_q_up, kv_norm, w_kv_up):
    b, s, _ = c_q.shape
    q = (rms_norm(c_q, q_norm) @ w_q_up).reshape(b, s, MLA_HEADS, MLA_QK_DIM)
    q_nope = q[..., :MLA_NOPE_DIM]
    q_rope = rope(q[..., MLA_NOPE_DIM:], cos[:, :, None, :], sin[:, :, None, :])
    kv = (rms_norm(c_kv, kv_norm) @ w_kv_up).reshape(b, s, MLA_HEADS, MLA_NOPE_DIM + MLA_V_DIM)
    k_nope = kv[..., :MLA_NOPE_DIM]
    v = kv[..., MLA_NOPE_DIM:]
    k_rope = rope(k_rope, cos, sin)
    scale = MLA_QK_DIM ** -0.5
    key_idx = jnp.arange(s)

    def block(i):
        start = i * Q_BLOCK
        qn = lax.dynamic_slice_in_dim(q_nope, start, Q_BLOCK, axis=1)
        qr = lax.dynamic_slice_in_dim(q_rope, start, Q_BLOCK, axis=1)
        sc = jnp.einsum("bqhd,bkhd->bhqk", qn, k_nope) + jnp.einsum("bqhd,bkd->bhqk", qr, k_rope)
        sc = sc.astype(jnp.float32) * scale
        causal = (start + jnp.arange(Q_BLOCK))[:, None] >= key_idx[None, :]
        p = jax.nn.softmax(jnp.where(causal, sc, -jnp.inf), axis=-1)
        return jnp.einsum("bhqk,bkhd->bqhd", p.astype(v.dtype), v)

    out = lax.map(block, jnp.arange(s // Q_BLOCK))
    return out.transpose(1, 0, 2, 3, 4).reshape(b, s, MLA_OUT)


def short_conv_mixer(gate_b, gate_c, h, conv_w):
    return gate_b * causal_dwconv(gate_c * h, conv_w)


def ssd_scan(xdt, a_dt, bh, ch):
    b, s, h, p = xdt.shape
    n = bh.shape[-1]
    L = SSD_CHUNK
    c = s // L
    xdt = xdt.reshape(b, c, L, h, p)
    bh = bh.reshape(b, c, L, h, n)
    ch = ch.reshape(b, c, L, h, n)
    a_cs = jnp.cumsum(a_dt.reshape(b, c, L, h).transpose(0, 3, 1, 2), axis=-1)
    diff = a_cs[..., :, None] - a_cs[..., None, :]
    tri = jnp.tril(jnp.ones((L, L), dtype=bool))
    decay_in = jnp.exp(jnp.where(tri, diff, -jnp.inf))
    scores = jnp.einsum("bclhn,bcshn->bhcls", ch, bh) * decay_in
    y_diag = jnp.einsum("bhcls,bcshp->bclhp", scores, xdt)
    decay_to_end = jnp.exp(a_cs[..., -1:] - a_cs).transpose(0, 2, 3, 1)
    chunk_states = jnp.einsum("bclhn,bclhp->bchpn", bh * decay_to_end[..., None], xdt)
    chunk_decay = jnp.exp(a_cs[..., -1]).transpose(2, 0, 1)

    def step(state, inp):
        st, dec = inp
        return state * dec[..., None, None] + st, state

    init = jnp.zeros((b, h, p, n), chunk_states.dtype)
    _, prev = lax.scan(step, init, (chunk_states.transpose(1, 0, 2, 3, 4), chunk_decay))
    prev = prev.transpose(1, 0, 2, 3, 4)
    decay_from_start = jnp.exp(a_cs).transpose(0, 2, 3, 1)
    y_off = jnp.einsum("bclhn,bchpn->bclhp", ch, prev) * decay_from_start[..., None]
    return (y_diag + y_off).reshape(b, s, h, p)


def ssd_mixer(zxbcdt, conv_w, conv_b, dt_bias, a_log, d_skip, norm_w):
    b, s, _ = zxbcdt.shape
    z = zxbcdt[..., :SSD_DIM]
    xbc = zxbcdt[..., SSD_DIM:SSD_DIM + SSD_CONV_DIM]
    dt = zxbcdt[..., SSD_DIM + SSD_CONV_DIM:]
    xbc = jax.nn.silu(causal_dwconv(xbc, conv_w) + conv_b)
    xs = xbc[..., :SSD_DIM].reshape(b, s, SSD_HEADS, SSD_HEAD_DIM)
    heads_per_group = SSD_HEADS // SSD_GROUPS
    gn = SSD_GROUPS * SSD_STATE
    bm = jnp.repeat(xbc[..., SSD_DIM:SSD_DIM + gn].reshape(b, s, SSD_GROUPS, SSD_STATE), heads_per_group, axis=2)
    cm = jnp.repeat(xbc[..., SSD_DIM + gn:].reshape(b, s, SSD_GROUPS, SSD_STATE), heads_per_group, axis=2)
    dt = jax.nn.softplus(dt.astype(jnp.float32) + dt_bias.astype(jnp.float32))
    a = -jnp.exp(a_log.astype(jnp.float32))
    y = ssd_scan(xs * dt[..., None], dt * a, bm, cm)
    y = y + xs * d_skip[:, None]
    y = y.reshape(b, s, SSD_DIM).astype(zxbcdt.dtype)
    return rms_norm(y * jax.nn.silu(z), norm_w)


def _fwd_setup_inputs(seed: int = 0) -> dict:
    key = jax.random.key(seed)
    ks = jax.random.split(key, 24)
    L = DEPTH

    def normal(k, shape, scale):
        return scale * jax.random.normal(k, shape, jnp.float32)

    def gain(k, n):
        return 1.0 + normal(k, (L, n), 0.02)

    dt0 = jnp.exp(jax.random.uniform(ks[15], (L, SSD_HEADS), jnp.float32, math.log(1e-3), math.log(1e-1)))
    return {
        "x": normal(ks[0], (BATCH, SEQ, D_MODEL), 1.0),
        "positions": jnp.broadcast_to(jnp.arange(SEQ, dtype=jnp.int32), (BATCH, SEQ)),
        "norm_mix_pre": gain(ks[1], D_MODEL),
        "norm_mix_post": gain(ks[2], D_MODEL),
        "norm_ffn_pre": gain(ks[3], D_MODEL),
        "norm_ffn_post": gain(ks[4], D_MODEL),
        "w_in": normal(ks[5], (L, D_MODEL, D_IN), D_MODEL ** -0.5),
        "mla_q_norm": gain(ks[6], MLA_Q_LORA),
        "mla_w_q_up": normal(ks[7], (L, MLA_Q_LORA, MLA_HEADS * MLA_QK_DIM), MLA_Q_LORA ** -0.5),
        "mla_kv_norm": gain(ks[8], MLA_KV_LORA),
        "mla_w_kv_up": normal(ks[9], (L, MLA_KV_LORA, MLA_HEADS * (MLA_NOPE_DIM + MLA_V_DIM)), MLA_KV_LORA ** -0.5),
        "sc_conv_w": normal(ks[10], (L, SC_WIDTH, SC_DIM), SC_WIDTH ** -0.5),
        "ssd_conv_w": normal(ks[11], (L, SSD_CONV_WIDTH, SSD_CONV_DIM), SSD_CONV_WIDTH ** -0.5),
        "ssd_conv_b": normal(ks[12], (L, SSD_CONV_DIM), 0.02),
        "ssd_dt_bias": dt0 + jnp.log(-jnp.expm1(-dt0)),
        "ssd_a_log": jnp.log(jax.random.uniform(ks[13], (L, SSD_HEADS), jnp.float32, 1.0, 16.0)),
        "ssd_d": 1.0 + normal(ks[14], (L, SSD_HEADS), 0.1),
        "ssd_norm": gain(ks[16], SSD_DIM),
        "w_out": normal(ks[17], (L, D_MIX, D_MODEL), D_MIX ** -0.5),
        "ffn_w_up": normal(ks[18], (L, D_MODEL, 2 * FFN_DIM), D_MODEL ** -0.5),
        "ffn_conv_w": normal(ks[19], (L, FFN_CONV_WIDTH, 2 * FFN_DIM), FFN_CONV_WIDTH ** -0.5),
        "ffn_conv_b": normal(ks[20], (L, 2 * FFN_DIM), 0.02),
        "ffn_w_down": normal(ks[21], (L, FFN_DIM, D_MODEL), FFN_DIM ** -0.5),
    }


def _fwd_reference(x, positions, norm_mix_pre, norm_mix_post, norm_ffn_pre, norm_ffn_post, w_in,
              mla_q_norm, mla_w_q_up, mla_kv_norm, mla_w_kv_up, sc_conv_w, ssd_conv_w, ssd_conv_b,
              ssd_dt_bias, ssd_a_log, ssd_d, ssd_norm, w_out, ffn_w_up, ffn_conv_w, ffn_conv_b,
              ffn_w_down):
    inv_freq = 1.0 / (ROPE_THETA ** (jnp.arange(0, MLA_ROPE_DIM, 2, dtype=jnp.float32) / MLA_ROPE_DIM))
    ang = positions.astype(jnp.float32)[..., None] * inv_freq
    cos = jnp.cos(ang).astype(x.dtype)
    sin = jnp.sin(ang).astype(x.dtype)
    for l in range(DEPTH):
        h = rms_norm(x, norm_mix_pre[l])
        c_q, c_kv, k_rope, sc_b, sc_c, sc_h, ssd_in = jnp.split(h @ w_in[l], IN_SPLITS, axis=-1)
        y_att = mla_mixer(c_q, c_kv, k_rope, cos, sin, mla_q_norm[l], mla_w_q_up[l], mla_kv_norm[l], mla_w_kv_up[l])
        y_conv = short_conv_mixer(sc_b, sc_c, sc_h, sc_conv_w[l])
        y_ssd = ssd_mixer(ssd_in, ssd_conv_w[l], ssd_conv_b[l], ssd_dt_bias[l], ssd_a_log[l], ssd_d[l], ssd_norm[l])
        mixed = jnp.concatenate([y_att, y_conv, y_ssd], axis=-1) @ w_out[l]
        x = x + rms_norm(mixed, norm_mix_post[l])
        h = rms_norm(x, norm_ffn_pre[l])
        u = causal_dwconv(h @ ffn_w_up[l], ffn_conv_w[l]) + ffn_conv_b[l]
        gate, up = jnp.split(u, 2, axis=-1)
        x = x + rms_norm((jax.nn.silu(gate) * up) @ ffn_w_down[l], norm_ffn_post[l])
    return x


import jax as _jax
import jax.numpy as _jnp

TWIN_FORMAT = 'train_step'
FWD_PARAMS = ['x', 'positions', 'norm_mix_pre', 'norm_mix_post', 'norm_ffn_pre', 'norm_ffn_post', 'w_in', 'mla_q_norm', 'mla_w_q_up', 'mla_kv_norm', 'mla_w_kv_up', 'sc_conv_w', 'ssd_conv_w', 'ssd_conv_b', 'ssd_dt_bias', 'ssd_a_log', 'ssd_d', 'ssd_norm', 'w_out', 'ffn_w_up', 'ffn_conv_w', 'ffn_conv_b', 'ffn_w_down']
TWIN_WEIGHTS = ['norm_mix_pre', 'norm_mix_post', 'norm_ffn_pre', 'norm_ffn_post', 'w_in', 'mla_q_norm', 'mla_w_q_up', 'mla_kv_norm', 'mla_w_kv_up', 'sc_conv_w', 'ssd_conv_w', 'ssd_conv_b', 'ssd_dt_bias', 'ssd_a_log', 'ssd_d', 'ssd_norm', 'w_out', 'ffn_w_up', 'ffn_conv_w', 'ffn_conv_b', 'ffn_w_down']
TWIN_DIFF_INPUT = 'x'
TWIN_INPUTS = ['x', 'positions', 'norm_mix_pre', 'norm_mix_post', 'norm_ffn_pre', 'norm_ffn_post', 'w_in', 'mla_q_norm', 'mla_w_q_up', 'mla_kv_norm', 'mla_w_kv_up', 'sc_conv_w', 'ssd_conv_w', 'ssd_conv_b', 'ssd_dt_bias', 'ssd_a_log', 'ssd_d', 'ssd_norm', 'w_out', 'ffn_w_up', 'ffn_conv_w', 'ffn_conv_b', 'ffn_w_down', 'loss_target', 'm_norm_mix_pre', 'm_norm_mix_post', 'm_norm_ffn_pre', 'm_norm_ffn_post', 'm_w_in', 'm_mla_q_norm', 'm_mla_w_q_up', 'm_mla_kv_norm', 'm_mla_w_kv_up', 'm_sc_conv_w', 'm_ssd_conv_w', 'm_ssd_conv_b', 'm_ssd_dt_bias', 'm_ssd_a_log', 'm_ssd_d', 'm_ssd_norm', 'm_w_out', 'm_ffn_w_up', 'm_ffn_conv_w', 'm_ffn_conv_b', 'm_ffn_w_down', 'v_norm_mix_pre', 'v_norm_mix_post', 'v_norm_ffn_pre', 'v_norm_ffn_post', 'v_w_in', 'v_mla_q_norm', 'v_mla_w_q_up', 'v_mla_kv_norm', 'v_mla_w_kv_up', 'v_sc_conv_w', 'v_ssd_conv_w', 'v_ssd_conv_b', 'v_ssd_dt_bias', 'v_ssd_a_log', 'v_ssd_d', 'v_ssd_norm', 'v_w_out', 'v_ffn_w_up', 'v_ffn_conv_w', 'v_ffn_conv_b', 'v_ffn_w_down']
TWIN_OUTPUTS = ['loss', 'grad_x', 'grad_norm_mix_pre', 'grad_norm_mix_post', 'grad_norm_ffn_pre', 'grad_norm_ffn_post', 'grad_w_in', 'grad_mla_q_norm', 'grad_mla_w_q_up', 'grad_mla_kv_norm', 'grad_mla_w_kv_up', 'grad_sc_conv_w', 'grad_ssd_conv_w', 'grad_ssd_conv_b', 'grad_ssd_dt_bias', 'grad_ssd_a_log', 'grad_ssd_d', 'grad_ssd_norm', 'grad_w_out', 'grad_ffn_w_up', 'grad_ffn_conv_w', 'grad_ffn_conv_b', 'grad_ffn_w_down', 'delta_norm_mix_pre', 'delta_norm_mix_post', 'delta_norm_ffn_pre', 'delta_norm_ffn_post', 'delta_w_in', 'delta_mla_q_norm', 'delta_mla_w_q_up', 'delta_mla_kv_norm', 'delta_mla_w_kv_up', 'delta_sc_conv_w', 'delta_ssd_conv_w', 'delta_ssd_conv_b', 'delta_ssd_dt_bias', 'delta_ssd_a_log', 'delta_ssd_d', 'delta_ssd_norm', 'delta_w_out', 'delta_ffn_w_up', 'delta_ffn_conv_w', 'delta_ffn_conv_b', 'delta_ffn_w_down', 'new_m_norm_mix_pre', 'new_m_norm_mix_post', 'new_m_norm_ffn_pre', 'new_m_norm_ffn_post', 'new_m_w_in', 'new_m_mla_q_norm', 'new_m_mla_w_q_up', 'new_m_mla_kv_norm', 'new_m_mla_w_kv_up', 'new_m_sc_conv_w', 'new_m_ssd_conv_w', 'new_m_ssd_conv_b', 'new_m_ssd_dt_bias', 'new_m_ssd_a_log', 'new_m_ssd_d', 'new_m_ssd_norm', 'new_m_w_out', 'new_m_ffn_w_up', 'new_m_ffn_conv_w', 'new_m_ffn_conv_b', 'new_m_ffn_w_down', 'new_v_norm_mix_pre', 'new_v_norm_mix_post', 'new_v_norm_ffn_pre', 'new_v_norm_ffn_post', 'new_v_w_in', 'new_v_mla_q_norm', 'new_v_mla_w_q_up', 'new_v_mla_kv_norm', 'new_v_mla_w_kv_up', 'new_v_sc_conv_w', 'new_v_ssd_conv_w', 'new_v_ssd_conv_b', 'new_v_ssd_dt_bias', 'new_v_ssd_a_log', 'new_v_ssd_d', 'new_v_ssd_norm', 'new_v_w_out', 'new_v_ffn_w_up', 'new_v_ffn_conv_w', 'new_v_ffn_conv_b', 'new_v_ffn_w_down']
TWIN_LEAF_KINDS = {'loss': 'loss', 'grad_x': 'grad_x', 'grad_norm_mix_pre': 'grad_w', 'grad_norm_mix_post': 'grad_w', 'grad_norm_ffn_pre': 'grad_w', 'grad_norm_ffn_post': 'grad_w', 'grad_w_in': 'grad_w', 'grad_mla_q_norm': 'grad_w', 'grad_mla_w_q_up': 'grad_w', 'grad_mla_kv_norm': 'grad_w', 'grad_mla_w_kv_up': 'grad_w', 'grad_sc_conv_w': 'grad_w', 'grad_ssd_conv_w': 'grad_w', 'grad_ssd_conv_b': 'grad_w', 'grad_ssd_dt_bias': 'grad_w', 'grad_ssd_a_log': 'grad_w', 'grad_ssd_d': 'grad_w', 'grad_ssd_norm': 'grad_w', 'grad_w_out': 'grad_w', 'grad_ffn_w_up': 'grad_w', 'grad_ffn_conv_w': 'grad_w', 'grad_ffn_conv_b': 'grad_w', 'grad_ffn_w_down': 'grad_w', 'delta_norm_mix_pre': 'delta_w', 'delta_norm_mix_post': 'delta_w', 'delta_norm_ffn_pre': 'delta_w', 'delta_norm_ffn_post': 'delta_w', 'delta_w_in': 'delta_w', 'delta_mla_q_norm': 'delta_w', 'delta_mla_w_q_up': 'delta_w', 'delta_mla_kv_norm': 'delta_w', 'delta_mla_w_kv_up': 'delta_w', 'delta_sc_conv_w': 'delta_w', 'delta_ssd_conv_w': 'delta_w', 'delta_ssd_conv_b': 'delta_w', 'delta_ssd_dt_bias': 'delta_w', 'delta_ssd_a_log': 'delta_w', 'delta_ssd_d': 'delta_w', 'delta_ssd_norm': 'delta_w', 'delta_w_out': 'delta_w', 'delta_ffn_w_up': 'delta_w', 'delta_ffn_conv_w': 'delta_w', 'delta_ffn_conv_b': 'delta_w', 'delta_ffn_w_down': 'delta_w', 'new_m_norm_mix_pre': 'new_m', 'new_m_norm_mix_post': 'new_m', 'new_m_norm_ffn_pre': 'new_m', 'new_m_norm_ffn_post': 'new_m', 'new_m_w_in': 'new_m', 'new_m_mla_q_norm': 'new_m', 'new_m_mla_w_q_up': 'new_m', 'new_m_mla_kv_norm': 'new_m', 'new_m_mla_w_kv_up': 'new_m', 'new_m_sc_conv_w': 'new_m', 'new_m_ssd_conv_w': 'new_m', 'new_m_ssd_conv_b': 'new_m', 'new_m_ssd_dt_bias': 'new_m', 'new_m_ssd_a_log': 'new_m', 'new_m_ssd_d': 'new_m', 'new_m_ssd_norm': 'new_m', 'new_m_w_out': 'new_m', 'new_m_ffn_w_up': 'new_m', 'new_m_ffn_conv_w': 'new_m', 'new_m_ffn_conv_b': 'new_m', 'new_m_ffn_w_down': 'new_m', 'new_v_norm_mix_pre': 'new_v', 'new_v_norm_mix_post': 'new_v', 'new_v_norm_ffn_pre': 'new_v', 'new_v_norm_ffn_post': 'new_v', 'new_v_w_in': 'new_v', 'new_v_mla_q_norm': 'new_v', 'new_v_mla_w_q_up': 'new_v', 'new_v_mla_kv_norm': 'new_v', 'new_v_mla_w_kv_up': 'new_v', 'new_v_sc_conv_w': 'new_v', 'new_v_ssd_conv_w': 'new_v', 'new_v_ssd_conv_b': 'new_v', 'new_v_ssd_dt_bias': 'new_v', 'new_v_ssd_a_log': 'new_v', 'new_v_ssd_d': 'new_v', 'new_v_ssd_norm': 'new_v', 'new_v_w_out': 'new_v', 'new_v_ffn_w_up': 'new_v', 'new_v_ffn_conv_w': 'new_v', 'new_v_ffn_conv_b': 'new_v', 'new_v_ffn_w_down': 'new_v'}


def _forward(args):
    return _fwd_reference(*[args[k] for k in FWD_PARAMS])


def _output_shape():
    out = _jax.eval_shape(lambda: _forward(_fwd_setup_inputs(0)))
    return out.shape, out.dtype

N_MICROBATCH = 1
ADAM_LR = 0.001
ADAM_B1 = 0.9
ADAM_B2 = 0.999
ADAM_EPS = 1e-08
ADAM_WD = 0.01
ADAM_STEP = 10
PER_EXAMPLE_BATCH_AXIS = {'x': 0, 'positions': 0, 'loss_target': 0}
SHARED_INPUTS = []
_WEIGHT_DTYPES = {'norm_mix_pre': _jnp.float32, 'norm_mix_post': _jnp.float32, 'norm_ffn_pre': _jnp.float32, 'norm_ffn_post': _jnp.float32, 'w_in': _jnp.float32, 'mla_q_norm': _jnp.float32, 'mla_w_q_up': _jnp.float32, 'mla_kv_norm': _jnp.float32, 'mla_w_kv_up': _jnp.float32, 'sc_conv_w': _jnp.float32, 'ssd_conv_w': _jnp.float32, 'ssd_conv_b': _jnp.float32, 'ssd_dt_bias': _jnp.float32, 'ssd_a_log': _jnp.float32, 'ssd_d': _jnp.float32, 'ssd_norm': _jnp.float32, 'w_out': _jnp.float32, 'ffn_w_up': _jnp.float32, 'ffn_conv_w': _jnp.float32, 'ffn_conv_b': _jnp.float32, 'ffn_w_down': _jnp.float32}
MOMENT_SCALE = {'norm_mix_pre': 2.489747e+00, 'norm_mix_post': 3.146824e+01, 'norm_ffn_pre': 1.611442e+00, 'norm_ffn_post': 3.164393e+01, 'w_in': 1.696888e+00, 'mla_q_norm': 5.274611e-01, 'mla_w_q_up': 3.095609e-01, 'mla_kv_norm': 1.408331e+00, 'mla_w_kv_up': 4.671094e-01, 'sc_conv_w': 2.084694e+00, 'ssd_conv_w': 1.564639e+00, 'ssd_conv_b': 3.586432e+00, 'ssd_dt_bias': 4.870190e+00, 'ssd_a_log': 2.627886e+01, 'ssd_d': 1.705529e+01, 'ssd_norm': 3.035315e+00, 'w_out': 1.830392e+00, 'ffn_w_up': 6.988736e-01, 'ffn_conv_w': 7.158310e-01, 'ffn_conv_b': 1.626703e+00, 'ffn_w_down': 1.199837e+00}


def _to_microbatches(a, axis):
    t = _jnp.moveaxis(a, axis, 0)
    t = t.reshape((N_MICROBATCH, t.shape[0] // N_MICROBATCH) + t.shape[1:])
    return _jnp.moveaxis(t, 1, axis + 1)


def setup_inputs(seed: int = 0) -> dict:
    inp = _fwd_setup_inputs(seed)
    key = _jax.random.fold_in(_jax.random.key(seed), 7919)
    shape, _ = _output_shape()
    out = dict(inp)
    out["loss_target"] = _jax.random.normal(_jax.random.fold_in(key, 0), shape, _jnp.float32)
    for i, name in enumerate(TWIN_WEIGHTS):
        w = inp[name].astype(_jnp.float32)
        if MOMENT_SCALE is None:
            s = _jnp.sqrt(_jnp.mean(_jnp.square(w)) + 1e-30)
        else:
            s = MOMENT_SCALE[name]
        km, kv = _jax.random.split(_jax.random.fold_in(key, i + 1))
        out[name] = w
        out["m_" + name] = s * _jax.random.normal(km, w.shape, _jnp.float32)
        out["v_" + name] = (s * s) * _jax.random.uniform(kv, w.shape, _jnp.float32, 0.5, 1.5)
    if N_MICROBATCH > 1:
        for name, axis in PER_EXAMPLE_BATCH_AXIS.items():
            out[name] = _to_microbatches(out[name], axis)
    return {'x': out['x'], 'positions': out['positions'], 'norm_mix_pre': out['norm_mix_pre'], 'norm_mix_post': out['norm_mix_post'], 'norm_ffn_pre': out['norm_ffn_pre'], 'norm_ffn_post': out['norm_ffn_post'], 'w_in': out['w_in'], 'mla_q_norm': out['mla_q_norm'], 'mla_w_q_up': out['mla_w_q_up'], 'mla_kv_norm': out['mla_kv_norm'], 'mla_w_kv_up': out['mla_w_kv_up'], 'sc_conv_w': out['sc_conv_w'], 'ssd_conv_w': out['ssd_conv_w'], 'ssd_conv_b': out['ssd_conv_b'], 'ssd_dt_bias': out['ssd_dt_bias'], 'ssd_a_log': out['ssd_a_log'], 'ssd_d': out['ssd_d'], 'ssd_norm': out['ssd_norm'], 'w_out': out['w_out'], 'ffn_w_up': out['ffn_w_up'], 'ffn_conv_w': out['ffn_conv_w'], 'ffn_conv_b': out['ffn_conv_b'], 'ffn_w_down': out['ffn_w_down'], 'loss_target': out['loss_target'], 'm_norm_mix_pre': out['m_norm_mix_pre'], 'm_norm_mix_post': out['m_norm_mix_post'], 'm_norm_ffn_pre': out['m_norm_ffn_pre'], 'm_norm_ffn_post': out['m_norm_ffn_post'], 'm_w_in': out['m_w_in'], 'm_mla_q_norm': out['m_mla_q_norm'], 'm_mla_w_q_up': out['m_mla_w_q_up'], 'm_mla_kv_norm': out['m_mla_kv_norm'], 'm_mla_w_kv_up': out['m_mla_w_kv_up'], 'm_sc_conv_w': out['m_sc_conv_w'], 'm_ssd_conv_w': out['m_ssd_conv_w'], 'm_ssd_conv_b': out['m_ssd_conv_b'], 'm_ssd_dt_bias': out['m_ssd_dt_bias'], 'm_ssd_a_log': out['m_ssd_a_log'], 'm_ssd_d': out['m_ssd_d'], 'm_ssd_norm': out['m_ssd_norm'], 'm_w_out': out['m_w_out'], 'm_ffn_w_up': out['m_ffn_w_up'], 'm_ffn_conv_w': out['m_ffn_conv_w'], 'm_ffn_conv_b': out['m_ffn_conv_b'], 'm_ffn_w_down': out['m_ffn_w_down'], 'v_norm_mix_pre': out['v_norm_mix_pre'], 'v_norm_mix_post': out['v_norm_mix_post'], 'v_norm_ffn_pre': out['v_norm_ffn_pre'], 'v_norm_ffn_post': out['v_norm_ffn_post'], 'v_w_in': out['v_w_in'], 'v_mla_q_norm': out['v_mla_q_norm'], 'v_mla_w_q_up': out['v_mla_w_q_up'], 'v_mla_kv_norm': out['v_mla_kv_norm'], 'v_mla_w_kv_up': out['v_mla_w_kv_up'], 'v_sc_conv_w': out['v_sc_conv_w'], 'v_ssd_conv_w': out['v_ssd_conv_w'], 'v_ssd_conv_b': out['v_ssd_conv_b'], 'v_ssd_dt_bias': out['v_ssd_dt_bias'], 'v_ssd_a_log': out['v_ssd_a_log'], 'v_ssd_d': out['v_ssd_d'], 'v_ssd_norm': out['v_ssd_norm'], 'v_w_out': out['v_w_out'], 'v_ffn_w_up': out['v_ffn_w_up'], 'v_ffn_conv_w': out['v_ffn_conv_w'], 'v_ffn_conv_b': out['v_ffn_conv_b'], 'v_ffn_w_down': out['v_ffn_w_down']}


def _loss(weights, diff, rest, loss_target):
    with _jax.named_scope("forward"):
        args = {**rest, TWIN_DIFF_INPUT: diff, **{k: w.astype(_WEIGHT_DTYPES[k]) for k, w in weights.items()}}
        y = _forward(args)
    with _jax.named_scope("loss_head"):
        err = _jnp.square(y.astype(_jnp.float32) - loss_target)
        return 0.5 * _jnp.sum(_jnp.mean(err, axis=-1)) if err.ndim else 0.5 * err


def _adamw(w, g, m, v):
    m = ADAM_B1 * m + (1.0 - ADAM_B1) * g
    v = ADAM_B2 * v + (1.0 - ADAM_B2) * _jnp.square(g)
    m_hat = m / (1.0 - ADAM_B1 ** ADAM_STEP)
    v_hat = v / (1.0 - ADAM_B2 ** ADAM_STEP)
    delta = -ADAM_LR * (m_hat / (_jnp.sqrt(v_hat) + ADAM_EPS) + ADAM_WD * w)
    return delta, m, v


def reference(x, positions, norm_mix_pre, norm_mix_post, norm_ffn_pre, norm_ffn_post, w_in, mla_q_norm, mla_w_q_up, mla_kv_norm, mla_w_kv_up, sc_conv_w, ssd_conv_w, ssd_conv_b, ssd_dt_bias, ssd_a_log, ssd_d, ssd_norm, w_out, ffn_w_up, ffn_conv_w, ffn_conv_b, ffn_w_down, loss_target, m_norm_mix_pre, m_norm_mix_post, m_norm_ffn_pre, m_norm_ffn_post, m_w_in, m_mla_q_norm, m_mla_w_q_up, m_mla_kv_norm, m_mla_w_kv_up, m_sc_conv_w, m_ssd_conv_w, m_ssd_conv_b, m_ssd_dt_bias, m_ssd_a_log, m_ssd_d, m_ssd_norm, m_w_out, m_ffn_w_up, m_ffn_conv_w, m_ffn_conv_b, m_ffn_w_down, v_norm_mix_pre, v_norm_mix_post, v_norm_ffn_pre, v_norm_ffn_post, v_w_in, v_mla_q_norm, v_mla_w_q_up, v_mla_kv_norm, v_mla_w_kv_up, v_sc_conv_w, v_ssd_conv_w, v_ssd_conv_b, v_ssd_dt_bias, v_ssd_a_log, v_ssd_d, v_ssd_norm, v_w_out, v_ffn_w_up, v_ffn_conv_w, v_ffn_conv_b, v_ffn_w_down):
    given = dict(x=x, positions=positions, norm_mix_pre=norm_mix_pre, norm_mix_post=norm_mix_post, norm_ffn_pre=norm_ffn_pre, norm_ffn_post=norm_ffn_post, w_in=w_in, mla_q_norm=mla_q_norm, mla_w_q_up=mla_w_q_up, mla_kv_norm=mla_kv_norm, mla_w_kv_up=mla_w_kv_up, sc_conv_w=sc_conv_w, ssd_conv_w=ssd_conv_w, ssd_conv_b=ssd_conv_b, ssd_dt_bias=ssd_dt_bias, ssd_a_log=ssd_a_log, ssd_d=ssd_d, ssd_norm=ssd_norm, w_out=w_out, ffn_w_up=ffn_w_up, ffn_conv_w=ffn_conv_w, ffn_conv_b=ffn_conv_b, ffn_w_down=ffn_w_down, loss_target=loss_target, m_norm_mix_pre=m_norm_mix_pre, m_norm_mix_post=m_norm_mix_post, m_norm_ffn_pre=m_norm_ffn_pre, m_norm_ffn_post=m_norm_ffn_post, m_w_in=m_w_in, m_mla_q_norm=m_mla_q_norm, m_mla_w_q_up=m_mla_w_q_up, m_mla_kv_norm=m_mla_kv_norm, m_mla_w_kv_up=m_mla_w_kv_up, m_sc_conv_w=m_sc_conv_w, m_ssd_conv_w=m_ssd_conv_w, m_ssd_conv_b=m_ssd_conv_b, m_ssd_dt_bias=m_ssd_dt_bias, m_ssd_a_log=m_ssd_a_log, m_ssd_d=m_ssd_d, m_ssd_norm=m_ssd_norm, m_w_out=m_w_out, m_ffn_w_up=m_ffn_w_up, m_ffn_conv_w=m_ffn_conv_w, m_ffn_conv_b=m_ffn_conv_b, m_ffn_w_down=m_ffn_w_down, v_norm_mix_pre=v_norm_mix_pre, v_norm_mix_post=v_norm_mix_post, v_norm_ffn_pre=v_norm_ffn_pre, v_norm_ffn_post=v_norm_ffn_post, v_w_in=v_w_in, v_mla_q_norm=v_mla_q_norm, v_mla_w_q_up=v_mla_w_q_up, v_mla_kv_norm=v_mla_kv_norm, v_mla_w_kv_up=v_mla_w_kv_up, v_sc_conv_w=v_sc_conv_w, v_ssd_conv_w=v_ssd_conv_w, v_ssd_conv_b=v_ssd_conv_b, v_ssd_dt_bias=v_ssd_dt_bias, v_ssd_a_log=v_ssd_a_log, v_ssd_d=v_ssd_d, v_ssd_norm=v_ssd_norm, v_w_out=v_w_out, v_ffn_w_up=v_ffn_w_up, v_ffn_conv_w=v_ffn_conv_w, v_ffn_conv_b=v_ffn_conv_b, v_ffn_w_down=v_ffn_w_down)
    weights = {n: given[n] for n in TWIN_WEIGHTS}
    shared = {n: given[n] for n in SHARED_INPUTS}
    per_example = {n: given[n] for n in ['x', 'positions']}
    grad_fn = _jax.value_and_grad(_loss, argnums=(0, 1))

    def one_microbatch(ex, loss_target):
        ex = dict(ex)
        diff = ex.pop(TWIN_DIFF_INPUT)
        return grad_fn(weights, diff, {**shared, **ex}, loss_target)

    if N_MICROBATCH == 1:
        loss, (grad_w, grad_x) = one_microbatch(per_example, given["loss_target"])
    else:
        def body(carry, xs):
            loss_sum, grad_sum = carry
            l_k, (gw_k, gx_k) = one_microbatch(xs[0], xs[1])
            with _jax.named_scope("update"):
                return (loss_sum + l_k, _jax.tree.map(_jnp.add, grad_sum, gw_k)), gx_k

        init = (_jnp.zeros((), _jnp.float32), _jax.tree.map(_jnp.zeros_like, weights))
        (loss, grad_w), grad_x = _jax.lax.scan(body, init, (per_example, given["loss_target"]))
    with _jax.named_scope("update"):
        delta_w, new_m, new_v = {}, {}, {}
        for n in TWIN_WEIGHTS:
            delta_w[n], new_m[n], new_v[n] = _adamw(weights[n], grad_w[n], given["m_" + n], given["v_" + n])
    return (loss, grad_x, *[grad_w[n] for n in TWIN_WEIGHTS], *[delta_w[n] for n in TWIN_WEIGHTS],
            *[new_m[n] for n in TWIN_WEIGHTS], *[new_v[n] for n in TWIN_WEIGHTS])
```

```python
import functools
import math

import jax
import jax.numpy as jnp
from jax import lax
from jax.experimental import pallas as pl
from jax.experimental.pallas import tpu as pltpu

F32 = jnp.float32
BF16 = jnp.bfloat16

D = 1024
DEPTH = 4
NDEV = 8
HEADS = 8
QL = 256
KVL = 128
ROPE = 32
NOPE = 64
SC = 256
SSD_DIM = 256
SSD_CONV = 768
SSD_H = 4
SSD_L = 128
FFN = 2816
FB = 704
EPS = 1e-6
ROPE_THETA = 10000.0
ATT_SCALE = 96 ** -0.5
LR, B1, B2, AEPS, WD, STEP = 0.001, 0.9, 0.999, 1e-08, 0.01, 10

PW = 2432
CATW = 1536

ROW_TILE = 512
ATT_TILE = 512
HALO = 16
LANE = 128
NEG = -1e30
HI = lax.Precision.HIGHEST
NN = (((1,), (0,)), ((), ()))
NT = (((1,), (1,)), ((), ()))
TN = (((0,), (0,)), ((), ()))
VMEM_LIMIT = 56 * 1024 * 1024

SHARDED = (
    ("w_in", (4, 128, 2212)),
    ("mla_w_q_up", (4, 256, 96)),
    ("mla_w_kv_up", (4, 128, 128)),
    ("sc_conv_w", (4, 3, 32)),
    ("ssd_conv_w", (4, 4, 96)),
    ("w_out", (4, 128, 1024)),
    ("ffn_w_up", (4, 1024, 704)),
    ("ffn_conv_w", (4, 3, 704)),
    ("ffn_w_down", (4, 352, 1024)),
)
SMALL = (
    ("norm_mix_pre", 1024), ("norm_mix_post", 1024), ("norm_ffn_pre", 1024), ("norm_ffn_post", 1024),
    ("mla_q_norm", 256), ("mla_kv_norm", 128), ("ssd_conv_b", 768), ("ssd_dt_bias", 4), ("ssd_a_log", 4),
    ("ssd_d", 4), ("ssd_norm", 256), ("ffn_conv_b", 5632),
)
WEIGHTS = ("norm_mix_pre", "norm_mix_post", "norm_ffn_pre", "norm_ffn_post", "w_in", "mla_q_norm", "mla_w_q_up",
           "mla_kv_norm", "mla_w_kv_up", "sc_conv_w", "ssd_conv_w", "ssd_conv_b", "ssd_dt_bias", "ssd_a_log", "ssd_d",
           "ssd_norm", "w_out", "ffn_w_up", "ffn_conv_w", "ffn_conv_b", "ffn_w_down")


def _dot(a, b, dims=NN, precision=None):
    return lax.dot_general(a, b, dims, precision=precision, preferred_element_type=F32)


def _sig(v):
    return 1.0 / (1.0 + jnp.exp(-v))


def _cp(*sem):
    return pltpu.CompilerParams(dimension_semantics=sem, vmem_limit_bytes=VMEM_LIMIT)


def _rowsum(v):
    return jnp.sum(v, axis=0, keepdims=True)


def _prev_halo(i, ts):
    return jnp.maximum(i * (ts // HALO) - 1, 0)


def _next_halo(i, ts, n):
    return jnp.minimum((i + 1) * (ts // HALO), n * (ts // HALO) - 1)


def _all_gather(xs, name):
    r, c = xs.shape

    def body(x_ref, out_ref, send_sems, recv_sems, local_sem):
        x, y, cc = lax.axis_index("x"), lax.axis_index("y"), lax.axis_index("c")
        me, sibling = (x, y, cc), (x, y, 1 - cc)
        chips = [(1 - x, y), (x, 1 - y), (1 - x, 1 - y)]

        def rows(px, py, pc):
            return out_ref.at[4 * px + 2 * py + pc]

        def copy(k, block, to, src=None):
            return pltpu.make_async_remote_copy(
                src_ref=rows(*block) if src is None else src, dst_ref=rows(*block),
                send_sem=send_sems.at[k], recv_sem=recv_sems.at[k], device_id=to, device_id_type=pl.DeviceIdType.MESH)

        mine = pltpu.make_async_copy(x_ref, rows(*me), local_sem)
        mine.start()
        first = [copy(0, me, sibling, src=x_ref)]
        first += [copy(1 + j, me, (*chip, cc), src=x_ref) for j, chip in enumerate(chips)]
        for cp in first:
            cp.start()
        passed = [copy(4 + j, (*chip, cc), sibling) for j, chip in enumerate(chips)]
        for j, chip in enumerate(chips):
            copy(1 + j, (*chip, cc), me).wait_recv()
            passed[j].start()
        copy(0, sibling, me).wait_recv()
        for j, chip in enumerate(chips):
            copy(4 + j, (*chip, 1 - cc), me).wait_recv()
        for cp in first + passed:
            cp.wait_send()
        mine.wait()

    return pl.pallas_call(
        body, name=name, out_shape=jax.ShapeDtypeStruct((NDEV, r, c), xs.dtype),
        in_specs=[pl.BlockSpec(memory_space=pl.ANY)], out_specs=pl.BlockSpec(memory_space=pl.ANY),
        scratch_shapes=[pltpu.SemaphoreType.DMA((7,)), pltpu.SemaphoreType.DMA((7,)), pltpu.SemaphoreType.DMA],
    )(xs)


def _all_to_all(xs, name):
    _, r, c = xs.shape

    def body(x_ref, out_ref, send_sems, recv_sems, local_sem):
        x, y, cc = lax.axis_index("x"), lax.axis_index("y"), lax.axis_index("c")
        me = 4 * x + 2 * y + cc
        mine = pltpu.make_async_copy(x_ref.at[me], out_ref.at[me], local_sem)
        mine.start()
        copies = []
        for k in range(1, NDEV):
            px = 1 - x if k & 4 else x
            py = 1 - y if k & 2 else y
            pc = 1 - cc if k & 1 else cc
            peer = 4 * px + 2 * py + pc
            copies.append(pltpu.make_async_remote_copy(
                src_ref=x_ref.at[peer], dst_ref=out_ref.at[me], send_sem=send_sems.at[k - 1], recv_sem=recv_sems.at[k - 1],
                device_id=(px, py, pc), device_id_type=pl.DeviceIdType.MESH))
        for cp in copies:
            cp.start()
        for cp in copies:
            cp.wait()
        mine.wait()

    return pl.pallas_call(
        body, name=name, out_shape=jax.ShapeDtypeStruct((NDEV, r, c), xs.dtype),
        in_specs=[pl.BlockSpec(memory_space=pl.ANY)], out_specs=pl.BlockSpec(memory_space=pl.ANY),
        scratch_shapes=[pltpu.SemaphoreType.DMA((7,)), pltpu.SemaphoreType.DMA((7,)), pltpu.SemaphoreType.DMA],
    )(xs)


def _mm(name, a, b, out_shape, grid, a_spec, b_spec, o_spec, dims, acc_shape):
    nk = grid[2]

    def body(a_ref, b_ref, o_ref, acc_ref):
        k = pl.program_id(2)

        @pl.when(k == 0)
        def _():
            acc_ref[...] = jnp.zeros_like(acc_ref)

        acc_ref[...] += _dot(a_ref[...], b_ref[...], dims)

        @pl.when(k == nk - 1)
        def _():
            o_ref[...] = acc_ref[...].astype(o_ref.dtype)

    return pl.pallas_call(
        body, name=name, grid=grid, out_shape=out_shape, in_specs=[a_spec, b_spec], out_specs=o_spec,
        scratch_shapes=[pltpu.VMEM(acc_shape, F32)], compiler_params=_cp("parallel", "parallel", "arbitrary"),
    )(a, b)


def _mm_rows(name, a, w, out_dtype, dims):
    s, k = a.shape
    n = w.shape[1] if dims == NN else w.shape[0]
    tm = min(ROW_TILE, s)
    return _mm(name, a, w, jax.ShapeDtypeStruct((s, n), out_dtype), (s // tm, 1, 1),
               pl.BlockSpec((tm, k), lambda i, j, kk: (i, 0)), pl.BlockSpec(w.shape, lambda i, j, kk: (0, 0)),
               pl.BlockSpec((tm, n), lambda i, j, kk: (i, 0)), dims, (tm, n))


def _mm_wgrad(name, a, g, out_dtype):
    s, m = a.shape
    n = g.shape[1]
    tk = min(ROW_TILE, s)
    return _mm(name, a, g, jax.ShapeDtypeStruct((m, n), out_dtype), (1, 1, s // tk),
               pl.BlockSpec((tk, m), lambda i, j, kk: (kk, 0)), pl.BlockSpec((tk, n), lambda i, j, kk: (kk, 0)),
               pl.BlockSpec((m, n), lambda i, j, kk: (0, 0)), TN, (m, n))


def _mm_up(h2, wup):
    s = h2.shape[0]
    tm = min(ROW_TILE, s)
    return _mm("ffn_up", h2, wup, jax.ShapeDtypeStruct((NDEV, s, FB), BF16), (NDEV, s // tm, 1),
               pl.BlockSpec((tm, D), lambda j, i, kk: (i, 0)), pl.BlockSpec((None, D, FB), lambda j, i, kk: (j, 0, 0)),
               pl.BlockSpec((None, tm, FB), lambda j, i, kk: (j, i, 0)), NN, (tm, FB))


def _mm_down(act, wdown):
    s = act.shape[1]
    tm = min(ROW_TILE, s)
    return _mm("ffn_down", act, wdown, jax.ShapeDtypeStruct((s, D), F32), (s // tm, 1, 4),
               pl.BlockSpec((None, tm, FB), lambda i, j, kk: (kk, i, 0)), pl.BlockSpec((None, FB, D), lambda i, j, kk: (kk, 0, 0)),
               pl.BlockSpec((tm, D), lambda i, j, kk: (i, 0)), NN, (tm, D))


def _mm_dact(df, wdown):
    s = df.shape[0]
    tm = min(ROW_TILE, s)
    return _mm("ffn_dact", df, wdown, jax.ShapeDtypeStruct((4, s, FB), BF16), (4, s // tm, 1),
               pl.BlockSpec((tm, D), lambda j, i, kk: (i, 0)), pl.BlockSpec((None, FB, D), lambda j, i, kk: (j, 0, 0)),
               pl.BlockSpec((None, tm, FB), lambda j, i, kk: (j, i, 0)), NT, (tm, FB))


def _mm_dwdown(act, df):
    s = df.shape[0]
    tk = min(ROW_TILE, s)
    return _mm("ffn_dwdown", act, df, jax.ShapeDtypeStruct((4, FB, D), BF16), (4, 1, s // tk),
               pl.BlockSpec((None, tk, FB), lambda j, i, kk: (j, kk, 0)), pl.BlockSpec((tk, D), lambda j, i, kk: (kk, 0)),
               pl.BlockSpec((None, FB, D), lambda j, i, kk: (j, 0, 0)), TN, (FB, D))


def _mm_dh2(dupre, wup):
    s = dupre.shape[1]
    tm = min(ROW_TILE, s)
    return _mm("ffn_dh2", dupre, wup, jax.ShapeDtypeStruct((s, D), BF16), (s // tm, 1, NDEV),
               pl.BlockSpec((None, tm, FB), lambda i, j, kk: (kk, i, 0)), pl.BlockSpec((None, D, FB), lambda i, j, kk: (kk, 0, 0)),
               pl.BlockSpec((tm, D), lambda i, j, kk: (i, 0)), NT, (tm, D))


def _mm_dwup(h2, dupre):
    s = h2.shape[0]
    tk = min(ROW_TILE, s)
    return _mm("ffn_dwup", h2, dupre, jax.ShapeDtypeStruct((NDEV, D, FB), BF16), (NDEV, 1, s // tk),
               pl.BlockSpec((tk, D), lambda j, i, kk: (kk, 0)), pl.BlockSpec((None, tk, FB), lambda j, i, kk: (j, kk, 0)),
               pl.BlockSpec((None, D, FB), lambda j, i, kk: (j, 0, 0)), TN, (D, FB))


def _rms(xv, w, out_dtype, name):
    s, d = xv.shape
    ts = min(ROW_TILE, s)

    def body(x_ref, w_ref, o_ref):
        xf = x_ref[...].astype(F32)
        r = lax.rsqrt(jnp.mean(xf * xf, axis=-1, keepdims=True) + EPS)
        o_ref[...] = (xf * r * w_ref[...]).astype(o_ref.dtype)

    return pl.pallas_call(
        body, name=name, grid=(s // ts,), out_shape=jax.ShapeDtypeStruct((s, d), out_dtype),
        in_specs=[pl.BlockSpec((ts, d), lambda i: (i, 0)), pl.BlockSpec((1, d), lambda i: (0, 0))],
        out_specs=pl.BlockSpec((ts, d), lambda i: (i, 0)), compiler_params=_cp("parallel"),
    )(xv, w)


def _add_rms(xv, mv, w, name):
    s, d = xv.shape
    ts = min(ROW_TILE, s)

    def body(x_ref, m_ref, w_ref, o_ref):
        mf = m_ref[...].astype(F32)
        r = lax.rsqrt(jnp.mean(mf * mf, axis=-1, keepdims=True) + EPS)
        o_ref[...] = x_ref[...] + mf * r * w_ref[...]

    return pl.pallas_call(
        body, name=name, grid=(s // ts,), out_shape=jax.ShapeDtypeStruct((s, d), F32),
        in_specs=[pl.BlockSpec((ts, d), lambda i: (i, 0)), pl.BlockSpec((ts, d), lambda i: (i, 0)), pl.BlockSpec((1, d), lambda i: (0, 0))],
        out_specs=pl.BlockSpec((ts, d), lambda i: (i, 0)), compiler_params=_cp("parallel"),
    )(xv, mv, w)


def _rms_bwd_math(xf, w, dy):
    r = lax.rsqrt(jnp.mean(xf * xf, axis=-1, keepdims=True) + EPS)
    xh = xf * r
    dxh = dy * w
    dx = r * (dxh - xh * jnp.mean(dxh * xh, axis=-1, keepdims=True))
    return dx, dy * xh


def _rms_bwd(xv, w, dy, dres, out_dtype, name):
    s, d = xv.shape
    ts = min(ROW_TILE, s)
    with_res = dres is not None

    def body(*refs):
        if with_res:
            x_ref, w_ref, dy_ref, dres_ref, dx_ref, dw_ref = refs
        else:
            x_ref, w_ref, dy_ref, dx_ref, dw_ref = refs
        dx, dwt = _rms_bwd_math(x_ref[...].astype(F32), w_ref[...], dy_ref[...].astype(F32))
        if with_res:
            dx = dx + dres_ref[...]
        dx_ref[...] = dx.astype(dx_ref.dtype)

        @pl.when(pl.program_id(0) == 0)
        def _():
            dw_ref[...] = jnp.zeros_like(dw_ref)

        dw_ref[...] += _rowsum(dwt)

    row = pl.BlockSpec((ts, d), lambda i: (i, 0))
    vec = pl.BlockSpec((1, d), lambda i: (0, 0))
    ins = [xv, w, dy] + ([dres] if with_res else [])
    return pl.pallas_call(
        body, name=name, grid=(s // ts,),
        out_shape=(jax.ShapeDtypeStruct((s, d), out_dtype), jax.ShapeDtypeStruct((1, d), F32)),
        in_specs=[row, vec, row] + ([row] if with_res else []), out_specs=(row, vec), compiler_params=_cp("arbitrary"),
    )(*ins)


def _loss_head(yv, tv):
    s, d = yv.shape
    ts = min(ROW_TILE, s)

    def body(y_ref, t_ref, l_ref, dy_ref):
        e = y_ref[...] - t_ref[...]
        dy_ref[...] = e * (1.0 / d)

        @pl.when(pl.program_id(0) == 0)
        def _():
            l_ref[...] = jnp.zeros_like(l_ref)

        tot = jnp.sum(jnp.sum(e * e, axis=1, keepdims=True), axis=0, keepdims=True)
        l_ref[...] += jnp.broadcast_to(tot * (0.5 / d), (8, LANE))

    row = pl.BlockSpec((ts, d), lambda i: (i, 0))
    return pl.pallas_call(
        body, name="loss_head", grid=(s // ts,),
        out_shape=(jax.ShapeDtypeStruct((8, LANE), F32), jax.ShapeDtypeStruct((s, d), F32)),
        in_specs=[row, row], out_specs=(pl.BlockSpec((8, LANE), lambda i: (0, 0)), row), compiler_params=_cp("arbitrary"),
    )(yv, tv)


def _rope(v, c, a, b):
    return v * c + pltpu.roll(v, LANE - 16, 1) * a + pltpu.roll(v, 16, 1) * b


def _rope_t(dv, c, a, b):
    return dv * c + pltpu.roll(dv * a, 16, 1) + pltpu.roll(dv * b, LANE - 16, 1)


def _mla_prep(proj, tabs, qnw, kvnw, wq, wkv):
    s = proj.shape[0]
    ts = min(ROW_TILE, s)
    tc, ta, tb = tabs

    def body(cq_ref, ckv_ref, kr_ref, c_ref, a_ref, b_ref, qnw_ref, kvnw_ref, wq_ref, wkv_ref, q_ref, k_ref, kv_ref):
        c, a, b = c_ref[...], a_ref[...], b_ref[...]
        cq = cq_ref[...].astype(F32)
        qn = (cq * lax.rsqrt(jnp.mean(cq * cq, axis=-1, keepdims=True) + EPS) * qnw_ref[...]).astype(BF16)
        ckv = ckv_ref[...].astype(F32)
        kvn = (ckv * lax.rsqrt(jnp.mean(ckv * ckv, axis=-1, keepdims=True) + EPS) * kvnw_ref[...]).astype(BF16)
        kr = _rope(kr_ref[...].astype(F32), c, a, b)
        lane = lax.broadcasted_iota(jnp.int32, (ts, LANE), 1)
        for h in range(HEADS):
            q_ref[h] = _rope(_dot(qn, wq_ref[h]), c, a, b).astype(BF16)
            kv = _dot(kvn, wkv_ref[h])
            kv_ref[h] = kv.astype(BF16)
            k_ref[h] = jnp.where(lane < NOPE, kv, kr).astype(BF16)

    tab = pl.BlockSpec((ts, LANE), lambda i: (i, 0))
    hd = pl.BlockSpec((HEADS, ts, LANE), lambda i: (0, i, 0))
    out = jax.ShapeDtypeStruct((HEADS, s, LANE), BF16)
    return pl.pallas_call(
        body, name="mla_prep", grid=(s // ts,), out_shape=(out, out, out),
        in_specs=[pl.BlockSpec((ts, QL), lambda i: (i, 0)), pl.BlockSpec((ts, LANE), lambda i: (i, 2)),
                  pl.BlockSpec((ts, LANE), lambda i: (i, 3)), tab, tab, tab,
                  pl.BlockSpec((1, QL), lambda i: (0, 0)), pl.BlockSpec((1, KVL), lambda i: (0, 0)),
                  pl.BlockSpec((HEADS, QL, LANE), lambda i: (0, 0, 0)), pl.BlockSpec((HEADS, KVL, LANE), lambda i: (0, 0, 0))],
        out_specs=(hd, hd, hd), compiler_params=_cp("parallel"),
    )(proj, proj, proj, tc, ta, tb, qnw, kvnw, wq, wkv)


def _mla_prep_bwd(proj, tabs, qnw, kvnw, wq, wkv, dq, dk, dv):
    s = proj.shape[0]
    ts = min(ROW_TILE, s)
    tc, ta, tb = tabs

    def body(cq_ref, ckv_ref, c_ref, a_ref, b_ref, qnw_ref, kvnw_ref, wq_ref, wkv_ref, dq_ref, dk_ref, dv_ref,
             dcq_ref, dckv_ref, dkr_ref, dwq_ref, dwkv_ref, dqnw_ref, dkvnw_ref):
        @pl.when(pl.program_id(0) == 0)
        def _():
            dwq_ref[...] = jnp.zeros_like(dwq_ref)
            dwkv_ref[...] = jnp.zeros_like(dwkv_ref)
            dqnw_ref[...] = jnp.zeros_like(dqnw_ref)
            dkvnw_ref[...] = jnp.zeros_like(dkvnw_ref)

        c, a, b = c_ref[...], a_ref[...], b_ref[...]
        cq = cq_ref[...].astype(F32)
        qn = (cq * lax.rsqrt(jnp.mean(cq * cq, axis=-1, keepdims=True) + EPS) * qnw_ref[...]).astype(BF16)
        ckv = ckv_ref[...].astype(F32)
        kvn = (ckv * lax.rsqrt(jnp.mean(ckv * ckv, axis=-1, keepdims=True) + EPS) * kvnw_ref[...]).astype(BF16)
        lane = lax.broadcasted_iota(jnp.int32, (ts, LANE), 1)
        dqn = jnp.zeros((ts, QL), F32)
        dkvn = jnp.zeros((ts, KVL), F32)
        dkr = jnp.zeros((ts, LANE), F32)
        for h in range(HEADS):
            dqh = _rope_t(dq_ref[h], c, a, b).astype(BF16)
            dwq_ref[h] += _dot(qn, dqh, TN)
            dqn += _dot(dqh, wq_ref[h], NT)
            dkh = dk_ref[h].astype(F32)
            dkvh = jnp.where(lane < NOPE, dkh, dv_ref[h].astype(F32)).astype(BF16)
            dkr += jnp.where(lane < NOPE, 0.0, dkh)
            dwkv_ref[h] += _dot(kvn, dkvh, TN)
            dkvn += _dot(dkvh, wkv_ref[h], NT)
        dkr_ref[...] = _rope_t(dkr, c, a, b).astype(BF16)
        dcq, dwt = _rms_bwd_math(cq, qnw_ref[...], dqn)
        dcq_ref[...] = dcq.astype(BF16)
        dqnw_ref[...] += _rowsum(dwt)
        dckv, dwt = _rms_bwd_math(ckv, kvnw_ref[...], dkvn)
        dckv_ref[...] = dckv.astype(BF16)
        dkvnw_ref[...] += _rowsum(dwt)

    tab = pl.BlockSpec((ts, LANE), lambda i: (i, 0))
    hd = pl.BlockSpec((HEADS, ts, LANE), lambda i: (0, i, 0))
    wq_spec = pl.BlockSpec((HEADS, QL, LANE), lambda i: (0, 0, 0))
    wkv_spec = pl.BlockSpec((HEADS, KVL, LANE), lambda i: (0, 0, 0))
    return pl.pallas_call(
        body, name="mla_prep_bwd", grid=(s // ts,),
        out_shape=(jax.ShapeDtypeStruct((s, QL), BF16), jax.ShapeDtypeStruct((s, KVL), BF16), jax.ShapeDtypeStruct((s, LANE), BF16),
                   jax.ShapeDtypeStruct((HEADS, QL, LANE), F32), jax.ShapeDtypeStruct((HEADS, KVL, LANE), F32),
                   jax.ShapeDtypeStruct((1, QL), F32), jax.ShapeDtypeStruct((1, KVL), F32)),
        in_specs=[pl.BlockSpec((ts, QL), lambda i: (i, 0)), pl.BlockSpec((ts, LANE), lambda i: (i, 2)), tab, tab, tab,
                  pl.BlockSpec((1, QL), lambda i: (0, 0)), pl.BlockSpec((1, KVL), lambda i: (0, 0)), wq_spec, wkv_spec, hd, hd, hd],
        out_specs=(pl.BlockSpec((ts, QL), lambda i: (i, 0)), pl.BlockSpec((ts, KVL), lambda i: (i, 0)), tab, wq_spec, wkv_spec,
                   pl.BlockSpec((1, QL), lambda i: (0, 0)), pl.BlockSpec((1, KVL), lambda i: (0, 0))),
        compiler_params=_cp("arbitrary"),
    )(proj, proj, tc, ta, tb, qnw, kvnw, wq, wkv, dq, dk, dv)


def _causal_mask(i, j, tq, tk):
    row = i * tq + lax.broadcasted_iota(jnp.int32, (tq, tk), 0)
    col = j * tk + lax.broadcasted_iota(jnp.int32, (tq, tk), 1)
    return row >= col


def _flash_fwd(q, k, kv):
    s = q.shape[1]
    t = min(ATT_TILE, s)
    n = s // t

    def body(q_ref, k_ref, kv_ref, o_ref, lse_ref, m_sc, l_sc, acc_sc):
        i, j = pl.program_id(1), pl.program_id(2)

        @pl.when(j == 0)
        def _():
            m_sc[...] = jnp.full_like(m_sc, NEG)
            l_sc[...] = jnp.zeros_like(l_sc)
            acc_sc[...] = jnp.zeros_like(acc_sc)

        @pl.when(j <= i)
        def _():
            sc = _dot(q_ref[...], k_ref[...], NT) * ATT_SCALE
            sc = jnp.where(_causal_mask(i, j, t, t), sc, NEG)
            m_new = jnp.maximum(m_sc[...], jnp.max(sc, axis=1, keepdims=True))
            alpha = jnp.exp(m_sc[...] - m_new)
            p = jnp.exp(sc - m_new)
            l_sc[...] = alpha * l_sc[...] + jnp.sum(p, axis=1, keepdims=True)
            acc_sc[...] = alpha * acc_sc[...] + _dot(p.astype(BF16), kv_ref[...])
            m_sc[...] = m_new

        @pl.when(j == n - 1)
        def _():
            o_ref[...] = (acc_sc[...] / l_sc[...]).astype(BF16)
            lse_ref[...] = m_sc[...] + jnp.log(l_sc[...])

    kspec = pl.BlockSpec((None, t, LANE), lambda h, i, j: (h, jnp.minimum(j, i), 0))
    return pl.pallas_call(
        body, name="flash_fwd", grid=(HEADS, n, n),
        out_shape=(jax.ShapeDtypeStruct((s, HEADS * LANE), BF16), jax.ShapeDtypeStruct((HEADS, s, 1), F32)),
        in_specs=[pl.BlockSpec((None, t, LANE), lambda h, i, j: (h, i, 0)), kspec, kspec],
        out_specs=(pl.BlockSpec((t, LANE), lambda h, i, j: (i, h)), pl.BlockSpec((None, t, 1), lambda h, i, j: (h, i, 0))),
        scratch_shapes=[pltpu.VMEM((t, 1), F32), pltpu.VMEM((t, 1), F32), pltpu.VMEM((t, LANE), F32)],
        compiler_params=_cp("parallel", "parallel", "arbitrary"),
    )(q, k, kv)


def _flash_bwd(q, k, kv, cat, dcat, lse):
    s = q.shape[1]
    t = min(ATT_TILE, s)
    n = s // t

    def body(q_ref, k_ref, kv_ref, o_ref, do_ref, lse_ref, dq_ref, dk_ref, dv_ref, dk_sc, dv_sc):
        j, i = pl.program_id(1), pl.program_id(2)

        @pl.when((j == 0) & (i == 0))
        def _():
            dq_ref[...] = jnp.zeros_like(dq_ref)

        @pl.when(i == 0)
        def _():
            dk_sc[...] = jnp.zeros_like(dk_sc)
            dv_sc[...] = jnp.zeros_like(dv_sc)

        @pl.when(i >= j)
        def _():
            qv, kk, do = q_ref[...], k_ref[...], do_ref[...]
            sc = _dot(qv, kk, NT) * ATT_SCALE
            p = jnp.where(_causal_mask(i, j, t, t), jnp.exp(sc - lse_ref[...]), 0.0)
            dp = _dot(do, kv_ref[...], NT)
            delta = jnp.sum(do.astype(F32) * o_ref[...].astype(F32), axis=1, keepdims=True)
            ds = (p * (dp - delta) * ATT_SCALE).astype(BF16)
            dv_sc[...] += _dot(p.astype(BF16), do, TN)
            dk_sc[...] += _dot(ds, qv, TN)
            rows = pl.ds(pl.multiple_of(i * t, t), t)
            dq_ref[rows, :] += _dot(ds, kk)

        @pl.when(i == n - 1)
        def _():
            dk_ref[...] = dk_sc[...].astype(BF16)
            dv_ref[...] = dv_sc[...].astype(BF16)

    qspec = pl.BlockSpec((None, t, LANE), lambda h, j, i: (h, jnp.maximum(i, j), 0))
    kspec = pl.BlockSpec((None, t, LANE), lambda h, j, i: (h, j, 0))
    ospec = pl.BlockSpec((t, LANE), lambda h, j, i: (jnp.maximum(i, j), h))
    return pl.pallas_call(
        body, name="flash_bwd", grid=(HEADS, n, n),
        out_shape=(jax.ShapeDtypeStruct((HEADS, s, LANE), F32), jax.ShapeDtypeStruct((HEADS, s, LANE), BF16),
                   jax.ShapeDtypeStruct((HEADS, s, LANE), BF16)),
        in_specs=[qspec, kspec, kspec, ospec, ospec, pl.BlockSpec((None, t, 1), lambda h, j, i: (h, jnp.maximum(i, j), 0))],
        out_specs=(pl.BlockSpec((None, s, LANE), lambda h, j, i: (h, 0, 0)), kspec, kspec),
        scratch_shapes=[pltpu.VMEM((t, LANE), F32), pltpu.VMEM((t, LANE), F32)],
        compiler_params=_cp("parallel", "arbitrary", "arbitrary"),
    )(q, k, kv, cat, dcat, lse)


def _conv3(ext, w_ref, ts):
    return (w_ref[0:1, :] * ext[pl.ds(HALO - 2, ts), :] + w_ref[1:2, :] * ext[pl.ds(HALO - 1, ts), :]
            + w_ref[2:3, :] * ext[pl.ds(HALO, ts), :])


def _conv3_t(ext2, w_ref, ts):
    return (w_ref[0:1, :] * ext2[pl.ds(2, ts), :] + w_ref[1:2, :] * ext2[pl.ds(1, ts), :] + w_ref[2:3, :] * ext2[pl.ds(0, ts), :])


def _sconv_fwd(proj, w):
    s = proj.shape[0]
    ts = min(ROW_TILE, s)

    def body(b_ref, c_ref, h_ref, hc_ref, hh_ref, w_ref, o_ref, ext):
        i = pl.program_id(0)
        ext[0:HALO, :] = hc_ref[...].astype(F32) * hh_ref[...].astype(F32) * (i > 0).astype(F32)
        ext[HALO:HALO + ts, :] = c_ref[...].astype(F32) * h_ref[...].astype(F32)
        o_ref[...] = (b_ref[...].astype(F32) * _conv3(ext, w_ref, ts)).astype(BF16)

    def col(cb):
        return pl.BlockSpec((ts, SC), lambda i: (i, cb))

    def halo(cb):
        return pl.BlockSpec((HALO, SC), lambda i: (_prev_halo(i, ts), cb))

    return pl.pallas_call(
        body, name="sconv_fwd", grid=(s // ts,), out_shape=jax.ShapeDtypeStruct((s, SC), BF16),
        in_specs=[col(2), col(3), col(4), halo(3), halo(4), pl.BlockSpec((3, SC), lambda i: (0, 0))],
        out_specs=pl.BlockSpec((ts, SC), lambda i: (i, 0)), scratch_shapes=[pltpu.VMEM((ts + HALO, SC), F32)],
        compiler_params=_cp("parallel"),
    )(proj, proj, proj, proj, proj, w)


def _sconv_bwd(proj, dcat, w):
    s = proj.shape[0]
    ts = min(ROW_TILE, s)
    n = s // ts

    def body(b_ref, c_ref, h_ref, hc_ref, hh_ref, dy_ref, ndy_ref, nb_ref, w_ref, db_ref, dc_ref, dh_ref, dw_ref, ext, ext2):
        i = pl.program_id(0)

        @pl.when(i == 0)
        def _():
            dw_ref[...] = jnp.zeros_like(dw_ref)

        cv, hv, bv = c_ref[...].astype(F32), h_ref[...].astype(F32), b_ref[...].astype(F32)
        ext[0:HALO, :] = hc_ref[...].astype(F32) * hh_ref[...].astype(F32) * (i > 0).astype(F32)
        ext[HALO:HALO + ts, :] = cv * hv
        dy = dy_ref[...].astype(F32)
        db_ref[...] = (dy * _conv3(ext, w_ref, ts)).astype(BF16)
        dyb = dy * bv
        ext2[0:ts, :] = dyb
        ext2[ts:ts + HALO, :] = ndy_ref[...].astype(F32) * nb_ref[...].astype(F32) * (i < n - 1).astype(F32)
        dg = _conv3_t(ext2, w_ref, ts)
        dc_ref[...] = (dg * hv).astype(BF16)
        dh_ref[...] = (dg * cv).astype(BF16)
        for kk in range(3):
            dw_ref[kk:kk + 1, :] += _rowsum(dyb * ext[pl.ds(HALO - 2 + kk, ts), :])

    def col(cb):
        return pl.BlockSpec((ts, SC), lambda i: (i, cb))

    def halo(cb):
        return pl.BlockSpec((HALO, SC), lambda i: (_prev_halo(i, ts), cb))

    def nxt(cb):
        return pl.BlockSpec((HALO, SC), lambda i: (_next_halo(i, ts, n), cb))

    out = jax.ShapeDtypeStruct((s, SC), BF16)
    o0 = pl.BlockSpec((ts, SC), lambda i: (i, 0))
    return pl.pallas_call(
        body, name="sconv_bwd", grid=(n,), out_shape=(out, out, out, jax.ShapeDtypeStruct((3, SC), F32)),
        in_specs=[col(2), col(3), col(4), halo(3), halo(4), col(4), nxt(4), nxt(2), pl.BlockSpec((3, SC), lambda i: (0, 0))],
        out_specs=(o0, o0, o0, pl.BlockSpec((3, SC), lambda i: (0, 0))),
        scratch_shapes=[pltpu.VMEM((ts + HALO, SC), F32), pltpu.VMEM((ts + HALO, SC), F32)], compiler_params=_cp("arbitrary"),
    )(proj, proj, proj, proj, proj, dcat, dcat, proj, w)


def _ffn_stage(ext, u_ref, halo_ref, i, ts):
    ext[0:HALO, :] = halo_ref[...].astype(F32) * (i > 0).astype(F32)
    ext[HALO:HALO + ts, :] = u_ref[...].astype(F32)


def _ffn_specs(ts):
    cur = pl.BlockSpec((2, None, ts, FB), lambda j, i: (0, j, i, 0))
    halo = pl.BlockSpec((2, None, HALO, FB), lambda j, i: (0, j, _prev_halo(i, ts), 0))
    w = pl.BlockSpec((2, None, 3, FB), lambda j, i: (0, j, 0, 0))
    b = pl.BlockSpec((2, None, 1, FB), lambda j, i: (0, j, 0, 0))
    return cur, halo, w, b


def _ffn_act(upre, fcw, fcb):
    s = upre.shape[1]
    ts = min(ROW_TILE, s)

    def body(u_ref, halo_ref, w_ref, b_ref, o_ref, ext):
        i = pl.program_id(1)
        _ffn_stage(ext, u_ref.at[0], halo_ref.at[0], i, ts)
        gate = b_ref[0] + _conv3(ext, w_ref.at[0], ts)
        _ffn_stage(ext, u_ref.at[1], halo_ref.at[1], i, ts)
        up = b_ref[1] + _conv3(ext, w_ref.at[1], ts)
        o_ref[...] = (gate * _sig(gate) * up).astype(BF16)

    cur, halo, w, b = _ffn_specs(ts)
    u4 = upre.reshape(2, 4, s, FB)
    return pl.pallas_call(
        body, name="ffn_act", grid=(4, s // ts), out_shape=jax.ShapeDtypeStruct((4, s, FB), BF16),
        in_specs=[cur, halo, w, b], out_specs=pl.BlockSpec((None, ts, FB), lambda j, i: (j, i, 0)),
        scratch_shapes=[pltpu.VMEM((ts + HALO, FB), F32)], compiler_params=_cp("parallel", "parallel"),
    )(u4, u4, fcw.reshape(2, 4, 3, FB), fcb.reshape(2, 4, 1, FB))


def _ffn_bwd_a(upre, dact, fcw, fcb):
    s = upre.shape[1]
    ts = min(ROW_TILE, s)

    def body(u_ref, halo_ref, w_ref, b_ref, da_ref, du_ref, db_ref, ext):
        i = pl.program_id(1)

        @pl.when(i == 0)
        def _():
            db_ref[...] = jnp.zeros_like(db_ref)

        _ffn_stage(ext, u_ref.at[0], halo_ref.at[0], i, ts)
        gate = b_ref[0] + _conv3(ext, w_ref.at[0], ts)
        _ffn_stage(ext, u_ref.at[1], halo_ref.at[1], i, ts)
        up = b_ref[1] + _conv3(ext, w_ref.at[1], ts)
        sg = _sig(gate)
        da = da_ref[...].astype(F32)
        dgate = da * up * sg * (1.0 + gate * (1.0 - sg))
        dup = da * gate * sg
        du_ref[0] = dgate.astype(BF16)
        du_ref[1] = dup.astype(BF16)
        db_ref[0] += _rowsum(dgate)
        db_ref[1] += _rowsum(dup)

    cur, halo, w, b = _ffn_specs(ts)
    u4 = upre.reshape(2, 4, s, FB)
    du, db = pl.pallas_call(
        body, name="ffn_bwd_a", grid=(4, s // ts),
        out_shape=(jax.ShapeDtypeStruct((2, 4, s, FB), BF16), jax.ShapeDtypeStruct((2, 4, 1, FB), F32)),
        in_specs=[cur, halo, w, b, pl.BlockSpec((None, ts, FB), lambda j, i: (j, i, 0))], out_specs=(cur, b),
        scratch_shapes=[pltpu.VMEM((ts + HALO, FB), F32)], compiler_params=_cp("parallel", "arbitrary"),
    )(u4, u4, fcw.reshape(2, 4, 3, FB), fcb.reshape(2, 4, 1, FB), dact)
    return du.reshape(NDEV, s, FB), db.reshape(NDEV, 1, FB)


def _ffn_bwd_b(du, upre, fcw):
    s = upre.shape[1]
    ts = min(ROW_TILE, s)
    n = s // ts

    def body(du_ref, ndu_ref, u_ref, halo_ref, w_ref, dup_ref, dw_ref, ext, ext2):
        i = pl.program_id(1)

        @pl.when(i == 0)
        def _():
            dw_ref[...] = jnp.zeros_like(dw_ref)

        duv = du_ref[...].astype(F32)
        ext2[0:ts, :] = duv
        ext2[ts:ts + HALO, :] = ndu_ref[...].astype(F32) * (i < n - 1).astype(F32)
        dup_ref[...] = _conv3_t(ext2, w_ref, ts).astype(BF16)
        _ffn_stage(ext, u_ref, halo_ref, i, ts)
        for kk in range(3):
            dw_ref[kk:kk + 1, :] += _rowsum(duv * ext[pl.ds(HALO - 2 + kk, ts), :])

    cur = pl.BlockSpec((None, ts, FB), lambda j, i: (j, i, 0))
    w = pl.BlockSpec((None, 3, FB), lambda j, i: (j, 0, 0))
    return pl.pallas_call(
        body, name="ffn_bwd_b", grid=(NDEV, n),
        out_shape=(jax.ShapeDtypeStruct((NDEV, s, FB), BF16), jax.ShapeDtypeStruct((NDEV, 3, FB), F32)),
        in_specs=[cur, pl.BlockSpec((None, HALO, FB), lambda j, i: (j, _next_halo(i, ts, n), 0)), cur,
                  pl.BlockSpec((None, HALO, FB), lambda j, i: (j, _prev_halo(i, ts), 0)), w],
        out_specs=(cur, w), scratch_shapes=[pltpu.VMEM((ts + HALO, FB), F32), pltpu.VMEM((ts + HALO, FB), F32)],
        compiler_params=_cp("parallel", "arbitrary"),
    )(du, du, upre, upre, fcw)


def _softplus(v):
    e = jnp.exp(-jnp.abs(v))
    return jnp.maximum(v, 0.0) + jnp.where(e < 1e-4, e * (1.0 - 0.5 * e), jnp.log(1.0 + e))


def _ssd_consts():
    L = SSD_L
    r = lax.broadcasted_iota(jnp.int32, (L, L), 0)
    c = lax.broadcasted_iota(jnp.int32, (L, L), 1)
    tri = r >= c
    er = lax.broadcasted_iota(jnp.int32, (LANE, SSD_DIM), 0)
    ec = lax.broadcasted_iota(jnp.int32, (LANE, SSD_DIM), 1)
    expand = ((ec >= er * 64) & (ec < er * 64 + 64)).astype(F32)
    return tri, expand


def _ssd_conv4(ext, cw_ref, cb_ref):
    L = SSD_L
    pre = cb_ref[...] + cw_ref[0:1, :] * ext[pl.ds(HALO - 3, L), :]
    for kk in range(1, 4):
        pre = pre + cw_ref[kk:kk + 1, :] * ext[pl.ds(HALO - 3 + kk, L), :]
    return pre


def _ssd_common(xbc_ref, halo_ref, dt_ref, cw_ref, cb_ref, dtb_ref, alog_ref, ext, first):
    L = SSD_L
    tri, expand = _ssd_consts()
    ext[0:HALO, :] = halo_ref[...].astype(F32) * (1.0 - first.astype(F32))
    ext[HALO:HALO + L, :] = xbc_ref[...].astype(F32)
    pre = _ssd_conv4(ext, cw_ref, cb_ref)
    sg = _sig(pre)
    act = pre * sg
    lane = lax.broadcasted_iota(jnp.int32, (1, LANE), 1)
    m4 = lane < SSD_H
    raw = dt_ref[...].astype(F32) + dtb_ref[...]
    dtv = jnp.where(m4, _softplus(raw), 0.0)
    av = jnp.where(m4, -jnp.exp(alog_ref[...]), 0.0)
    adt = dtv * av
    acs = _dot(tri.astype(F32), adt, precision=HI)
    acs_b = _dot(acs, expand, precision=HI)
    dt_b = _dot(dtv, expand, precision=HI)
    return dict(tri=tri, expand=expand, pre=pre, sg=sg, act=act, raw=raw, dtv=dtv, av=av, m4=m4, acs=acs, acs_b=acs_b,
                dt_b=dt_b, lane=lane)


def _head_terms(cm, h):
    L = SSD_L
    acs, tri = cm["acs"], cm["tri"]
    lane_l = lax.broadcasted_iota(jnp.int32, (L, LANE), 1)
    sub_l = lax.broadcasted_iota(jnp.int32, (LANE, L), 0)
    col = jnp.sum(jnp.where(lane_l == h, acs, 0.0), axis=1, keepdims=True)
    row = jnp.sum(jnp.where(sub_l == h, acs.T, 0.0), axis=0, keepdims=True)
    dec = jnp.where(tri, jnp.exp(jnp.where(tri, col - row, NEG)), 0.0)
    rowi = lax.broadcasted_iota(jnp.int32, (L, 1), 0)
    last = jnp.sum(jnp.where(rowi == L - 1, col, 0.0), axis=0, keepdims=True)
    dte = jnp.exp(last - col)
    return col, dec, last, dte


def _ssd_fwd(proj, cw, cb, dtb, alog, dvec, nw):
    s = proj.shape[0]
    L = SSD_L
    nc = s // L

    def body(z_ref, xbc_ref, halo_ref, dt_ref, cw_ref, cb_ref, dtb_ref, alog_ref, d_ref, nw_ref, y_ref, ypre_ref, st_ref, ext, state):
        i = pl.program_id(0)

        @pl.when(i == 0)
        def _():
            state[...] = jnp.zeros_like(state)

        cm = _ssd_common(xbc_ref, halo_ref, dt_ref, cw_ref, cb_ref, dtb_ref, alog_ref, ext, i == 0)
        act = cm["act"]
        xs = act[:, 0:256]
        bm = (act[:, 256:384], act[:, 384:512])
        cmat = (act[:, 512:640].astype(BF16), act[:, 640:768].astype(BF16))
        xdt = xs * cm["dt_b"]
        prev = state[...]
        st_ref[...] = prev
        prev_bf = prev.astype(BF16)
        gm = [_dot(cmat[g], bm[g].astype(BF16), NT) for g in range(2)]
        lane2 = lax.broadcasted_iota(jnp.int32, (1, SSD_DIM), 1)
        rows2 = lax.broadcasted_iota(jnp.int32, (SSD_DIM, 1), 0)
        ydiag = jnp.zeros((L, SSD_DIM), F32)
        contrib = jnp.zeros((SSD_DIM, LANE), F32)
        cd_rows = jnp.zeros((SSD_DIM, 1), F32)
        for h in range(SSD_H):
            g = h // 2
            col, dec, last, dte = _head_terms(cm, h)
            mh = (lane2 >= 64 * h) & (lane2 < 64 * h + 64)
            xm = jnp.where(mh, xdt, 0.0).astype(BF16)
            ydiag += _dot((gm[g] * dec).astype(BF16), xm)
            contrib += _dot(xm, (bm[g] * dte).astype(BF16), TN)
            cd_rows += jnp.where((rows2 >= 64 * h) & (rows2 < 64 * h + 64), jnp.exp(last), 0.0)
        yo = jnp.where(lane2 < 128, _dot(cmat[0], prev_bf, NT), _dot(cmat[1], prev_bf, NT))
        y = ydiag + yo * jnp.exp(cm["acs_b"]) + xs * d_ref[...]
        state[...] = prev * cd_rows + contrib
        ypre_ref[...] = y
        zz = z_ref[...].astype(F32)
        gt = y * zz * _sig(zz)
        y_ref[...] = (gt * lax.rsqrt(jnp.mean(gt * gt, axis=-1, keepdims=True) + EPS) * nw_ref[...]).astype(BF16)

    def vec(w):
        return pl.BlockSpec((1, w), lambda i: (0, 0))

    return pl.pallas_call(
        body, name="ssd_fwd", grid=(nc,),
        out_shape=(jax.ShapeDtypeStruct((s, SSD_DIM), BF16), jax.ShapeDtypeStruct((s, SSD_DIM), F32),
                   jax.ShapeDtypeStruct((nc, SSD_DIM, LANE), F32)),
        in_specs=[pl.BlockSpec((L, SSD_DIM), lambda i: (i, 5)), pl.BlockSpec((L, SSD_CONV), lambda i: (i, 2)),
                  pl.BlockSpec((HALO, SSD_CONV), lambda i: (_prev_halo(i, L), 2)), pl.BlockSpec((L, LANE), lambda i: (i, 18)),
                  pl.BlockSpec((4, SSD_CONV), lambda i: (0, 0)), vec(SSD_CONV), vec(LANE), vec(LANE), vec(SSD_DIM), vec(SSD_DIM)],
        out_specs=(pl.BlockSpec((L, SSD_DIM), lambda i: (i, 0)), pl.BlockSpec((L, SSD_DIM), lambda i: (i, 0)),
                   pl.BlockSpec((None, SSD_DIM, LANE), lambda i: (i, 0, 0))),
        scratch_shapes=[pltpu.VMEM((L + HALO, SSD_CONV), F32), pltpu.VMEM((SSD_DIM, LANE), F32)], compiler_params=_cp("arbitrary"),
    )(proj, proj, proj, proj, cw, cb, dtb, alog, dvec, nw)


def _ssd_bwd(proj, dcat, ypre, states, cw, cb, dtb, alog, dvec, nw):
    s = proj.shape[0]
    L = SSD_L
    nc = s // L

    def body(z_ref, xbc_ref, halo_ref, dt_ref, dy_ref, ypre_ref, st_ref, cw_ref, cb_ref, dtb_ref, alog_ref, d_ref, nw_ref,
             dz_ref, dxbc_ref, ddt_ref, dcw_ref, dcb_ref, ddtb_ref, dalog_ref, dd_ref, dnw_ref, ext, ext2, carry, dstate, ddl):
        i = pl.program_id(0)
        r = nc - 1 - i

        @pl.when(i == 0)
        def _():
            for ref in (dcw_ref, dcb_ref, ddtb_ref, dalog_ref, dd_ref, dnw_ref, carry, dstate, ddl):
                ref[...] = jnp.zeros_like(ref)

        cm = _ssd_common(xbc_ref, halo_ref, dt_ref, cw_ref, cb_ref, dtb_ref, alog_ref, ext, r == 0)
        tri, expand, act = cm["tri"], cm["expand"], cm["act"]
        xs = act[:, 0:256]
        bm = (act[:, 256:384], act[:, 384:512])
        cmat = (act[:, 512:640], act[:, 640:768])
        bm_bf = [v.astype(BF16) for v in bm]
        cm_bf = [v.astype(BF16) for v in cmat]
        dt_b = cm["dt_b"]
        xdt = xs * dt_b
        xdt_bf = xdt.astype(BF16)
        ea_b = jnp.exp(cm["acs_b"])
        prev = st_ref[...]
        prev_bf = prev.astype(BF16)
        lane2 = lax.broadcasted_iota(jnp.int32, (1, SSD_DIM), 1)
        rows2 = lax.broadcasted_iota(jnp.int32, (SSD_DIM, 1), 0)
        lane_l = lax.broadcasted_iota(jnp.int32, (L, LANE), 1)
        rowi = lax.broadcasted_iota(jnp.int32, (L, 1), 0)

        y = ypre_ref[...]
        zz = z_ref[...].astype(F32)
        sz = _sig(zz)
        gt = y * zz * sz
        dgt, dwt = _rms_bwd_math(gt, nw_ref[...], dy_ref[...].astype(F32))
        dnw_ref[...] += _rowsum(dwt)
        dy = dgt * zz * sz
        dz_ref[...] = (dgt * y * sz * (1.0 + zz * (1.0 - sz))).astype(BF16)

        ddl[0:1, :] += _rowsum(dy * xs)
        dxs = dy * d_ref[...]

        yo = jnp.where(lane2 < 128, _dot(cm_bf[0], prev_bf, NT), _dot(cm_bf[1], prev_bf, NT))
        dacs_b = dy * yo * ea_b
        dyo = dy * ea_b
        dyo_g = (jnp.where(lane2 < 128, dyo, 0.0).astype(BF16), jnp.where(lane2 >= 128, dyo, 0.0).astype(BF16))
        dc = [_dot(dyo_g[g], prev_bf) for g in range(2)]
        dprev = _dot(dyo_g[0], cm_bf[0], TN) + _dot(dyo_g[1], cm_bf[1], TN)

        gm = [_dot(cm_bf[g], bm_bf[g], NT) for g in range(2)]
        dgm = [jnp.zeros((L, L), F32), jnp.zeros((L, L), F32)]
        db = [jnp.zeros((L, LANE), F32), jnp.zeros((L, LANE), F32)]
        dxdt = jnp.zeros((L, SSD_DIM), F32)
        dacs = jnp.zeros((L, LANE), F32)
        dlast = jnp.zeros((1, LANE), F32)
        cd_rows = jnp.zeros((SSD_DIM, 1), F32)
        dst = dstate[...]
        dst_bf = dst.astype(BF16)
        dsp = dst * prev
        ones = jnp.ones((L, LANE), F32)
        for h in range(SSD_H):
            g = h // 2
            col, dec, last, dte = _head_terms(cm, h)
            mh = (lane2 >= 64 * h) & (lane2 < 64 * h + 64)
            rh = (rows2 >= 64 * h) & (rows2 < 64 * h + 64)
            sc = gm[g] * dec
            xm = jnp.where(mh, xdt, 0.0).astype(BF16)
            dym = jnp.where(mh, dy, 0.0).astype(BF16)
            dsc = _dot(dym, xdt_bf, NT)
            dxdt += _dot(sc.astype(BF16), dym, TN)
            dgm[g] += dsc * dec
            dd = dsc * sc
            rs = jnp.sum(dd, axis=1, keepdims=True)
            cs = _dot(dd, ones, TN, precision=HI)
            dacs += jnp.where(lane_l == h, rs - cs, 0.0)
            bd = (bm[g] * dte).astype(BF16)
            dxdt += jnp.where(mh, _dot(bd, dst_bf, NT), 0.0)
            dbd = _dot(xm, dst_bf)
            db[g] += dbd * dte
            tt = jnp.sum(dbd * bm[g], axis=1, keepdims=True) * dte
            dacs += jnp.where(lane_l == h, -tt, 0.0)
            cdh = jnp.exp(last)
            dcd = jnp.sum(jnp.sum(jnp.where(rh, dsp, 0.0), axis=1, keepdims=True), axis=0, keepdims=True)
            dlast += jnp.where(cm["lane"] == h, jnp.sum(tt, axis=0, keepdims=True) + dcd * cdh, 0.0)
            cd_rows += jnp.where(rh, cdh, 0.0)
        dacs += jnp.where(rowi == L - 1, dlast, 0.0)
        dacs += _dot(dacs_b, expand, NT, precision=HI)
        dstate[...] = dprev + dst * cd_rows

        for g in range(2):
            dgb = dgm[g].astype(BF16)
            dc[g] += _dot(dgb, bm_bf[g])
            db[g] += _dot(dgb, cm_bf[g], TN)

        dadt = _dot(tri.astype(F32), dacs, TN, precision=HI)
        ddtv = dadt * cm["av"] + _dot(dxdt * xs, expand, NT, precision=HI)
        dalog_ref[...] += _rowsum(dadt * cm["dtv"]) * cm["av"]
        dxs += dxdt * dt_b
        draw = jnp.where(cm["m4"], ddtv * _sig(cm["raw"]), 0.0)
        ddtb_ref[...] += _rowsum(draw)
        ddt_ref[...] = draw.astype(BF16)

        dact = jnp.concatenate([dxs, db[0], db[1], dc[0], dc[1]], axis=1)
        sg, pre = cm["sg"], cm["pre"]
        dpre = dact * sg * (1.0 + pre * (1.0 - sg))
        dcb_ref[...] += _rowsum(dpre)
        for kk in range(4):
            dcw_ref[kk:kk + 1, :] += _rowsum(dpre * ext[pl.ds(HALO - 3 + kk, L), :])
        ext2[0:L, :] = dpre
        ext2[L:L + HALO, :] = carry[...]
        dx = cw_ref[3:4, :] * ext2[pl.ds(0, L), :]
        for kk in range(3):
            dx = dx + cw_ref[kk:kk + 1, :] * ext2[pl.ds(3 - kk, L), :]
        dxbc_ref[...] = dx.astype(BF16)
        carry[...] = dpre[0:HALO, :]

        @pl.when(i == nc - 1)
        def _():
            dd_ref[...] = _dot(ddl[...], expand, NT, precision=HI)

    def vec(w):
        return pl.BlockSpec((1, w), lambda i: (0, 0))

    def rv(i):
        return nc - 1 - i

    return pl.pallas_call(
        body, name="ssd_bwd", grid=(nc,),
        out_shape=(jax.ShapeDtypeStruct((s, SSD_DIM), BF16), jax.ShapeDtypeStruct((s, SSD_CONV), BF16), jax.ShapeDtypeStruct((s, LANE), BF16),
                   jax.ShapeDtypeStruct((4, SSD_CONV), F32), jax.ShapeDtypeStruct((1, SSD_CONV), F32), jax.ShapeDtypeStruct((1, LANE), F32),
                   jax.ShapeDtypeStruct((1, LANE), F32), jax.ShapeDtypeStruct((8, LANE), F32), jax.ShapeDtypeStruct((1, SSD_DIM), F32)),
        in_specs=[pl.BlockSpec((L, SSD_DIM), lambda i: (rv(i), 5)), pl.BlockSpec((L, SSD_CONV), lambda i: (rv(i), 2)),
                  pl.BlockSpec((HALO, SSD_CONV), lambda i: (_prev_halo(rv(i), L), 2)), pl.BlockSpec((L, LANE), lambda i: (rv(i), 18)),
                  pl.BlockSpec((L, SSD_DIM), lambda i: (rv(i), 5)), pl.BlockSpec((L, SSD_DIM), lambda i: (rv(i), 0)),
                  pl.BlockSpec((None, SSD_DIM, LANE), lambda i: (rv(i), 0, 0)),
                  pl.BlockSpec((4, SSD_CONV), lambda i: (0, 0)), vec(SSD_CONV), vec(LANE), vec(LANE), vec(SSD_DIM), vec(SSD_DIM)],
        out_specs=(pl.BlockSpec((L, SSD_DIM), lambda i: (rv(i), 0)), pl.BlockSpec((L, SSD_CONV), lambda i: (rv(i), 0)),
                   pl.BlockSpec((L, LANE), lambda i: (rv(i), 0)), pl.BlockSpec((4, SSD_CONV), lambda i: (0, 0)), vec(SSD_CONV),
                   vec(LANE), vec(LANE), pl.BlockSpec((8, LANE), lambda i: (0, 0)), vec(SSD_DIM)),
        scratch_shapes=[pltpu.VMEM((L + HALO, SSD_CONV), F32), pltpu.VMEM((L + HALO, SSD_CONV), F32), pltpu.VMEM((HALO, SSD_CONV), F32),
                        pltpu.VMEM((SSD_DIM, LANE), F32), pltpu.VMEM((8, SSD_DIM), F32)],
        compiler_params=_cp("arbitrary"),
    )(proj, proj, proj, proj, dcat, ypre, states, cw, cb, dtb, alog, dvec, nw)


def _adamw(parts, w, m, v, name):
    r, c = w.shape
    tr = r
    for cand in (256, 128, 64, 32, 16, 8):
        if r % cand == 0 and (cand * c * 4) <= 2 * 1024 * 1024:
            tr = cand
            break
    c1 = 1.0 - B1 ** STEP
    c2 = 1.0 - B2 ** STEP

    def body(p_ref, w_ref, m_ref, v_ref, g_ref, d_ref, nm_ref, nv_ref):
        g = p_ref[0].astype(F32)
        for dev in range(1, NDEV):
            g = g + p_ref[dev].astype(F32)
        mn = B1 * m_ref[...] + (1.0 - B1) * g
        vn = B2 * v_ref[...] + (1.0 - B2) * (g * g)
        g_ref[...] = g
        nm_ref[...] = mn
        nv_ref[...] = vn
        d_ref[...] = -LR * ((mn / c1) / (jnp.sqrt(vn / c2) + AEPS) + WD * w_ref[...])

    blk = pl.BlockSpec((tr, c), lambda i: (i, 0))
    out = jax.ShapeDtypeStruct((r, c), F32)
    return pl.pallas_call(
        body, name=name, grid=(r // tr,), out_shape=(out, out, out, out),
        in_specs=[pl.BlockSpec((NDEV, tr, c), lambda i: (0, i, 0)), blk, blk, blk], out_specs=(blk, blk, blk, blk),
        compiler_params=_cp("parallel"),
    )(parts, w, m, v)


def _pad_win(w):
    z = lambda n: jnp.zeros((w.shape[0], n), w.dtype)
    return jnp.concatenate([w[:, :384], z(64), w[:, 384:416], z(32), w[:, 416:], z(124)], axis=1)


def _unpad_win(g):
    return jnp.concatenate([g[:, :384], g[:, 448:480], g[:, 512:2308]], axis=1)


def _pad_wout(w):
    att = jnp.pad(w[:512].reshape(HEADS, 64, D), ((0, 0), (64, 0), (0, 0))).reshape(HEADS * LANE, D)
    return jnp.concatenate([att, w[512:]], axis=0)


def _unpad_wout(g):
    att = g[:HEADS * LANE].reshape(HEADS, LANE, D)[:, 64:, :].reshape(512, D)
    return jnp.concatenate([att, g[HEADS * LANE:]], axis=0)


def _lanes(v, n=LANE):
    return jnp.pad(v, (0, n - v.shape[0])).reshape(1, n)


def _prep_layer(big, small, l):
    p = {}
    p["win"] = _pad_win(big["w_in"][:, l].reshape(D, 2212))
    p["wq"] = jnp.pad(big["mla_w_q_up"][:, l], ((0, 0), (0, 0), (0, LANE - 96)))
    p["wkv"] = big["mla_w_kv_up"][:, l]
    p["scw"] = big["sc_conv_w"][:, l].astype(F32).transpose(1, 0, 2).reshape(3, SC)
    p["ssdcw"] = big["ssd_conv_w"][:, l].astype(F32).transpose(1, 0, 2).reshape(4, SSD_CONV)
    p["wout"] = _pad_wout(big["w_out"][:, l].reshape(1024, D))
    p["wup"] = big["ffn_w_up"][:, l]
    p["fcw"] = big["ffn_conv_w"][:, l].astype(F32)
    p["wdown"] = big["ffn_w_down"][:, l].reshape(4, FB, D)
    for nm in ("norm_mix_pre", "norm_mix_post", "norm_ffn_pre", "norm_ffn_post", "mla_q_norm", "mla_kv_norm", "ssd_conv_b", "ssd_norm"):
        p[nm] = small[nm][l].reshape(1, -1)
    p["dtb"] = _lanes(small["ssd_dt_bias"][l])
    p["alog"] = _lanes(small["ssd_a_log"][l])
    p["dvec"] = jnp.repeat(small["ssd_d"][l], 64).reshape(1, SSD_DIM)
    p["fcb"] = small["ffn_conv_b"][l].reshape(NDEV, 1, FB)
    return p


def _rope_tables(positions):
    inv_freq = 1.0 / (ROPE_THETA ** (jnp.arange(0, ROPE, 2, dtype=F32) / ROPE))
    ang = positions.astype(F32)[:, None] * inv_freq
    cos, sin = jnp.cos(ang), jnp.sin(ang)
    s = positions.shape[0]
    z = lambda n: jnp.zeros((s, n), F32)
    tc = jnp.concatenate([jnp.ones((s, 64), F32), cos, cos, z(32)], axis=1)
    ta = jnp.concatenate([z(64), -sin, z(48)], axis=1)
    tb = jnp.concatenate([z(80), sin, z(32)], axis=1)
    return tc, ta, tb


def _layer_fwd(xv, p, tabs):
    h = _rms(xv, p["norm_mix_pre"], BF16, "rms_pre")
    proj = _mm_rows("in_proj", h, p["win"], BF16, NN)
    q, k, kv = _mla_prep(proj, tabs, p["mla_q_norm"], p["mla_kv_norm"], p["wq"], p["wkv"])
    o, lse = _flash_fwd(q, k, kv)
    yconv = _sconv_fwd(proj, p["scw"])
    yssd, ypre, states = _ssd_fwd(proj, p["ssdcw"], p["ssd_conv_b"], p["dtb"], p["alog"], p["dvec"], p["ssd_norm"])
    cat = jnp.concatenate([o, yconv, yssd], axis=1)
    mixed = _mm_rows("out_proj", cat, p["wout"], F32, NN)
    x1 = _add_rms(xv, mixed, p["norm_mix_post"], "add_rms")
    h2 = _rms(x1, p["norm_ffn_pre"], BF16, "rms_pre")
    upre = _mm_up(h2, p["wup"])
    act = _ffn_act(upre, p["fcw"], p["fcb"])
    f = _mm_down(act, p["wdown"])
    x2 = _add_rms(x1, f, p["norm_ffn_post"], "add_rms")
    saved = dict(x=xv, h=h, proj=proj, q=q, k=k, kv=kv, lse=lse, ypre=ypre, states=states, cat=cat, mixed=mixed, x1=x1, h2=h2,
                 upre=upre, act=act, f=f)
    return x2, saved


def _layer_bwd(dx2, sv, p, tabs):
    df, g_nfpo = _rms_bwd(sv["f"], p["norm_ffn_post"], dx2, None, BF16, "rms_bwd_post")
    dact = _mm_dact(df, p["wdown"])
    g_wdown = _mm_dwdown(sv["act"], df)
    du, g_fcb = _ffn_bwd_a(sv["upre"], dact, p["fcw"], p["fcb"])
    dupre, g_fcw = _ffn_bwd_b(du, sv["upre"], p["fcw"])
    dh2 = _mm_dh2(dupre, p["wup"])
    g_wup = _mm_dwup(sv["h2"], dupre)
    dx1, g_nfp = _rms_bwd(sv["x1"], p["norm_ffn_pre"], dh2, dx2, F32, "rms_bwd_pre")
    dmixed, g_nmpo = _rms_bwd(sv["mixed"], p["norm_mix_post"], dx1, None, BF16, "rms_bwd_post")
    dcat = _mm_rows("dcat", dmixed, p["wout"], BF16, NT)
    g_wout = _mm_wgrad("dw_out", sv["cat"], dmixed, BF16)
    dq, dk, dv = _flash_bwd(sv["q"], sv["k"], sv["kv"], sv["cat"], dcat, sv["lse"])
    dcq, dckv, dkr, g_wq, g_wkv, g_qn, g_kvn = _mla_prep_bwd(sv["proj"], tabs, p["mla_q_norm"], p["mla_kv_norm"], p["wq"], p["wkv"], dq, dk, dv)
    dscb, dscc, dsch, g_scw = _sconv_bwd(sv["proj"], dcat, p["scw"])
    dz, dxbc, ddt, g_cw, g_cb, g_dtb, g_alog, g_d, g_nw = _ssd_bwd(
        sv["proj"], dcat, sv["ypre"], sv["states"], p["ssdcw"], p["ssd_conv_b"], p["dtb"], p["alog"], p["dvec"], p["ssd_norm"])
    dproj = jnp.concatenate([dcq, dckv, dkr, dscb, dscc, dsch, dz, dxbc, ddt], axis=1)
    dh = _mm_rows("dh", dproj, p["win"], BF16, NT)
    g_win = _mm_wgrad("dw_in", sv["h"], dproj, BF16)
    dx, g_nmp = _rms_bwd(sv["x"], p["norm_mix_pre"], dh, dx1, F32, "rms_bwd_pre")
    big = {
        "w_in": _unpad_win(g_win).reshape(NDEV, 128, 2212),
        "mla_w_q_up": g_wq[:, :, :96].astype(BF16),
        "mla_w_kv_up": g_wkv.astype(BF16),
        "sc_conv_w": g_scw.reshape(3, NDEV, 32).transpose(1, 0, 2).astype(BF16),
        "ssd_conv_w": g_cw.reshape(4, NDEV, 96).transpose(1, 0, 2).astype(BF16),
        "w_out": _unpad_wout(g_wout).reshape(NDEV, 128, D),
        "ffn_w_up": g_wup,
        "ffn_conv_w": g_fcw.astype(BF16),
        "ffn_w_down": g_wdown.reshape(NDEV, 352, D),
    }
    small = {
        "norm_mix_pre": g_nmp[0], "norm_mix_post": g_nmpo[0], "norm_ffn_pre": g_nfp[0], "norm_ffn_post": g_nfpo[0],
        "mla_q_norm": g_qn[0], "mla_kv_norm": g_kvn[0], "ssd_conv_b": g_cb[0], "ssd_dt_bias": g_dtb[0, :SSD_H],
        "ssd_a_log": g_alog[0, :SSD_H], "ssd_d": g_d[0, :SSD_H], "ssd_norm": g_nw[0], "ffn_conv_b": g_fcb.reshape(-1),
    }
    return dx, big, small


def _local_step(xv, positions, target, layers):
    tabs = _rope_tables(positions)
    saved = []
    for p in layers:
        xv, sv = _layer_fwd(xv, p, tabs)
        saved.append(sv)
    loss, dx = _loss_head(xv, target)
    bigs, smalls = [None] * DEPTH, [None] * DEPTH
    for l in reversed(range(len(layers))):
        dx, bigs[l], smalls[l] = _layer_bwd(dx, saved[l], layers[l], tabs)
    return loss[0, 0], dx, bigs, smalls


def _pack_rows(flat, lead, width, mult):
    n = flat.shape[-1]
    rows = -(-n // (width * mult)) * mult
    pad = [(0, 0)] * (flat.ndim - 1) + [(0, rows * width - n)]
    return jnp.pad(flat, pad).reshape(lead + (rows, width))


def kernel(x, positions, norm_mix_pre, norm_mix_post, norm_ffn_pre, norm_ffn_post, w_in, mla_q_norm, mla_w_q_up, mla_kv_norm, mla_w_kv_up, sc_conv_w, ssd_conv_w, ssd_conv_b, ssd_dt_bias, ssd_a_log, ssd_d, ssd_norm, w_out, ffn_w_up, ffn_conv_w, ffn_conv_b, ffn_w_down, loss_target, m_norm_mix_pre, m_norm_mix_post, m_norm_ffn_pre, m_norm_ffn_post, m_w_in, m_mla_q_norm, m_mla_w_q_up, m_mla_kv_norm, m_mla_w_kv_up, m_sc_conv_w, m_ssd_conv_w, m_ssd_conv_b, m_ssd_dt_bias, m_ssd_a_log, m_ssd_d, m_ssd_norm, m_w_out, m_ffn_w_up, m_ffn_conv_w, m_ffn_conv_b, m_ffn_w_down, v_norm_mix_pre, v_norm_mix_post, v_norm_ffn_pre, v_norm_ffn_post, v_w_in, v_mla_q_norm, v_mla_w_q_up, v_mla_kv_norm, v_mla_w_kv_up, v_sc_conv_w, v_ssd_conv_w, v_ssd_conv_b, v_ssd_dt_bias, v_ssd_a_log, v_ssd_d, v_ssd_norm, v_w_out, v_ffn_w_up, v_ffn_conv_w, v_ffn_conv_b, v_ffn_w_down):
    given = dict(locals())
    w = {n: given[n] for n in WEIGHTS}
    m = {n: given["m_" + n] for n in WEIGHTS}
    v = {n: given["v_" + n] for n in WEIGHTS}

    sizes = [math.prod(shape) for _, shape in SHARDED]
    flat = jnp.concatenate([w[n].astype(BF16).reshape(-1) for n, _ in SHARDED])
    gathered = _all_gather(_pack_rows(flat, (), 1024, 16), "gather_weights").reshape(NDEV, -1)
    big, off = {}, 0
    for (n, shape), sz in zip(SHARDED, sizes):
        big[n] = gathered[:, off:off + sz].reshape((NDEV,) + shape)
        off += sz
    small_w = {n: w[n] for n, _ in SMALL}
    layers = [_prep_layer(big, small_w, l) for l in range(DEPTH)]

    loss, dx, bigs, smalls = _local_step(x[0], positions[0], loss_target[0], layers)
    loss = lax.psum(loss, ("x", "y", "c"))

    send = jnp.concatenate([jnp.stack([bigs[l][n] for l in range(DEPTH)], axis=1).reshape(NDEV, -1) for n, _ in SHARDED], axis=1)
    recv = _all_to_all(_pack_rows(send, (NDEV,), 1024, 16), "exchange_grads").reshape(NDEV, -1)
    out = {}
    off = 0
    for (n, shape), sz in zip(SHARDED, sizes):
        r2 = (math.prod(shape[:-1]), shape[-1])
        parts = recv[:, off:off + sz].reshape((NDEV,) + r2)
        off += sz
        res = _adamw(parts, w[n].reshape(r2), m[n].reshape(r2), v[n].reshape(r2), "adamw_" + n)
        out[n] = [a.reshape(shape) for a in res]

    sflat = jnp.concatenate([jnp.stack([smalls[l][n] for l in range(DEPTH)]).reshape(-1) for n, _ in SMALL])
    sparts = _all_gather(_pack_rows(sflat, (), LANE, 8), "gather_small_grads")
    pk = lambda d: _pack_rows(jnp.concatenate([d[n].reshape(-1) for n, _ in SMALL]), (), LANE, 8)
    res = _adamw(sparts, pk(w), pk(m), pk(v), "adamw_small")
    off = 0
    for n, width in SMALL:
        out[n] = [a.reshape(-1)[off:off + DEPTH * width].reshape(DEPTH, width) for a in res]
        off += DEPTH * width

    return (loss, dx[None], *[out[n][0] for n in WEIGHTS], *[out[n][1] for n in WEIGHTS],
            *[out[n][2] for n in WEIGHTS], *[out[n][3] for n in WEIGHTS])
```

```python
import functools
import math

import jax
import jax.numpy as jnp
from jax import lax
from jax.experimental import pallas as pl
from jax.experimental.pallas import tpu as pltpu

F32 = jnp.float32
BF16 = jnp.bfloat16

D = 1024
DEPTH = 4
NDEV = 8
HEADS = 8
QL = 256
KVL = 128
ROPE = 32
NOPE = 64
SC = 256
SSD_DIM = 256
SSD_CONV = 768
SSD_H = 4
SSD_L = 128
FFN = 2816
FB = 704
EPS = 1e-6
ROPE_THETA = 10000.0
ATT_SCALE = 96 ** -0.5
LR, B1, B2, AEPS, WD, STEP = 0.001, 0.9, 0.999, 1e-08, 0.01, 10

PW = 2432
CATW = 1536

ROW_TILE = 512
ATT_TILE = 512
HALO = 16
LANE = 128
NEG = -1e30
HI = lax.Precision.HIGHEST
NN = (((1,), (0,)), ((), ()))
NT = (((1,), (1,)), ((), ()))
TN = (((0,), (0,)), ((), ()))
VMEM_LIMIT = 56 * 1024 * 1024

SHARDED = (
    ("w_in", (4, 128, 2212)),
    ("mla_w_q_up", (4, 256, 96)),
    ("mla_w_kv_up", (4, 128, 128)),
    ("sc_conv_w", (4, 3, 32)),
    ("ssd_conv_w", (4, 4, 96)),
    ("w_out", (4, 128, 1024)),
    ("ffn_w_up", (4, 1024, 704)),
    ("ffn_conv_w", (4, 3, 704)),
    ("ffn_w_down", (4, 352, 1024)),
)
SHAPES = dict(SHARDED)
GROUPS = (
    (2212, ("w_in",)),
    (1024, ("w_out", "ffn_w_down")),
    (704, ("ffn_w_up", "ffn_conv_w")),
    (96, ("mla_w_q_up", "ssd_conv_w", "sc_conv_w")),
    (128, ("mla_w_kv_up",)),
)
SMALL = (
    ("norm_mix_pre", 1024), ("norm_mix_post", 1024), ("norm_ffn_pre", 1024), ("norm_ffn_post", 1024),
    ("mla_q_norm", 256), ("mla_kv_norm", 128), ("ssd_conv_b", 768), ("ssd_dt_bias", 4), ("ssd_a_log", 4),
    ("ssd_d", 4), ("ssd_norm", 256), ("ffn_conv_b", 5632),
)
WEIGHTS = ("norm_mix_pre", "norm_mix_post", "norm_ffn_pre", "norm_ffn_post", "w_in", "mla_q_norm", "mla_w_q_up",
           "mla_kv_norm", "mla_w_kv_up", "sc_conv_w", "ssd_conv_w", "ssd_conv_b", "ssd_dt_bias", "ssd_a_log", "ssd_d",
           "ssd_norm", "w_out", "ffn_w_up", "ffn_conv_w", "ffn_conv_b", "ffn_w_down")


def _dot(a, b, dims=NN, precision=None):
    return lax.dot_general(a, b, dims, precision=precision, preferred_element_type=F32)


def _sig(v):
    return 1.0 / (1.0 + jnp.exp(-v))


def _cp(*sem):
    return pltpu.CompilerParams(dimension_semantics=sem, vmem_limit_bytes=VMEM_LIMIT)


def _rowsum(v):
    return jnp.sum(v, axis=0, keepdims=True)


def _prev_halo(i, ts):
    return jnp.maximum(i * (ts // HALO) - 1, 0)


def _next_halo(i, ts, n):
    return jnp.minimum((i + 1) * (ts // HALO), n * (ts // HALO) - 1)


def _all_gather(xs, name):
    n = len(xs)

    def body(*refs):
        x_refs, out_refs = refs[:n], refs[n:2 * n]
        send_sems, recv_sems, local_sems = refs[2 * n:]
        x, y, cc = lax.axis_index("x"), lax.axis_index("y"), lax.axis_index("c")
        me, sibling = (x, y, cc), (x, y, 1 - cc)
        chips = [(1 - x, y), (x, 1 - y), (1 - x, 1 - y)]

        def rows(t, px, py, pc):
            return out_refs[t].at[4 * px + 2 * py + pc]

        def copy(t, k, block, to, src=None):
            return pltpu.make_async_remote_copy(
                src_ref=rows(t, *block) if src is None else src, dst_ref=rows(t, *block),
                send_sem=send_sems.at[7 * t + k], recv_sem=recv_sems.at[7 * t + k], device_id=to, device_id_type=pl.DeviceIdType.MESH)

        mine = [pltpu.make_async_copy(x_refs[t], rows(t, *me), local_sems.at[t]) for t in range(n)]
        for cp in mine:
            cp.start()
        first = []
        for t in range(n):
            first.append(copy(t, 0, me, sibling, src=x_refs[t]))
            first += [copy(t, 1 + j, me, (*chip, cc), src=x_refs[t]) for j, chip in enumerate(chips)]
        for cp in first:
            cp.start()
        passed = []
        for j, chip in enumerate(chips):
            for t in range(n):
                copy(t, 1 + j, (*chip, cc), me).wait_recv()
                passed.append(copy(t, 4 + j, (*chip, cc), sibling))
                passed[-1].start()
        for t in range(n):
            copy(t, 0, sibling, me).wait_recv()
            for j, chip in enumerate(chips):
                copy(t, 4 + j, (*chip, 1 - cc), me).wait_recv()
        for cp in first + passed:
            cp.wait_send()
        for cp in mine:
            cp.wait()

    anyspec = pl.BlockSpec(memory_space=pl.ANY)
    return pl.pallas_call(
        body, name=name, out_shape=[jax.ShapeDtypeStruct((NDEV,) + a.shape, a.dtype) for a in xs],
        in_specs=[anyspec] * n, out_specs=[anyspec] * n,
        scratch_shapes=[pltpu.SemaphoreType.DMA((7 * n,)), pltpu.SemaphoreType.DMA((7 * n,)), pltpu.SemaphoreType.DMA((n,))],
    )(*xs)


def _all_to_all(xs, name):
    n = len(xs)

    def body(*refs):
        x_refs, out_refs = refs[:n], refs[n:2 * n]
        send_sems, recv_sems, local_sems = refs[2 * n:]
        x, y, cc = lax.axis_index("x"), lax.axis_index("y"), lax.axis_index("c")
        me = 4 * x + 2 * y + cc
        mine = [pltpu.make_async_copy(x_refs[t].at[me], out_refs[t].at[me], local_sems.at[t]) for t in range(n)]
        for cp in mine:
            cp.start()
        copies = []
        for k in range(1, NDEV):
            px = 1 - x if k & 4 else x
            py = 1 - y if k & 2 else y
            pc = 1 - cc if k & 1 else cc
            peer = 4 * px + 2 * py + pc
            for t in range(n):
                copies.append(pltpu.make_async_remote_copy(
                    src_ref=x_refs[t].at[peer], dst_ref=out_refs[t].at[me], send_sem=send_sems.at[7 * t + k - 1],
                    recv_sem=recv_sems.at[7 * t + k - 1], device_id=(px, py, pc), device_id_type=pl.DeviceIdType.MESH))
        for cp in copies:
            cp.start()
        for cp in copies:
            cp.wait()
        for cp in mine:
            cp.wait()

    anyspec = pl.BlockSpec(memory_space=pl.ANY)
    return pl.pallas_call(
        body, name=name, out_shape=[jax.ShapeDtypeStruct(a.shape, a.dtype) for a in xs],
        in_specs=[anyspec] * n, out_specs=[anyspec] * n,
        scratch_shapes=[pltpu.SemaphoreType.DMA((7 * n,)), pltpu.SemaphoreType.DMA((7 * n,)), pltpu.SemaphoreType.DMA((n,))],
    )(*xs)


def _mm(name, a, b, out_shape, grid, a_spec, b_spec, o_spec, dims, acc_shape):
    nk = grid[2]

    def body(a_ref, b_ref, o_ref, acc_ref):
        k = pl.program_id(2)

        @pl.when(k == 0)
        def _():
            acc_ref[...] = jnp.zeros_like(acc_ref)

        acc_ref[...] += _dot(a_ref[...], b_ref[...], dims)

        @pl.when(k == nk - 1)
        def _():
            o_ref[...] = acc_ref[...].astype(o_ref.dtype)

    return pl.pallas_call(
        body, name=name, grid=grid, out_shape=out_shape, in_specs=[a_spec, b_spec], out_specs=o_spec,
        scratch_shapes=[pltpu.VMEM(acc_shape, F32)], compiler_params=_cp("parallel", "parallel", "arbitrary"),
    )(a, b)


def _mm_rows(name, a, w, out_dtype, dims):
    s, k = a.shape
    n = w.shape[1] if dims == NN else w.shape[0]
    tm = min(ROW_TILE, s)
    return _mm(name, a, w, jax.ShapeDtypeStruct((s, n), out_dtype), (s // tm, 1, 1),
               pl.BlockSpec((tm, k), lambda i, j, kk: (i, 0)), pl.BlockSpec(w.shape, lambda i, j, kk: (0, 0)),
               pl.BlockSpec((tm, n), lambda i, j, kk: (i, 0)), dims, (tm, n))


def _mm_wgrad(name, a, g, out_dtype):
    s, m = a.shape
    n = g.shape[1]
    tk = min(ROW_TILE, s)
    return _mm(name, a, g, jax.ShapeDtypeStruct((m, n), out_dtype), (1, 1, s // tk),
               pl.BlockSpec((tk, m), lambda i, j, kk: (kk, 0)), pl.BlockSpec((tk, n), lambda i, j, kk: (kk, 0)),
               pl.BlockSpec((m, n), lambda i, j, kk: (0, 0)), TN, (m, n))


def _mm_up(h2, wup):
    s = h2.shape[0]
    tm = min(ROW_TILE, s)
    return _mm("ffn_up", h2, wup, jax.ShapeDtypeStruct((NDEV, s, FB), BF16), (NDEV, s // tm, 1),
               pl.BlockSpec((tm, D), lambda j, i, kk: (i, 0)), pl.BlockSpec((None, D, FB), lambda j, i, kk: (j, 0, 0)),
               pl.BlockSpec((None, tm, FB), lambda j, i, kk: (j, i, 0)), NN, (tm, FB))


def _mm_down(act, wdown):
    s = act.shape[1]
    tm = min(ROW_TILE, s)
    return _mm("ffn_down", act, wdown, jax.ShapeDtypeStruct((s, D), F32), (s // tm, 1, 4),
               pl.BlockSpec((None, tm, FB), lambda i, j, kk: (kk, i, 0)), pl.BlockSpec((None, FB, D), lambda i, j, kk: (kk, 0, 0)),
               pl.BlockSpec((tm, D), lambda i, j, kk: (i, 0)), NN, (tm, D))


def _mm_dact(df, wdown):
    s = df.shape[0]
    tm = min(ROW_TILE, s)
    return _mm("ffn_dact", df, wdown, jax.ShapeDtypeStruct((4, s, FB), BF16), (4, s // tm, 1),
               pl.BlockSpec((tm, D), lambda j, i, kk: (i, 0)), pl.BlockSpec((None, FB, D), lambda j, i, kk: (j, 0, 0)),
               pl.BlockSpec((None, tm, FB), lambda j, i, kk: (j, i, 0)), NT, (tm, FB))


def _mm_dwdown(act, df):
    s = df.shape[0]
    tk = min(ROW_TILE, s)
    return _mm("ffn_dwdown", act, df, jax.ShapeDtypeStruct((4, FB, D), BF16), (4, 1, s // tk),
               pl.BlockSpec((None, tk, FB), lambda j, i, kk: (j, kk, 0)), pl.BlockSpec((tk, D), lambda j, i, kk: (kk, 0)),
               pl.BlockSpec((None, FB, D), lambda j, i, kk: (j, 0, 0)), TN, (FB, D))


def _mm_dh2(dupre, wup):
    s = dupre.shape[1]
    tm = min(ROW_TILE, s)
    return _mm("ffn_dh2", dupre, wup, jax.ShapeDtypeStruct((s, D), BF16), (s // tm, 1, NDEV),
               pl.BlockSpec((None, tm, FB), lambda i, j, kk: (kk, i, 0)), pl.BlockSpec((None, D, FB), lambda i, j, kk: (kk, 0, 0)),
               pl.BlockSpec((tm, D), lambda i, j, kk: (i, 0)), NT, (tm, D))


def _mm_dwup(h2, dupre):
    s = h2.shape[0]
    tk = min(ROW_TILE, s)
    return _mm("ffn_dwup", h2, dupre, jax.ShapeDtypeStruct((NDEV, D, FB), BF16), (NDEV, 1, s // tk),
               pl.BlockSpec((tk, D), lambda j, i, kk: (kk, 0)), pl.BlockSpec((None, tk, FB), lambda j, i, kk: (j, kk, 0)),
               pl.BlockSpec((None, D, FB), lambda j, i, kk: (j, 0, 0)), TN, (D, FB))


def _rms(xv, w, out_dtype, name):
    s, d = xv.shape
    ts = min(ROW_TILE, s)

    def body(x_ref, w_ref, o_ref):
        xf = x_ref[...].astype(F32)
        r = lax.rsqrt(jnp.mean(xf * xf, axis=-1, keepdims=True) + EPS)
        o_ref[...] = (xf * r * w_ref[...]).astype(o_ref.dtype)

    return pl.pallas_call(
        body, name=name, grid=(s // ts,), out_shape=jax.ShapeDtypeStruct((s, d), out_dtype),
        in_specs=[pl.BlockSpec((ts, d), lambda i: (i, 0)), pl.BlockSpec((1, d), lambda i: (0, 0))],
        out_specs=pl.BlockSpec((ts, d), lambda i: (i, 0)), compiler_params=_cp("parallel"),
    )(xv, w)


def _add_rms(xv, mv, w, name):
    s, d = xv.shape
    ts = min(ROW_TILE, s)

    def body(x_ref, m_ref, w_ref, o_ref):
        mf = m_ref[...].astype(F32)
        r = lax.rsqrt(jnp.mean(mf * mf, axis=-1, keepdims=True) + EPS)
        o_ref[...] = x_ref[...] + mf * r * w_ref[...]

    return pl.pallas_call(
        body, name=name, grid=(s // ts,), out_shape=jax.ShapeDtypeStruct((s, d), F32),
        in_specs=[pl.BlockSpec((ts, d), lambda i: (i, 0)), pl.BlockSpec((ts, d), lambda i: (i, 0)), pl.BlockSpec((1, d), lambda i: (0, 0))],
        out_specs=pl.BlockSpec((ts, d), lambda i: (i, 0)), compiler_params=_cp("parallel"),
    )(xv, mv, w)


def _rms_bwd_math(xf, w, dy):
    r = lax.rsqrt(jnp.mean(xf * xf, axis=-1, keepdims=True) + EPS)
    xh = xf * r
    dxh = dy * w
    dx = r * (dxh - xh * jnp.mean(dxh * xh, axis=-1, keepdims=True))
    return dx, dy * xh


def _rms_bwd(xv, w, dy, dres, out_dtype, name):
    s, d = xv.shape
    ts = min(ROW_TILE, s)
    with_res = dres is not None

    def body(*refs):
        if with_res:
            x_ref, w_ref, dy_ref, dres_ref, dx_ref, dw_ref = refs
        else:
            x_ref, w_ref, dy_ref, dx_ref, dw_ref = refs
        dx, dwt = _rms_bwd_math(x_ref[...].astype(F32), w_ref[...], dy_ref[...].astype(F32))
        if with_res:
            dx = dx + dres_ref[...]
        dx_ref[...] = dx.astype(dx_ref.dtype)

        @pl.when(pl.program_id(0) == 0)
        def _():
            dw_ref[...] = jnp.zeros_like(dw_ref)

        dw_ref[...] += _rowsum(dwt)

    row = pl.BlockSpec((ts, d), lambda i: (i, 0))
    vec = pl.BlockSpec((1, d), lambda i: (0, 0))
    ins = [xv, w, dy] + ([dres] if with_res else [])
    return pl.pallas_call(
        body, name=name, grid=(s // ts,),
        out_shape=(jax.ShapeDtypeStruct((s, d), out_dtype), jax.ShapeDtypeStruct((1, d), F32)),
        in_specs=[row, vec, row] + ([row] if with_res else []), out_specs=(row, vec), compiler_params=_cp("arbitrary"),
    )(*ins)


def _loss_head(yv, tv):
    s, d = yv.shape
    ts = min(ROW_TILE, s)

    def body(y_ref, t_ref, l_ref, dy_ref):
        e = y_ref[...] - t_ref[...]
        dy_ref[...] = e * (1.0 / d)

        @pl.when(pl.program_id(0) == 0)
        def _():
            l_ref[...] = jnp.zeros_like(l_ref)

        tot = jnp.sum(jnp.sum(e * e, axis=1, keepdims=True), axis=0, keepdims=True)
        l_ref[...] += jnp.broadcast_to(tot * (0.5 / d), (8, LANE))

    row = pl.BlockSpec((ts, d), lambda i: (i, 0))
    return pl.pallas_call(
        body, name="loss_head", grid=(s // ts,),
        out_shape=(jax.ShapeDtypeStruct((8, LANE), F32), jax.ShapeDtypeStruct((s, d), F32)),
        in_specs=[row, row], out_specs=(pl.BlockSpec((8, LANE), lambda i: (0, 0)), row), compiler_params=_cp("arbitrary"),
    )(yv, tv)


def _rope(v, c, a, b):
    return v * c + pltpu.roll(v, LANE - 16, 1) * a + pltpu.roll(v, 16, 1) * b


def _rope_t(dv, c, a, b):
    return dv * c + pltpu.roll(dv * a, 16, 1) + pltpu.roll(dv * b, LANE - 16, 1)


def _mla_prep(proj, tabs, qnw, kvnw, wq, wkv):
    s = proj.shape[0]
    ts = min(ROW_TILE, s)
    tc, ta, tb = tabs

    def body(cq_ref, ckv_ref, kr_ref, c_ref, a_ref, b_ref, qnw_ref, kvnw_ref, wq_ref, wkv_ref, q_ref, k_ref, kv_ref):
        c, a, b = c_ref[...], a_ref[...], b_ref[...]
        cq = cq_ref[...].astype(F32)
        qn = (cq * lax.rsqrt(jnp.mean(cq * cq, axis=-1, keepdims=True) + EPS) * qnw_ref[...]).astype(BF16)
        ckv = ckv_ref[...].astype(F32)
        kvn = (ckv * lax.rsqrt(jnp.mean(ckv * ckv, axis=-1, keepdims=True) + EPS) * kvnw_ref[...]).astype(BF16)
        kr = _rope(kr_ref[...].astype(F32), c, a, b)
        lane = lax.broadcasted_iota(jnp.int32, (ts, LANE), 1)
        for h in range(HEADS):
            q_ref[h] = _rope(_dot(qn, wq_ref[h]), c, a, b).astype(BF16)
            kv = _dot(kvn, wkv_ref[h])
            kv_ref[h] = kv.astype(BF16)
            k_ref[h] = jnp.where(lane < NOPE, kv, kr).astype(BF16)

    tab = pl.BlockSpec((ts, LANE), lambda i: (i, 0))
    hd = pl.BlockSpec((HEADS, ts, LANE), lambda i: (0, i, 0))
    out = jax.ShapeDtypeStruct((HEADS, s, LANE), BF16)
    return pl.pallas_call(
        body, name="mla_prep", grid=(s // ts,), out_shape=(out, out, out),
        in_specs=[pl.BlockSpec((ts, QL), lambda i: (i, 0)), pl.BlockSpec((ts, LANE), lambda i: (i, 2)),
                  pl.BlockSpec((ts, LANE), lambda i: (i, 3)), tab, tab, tab,
                  pl.BlockSpec((1, QL), lambda i: (0, 0)), pl.BlockSpec((1, KVL), lambda i: (0, 0)),
                  pl.BlockSpec((HEADS, QL, LANE), lambda i: (0, 0, 0)), pl.BlockSpec((HEADS, KVL, LANE), lambda i: (0, 0, 0))],
        out_specs=(hd, hd, hd), compiler_params=_cp("parallel"),
    )(proj, proj, proj, tc, ta, tb, qnw, kvnw, wq, wkv)


def _mla_prep_bwd(proj, tabs, qnw, kvnw, wq, wkv, dq, dk, dv):
    s = proj.shape[0]
    ts = min(ROW_TILE, s)
    tc, ta, tb = tabs

    def body(cq_ref, ckv_ref, c_ref, a_ref, b_ref, qnw_ref, kvnw_ref, wq_ref, wkv_ref, dq_ref, dk_ref, dv_ref,
             dcq_ref, dckv_ref, dkr_ref, dwq_ref, dwkv_ref, dqnw_ref, dkvnw_ref):
        @pl.when(pl.program_id(0) == 0)
        def _():
            dwq_ref[...] = jnp.zeros_like(dwq_ref)
            dwkv_ref[...] = jnp.zeros_like(dwkv_ref)
            dqnw_ref[...] = jnp.zeros_like(dqnw_ref)
            dkvnw_ref[...] = jnp.zeros_like(dkvnw_ref)

        c, a, b = c_ref[...], a_ref[...], b_ref[...]
        cq = cq_ref[...].astype(F32)
        qn = (cq * lax.rsqrt(jnp.mean(cq * cq, axis=-1, keepdims=True) + EPS) * qnw_ref[...]).astype(BF16)
        ckv = ckv_ref[...].astype(F32)
        kvn = (ckv * lax.rsqrt(jnp.mean(ckv * ckv, axis=-1, keepdims=True) + EPS) * kvnw_ref[...]).astype(BF16)
        lane = lax.broadcasted_iota(jnp.int32, (ts, LANE), 1)
        dqn = jnp.zeros((ts, QL), F32)
        dkvn = jnp.zeros((ts, KVL), F32)
        dkr = jnp.zeros((ts, LANE), F32)
        for h in range(HEADS):
            dqh = _rope_t(dq_ref[h], c, a, b).astype(BF16)
            dwq_ref[h] += _dot(qn, dqh, TN)
            dqn += _dot(dqh, wq_ref[h], NT)
            dkh = dk_ref[h].astype(F32)
            dkvh = jnp.where(lane < NOPE, dkh, dv_ref[h].astype(F32)).astype(BF16)
            dkr += jnp.where(lane < NOPE, 0.0, dkh)
            dwkv_ref[h] += _dot(kvn, dkvh, TN)
            dkvn += _dot(dkvh, wkv_ref[h], NT)
        dkr_ref[...] = _rope_t(dkr, c, a, b).astype(BF16)
        dcq, dwt = _rms_bwd_math(cq, qnw_ref[...], dqn)
        dcq_ref[...] = dcq.astype(BF16)
        dqnw_ref[...] += _rowsum(dwt)
        dckv, dwt = _rms_bwd_math(ckv, kvnw_ref[...], dkvn)
        dckv_ref[...] = dckv.astype(BF16)
        dkvnw_ref[...] += _rowsum(dwt)

    tab = pl.BlockSpec((ts, LANE), lambda i: (i, 0))
    hd = pl.BlockSpec((HEADS, ts, LANE), lambda i: (0, i, 0))
    wq_spec = pl.BlockSpec((HEADS, QL, LANE), lambda i: (0, 0, 0))
    wkv_spec = pl.BlockSpec((HEADS, KVL, LANE), lambda i: (0, 0, 0))
    return pl.pallas_call(
        body, name="mla_prep_bwd", grid=(s // ts,),
        out_shape=(jax.ShapeDtypeStruct((s, QL), BF16), jax.ShapeDtypeStruct((s, KVL), BF16), jax.ShapeDtypeStruct((s, LANE), BF16),
                   jax.ShapeDtypeStruct((HEADS, QL, LANE), F32), jax.ShapeDtypeStruct((HEADS, KVL, LANE), F32),
                   jax.ShapeDtypeStruct((1, QL), F32), jax.ShapeDtypeStruct((1, KVL), F32)),
        in_specs=[pl.BlockSpec((ts, QL), lambda i: (i, 0)), pl.BlockSpec((ts, LANE), lambda i: (i, 2)), tab, tab, tab,
                  pl.BlockSpec((1, QL), lambda i: (0, 0)), pl.BlockSpec((1, KVL), lambda i: (0, 0)), wq_spec, wkv_spec, hd, hd, hd],
        out_specs=(pl.BlockSpec((ts, QL), lambda i: (i, 0)), pl.BlockSpec((ts, KVL), lambda i: (i, 0)), tab, wq_spec, wkv_spec,
                   pl.BlockSpec((1, QL), lambda i: (0, 0)), pl.BlockSpec((1, KVL), lambda i: (0, 0))),
        compiler_params=_cp("arbitrary"),
    )(proj, proj, tc, ta, tb, qnw, kvnw, wq, wkv, dq, dk, dv)


def _causal_mask(i, j, tq, tk):
    row = i * tq + lax.broadcasted_iota(jnp.int32, (tq, tk), 0)
    col = j * tk + lax.broadcasted_iota(jnp.int32, (tq, tk), 1)
    return row >= col


def _flash_fwd(q, k, kv):
    s = q.shape[1]
    t = min(ATT_TILE, s)
    n = s // t

    def body(q_ref, k_ref, kv_ref, o_ref, lse_ref, m_sc, l_sc, acc_sc):
        i, j = pl.program_id(1), pl.program_id(2)

        @pl.when(j == 0)
        def _():
            m_sc[...] = jnp.full_like(m_sc, NEG)
            l_sc[...] = jnp.zeros_like(l_sc)
            acc_sc[...] = jnp.zeros_like(acc_sc)

        @pl.when(j <= i)
        def _():
            sc = _dot(q_ref[...], k_ref[...], NT) * ATT_SCALE
            sc = jnp.where(_causal_mask(i, j, t, t), sc, NEG)
            m_new = jnp.maximum(m_sc[...], jnp.max(sc, axis=1, keepdims=True))
            alpha = jnp.exp(m_sc[...] - m_new)
            p = jnp.exp(sc - m_new)
            l_sc[...] = alpha * l_sc[...] + jnp.sum(p, axis=1, keepdims=True)
            acc_sc[...] = alpha * acc_sc[...] + _dot(p.astype(BF16), kv_ref[...])
            m_sc[...] = m_new

        @pl.when(j == n - 1)
        def _():
            o_ref[...] = (acc_sc[...] / l_sc[...]).astype(BF16)
            lse_ref[...] = m_sc[...] + jnp.log(l_sc[...])

    kspec = pl.BlockSpec((None, t, LANE), lambda h, i, j: (h, jnp.minimum(j, i), 0))
    return pl.pallas_call(
        body, name="flash_fwd", grid=(HEADS, n, n),
        out_shape=(jax.ShapeDtypeStruct((s, HEADS * LANE), BF16), jax.ShapeDtypeStruct((HEADS, s, 1), F32)),
        in_specs=[pl.BlockSpec((None, t, LANE), lambda h, i, j: (h, i, 0)), kspec, kspec],
        out_specs=(pl.BlockSpec((t, LANE), lambda h, i, j: (i, h)), pl.BlockSpec((None, t, 1), lambda h, i, j: (h, i, 0))),
        scratch_shapes=[pltpu.VMEM((t, 1), F32), pltpu.VMEM((t, 1), F32), pltpu.VMEM((t, LANE), F32)],
        compiler_params=_cp("parallel", "parallel", "arbitrary"),
    )(q, k, kv)


def _flash_bwd(q, k, kv, cat, dcat, lse):
    s = q.shape[1]
    t = min(ATT_TILE, s)
    n = s // t

    def body(q_ref, k_ref, kv_ref, o_ref, do_ref, lse_ref, dq_ref, dk_ref, dv_ref, dk_sc, dv_sc):
        j, i = pl.program_id(1), pl.program_id(2)

        @pl.when((j == 0) & (i == 0))
        def _():
            dq_ref[...] = jnp.zeros_like(dq_ref)

        @pl.when(i == 0)
        def _():
            dk_sc[...] = jnp.zeros_like(dk_sc)
            dv_sc[...] = jnp.zeros_like(dv_sc)

        @pl.when(i >= j)
        def _():
            qv, kk, do = q_ref[...], k_ref[...], do_ref[...]
            sc = _dot(qv, kk, NT) * ATT_SCALE
            p = jnp.where(_causal_mask(i, j, t, t), jnp.exp(sc - lse_ref[...]), 0.0)
            dp = _dot(do, kv_ref[...], NT)
            delta = jnp.sum(do.astype(F32) * o_ref[...].astype(F32), axis=1, keepdims=True)
            ds = (p * (dp - delta) * ATT_SCALE).astype(BF16)
            dv_sc[...] += _dot(p.astype(BF16), do, TN)
            dk_sc[...] += _dot(ds, qv, TN)
            rows = pl.ds(pl.multiple_of(i * t, t), t)
            dq_ref[rows, :] += _dot(ds, kk)

        @pl.when(i == n - 1)
        def _():
            dk_ref[...] = dk_sc[...].astype(BF16)
            dv_ref[...] = dv_sc[...].astype(BF16)

    qspec = pl.BlockSpec((None, t, LANE), lambda h, j, i: (h, jnp.maximum(i, j), 0))
    kspec = pl.BlockSpec((None, t, LANE), lambda h, j, i: (h, j, 0))
    ospec = pl.BlockSpec((t, LANE), lambda h, j, i: (jnp.maximum(i, j), h))
    return pl.pallas_call(
        body, name="flash_bwd", grid=(HEADS, n, n),
        out_shape=(jax.ShapeDtypeStruct((HEADS, s, LANE), F32), jax.ShapeDtypeStruct((HEADS, s, LANE), BF16),
                   jax.ShapeDtypeStruct((HEADS, s, LANE), BF16)),
        in_specs=[qspec, kspec, kspec, ospec, ospec, pl.BlockSpec((None, t, 1), lambda h, j, i: (h, jnp.maximum(i, j), 0))],
        out_specs=(pl.BlockSpec((None, s, LANE), lambda h, j, i: (h, 0, 0)), kspec, kspec),
        scratch_shapes=[pltpu.VMEM((t, LANE), F32), pltpu.VMEM((t, LANE), F32)],
        compiler_params=_cp("parallel", "arbitrary", "arbitrary"),
    )(q, k, kv, cat, dcat, lse)


def _conv3(ext, w_ref, ts):
    return (w_ref[0:1, :] * ext[pl.ds(HALO - 2, ts), :] + w_ref[1:2, :] * ext[pl.ds(HALO - 1, ts), :]
            + w_ref[2:3, :] * ext[pl.ds(HALO, ts), :])


def _conv3_t(ext2, w_ref, ts):
    return (w_ref[0:1, :] * ext2[pl.ds(2, ts), :] + w_ref[1:2, :] * ext2[pl.ds(1, ts), :] + w_ref[2:3, :] * ext2[pl.ds(0, ts), :])


def _sconv_fwd(proj, w):
    s = proj.shape[0]
    ts = min(ROW_TILE, s)

    def body(b_ref, c_ref, h_ref, hc_ref, hh_ref, w_ref, o_ref, ext):
        i = pl.program_id(0)
        ext[0:HALO, :] = hc_ref[...].astype(F32) * hh_ref[...].astype(F32) * (i > 0).astype(F32)
        ext[HALO:HALO + ts, :] = c_ref[...].astype(F32) * h_ref[...].astype(F32)
        o_ref[...] = (b_ref[...].astype(F32) * _conv3(ext, w_ref, ts)).astype(BF16)

    def col(cb):
        return pl.BlockSpec((ts, SC), lambda i: (i, cb))

    def halo(cb):
        return pl.BlockSpec((HALO, SC), lambda i: (_prev_halo(i, ts), cb))

    return pl.pallas_call(
        body, name="sconv_fwd", grid=(s // ts,), out_shape=jax.ShapeDtypeStruct((s, SC), BF16),
        in_specs=[col(2), col(3), col(4), halo(3), halo(4), pl.BlockSpec((3, SC), lambda i: (0, 0))],
        out_specs=pl.BlockSpec((ts, SC), lambda i: (i, 0)), scratch_shapes=[pltpu.VMEM((ts + HALO, SC), F32)],
        compiler_params=_cp("parallel"),
    )(proj, proj, proj, proj, proj, w)


def _sconv_bwd(proj, dcat, w):
    s = proj.shape[0]
    ts = min(ROW_TILE, s)
    n = s // ts

    def body(b_ref, c_ref, h_ref, hc_ref, hh_ref, dy_ref, ndy_ref, nb_ref, w_ref, db_ref, dc_ref, dh_ref, dw_ref, ext, ext2):
        i = pl.program_id(0)

        @pl.when(i == 0)
        def _():
            dw_ref[...] = jnp.zeros_like(dw_ref)

        cv, hv, bv = c_ref[...].astype(F32), h_ref[...].astype(F32), b_ref[...].astype(F32)
        ext[0:HALO, :] = hc_ref[...].astype(F32) * hh_ref[...].astype(F32) * (i > 0).astype(F32)
        ext[HALO:HALO + ts, :] = cv * hv
        dy = dy_ref[...].astype(F32)
        db_ref[...] = (dy * _conv3(ext, w_ref, ts)).astype(BF16)
        dyb = dy * bv
        ext2[0:ts, :] = dyb
        ext2[ts:ts + HALO, :] = ndy_ref[...].astype(F32) * nb_ref[...].astype(F32) * (i < n - 1).astype(F32)
        dg = _conv3_t(ext2, w_ref, ts)
        dc_ref[...] = (dg * hv).astype(BF16)
        dh_ref[...] = (dg * cv).astype(BF16)
        for kk in range(3):
            dw_ref[kk:kk + 1, :] += _rowsum(dyb * ext[pl.ds(HALO - 2 + kk, ts), :])

    def col(cb):
        return pl.BlockSpec((ts, SC), lambda i: (i, cb))

    def halo(cb):
        return pl.BlockSpec((HALO, SC), lambda i: (_prev_halo(i, ts), cb))

    def nxt(cb):
        return pl.BlockSpec((HALO, SC), lambda i: (_next_halo(i, ts, n), cb))

    out = jax.ShapeDtypeStruct((s, SC), BF16)
    o0 = pl.BlockSpec((ts, SC), lambda i: (i, 0))
    return pl.pallas_call(
        body, name="sconv_bwd", grid=(n,), out_shape=(out, out, out, jax.ShapeDtypeStruct((3, SC), F32)),
        in_specs=[col(2), col(3), col(4), halo(3), halo(4), col(4), nxt(4), nxt(2), pl.BlockSpec((3, SC), lambda i: (0, 0))],
        out_specs=(o0, o0, o0, pl.BlockSpec((3, SC), lambda i: (0, 0))),
        scratch_shapes=[pltpu.VMEM((ts + HALO, SC), F32), pltpu.VMEM((ts + HALO, SC), F32)], compiler_params=_cp("arbitrary"),
    )(proj, proj, proj, proj, proj, dcat, dcat, proj, w)


def _ffn_stage(ext, u_ref, halo_ref, i, ts):
    ext[0:HALO, :] = halo_ref[...].astype(F32) * (i > 0).astype(F32)
    ext[HALO:HALO + ts, :] = u_ref[...].astype(F32)


def _ffn_specs(ts):
    cur = pl.BlockSpec((2, None, ts, FB), lambda j, i: (0, j, i, 0))
    halo = pl.BlockSpec((2, None, HALO, FB), lambda j, i: (0, j, _prev_halo(i, ts), 0))
    w = pl.BlockSpec((2, None, 3, FB), lambda j, i: (0, j, 0, 0))
    b = pl.BlockSpec((2, None, 1, FB), lambda j, i: (0, j, 0, 0))
    return cur, halo, w, b


def _ffn_act(upre, fcw, fcb):
    s = upre.shape[1]
    ts = min(ROW_TILE, s)

    def body(u_ref, halo_ref, w_ref, b_ref, o_ref, ext):
        i = pl.program_id(1)
        _ffn_stage(ext, u_ref.at[0], halo_ref.at[0], i, ts)
        gate = b_ref[0] + _conv3(ext, w_ref.at[0], ts)
        _ffn_stage(ext, u_ref.at[1], halo_ref.at[1], i, ts)
        up = b_ref[1] + _conv3(ext, w_ref.at[1], ts)
        o_ref[...] = (gate * _sig(gate) * up).astype(BF16)

    cur, halo, w, b = _ffn_specs(ts)
    u4 = upre.reshape(2, 4, s, FB)
    return pl.pallas_call(
        body, name="ffn_act", grid=(4, s // ts), out_shape=jax.ShapeDtypeStruct((4, s, FB), BF16),
        in_specs=[cur, halo, w, b], out_specs=pl.BlockSpec((None, ts, FB), lambda j, i: (j, i, 0)),
        scratch_shapes=[pltpu.VMEM((ts + HALO, FB), F32)], compiler_params=_cp("parallel", "parallel"),
    )(u4, u4, fcw.reshape(2, 4, 3, FB), fcb.reshape(2, 4, 1, FB))


def _ffn_bwd_a(upre, dact, fcw, fcb):
    s = upre.shape[1]
    ts = min(ROW_TILE, s)

    def body(u_ref, halo_ref, w_ref, b_ref, da_ref, du_ref, db_ref, ext):
        i = pl.program_id(1)

        @pl.when(i == 0)
        def _():
            db_ref[...] = jnp.zeros_like(db_ref)

        _ffn_stage(ext, u_ref.at[0], halo_ref.at[0], i, ts)
        gate = b_ref[0] + _conv3(ext, w_ref.at[0], ts)
        _ffn_stage(ext, u_ref.at[1], halo_ref.at[1], i, ts)
        up = b_ref[1] + _conv3(ext, w_ref.at[1], ts)
        sg = _sig(gate)
        da = da_ref[...].astype(F32)
        dgate = da * up * sg * (1.0 + gate * (1.0 - sg))
        dup = da * gate * sg
        du_ref[0] = dgate.astype(BF16)
        du_ref[1] = dup.astype(BF16)
        db_ref[0] += _rowsum(dgate)
        db_ref[1] += _rowsum(dup)

    cur, halo, w, b = _ffn_specs(ts)
    u4 = upre.reshape(2, 4, s, FB)
    du, db = pl.pallas_call(
        body, name="ffn_bwd_a", grid=(4, s // ts),
        out_shape=(jax.ShapeDtypeStruct((2, 4, s, FB), BF16), jax.ShapeDtypeStruct((2, 4, 1, FB), F32)),
        in_specs=[cur, halo, w, b, pl.BlockSpec((None, ts, FB), lambda j, i: (j, i, 0))], out_specs=(cur, b),
        scratch_shapes=[pltpu.VMEM((ts + HALO, FB), F32)], compiler_params=_cp("parallel", "arbitrary"),
    )(u4, u4, fcw.reshape(2, 4, 3, FB), fcb.reshape(2, 4, 1, FB), dact)
    return du.reshape(NDEV, s, FB), db.reshape(NDEV, 1, FB)


def _ffn_bwd_b(du, upre, fcw):
    s = upre.shape[1]
    ts = min(ROW_TILE, s)
    n = s // ts

    def body(du_ref, ndu_ref, u_ref, halo_ref, w_ref, dup_ref, dw_ref, ext, ext2):
        i = pl.program_id(1)

        @pl.when(i == 0)
        def _():
            dw_ref[...] = jnp.zeros_like(dw_ref)

        duv = du_ref[...].astype(F32)
        ext2[0:ts, :] = duv
        ext2[ts:ts + HALO, :] = ndu_ref[...].astype(F32) * (i < n - 1).astype(F32)
        dup_ref[...] = _conv3_t(ext2, w_ref, ts).astype(BF16)
        _ffn_stage(ext, u_ref, halo_ref, i, ts)
        for kk in range(3):
            dw_ref[kk:kk + 1, :] += _rowsum(duv * ext[pl.ds(HALO - 2 + kk, ts), :])

    cur = pl.BlockSpec((None, ts, FB), lambda j, i: (j, i, 0))
    w = pl.BlockSpec((None, 3, FB), lambda j, i: (j, 0, 0))
    return pl.pallas_call(
        body, name="ffn_bwd_b", grid=(NDEV, n),
        out_shape=(jax.ShapeDtypeStruct((NDEV, s, FB), BF16), jax.ShapeDtypeStruct((NDEV, 3, FB), F32)),
        in_specs=[cur, pl.BlockSpec((None, HALO, FB), lambda j, i: (j, _next_halo(i, ts, n), 0)), cur,
                  pl.BlockSpec((None, HALO, FB), lambda j, i: (j, _prev_halo(i, ts), 0)), w],
        out_specs=(cur, w), scratch_shapes=[pltpu.VMEM((ts + HALO, FB), F32), pltpu.VMEM((ts + HALO, FB), F32)],
        compiler_params=_cp("parallel", "arbitrary"),
    )(du, du, upre, upre, fcw)


def _softplus(v):
    e = jnp.exp(-jnp.abs(v))
    return jnp.maximum(v, 0.0) + jnp.where(e < 1e-4, e * (1.0 - 0.5 * e), jnp.log(1.0 + e))


def _ssd_consts():
    L = SSD_L
    r = lax.broadcasted_iota(jnp.int32, (L, L), 0)
    c = lax.broadcasted_iota(jnp.int32, (L, L), 1)
    tri = r >= c
    er = lax.broadcasted_iota(jnp.int32, (LANE, SSD_DIM), 0)
    ec = lax.broadcasted_iota(jnp.int32, (LANE, SSD_DIM), 1)
    expand = ((ec >= er * 64) & (ec < er * 64 + 64)).astype(F32)
    return tri, expand


def _ssd_conv4(ext, cw_ref, cb_ref):
    L = SSD_L
    pre = cb_ref[...] + cw_ref[0:1, :] * ext[pl.ds(HALO - 3, L), :]
    for kk in range(1, 4):
        pre = pre + cw_ref[kk:kk + 1, :] * ext[pl.ds(HALO - 3 + kk, L), :]
    return pre


def _ssd_common(xbc_ref, halo_ref, dt_ref, cw_ref, cb_ref, dtb_ref, alog_ref, ext, first):
    L = SSD_L
    tri, expand = _ssd_consts()
    ext[0:HALO, :] = halo_ref[...].astype(F32) * (1.0 - first.astype(F32))
    ext[HALO:HALO + L, :] = xbc_ref[...].astype(F32)
    pre = _ssd_conv4(ext, cw_ref, cb_ref)
    sg = _sig(pre)
    act = pre * sg
    lane = lax.broadcasted_iota(jnp.int32, (1, LANE), 1)
    m4 = lane < SSD_H
    raw = dt_ref[...].astype(F32) + dtb_ref[...]
    dtv = jnp.where(m4, _softplus(raw), 0.0)
    av = jnp.where(m4, -jnp.exp(alog_ref[...]), 0.0)
    adt = dtv * av
    acs = _dot(tri.astype(F32), adt, precision=HI)
    acs_b = _dot(acs, expand, precision=HI)
    dt_b = _dot(dtv, expand, precision=HI)
    return dict(tri=tri, expand=expand, pre=pre, sg=sg, act=act, raw=raw, dtv=dtv, av=av, m4=m4, acs=acs, acs_b=acs_b,
                dt_b=dt_b, lane=lane)


def _head_terms(cm, h):
    L = SSD_L
    acs, tri = cm["acs"], cm["tri"]
    lane_l = lax.broadcasted_iota(jnp.int32, (L, LANE), 1)
    sub_l = lax.broadcasted_iota(jnp.int32, (LANE, L), 0)
    col = jnp.sum(jnp.where(lane_l == h, acs, 0.0), axis=1, keepdims=True)
    row = jnp.sum(jnp.where(sub_l == h, acs.T, 0.0), axis=0, keepdims=True)
    dec = jnp.where(tri, jnp.exp(jnp.where(tri, col - row, NEG)), 0.0)
    rowi = lax.broadcasted_iota(jnp.int32, (L, 1), 0)
    last = jnp.sum(jnp.where(rowi == L - 1, col, 0.0), axis=0, keepdims=True)
    dte = jnp.exp(last - col)
    return col, dec, last, dte


def _ssd_fwd(proj, cw, cb, dtb, alog, dvec, nw):
    s = proj.shape[0]
    L = SSD_L
    nc = s // L

    def body(z_ref, xbc_ref, halo_ref, dt_ref, cw_ref, cb_ref, dtb_ref, alog_ref, d_ref, nw_ref, y_ref, ypre_ref, st_ref, ext, state):
        i = pl.program_id(0)

        @pl.when(i == 0)
        def _():
            state[...] = jnp.zeros_like(state)

        cm = _ssd_common(xbc_ref, halo_ref, dt_ref, cw_ref, cb_ref, dtb_ref, alog_ref, ext, i == 0)
        act = cm["act"]
        xs = act[:, 0:256]
        bm = (act[:, 256:384], act[:, 384:512])
        cmat = (act[:, 512:640].astype(BF16), act[:, 640:768].astype(BF16))
        xdt = xs * cm["dt_b"]
        prev = state[...]
        st_ref[...] = prev
        prev_bf = prev.astype(BF16)
        gm = [_dot(cmat[g], bm[g].astype(BF16), NT) for g in range(2)]
        lane2 = lax.broadcasted_iota(jnp.int32, (1, SSD_DIM), 1)
        rows2 = lax.broadcasted_iota(jnp.int32, (SSD_DIM, 1), 0)
        ydiag = jnp.zeros((L, SSD_DIM), F32)
        contrib = jnp.zeros((SSD_DIM, LANE), F32)
        cd_rows = jnp.zeros((SSD_DIM, 1), F32)
        for h in range(SSD_H):
            g = h // 2
            col, dec, last, dte = _head_terms(cm, h)
            mh = (lane2 >= 64 * h) & (lane2 < 64 * h + 64)
            xm = jnp.where(mh, xdt, 0.0).astype(BF16)
            ydiag += _dot((gm[g] * dec).astype(BF16), xm)
            contrib += _dot(xm, (bm[g] * dte).astype(BF16), TN)
            cd_rows += jnp.where((rows2 >= 64 * h) & (rows2 < 64 * h + 64), jnp.exp(last), 0.0)
        yo = jnp.where(lane2 < 128, _dot(cmat[0], prev_bf, NT), _dot(cmat[1], prev_bf, NT))
        y = ydiag + yo * jnp.exp(cm["acs_b"]) + xs * d_ref[...]
        state[...] = prev * cd_rows + contrib
        ypre_ref[...] = y
        zz = z_ref[...].astype(F32)
        gt = y * zz * _sig(zz)
        y_ref[...] = (gt * lax.rsqrt(jnp.mean(gt * gt, axis=-1, keepdims=True) + EPS) * nw_ref[...]).astype(BF16)

    def vec(w):
        return pl.BlockSpec((1, w), lambda i: (0, 0))

    return pl.pallas_call(
        body, name="ssd_fwd", grid=(nc,),
        out_shape=(jax.ShapeDtypeStruct((s, SSD_DIM), BF16), jax.ShapeDtypeStruct((s, SSD_DIM), F32),
                   jax.ShapeDtypeStruct((nc, SSD_DIM, LANE), F32)),
        in_specs=[pl.BlockSpec((L, SSD_DIM), lambda i: (i, 5)), pl.BlockSpec((L, SSD_CONV), lambda i: (i, 2)),
                  pl.BlockSpec((HALO, SSD_CONV), lambda i: (_prev_halo(i, L), 2)), pl.BlockSpec((L, LANE), lambda i: (i, 18)),
                  pl.BlockSpec((4, SSD_CONV), lambda i: (0, 0)), vec(SSD_CONV), vec(LANE), vec(LANE), vec(SSD_DIM), vec(SSD_DIM)],
        out_specs=(pl.BlockSpec((L, SSD_DIM), lambda i: (i, 0)), pl.BlockSpec((L, SSD_DIM), lambda i: (i, 0)),
                   pl.BlockSpec((None, SSD_DIM, LANE), lambda i: (i, 0, 0))),
        scratch_shapes=[pltpu.VMEM((L + HALO, SSD_CONV), F32), pltpu.VMEM((SSD_DIM, LANE), F32)], compiler_params=_cp("arbitrary"),
    )(proj, proj, proj, proj, cw, cb, dtb, alog, dvec, nw)


def _ssd_bwd(proj, dcat, ypre, states, cw, cb, dtb, alog, dvec, nw):
    s = proj.shape[0]
    L = SSD_L
    nc = s // L

    def body(z_ref, xbc_ref, halo_ref, dt_ref, dy_ref, ypre_ref, st_ref, cw_ref, cb_ref, dtb_ref, alog_ref, d_ref, nw_ref,
             dz_ref, dxbc_ref, ddt_ref, dcw_ref, dcb_ref, ddtb_ref, dalog_ref, dd_ref, dnw_ref, ext, ext2, carry, dstate, ddl):
        i = pl.program_id(0)
        r = nc - 1 - i

        @pl.when(i == 0)
        def _():
            for ref in (dcw_ref, dcb_ref, ddtb_ref, dalog_ref, dd_ref, dnw_ref, carry, dstate, ddl):
                ref[...] = jnp.zeros_like(ref)

        cm = _ssd_common(xbc_ref, halo_ref, dt_ref, cw_ref, cb_ref, dtb_ref, alog_ref, ext, r == 0)
        tri, expand, act = cm["tri"], cm["expand"], cm["act"]
        xs = act[:, 0:256]
        bm = (act[:, 256:384], act[:, 384:512])
        cmat = (act[:, 512:640], act[:, 640:768])
        bm_bf = [v.astype(BF16) for v in bm]
        cm_bf = [v.astype(BF16) for v in cmat]
        dt_b = cm["dt_b"]
        xdt = xs * dt_b
        xdt_bf = xdt.astype(BF16)
        ea_b = jnp.exp(cm["acs_b"])
        prev = st_ref[...]
        prev_bf = prev.astype(BF16)
        lane2 = lax.broadcasted_iota(jnp.int32, (1, SSD_DIM), 1)
        rows2 = lax.broadcasted_iota(jnp.int32, (SSD_DIM, 1), 0)
        lane_l = lax.broadcasted_iota(jnp.int32, (L, LANE), 1)
        rowi = lax.broadcasted_iota(jnp.int32, (L, 1), 0)

        y = ypre_ref[...]
        zz = z_ref[...].astype(F32)
        sz = _sig(zz)
        gt = y * zz * sz
        dgt, dwt = _rms_bwd_math(gt, nw_ref[...], dy_ref[...].astype(F32))
        dnw_ref[...] += _rowsum(dwt)
        dy = dgt * zz * sz
        dz_ref[...] = (dgt * y * sz * (1.0 + zz * (1.0 - sz))).astype(BF16)

        ddl[0:1, :] += _rowsum(dy * xs)
        dxs = dy * d_ref[...]

        yo = jnp.where(lane2 < 128, _dot(cm_bf[0], prev_bf, NT), _dot(cm_bf[1], prev_bf, NT))
        dacs_b = dy * yo * ea_b
        dyo = dy * ea_b
        dyo_g = (jnp.where(lane2 < 128, dyo, 0.0).astype(BF16), jnp.where(lane2 >= 128, dyo, 0.0).astype(BF16))
        dc = [_dot(dyo_g[g], prev_bf) for g in range(2)]
        dprev = _dot(dyo_g[0], cm_bf[0], TN) + _dot(dyo_g[1], cm_bf[1], TN)

        gm = [_dot(cm_bf[g], bm_bf[g], NT) for g in range(2)]
        dgm = [jnp.zeros((L, L), F32), jnp.zeros((L, L), F32)]
        db = [jnp.zeros((L, LANE), F32), jnp.zeros((L, LANE), F32)]
        dxdt = jnp.zeros((L, SSD_DIM), F32)
        dacs = jnp.zeros((L, LANE), F32)
        dlast = jnp.zeros((1, LANE), F32)
        cd_rows = jnp.zeros((SSD_DIM, 1), F32)
        dst = dstate[...]
        dst_bf = dst.astype(BF16)
        dsp = dst * prev
        ones = jnp.ones((L, LANE), F32)
        for h in range(SSD_H):
            g = h // 2
            col, dec, last, dte = _head_terms(cm, h)
            mh = (lane2 >= 64 * h) & (lane2 < 64 * h + 64)
            rh = (rows2 >= 64 * h) & (rows2 < 64 * h + 64)
            sc = gm[g] * dec
            xm = jnp.where(mh, xdt, 0.0).astype(BF16)
            dym = jnp.where(mh, dy, 0.0).astype(BF16)
            dsc = _dot(dym, xdt_bf, NT)
            dxdt += _dot(sc.astype(BF16), dym, TN)
            dgm[g] += dsc * dec
            dd = dsc * sc
            rs = jnp.sum(dd, axis=1, keepdims=True)
            cs = _dot(dd, ones, TN, precision=HI)
            dacs += jnp.where(lane_l == h, rs - cs, 0.0)
            bd = (bm[g] * dte).astype(BF16)
            dxdt += jnp.where(mh, _dot(bd, dst_bf, NT), 0.0)
            dbd = _dot(xm, dst_bf)
            db[g] += dbd * dte
            tt = jnp.sum(dbd * bm[g], axis=1, keepdims=True) * dte
            dacs += jnp.where(lane_l == h, -tt, 0.0)
            cdh = jnp.exp(last)
            dcd = jnp.sum(jnp.sum(jnp.where(rh, dsp, 0.0), axis=1, keepdims=True), axis=0, keepdims=True)
            dlast += jnp.where(cm["lane"] == h, jnp.sum(tt, axis=0, keepdims=True) + dcd * cdh, 0.0)
            cd_rows += jnp.where(rh, cdh, 0.0)
        dacs += jnp.where(rowi == L - 1, dlast, 0.0)
        dacs += _dot(dacs_b, expand, NT, precision=HI)
        dstate[...] = dprev + dst * cd_rows

        for g in range(2):
            dgb = dgm[g].astype(BF16)
            dc[g] += _dot(dgb, bm_bf[g])
            db[g] += _dot(dgb, cm_bf[g], TN)

        dadt = _dot(tri.astype(F32), dacs, TN, precision=HI)
        ddtv = dadt * cm["av"] + _dot(dxdt * xs, expand, NT, precision=HI)
        dalog_ref[...] += _rowsum(dadt * cm["dtv"]) * cm["av"]
        dxs += dxdt * dt_b
        draw = jnp.where(cm["m4"], ddtv * _sig(cm["raw"]), 0.0)
        ddtb_ref[...] += _rowsum(draw)
        ddt_ref[...] = draw.astype(BF16)

        dact = jnp.concatenate([dxs, db[0], db[1], dc[0], dc[1]], axis=1)
        sg, pre = cm["sg"], cm["pre"]
        dpre = dact * sg * (1.0 + pre * (1.0 - sg))
        dcb_ref[...] += _rowsum(dpre)
        for kk in range(4):
            dcw_ref[kk:kk + 1, :] += _rowsum(dpre * ext[pl.ds(HALO - 3 + kk, L), :])
        ext2[0:L, :] = dpre
        ext2[L:L + HALO, :] = carry[...]
        dx = cw_ref[3:4, :] * ext2[pl.ds(0, L), :]
        for kk in range(3):
            dx = dx + cw_ref[kk:kk + 1, :] * ext2[pl.ds(3 - kk, L), :]
        dxbc_ref[...] = dx.astype(BF16)
        carry[...] = dpre[0:HALO, :]

        @pl.when(i == nc - 1)
        def _():
            dd_ref[...] = _dot(ddl[...], expand, NT, precision=HI)

    def vec(w):
        return pl.BlockSpec((1, w), lambda i: (0, 0))

    def rv(i):
        return nc - 1 - i

    return pl.pallas_call(
        body, name="ssd_bwd", grid=(nc,),
        out_shape=(jax.ShapeDtypeStruct((s, SSD_DIM), BF16), jax.ShapeDtypeStruct((s, SSD_CONV), BF16), jax.ShapeDtypeStruct((s, LANE), BF16),
                   jax.ShapeDtypeStruct((4, SSD_CONV), F32), jax.ShapeDtypeStruct((1, SSD_CONV), F32), jax.ShapeDtypeStruct((1, LANE), F32),
                   jax.ShapeDtypeStruct((1, LANE), F32), jax.ShapeDtypeStruct((8, LANE), F32), jax.ShapeDtypeStruct((1, SSD_DIM), F32)),
        in_specs=[pl.BlockSpec((L, SSD_DIM), lambda i: (rv(i), 5)), pl.BlockSpec((L, SSD_CONV), lambda i: (rv(i), 2)),
                  pl.BlockSpec((HALO, SSD_CONV), lambda i: (_prev_halo(rv(i), L), 2)), pl.BlockSpec((L, LANE), lambda i: (rv(i), 18)),
                  pl.BlockSpec((L, SSD_DIM), lambda i: (rv(i), 5)), pl.BlockSpec((L, SSD_DIM), lambda i: (rv(i), 0)),
                  pl.BlockSpec((None, SSD_DIM, LANE), lambda i: (rv(i), 0, 0)),
                  pl.BlockSpec((4, SSD_CONV), lambda i: (0, 0)), vec(SSD_CONV), vec(LANE), vec(LANE), vec(SSD_DIM), vec(SSD_DIM)],
        out_specs=(pl.BlockSpec((L, SSD_DIM), lambda i: (rv(i), 0)), pl.BlockSpec((L, SSD_CONV), lambda i: (rv(i), 0)),
                   pl.BlockSpec((L, LANE), lambda i: (rv(i), 0)), pl.BlockSpec((4, SSD_CONV), lambda i: (0, 0)), vec(SSD_CONV),
                   vec(LANE), vec(LANE), pl.BlockSpec((8, LANE), lambda i: (0, 0)), vec(SSD_DIM)),
        scratch_shapes=[pltpu.VMEM((L + HALO, SSD_CONV), F32), pltpu.VMEM((L + HALO, SSD_CONV), F32), pltpu.VMEM((HALO, SSD_CONV), F32),
                        pltpu.VMEM((SSD_DIM, LANE), F32), pltpu.VMEM((8, SSD_DIM), F32)],
        compiler_params=_cp("arbitrary"),
    )(proj, proj, proj, proj, dcat, ypre, states, cw, cb, dtb, alog, dvec, nw)


def _adamw(parts, w, m, v, name):
    r, c = w.shape
    tr = r
    for cand in (256, 128, 64, 32, 16, 8):
        if r % cand == 0 and (cand * c * 4) <= 2 * 1024 * 1024:
            tr = cand
            break
    c1 = 1.0 - B1 ** STEP
    c2 = 1.0 - B2 ** STEP

    def body(p_ref, w_ref, m_ref, v_ref, g_ref, d_ref, nm_ref, nv_ref):
        g = p_ref[0].astype(F32)
        for dev in range(1, NDEV):
            g = g + p_ref[dev].astype(F32)
        mn = B1 * m_ref[...] + (1.0 - B1) * g
        vn = B2 * v_ref[...] + (1.0 - B2) * (g * g)
        g_ref[...] = g
        nm_ref[...] = mn
        nv_ref[...] = vn
        d_ref[...] = -LR * ((mn / c1) / (jnp.sqrt(vn / c2) + AEPS) + WD * w_ref[...])

    blk = pl.BlockSpec((tr, c), lambda i: (i, 0))
    out = jax.ShapeDtypeStruct((r, c), F32)
    return pl.pallas_call(
        body, name=name, grid=(r // tr,), out_shape=(out, out, out, out),
        in_specs=[pl.BlockSpec((NDEV, tr, c), lambda i: (0, i, 0)), blk, blk, blk], out_specs=(blk, blk, blk, blk),
        compiler_params=_cp("parallel"),
    )(parts, w, m, v)


def _pad_win(w):
    z = lambda n: jnp.zeros((w.shape[0], n), w.dtype)
    return jnp.concatenate([w[:, :384], z(64), w[:, 384:416], z(32), w[:, 416:], z(124)], axis=1)


def _unpad_win(g):
    return jnp.concatenate([g[:, :384], g[:, 448:480], g[:, 512:2308]], axis=1)


def _pad_wout(w):
    att = jnp.pad(w[:512].reshape(HEADS, 64, D), ((0, 0), (64, 0), (0, 0))).reshape(HEADS * LANE, D)
    return jnp.concatenate([att, w[512:]], axis=0)


def _unpad_wout(g):
    att = g[:HEADS * LANE].reshape(HEADS, LANE, D)[:, 64:, :].reshape(512, D)
    return jnp.concatenate([att, g[HEADS * LANE:]], axis=0)


def _lanes(v, n=LANE):
    return jnp.pad(v, (0, n - v.shape[0])).reshape(1, n)


def _prep_layer(big, small, l):
    p = {}
    p["win"] = _pad_win(big["w_in"][:, l].reshape(D, 2212))
    p["wq"] = jnp.pad(big["mla_w_q_up"][:, l], ((0, 0), (0, 0), (0, LANE - 96)))
    p["wkv"] = big["mla_w_kv_up"][:, l]
    p["scw"] = big["sc_conv_w"][:, l].astype(F32).transpose(1, 0, 2).reshape(3, SC)
    p["ssdcw"] = big["ssd_conv_w"][:, l].astype(F32).transpose(1, 0, 2).reshape(4, SSD_CONV)
    p["wout"] = _pad_wout(big["w_out"][:, l].reshape(1024, D))
    p["wup"] = big["ffn_w_up"][:, l]
    p["fcw"] = big["ffn_conv_w"][:, l].astype(F32)
    p["wdown"] = big["ffn_w_down"][:, l].reshape(4, FB, D)
    for nm in ("norm_mix_pre", "norm_mix_post", "norm_ffn_pre", "norm_ffn_post", "mla_q_norm", "mla_kv_norm", "ssd_conv_b", "ssd_norm"):
        p[nm] = small[nm][l].reshape(1, -1)
    p["dtb"] = _lanes(small["ssd_dt_bias"][l])
    p["alog"] = _lanes(small["ssd_a_log"][l])
    p["dvec"] = jnp.repeat(small["ssd_d"][l], 64).reshape(1, SSD_DIM)
    p["fcb"] = small["ffn_conv_b"][l].reshape(NDEV, 1, FB)
    return p


def _rope_tables(positions):
    inv_freq = 1.0 / (ROPE_THETA ** (jnp.arange(0, ROPE, 2, dtype=F32) / ROPE))
    ang = positions.astype(F32)[:, None] * inv_freq
    cos, sin = jnp.cos(ang), jnp.sin(ang)
    s = positions.shape[0]
    z = lambda n: jnp.zeros((s, n), F32)
    tc = jnp.concatenate([jnp.ones((s, 64), F32), cos, cos, z(32)], axis=1)
    ta = jnp.concatenate([z(64), -sin, z(48)], axis=1)
    tb = jnp.concatenate([z(80), sin, z(32)], axis=1)
    return tc, ta, tb


def _layer_fwd(xv, p, tabs):
    h = _rms(xv, p["norm_mix_pre"], BF16, "rms_pre")
    proj = _mm_rows("in_proj", h, p["win"], BF16, NN)
    q, k, kv = _mla_prep(proj, tabs, p["mla_q_norm"], p["mla_kv_norm"], p["wq"], p["wkv"])
    o, lse = _flash_fwd(q, k, kv)
    yconv = _sconv_fwd(proj, p["scw"])
    yssd, ypre, states = _ssd_fwd(proj, p["ssdcw"], p["ssd_conv_b"], p["dtb"], p["alog"], p["dvec"], p["ssd_norm"])
    cat = jnp.concatenate([o, yconv, yssd], axis=1)
    mixed = _mm_rows("out_proj", cat, p["wout"], F32, NN)
    x1 = _add_rms(xv, mixed, p["norm_mix_post"], "add_rms")
    h2 = _rms(x1, p["norm_ffn_pre"], BF16, "rms_pre")
    upre = _mm_up(h2, p["wup"])
    act = _ffn_act(upre, p["fcw"], p["fcb"])
    f = _mm_down(act, p["wdown"])
    x2 = _add_rms(x1, f, p["norm_ffn_post"], "add_rms")
    saved = dict(x=xv, h=h, proj=proj, q=q, k=k, kv=kv, lse=lse, ypre=ypre, states=states, cat=cat, mixed=mixed, x1=x1, h2=h2,
                 upre=upre, act=act, f=f)
    return x2, saved


def _layer_bwd(dx2, sv, p, tabs):
    df, g_nfpo = _rms_bwd(sv["f"], p["norm_ffn_post"], dx2, None, BF16, "rms_bwd_post")
    dact = _mm_dact(df, p["wdown"])
    g_wdown = _mm_dwdown(sv["act"], df)
    du, g_fcb = _ffn_bwd_a(sv["upre"], dact, p["fcw"], p["fcb"])
    dupre, g_fcw = _ffn_bwd_b(du, sv["upre"], p["fcw"])
    dh2 = _mm_dh2(dupre, p["wup"])
    g_wup = _mm_dwup(sv["h2"], dupre)
    dx1, g_nfp = _rms_bwd(sv["x1"], p["norm_ffn_pre"], dh2, dx2, F32, "rms_bwd_pre")
    dmixed, g_nmpo = _rms_bwd(sv["mixed"], p["norm_mix_post"], dx1, None, BF16, "rms_bwd_post")
    dcat = _mm_rows("dcat", dmixed, p["wout"], BF16, NT)
    g_wout = _mm_wgrad("dw_out", sv["cat"], dmixed, BF16)
    dq, dk, dv = _flash_bwd(sv["q"], sv["k"], sv["kv"], sv["cat"], dcat, sv["lse"])
    dcq, dckv, dkr, g_wq, g_wkv, g_qn, g_kvn = _mla_prep_bwd(sv["proj"], tabs, p["mla_q_norm"], p["mla_kv_norm"], p["wq"], p["wkv"], dq, dk, dv)
    dscb, dscc, dsch, g_scw = _sconv_bwd(sv["proj"], dcat, p["scw"])
    dz, dxbc, ddt, g_cw, g_cb, g_dtb, g_alog, g_d, g_nw = _ssd_bwd(
        sv["proj"], dcat, sv["ypre"], sv["states"], p["ssdcw"], p["ssd_conv_b"], p["dtb"], p["alog"], p["dvec"], p["ssd_norm"])
    dproj = jnp.concatenate([dcq, dckv, dkr, dscb, dscc, dsch, dz, dxbc, ddt], axis=1)
    dh = _mm_rows("dh", dproj, p["win"], BF16, NT)
    g_win = _mm_wgrad("dw_in", sv["h"], dproj, BF16)
    dx, g_nmp = _rms_bwd(sv["x"], p["norm_mix_pre"], dh, dx1, F32, "rms_bwd_pre")
    big = {
        "w_in": _unpad_win(g_win).reshape(NDEV, 128, 2212),
        "mla_w_q_up": g_wq[:, :, :96].astype(BF16),
        "mla_w_kv_up": g_wkv.astype(BF16),
        "sc_conv_w": g_scw.reshape(3, NDEV, 32).transpose(1, 0, 2).astype(BF16),
        "ssd_conv_w": g_cw.reshape(4, NDEV, 96).transpose(1, 0, 2).astype(BF16),
        "w_out": _unpad_wout(g_wout).reshape(NDEV, 128, D),
        "ffn_w_up": g_wup,
        "ffn_conv_w": g_fcw.astype(BF16),
        "ffn_w_down": g_wdown.reshape(NDEV, 352, D),
    }
    small = {
        "norm_mix_pre": g_nmp[0], "norm_mix_post": g_nmpo[0], "norm_ffn_pre": g_nfp[0], "norm_ffn_post": g_nfpo[0],
        "mla_q_norm": g_qn[0], "mla_kv_norm": g_kvn[0], "ssd_conv_b": g_cb[0], "ssd_dt_bias": g_dtb[0, :SSD_H],
        "ssd_a_log": g_alog[0, :SSD_H], "ssd_d": g_d[0, :SSD_H], "ssd_norm": g_nw[0], "ffn_conv_b": g_fcb.reshape(-1),
    }
    return dx, big, small


def _local_step(xv, positions, target, layers):
    tabs = _rope_tables(positions)
    saved = []
    for p in layers:
        xv, sv = _layer_fwd(xv, p, tabs)
        saved.append(sv)
    loss, dx = _loss_head(xv, target)
    bigs, smalls = [None] * DEPTH, [None] * DEPTH
    for l in reversed(range(len(layers))):
        dx, bigs[l], smalls[l] = _layer_bwd(dx, saved[l], layers[l], tabs)
    return loss[0, 0], dx, bigs, smalls


def _pack_rows(flat, lead, width, mult):
    n = flat.shape[-1]
    rows = -(-n // (width * mult)) * mult
    pad = [(0, 0)] * (flat.ndim - 1) + [(0, rows * width - n)]
    return jnp.pad(flat, pad).reshape(lead + (rows, width))


def _rows2(n):
    shape = SHAPES[n]
    return (math.prod(shape[:-1]), shape[-1])


def _group_pack(group, get, lead):
    width, names = group
    pieces = []
    for n in names:
        rows, cols = _rows2(n)
        pad = [(0, 0)] * len(lead) + [(0, -rows % 16), (0, width - cols)]
        pieces.append(jnp.pad(get(n), pad))
    return pieces[0] if len(pieces) == 1 else jnp.concatenate(pieces, axis=len(lead))


def _group_unpack(group, buf):
    _, names = group
    res, off = {}, 0
    for n in names:
        rows, cols = _rows2(n)
        res[n] = buf[:, off:off + rows, :cols]
        off += rows + (-rows % 16)
    return res


def kernel(x, positions, norm_mix_pre, norm_mix_post, norm_ffn_pre, norm_ffn_post, w_in, mla_q_norm, mla_w_q_up, mla_kv_norm, mla_w_kv_up, sc_conv_w, ssd_conv_w, ssd_conv_b, ssd_dt_bias, ssd_a_log, ssd_d, ssd_norm, w_out, ffn_w_up, ffn_conv_w, ffn_conv_b, ffn_w_down, loss_target, m_norm_mix_pre, m_norm_mix_post, m_norm_ffn_pre, m_norm_ffn_post, m_w_in, m_mla_q_norm, m_mla_w_q_up, m_mla_kv_norm, m_mla_w_kv_up, m_sc_conv_w, m_ssd_conv_w, m_ssd_conv_b, m_ssd_dt_bias, m_ssd_a_log, m_ssd_d, m_ssd_norm, m_w_out, m_ffn_w_up, m_ffn_conv_w, m_ffn_conv_b, m_ffn_w_down, v_norm_mix_pre, v_norm_mix_post, v_norm_ffn_pre, v_norm_ffn_post, v_w_in, v_mla_q_norm, v_mla_w_q_up, v_mla_kv_norm, v_mla_w_kv_up, v_sc_conv_w, v_ssd_conv_w, v_ssd_conv_b, v_ssd_dt_bias, v_ssd_a_log, v_ssd_d, v_ssd_norm, v_w_out, v_ffn_w_up, v_ffn_conv_w, v_ffn_conv_b, v_ffn_w_down):
    given = dict(locals())
    w = {n: given[n] for n in WEIGHTS}
    m = {n: given["m_" + n] for n in WEIGHTS}
    v = {n: given["v_" + n] for n in WEIGHTS}

    gathered = _all_gather([_group_pack(g, lambda n: w[n].astype(BF16).reshape(_rows2(n)), ()) for g in GROUPS], "gather_weights")
    big = {}
    for g, buf in zip(GROUPS, gathered):
        for n, piece in _group_unpack(g, buf).items():
            big[n] = piece.reshape((NDEV,) + SHAPES[n])
    small_w = {n: w[n] for n, _ in SMALL}
    layers = [_prep_layer(big, small_w, l) for l in range(DEPTH)]

    loss, dx, bigs, smalls = _local_step(x[0], positions[0], loss_target[0], layers)
    loss = lax.psum(loss, ("x", "y", "c"))

    stacked = lambda n: jnp.stack([bigs[l][n] for l in range(DEPTH)], axis=1).reshape((NDEV,) + _rows2(n))
    recv = _all_to_all([_group_pack(g, stacked, (NDEV,)) for g in GROUPS], "exchange_grads")
    out = {}
    for g, buf in zip(GROUPS, recv):
        for n, parts in _group_unpack(g, buf).items():
            r2 = _rows2(n)
            res = _adamw(parts, w[n].reshape(r2), m[n].reshape(r2), v[n].reshape(r2), "adamw_" + n)
            out[n] = [a.reshape(SHAPES[n]) for a in res]

    sflat = jnp.concatenate([jnp.stack([smalls[l][n] for l in range(DEPTH)]).reshape(-1) for n, _ in SMALL])
    sparts = _all_gather([_pack_rows(sflat, (), LANE, 8)], "gather_small_grads")[0]
    pk = lambda d: _pack_rows(jnp.concatenate([d[n].reshape(-1) for n, _ in SMALL]), (), LANE, 8)
    res = _adamw(sparts, pk(w), pk(m), pk(v), "adamw_small")
    off = 0
    for n, width in SMALL:
        out[n] = [a.reshape(-1)[off:off + DEPTH * width].reshape(DEPTH, width) for a in res]
        off += DEPTH * width

    return (loss, dx[None], *[out[n][0] for n in WEIGHTS], *[out[n][1] for n in WEIGHTS],
            *[out[n][2] for n in WEIGHTS], *[out[n][3] for n in WEIGHTS])
```

```python
import functools
import math

import jax
import jax.numpy as jnp
from jax import lax
from jax.experimental import pallas as pl
from jax.experimental.pallas import tpu as pltpu

F32 = jnp.float32
BF16 = jnp.bfloat16

D = 1024
DEPTH = 4
NDEV = 8
HEADS = 8
QL = 256
KVL = 128
ROPE = 32
NOPE = 64
SC = 256
SSD_DIM = 256
SSD_CONV = 768
SSD_H = 4
SSD_L = 128
FFN = 2816
FB = 704
EPS = 1e-6
ROPE_THETA = 10000.0
ATT_SCALE = 96 ** -0.5
LR, B1, B2, AEPS, WD, STEP = 0.001, 0.9, 0.999, 1e-08, 0.01, 10

PW = 2432
CATW = 1536

ROW_TILE = 512
ATT_TILE = 256
FWD_HEADS = 4
BWD_HEADS = 2
HALO = 16
LANE = 128
NEG = -1e30
HI = lax.Precision.HIGHEST
NN = (((1,), (0,)), ((), ()))
NT = (((1,), (1,)), ((), ()))
TN = (((0,), (0,)), ((), ()))
VMEM_LIMIT = 56 * 1024 * 1024

SHARDED = (
    ("w_in", (4, 128, 2212)),
    ("mla_w_q_up", (4, 256, 96)),
    ("mla_w_kv_up", (4, 128, 128)),
    ("sc_conv_w", (4, 3, 32)),
    ("ssd_conv_w", (4, 4, 96)),
    ("w_out", (4, 128, 1024)),
    ("ffn_w_up", (4, 1024, 704)),
    ("ffn_conv_w", (4, 3, 704)),
    ("ffn_w_down", (4, 352, 1024)),
)
SHAPES = dict(SHARDED)
GROUPS = (
    (2212, ("w_in",)),
    (1024, ("w_out", "ffn_w_down")),
    (704, ("ffn_w_up", "ffn_conv_w")),
    (96, ("mla_w_q_up", "ssd_conv_w", "sc_conv_w")),
    (128, ("mla_w_kv_up",)),
)
SMALL = (
    ("norm_mix_pre", 1024), ("norm_mix_post", 1024), ("norm_ffn_pre", 1024), ("norm_ffn_post", 1024),
    ("mla_q_norm", 256), ("mla_kv_norm", 128), ("ssd_conv_b", 768), ("ssd_dt_bias", 4), ("ssd_a_log", 4),
    ("ssd_d", 4), ("ssd_norm", 256), ("ffn_conv_b", 5632),
)
WEIGHTS = ("norm_mix_pre", "norm_mix_post", "norm_ffn_pre", "norm_ffn_post", "w_in", "mla_q_norm", "mla_w_q_up",
           "mla_kv_norm", "mla_w_kv_up", "sc_conv_w", "ssd_conv_w", "ssd_conv_b", "ssd_dt_bias", "ssd_a_log", "ssd_d",
           "ssd_norm", "w_out", "ffn_w_up", "ffn_conv_w", "ffn_conv_b", "ffn_w_down")


def _dot(a, b, dims=NN, precision=None):
    return lax.dot_general(a, b, dims, precision=precision, preferred_element_type=F32)


def _sig(v):
    return 1.0 / (1.0 + jnp.exp(-v))


def _cp(*sem):
    return pltpu.CompilerParams(dimension_semantics=sem, vmem_limit_bytes=VMEM_LIMIT)


def _rowsum(v):
    return jnp.sum(v, axis=0, keepdims=True)


def _prev_halo(i, ts):
    return jnp.maximum(i * (ts // HALO) - 1, 0)


def _next_halo(i, ts, n):
    return jnp.minimum((i + 1) * (ts // HALO), n * (ts // HALO) - 1)


def _all_gather(xs, name):
    n = len(xs)

    def body(*refs):
        x_refs, out_refs = refs[:n], refs[n:2 * n]
        send_sems, recv_sems, local_sems = refs[2 * n:]
        x, y, cc = lax.axis_index("x"), lax.axis_index("y"), lax.axis_index("c")
        me, sibling = (x, y, cc), (x, y, 1 - cc)
        chips = [(1 - x, y), (x, 1 - y), (1 - x, 1 - y)]

        def rows(t, px, py, pc):
            return out_refs[t].at[4 * px + 2 * py + pc]

        def copy(t, k, block, to, src=None):
            return pltpu.make_async_remote_copy(
                src_ref=rows(t, *block) if src is None else src, dst_ref=rows(t, *block),
                send_sem=send_sems.at[7 * t + k], recv_sem=recv_sems.at[7 * t + k], device_id=to, device_id_type=pl.DeviceIdType.MESH)

        mine = [pltpu.make_async_copy(x_refs[t], rows(t, *me), local_sems.at[t]) for t in range(n)]
        for cp in mine:
            cp.start()
        first = []
        for t in range(n):
            first.append(copy(t, 0, me, sibling, src=x_refs[t]))
            first += [copy(t, 1 + j, me, (*chip, cc), src=x_refs[t]) for j, chip in enumerate(chips)]
        for cp in first:
            cp.start()
        passed = []
        for j, chip in enumerate(chips):
            for t in range(n):
                copy(t, 1 + j, (*chip, cc), me).wait_recv()
                passed.append(copy(t, 4 + j, (*chip, cc), sibling))
                passed[-1].start()
        for t in range(n):
            copy(t, 0, sibling, me).wait_recv()
            for j, chip in enumerate(chips):
                copy(t, 4 + j, (*chip, 1 - cc), me).wait_recv()
        for cp in first + passed:
            cp.wait_send()
        for cp in mine:
            cp.wait()

    anyspec = pl.BlockSpec(memory_space=pl.ANY)
    return pl.pallas_call(
        body, name=name, out_shape=[jax.ShapeDtypeStruct((NDEV,) + a.shape, a.dtype) for a in xs],
        in_specs=[anyspec] * n, out_specs=[anyspec] * n,
        scratch_shapes=[pltpu.SemaphoreType.DMA((7 * n,)), pltpu.SemaphoreType.DMA((7 * n,)), pltpu.SemaphoreType.DMA((n,))],
    )(*xs)


def _all_to_all(xs, name):
    n = len(xs)

    def body(*refs):
        x_refs, out_refs = refs[:n], refs[n:2 * n]
        send_sems, recv_sems, local_sems = refs[2 * n:]
        x, y, cc = lax.axis_index("x"), lax.axis_index("y"), lax.axis_index("c")
        me = 4 * x + 2 * y + cc
        mine = [pltpu.make_async_copy(x_refs[t].at[me], out_refs[t].at[me], local_sems.at[t]) for t in range(n)]
        for cp in mine:
            cp.start()
        copies = []
        for k in range(1, NDEV):
            px = 1 - x if k & 4 else x
            py = 1 - y if k & 2 else y
            pc = 1 - cc if k & 1 else cc
            peer = 4 * px + 2 * py + pc
            for t in range(n):
                copies.append(pltpu.make_async_remote_copy(
                    src_ref=x_refs[t].at[peer], dst_ref=out_refs[t].at[me], send_sem=send_sems.at[7 * t + k - 1],
                    recv_sem=recv_sems.at[7 * t + k - 1], device_id=(px, py, pc), device_id_type=pl.DeviceIdType.MESH))
        for cp in copies:
            cp.start()
        for cp in copies:
            cp.wait()
        for cp in mine:
            cp.wait()

    anyspec = pl.BlockSpec(memory_space=pl.ANY)
    return pl.pallas_call(
        body, name=name, out_shape=[jax.ShapeDtypeStruct(a.shape, a.dtype) for a in xs],
        in_specs=[anyspec] * n, out_specs=[anyspec] * n,
        scratch_shapes=[pltpu.SemaphoreType.DMA((7 * n,)), pltpu.SemaphoreType.DMA((7 * n,)), pltpu.SemaphoreType.DMA((n,))],
    )(*xs)


def _mm(name, a, b, out_shape, grid, a_spec, b_spec, o_spec, dims, acc_shape):
    nk = grid[2]

    def body(a_ref, b_ref, o_ref, acc_ref):
        k = pl.program_id(2)

        @pl.when(k == 0)
        def _():
            acc_ref[...] = jnp.zeros_like(acc_ref)

        acc_ref[...] += _dot(a_ref[...], b_ref[...], dims)

        @pl.when(k == nk - 1)
        def _():
            o_ref[...] = acc_ref[...].astype(o_ref.dtype)

    return pl.pallas_call(
        body, name=name, grid=grid, out_shape=out_shape, in_specs=[a_spec, b_spec], out_specs=o_spec,
        scratch_shapes=[pltpu.VMEM(acc_shape, F32)], compiler_params=_cp("parallel", "parallel", "arbitrary"),
    )(a, b)


def _mm_rows(name, a, w, out_dtype, dims):
    s, k = a.shape
    n = w.shape[1] if dims == NN else w.shape[0]
    tm = min(ROW_TILE, s)
    return _mm(name, a, w, jax.ShapeDtypeStruct((s, n), out_dtype), (s // tm, 1, 1),
               pl.BlockSpec((tm, k), lambda i, j, kk: (i, 0)), pl.BlockSpec(w.shape, lambda i, j, kk: (0, 0)),
               pl.BlockSpec((tm, n), lambda i, j, kk: (i, 0)), dims, (tm, n))


def _mm_wgrad(name, a, g, out_dtype):
    s, m = a.shape
    n = g.shape[1]
    tk = min(ROW_TILE, s)
    return _mm(name, a, g, jax.ShapeDtypeStruct((m, n), out_dtype), (1, 1, s // tk),
               pl.BlockSpec((tk, m), lambda i, j, kk: (kk, 0)), pl.BlockSpec((tk, n), lambda i, j, kk: (kk, 0)),
               pl.BlockSpec((m, n), lambda i, j, kk: (0, 0)), TN, (m, n))


def _mm_up(h2, wup):
    s = h2.shape[0]
    tm = min(ROW_TILE, s)
    return _mm("ffn_up", h2, wup, jax.ShapeDtypeStruct((NDEV, s, FB), BF16), (NDEV, s // tm, 1),
               pl.BlockSpec((tm, D), lambda j, i, kk: (i, 0)), pl.BlockSpec((None, D, FB), lambda j, i, kk: (j, 0, 0)),
               pl.BlockSpec((None, tm, FB), lambda j, i, kk: (j, i, 0)), NN, (tm, FB))


def _mm_down(act, wdown):
    s = act.shape[1]
    tm = min(ROW_TILE, s)
    return _mm("ffn_down", act, wdown, jax.ShapeDtypeStruct((s, D), F32), (s // tm, 1, 4),
               pl.BlockSpec((None, tm, FB), lambda i, j, kk: (kk, i, 0)), pl.BlockSpec((None, FB, D), lambda i, j, kk: (kk, 0, 0)),
               pl.BlockSpec((tm, D), lambda i, j, kk: (i, 0)), NN, (tm, D))


def _mm_dact(df, wdown):
    s = df.shape[0]
    tm = min(ROW_TILE, s)
    return _mm("ffn_dact", df, wdown, jax.ShapeDtypeStruct((4, s, FB), BF16), (4, s // tm, 1),
               pl.BlockSpec((tm, D), lambda j, i, kk: (i, 0)), pl.BlockSpec((None, FB, D), lambda j, i, kk: (j, 0, 0)),
               pl.BlockSpec((None, tm, FB), lambda j, i, kk: (j, i, 0)), NT, (tm, FB))


def _mm_dwdown(act, df):
    s = df.shape[0]
    tk = min(ROW_TILE, s)
    return _mm("ffn_dwdown", act, df, jax.ShapeDtypeStruct((4, FB, D), BF16), (4, 1, s // tk),
               pl.BlockSpec((None, tk, FB), lambda j, i, kk: (j, kk, 0)), pl.BlockSpec((tk, D), lambda j, i, kk: (kk, 0)),
               pl.BlockSpec((None, FB, D), lambda j, i, kk: (j, 0, 0)), TN, (FB, D))


def _mm_dh2(dupre, wup):
    s = dupre.shape[1]
    tm = min(ROW_TILE, s)
    return _mm("ffn_dh2", dupre, wup, jax.ShapeDtypeStruct((s, D), BF16), (s // tm, 1, NDEV),
               pl.BlockSpec((None, tm, FB), lambda i, j, kk: (kk, i, 0)), pl.BlockSpec((None, D, FB), lambda i, j, kk: (kk, 0, 0)),
               pl.BlockSpec((tm, D), lambda i, j, kk: (i, 0)), NT, (tm, D))


def _mm_dwup(h2, dupre):
    s = h2.shape[0]
    tk = min(ROW_TILE, s)
    return _mm("ffn_dwup", h2, dupre, jax.ShapeDtypeStruct((NDEV, D, FB), BF16), (NDEV, 1, s // tk),
               pl.BlockSpec((tk, D), lambda j, i, kk: (kk, 0)), pl.BlockSpec((None, tk, FB), lambda j, i, kk: (j, kk, 0)),
               pl.BlockSpec((None, D, FB), lambda j, i, kk: (j, 0, 0)), TN, (D, FB))


def _rms(xv, w, out_dtype, name):
    s, d = xv.shape
    ts = min(ROW_TILE, s)

    def body(x_ref, w_ref, o_ref):
        xf = x_ref[...].astype(F32)
        r = lax.rsqrt(jnp.mean(xf * xf, axis=-1, keepdims=True) + EPS)
        o_ref[...] = (xf * r * w_ref[...]).astype(o_ref.dtype)

    return pl.pallas_call(
        body, name=name, grid=(s // ts,), out_shape=jax.ShapeDtypeStruct((s, d), out_dtype),
        in_specs=[pl.BlockSpec((ts, d), lambda i: (i, 0)), pl.BlockSpec((1, d), lambda i: (0, 0))],
        out_specs=pl.BlockSpec((ts, d), lambda i: (i, 0)), compiler_params=_cp("parallel"),
    )(xv, w)


def _add_rms(xv, mv, w, name):
    s, d = xv.shape
    ts = min(ROW_TILE, s)

    def body(x_ref, m_ref, w_ref, o_ref):
        mf = m_ref[...].astype(F32)
        r = lax.rsqrt(jnp.mean(mf * mf, axis=-1, keepdims=True) + EPS)
        o_ref[...] = x_ref[...] + mf * r * w_ref[...]

    return pl.pallas_call(
        body, name=name, grid=(s // ts,), out_shape=jax.ShapeDtypeStruct((s, d), F32),
        in_specs=[pl.BlockSpec((ts, d), lambda i: (i, 0)), pl.BlockSpec((ts, d), lambda i: (i, 0)), pl.BlockSpec((1, d), lambda i: (0, 0))],
        out_specs=pl.BlockSpec((ts, d), lambda i: (i, 0)), compiler_params=_cp("parallel"),
    )(xv, mv, w)


def _rms_bwd_math(xf, w, dy):
    r = lax.rsqrt(jnp.mean(xf * xf, axis=-1, keepdims=True) + EPS)
    xh = xf * r
    dxh = dy * w
    dx = r * (dxh - xh * jnp.mean(dxh * xh, axis=-1, keepdims=True))
    return dx, dy * xh


def _rms_bwd(xv, w, dy, dres, out_dtype, name):
    s, d = xv.shape
    ts = min(ROW_TILE, s)
    with_res = dres is not None

    def body(*refs):
        if with_res:
            x_ref, w_ref, dy_ref, dres_ref, dx_ref, dw_ref = refs
        else:
            x_ref, w_ref, dy_ref, dx_ref, dw_ref = refs
        dx, dwt = _rms_bwd_math(x_ref[...].astype(F32), w_ref[...], dy_ref[...].astype(F32))
        if with_res:
            dx = dx + dres_ref[...]
        dx_ref[...] = dx.astype(dx_ref.dtype)

        @pl.when(pl.program_id(0) == 0)
        def _():
            dw_ref[...] = jnp.zeros_like(dw_ref)

        dw_ref[...] += _rowsum(dwt)

    row = pl.BlockSpec((ts, d), lambda i: (i, 0))
    vec = pl.BlockSpec((1, d), lambda i: (0, 0))
    ins = [xv, w, dy] + ([dres] if with_res else [])
    return pl.pallas_call(
        body, name=name, grid=(s // ts,),
        out_shape=(jax.ShapeDtypeStruct((s, d), out_dtype), jax.ShapeDtypeStruct((1, d), F32)),
        in_specs=[row, vec, row] + ([row] if with_res else []), out_specs=(row, vec), compiler_params=_cp("arbitrary"),
    )(*ins)


def _loss_head(yv, tv):
    s, d = yv.shape
    ts = min(ROW_TILE, s)

    def body(y_ref, t_ref, l_ref, dy_ref):
        e = y_ref[...] - t_ref[...]
        dy_ref[...] = e * (1.0 / d)

        @pl.when(pl.program_id(0) == 0)
        def _():
            l_ref[...] = jnp.zeros_like(l_ref)

        tot = jnp.sum(jnp.sum(e * e, axis=1, keepdims=True), axis=0, keepdims=True)
        l_ref[...] += jnp.broadcast_to(tot * (0.5 / d), (8, LANE))

    row = pl.BlockSpec((ts, d), lambda i: (i, 0))
    return pl.pallas_call(
        body, name="loss_head", grid=(s // ts,),
        out_shape=(jax.ShapeDtypeStruct((8, LANE), F32), jax.ShapeDtypeStruct((s, d), F32)),
        in_specs=[row, row], out_specs=(pl.BlockSpec((8, LANE), lambda i: (0, 0)), row), compiler_params=_cp("arbitrary"),
    )(yv, tv)


def _rope(v, c, a, b):
    return v * c + pltpu.roll(v, LANE - 16, 1) * a + pltpu.roll(v, 16, 1) * b


def _rope_t(dv, c, a, b):
    return dv * c + pltpu.roll(dv * a, 16, 1) + pltpu.roll(dv * b, LANE - 16, 1)


def _mla_prep(proj, tabs, qnw, kvnw, wq, wkv):
    s = proj.shape[0]
    ts = min(ROW_TILE, s)
    tc, ta, tb = tabs

    def body(cq_ref, ckv_ref, kr_ref, c_ref, a_ref, b_ref, qnw_ref, kvnw_ref, wq_ref, wkv_ref, q_ref, k_ref, kv_ref):
        c, a, b = c_ref[...], a_ref[...], b_ref[...]
        cq = cq_ref[...].astype(F32)
        qn = (cq * lax.rsqrt(jnp.mean(cq * cq, axis=-1, keepdims=True) + EPS) * qnw_ref[...]).astype(BF16)
        ckv = ckv_ref[...].astype(F32)
        kvn = (ckv * lax.rsqrt(jnp.mean(ckv * ckv, axis=-1, keepdims=True) + EPS) * kvnw_ref[...]).astype(BF16)
        kr = _rope(kr_ref[...].astype(F32), c, a, b)
        lane = lax.broadcasted_iota(jnp.int32, (ts, LANE), 1)
        for h in range(HEADS):
            q_ref[h] = _rope(_dot(qn, wq_ref[h]), c, a, b).astype(BF16)
            kv = _dot(kvn, wkv_ref[h])
            kv_ref[h] = kv.astype(BF16)
            k_ref[h] = jnp.where(lane < NOPE, kv, kr).astype(BF16)

    tab = pl.BlockSpec((ts, LANE), lambda i: (i, 0))
    hd = pl.BlockSpec((HEADS, ts, LANE), lambda i: (0, i, 0))
    out = jax.ShapeDtypeStruct((HEADS, s, LANE), BF16)
    return pl.pallas_call(
        body, name="mla_prep", grid=(s // ts,), out_shape=(out, out, out),
        in_specs=[pl.BlockSpec((ts, QL), lambda i: (i, 0)), pl.BlockSpec((ts, LANE), lambda i: (i, 2)),
                  pl.BlockSpec((ts, LANE), lambda i: (i, 3)), tab, tab, tab,
                  pl.BlockSpec((1, QL), lambda i: (0, 0)), pl.BlockSpec((1, KVL), lambda i: (0, 0)),
                  pl.BlockSpec((HEADS, QL, LANE), lambda i: (0, 0, 0)), pl.BlockSpec((HEADS, KVL, LANE), lambda i: (0, 0, 0))],
        out_specs=(hd, hd, hd), compiler_params=_cp("parallel"),
    )(proj, proj, proj, tc, ta, tb, qnw, kvnw, wq, wkv)


def _mla_prep_bwd(proj, tabs, qnw, kvnw, wq, wkv, dq, dk, dv):
    s = proj.shape[0]
    ts = min(ROW_TILE, s)
    tc, ta, tb = tabs

    def body(cq_ref, ckv_ref, c_ref, a_ref, b_ref, qnw_ref, kvnw_ref, wq_ref, wkv_ref, dq_ref, dk_ref, dv_ref,
             dcq_ref, dckv_ref, dkr_ref, dwq_ref, dwkv_ref, dqnw_ref, dkvnw_ref):
        @pl.when(pl.program_id(0) == 0)
        def _():
            dwq_ref[...] = jnp.zeros_like(dwq_ref)
            dwkv_ref[...] = jnp.zeros_like(dwkv_ref)
            dqnw_ref[...] = jnp.zeros_like(dqnw_ref)
            dkvnw_ref[...] = jnp.zeros_like(dkvnw_ref)

        c, a, b = c_ref[...], a_ref[...], b_ref[...]
        cq = cq_ref[...].astype(F32)
        qn = (cq * lax.rsqrt(jnp.mean(cq * cq, axis=-1, keepdims=True) + EPS) * qnw_ref[...]).astype(BF16)
        ckv = ckv_ref[...].astype(F32)
        kvn = (ckv * lax.rsqrt(jnp.mean(ckv * ckv, axis=-1, keepdims=True) + EPS) * kvnw_ref[...]).astype(BF16)
        lane = lax.broadcasted_iota(jnp.int32, (ts, LANE), 1)
        dqn = jnp.zeros((ts, QL), F32)
        dkvn = jnp.zeros((ts, KVL), F32)
        dkr = jnp.zeros((ts, LANE), F32)
        for h in range(HEADS):
            dqh = _rope_t(dq_ref[h], c, a, b).astype(BF16)
            dwq_ref[h] += _dot(qn, dqh, TN)
            dqn += _dot(dqh, wq_ref[h], NT)
            dkh = dk_ref[h].astype(F32)
            dkvh = jnp.where(lane < NOPE, dkh, dv_ref[h].astype(F32)).astype(BF16)
            dkr += jnp.where(lane < NOPE, 0.0, dkh)
            dwkv_ref[h] += _dot(kvn, dkvh, TN)
            dkvn += _dot(dkvh, wkv_ref[h], NT)
        dkr_ref[...] = _rope_t(dkr, c, a, b).astype(BF16)
        dcq, dwt = _rms_bwd_math(cq, qnw_ref[...], dqn)
        dcq_ref[...] = dcq.astype(BF16)
        dqnw_ref[...] += _rowsum(dwt)
        dckv, dwt = _rms_bwd_math(ckv, kvnw_ref[...], dkvn)
        dckv_ref[...] = dckv.astype(BF16)
        dkvnw_ref[...] += _rowsum(dwt)

    tab = pl.BlockSpec((ts, LANE), lambda i: (i, 0))
    hd = pl.BlockSpec((HEADS, ts, LANE), lambda i: (0, i, 0))
    wq_spec = pl.BlockSpec((HEADS, QL, LANE), lambda i: (0, 0, 0))
    wkv_spec = pl.BlockSpec((HEADS, KVL, LANE), lambda i: (0, 0, 0))
    return pl.pallas_call(
        body, name="mla_prep_bwd", grid=(s // ts,),
        out_shape=(jax.ShapeDtypeStruct((s, QL), BF16), jax.ShapeDtypeStruct((s, KVL), BF16), jax.ShapeDtypeStruct((s, LANE), BF16),
                   jax.ShapeDtypeStruct((HEADS, QL, LANE), F32), jax.ShapeDtypeStruct((HEADS, KVL, LANE), F32),
                   jax.ShapeDtypeStruct((1, QL), F32), jax.ShapeDtypeStruct((1, KVL), F32)),
        in_specs=[pl.BlockSpec((ts, QL), lambda i: (i, 0)), pl.BlockSpec((ts, LANE), lambda i: (i, 2)), tab, tab, tab,
                  pl.BlockSpec((1, QL), lambda i: (0, 0)), pl.BlockSpec((1, KVL), lambda i: (0, 0)), wq_spec, wkv_spec, hd, hd, hd],
        out_specs=(pl.BlockSpec((ts, QL), lambda i: (i, 0)), pl.BlockSpec((ts, KVL), lambda i: (i, 0)), tab, wq_spec, wkv_spec,
                   pl.BlockSpec((1, QL), lambda i: (0, 0)), pl.BlockSpec((1, KVL), lambda i: (0, 0))),
        compiler_params=_cp("arbitrary"),
    )(proj, proj, tc, ta, tb, qnw, kvnw, wq, wkv, dq, dk, dv)


def _transpose_bf16(v):
    return v.astype(F32).T.astype(BF16)


def _flash_fwd(q, k, kv):
    s = q.shape[1]
    t = min(ATT_TILE, s)
    n = s // t

    g = FWD_HEADS

    def body(q_ref, k_ref, kv_ref, o_ref, lse_ref, kvt_sc):
        i = pl.program_id(1)

        @pl.when(i == 0)
        def _():
            for hh in range(g):
                kvt_sc[hh] = _transpose_bf16(kv_ref[hh])

        qt = [_transpose_bf16(q_ref[hh]) for hh in range(g)]
        kpos = lax.broadcasted_iota(jnp.int32, (t, t), 0)
        qpos = lax.broadcasted_iota(jnp.int32, (t, t), 1)

        def chunk(j, carry, diagonal):
            start = pl.multiple_of(j * t, t)
            scs = [_dot(k_ref[hh, pl.ds(start, t), :], qt[hh]) for hh in range(g)]
            soft = []
            for hh in range(g):
                m, l, _ = carry[hh]
                sc = scs[hh] * ATT_SCALE
                if diagonal:
                    sc = jnp.where(qpos >= kpos, sc, NEG)
                m_new = jnp.maximum(m, jnp.max(sc, axis=0, keepdims=True))
                alpha = jnp.exp(m - m_new)
                p = jnp.exp(sc - m_new)
                soft.append((m_new, alpha * l + jnp.sum(p, axis=0, keepdims=True), alpha, p.astype(BF16)))
            pvs = [_dot(kvt_sc[hh, :, pl.ds(start, t)], soft[hh][3]) for hh in range(g)]
            return tuple((soft[hh][0], soft[hh][1], soft[hh][2] * carry[hh][2] + pvs[hh]) for hh in range(g))

        init = tuple((jnp.full((1, t), NEG, F32), jnp.zeros((1, t), F32), jnp.zeros((LANE, t), F32)) for _ in range(g))
        carry = lax.fori_loop(0, i, lambda j, c: chunk(j, c, False), init)
        carry = chunk(i, carry, True)
        for hh in range(g):
            m, l, acc = carry[hh]
            o_ref[:, hh * LANE:(hh + 1) * LANE] = (acc / l).T.astype(BF16)
            lse_ref[hh] = m + jnp.log(l)

    whole = pl.BlockSpec((g, s, LANE), lambda h, i: (h, 0, 0))
    return pl.pallas_call(
        body, name="flash_fwd", grid=(HEADS // g, n),
        out_shape=(jax.ShapeDtypeStruct((s, HEADS * LANE), BF16), jax.ShapeDtypeStruct((HEADS, 1, s), F32)),
        in_specs=[pl.BlockSpec((g, t, LANE), lambda h, i: (h, i, 0)), whole, whole],
        out_specs=(pl.BlockSpec((t, g * LANE), lambda h, i: (i, h)), pl.BlockSpec((g, 1, t), lambda h, i: (h, 0, i))),
        scratch_shapes=[pltpu.VMEM((g, LANE, s), BF16)], compiler_params=_cp("arbitrary", "arbitrary"),
    )(q, k, kv)


def _flash_bwd(q, k, kv, cat, dcat, lse):
    s = q.shape[1]
    t = min(ATT_TILE, s)
    n = s // t

    g = BWD_HEADS

    def body(q_ref, k_ref, kv_ref, o_ref, do_ref, lse_ref, dq_ref, dk_ref, dv_ref, qt_sc, dot_sc, delta_sc, dqt_sc):
        j = pl.program_id(1)

        @pl.when(j == 0)
        def _():
            for hh in range(g):
                lanes = slice(hh * LANE, (hh + 1) * LANE)
                qt_sc[hh] = _transpose_bf16(q_ref[hh])
                dof = do_ref[:, lanes].astype(F32)
                dot_sc[hh] = dof.T.astype(BF16)
                delta_sc[hh] = _dot(jnp.ones((8, LANE), F32), dof * o_ref[:, lanes].astype(F32), NT, precision=HI)
            dqt_sc[...] = jnp.zeros_like(dqt_sc)

        kjt = [_transpose_bf16(k_ref[hh]) for hh in range(g)]
        kpos = lax.broadcasted_iota(jnp.int32, (t, t), 0)
        qpos = lax.broadcasted_iota(jnp.int32, (t, t), 1)

        def chunk(i, carry, diagonal):
            start = pl.multiple_of(i * t, t)
            cols = pl.ds(start, t)
            scs = [_dot(k_ref[hh], qt_sc[hh, :, cols]) for hh in range(g)]
            dps = [_dot(kv_ref[hh], dot_sc[hh, :, cols]) for hh in range(g)]
            pds = []
            for hh in range(g):
                p = jnp.exp(scs[hh] * ATT_SCALE - lse_ref[hh, :, cols])
                if diagonal:
                    p = jnp.where(qpos >= kpos, p, 0.0)
                ds = (p * (dps[hh] - delta_sc[hh, 0:1, cols]) * ATT_SCALE).astype(BF16)
                pds.append((p.astype(BF16), ds))
            out = []
            for hh in range(g):
                dk, dv = carry[hh]
                dv = dv + _dot(pds[hh][0], do_ref[pl.ds(start, t), hh * LANE:(hh + 1) * LANE])
                dk = dk + _dot(pds[hh][1], q_ref[hh, pl.ds(start, t), :])
                dqt_sc[hh, :, cols] += _dot(kjt[hh], pds[hh][1])
                out.append((dk, dv))
            return tuple(out)

        zero = jnp.zeros((t, LANE), F32)
        carry = chunk(j, tuple((zero, zero) for _ in range(g)), True)
        carry = lax.fori_loop(j + 1, n, lambda i, c: chunk(i, c, False), carry)
        for hh in range(g):
            dk_ref[hh] = carry[hh][0].astype(BF16)
            dv_ref[hh] = carry[hh][1].astype(BF16)

        @pl.when(j == n - 1)
        def _():
            for hh in range(g):
                dq_ref[hh] = dqt_sc[hh].T

    whole = pl.BlockSpec((g, s, LANE), lambda h, j: (h, 0, 0))
    kspec = pl.BlockSpec((g, t, LANE), lambda h, j: (h, j, 0))
    ospec = pl.BlockSpec((s, g * LANE), lambda h, j: (0, h))
    return pl.pallas_call(
        body, name="flash_bwd", grid=(HEADS // g, n),
        out_shape=(jax.ShapeDtypeStruct((HEADS, s, LANE), F32), jax.ShapeDtypeStruct((HEADS, s, LANE), BF16),
                   jax.ShapeDtypeStruct((HEADS, s, LANE), BF16)),
        in_specs=[whole, kspec, kspec, ospec, ospec, pl.BlockSpec((g, 1, s), lambda h, j: (h, 0, 0))],
        out_specs=(whole, kspec, kspec),
        scratch_shapes=[pltpu.VMEM((g, LANE, s), BF16), pltpu.VMEM((g, LANE, s), BF16), pltpu.VMEM((g, 8, s), F32),
                        pltpu.VMEM((g, LANE, s), F32)],
        compiler_params=_cp("arbitrary", "arbitrary"),
    )(q, k, kv, cat, dcat, lse)


def _conv3(ext, w_ref, ts):
    return (w_ref[0:1, :] * ext[pl.ds(HALO - 2, ts), :] + w_ref[1:2, :] * ext[pl.ds(HALO - 1, ts), :]
            + w_ref[2:3, :] * ext[pl.ds(HALO, ts), :])


def _conv3_t(ext2, w_ref, ts):
    return (w_ref[0:1, :] * ext2[pl.ds(2, ts), :] + w_ref[1:2, :] * ext2[pl.ds(1, ts), :] + w_ref[2:3, :] * ext2[pl.ds(0, ts), :])


def _sconv_fwd(proj, w):
    s = proj.shape[0]
    ts = min(ROW_TILE, s)

    def body(b_ref, c_ref, h_ref, hc_ref, hh_ref, w_ref, o_ref, ext):
        i = pl.program_id(0)
        ext[0:HALO, :] = hc_ref[...].astype(F32) * hh_ref[...].astype(F32) * (i > 0).astype(F32)
        ext[HALO:HALO + ts, :] = c_ref[...].astype(F32) * h_ref[...].astype(F32)
        o_ref[...] = (b_ref[...].astype(F32) * _conv3(ext, w_ref, ts)).astype(BF16)

    def col(cb):
        return pl.BlockSpec((ts, SC), lambda i: (i, cb))

    def halo(cb):
        return pl.BlockSpec((HALO, SC), lambda i: (_prev_halo(i, ts), cb))

    return pl.pallas_call(
        body, name="sconv_fwd", grid=(s // ts,), out_shape=jax.ShapeDtypeStruct((s, SC), BF16),
        in_specs=[col(2), col(3), col(4), halo(3), halo(4), pl.BlockSpec((3, SC), lambda i: (0, 0))],
        out_specs=pl.BlockSpec((ts, SC), lambda i: (i, 0)), scratch_shapes=[pltpu.VMEM((ts + HALO, SC), F32)],
        compiler_params=_cp("parallel"),
    )(proj, proj, proj, proj, proj, w)


def _sconv_bwd(proj, dcat, w):
    s = proj.shape[0]
    ts = min(ROW_TILE, s)
    n = s // ts

    def body(b_ref, c_ref, h_ref, hc_ref, hh_ref, dy_ref, ndy_ref, nb_ref, w_ref, db_ref, dc_ref, dh_ref, dw_ref, ext, ext2):
        i = pl.program_id(0)

        @pl.when(i == 0)
        def _():
            dw_ref[...] = jnp.zeros_like(dw_ref)

        cv, hv, bv = c_ref[...].astype(F32), h_ref[...].astype(F32), b_ref[...].astype(F32)
        ext[0:HALO, :] = hc_ref[...].astype(F32) * hh_ref[...].astype(F32) * (i > 0).astype(F32)
        ext[HALO:HALO + ts, :] = cv * hv
        dy = dy_ref[...].astype(F32)
        db_ref[...] = (dy * _conv3(ext, w_ref, ts)).astype(BF16)
        dyb = dy * bv
        ext2[0:ts, :] = dyb
        ext2[ts:ts + HALO, :] = ndy_ref[...].astype(F32) * nb_ref[...].astype(F32) * (i < n - 1).astype(F32)
        dg = _conv3_t(ext2, w_ref, ts)
        dc_ref[...] = (dg * hv).astype(BF16)
        dh_ref[...] = (dg * cv).astype(BF16)
        for kk in range(3):
            dw_ref[kk:kk + 1, :] += _rowsum(dyb * ext[pl.ds(HALO - 2 + kk, ts), :])

    def col(cb):
        return pl.BlockSpec((ts, SC), lambda i: (i, cb))

    def halo(cb):
        return pl.BlockSpec((HALO, SC), lambda i: (_prev_halo(i, ts), cb))

    def nxt(cb):
        return pl.BlockSpec((HALO, SC), lambda i: (_next_halo(i, ts, n), cb))

    out = jax.ShapeDtypeStruct((s, SC), BF16)
    o0 = pl.BlockSpec((ts, SC), lambda i: (i, 0))
    return pl.pallas_call(
        body, name="sconv_bwd", grid=(n,), out_shape=(out, out, out, jax.ShapeDtypeStruct((3, SC), F32)),
        in_specs=[col(2), col(3), col(4), halo(3), halo(4), col(4), nxt(4), nxt(2), pl.BlockSpec((3, SC), lambda i: (0, 0))],
        out_specs=(o0, o0, o0, pl.BlockSpec((3, SC), lambda i: (0, 0))),
        scratch_shapes=[pltpu.VMEM((ts + HALO, SC), F32), pltpu.VMEM((ts + HALO, SC), F32)], compiler_params=_cp("arbitrary"),
    )(proj, proj, proj, proj, proj, dcat, dcat, proj, w)


def _ffn_stage(ext, u_ref, halo_ref, i, ts):
    ext[0:HALO, :] = halo_ref[...].astype(F32) * (i > 0).astype(F32)
    ext[HALO:HALO + ts, :] = u_ref[...].astype(F32)


def _ffn_specs(ts):
    cur = pl.BlockSpec((2, None, ts, FB), lambda j, i: (0, j, i, 0))
    halo = pl.BlockSpec((2, None, HALO, FB), lambda j, i: (0, j, _prev_halo(i, ts), 0))
    w = pl.BlockSpec((2, None, 3, FB), lambda j, i: (0, j, 0, 0))
    b = pl.BlockSpec((2, None, 1, FB), lambda j, i: (0, j, 0, 0))
    return cur, halo, w, b


def _ffn_act(upre, fcw, fcb):
    s = upre.shape[1]
    ts = min(ROW_TILE, s)

    def body(u_ref, halo_ref, w_ref, b_ref, o_ref, ext):
        i = pl.program_id(1)
        _ffn_stage(ext, u_ref.at[0], halo_ref.at[0], i, ts)
        gate = b_ref[0] + _conv3(ext, w_ref.at[0], ts)
        _ffn_stage(ext, u_ref.at[1], halo_ref.at[1], i, ts)
        up = b_ref[1] + _conv3(ext, w_ref.at[1], ts)
        o_ref[...] = (gate * _sig(gate) * up).astype(BF16)

    cur, halo, w, b = _ffn_specs(ts)
    u4 = upre.reshape(2, 4, s, FB)
    return pl.pallas_call(
        body, name="ffn_act", grid=(4, s // ts), out_shape=jax.ShapeDtypeStruct((4, s, FB), BF16),
        in_specs=[cur, halo, w, b], out_specs=pl.BlockSpec((None, ts, FB), lambda j, i: (j, i, 0)),
        scratch_shapes=[pltpu.VMEM((ts + HALO, FB), F32)], compiler_params=_cp("parallel", "parallel"),
    )(u4, u4, fcw.reshape(2, 4, 3, FB), fcb.reshape(2, 4, 1, FB))


def _ffn_bwd_a(upre, dact, fcw, fcb):
    s = upre.shape[1]
    ts = min(ROW_TILE, s)

    def body(u_ref, halo_ref, w_ref, b_ref, da_ref, du_ref, db_ref, ext):
        i = pl.program_id(1)

        @pl.when(i == 0)
        def _():
            db_ref[...] = jnp.zeros_like(db_ref)

        _ffn_stage(ext, u_ref.at[0], halo_ref.at[0], i, ts)
        gate = b_ref[0] + _conv3(ext, w_ref.at[0], ts)
        _ffn_stage(ext, u_ref.at[1], halo_ref.at[1], i, ts)
        up = b_ref[1] + _conv3(ext, w_ref.at[1], ts)
        sg = _sig(gate)
        da = da_ref[...].astype(F32)
        dgate = da * up * sg * (1.0 + gate * (1.0 - sg))
        dup = da * gate * sg
        du_ref[0] = dgate.astype(BF16)
        du_ref[1] = dup.astype(BF16)
        db_ref[0] += _rowsum(dgate)
        db_ref[1] += _rowsum(dup)

    cur, halo, w, b = _ffn_specs(ts)
    u4 = upre.reshape(2, 4, s, FB)
    du, db = pl.pallas_call(
        body, name="ffn_bwd_a", grid=(4, s // ts),
        out_shape=(jax.ShapeDtypeStruct((2, 4, s, FB), BF16), jax.ShapeDtypeStruct((2, 4, 1, FB), F32)),
        in_specs=[cur, halo, w, b, pl.BlockSpec((None, ts, FB), lambda j, i: (j, i, 0))], out_specs=(cur, b),
        scratch_shapes=[pltpu.VMEM((ts + HALO, FB), F32)], compiler_params=_cp("parallel", "arbitrary"),
    )(u4, u4, fcw.reshape(2, 4, 3, FB), fcb.reshape(2, 4, 1, FB), dact)
    return du.reshape(NDEV, s, FB), db.reshape(NDEV, 1, FB)


def _ffn_bwd_b(du, upre, fcw):
    s = upre.shape[1]
    ts = min(ROW_TILE, s)
    n = s // ts

    def body(du_ref, ndu_ref, u_ref, halo_ref, w_ref, dup_ref, dw_ref, ext, ext2):
        i = pl.program_id(1)

        @pl.when(i == 0)
        def _():
            dw_ref[...] = jnp.zeros_like(dw_ref)

        duv = du_ref[...].astype(F32)
        ext2[0:ts, :] = duv
        ext2[ts:ts + HALO, :] = ndu_ref[...].astype(F32) * (i < n - 1).astype(F32)
        dup_ref[...] = _conv3_t(ext2, w_ref, ts).astype(BF16)
        _ffn_stage(ext, u_ref, halo_ref, i, ts)
        for kk in range(3):
            dw_ref[kk:kk + 1, :] += _rowsum(duv * ext[pl.ds(HALO - 2 + kk, ts), :])

    cur = pl.BlockSpec((None, ts, FB), lambda j, i: (j, i, 0))
    w = pl.BlockSpec((None, 3, FB), lambda j, i: (j, 0, 0))
    return pl.pallas_call(
        body, name="ffn_bwd_b", grid=(NDEV, n),
        out_shape=(jax.ShapeDtypeStruct((NDEV, s, FB), BF16), jax.ShapeDtypeStruct((NDEV, 3, FB), F32)),
        in_specs=[cur, pl.BlockSpec((None, HALO, FB), lambda j, i: (j, _next_halo(i, ts, n), 0)), cur,
                  pl.BlockSpec((None, HALO, FB), lambda j, i: (j, _prev_halo(i, ts), 0)), w],
        out_specs=(cur, w), scratch_shapes=[pltpu.VMEM((ts + HALO, FB), F32), pltpu.VMEM((ts + HALO, FB), F32)],
        compiler_params=_cp("parallel", "arbitrary"),
    )(du, du, upre, upre, fcw)


def _softplus(v):
    e = jnp.exp(-jnp.abs(v))
    return jnp.maximum(v, 0.0) + jnp.where(e < 1e-4, e * (1.0 - 0.5 * e), jnp.log(1.0 + e))


def _ssd_consts():
    L = SSD_L
    r = lax.broadcasted_iota(jnp.int32, (L, L), 0)
    c = lax.broadcasted_iota(jnp.int32, (L, L), 1)
    tri = r >= c
    er = lax.broadcasted_iota(jnp.int32, (LANE, SSD_DIM), 0)
    ec = lax.broadcasted_iota(jnp.int32, (LANE, SSD_DIM), 1)
    expand = ((ec >= er * 64) & (ec < er * 64 + 64)).astype(F32)
    return tri, expand


def _ssd_conv4(ext, cw_ref, cb_ref):
    L = SSD_L
    pre = cb_ref[...] + cw_ref[0:1, :] * ext[pl.ds(HALO - 3, L), :]
    for kk in range(1, 4):
        pre = pre + cw_ref[kk:kk + 1, :] * ext[pl.ds(HALO - 3 + kk, L), :]
    return pre


def _ssd_common(xbc_ref, halo_ref, dt_ref, cw_ref, cb_ref, dtb_ref, alog_ref, ext, first):
    L = SSD_L
    tri, expand = _ssd_consts()
    ext[0:HALO, :] = halo_ref[...].astype(F32) * (1.0 - first.astype(F32))
    ext[HALO:HALO + L, :] = xbc_ref[...].astype(F32)
    pre = _ssd_conv4(ext, cw_ref, cb_ref)
    sg = _sig(pre)
    act = pre * sg
    lane = lax.broadcasted_iota(jnp.int32, (1, LANE), 1)
    m4 = lane < SSD_H
    raw = dt_ref[...].astype(F32) + dtb_ref[...]
    dtv = jnp.where(m4, _softplus(raw), 0.0)
    av = jnp.where(m4, -jnp.exp(alog_ref[...]), 0.0)
    adt = dtv * av
    acs = _dot(tri.astype(F32), adt, precision=HI)
    acs_b = _dot(acs, expand, precision=HI)
    dt_b = _dot(dtv, expand, precision=HI)
    return dict(tri=tri, expand=expand, pre=pre, sg=sg, act=act, raw=raw, dtv=dtv, av=av, m4=m4, acs=acs, acs_b=acs_b,
                dt_b=dt_b, lane=lane)


def _head_terms(cm, h):
    L = SSD_L
    acs, tri = cm["acs"], cm["tri"]
    lane_l = lax.broadcasted_iota(jnp.int32, (L, LANE), 1)
    sub_l = lax.broadcasted_iota(jnp.int32, (LANE, L), 0)
    col = jnp.sum(jnp.where(lane_l == h, acs, 0.0), axis=1, keepdims=True)
    row = jnp.sum(jnp.where(sub_l == h, acs.T, 0.0), axis=0, keepdims=True)
    dec = jnp.where(tri, jnp.exp(jnp.where(tri, col - row, NEG)), 0.0)
    rowi = lax.broadcasted_iota(jnp.int32, (L, 1), 0)
    last = jnp.sum(jnp.where(rowi == L - 1, col, 0.0), axis=0, keepdims=True)
    dte = jnp.exp(last - col)
    return col, dec, last, dte


def _ssd_fwd(proj, cw, cb, dtb, alog, dvec, nw):
    s = proj.shape[0]
    L = SSD_L
    nc = s // L

    def body(z_ref, xbc_ref, halo_ref, dt_ref, cw_ref, cb_ref, dtb_ref, alog_ref, d_ref, nw_ref, y_ref, ypre_ref, st_ref, ext, state):
        i = pl.program_id(0)

        @pl.when(i == 0)
        def _():
            state[...] = jnp.zeros_like(state)

        cm = _ssd_common(xbc_ref, halo_ref, dt_ref, cw_ref, cb_ref, dtb_ref, alog_ref, ext, i == 0)
        act = cm["act"]
        xs = act[:, 0:256]
        bm = (act[:, 256:384], act[:, 384:512])
        cmat = (act[:, 512:640].astype(BF16), act[:, 640:768].astype(BF16))
        xdt = xs * cm["dt_b"]
        prev = state[...]
        st_ref[...] = prev
        prev_bf = prev.astype(BF16)
        gm = [_dot(cmat[g], bm[g].astype(BF16), NT) for g in range(2)]
        lane2 = lax.broadcasted_iota(jnp.int32, (1, SSD_DIM), 1)
        rows2 = lax.broadcasted_iota(jnp.int32, (SSD_DIM, 1), 0)
        ydiag = jnp.zeros((L, SSD_DIM), F32)
        contrib = jnp.zeros((SSD_DIM, LANE), F32)
        cd_rows = jnp.zeros((SSD_DIM, 1), F32)
        for h in range(SSD_H):
            g = h // 2
            col, dec, last, dte = _head_terms(cm, h)
            mh = (lane2 >= 64 * h) & (lane2 < 64 * h + 64)
            xm = jnp.where(mh, xdt, 0.0).astype(BF16)
            ydiag += _dot((gm[g] * dec).astype(BF16), xm)
            contrib += _dot(xm, (bm[g] * dte).astype(BF16), TN)
            cd_rows += jnp.where((rows2 >= 64 * h) & (rows2 < 64 * h + 64), jnp.exp(last), 0.0)
        yo = jnp.where(lane2 < 128, _dot(cmat[0], prev_bf, NT), _dot(cmat[1], prev_bf, NT))
        y = ydiag + yo * jnp.exp(cm["acs_b"]) + xs * d_ref[...]
        state[...] = prev * cd_rows + contrib
        ypre_ref[...] = y
        zz = z_ref[...].astype(F32)
        gt = y * zz * _sig(zz)
        y_ref[...] = (gt * lax.rsqrt(jnp.mean(gt * gt, axis=-1, keepdims=True) + EPS) * nw_ref[...]).astype(BF16)

    def vec(w):
        return pl.BlockSpec((1, w), lambda i: (0, 0))

    return pl.pallas_call(
        body, name="ssd_fwd", grid=(nc,),
        out_shape=(jax.ShapeDtypeStruct((s, SSD_DIM), BF16), jax.ShapeDtypeStruct((s, SSD_DIM), F32),
                   jax.ShapeDtypeStruct((nc, SSD_DIM, LANE), F32)),
        in_specs=[pl.BlockSpec((L, SSD_DIM), lambda i: (i, 5)), pl.BlockSpec((L, SSD_CONV), lambda i: (i, 2)),
                  pl.BlockSpec((HALO, SSD_CONV), lambda i: (_prev_halo(i, L), 2)), pl.BlockSpec((L, LANE), lambda i: (i, 18)),
                  pl.BlockSpec((4, SSD_CONV), lambda i: (0, 0)), vec(SSD_CONV), vec(LANE), vec(LANE), vec(SSD_DIM), vec(SSD_DIM)],
        out_specs=(pl.BlockSpec((L, SSD_DIM), lambda i: (i, 0)), pl.BlockSpec((L, SSD_DIM), lambda i: (i, 0)),
                   pl.BlockSpec((None, SSD_DIM, LANE), lambda i: (i, 0, 0))),
        scratch_shapes=[pltpu.VMEM((L + HALO, SSD_CONV), F32), pltpu.VMEM((SSD_DIM, LANE), F32)], compiler_params=_cp("arbitrary"),
    )(proj, proj, proj, proj, cw, cb, dtb, alog, dvec, nw)


def _ssd_bwd(proj, dcat, ypre, states, cw, cb, dtb, alog, dvec, nw):
    s = proj.shape[0]
    L = SSD_L
    nc = s // L

    def body(z_ref, xbc_ref, halo_ref, dt_ref, dy_ref, ypre_ref, st_ref, cw_ref, cb_ref, dtb_ref, alog_ref, d_ref, nw_ref,
             dz_ref, dxbc_ref, ddt_ref, dcw_ref, dcb_ref, ddtb_ref, dalog_ref, dd_ref, dnw_ref, ext, ext2, carry, dstate, ddl):
        i = pl.program_id(0)
        r = nc - 1 - i

        @pl.when(i == 0)
        def _():
            for ref in (dcw_ref, dcb_ref, ddtb_ref, dalog_ref, dd_ref, dnw_ref, carry, dstate, ddl):
                ref[...] = jnp.zeros_like(ref)

        cm = _ssd_common(xbc_ref, halo_ref, dt_ref, cw_ref, cb_ref, dtb_ref, alog_ref, ext, r == 0)
        tri, expand, act = cm["tri"], cm["expand"], cm["act"]
        xs = act[:, 0:256]
        bm = (act[:, 256:384], act[:, 384:512])
        cmat = (act[:, 512:640], act[:, 640:768])
        bm_bf = [v.astype(BF16) for v in bm]
        cm_bf = [v.astype(BF16) for v in cmat]
        dt_b = cm["dt_b"]
        xdt = xs * dt_b
        xdt_bf = xdt.astype(BF16)
        ea_b = jnp.exp(cm["acs_b"])
        prev = st_ref[...]
        prev_bf = prev.astype(BF16)
        lane2 = lax.broadcasted_iota(jnp.int32, (1, SSD_DIM), 1)
        rows2 = lax.broadcasted_iota(jnp.int32, (SSD_DIM, 1), 0)
        lane_l = lax.broadcasted_iota(jnp.int32, (L, LANE), 1)
        rowi = lax.broadcasted_iota(jnp.int32, (L, 1), 0)

        y = ypre_ref[...]
        zz = z_ref[...].astype(F32)
        sz = _sig(zz)
        gt = y * zz * sz
        dgt, dwt = _rms_bwd_math(gt, nw_ref[...], dy_ref[...].astype(F32))
        dnw_ref[...] += _rowsum(dwt)
        dy = dgt * zz * sz
        dz_ref[...] = (dgt * y * sz * (1.0 + zz * (1.0 - sz))).astype(BF16)

        ddl[0:1, :] += _rowsum(dy * xs)
        dxs = dy * d_ref[...]

        yo = jnp.where(lane2 < 128, _dot(cm_bf[0], prev_bf, NT), _dot(cm_bf[1], prev_bf, NT))
        dacs_b = dy * yo * ea_b
        dyo = dy * ea_b
        dyo_g = (jnp.where(lane2 < 128, dyo, 0.0).astype(BF16), jnp.where(lane2 >= 128, dyo, 0.0).astype(BF16))
        dc = [_dot(dyo_g[g], prev_bf) for g in range(2)]
        dprev = _dot(dyo_g[0], cm_bf[0], TN) + _dot(dyo_g[1], cm_bf[1], TN)

        gm = [_dot(cm_bf[g], bm_bf[g], NT) for g in range(2)]
        dgm = [jnp.zeros((L, L), F32), jnp.zeros((L, L), F32)]
        db = [jnp.zeros((L, LANE), F32), jnp.zeros((L, LANE), F32)]
        dxdt = jnp.zeros((L, SSD_DIM), F32)
        dacs = jnp.zeros((L, LANE), F32)
        dlast = jnp.zeros((1, LANE), F32)
        cd_rows = jnp.zeros((SSD_DIM, 1), F32)
        dst = dstate[...]
        dst_bf = dst.astype(BF16)
        dsp = dst * prev
        ones = jnp.ones((L, LANE), F32)
        for h in range(SSD_H):
            g = h // 2
            col, dec, last, dte = _head_terms(cm, h)
            mh = (lane2 >= 64 * h) & (lane2 < 64 * h + 64)
            rh = (rows2 >= 64 * h) & (rows2 < 64 * h + 64)
            sc = gm[g] * dec
            xm = jnp.where(mh, xdt, 0.0).astype(BF16)
            dym = jnp.where(mh, dy, 0.0).astype(BF16)
            dsc = _dot(dym, xdt_bf, NT)
            dxdt += _dot(sc.astype(BF16), dym, TN)
            dgm[g] += dsc * dec
            dd = dsc * sc
            rs = jnp.sum(dd, axis=1, keepdims=True)
            cs = _dot(dd, ones, TN, precision=HI)
            dacs += jnp.where(lane_l == h, rs - cs, 0.0)
            bd = (bm[g] * dte).astype(BF16)
            dxdt += jnp.where(mh, _dot(bd, dst_bf, NT), 0.0)
            dbd = _dot(xm, dst_bf)
            db[g] += dbd * dte
            tt = jnp.sum(dbd * bm[g], axis=1, keepdims=True) * dte
            dacs += jnp.where(lane_l == h, -tt, 0.0)
            cdh = jnp.exp(last)
            dcd = jnp.sum(jnp.sum(jnp.where(rh, dsp, 0.0), axis=1, keepdims=True), axis=0, keepdims=True)
            dlast += jnp.where(cm["lane"] == h, jnp.sum(tt, axis=0, keepdims=True) + dcd * cdh, 0.0)
            cd_rows += jnp.where(rh, cdh, 0.0)
        dacs += jnp.where(rowi == L - 1, dlast, 0.0)
        dacs += _dot(dacs_b, expand, NT, precision=HI)
        dstate[...] = dprev + dst * cd_rows

        for g in range(2):
            dgb = dgm[g].astype(BF16)
            dc[g] += _dot(dgb, bm_bf[g])
            db[g] += _dot(dgb, cm_bf[g], TN)

        dadt = _dot(tri.astype(F32), dacs, TN, precision=HI)
        ddtv = dadt * cm["av"] + _dot(dxdt * xs, expand, NT, precision=HI)
        dalog_ref[...] += _rowsum(dadt * cm["dtv"]) * cm["av"]
        dxs += dxdt * dt_b
        draw = jnp.where(cm["m4"], ddtv * _sig(cm["raw"]), 0.0)
        ddtb_ref[...] += _rowsum(draw)
        ddt_ref[...] = draw.astype(BF16)

        dact = jnp.concatenate([dxs, db[0], db[1], dc[0], dc[1]], axis=1)
        sg, pre = cm["sg"], cm["pre"]
        dpre = dact * sg * (1.0 + pre * (1.0 - sg))
        dcb_ref[...] += _rowsum(dpre)
        for kk in range(4):
            dcw_ref[kk:kk + 1, :] += _rowsum(dpre * ext[pl.ds(HALO - 3 + kk, L), :])
        ext2[0:L, :] = dpre
        ext2[L:L + HALO, :] = carry[...]
        dx = cw_ref[3:4, :] * ext2[pl.ds(0, L), :]
        for kk in range(3):
            dx = dx + cw_ref[kk:kk + 1, :] * ext2[pl.ds(3 - kk, L), :]
        dxbc_ref[...] = dx.astype(BF16)
        carry[...] = dpre[0:HALO, :]

        @pl.when(i == nc - 1)
        def _():
            dd_ref[...] = _dot(ddl[...], expand, NT, precision=HI)

    def vec(w):
        return pl.BlockSpec((1, w), lambda i: (0, 0))

    def rv(i):
        return nc - 1 - i

    return pl.pallas_call(
        body, name="ssd_bwd", grid=(nc,),
        out_shape=(jax.ShapeDtypeStruct((s, SSD_DIM), BF16), jax.ShapeDtypeStruct((s, SSD_CONV), BF16), jax.ShapeDtypeStruct((s, LANE), BF16),
                   jax.ShapeDtypeStruct((4, SSD_CONV), F32), jax.ShapeDtypeStruct((1, SSD_CONV), F32), jax.ShapeDtypeStruct((1, LANE), F32),
                   jax.ShapeDtypeStruct((1, LANE), F32), jax.ShapeDtypeStruct((8, LANE), F32), jax.ShapeDtypeStruct((1, SSD_DIM), F32)),
        in_specs=[pl.BlockSpec((L, SSD_DIM), lambda i: (rv(i), 5)), pl.BlockSpec((L, SSD_CONV), lambda i: (rv(i), 2)),
                  pl.BlockSpec((HALO, SSD_CONV), lambda i: (_prev_halo(rv(i), L), 2)), pl.BlockSpec((L, LANE), lambda i: (rv(i), 18)),
                  pl.BlockSpec((L, SSD_DIM), lambda i: (rv(i), 5)), pl.BlockSpec((L, SSD_DIM), lambda i: (rv(i), 0)),
                  pl.BlockSpec((None, SSD_DIM, LANE), lambda i: (rv(i), 0, 0)),
                  pl.BlockSpec((4, SSD_CONV), lambda i: (0, 0)), vec(SSD_CONV), vec(LANE), vec(LANE), vec(SSD_DIM), vec(SSD_DIM)],
        out_specs=(pl.BlockSpec((L, SSD_DIM), lambda i: (rv(i), 0)), pl.BlockSpec((L, SSD_CONV), lambda i: (rv(i), 0)),
                   pl.BlockSpec((L, LANE), lambda i: (rv(i), 0)), pl.BlockSpec((4, SSD_CONV), lambda i: (0, 0)), vec(SSD_CONV),
                   vec(LANE), vec(LANE), pl.BlockSpec((8, LANE), lambda i: (0, 0)), vec(SSD_DIM)),
        scratch_shapes=[pltpu.VMEM((L + HALO, SSD_CONV), F32), pltpu.VMEM((L + HALO, SSD_CONV), F32), pltpu.VMEM((HALO, SSD_CONV), F32),
                        pltpu.VMEM((SSD_DIM, LANE), F32), pltpu.VMEM((8, SSD_DIM), F32)],
        compiler_params=_cp("arbitrary"),
    )(proj, proj, proj, proj, dcat, ypre, states, cw, cb, dtb, alog, dvec, nw)


def _adamw(parts, w, m, v, name):
    r, c = w.shape
    tr = r
    for cand in (256, 128, 64, 32, 16, 8):
        if r % cand == 0 and (cand * c * 4) <= 2 * 1024 * 1024:
            tr = cand
            break
    c1 = 1.0 - B1 ** STEP
    c2 = 1.0 - B2 ** STEP

    def body(p_ref, w_ref, m_ref, v_ref, g_ref, d_ref, nm_ref, nv_ref):
        g = p_ref[0].astype(F32)
        for dev in range(1, NDEV):
            g = g + p_ref[dev].astype(F32)
        mn = B1 * m_ref[...] + (1.0 - B1) * g
        vn = B2 * v_ref[...] + (1.0 - B2) * (g * g)
        g_ref[...] = g
        nm_ref[...] = mn
        nv_ref[...] = vn
        d_ref[...] = -LR * ((mn / c1) / (jnp.sqrt(vn / c2) + AEPS) + WD * w_ref[...])

    blk = pl.BlockSpec((tr, c), lambda i: (i, 0))
    out = jax.ShapeDtypeStruct((r, c), F32)
    return pl.pallas_call(
        body, name=name, grid=(r // tr,), out_shape=(out, out, out, out),
        in_specs=[pl.BlockSpec((NDEV, tr, c), lambda i: (0, i, 0)), blk, blk, blk], out_specs=(blk, blk, blk, blk),
        compiler_params=_cp("parallel"),
    )(parts, w, m, v)


def _pad_win(w):
    z = lambda n: jnp.zeros((w.shape[0], n), w.dtype)
    return jnp.concatenate([w[:, :384], z(64), w[:, 384:416], z(32), w[:, 416:], z(124)], axis=1)


def _unpad_win(g):
    return jnp.concatenate([g[:, :384], g[:, 448:480], g[:, 512:2308]], axis=1)


def _pad_wout(w):
    att = jnp.pad(w[:512].reshape(HEADS, 64, D), ((0, 0), (64, 0), (0, 0))).reshape(HEADS * LANE, D)
    return jnp.concatenate([att, w[512:]], axis=0)


def _unpad_wout(g):
    att = g[:HEADS * LANE].reshape(HEADS, LANE, D)[:, 64:, :].reshape(512, D)
    return jnp.concatenate([att, g[HEADS * LANE:]], axis=0)


def _lanes(v, n=LANE):
    return jnp.pad(v, (0, n - v.shape[0])).reshape(1, n)


def _prep_layer(big, small, l):
    p = {}
    p["win"] = _pad_win(big["w_in"][:, l].reshape(D, 2212))
    p["wq"] = jnp.pad(big["mla_w_q_up"][:, l], ((0, 0), (0, 0), (0, LANE - 96)))
    p["wkv"] = big["mla_w_kv_up"][:, l]
    p["scw"] = big["sc_conv_w"][:, l].astype(F32).transpose(1, 0, 2).reshape(3, SC)
    p["ssdcw"] = big["ssd_conv_w"][:, l].astype(F32).transpose(1, 0, 2).reshape(4, SSD_CONV)
    p["wout"] = _pad_wout(big["w_out"][:, l].reshape(1024, D))
    p["wup"] = big["ffn_w_up"][:, l]
    p["fcw"] = big["ffn_conv_w"][:, l].astype(F32)
    p["wdown"] = big["ffn_w_down"][:, l].reshape(4, FB, D)
    for nm in ("norm_mix_pre", "norm_mix_post", "norm_ffn_pre", "norm_ffn_post", "mla_q_norm", "mla_kv_norm", "ssd_conv_b", "ssd_norm"):
        p[nm] = small[nm][l].reshape(1, -1)
    p["dtb"] = _lanes(small["ssd_dt_bias"][l])
    p["alog"] = _lanes(small["ssd_a_log"][l])
    p["dvec"] = jnp.repeat(small["ssd_d"][l], 64).reshape(1, SSD_DIM)
    p["fcb"] = small["ffn_conv_b"][l].reshape(NDEV, 1, FB)
    return p


def _rope_tables(positions):
    inv_freq = 1.0 / (ROPE_THETA ** (jnp.arange(0, ROPE, 2, dtype=F32) / ROPE))
    ang = positions.astype(F32)[:, None] * inv_freq
    cos, sin = jnp.cos(ang), jnp.sin(ang)
    s = positions.shape[0]
    z = lambda n: jnp.zeros((s, n), F32)
    tc = jnp.concatenate([jnp.ones((s, 64), F32), cos, cos, z(32)], axis=1)
    ta = jnp.concatenate([z(64), -sin, z(48)], axis=1)
    tb = jnp.concatenate([z(80), sin, z(32)], axis=1)
    return tc, ta, tb


def _layer_fwd(xv, p, tabs):
    h = _rms(xv, p["norm_mix_pre"], BF16, "rms_pre")
    proj = _mm_rows("in_proj", h, p["win"], BF16, NN)
    q, k, kv = _mla_prep(proj, tabs, p["mla_q_norm"], p["mla_kv_norm"], p["wq"], p["wkv"])
    o, lse = _flash_fwd(q, k, kv)
    yconv = _sconv_fwd(proj, p["scw"])
    yssd, ypre, states = _ssd_fwd(proj, p["ssdcw"], p["ssd_conv_b"], p["dtb"], p["alog"], p["dvec"], p["ssd_norm"])
    cat = jnp.concatenate([o, yconv, yssd], axis=1)
    mixed = _mm_rows("out_proj", cat, p["wout"], F32, NN)
    x1 = _add_rms(xv, mixed, p["norm_mix_post"], "add_rms")
    h2 = _rms(x1, p["norm_ffn_pre"], BF16, "rms_pre")
    upre = _mm_up(h2, p["wup"])
    act = _ffn_act(upre, p["fcw"], p["fcb"])
    f = _mm_down(act, p["wdown"])
    x2 = _add_rms(x1, f, p["norm_ffn_post"], "add_rms")
    saved = dict(x=xv, h=h, proj=proj, q=q, k=k, kv=kv, lse=lse, ypre=ypre, states=states, cat=cat, mixed=mixed, x1=x1, h2=h2,
                 upre=upre, act=act, f=f)
    return x2, saved


def _layer_bwd(dx2, sv, p, tabs):
    df, g_nfpo = _rms_bwd(sv["f"], p["norm_ffn_post"], dx2, None, BF16, "rms_bwd_post")
    dact = _mm_dact(df, p["wdown"])
    g_wdown = _mm_dwdown(sv["act"], df)
    du, g_fcb = _ffn_bwd_a(sv["upre"], dact, p["fcw"], p["fcb"])
    dupre, g_fcw = _ffn_bwd_b(du, sv["upre"], p["fcw"])
    dh2 = _mm_dh2(dupre, p["wup"])
    g_wup = _mm_dwup(sv["h2"], dupre)
    dx1, g_nfp = _rms_bwd(sv["x1"], p["norm_ffn_pre"], dh2, dx2, F32, "rms_bwd_pre")
    dmixed, g_nmpo = _rms_bwd(sv["mixed"], p["norm_mix_post"], dx1, None, BF16, "rms_bwd_post")
    dcat = _mm_rows("dcat", dmixed, p["wout"], BF16, NT)
    g_wout = _mm_wgrad("dw_out", sv["cat"], dmixed, BF16)
    dq, dk, dv = _flash_bwd(sv["q"], sv["k"], sv["kv"], sv["cat"], dcat, sv["lse"])
    dcq, dckv, dkr, g_wq, g_wkv, g_qn, g_kvn = _mla_prep_bwd(sv["proj"], tabs, p["mla_q_norm"], p["mla_kv_norm"], p["wq"], p["wkv"], dq, dk, dv)
    dscb, dscc, dsch, g_scw = _sconv_bwd(sv["proj"], dcat, p["scw"])
    dz, dxbc, ddt, g_cw, g_cb, g_dtb, g_alog, g_d, g_nw = _ssd_bwd(
        sv["proj"], dcat, sv["ypre"], sv["states"], p["ssdcw"], p["ssd_conv_b"], p["dtb"], p["alog"], p["dvec"], p["ssd_norm"])
    dproj = jnp.concatenate([dcq, dckv, dkr, dscb, dscc, dsch, dz, dxbc, ddt], axis=1)
    dh = _mm_rows("dh", dproj, p["win"], BF16, NT)
    g_win = _mm_wgrad("dw_in", sv["h"], dproj, BF16)
    dx, g_nmp = _rms_bwd(sv["x"], p["norm_mix_pre"], dh, dx1, F32, "rms_bwd_pre")
    big = {
        "w_in": _unpad_win(g_win).reshape(NDEV, 128, 2212),
        "mla_w_q_up": g_wq[:, :, :96].astype(BF16),
        "mla_w_kv_up": g_wkv.astype(BF16),
        "sc_conv_w": g_scw.reshape(3, NDEV, 32).transpose(1, 0, 2).astype(BF16),
        "ssd_conv_w": g_cw.reshape(4, NDEV, 96).transpose(1, 0, 2).astype(BF16),
        "w_out": _unpad_wout(g_wout).reshape(NDEV, 128, D),
        "ffn_w_up": g_wup,
        "ffn_conv_w": g_fcw.astype(BF16),
        "ffn_w_down": g_wdown.reshape(NDEV, 352, D),
    }
    small = {
        "norm_mix_pre": g_nmp[0], "norm_mix_post": g_nmpo[0], "norm_ffn_pre": g_nfp[0], "norm_ffn_post": g_nfpo[0],
        "mla_q_norm": g_qn[0], "mla_kv_norm": g_kvn[0], "ssd_conv_b": g_cb[0], "ssd_dt_bias": g_dtb[0, :SSD_H],
        "ssd_a_log": g_alog[0, :SSD_H], "ssd_d": g_d[0, :SSD_H], "ssd_norm": g_nw[0], "ffn_conv_b": g_fcb.reshape(-1),
    }
    return dx, big, small


def _local_step(xv, positions, target, layers):
    tabs = _rope_tables(positions)
    saved = []
    for p in layers:
        xv, sv = _layer_fwd(xv, p, tabs)
        saved.append(sv)
    loss, dx = _loss_head(xv, target)
    bigs, smalls = [None] * DEPTH, [None] * DEPTH
    for l in reversed(range(len(layers))):
        dx, bigs[l], smalls[l] = _layer_bwd(dx, saved[l], layers[l], tabs)
    return loss[0, 0], dx, bigs, smalls


def _pack_rows(flat, lead, width, mult):
    n = flat.shape[-1]
    rows = -(-n // (width * mult)) * mult
    pad = [(0, 0)] * (flat.ndim - 1) + [(0, rows * width - n)]
    return jnp.pad(flat, pad).reshape(lead + (rows, width))


def _rows2(n):
    shape = SHAPES[n]
    return (math.prod(shape[:-1]), shape[-1])


def _group_pack(group, get, lead):
    width, names = group
    pieces = []
    for n in names:
        rows, cols = _rows2(n)
        pad = [(0, 0)] * len(lead) + [(0, -rows % 16), (0, width - cols)]
        pieces.append(jnp.pad(get(n), pad))
    return pieces[0] if len(pieces) == 1 else jnp.concatenate(pieces, axis=len(lead))


def _group_unpack(group, buf):
    _, names = group
    res, off = {}, 0
    for n in names:
        rows, cols = _rows2(n)
        res[n] = buf[:, off:off + rows, :cols]
        off += rows + (-rows % 16)
    return res


def kernel(x, positions, norm_mix_pre, norm_mix_post, norm_ffn_pre, norm_ffn_post, w_in, mla_q_norm, mla_w_q_up, mla_kv_norm, mla_w_kv_up, sc_conv_w, ssd_conv_w, ssd_conv_b, ssd_dt_bias, ssd_a_log, ssd_d, ssd_norm, w_out, ffn_w_up, ffn_conv_w, ffn_conv_b, ffn_w_down, loss_target, m_norm_mix_pre, m_norm_mix_post, m_norm_ffn_pre, m_norm_ffn_post, m_w_in, m_mla_q_norm, m_mla_w_q_up, m_mla_kv_norm, m_mla_w_kv_up, m_sc_conv_w, m_ssd_conv_w, m_ssd_conv_b, m_ssd_dt_bias, m_ssd_a_log, m_ssd_d, m_ssd_norm, m_w_out, m_ffn_w_up, m_ffn_conv_w, m_ffn_conv_b, m_ffn_w_down, v_norm_mix_pre, v_norm_mix_post, v_norm_ffn_pre, v_norm_ffn_post, v_w_in, v_mla_q_norm, v_mla_w_q_up, v_mla_kv_norm, v_mla_w_kv_up, v_sc_conv_w, v_ssd_conv_w, v_ssd_conv_b, v_ssd_dt_bias, v_ssd_a_log, v_ssd_d, v_ssd_norm, v_w_out, v_ffn_w_up, v_ffn_conv_w, v_ffn_conv_b, v_ffn_w_down):
    given = dict(locals())
    w = {n: given[n] for n in WEIGHTS}
    m = {n: given["m_" + n] for n in WEIGHTS}
    v = {n: given["v_" + n] for n in WEIGHTS}

    gathered = _all_gather([_group_pack(g, lambda n: w[n].astype(BF16).reshape(_rows2(n)), ()) for g in GROUPS], "gather_weights")
    big = {}
    for g, buf in zip(GROUPS, gathered):
        for n, piece in _group_unpack(g, buf).items():
            big[n] = piece.reshape((NDEV,) + SHAPES[n])
    small_w = {n: w[n] for n, _ in SMALL}
    layers = [_prep_layer(big, small_w, l) for l in range(DEPTH)]

    loss, dx, bigs, smalls = _local_step(x[0], positions[0], loss_target[0], layers)
    loss = lax.psum(loss, ("x", "y", "c"))

    stacked = lambda n: jnp.stack([bigs[l][n] for l in range(DEPTH)], axis=1).reshape((NDEV,) + _rows2(n))
    recv = _all_to_all([_group_pack(g, stacked, (NDEV,)) for g in GROUPS], "exchange_grads")
    out = {}
    for g, buf in zip(GROUPS, recv):
        for n, parts in _group_unpack(g, buf).items():
            r2 = _rows2(n)
            res = _adamw(parts, w[n].reshape(r2), m[n].reshape(r2), v[n].reshape(r2), "adamw_" + n)
            out[n] = [a.reshape(SHAPES[n]) for a in res]

    sflat = jnp.concatenate([jnp.stack([smalls[l][n] for l in range(DEPTH)]).reshape(-1) for n, _ in SMALL])
    sparts = _all_gather([_pack_rows(sflat, (), LANE, 8)], "gather_small_grads")[0]
    pk = lambda d: _pack_rows(jnp.concatenate([d[n].reshape(-1) for n, _ in SMALL]), (), LANE, 8)
    res = _adamw(sparts, pk(w), pk(m), pk(v), "adamw_small")
    off = 0
    for n, width in SMALL:
        out[n] = [a.reshape(-1)[off:off + DEPTH * width].reshape(DEPTH, width) for a in res]
        off += DEPTH * width

    return (loss, dx[None], *[out[n][0] for n in WEIGHTS], *[out[n][1] for n in WEIGHTS],
            *[out[n][2] for n in WEIGHTS], *[out[n][3] for n in WEIGHTS])
```

```python
import functools
import math

import jax
import jax.numpy as jnp
from jax import lax
from jax.experimental import pallas as pl
from jax.experimental.pallas import tpu as pltpu

F32 = jnp.float32
BF16 = jnp.bfloat16

D = 1024
DEPTH = 4
NDEV = 8
HEADS = 8
QL = 256
KVL = 128
ROPE = 32
NOPE = 64
SC = 256
SSD_DIM = 256
SSD_CONV = 768
SSD_H = 4
SSD_L = 128
FFN = 2816
FB = 704
EPS = 1e-6
ROPE_THETA = 10000.0
ATT_SCALE = 96 ** -0.5
LR, B1, B2, AEPS, WD, STEP = 0.001, 0.9, 0.999, 1e-08, 0.01, 10

PW = 2432
CATW = 1536

ROW_TILE = 512
ATT_TILE = 256
FWD_HEADS = 4
BWD_HEADS = 2
HALO = 16
LANE = 128
NEG = -1e30
HI = lax.Precision.HIGHEST
NN = (((1,), (0,)), ((), ()))
NT = (((1,), (1,)), ((), ()))
TN = (((0,), (0,)), ((), ()))
VMEM_LIMIT = 56 * 1024 * 1024

SHARDED = (
    ("w_in", (4, 128, 2212)),
    ("mla_w_q_up", (4, 256, 96)),
    ("mla_w_kv_up", (4, 128, 128)),
    ("sc_conv_w", (4, 3, 32)),
    ("ssd_conv_w", (4, 4, 96)),
    ("w_out", (4, 128, 1024)),
    ("ffn_w_up", (4, 1024, 704)),
    ("ffn_conv_w", (4, 3, 704)),
    ("ffn_w_down", (4, 352, 1024)),
)
SHAPES = dict(SHARDED)
GROUPS = (
    (2212, ("w_in",)),
    (1024, ("w_out", "ffn_w_down")),
    (704, ("ffn_w_up", "ffn_conv_w")),
    (96, ("mla_w_q_up", "ssd_conv_w", "sc_conv_w")),
    (128, ("mla_w_kv_up",)),
)
SMALL = (
    ("norm_mix_pre", 1024), ("norm_mix_post", 1024), ("norm_ffn_pre", 1024), ("norm_ffn_post", 1024),
    ("mla_q_norm", 256), ("mla_kv_norm", 128), ("ssd_conv_b", 768), ("ssd_dt_bias", 4), ("ssd_a_log", 4),
    ("ssd_d", 4), ("ssd_norm", 256), ("ffn_conv_b", 5632),
)
WEIGHTS = ("norm_mix_pre", "norm_mix_post", "norm_ffn_pre", "norm_ffn_post", "w_in", "mla_q_norm", "mla_w_q_up",
           "mla_kv_norm", "mla_w_kv_up", "sc_conv_w", "ssd_conv_w", "ssd_conv_b", "ssd_dt_bias", "ssd_a_log", "ssd_d",
           "ssd_norm", "w_out", "ffn_w_up", "ffn_conv_w", "ffn_conv_b", "ffn_w_down")


def _dot(a, b, dims=NN, precision=None):
    return lax.dot_general(a, b, dims, precision=precision, preferred_element_type=F32)


def _sig(v):
    return 1.0 / (1.0 + jnp.exp(-v))


def _cp(*sem):
    return pltpu.CompilerParams(dimension_semantics=sem, vmem_limit_bytes=VMEM_LIMIT)


def _rowsum(v):
    return jnp.sum(v, axis=0, keepdims=True)


def _prev_halo(i, ts):
    return jnp.maximum(i * (ts // HALO) - 1, 0)


def _next_halo(i, ts, n):
    return jnp.minimum((i + 1) * (ts // HALO), n * (ts // HALO) - 1)


def _gather_plan(x_refs, out_refs, send_sems, recv_sems, local_sems):
    n = len(x_refs)
    x, y, cc = lax.axis_index("x"), lax.axis_index("y"), lax.axis_index("c")
    me, sibling = (x, y, cc), (x, y, 1 - cc)
    chips = [(1 - x, y), (x, 1 - y), (1 - x, 1 - y)]

    def rows(t, px, py, pc):
        return out_refs[t].at[4 * px + 2 * py + pc]

    def copy(t, k, block, to, own=False):
        return pltpu.make_async_remote_copy(
            src_ref=x_refs[t] if own else rows(t, *block), dst_ref=rows(t, *block),
            send_sem=send_sems.at[7 * t + k], recv_sem=recv_sems.at[7 * t + k], device_id=to, device_id_type=pl.DeviceIdType.MESH)

    def local(t):
        return pltpu.make_async_copy(x_refs[t], rows(t, *me), local_sems.at[t])

    def start():
        for t in range(n):
            local(t).start()
            copy(t, 0, me, sibling, own=True).start()
            for j, chip in enumerate(chips):
                copy(t, 1 + j, me, (*chip, cc), own=True).start()

    def finish():
        for j, chip in enumerate(chips):
            for t in range(n):
                copy(t, 1 + j, (*chip, cc), me).wait_recv()
                copy(t, 4 + j, (*chip, cc), sibling).start()
        for t in range(n):
            copy(t, 0, sibling, me).wait_recv()
            for j, chip in enumerate(chips):
                copy(t, 4 + j, (*chip, 1 - cc), me).wait_recv()
        for t in range(n):
            copy(t, 0, me, sibling, own=True).wait_send()
            for j, chip in enumerate(chips):
                copy(t, 1 + j, me, (*chip, cc), own=True).wait_send()
                copy(t, 4 + j, (*chip, cc), sibling).wait_send()
            local(t).wait()

    return start, finish


def _exchange_plan(x_refs, out_refs, send_sems, recv_sems, local_sems):
    n = len(x_refs)
    x, y, cc = lax.axis_index("x"), lax.axis_index("y"), lax.axis_index("c")
    me = 4 * x + 2 * y + cc

    def copies():
        res = [pltpu.make_async_copy(x_refs[t].at[me], out_refs[t].at[me], local_sems.at[t]) for t in range(n)]
        for k in range(1, NDEV):
            px = 1 - x if k & 4 else x
            py = 1 - y if k & 2 else y
            pc = 1 - cc if k & 1 else cc
            peer = 4 * px + 2 * py + pc
            for t in range(n):
                res.append(pltpu.make_async_remote_copy(
                    src_ref=x_refs[t].at[peer], dst_ref=out_refs[t].at[me], send_sem=send_sems.at[7 * t + k - 1],
                    recv_sem=recv_sems.at[7 * t + k - 1], device_id=(px, py, pc), device_id_type=pl.DeviceIdType.MESH))
        return res

    def start():
        for cp in copies():
            cp.start()

    def finish():
        for cp in copies():
            cp.wait()

    return start, finish


def _comm_scratch(n):
    return [pltpu.SemaphoreType.DMA((7 * n,)), pltpu.SemaphoreType.DMA((7 * n,)), pltpu.SemaphoreType.DMA((n,))]


ANY = pl.BlockSpec(memory_space=pl.ANY)


def _all_gather(xs, name):
    n = len(xs)

    def body(*refs):
        start, finish = _gather_plan(refs[:n], refs[n:2 * n], *refs[2 * n:])
        start()
        finish()

    return pl.pallas_call(
        body, name=name, out_shape=[jax.ShapeDtypeStruct((NDEV,) + a.shape, a.dtype) for a in xs],
        in_specs=[ANY] * n, out_specs=[ANY] * n, scratch_shapes=_comm_scratch(n),
    )(*xs)


def _all_to_all(xs, name):
    n = len(xs)

    def body(*refs):
        start, finish = _exchange_plan(refs[:n], refs[n:2 * n], *refs[2 * n:])
        start()
        finish()

    return pl.pallas_call(
        body, name=name, out_shape=[jax.ShapeDtypeStruct(a.shape, a.dtype) for a in xs],
        in_specs=[ANY] * n, out_specs=[ANY] * n, scratch_shapes=_comm_scratch(n),
    )(*xs)


def _mm(name, a, b, out_shape, grid, a_spec, b_spec, o_spec, dims, acc_shape):
    nk = grid[2]

    def body(a_ref, b_ref, o_ref, acc_ref):
        k = pl.program_id(2)

        @pl.when(k == 0)
        def _():
            acc_ref[...] = jnp.zeros_like(acc_ref)

        acc_ref[...] += _dot(a_ref[...], b_ref[...], dims)

        @pl.when(k == nk - 1)
        def _():
            o_ref[...] = acc_ref[...].astype(o_ref.dtype)

    return pl.pallas_call(
        body, name=name, grid=grid, out_shape=out_shape, in_specs=[a_spec, b_spec], out_specs=o_spec,
        scratch_shapes=[pltpu.VMEM(acc_shape, F32)], compiler_params=_cp("parallel", "parallel", "arbitrary"),
    )(a, b)


def _mm_rows(name, a, w, out_dtype, dims):
    s, k = a.shape
    n = w.shape[1] if dims == NN else w.shape[0]
    tm = min(ROW_TILE, s)
    return _mm(name, a, w, jax.ShapeDtypeStruct((s, n), out_dtype), (s // tm, 1, 1),
               pl.BlockSpec((tm, k), lambda i, j, kk: (i, 0)), pl.BlockSpec(w.shape, lambda i, j, kk: (0, 0)),
               pl.BlockSpec((tm, n), lambda i, j, kk: (i, 0)), dims, (tm, n))


def _mm_wgrad(name, a, g, out_dtype):
    s, m = a.shape
    n = g.shape[1]
    tk = min(ROW_TILE, s)
    return _mm(name, a, g, jax.ShapeDtypeStruct((m, n), out_dtype), (1, 1, s // tk),
               pl.BlockSpec((tk, m), lambda i, j, kk: (kk, 0)), pl.BlockSpec((tk, n), lambda i, j, kk: (kk, 0)),
               pl.BlockSpec((m, n), lambda i, j, kk: (0, 0)), TN, (m, n))


def _mm_up(h2, wup):
    s = h2.shape[0]
    tm = min(ROW_TILE, s)
    return _mm("ffn_up", h2, wup, jax.ShapeDtypeStruct((NDEV, s, FB), BF16), (NDEV, s // tm, 1),
               pl.BlockSpec((tm, D), lambda j, i, kk: (i, 0)), pl.BlockSpec((None, D, FB), lambda j, i, kk: (j, 0, 0)),
               pl.BlockSpec((None, tm, FB), lambda j, i, kk: (j, i, 0)), NN, (tm, FB))


def _mm_down(act, wdown):
    s = act.shape[1]
    tm = min(ROW_TILE, s)
    return _mm("ffn_down", act, wdown, jax.ShapeDtypeStruct((s, D), F32), (s // tm, 1, 4),
               pl.BlockSpec((None, tm, FB), lambda i, j, kk: (kk, i, 0)), pl.BlockSpec((None, FB, D), lambda i, j, kk: (kk, 0, 0)),
               pl.BlockSpec((tm, D), lambda i, j, kk: (i, 0)), NN, (tm, D))


def _mm_dact(df, wdown):
    s = df.shape[0]
    tm = min(ROW_TILE, s)
    return _mm("ffn_dact", df, wdown, jax.ShapeDtypeStruct((4, s, FB), BF16), (4, s // tm, 1),
               pl.BlockSpec((tm, D), lambda j, i, kk: (i, 0)), pl.BlockSpec((None, FB, D), lambda j, i, kk: (j, 0, 0)),
               pl.BlockSpec((None, tm, FB), lambda j, i, kk: (j, i, 0)), NT, (tm, FB))


def _mm_dwdown(act, df):
    s = df.shape[0]
    tk = min(ROW_TILE, s)
    return _mm("ffn_dwdown", act, df, jax.ShapeDtypeStruct((4, FB, D), BF16), (4, 1, s // tk),
               pl.BlockSpec((None, tk, FB), lambda j, i, kk: (j, kk, 0)), pl.BlockSpec((tk, D), lambda j, i, kk: (kk, 0)),
               pl.BlockSpec((None, FB, D), lambda j, i, kk: (j, 0, 0)), TN, (FB, D))


def _mm_dh2(dupre, wup):
    s = dupre.shape[1]
    tm = min(ROW_TILE, s)
    return _mm("ffn_dh2", dupre, wup, jax.ShapeDtypeStruct((s, D), BF16), (s // tm, 1, NDEV),
               pl.BlockSpec((None, tm, FB), lambda i, j, kk: (kk, i, 0)), pl.BlockSpec((None, D, FB), lambda i, j, kk: (kk, 0, 0)),
               pl.BlockSpec((tm, D), lambda i, j, kk: (i, 0)), NT, (tm, D))


def _mm_dwup(h2, dupre):
    s = h2.shape[0]
    tk = min(ROW_TILE, s)
    return _mm("ffn_dwup", h2, dupre, jax.ShapeDtypeStruct((NDEV, D, FB), BF16), (NDEV, 1, s // tk),
               pl.BlockSpec((tk, D), lambda j, i, kk: (kk, 0)), pl.BlockSpec((None, tk, FB), lambda j, i, kk: (j, kk, 0)),
               pl.BlockSpec((None, D, FB), lambda j, i, kk: (j, 0, 0)), TN, (D, FB))


def _rms(xv, w, out_dtype, name):
    s, d = xv.shape
    ts = min(ROW_TILE, s)

    def body(x_ref, w_ref, o_ref):
        xf = x_ref[...].astype(F32)
        r = lax.rsqrt(jnp.mean(xf * xf, axis=-1, keepdims=True) + EPS)
        o_ref[...] = (xf * r * w_ref[...]).astype(o_ref.dtype)

    return pl.pallas_call(
        body, name=name, grid=(s // ts,), out_shape=jax.ShapeDtypeStruct((s, d), out_dtype),
        in_specs=[pl.BlockSpec((ts, d), lambda i: (i, 0)), pl.BlockSpec((1, d), lambda i: (0, 0))],
        out_specs=pl.BlockSpec((ts, d), lambda i: (i, 0)), compiler_params=_cp("parallel"),
    )(xv, w)


def _add_rms(xv, mv, w, name):
    s, d = xv.shape
    ts = min(ROW_TILE, s)

    def body(x_ref, m_ref, w_ref, o_ref):
        mf = m_ref[...].astype(F32)
        r = lax.rsqrt(jnp.mean(mf * mf, axis=-1, keepdims=True) + EPS)
        o_ref[...] = x_ref[...] + mf * r * w_ref[...]

    return pl.pallas_call(
        body, name=name, grid=(s // ts,), out_shape=jax.ShapeDtypeStruct((s, d), F32),
        in_specs=[pl.BlockSpec((ts, d), lambda i: (i, 0)), pl.BlockSpec((ts, d), lambda i: (i, 0)), pl.BlockSpec((1, d), lambda i: (0, 0))],
        out_specs=pl.BlockSpec((ts, d), lambda i: (i, 0)), compiler_params=_cp("parallel"),
    )(xv, mv, w)


def _rms_bwd_math(xf, w, dy):
    r = lax.rsqrt(jnp.mean(xf * xf, axis=-1, keepdims=True) + EPS)
    xh = xf * r
    dxh = dy * w
    dx = r * (dxh - xh * jnp.mean(dxh * xh, axis=-1, keepdims=True))
    return dx, dy * xh


def _rms_bwd(xv, w, dy, dres, out_dtype, name):
    s, d = xv.shape
    ts = min(ROW_TILE, s)
    with_res = dres is not None

    def body(*refs):
        if with_res:
            x_ref, w_ref, dy_ref, dres_ref, dx_ref, dw_ref = refs
        else:
            x_ref, w_ref, dy_ref, dx_ref, dw_ref = refs
        dx, dwt = _rms_bwd_math(x_ref[...].astype(F32), w_ref[...], dy_ref[...].astype(F32))
        if with_res:
            dx = dx + dres_ref[...]
        dx_ref[...] = dx.astype(dx_ref.dtype)

        @pl.when(pl.program_id(0) == 0)
        def _():
            dw_ref[...] = jnp.zeros_like(dw_ref)

        dw_ref[...] += _rowsum(dwt)

    row = pl.BlockSpec((ts, d), lambda i: (i, 0))
    vec = pl.BlockSpec((1, d), lambda i: (0, 0))
    ins = [xv, w, dy] + ([dres] if with_res else [])
    return pl.pallas_call(
        body, name=name, grid=(s // ts,),
        out_shape=(jax.ShapeDtypeStruct((s, d), out_dtype), jax.ShapeDtypeStruct((1, d), F32)),
        in_specs=[row, vec, row] + ([row] if with_res else []), out_specs=(row, vec), compiler_params=_cp("arbitrary"),
    )(*ins)


def _loss_head(yv, tv):
    s, d = yv.shape
    ts = min(ROW_TILE, s)

    def body(y_ref, t_ref, l_ref, dy_ref):
        e = y_ref[...] - t_ref[...]
        dy_ref[...] = e * (1.0 / d)

        @pl.when(pl.program_id(0) == 0)
        def _():
            l_ref[...] = jnp.zeros_like(l_ref)

        tot = jnp.sum(jnp.sum(e * e, axis=1, keepdims=True), axis=0, keepdims=True)
        l_ref[...] += jnp.broadcast_to(tot * (0.5 / d), (8, LANE))

    row = pl.BlockSpec((ts, d), lambda i: (i, 0))
    return pl.pallas_call(
        body, name="loss_head", grid=(s // ts,),
        out_shape=(jax.ShapeDtypeStruct((8, LANE), F32), jax.ShapeDtypeStruct((s, d), F32)),
        in_specs=[row, row], out_specs=(pl.BlockSpec((8, LANE), lambda i: (0, 0)), row), compiler_params=_cp("arbitrary"),
    )(yv, tv)


def _rope(v, c, a, b):
    return v * c + pltpu.roll(v, LANE - 16, 1) * a + pltpu.roll(v, 16, 1) * b


def _rope_t(dv, c, a, b):
    return dv * c + pltpu.roll(dv * a, 16, 1) + pltpu.roll(dv * b, LANE - 16, 1)


def _mla_prep(proj, tabs, qnw, kvnw, wq, wkv):
    s = proj.shape[0]
    ts = min(ROW_TILE, s)
    tc, ta, tb = tabs

    def body(cq_ref, ckv_ref, kr_ref, c_ref, a_ref, b_ref, qnw_ref, kvnw_ref, wq_ref, wkv_ref, q_ref, k_ref, kv_ref):
        c, a, b = c_ref[...], a_ref[...], b_ref[...]
        cq = cq_ref[...].astype(F32)
        qn = (cq * lax.rsqrt(jnp.mean(cq * cq, axis=-1, keepdims=True) + EPS) * qnw_ref[...]).astype(BF16)
        ckv = ckv_ref[...].astype(F32)
        kvn = (ckv * lax.rsqrt(jnp.mean(ckv * ckv, axis=-1, keepdims=True) + EPS) * kvnw_ref[...]).astype(BF16)
        kr = _rope(kr_ref[...].astype(F32), c, a, b)
        lane = lax.broadcasted_iota(jnp.int32, (ts, LANE), 1)
        for h in range(HEADS):
            q_ref[h] = _rope(_dot(qn, wq_ref[h]), c, a, b).astype(BF16)
            kv = _dot(kvn, wkv_ref[h])
            kv_ref[h] = kv.astype(BF16)
            k_ref[h] = jnp.where(lane < NOPE, kv, kr).astype(BF16)

    tab = pl.BlockSpec((ts, LANE), lambda i: (i, 0))
    hd = pl.BlockSpec((HEADS, ts, LANE), lambda i: (0, i, 0))
    out = jax.ShapeDtypeStruct((HEADS, s, LANE), BF16)
    return pl.pallas_call(
        body, name="mla_prep", grid=(s // ts,), out_shape=(out, out, out),
        in_specs=[pl.BlockSpec((ts, QL), lambda i: (i, 0)), pl.BlockSpec((ts, LANE), lambda i: (i, 2)),
                  pl.BlockSpec((ts, LANE), lambda i: (i, 3)), tab, tab, tab,
                  pl.BlockSpec((1, QL), lambda i: (0, 0)), pl.BlockSpec((1, KVL), lambda i: (0, 0)),
                  pl.BlockSpec((HEADS, QL, LANE), lambda i: (0, 0, 0)), pl.BlockSpec((HEADS, KVL, LANE), lambda i: (0, 0, 0))],
        out_specs=(hd, hd, hd), compiler_params=_cp("parallel"),
    )(proj, proj, proj, tc, ta, tb, qnw, kvnw, wq, wkv)


def _mla_prep_bwd(proj, tabs, qnw, kvnw, wq, wkv, dq, dk, dv):
    s = proj.shape[0]
    ts = min(ROW_TILE, s)
    tc, ta, tb = tabs

    def body(cq_ref, ckv_ref, c_ref, a_ref, b_ref, qnw_ref, kvnw_ref, wq_ref, wkv_ref, dq_ref, dk_ref, dv_ref,
             dcq_ref, dckv_ref, dkr_ref, dwq_ref, dwkv_ref, dqnw_ref, dkvnw_ref):
        @pl.when(pl.program_id(0) == 0)
        def _():
            dwq_ref[...] = jnp.zeros_like(dwq_ref)
            dwkv_ref[...] = jnp.zeros_like(dwkv_ref)
            dqnw_ref[...] = jnp.zeros_like(dqnw_ref)
            dkvnw_ref[...] = jnp.zeros_like(dkvnw_ref)

        c, a, b = c_ref[...], a_ref[...], b_ref[...]
        cq = cq_ref[...].astype(F32)
        qn = (cq * lax.rsqrt(jnp.mean(cq * cq, axis=-1, keepdims=True) + EPS) * qnw_ref[...]).astype(BF16)
        ckv = ckv_ref[...].astype(F32)
        kvn = (ckv * lax.rsqrt(jnp.mean(ckv * ckv, axis=-1, keepdims=True) + EPS) * kvnw_ref[...]).astype(BF16)
        lane = lax.broadcasted_iota(jnp.int32, (ts, LANE), 1)
        dqn = jnp.zeros((ts, QL), F32)
        dkvn = jnp.zeros((ts, KVL), F32)
        dkr = jnp.zeros((ts, LANE), F32)
        for h in range(HEADS):
            dqh = _rope_t(dq_ref[h], c, a, b).astype(BF16)
            dwq_ref[h] += _dot(qn, dqh, TN)
            dqn += _dot(dqh, wq_ref[h], NT)
            dkh = dk_ref[h].astype(F32)
            dkvh = jnp.where(lane < NOPE, dkh, dv_ref[h].astype(F32)).astype(BF16)
            dkr += jnp.where(lane < NOPE, 0.0, dkh)
            dwkv_ref[h] += _dot(kvn, dkvh, TN)
            dkvn += _dot(dkvh, wkv_ref[h], NT)
        dkr_ref[...] = _rope_t(dkr, c, a, b).astype(BF16)
        dcq, dwt = _rms_bwd_math(cq, qnw_ref[...], dqn)
        dcq_ref[...] = dcq.astype(BF16)
        dqnw_ref[...] += _rowsum(dwt)
        dckv, dwt = _rms_bwd_math(ckv, kvnw_ref[...], dkvn)
        dckv_ref[...] = dckv.astype(BF16)
        dkvnw_ref[...] += _rowsum(dwt)

    tab = pl.BlockSpec((ts, LANE), lambda i: (i, 0))
    hd = pl.BlockSpec((HEADS, ts, LANE), lambda i: (0, i, 0))
    wq_spec = pl.BlockSpec((HEADS, QL, LANE), lambda i: (0, 0, 0))
    wkv_spec = pl.BlockSpec((HEADS, KVL, LANE), lambda i: (0, 0, 0))
    return pl.pallas_call(
        body, name="mla_prep_bwd", grid=(s // ts,),
        out_shape=(jax.ShapeDtypeStruct((s, QL), BF16), jax.ShapeDtypeStruct((s, KVL), BF16), jax.ShapeDtypeStruct((s, LANE), BF16),
                   jax.ShapeDtypeStruct((HEADS, QL, LANE), F32), jax.ShapeDtypeStruct((HEADS, KVL, LANE), F32),
                   jax.ShapeDtypeStruct((1, QL), F32), jax.ShapeDtypeStruct((1, KVL), F32)),
        in_specs=[pl.BlockSpec((ts, QL), lambda i: (i, 0)), pl.BlockSpec((ts, LANE), lambda i: (i, 2)), tab, tab, tab,
                  pl.BlockSpec((1, QL), lambda i: (0, 0)), pl.BlockSpec((1, KVL), lambda i: (0, 0)), wq_spec, wkv_spec, hd, hd, hd],
        out_specs=(pl.BlockSpec((ts, QL), lambda i: (i, 0)), pl.BlockSpec((ts, KVL), lambda i: (i, 0)), tab, wq_spec, wkv_spec,
                   pl.BlockSpec((1, QL), lambda i: (0, 0)), pl.BlockSpec((1, KVL), lambda i: (0, 0))),
        compiler_params=_cp("arbitrary"),
    )(proj, proj, tc, ta, tb, qnw, kvnw, wq, wkv, dq, dk, dv)


def _transpose_bf16(v):
    return v.astype(F32).T.astype(BF16)


def _flash_fwd(q, k, kv, prefetch=None):
    s = q.shape[1]
    t = min(ATT_TILE, s)
    n = s // t
    g = FWD_HEADS
    nx = len(prefetch) if prefetch else 0

    def body(*refs):
        q_ref, k_ref, kv_ref = refs[:3]
        o_ref, lse_ref = refs[3 + nx:5 + nx]
        kvt_sc = refs[5 + 2 * nx]
        step = pl.program_id(0) * n + pl.program_id(1)
        if nx:
            start, finish = _gather_plan(refs[3:3 + nx], refs[5 + nx:5 + 2 * nx], *refs[6 + 2 * nx:])
            pl.when(step == 0)(start)
        attend(q_ref, k_ref, kv_ref, o_ref, lse_ref, kvt_sc)
        if nx:
            pl.when(step == (HEADS // g) * n - 1)(finish)

    def attend(q_ref, k_ref, kv_ref, o_ref, lse_ref, kvt_sc):
        i = pl.program_id(1)

        @pl.when(i == 0)
        def _():
            for hh in range(g):
                kvt_sc[hh] = _transpose_bf16(kv_ref[hh])

        qt = [_transpose_bf16(q_ref[hh]) for hh in range(g)]
        kpos = lax.broadcasted_iota(jnp.int32, (t, t), 0)
        qpos = lax.broadcasted_iota(jnp.int32, (t, t), 1)

        def chunk(j, carry, diagonal):
            start = pl.multiple_of(j * t, t)
            scs = [_dot(k_ref[hh, pl.ds(start, t), :], qt[hh]) for hh in range(g)]
            soft = []
            for hh in range(g):
                m, l, _ = carry[hh]
                sc = scs[hh] * ATT_SCALE
                if diagonal:
                    sc = jnp.where(qpos >= kpos, sc, NEG)
                m_new = jnp.maximum(m, jnp.max(sc, axis=0, keepdims=True))
                alpha = jnp.exp(m - m_new)
                p = jnp.exp(sc - m_new)
                soft.append((m_new, alpha * l + jnp.sum(p, axis=0, keepdims=True), alpha, p.astype(BF16)))
            pvs = [_dot(kvt_sc[hh, :, pl.ds(start, t)], soft[hh][3]) for hh in range(g)]
            return tuple((soft[hh][0], soft[hh][1], soft[hh][2] * carry[hh][2] + pvs[hh]) for hh in range(g))

        init = tuple((jnp.full((1, t), NEG, F32), jnp.zeros((1, t), F32), jnp.zeros((LANE, t), F32)) for _ in range(g))
        carry = lax.fori_loop(0, i, lambda j, c: chunk(j, c, False), init)
        carry = chunk(i, carry, True)
        for hh in range(g):
            m, l, acc = carry[hh]
            o_ref[:, hh * LANE:(hh + 1) * LANE] = (acc / l).T.astype(BF16)
            lse_ref[hh] = m + jnp.log(l)

    whole = pl.BlockSpec((g, s, LANE), lambda h, i: (h, 0, 0))
    res = pl.pallas_call(
        body, name="flash_fwd_gather" if nx else "flash_fwd", grid=(HEADS // g, n),
        out_shape=[jax.ShapeDtypeStruct((s, HEADS * LANE), BF16), jax.ShapeDtypeStruct((HEADS, 1, s), F32)]
        + [jax.ShapeDtypeStruct((NDEV,) + a.shape, a.dtype) for a in (prefetch or [])],
        in_specs=[pl.BlockSpec((g, t, LANE), lambda h, i: (h, i, 0)), whole, whole] + [ANY] * nx,
        out_specs=[pl.BlockSpec((t, g * LANE), lambda h, i: (i, h)), pl.BlockSpec((g, 1, t), lambda h, i: (h, 0, i))] + [ANY] * nx,
        scratch_shapes=[pltpu.VMEM((g, LANE, s), BF16)] + (_comm_scratch(nx) if nx else []),
        compiler_params=_cp("arbitrary", "arbitrary"),
    )(q, k, kv, *(prefetch or []))
    return res[0], res[1], list(res[2:])


def _flash_bwd(q, k, kv, cat, dcat, lse, pending=None):
    s = q.shape[1]
    t = min(ATT_TILE, s)
    n = s // t
    g = BWD_HEADS
    nx = len(pending) if pending else 0

    def body(*refs):
        ins, outs, scr = refs[:6], refs[6 + nx:9 + nx], refs[9 + 2 * nx:13 + 2 * nx]
        step = pl.program_id(0) * n + pl.program_id(1)
        if nx:
            start, finish = _exchange_plan(refs[6:6 + nx], refs[9 + nx:9 + 2 * nx], *refs[13 + 2 * nx:])
            pl.when(step == 0)(start)
        attend(*ins, *outs, *scr)
        if nx:
            pl.when(step == (HEADS // g) * n - 1)(finish)

    def attend(q_ref, k_ref, kv_ref, o_ref, do_ref, lse_ref, dq_ref, dk_ref, dv_ref, qt_sc, dot_sc, delta_sc, dqt_sc):
        j = pl.program_id(1)

        @pl.when(j == 0)
        def _():
            for hh in range(g):
                lanes = slice(hh * LANE, (hh + 1) * LANE)
                qt_sc[hh] = _transpose_bf16(q_ref[hh])
                dof = do_ref[:, lanes].astype(F32)
                dot_sc[hh] = dof.T.astype(BF16)
                delta_sc[hh] = _dot(jnp.ones((8, LANE), F32), dof * o_ref[:, lanes].astype(F32), NT, precision=HI)
            dqt_sc[...] = jnp.zeros_like(dqt_sc)

        kjt = [_transpose_bf16(k_ref[hh]) for hh in range(g)]
        kpos = lax.broadcasted_iota(jnp.int32, (t, t), 0)
        qpos = lax.broadcasted_iota(jnp.int32, (t, t), 1)

        def chunk(i, carry, diagonal):
            start = pl.multiple_of(i * t, t)
            cols = pl.ds(start, t)
            scs = [_dot(k_ref[hh], qt_sc[hh, :, cols]) for hh in range(g)]
            dps = [_dot(kv_ref[hh], dot_sc[hh, :, cols]) for hh in range(g)]
            pds = []
            for hh in range(g):
                p = jnp.exp(scs[hh] * ATT_SCALE - lse_ref[hh, :, cols])
                if diagonal:
                    p = jnp.where(qpos >= kpos, p, 0.0)
                ds = (p * (dps[hh] - delta_sc[hh, 0:1, cols]) * ATT_SCALE).astype(BF16)
                pds.append((p.astype(BF16), ds))
            out = []
            for hh in range(g):
                dk, dv = carry[hh]
                dv = dv + _dot(pds[hh][0], do_ref[pl.ds(start, t), hh * LANE:(hh + 1) * LANE])
                dk = dk + _dot(pds[hh][1], q_ref[hh, pl.ds(start, t), :])
                dqt_sc[hh, :, cols] += _dot(kjt[hh], pds[hh][1])
                out.append((dk, dv))
            return tuple(out)

        zero = jnp.zeros((t, LANE), F32)
        carry = chunk(j, tuple((zero, zero) for _ in range(g)), True)
        carry = lax.fori_loop(j + 1, n, lambda i, c: chunk(i, c, False), carry)
        for hh in range(g):
            dk_ref[hh] = carry[hh][0].astype(BF16)
            dv_ref[hh] = carry[hh][1].astype(BF16)

        @pl.when(j == n - 1)
        def _():
            for hh in range(g):
                dq_ref[hh] = dqt_sc[hh].T

    whole = pl.BlockSpec((g, s, LANE), lambda h, j: (h, 0, 0))
    kspec = pl.BlockSpec((g, t, LANE), lambda h, j: (h, j, 0))
    ospec = pl.BlockSpec((s, g * LANE), lambda h, j: (0, h))
    res = pl.pallas_call(
        body, name="flash_bwd_exchange" if nx else "flash_bwd", grid=(HEADS // g, n),
        out_shape=[jax.ShapeDtypeStruct((HEADS, s, LANE), F32), jax.ShapeDtypeStruct((HEADS, s, LANE), BF16),
                   jax.ShapeDtypeStruct((HEADS, s, LANE), BF16)] + [jax.ShapeDtypeStruct(a.shape, a.dtype) for a in (pending or [])],
        in_specs=[whole, kspec, kspec, ospec, ospec, pl.BlockSpec((g, 1, s), lambda h, j: (h, 0, 0))] + [ANY] * nx,
        out_specs=[whole, kspec, kspec] + [ANY] * nx,
        scratch_shapes=[pltpu.VMEM((g, LANE, s), BF16), pltpu.VMEM((g, LANE, s), BF16), pltpu.VMEM((g, 8, s), F32),
                        pltpu.VMEM((g, LANE, s), F32)] + (_comm_scratch(nx) if nx else []),
        compiler_params=_cp("arbitrary", "arbitrary"),
    )(q, k, kv, cat, dcat, lse, *(pending or []))
    return res[0], res[1], res[2], list(res[3:])


def _conv3(ext, w_ref, ts):
    return (w_ref[0:1, :] * ext[pl.ds(HALO - 2, ts), :] + w_ref[1:2, :] * ext[pl.ds(HALO - 1, ts), :]
            + w_ref[2:3, :] * ext[pl.ds(HALO, ts), :])


def _conv3_t(ext2, w_ref, ts):
    return (w_ref[0:1, :] * ext2[pl.ds(2, ts), :] + w_ref[1:2, :] * ext2[pl.ds(1, ts), :] + w_ref[2:3, :] * ext2[pl.ds(0, ts), :])


def _sconv_fwd(proj, w):
    s = proj.shape[0]
    ts = min(ROW_TILE, s)

    def body(b_ref, c_ref, h_ref, hc_ref, hh_ref, w_ref, o_ref, ext):
        i = pl.program_id(0)
        ext[0:HALO, :] = hc_ref[...].astype(F32) * hh_ref[...].astype(F32) * (i > 0).astype(F32)
        ext[HALO:HALO + ts, :] = c_ref[...].astype(F32) * h_ref[...].astype(F32)
        o_ref[...] = (b_ref[...].astype(F32) * _conv3(ext, w_ref, ts)).astype(BF16)

    def col(cb):
        return pl.BlockSpec((ts, SC), lambda i: (i, cb))

    def halo(cb):
        return pl.BlockSpec((HALO, SC), lambda i: (_prev_halo(i, ts), cb))

    return pl.pallas_call(
        body, name="sconv_fwd", grid=(s // ts,), out_shape=jax.ShapeDtypeStruct((s, SC), BF16),
        in_specs=[col(2), col(3), col(4), halo(3), halo(4), pl.BlockSpec((3, SC), lambda i: (0, 0))],
        out_specs=pl.BlockSpec((ts, SC), lambda i: (i, 0)), scratch_shapes=[pltpu.VMEM((ts + HALO, SC), F32)],
        compiler_params=_cp("parallel"),
    )(proj, proj, proj, proj, proj, w)


def _sconv_bwd(proj, dcat, w):
    s = proj.shape[0]
    ts = min(ROW_TILE, s)
    n = s // ts

    def body(b_ref, c_ref, h_ref, hc_ref, hh_ref, dy_ref, ndy_ref, nb_ref, w_ref, db_ref, dc_ref, dh_ref, dw_ref, ext, ext2):
        i = pl.program_id(0)

        @pl.when(i == 0)
        def _():
            dw_ref[...] = jnp.zeros_like(dw_ref)

        cv, hv, bv = c_ref[...].astype(F32), h_ref[...].astype(F32), b_ref[...].astype(F32)
        ext[0:HALO, :] = hc_ref[...].astype(F32) * hh_ref[...].astype(F32) * (i > 0).astype(F32)
        ext[HALO:HALO + ts, :] = cv * hv
        dy = dy_ref[...].astype(F32)
        db_ref[...] = (dy * _conv3(ext, w_ref, ts)).astype(BF16)
        dyb = dy * bv
        ext2[0:ts, :] = dyb
        ext2[ts:ts + HALO, :] = ndy_ref[...].astype(F32) * nb_ref[...].astype(F32) * (i < n - 1).astype(F32)
        dg = _conv3_t(ext2, w_ref, ts)
        dc_ref[...] = (dg * hv).astype(BF16)
        dh_ref[...] = (dg * cv).astype(BF16)
        for kk in range(3):
            dw_ref[kk:kk + 1, :] += _rowsum(dyb * ext[pl.ds(HALO - 2 + kk, ts), :])

    def col(cb):
        return pl.BlockSpec((ts, SC), lambda i: (i, cb))

    def halo(cb):
        return pl.BlockSpec((HALO, SC), lambda i: (_prev_halo(i, ts), cb))

    def nxt(cb):
        return pl.BlockSpec((HALO, SC), lambda i: (_next_halo(i, ts, n), cb))

    out = jax.ShapeDtypeStruct((s, SC), BF16)
    o0 = pl.BlockSpec((ts, SC), lambda i: (i, 0))
    return pl.pallas_call(
        body, name="sconv_bwd", grid=(n,), out_shape=(out, out, out, jax.ShapeDtypeStruct((3, SC), F32)),
        in_specs=[col(2), col(3), col(4), halo(3), halo(4), col(4), nxt(4), nxt(2), pl.BlockSpec((3, SC), lambda i: (0, 0))],
        out_specs=(o0, o0, o0, pl.BlockSpec((3, SC), lambda i: (0, 0))),
        scratch_shapes=[pltpu.VMEM((ts + HALO, SC), F32), pltpu.VMEM((ts + HALO, SC), F32)], compiler_params=_cp("arbitrary"),
    )(proj, proj, proj, proj, proj, dcat, dcat, proj, w)


def _ffn_stage(ext, u_ref, halo_ref, i, ts):
    ext[0:HALO, :] = halo_ref[...].astype(F32) * (i > 0).astype(F32)
    ext[HALO:HALO + ts, :] = u_ref[...].astype(F32)


def _ffn_specs(ts):
    cur = pl.BlockSpec((2, None, ts, FB), lambda j, i: (0, j, i, 0))
    halo = pl.BlockSpec((2, None, HALO, FB), lambda j, i: (0, j, _prev_halo(i, ts), 0))
    w = pl.BlockSpec((2, None, 3, FB), lambda j, i: (0, j, 0, 0))
    b = pl.BlockSpec((2, None, 1, FB), lambda j, i: (0, j, 0, 0))
    return cur, halo, w, b


def _ffn_act(upre, fcw, fcb):
    s = upre.shape[1]
    ts = min(ROW_TILE, s)

    def body(u_ref, halo_ref, w_ref, b_ref, o_ref, ext):
        i = pl.program_id(1)
        _ffn_stage(ext, u_ref.at[0], halo_ref.at[0], i, ts)
        gate = b_ref[0] + _conv3(ext, w_ref.at[0], ts)
        _ffn_stage(ext, u_ref.at[1], halo_ref.at[1], i, ts)
        up = b_ref[1] + _conv3(ext, w_ref.at[1], ts)
        o_ref[...] = (gate * _sig(gate) * up).astype(BF16)

    cur, halo, w, b = _ffn_specs(ts)
    u4 = upre.reshape(2, 4, s, FB)
    return pl.pallas_call(
        body, name="ffn_act", grid=(4, s // ts), out_shape=jax.ShapeDtypeStruct((4, s, FB), BF16),
        in_specs=[cur, halo, w, b], out_specs=pl.BlockSpec((None, ts, FB), lambda j, i: (j, i, 0)),
        scratch_shapes=[pltpu.VMEM((ts + HALO, FB), F32)], compiler_params=_cp("parallel", "parallel"),
    )(u4, u4, fcw.reshape(2, 4, 3, FB), fcb.reshape(2, 4, 1, FB))


def _ffn_bwd_a(upre, dact, fcw, fcb):
    s = upre.shape[1]
    ts = min(ROW_TILE, s)

    def body(u_ref, halo_ref, w_ref, b_ref, da_ref, du_ref, db_ref, ext):
        i = pl.program_id(1)

        @pl.when(i == 0)
        def _():
            db_ref[...] = jnp.zeros_like(db_ref)

        _ffn_stage(ext, u_ref.at[0], halo_ref.at[0], i, ts)
        gate = b_ref[0] + _conv3(ext, w_ref.at[0], ts)
        _ffn_stage(ext, u_ref.at[1], halo_ref.at[1], i, ts)
        up = b_ref[1] + _conv3(ext, w_ref.at[1], ts)
        sg = _sig(gate)
        da = da_ref[...].astype(F32)
        dgate = da * up * sg * (1.0 + gate * (1.0 - sg))
        dup = da * gate * sg
        du_ref[0] = dgate.astype(BF16)
        du_ref[1] = dup.astype(BF16)
        db_ref[0] += _rowsum(dgate)
        db_ref[1] += _rowsum(dup)

    cur, halo, w, b = _ffn_specs(ts)
    u4 = upre.reshape(2, 4, s, FB)
    du, db = pl.pallas_call(
        body, name="ffn_bwd_a", grid=(4, s // ts),
        out_shape=(jax.ShapeDtypeStruct((2, 4, s, FB), BF16), jax.ShapeDtypeStruct((2, 4, 1, FB), F32)),
        in_specs=[cur, halo, w, b, pl.BlockSpec((None, ts, FB), lambda j, i: (j, i, 0))], out_specs=(cur, b),
        scratch_shapes=[pltpu.VMEM((ts + HALO, FB), F32)], compiler_params=_cp("parallel", "arbitrary"),
    )(u4, u4, fcw.reshape(2, 4, 3, FB), fcb.reshape(2, 4, 1, FB), dact)
    return du.reshape(NDEV, s, FB), db.reshape(NDEV, 1, FB)


def _ffn_bwd_b(du, upre, fcw):
    s = upre.shape[1]
    ts = min(ROW_TILE, s)
    n = s // ts

    def body(du_ref, ndu_ref, u_ref, halo_ref, w_ref, dup_ref, dw_ref, ext, ext2):
        i = pl.program_id(1)

        @pl.when(i == 0)
        def _():
            dw_ref[...] = jnp.zeros_like(dw_ref)

        duv = du_ref[...].astype(F32)
        ext2[0:ts, :] = duv
        ext2[ts:ts + HALO, :] = ndu_ref[...].astype(F32) * (i < n - 1).astype(F32)
        dup_ref[...] = _conv3_t(ext2, w_ref, ts).astype(BF16)
        _ffn_stage(ext, u_ref, halo_ref, i, ts)
        for kk in range(3):
            dw_ref[kk:kk + 1, :] += _rowsum(duv * ext[pl.ds(HALO - 2 + kk, ts), :])

    cur = pl.BlockSpec((None, ts, FB), lambda j, i: (j, i, 0))
    w = pl.BlockSpec((None, 3, FB), lambda j, i: (j, 0, 0))
    return pl.pallas_call(
        body, name="ffn_bwd_b", grid=(NDEV, n),
        out_shape=(jax.ShapeDtypeStruct((NDEV, s, FB), BF16), jax.ShapeDtypeStruct((NDEV, 3, FB), F32)),
        in_specs=[cur, pl.BlockSpec((None, HALO, FB), lambda j, i: (j, _next_halo(i, ts, n), 0)), cur,
                  pl.BlockSpec((None, HALO, FB), lambda j, i: (j, _prev_halo(i, ts), 0)), w],
        out_specs=(cur, w), scratch_shapes=[pltpu.VMEM((ts + HALO, FB), F32), pltpu.VMEM((ts + HALO, FB), F32)],
        compiler_params=_cp("parallel", "arbitrary"),
    )(du, du, upre, upre, fcw)


def _softplus(v):
    e = jnp.exp(-jnp.abs(v))
    return jnp.maximum(v, 0.0) + jnp.where(e < 1e-4, e * (1.0 - 0.5 * e), jnp.log(1.0 + e))


def _ssd_consts():
    L = SSD_L
    r = lax.broadcasted_iota(jnp.int32, (L, L), 0)
    c = lax.broadcasted_iota(jnp.int32, (L, L), 1)
    tri = r >= c
    er = lax.broadcasted_iota(jnp.int32, (LANE, SSD_DIM), 0)
    ec = lax.broadcasted_iota(jnp.int32, (LANE, SSD_DIM), 1)
    expand = ((ec >= er * 64) & (ec < er * 64 + 64)).astype(F32)
    return tri, expand


def _ssd_conv4(ext, cw_ref, cb_ref):
    L = SSD_L
    pre = cb_ref[...] + cw_ref[0:1, :] * ext[pl.ds(HALO - 3, L), :]
    for kk in range(1, 4):
        pre = pre + cw_ref[kk:kk + 1, :] * ext[pl.ds(HALO - 3 + kk, L), :]
    return pre


def _ssd_common(xbc_ref, halo_ref, dt_ref, cw_ref, cb_ref, dtb_ref, alog_ref, ext, first):
    L = SSD_L
    tri, expand = _ssd_consts()
    ext[0:HALO, :] = halo_ref[...].astype(F32) * (1.0 - first.astype(F32))
    ext[HALO:HALO + L, :] = xbc_ref[...].astype(F32)
    pre = _ssd_conv4(ext, cw_ref, cb_ref)
    sg = _sig(pre)
    act = pre * sg
    lane = lax.broadcasted_iota(jnp.int32, (1, LANE), 1)
    m4 = lane < SSD_H
    raw = dt_ref[...].astype(F32) + dtb_ref[...]
    dtv = jnp.where(m4, _softplus(raw), 0.0)
    av = jnp.where(m4, -jnp.exp(alog_ref[...]), 0.0)
    adt = dtv * av
    acs = _dot(tri.astype(F32), adt, precision=HI)
    acs_b = _dot(acs, expand, precision=HI)
    dt_b = _dot(dtv, expand, precision=HI)
    return dict(tri=tri, expand=expand, pre=pre, sg=sg, act=act, raw=raw, dtv=dtv, av=av, m4=m4, acs=acs, acs_b=acs_b,
                dt_b=dt_b, lane=lane)


def _head_terms(cm, h):
    L = SSD_L
    acs, tri = cm["acs"], cm["tri"]
    lane_l = lax.broadcasted_iota(jnp.int32, (L, LANE), 1)
    sub_l = lax.broadcasted_iota(jnp.int32, (LANE, L), 0)
    col = jnp.sum(jnp.where(lane_l == h, acs, 0.0), axis=1, keepdims=True)
    row = jnp.sum(jnp.where(sub_l == h, acs.T, 0.0), axis=0, keepdims=True)
    dec = jnp.where(tri, jnp.exp(jnp.where(tri, col - row, NEG)), 0.0)
    rowi = lax.broadcasted_iota(jnp.int32, (L, 1), 0)
    last = jnp.sum(jnp.where(rowi == L - 1, col, 0.0), axis=0, keepdims=True)
    dte = jnp.exp(last - col)
    return col, dec, last, dte


def _ssd_fwd(proj, cw, cb, dtb, alog, dvec, nw):
    s = proj.shape[0]
    L = SSD_L
    nc = s // L

    def body(z_ref, xbc_ref, halo_ref, dt_ref, cw_ref, cb_ref, dtb_ref, alog_ref, d_ref, nw_ref, y_ref, ypre_ref, st_ref, ext, state):
        i = pl.program_id(0)

        @pl.when(i == 0)
        def _():
            state[...] = jnp.zeros_like(state)

        cm = _ssd_common(xbc_ref, halo_ref, dt_ref, cw_ref, cb_ref, dtb_ref, alog_ref, ext, i == 0)
        act = cm["act"]
        xs = act[:, 0:256]
        bm = (act[:, 256:384], act[:, 384:512])
        cmat = (act[:, 512:640].astype(BF16), act[:, 640:768].astype(BF16))
        xdt = xs * cm["dt_b"]
        prev = state[...]
        st_ref[...] = prev
        prev_bf = prev.astype(BF16)
        gm = [_dot(cmat[g], bm[g].astype(BF16), NT) for g in range(2)]
        lane2 = lax.broadcasted_iota(jnp.int32, (1, SSD_DIM), 1)
        rows2 = lax.broadcasted_iota(jnp.int32, (SSD_DIM, 1), 0)
        ydiag = jnp.zeros((L, SSD_DIM), F32)
        contrib = jnp.zeros((SSD_DIM, LANE), F32)
        cd_rows = jnp.zeros((SSD_DIM, 1), F32)
        for h in range(SSD_H):
            g = h // 2
            col, dec, last, dte = _head_terms(cm, h)
            mh = (lane2 >= 64 * h) & (lane2 < 64 * h + 64)
            xm = jnp.where(mh, xdt, 0.0).astype(BF16)
            ydiag += _dot((gm[g] * dec).astype(BF16), xm)
            contrib += _dot(xm, (bm[g] * dte).astype(BF16), TN)
            cd_rows += jnp.where((rows2 >= 64 * h) & (rows2 < 64 * h + 64), jnp.exp(last), 0.0)
        yo = jnp.where(lane2 < 128, _dot(cmat[0], prev_bf, NT), _dot(cmat[1], prev_bf, NT))
        y = ydiag + yo * jnp.exp(cm["acs_b"]) + xs * d_ref[...]
        state[...] = prev * cd_rows + contrib
        ypre_ref[...] = y
        zz = z_ref[...].astype(F32)
        gt = y * zz * _sig(zz)
        y_ref[...] = (gt * lax.rsqrt(jnp.mean(gt * gt, axis=-1, keepdims=True) + EPS) * nw_ref[...]).astype(BF16)

    def vec(w):
        return pl.BlockSpec((1, w), lambda i: (0, 0))

    return pl.pallas_call(
        body, name="ssd_fwd", grid=(nc,),
        out_shape=(jax.ShapeDtypeStruct((s, SSD_DIM), BF16), jax.ShapeDtypeStruct((s, SSD_DIM), F32),
                   jax.ShapeDtypeStruct((nc, SSD_DIM, LANE), F32)),
        in_specs=[pl.BlockSpec((L, SSD_DIM), lambda i: (i, 5)), pl.BlockSpec((L, SSD_CONV), lambda i: (i, 2)),
                  pl.BlockSpec((HALO, SSD_CONV), lambda i: (_prev_halo(i, L), 2)), pl.BlockSpec((L, LANE), lambda i: (i, 18)),
                  pl.BlockSpec((4, SSD_CONV), lambda i: (0, 0)), vec(SSD_CONV), vec(LANE), vec(LANE), vec(SSD_DIM), vec(SSD_DIM)],
        out_specs=(pl.BlockSpec((L, SSD_DIM), lambda i: (i, 0)), pl.BlockSpec((L, SSD_DIM), lambda i: (i, 0)),
                   pl.BlockSpec((None, SSD_DIM, LANE), lambda i: (i, 0, 0))),
        scratch_shapes=[pltpu.VMEM((L + HALO, SSD_CONV), F32), pltpu.VMEM((SSD_DIM, LANE), F32)], compiler_params=_cp("arbitrary"),
    )(proj, proj, proj, proj, cw, cb, dtb, alog, dvec, nw)


def _ssd_bwd(proj, dcat, ypre, states, cw, cb, dtb, alog, dvec, nw):
    s = proj.shape[0]
    L = SSD_L
    nc = s // L

    def body(z_ref, xbc_ref, halo_ref, dt_ref, dy_ref, ypre_ref, st_ref, cw_ref, cb_ref, dtb_ref, alog_ref, d_ref, nw_ref,
             dz_ref, dxbc_ref, ddt_ref, dcw_ref, dcb_ref, ddtb_ref, dalog_ref, dd_ref, dnw_ref, ext, ext2, carry, dstate, ddl):
        i = pl.program_id(0)
        r = nc - 1 - i

        @pl.when(i == 0)
        def _():
            for ref in (dcw_ref, dcb_ref, ddtb_ref, dalog_ref, dd_ref, dnw_ref, carry, dstate, ddl):
                ref[...] = jnp.zeros_like(ref)

        cm = _ssd_common(xbc_ref, halo_ref, dt_ref, cw_ref, cb_ref, dtb_ref, alog_ref, ext, r == 0)
        tri, expand, act = cm["tri"], cm["expand"], cm["act"]
        xs = act[:, 0:256]
        bm = (act[:, 256:384], act[:, 384:512])
        cmat = (act[:, 512:640], act[:, 640:768])
        bm_bf = [v.astype(BF16) for v in bm]
        cm_bf = [v.astype(BF16) for v in cmat]
        dt_b = cm["dt_b"]
        xdt = xs * dt_b
        xdt_bf = xdt.astype(BF16)
        ea_b = jnp.exp(cm["acs_b"])
        prev = st_ref[...]
        prev_bf = prev.astype(BF16)
        lane2 = lax.broadcasted_iota(jnp.int32, (1, SSD_DIM), 1)
        rows2 = lax.broadcasted_iota(jnp.int32, (SSD_DIM, 1), 0)
        lane_l = lax.broadcasted_iota(jnp.int32, (L, LANE), 1)
        rowi = lax.broadcasted_iota(jnp.int32, (L, 1), 0)

        y = ypre_ref[...]
        zz = z_ref[...].astype(F32)
        sz = _sig(zz)
        gt = y * zz * sz
        dgt, dwt = _rms_bwd_math(gt, nw_ref[...], dy_ref[...].astype(F32))
        dnw_ref[...] += _rowsum(dwt)
        dy = dgt * zz * sz
        dz_ref[...] = (dgt * y * sz * (1.0 + zz * (1.0 - sz))).astype(BF16)

        ddl[0:1, :] += _rowsum(dy * xs)
        dxs = dy * d_ref[...]

        yo = jnp.where(lane2 < 128, _dot(cm_bf[0], prev_bf, NT), _dot(cm_bf[1], prev_bf, NT))
        dacs_b = dy * yo * ea_b
        dyo = dy * ea_b
        dyo_g = (jnp.where(lane2 < 128, dyo, 0.0).astype(BF16), jnp.where(lane2 >= 128, dyo, 0.0).astype(BF16))
        dc = [_dot(dyo_g[g], prev_bf) for g in range(2)]
        dprev = _dot(dyo_g[0], cm_bf[0], TN) + _dot(dyo_g[1], cm_bf[1], TN)

        gm = [_dot(cm_bf[g], bm_bf[g], NT) for g in range(2)]
        dgm = [jnp.zeros((L, L), F32), jnp.zeros((L, L), F32)]
        db = [jnp.zeros((L, LANE), F32), jnp.zeros((L, LANE), F32)]
        dxdt = jnp.zeros((L, SSD_DIM), F32)
        dacs = jnp.zeros((L, LANE), F32)
        dlast = jnp.zeros((1, LANE), F32)
        cd_rows = jnp.zeros((SSD_DIM, 1), F32)
        dst = dstate[...]
        dst_bf = dst.astype(BF16)
        dsp = dst * prev
        ones = jnp.ones((L, LANE), F32)
        for h in range(SSD_H):
            g = h // 2
            col, dec, last, dte = _head_terms(cm, h)
            mh = (lane2 >= 64 * h) & (lane2 < 64 * h + 64)
            rh = (rows2 >= 64 * h) & (rows2 < 64 * h + 64)
            sc = gm[g] * dec
            xm = jnp.where(mh, xdt, 0.0).astype(BF16)
            dym = jnp.where(mh, dy, 0.0).astype(BF16)
            dsc = _dot(dym, xdt_bf, NT)
            dxdt += _dot(sc.astype(BF16), dym, TN)
            dgm[g] += dsc * dec
            dd = dsc * sc
            rs = jnp.sum(dd, axis=1, keepdims=True)
            cs = _dot(dd, ones, TN, precision=HI)
            dacs += jnp.where(lane_l == h, rs - cs, 0.0)
            bd = (bm[g] * dte).astype(BF16)
            dxdt += jnp.where(mh, _dot(bd, dst_bf, NT), 0.0)
            dbd = _dot(xm, dst_bf)
            db[g] += dbd * dte
            tt = jnp.sum(dbd * bm[g], axis=1, keepdims=True) * dte
            dacs += jnp.where(lane_l == h, -tt, 0.0)
            cdh = jnp.exp(last)
            dcd = jnp.sum(jnp.sum(jnp.where(rh, dsp, 0.0), axis=1, keepdims=True), axis=0, keepdims=True)
            dlast += jnp.where(cm["lane"] == h, jnp.sum(tt, axis=0, keepdims=True) + dcd * cdh, 0.0)
            cd_rows += jnp.where(rh, cdh, 0.0)
        dacs += jnp.where(rowi == L - 1, dlast, 0.0)
        dacs += _dot(dacs_b, expand, NT, precision=HI)
        dstate[...] = dprev + dst * cd_rows

        for g in range(2):
            dgb = dgm[g].astype(BF16)
            dc[g] += _dot(dgb, bm_bf[g])
            db[g] += _dot(dgb, cm_bf[g], TN)

        dadt = _dot(tri.astype(F32), dacs, TN, precision=HI)
        ddtv = dadt * cm["av"] + _dot(dxdt * xs, expand, NT, precision=HI)
        dalog_ref[...] += _rowsum(dadt * cm["dtv"]) * cm["av"]
        dxs += dxdt * dt_b
        draw = jnp.where(cm["m4"], ddtv * _sig(cm["raw"]), 0.0)
        ddtb_ref[...] += _rowsum(draw)
        ddt_ref[...] = draw.astype(BF16)

        dact = jnp.concatenate([dxs, db[0], db[1], dc[0], dc[1]], axis=1)
        sg, pre = cm["sg"], cm["pre"]
        dpre = dact * sg * (1.0 + pre * (1.0 - sg))
        dcb_ref[...] += _rowsum(dpre)
        for kk in range(4):
            dcw_ref[kk:kk + 1, :] += _rowsum(dpre * ext[pl.ds(HALO - 3 + kk, L), :])
        ext2[0:L, :] = dpre
        ext2[L:L + HALO, :] = carry[...]
        dx = cw_ref[3:4, :] * ext2[pl.ds(0, L), :]
        for kk in range(3):
            dx = dx + cw_ref[kk:kk + 1, :] * ext2[pl.ds(3 - kk, L), :]
        dxbc_ref[...] = dx.astype(BF16)
        carry[...] = dpre[0:HALO, :]

        @pl.when(i == nc - 1)
        def _():
            dd_ref[...] = _dot(ddl[...], expand, NT, precision=HI)

    def vec(w):
        return pl.BlockSpec((1, w), lambda i: (0, 0))

    def rv(i):
        return nc - 1 - i

    return pl.pallas_call(
        body, name="ssd_bwd", grid=(nc,),
        out_shape=(jax.ShapeDtypeStruct((s, SSD_DIM), BF16), jax.ShapeDtypeStruct((s, SSD_CONV), BF16), jax.ShapeDtypeStruct((s, LANE), BF16),
                   jax.ShapeDtypeStruct((4, SSD_CONV), F32), jax.ShapeDtypeStruct((1, SSD_CONV), F32), jax.ShapeDtypeStruct((1, LANE), F32),
                   jax.ShapeDtypeStruct((1, LANE), F32), jax.ShapeDtypeStruct((8, LANE), F32), jax.ShapeDtypeStruct((1, SSD_DIM), F32)),
        in_specs=[pl.BlockSpec((L, SSD_DIM), lambda i: (rv(i), 5)), pl.BlockSpec((L, SSD_CONV), lambda i: (rv(i), 2)),
                  pl.BlockSpec((HALO, SSD_CONV), lambda i: (_prev_halo(rv(i), L), 2)), pl.BlockSpec((L, LANE), lambda i: (rv(i), 18)),
                  pl.BlockSpec((L, SSD_DIM), lambda i: (rv(i), 5)), pl.BlockSpec((L, SSD_DIM), lambda i: (rv(i), 0)),
                  pl.BlockSpec((None, SSD_DIM, LANE), lambda i: (rv(i), 0, 0)),
                  pl.BlockSpec((4, SSD_CONV), lambda i: (0, 0)), vec(SSD_CONV), vec(LANE), vec(LANE), vec(SSD_DIM), vec(SSD_DIM)],
        out_specs=(pl.BlockSpec((L, SSD_DIM), lambda i: (rv(i), 0)), pl.BlockSpec((L, SSD_CONV), lambda i: (rv(i), 0)),
                   pl.BlockSpec((L, LANE), lambda i: (rv(i), 0)), pl.BlockSpec((4, SSD_CONV), lambda i: (0, 0)), vec(SSD_CONV),
                   vec(LANE), vec(LANE), pl.BlockSpec((8, LANE), lambda i: (0, 0)), vec(SSD_DIM)),
        scratch_shapes=[pltpu.VMEM((L + HALO, SSD_CONV), F32), pltpu.VMEM((L + HALO, SSD_CONV), F32), pltpu.VMEM((HALO, SSD_CONV), F32),
                        pltpu.VMEM((SSD_DIM, LANE), F32), pltpu.VMEM((8, SSD_DIM), F32)],
        compiler_params=_cp("arbitrary"),
    )(proj, proj, proj, proj, dcat, ypre, states, cw, cb, dtb, alog, dvec, nw)


def _adamw(parts, w, m, v, name):
    r, c = w.shape
    tr = r
    for cand in (256, 128, 64, 32, 16, 8):
        if r % cand == 0 and (cand * c * 4) <= 2 * 1024 * 1024:
            tr = cand
            break
    c1 = 1.0 - B1 ** STEP
    c2 = 1.0 - B2 ** STEP

    def body(p_ref, w_ref, m_ref, v_ref, g_ref, d_ref, nm_ref, nv_ref):
        g = p_ref[0].astype(F32)
        for dev in range(1, NDEV):
            g = g + p_ref[dev].astype(F32)
        mn = B1 * m_ref[...] + (1.0 - B1) * g
        vn = B2 * v_ref[...] + (1.0 - B2) * (g * g)
        g_ref[...] = g
        nm_ref[...] = mn
        nv_ref[...] = vn
        d_ref[...] = -LR * ((mn / c1) / (jnp.sqrt(vn / c2) + AEPS) + WD * w_ref[...])

    blk = pl.BlockSpec((tr, c), lambda i: (i, 0))
    out = jax.ShapeDtypeStruct((r, c), F32)
    return pl.pallas_call(
        body, name=name, grid=(r // tr,), out_shape=(out, out, out, out),
        in_specs=[pl.BlockSpec((NDEV, tr, c), lambda i: (0, i, 0)), blk, blk, blk], out_specs=(blk, blk, blk, blk),
        compiler_params=_cp("parallel"),
    )(parts, w, m, v)


def _pad_win(w):
    z = lambda n: jnp.zeros((w.shape[0], n), w.dtype)
    return jnp.concatenate([w[:, :384], z(64), w[:, 384:416], z(32), w[:, 416:], z(124)], axis=1)


def _unpad_win(g):
    return jnp.concatenate([g[:, :384], g[:, 448:480], g[:, 512:2308]], axis=1)


def _pad_wout(w):
    att = jnp.pad(w[:512].reshape(HEADS, 64, D), ((0, 0), (64, 0), (0, 0))).reshape(HEADS * LANE, D)
    return jnp.concatenate([att, w[512:]], axis=0)


def _unpad_wout(g):
    att = g[:HEADS * LANE].reshape(HEADS, LANE, D)[:, 64:, :].reshape(512, D)
    return jnp.concatenate([att, g[HEADS * LANE:]], axis=0)


def _lanes(v, n=LANE):
    return jnp.pad(v, (0, n - v.shape[0])).reshape(1, n)


def _prep_layer(big, small, l):
    p = {}
    p["win"] = _pad_win(big["w_in"].reshape(D, 2212))
    p["wq"] = jnp.pad(big["mla_w_q_up"], ((0, 0), (0, 0), (0, LANE - 96)))
    p["wkv"] = big["mla_w_kv_up"]
    p["scw"] = big["sc_conv_w"].astype(F32).transpose(1, 0, 2).reshape(3, SC)
    p["ssdcw"] = big["ssd_conv_w"].astype(F32).transpose(1, 0, 2).reshape(4, SSD_CONV)
    p["wout"] = _pad_wout(big["w_out"].reshape(1024, D))
    p["wup"] = big["ffn_w_up"]
    p["fcw"] = big["ffn_conv_w"].astype(F32)
    p["wdown"] = big["ffn_w_down"].reshape(4, FB, D)
    for nm in ("norm_mix_pre", "norm_mix_post", "norm_ffn_pre", "norm_ffn_post", "mla_q_norm", "mla_kv_norm", "ssd_conv_b", "ssd_norm"):
        p[nm] = small[nm][l].reshape(1, -1)
    p["dtb"] = _lanes(small["ssd_dt_bias"][l])
    p["alog"] = _lanes(small["ssd_a_log"][l])
    p["dvec"] = jnp.repeat(small["ssd_d"][l], 64).reshape(1, SSD_DIM)
    p["fcb"] = small["ffn_conv_b"][l].reshape(NDEV, 1, FB)
    return p


def _rope_tables(positions):
    inv_freq = 1.0 / (ROPE_THETA ** (jnp.arange(0, ROPE, 2, dtype=F32) / ROPE))
    ang = positions.astype(F32)[:, None] * inv_freq
    cos, sin = jnp.cos(ang), jnp.sin(ang)
    s = positions.shape[0]
    z = lambda n: jnp.zeros((s, n), F32)
    tc = jnp.concatenate([jnp.ones((s, 64), F32), cos, cos, z(32)], axis=1)
    ta = jnp.concatenate([z(64), -sin, z(48)], axis=1)
    tb = jnp.concatenate([z(80), sin, z(32)], axis=1)
    return tc, ta, tb


def _layer_fwd(xv, p, tabs, prefetch=None):
    h = _rms(xv, p["norm_mix_pre"], BF16, "rms_pre")
    proj = _mm_rows("in_proj", h, p["win"], BF16, NN)
    q, k, kv = _mla_prep(proj, tabs, p["mla_q_norm"], p["mla_kv_norm"], p["wq"], p["wkv"])
    o, lse, gathered = _flash_fwd(q, k, kv, prefetch)
    yconv = _sconv_fwd(proj, p["scw"])
    yssd, ypre, states = _ssd_fwd(proj, p["ssdcw"], p["ssd_conv_b"], p["dtb"], p["alog"], p["dvec"], p["ssd_norm"])
    cat = jnp.concatenate([o, yconv, yssd], axis=1)
    mixed = _mm_rows("out_proj", cat, p["wout"], F32, NN)
    x1 = _add_rms(xv, mixed, p["norm_mix_post"], "add_rms")
    h2 = _rms(x1, p["norm_ffn_pre"], BF16, "rms_pre")
    upre = _mm_up(h2, p["wup"])
    act = _ffn_act(upre, p["fcw"], p["fcb"])
    f = _mm_down(act, p["wdown"])
    x2 = _add_rms(x1, f, p["norm_ffn_post"], "add_rms")
    saved = dict(x=xv, h=h, proj=proj, q=q, k=k, kv=kv, lse=lse, ypre=ypre, states=states, cat=cat, mixed=mixed, x1=x1, h2=h2,
                 upre=upre, act=act, f=f)
    return x2, saved, gathered


def _layer_bwd(dx2, sv, p, tabs, pending=None):
    df, g_nfpo = _rms_bwd(sv["f"], p["norm_ffn_post"], dx2, None, BF16, "rms_bwd_post")
    dact = _mm_dact(df, p["wdown"])
    g_wdown = _mm_dwdown(sv["act"], df)
    du, g_fcb = _ffn_bwd_a(sv["upre"], dact, p["fcw"], p["fcb"])
    dupre, g_fcw = _ffn_bwd_b(du, sv["upre"], p["fcw"])
    dh2 = _mm_dh2(dupre, p["wup"])
    g_wup = _mm_dwup(sv["h2"], dupre)
    dx1, g_nfp = _rms_bwd(sv["x1"], p["norm_ffn_pre"], dh2, dx2, F32, "rms_bwd_pre")
    dmixed, g_nmpo = _rms_bwd(sv["mixed"], p["norm_mix_post"], dx1, None, BF16, "rms_bwd_post")
    dcat = _mm_rows("dcat", dmixed, p["wout"], BF16, NT)
    g_wout = _mm_wgrad("dw_out", sv["cat"], dmixed, BF16)
    dq, dk, dv, received = _flash_bwd(sv["q"], sv["k"], sv["kv"], sv["cat"], dcat, sv["lse"], pending)
    dcq, dckv, dkr, g_wq, g_wkv, g_qn, g_kvn = _mla_prep_bwd(sv["proj"], tabs, p["mla_q_norm"], p["mla_kv_norm"], p["wq"], p["wkv"], dq, dk, dv)
    dscb, dscc, dsch, g_scw = _sconv_bwd(sv["proj"], dcat, p["scw"])
    dz, dxbc, ddt, g_cw, g_cb, g_dtb, g_alog, g_d, g_nw = _ssd_bwd(
        sv["proj"], dcat, sv["ypre"], sv["states"], p["ssdcw"], p["ssd_conv_b"], p["dtb"], p["alog"], p["dvec"], p["ssd_norm"])
    dproj = jnp.concatenate([dcq, dckv, dkr, dscb, dscc, dsch, dz, dxbc, ddt], axis=1)
    dh = _mm_rows("dh", dproj, p["win"], BF16, NT)
    g_win = _mm_wgrad("dw_in", sv["h"], dproj, BF16)
    dx, g_nmp = _rms_bwd(sv["x"], p["norm_mix_pre"], dh, dx1, F32, "rms_bwd_pre")
    big = {
        "w_in": _unpad_win(g_win).reshape(NDEV, 128, 2212),
        "mla_w_q_up": g_wq[:, :, :96].astype(BF16),
        "mla_w_kv_up": g_wkv.astype(BF16),
        "sc_conv_w": g_scw.reshape(3, NDEV, 32).transpose(1, 0, 2).astype(BF16),
        "ssd_conv_w": g_cw.reshape(4, NDEV, 96).transpose(1, 0, 2).astype(BF16),
        "w_out": _unpad_wout(g_wout).reshape(NDEV, 128, D),
        "ffn_w_up": g_wup,
        "ffn_conv_w": g_fcw.astype(BF16),
        "ffn_w_down": g_wdown.reshape(NDEV, 352, D),
    }
    small = {
        "norm_mix_pre": g_nmp[0], "norm_mix_post": g_nmpo[0], "norm_ffn_pre": g_nfp[0], "norm_ffn_post": g_nfpo[0],
        "mla_q_norm": g_qn[0], "mla_kv_norm": g_kvn[0], "ssd_conv_b": g_cb[0], "ssd_dt_bias": g_dtb[0, :SSD_H],
        "ssd_a_log": g_alog[0, :SSD_H], "ssd_d": g_d[0, :SSD_H], "ssd_norm": g_nw[0], "ffn_conv_b": g_fcb.reshape(-1),
    }
    return dx, big, small, received


def _local_step(xv, positions, target, layers):
    tabs = _rope_tables(positions)
    saved = []
    for p in layers:
        xv, sv, _ = _layer_fwd(xv, p, tabs)
        saved.append(sv)
    loss, dx = _loss_head(xv, target)
    bigs, smalls = [None] * DEPTH, [None] * DEPTH
    for l in reversed(range(len(layers))):
        dx, bigs[l], smalls[l], _ = _layer_bwd(dx, saved[l], layers[l], tabs)
    return loss[0, 0], dx, bigs, smalls


def _pack_rows(flat, lead, width, mult):
    n = flat.shape[-1]
    rows = -(-n // (width * mult)) * mult
    pad = [(0, 0)] * (flat.ndim - 1) + [(0, rows * width - n)]
    return jnp.pad(flat, pad).reshape(lead + (rows, width))


def _rows2(n, layer=False):
    shape = SHAPES[n][1:] if layer else SHAPES[n]
    return (math.prod(shape[:-1]), shape[-1])


def _group_pack(group, get, lead):
    width, names = group
    pieces = []
    for n in names:
        rows, cols = _rows2(n, True)
        pad = [(0, 0)] * len(lead) + [(0, -rows % 16), (0, width - cols)]
        pieces.append(jnp.pad(get(n), pad))
    return pieces[0] if len(pieces) == 1 else jnp.concatenate(pieces, axis=len(lead))


def _group_unpack(group, buf, padded=False):
    _, names = group
    res, off = {}, 0
    for n in names:
        rows, cols = _rows2(n, True)
        rp = rows + (-rows % 16)
        res[n] = buf[:, off:off + (rp if padded else rows), :cols]
        off += rp
    return res


def kernel(x, positions, norm_mix_pre, norm_mix_post, norm_ffn_pre, norm_ffn_post, w_in, mla_q_norm, mla_w_q_up, mla_kv_norm, mla_w_kv_up, sc_conv_w, ssd_conv_w, ssd_conv_b, ssd_dt_bias, ssd_a_log, ssd_d, ssd_norm, w_out, ffn_w_up, ffn_conv_w, ffn_conv_b, ffn_w_down, loss_target, m_norm_mix_pre, m_norm_mix_post, m_norm_ffn_pre, m_norm_ffn_post, m_w_in, m_mla_q_norm, m_mla_w_q_up, m_mla_kv_norm, m_mla_w_kv_up, m_sc_conv_w, m_ssd_conv_w, m_ssd_conv_b, m_ssd_dt_bias, m_ssd_a_log, m_ssd_d, m_ssd_norm, m_w_out, m_ffn_w_up, m_ffn_conv_w, m_ffn_conv_b, m_ffn_w_down, v_norm_mix_pre, v_norm_mix_post, v_norm_ffn_pre, v_norm_ffn_post, v_w_in, v_mla_q_norm, v_mla_w_q_up, v_mla_kv_norm, v_mla_w_kv_up, v_sc_conv_w, v_ssd_conv_w, v_ssd_conv_b, v_ssd_dt_bias, v_ssd_a_log, v_ssd_d, v_ssd_norm, v_w_out, v_ffn_w_up, v_ffn_conv_w, v_ffn_conv_b, v_ffn_w_down):
    given = dict(locals())
    w = {n: given[n] for n in WEIGHTS}
    m = {n: given["m_" + n] for n in WEIGHTS}
    v = {n: given["v_" + n] for n in WEIGHTS}

    def shards(l):
        return [_group_pack(g, lambda n: w[n][l].astype(BF16).reshape(_rows2(n, True)), ()) for g in GROUPS]

    small_w = {n: w[n] for n, _ in SMALL}
    tabs = _rope_tables(positions[0])
    xv, layers, saved = x[0], [], []
    gathered = _all_gather(shards(0), "gather_weights")
    for l in range(DEPTH):
        big = {}
        for g, buf in zip(GROUPS, gathered):
            for n, piece in _group_unpack(g, buf).items():
                big[n] = piece.reshape((NDEV,) + SHAPES[n][1:])
        layers.append(_prep_layer(big, small_w, l))
        xv, sv, gathered = _layer_fwd(xv, layers[l], tabs, shards(l + 1) if l + 1 < DEPTH else None)
        saved.append(sv)
    loss, dx = _loss_head(xv, loss_target[0])
    loss = lax.psum(loss[0, 0], ("x", "y", "c"))

    smalls, recvs, pending = [None] * DEPTH, [None] * DEPTH, None
    for l in reversed(range(DEPTH)):
        dx, grads, smalls[l], received = _layer_bwd(dx, saved[l], layers[l], tabs, pending)
        if pending is not None:
            recvs[l + 1] = received
        pending = [_group_pack(g, lambda n: grads[n].reshape((NDEV,) + _rows2(n, True)), (NDEV,)) for g in GROUPS]
    recvs[0] = _all_to_all(pending, "exchange_grads")
    out = {}
    for gi, g in enumerate(GROUPS):
        per_layer = [_group_unpack(g, recvs[l][gi], padded=True) for l in range(DEPTH)]
        for n in g[1]:
            r2 = _rows2(n)
            rows, cols = _rows2(n, True)
            parts = jnp.concatenate([per_layer[l][n] for l in range(DEPTH)], axis=1)
            parts = parts.reshape(NDEV, DEPTH, -1, cols)[:, :, :rows].reshape((NDEV,) + r2)
            res = _adamw(parts, w[n].reshape(r2), m[n].reshape(r2), v[n].reshape(r2), "adamw_" + n)
            out[n] = [a.reshape(SHAPES[n]) for a in res]

    sflat = jnp.concatenate([jnp.stack([smalls[l][n] for l in range(DEPTH)]).reshape(-1) for n, _ in SMALL])
    sparts = _all_gather([_pack_rows(sflat, (), LANE, 8)], "gather_small_grads")[0]
    pk = lambda d: _pack_rows(jnp.concatenate([d[n].reshape(-1) for n, _ in SMALL]), (), LANE, 8)
    res = _adamw(sparts, pk(w), pk(m), pk(v), "adamw_small")
    off = 0
    for n, width in SMALL:
        out[n] = [a.reshape(-1)[off:off + DEPTH * width].reshape(DEPTH, width) for a in res]
        off += DEPTH * width

    return (loss, dx[None], *[out[n][0] for n in WEIGHTS], *[out[n][1] for n in WEIGHTS],
            *[out[n][2] for n in WEIGHTS], *[out[n][3] for n in WEIGHTS])
```

```python
import functools
import math

import jax
import jax.numpy as jnp
from jax import lax
from jax.experimental import pallas as pl
from jax.experimental.pallas import tpu as pltpu

F32 = jnp.float32
BF16 = jnp.bfloat16

D = 1024
DEPTH = 4
NDEV = 8
HEADS = 8
QL = 256
KVL = 128
ROPE = 32
NOPE = 64
SC = 256
SSD_DIM = 256
SSD_CONV = 768
SSD_H = 4
SSD_L = 128
FFN = 2816
FB = 704
EPS = 1e-6
ROPE_THETA = 10000.0
ATT_SCALE = 96 ** -0.5
LR, B1, B2, AEPS, WD, STEP = 0.001, 0.9, 0.999, 1e-08, 0.01, 10

PW = 2432
CATW = 1536

ROW_TILE = 512
ATT_TILE = 256
FWD_HEADS = 4
BWD_HEADS = 2
HALO = 16
LANE = 128
NEG = -1e30
HI = lax.Precision.HIGHEST
NN = (((1,), (0,)), ((), ()))
NT = (((1,), (1,)), ((), ()))
TN = (((0,), (0,)), ((), ()))
VMEM_LIMIT = 56 * 1024 * 1024

SHARDED = (
    ("w_in", (4, 128, 2212)),
    ("mla_w_q_up", (4, 256, 96)),
    ("mla_w_kv_up", (4, 128, 128)),
    ("sc_conv_w", (4, 3, 32)),
    ("ssd_conv_w", (4, 4, 96)),
    ("w_out", (4, 128, 1024)),
    ("ffn_w_up", (4, 1024, 704)),
    ("ffn_conv_w", (4, 3, 704)),
    ("ffn_w_down", (4, 352, 1024)),
)
SHAPES = dict(SHARDED)
GROUPS = (
    (2212, ("w_in",)),
    (1024, ("w_out", "ffn_w_down")),
    (704, ("ffn_w_up", "ffn_conv_w")),
    (96, ("mla_w_q_up", "ssd_conv_w", "sc_conv_w")),
    (128, ("mla_w_kv_up",)),
)
ATT_SIDE = (0, 3, 4)
FFN_SIDE = (1, 2)
SMALL = (
    ("norm_mix_pre", 1024), ("norm_mix_post", 1024), ("norm_ffn_pre", 1024), ("norm_ffn_post", 1024),
    ("mla_q_norm", 256), ("mla_kv_norm", 128), ("ssd_conv_b", 768), ("ssd_dt_bias", 4), ("ssd_a_log", 4),
    ("ssd_d", 4), ("ssd_norm", 256), ("ffn_conv_b", 5632),
)
WEIGHTS = ("norm_mix_pre", "norm_mix_post", "norm_ffn_pre", "norm_ffn_post", "w_in", "mla_q_norm", "mla_w_q_up",
           "mla_kv_norm", "mla_w_kv_up", "sc_conv_w", "ssd_conv_w", "ssd_conv_b", "ssd_dt_bias", "ssd_a_log", "ssd_d",
           "ssd_norm", "w_out", "ffn_w_up", "ffn_conv_w", "ffn_conv_b", "ffn_w_down")


def _dot(a, b, dims=NN, precision=None):
    return lax.dot_general(a, b, dims, precision=precision, preferred_element_type=F32)


def _sig(v):
    return 1.0 / (1.0 + jnp.exp(-v))


def _cp(*sem):
    return pltpu.CompilerParams(dimension_semantics=sem, vmem_limit_bytes=VMEM_LIMIT)


def _rowsum(v):
    return jnp.sum(v, axis=0, keepdims=True)


def _prev_halo(i, ts):
    return jnp.maximum(i * (ts // HALO) - 1, 0)


def _next_halo(i, ts, n):
    return jnp.minimum((i + 1) * (ts // HALO), n * (ts // HALO) - 1)


def _gather_plan(x_refs, out_refs, send_sems, recv_sems, local_sems):
    n = len(x_refs)
    x, y, cc = lax.axis_index("x"), lax.axis_index("y"), lax.axis_index("c")
    me, sibling = (x, y, cc), (x, y, 1 - cc)
    chips = [(1 - x, y), (x, 1 - y), (1 - x, 1 - y)]

    def rows(t, px, py, pc):
        return out_refs[t].at[4 * px + 2 * py + pc]

    def copy(t, k, block, to, own=False):
        return pltpu.make_async_remote_copy(
            src_ref=x_refs[t] if own else rows(t, *block), dst_ref=rows(t, *block),
            send_sem=send_sems.at[7 * t + k], recv_sem=recv_sems.at[7 * t + k], device_id=to, device_id_type=pl.DeviceIdType.MESH)

    def local(t):
        return pltpu.make_async_copy(x_refs[t], rows(t, *me), local_sems.at[t])

    def start():
        for t in range(n):
            local(t).start()
            copy(t, 0, me, sibling, own=True).start()
            for j, chip in enumerate(chips):
                copy(t, 1 + j, me, (*chip, cc), own=True).start()

    def finish():
        for j, chip in enumerate(chips):
            for t in range(n):
                copy(t, 1 + j, (*chip, cc), me).wait_recv()
                copy(t, 4 + j, (*chip, cc), sibling).start()
        for t in range(n):
            copy(t, 0, sibling, me).wait_recv()
            for j, chip in enumerate(chips):
                copy(t, 4 + j, (*chip, 1 - cc), me).wait_recv()
        for t in range(n):
            copy(t, 0, me, sibling, own=True).wait_send()
            for j, chip in enumerate(chips):
                copy(t, 1 + j, me, (*chip, cc), own=True).wait_send()
                copy(t, 4 + j, (*chip, cc), sibling).wait_send()
            local(t).wait()

    return start, finish


def _exchange_plan(x_refs, out_refs, send_sems, recv_sems, local_sems):
    n = len(x_refs)
    x, y, cc = lax.axis_index("x"), lax.axis_index("y"), lax.axis_index("c")
    me = 4 * x + 2 * y + cc

    def copies():
        res = [pltpu.make_async_copy(x_refs[t].at[me], out_refs[t].at[me], local_sems.at[t]) for t in range(n)]
        for k in range(1, NDEV):
            px = 1 - x if k & 4 else x
            py = 1 - y if k & 2 else y
            pc = 1 - cc if k & 1 else cc
            peer = 4 * px + 2 * py + pc
            for t in range(n):
                res.append(pltpu.make_async_remote_copy(
                    src_ref=x_refs[t].at[peer], dst_ref=out_refs[t].at[me], send_sem=send_sems.at[7 * t + k - 1],
                    recv_sem=recv_sems.at[7 * t + k - 1], device_id=(px, py, pc), device_id_type=pl.DeviceIdType.MESH))
        return res

    def start():
        for cp in copies():
            cp.start()

    def finish():
        for cp in copies():
            cp.wait()

    return start, finish


def _comm_scratch(n):
    return [pltpu.SemaphoreType.DMA((7 * n,)), pltpu.SemaphoreType.DMA((7 * n,)), pltpu.SemaphoreType.DMA((n,))]


ANY = pl.BlockSpec(memory_space=pl.ANY)


def _all_gather(xs, name):
    n = len(xs)

    def body(*refs):
        start, finish = _gather_plan(refs[:n], refs[n:2 * n], *refs[2 * n:])
        start()
        finish()

    return pl.pallas_call(
        body, name=name, out_shape=[jax.ShapeDtypeStruct((NDEV,) + a.shape, a.dtype) for a in xs],
        in_specs=[ANY] * n, out_specs=[ANY] * n, scratch_shapes=_comm_scratch(n),
    )(*xs)


def _all_to_all(xs, name):
    n = len(xs)

    def body(*refs):
        start, finish = _exchange_plan(refs[:n], refs[n:2 * n], *refs[2 * n:])
        start()
        finish()

    return pl.pallas_call(
        body, name=name, out_shape=[jax.ShapeDtypeStruct(a.shape, a.dtype) for a in xs],
        in_specs=[ANY] * n, out_specs=[ANY] * n, scratch_shapes=_comm_scratch(n),
    )(*xs)


def _mm(name, a, b, out_shape, grid, a_spec, b_spec, o_spec, dims, acc_shape):
    nk = grid[2]

    def single(a_ref, b_ref, o_ref):
        o_ref[...] = _dot(a_ref[...], b_ref[...], dims).astype(o_ref.dtype)

    if nk == 1:
        return pl.pallas_call(
            single, name=name, grid=grid, out_shape=out_shape, in_specs=[a_spec, b_spec], out_specs=o_spec,
            compiler_params=_cp("parallel", "parallel", "arbitrary"),
        )(a, b)

    def body(a_ref, b_ref, o_ref, acc_ref):
        k = pl.program_id(2)

        @pl.when(k == 0)
        def _():
            acc_ref[...] = jnp.zeros_like(acc_ref)

        acc_ref[...] += _dot(a_ref[...], b_ref[...], dims)

        @pl.when(k == nk - 1)
        def _():
            o_ref[...] = acc_ref[...].astype(o_ref.dtype)

    return pl.pallas_call(
        body, name=name, grid=grid, out_shape=out_shape, in_specs=[a_spec, b_spec], out_specs=o_spec,
        scratch_shapes=[pltpu.VMEM(acc_shape, F32)], compiler_params=_cp("parallel", "parallel", "arbitrary"),
    )(a, b)


def _mm_rows(name, a, w, out_dtype, dims):
    s, k = a.shape
    n = w.shape[1] if dims == NN else w.shape[0]
    tm = min(ROW_TILE, s)
    return _mm(name, a, w, jax.ShapeDtypeStruct((s, n), out_dtype), (s // tm, 1, 1),
               pl.BlockSpec((tm, k), lambda i, j, kk: (i, 0)), pl.BlockSpec(w.shape, lambda i, j, kk: (0, 0)),
               pl.BlockSpec((tm, n), lambda i, j, kk: (i, 0)), dims, (tm, n))


def _mm_wgrad(name, a, g, out_dtype):
    s, m = a.shape
    n = g.shape[1]
    tk = min(ROW_TILE, s)
    return _mm(name, a, g, jax.ShapeDtypeStruct((m, n), out_dtype), (1, 1, s // tk),
               pl.BlockSpec((tk, m), lambda i, j, kk: (kk, 0)), pl.BlockSpec((tk, n), lambda i, j, kk: (kk, 0)),
               pl.BlockSpec((m, n), lambda i, j, kk: (0, 0)), TN, (m, n))


def _mm_up(h2, wup):
    s = h2.shape[0]
    tm = min(ROW_TILE, s)
    return _mm("ffn_up", h2, wup, jax.ShapeDtypeStruct((NDEV, s, FB), BF16), (NDEV, s // tm, 1),
               pl.BlockSpec((tm, D), lambda j, i, kk: (i, 0)), pl.BlockSpec((None, D, FB), lambda j, i, kk: (j, 0, 0)),
               pl.BlockSpec((None, tm, FB), lambda j, i, kk: (j, i, 0)), NN, (tm, FB))


def _mm_down(act, wdown):
    s = act.shape[1]
    tm = min(ROW_TILE, s)
    return _mm("ffn_down", act, wdown, jax.ShapeDtypeStruct((s, D), F32), (s // tm, 1, 4),
               pl.BlockSpec((None, tm, FB), lambda i, j, kk: (kk, i, 0)), pl.BlockSpec((None, FB, D), lambda i, j, kk: (kk, 0, 0)),
               pl.BlockSpec((tm, D), lambda i, j, kk: (i, 0)), NN, (tm, D))


def _mm_dact(df, wdown):
    s = df.shape[0]
    tm = min(ROW_TILE, s)
    return _mm("ffn_dact", df, wdown, jax.ShapeDtypeStruct((4, s, FB), BF16), (4, s // tm, 1),
               pl.BlockSpec((tm, D), lambda j, i, kk: (i, 0)), pl.BlockSpec((None, FB, D), lambda j, i, kk: (j, 0, 0)),
               pl.BlockSpec((None, tm, FB), lambda j, i, kk: (j, i, 0)), NT, (tm, FB))


def _mm_dwdown(act, df):
    s = df.shape[0]
    tk = min(ROW_TILE, s)
    return _mm("ffn_dwdown", act, df, jax.ShapeDtypeStruct((4, FB, D), BF16), (4, 1, s // tk),
               pl.BlockSpec((None, tk, FB), lambda j, i, kk: (j, kk, 0)), pl.BlockSpec((tk, D), lambda j, i, kk: (kk, 0)),
               pl.BlockSpec((None, FB, D), lambda j, i, kk: (j, 0, 0)), TN, (FB, D))


def _mm_dh2(dupre, wup):
    s = dupre.shape[1]
    tm = min(ROW_TILE, s)
    return _mm("ffn_dh2", dupre, wup, jax.ShapeDtypeStruct((s, D), BF16), (s // tm, 1, NDEV),
               pl.BlockSpec((None, tm, FB), lambda i, j, kk: (kk, i, 0)), pl.BlockSpec((None, D, FB), lambda i, j, kk: (kk, 0, 0)),
               pl.BlockSpec((tm, D), lambda i, j, kk: (i, 0)), NT, (tm, D))


def _mm_dwup(h2, dupre):
    s = h2.shape[0]
    tk = min(ROW_TILE, s)
    return _mm("ffn_dwup", h2, dupre, jax.ShapeDtypeStruct((NDEV, D, FB), BF16), (NDEV, 1, s // tk),
               pl.BlockSpec((tk, D), lambda j, i, kk: (kk, 0)), pl.BlockSpec((None, tk, FB), lambda j, i, kk: (j, kk, 0)),
               pl.BlockSpec((None, D, FB), lambda j, i, kk: (j, 0, 0)), TN, (D, FB))


def _rms(xv, w, out_dtype, name):
    s, d = xv.shape
    ts = min(ROW_TILE, s)

    def body(x_ref, w_ref, o_ref):
        xf = x_ref[...].astype(F32)
        r = lax.rsqrt(jnp.mean(xf * xf, axis=-1, keepdims=True) + EPS)
        o_ref[...] = (xf * r * w_ref[...]).astype(o_ref.dtype)

    return pl.pallas_call(
        body, name=name, grid=(s // ts,), out_shape=jax.ShapeDtypeStruct((s, d), out_dtype),
        in_specs=[pl.BlockSpec((ts, d), lambda i: (i, 0)), pl.BlockSpec((1, d), lambda i: (0, 0))],
        out_specs=pl.BlockSpec((ts, d), lambda i: (i, 0)), compiler_params=_cp("parallel"),
    )(xv, w)


def _add_rms(xv, mv, w, name):
    s, d = xv.shape
    ts = min(ROW_TILE, s)

    def body(x_ref, m_ref, w_ref, o_ref):
        mf = m_ref[...].astype(F32)
        r = lax.rsqrt(jnp.mean(mf * mf, axis=-1, keepdims=True) + EPS)
        o_ref[...] = x_ref[...] + mf * r * w_ref[...]

    return pl.pallas_call(
        body, name=name, grid=(s // ts,), out_shape=jax.ShapeDtypeStruct((s, d), F32),
        in_specs=[pl.BlockSpec((ts, d), lambda i: (i, 0)), pl.BlockSpec((ts, d), lambda i: (i, 0)), pl.BlockSpec((1, d), lambda i: (0, 0))],
        out_specs=pl.BlockSpec((ts, d), lambda i: (i, 0)), compiler_params=_cp("parallel"),
    )(xv, mv, w)


def _rms_bwd_math(xf, w, dy):
    r = lax.rsqrt(jnp.mean(xf * xf, axis=-1, keepdims=True) + EPS)
    xh = xf * r
    dxh = dy * w
    dx = r * (dxh - xh * jnp.mean(dxh * xh, axis=-1, keepdims=True))
    return dx, dy * xh


def _rms_bwd(xv, w, dy, dres, out_dtype, name):
    s, d = xv.shape
    ts = min(ROW_TILE, s)
    with_res = dres is not None

    def body(*refs):
        if with_res:
            x_ref, w_ref, dy_ref, dres_ref, dx_ref, dw_ref = refs
        else:
            x_ref, w_ref, dy_ref, dx_ref, dw_ref = refs
        dx, dwt = _rms_bwd_math(x_ref[...].astype(F32), w_ref[...], dy_ref[...].astype(F32))
        if with_res:
            dx = dx + dres_ref[...]
        dx_ref[...] = dx.astype(dx_ref.dtype)

        @pl.when(pl.program_id(0) == 0)
        def _():
            dw_ref[...] = jnp.zeros_like(dw_ref)

        dw_ref[...] += _rowsum(dwt)

    row = pl.BlockSpec((ts, d), lambda i: (i, 0))
    vec = pl.BlockSpec((1, d), lambda i: (0, 0))
    ins = [xv, w, dy] + ([dres] if with_res else [])
    return pl.pallas_call(
        body, name=name, grid=(s // ts,),
        out_shape=(jax.ShapeDtypeStruct((s, d), out_dtype), jax.ShapeDtypeStruct((1, d), F32)),
        in_specs=[row, vec, row] + ([row] if with_res else []), out_specs=(row, vec), compiler_params=_cp("arbitrary"),
    )(*ins)


def _loss_head(yv, tv):
    s, d = yv.shape
    ts = min(ROW_TILE, s)

    def body(y_ref, t_ref, l_ref, dy_ref):
        e = y_ref[...] - t_ref[...]
        dy_ref[...] = e * (1.0 / d)

        @pl.when(pl.program_id(0) == 0)
        def _():
            l_ref[...] = jnp.zeros_like(l_ref)

        tot = jnp.sum(jnp.sum(e * e, axis=1, keepdims=True), axis=0, keepdims=True)
        l_ref[...] += jnp.broadcast_to(tot * (0.5 / d), (8, LANE))

    row = pl.BlockSpec((ts, d), lambda i: (i, 0))
    return pl.pallas_call(
        body, name="loss_head", grid=(s // ts,),
        out_shape=(jax.ShapeDtypeStruct((8, LANE), F32), jax.ShapeDtypeStruct((s, d), F32)),
        in_specs=[row, row], out_specs=(pl.BlockSpec((8, LANE), lambda i: (0, 0)), row), compiler_params=_cp("arbitrary"),
    )(yv, tv)


def _rope(v, c, a, b):
    return v * c + pltpu.roll(v, LANE - 16, 1) * a + pltpu.roll(v, 16, 1) * b


def _rope_t(dv, c, a, b):
    return dv * c + pltpu.roll(dv * a, 16, 1) + pltpu.roll(dv * b, LANE - 16, 1)


def _mla_prep(proj, tabs, qnw, kvnw, wq, wkv):
    s = proj.shape[0]
    ts = min(ROW_TILE, s)
    tc, ta, tb = tabs

    def body(cq_ref, ckv_ref, kr_ref, c_ref, a_ref, b_ref, qnw_ref, kvnw_ref, wq_ref, wkv_ref, q_ref, k_ref, kv_ref):
        c, a, b = c_ref[...], a_ref[...], b_ref[...]
        cq = cq_ref[...].astype(F32)
        qn = (cq * lax.rsqrt(jnp.mean(cq * cq, axis=-1, keepdims=True) + EPS) * qnw_ref[...]).astype(BF16)
        ckv = ckv_ref[...].astype(F32)
        kvn = (ckv * lax.rsqrt(jnp.mean(ckv * ckv, axis=-1, keepdims=True) + EPS) * kvnw_ref[...]).astype(BF16)
        kr = _rope(kr_ref[...].astype(F32), c, a, b)
        lane = lax.broadcasted_iota(jnp.int32, (ts, LANE), 1)
        for h in range(HEADS):
            q_ref[h] = _rope(_dot(qn, wq_ref[h]), c, a, b).astype(BF16)
            kv = _dot(kvn, wkv_ref[h])
            kv_ref[h] = kv.astype(BF16)
            k_ref[h] = jnp.where(lane < NOPE, kv, kr).astype(BF16)

    tab = pl.BlockSpec((ts, LANE), lambda i: (i, 0))
    hd = pl.BlockSpec((HEADS, ts, LANE), lambda i: (0, i, 0))
    out = jax.ShapeDtypeStruct((HEADS, s, LANE), BF16)
    return pl.pallas_call(
        body, name="mla_prep", grid=(s // ts,), out_shape=(out, out, out),
        in_specs=[pl.BlockSpec((ts, QL), lambda i: (i, 0)), pl.BlockSpec((ts, LANE), lambda i: (i, 2)),
                  pl.BlockSpec((ts, LANE), lambda i: (i, 3)), tab, tab, tab,
                  pl.BlockSpec((1, QL), lambda i: (0, 0)), pl.BlockSpec((1, KVL), lambda i: (0, 0)),
                  pl.BlockSpec((HEADS, QL, LANE), lambda i: (0, 0, 0)), pl.BlockSpec((HEADS, KVL, LANE), lambda i: (0, 0, 0))],
        out_specs=(hd, hd, hd), compiler_params=_cp("parallel"),
    )(proj, proj, proj, tc, ta, tb, qnw, kvnw, wq, wkv)


def _mla_prep_bwd(proj, tabs, qnw, kvnw, wq, wkv, dq, dk, dv):
    s = proj.shape[0]
    ts = min(ROW_TILE, s)
    tc, ta, tb = tabs

    def body(cq_ref, ckv_ref, c_ref, a_ref, b_ref, qnw_ref, kvnw_ref, wq_ref, wkv_ref, dq_ref, dk_ref, dv_ref,
             dcq_ref, dckv_ref, dkr_ref, dwq_ref, dwkv_ref, dqnw_ref, dkvnw_ref):
        @pl.when(pl.program_id(0) == 0)
        def _():
            dwq_ref[...] = jnp.zeros_like(dwq_ref)
            dwkv_ref[...] = jnp.zeros_like(dwkv_ref)
            dqnw_ref[...] = jnp.zeros_like(dqnw_ref)
            dkvnw_ref[...] = jnp.zeros_like(dkvnw_ref)

        c, a, b = c_ref[...], a_ref[...], b_ref[...]
        cq = cq_ref[...].astype(F32)
        qn = (cq * lax.rsqrt(jnp.mean(cq * cq, axis=-1, keepdims=True) + EPS) * qnw_ref[...]).astype(BF16)
        ckv = ckv_ref[...].astype(F32)
        kvn = (ckv * lax.rsqrt(jnp.mean(ckv * ckv, axis=-1, keepdims=True) + EPS) * kvnw_ref[...]).astype(BF16)
        lane = lax.broadcasted_iota(jnp.int32, (ts, LANE), 1)
        dqn = jnp.zeros((ts, QL), F32)
        dkvn = jnp.zeros((ts, KVL), F32)
        dkr = jnp.zeros((ts, LANE), F32)
        for h in range(HEADS):
            dqh = _rope_t(dq_ref[h], c, a, b).astype(BF16)
            dwq_ref[h] += _dot(qn, dqh, TN)
            dqn += _dot(dqh, wq_ref[h], NT)
            dkh = dk_ref[h].astype(F32)
            dkvh = jnp.where(lane < NOPE, dkh, dv_ref[h].astype(F32)).astype(BF16)
            dkr += jnp.where(lane < NOPE, 0.0, dkh)
            dwkv_ref[h] += _dot(kvn, dkvh, TN)
            dkvn += _dot(dkvh, wkv_ref[h], NT)
        dkr_ref[...] = _rope_t(dkr, c, a, b).astype(BF16)
        dcq, dwt = _rms_bwd_math(cq, qnw_ref[...], dqn)
        dcq_ref[...] = dcq.astype(BF16)
        dqnw_ref[...] += _rowsum(dwt)
        dckv, dwt = _rms_bwd_math(ckv, kvnw_ref[...], dkvn)
        dckv_ref[...] = dckv.astype(BF16)
        dkvnw_ref[...] += _rowsum(dwt)

    tab = pl.BlockSpec((ts, LANE), lambda i: (i, 0))
    hd = pl.BlockSpec((HEADS, ts, LANE), lambda i: (0, i, 0))
    wq_spec = pl.BlockSpec((HEADS, QL, LANE), lambda i: (0, 0, 0))
    wkv_spec = pl.BlockSpec((HEADS, KVL, LANE), lambda i: (0, 0, 0))
    return pl.pallas_call(
        body, name="mla_prep_bwd", grid=(s // ts,),
        out_shape=(jax.ShapeDtypeStruct((s, QL), BF16), jax.ShapeDtypeStruct((s, KVL), BF16), jax.ShapeDtypeStruct((s, LANE), BF16),
                   jax.ShapeDtypeStruct((HEADS, QL, LANE), F32), jax.ShapeDtypeStruct((HEADS, KVL, LANE), F32),
                   jax.ShapeDtypeStruct((1, QL), F32), jax.ShapeDtypeStruct((1, KVL), F32)),
        in_specs=[pl.BlockSpec((ts, QL), lambda i: (i, 0)), pl.BlockSpec((ts, LANE), lambda i: (i, 2)), tab, tab, tab,
                  pl.BlockSpec((1, QL), lambda i: (0, 0)), pl.BlockSpec((1, KVL), lambda i: (0, 0)), wq_spec, wkv_spec, hd, hd, hd],
        out_specs=(pl.BlockSpec((ts, QL), lambda i: (i, 0)), pl.BlockSpec((ts, KVL), lambda i: (i, 0)), tab, wq_spec, wkv_spec,
                   pl.BlockSpec((1, QL), lambda i: (0, 0)), pl.BlockSpec((1, KVL), lambda i: (0, 0))),
        compiler_params=_cp("arbitrary"),
    )(proj, proj, tc, ta, tb, qnw, kvnw, wq, wkv, dq, dk, dv)


def _transpose_bf16(v):
    return v.astype(F32).T.astype(BF16)


def _flash_fwd(q, k, kv, prefetch=None):
    s = q.shape[1]
    t = min(ATT_TILE, s)
    n = s // t
    g = FWD_HEADS
    nx = len(prefetch) if prefetch else 0

    def body(*refs):
        q_ref, k_ref, kv_ref = refs[:3]
        o_ref, lse_ref = refs[3 + nx:5 + nx]
        kvt_sc = refs[5 + 2 * nx]
        step = pl.program_id(0) * n + pl.program_id(1)
        if nx:
            start, finish = _gather_plan(refs[3:3 + nx], refs[5 + nx:5 + 2 * nx], *refs[6 + 2 * nx:])
            pl.when(step == 0)(start)
        attend(q_ref, k_ref, kv_ref, o_ref, lse_ref, kvt_sc)
        if nx:
            pl.when(step == (HEADS // g) * n - 1)(finish)

    def attend(q_ref, k_ref, kv_ref, o_ref, lse_ref, kvt_sc):
        i = pl.program_id(1)

        @pl.when(i == 0)
        def _():
            for hh in range(g):
                kvt_sc[hh] = _transpose_bf16(kv_ref[hh])

        qt = [_transpose_bf16(q_ref[hh]) for hh in range(g)]
        kpos = lax.broadcasted_iota(jnp.int32, (t, t), 0)
        qpos = lax.broadcasted_iota(jnp.int32, (t, t), 1)

        def chunk(j, carry, diagonal):
            start = pl.multiple_of(j * t, t)
            scs = [_dot(k_ref[hh, pl.ds(start, t), :], qt[hh]) for hh in range(g)]
            soft = []
            for hh in range(g):
                m, l, _ = carry[hh]
                sc = scs[hh] * ATT_SCALE
                if diagonal:
                    sc = jnp.where(qpos >= kpos, sc, NEG)
                m_new = jnp.maximum(m, jnp.max(sc, axis=0, keepdims=True))
                alpha = jnp.exp(m - m_new)
                p = jnp.exp(sc - m_new)
                soft.append((m_new, alpha * l + jnp.sum(p, axis=0, keepdims=True), alpha, p.astype(BF16)))
            pvs = [_dot(kvt_sc[hh, :, pl.ds(start, t)], soft[hh][3]) for hh in range(g)]
            return tuple((soft[hh][0], soft[hh][1], soft[hh][2] * carry[hh][2] + pvs[hh]) for hh in range(g))

        init = tuple((jnp.full((1, t), NEG, F32), jnp.zeros((1, t), F32), jnp.zeros((LANE, t), F32)) for _ in range(g))
        carry = lax.fori_loop(0, i, lambda j, c: chunk(j, c, False), init)
        carry = chunk(i, carry, True)
        for hh in range(g):
            m, l, acc = carry[hh]
            o_ref[:, hh * LANE:(hh + 1) * LANE] = (acc / l).T.astype(BF16)
            lse_ref[hh] = m + jnp.log(l)

    whole = pl.BlockSpec((g, s, LANE), lambda h, i: (h, 0, 0))
    res = pl.pallas_call(
        body, name="flash_fwd_gather" if nx else "flash_fwd", grid=(HEADS // g, n),
        out_shape=[jax.ShapeDtypeStruct((s, HEADS * LANE), BF16), jax.ShapeDtypeStruct((HEADS, 1, s), F32)]
        + [jax.ShapeDtypeStruct((NDEV,) + a.shape, a.dtype) for a in (prefetch or [])],
        in_specs=[pl.BlockSpec((g, t, LANE), lambda h, i: (h, i, 0)), whole, whole] + [ANY] * nx,
        out_specs=[pl.BlockSpec((t, g * LANE), lambda h, i: (i, h)), pl.BlockSpec((g, 1, t), lambda h, i: (h, 0, i))] + [ANY] * nx,
        scratch_shapes=[pltpu.VMEM((g, LANE, s), BF16)] + (_comm_scratch(nx) if nx else []),
        compiler_params=_cp("arbitrary", "arbitrary"),
    )(q, k, kv, *(prefetch or []))
    return res[0], res[1], list(res[2:])


def _flash_bwd(q, k, kv, cat, dcat, lse, pending=None):
    s = q.shape[1]
    t = min(ATT_TILE, s)
    n = s // t
    g = BWD_HEADS
    nx = len(pending) if pending else 0

    def body(*refs):
        ins, outs, scr = refs[:6], refs[6 + nx:9 + nx], refs[9 + 2 * nx:13 + 2 * nx]
        step = pl.program_id(0) * n + pl.program_id(1)
        if nx:
            start, finish = _exchange_plan(refs[6:6 + nx], refs[9 + nx:9 + 2 * nx], *refs[13 + 2 * nx:])
            pl.when(step == 0)(start)
        attend(*ins, *outs, *scr)
        if nx:
            pl.when(step == (HEADS // g) * n - 1)(finish)

    def attend(q_ref, k_ref, kv_ref, o_ref, do_ref, lse_ref, dq_ref, dk_ref, dv_ref, qt_sc, dot_sc, delta_sc, dqt_sc):
        j = pl.program_id(1)

        @pl.when(j == 0)
        def _():
            for hh in range(g):
                lanes = slice(hh * LANE, (hh + 1) * LANE)
                qt_sc[hh] = _transpose_bf16(q_ref[hh])
                dof = do_ref[:, lanes].astype(F32)
                dot_sc[hh] = dof.T.astype(BF16)
                delta_sc[hh] = _dot(jnp.ones((8, LANE), F32), dof * o_ref[:, lanes].astype(F32), NT, precision=HI)
            dqt_sc[...] = jnp.zeros_like(dqt_sc)

        kjt = [_transpose_bf16(k_ref[hh]) for hh in range(g)]
        kpos = lax.broadcasted_iota(jnp.int32, (t, t), 0)
        qpos = lax.broadcasted_iota(jnp.int32, (t, t), 1)

        def chunk(i, carry, diagonal):
            start = pl.multiple_of(i * t, t)
            cols = pl.ds(start, t)
            scs = [_dot(k_ref[hh], qt_sc[hh, :, cols]) for hh in range(g)]
            dps = [_dot(kv_ref[hh], dot_sc[hh, :, cols]) for hh in range(g)]
            pds = []
            for hh in range(g):
                p = jnp.exp(scs[hh] * ATT_SCALE - lse_ref[hh, :, cols])
                if diagonal:
                    p = jnp.where(qpos >= kpos, p, 0.0)
                ds = (p * (dps[hh] - delta_sc[hh, 0:1, cols]) * ATT_SCALE).astype(BF16)
                pds.append((p.astype(BF16), ds))
            out = []
            for hh in range(g):
                dk, dv = carry[hh]
                dv = dv + _dot(pds[hh][0], do_ref[pl.ds(start, t), hh * LANE:(hh + 1) * LANE])
                dk = dk + _dot(pds[hh][1], q_ref[hh, pl.ds(start, t), :])
                dqt_sc[hh, :, cols] += _dot(kjt[hh], pds[hh][1])
                out.append((dk, dv))
            return tuple(out)

        zero = jnp.zeros((t, LANE), F32)
        carry = chunk(j, tuple((zero, zero) for _ in range(g)), True)
        carry = lax.fori_loop(j + 1, n, lambda i, c: chunk(i, c, False), carry)
        for hh in range(g):
            dk_ref[hh] = carry[hh][0].astype(BF16)
            dv_ref[hh] = carry[hh][1].astype(BF16)

        @pl.when(j == n - 1)
        def _():
            for hh in range(g):
                dq_ref[hh] = dqt_sc[hh].T

    whole = pl.BlockSpec((g, s, LANE), lambda h, j: (h, 0, 0))
    kspec = pl.BlockSpec((g, t, LANE), lambda h, j: (h, j, 0))
    ospec = pl.BlockSpec((s, g * LANE), lambda h, j: (0, h))
    res = pl.pallas_call(
        body, name="flash_bwd_exchange" if nx else "flash_bwd", grid=(HEADS // g, n),
        out_shape=[jax.ShapeDtypeStruct((HEADS, s, LANE), F32), jax.ShapeDtypeStruct((HEADS, s, LANE), BF16),
                   jax.ShapeDtypeStruct((HEADS, s, LANE), BF16)] + [jax.ShapeDtypeStruct(a.shape, a.dtype) for a in (pending or [])],
        in_specs=[whole, kspec, kspec, ospec, ospec, pl.BlockSpec((g, 1, s), lambda h, j: (h, 0, 0))] + [ANY] * nx,
        out_specs=[whole, kspec, kspec] + [ANY] * nx,
        scratch_shapes=[pltpu.VMEM((g, LANE, s), BF16), pltpu.VMEM((g, LANE, s), BF16), pltpu.VMEM((g, 8, s), F32),
                        pltpu.VMEM((g, LANE, s), F32)] + (_comm_scratch(nx) if nx else []),
        compiler_params=_cp("arbitrary", "arbitrary"),
    )(q, k, kv, cat, dcat, lse, *(pending or []))
    return res[0], res[1], res[2], list(res[3:])


def _conv3(ext, w_ref, ts):
    return (w_ref[0:1, :] * ext[pl.ds(HALO - 2, ts), :] + w_ref[1:2, :] * ext[pl.ds(HALO - 1, ts), :]
            + w_ref[2:3, :] * ext[pl.ds(HALO, ts), :])


def _conv3_t(ext2, w_ref, ts):
    return (w_ref[0:1, :] * ext2[pl.ds(2, ts), :] + w_ref[1:2, :] * ext2[pl.ds(1, ts), :] + w_ref[2:3, :] * ext2[pl.ds(0, ts), :])


def _sconv_fwd(proj, w):
    s = proj.shape[0]
    ts = min(ROW_TILE, s)

    def body(b_ref, c_ref, h_ref, hc_ref, hh_ref, w_ref, o_ref, ext):
        i = pl.program_id(0)
        ext[0:HALO, :] = hc_ref[...].astype(F32) * hh_ref[...].astype(F32) * (i > 0).astype(F32)
        ext[HALO:HALO + ts, :] = c_ref[...].astype(F32) * h_ref[...].astype(F32)
        o_ref[...] = (b_ref[...].astype(F32) * _conv3(ext, w_ref, ts)).astype(BF16)

    def col(cb):
        return pl.BlockSpec((ts, SC), lambda i: (i, cb))

    def halo(cb):
        return pl.BlockSpec((HALO, SC), lambda i: (_prev_halo(i, ts), cb))

    return pl.pallas_call(
        body, name="sconv_fwd", grid=(s // ts,), out_shape=jax.ShapeDtypeStruct((s, SC), BF16),
        in_specs=[col(2), col(3), col(4), halo(3), halo(4), pl.BlockSpec((3, SC), lambda i: (0, 0))],
        out_specs=pl.BlockSpec((ts, SC), lambda i: (i, 0)), scratch_shapes=[pltpu.VMEM((ts + HALO, SC), F32)],
        compiler_params=_cp("parallel"),
    )(proj, proj, proj, proj, proj, w)


def _sconv_bwd(proj, dcat, w):
    s = proj.shape[0]
    ts = min(ROW_TILE, s)
    n = s // ts

    def body(b_ref, c_ref, h_ref, hc_ref, hh_ref, dy_ref, ndy_ref, nb_ref, w_ref, db_ref, dc_ref, dh_ref, dw_ref, ext, ext2):
        i = pl.program_id(0)

        @pl.when(i == 0)
        def _():
            dw_ref[...] = jnp.zeros_like(dw_ref)

        cv, hv, bv = c_ref[...].astype(F32), h_ref[...].astype(F32), b_ref[...].astype(F32)
        ext[0:HALO, :] = hc_ref[...].astype(F32) * hh_ref[...].astype(F32) * (i > 0).astype(F32)
        ext[HALO:HALO + ts, :] = cv * hv
        dy = dy_ref[...].astype(F32)
        db_ref[...] = (dy * _conv3(ext, w_ref, ts)).astype(BF16)
        dyb = dy * bv
        ext2[0:ts, :] = dyb
        ext2[ts:ts + HALO, :] = ndy_ref[...].astype(F32) * nb_ref[...].astype(F32) * (i < n - 1).astype(F32)
        dg = _conv3_t(ext2, w_ref, ts)
        dc_ref[...] = (dg * hv).astype(BF16)
        dh_ref[...] = (dg * cv).astype(BF16)
        for kk in range(3):
            dw_ref[kk:kk + 1, :] += _rowsum(dyb * ext[pl.ds(HALO - 2 + kk, ts), :])

    def col(cb):
        return pl.BlockSpec((ts, SC), lambda i: (i, cb))

    def halo(cb):
        return pl.BlockSpec((HALO, SC), lambda i: (_prev_halo(i, ts), cb))

    def nxt(cb):
        return pl.BlockSpec((HALO, SC), lambda i: (_next_halo(i, ts, n), cb))

    out = jax.ShapeDtypeStruct((s, SC), BF16)
    o0 = pl.BlockSpec((ts, SC), lambda i: (i, 0))
    return pl.pallas_call(
        body, name="sconv_bwd", grid=(n,), out_shape=(out, out, out, jax.ShapeDtypeStruct((3, SC), F32)),
        in_specs=[col(2), col(3), col(4), halo(3), halo(4), col(4), nxt(4), nxt(2), pl.BlockSpec((3, SC), lambda i: (0, 0))],
        out_specs=(o0, o0, o0, pl.BlockSpec((3, SC), lambda i: (0, 0))),
        scratch_shapes=[pltpu.VMEM((ts + HALO, SC), F32), pltpu.VMEM((ts + HALO, SC), F32)], compiler_params=_cp("arbitrary"),
    )(proj, proj, proj, proj, proj, dcat, dcat, proj, w)


def _ffn_stage(ext, u_ref, halo_ref, i, ts):
    ext[0:HALO, :] = halo_ref[...].astype(F32) * (i > 0).astype(F32)
    ext[HALO:HALO + ts, :] = u_ref[...].astype(F32)


def _ffn_specs(ts):
    cur = pl.BlockSpec((2, None, ts, FB), lambda j, i: (0, j, i, 0))
    halo = pl.BlockSpec((2, None, HALO, FB), lambda j, i: (0, j, _prev_halo(i, ts), 0))
    w = pl.BlockSpec((2, None, 3, FB), lambda j, i: (0, j, 0, 0))
    b = pl.BlockSpec((2, None, 1, FB), lambda j, i: (0, j, 0, 0))
    return cur, halo, w, b


def _ffn_act(upre, fcw, fcb):
    s = upre.shape[1]
    ts = min(ROW_TILE, s)

    def body(u_ref, halo_ref, w_ref, b_ref, o_ref, ext):
        i = pl.program_id(1)
        _ffn_stage(ext, u_ref.at[0], halo_ref.at[0], i, ts)
        gate = b_ref[0] + _conv3(ext, w_ref.at[0], ts)
        _ffn_stage(ext, u_ref.at[1], halo_ref.at[1], i, ts)
        up = b_ref[1] + _conv3(ext, w_ref.at[1], ts)
        o_ref[...] = (gate * _sig(gate) * up).astype(BF16)

    cur, halo, w, b = _ffn_specs(ts)
    u4 = upre.reshape(2, 4, s, FB)
    return pl.pallas_call(
        body, name="ffn_act", grid=(4, s // ts), out_shape=jax.ShapeDtypeStruct((4, s, FB), BF16),
        in_specs=[cur, halo, w, b], out_specs=pl.BlockSpec((None, ts, FB), lambda j, i: (j, i, 0)),
        scratch_shapes=[pltpu.VMEM((ts + HALO, FB), F32)], compiler_params=_cp("parallel", "parallel"),
    )(u4, u4, fcw.reshape(2, 4, 3, FB), fcb.reshape(2, 4, 1, FB))


def _ffn_bwd_a(upre, dact, fcw, fcb):
    s = upre.shape[1]
    ts = min(ROW_TILE, s)

    def body(u_ref, halo_ref, w_ref, b_ref, da_ref, du_ref, db_ref, ext):
        i = pl.program_id(1)

        @pl.when(i == 0)
        def _():
            db_ref[...] = jnp.zeros_like(db_ref)

        _ffn_stage(ext, u_ref.at[0], halo_ref.at[0], i, ts)
        gate = b_ref[0] + _conv3(ext, w_ref.at[0], ts)
        _ffn_stage(ext, u_ref.at[1], halo_ref.at[1], i, ts)
        up = b_ref[1] + _conv3(ext, w_ref.at[1], ts)
        sg = _sig(gate)
        da = da_ref[...].astype(F32)
        dgate = da * up * sg * (1.0 + gate * (1.0 - sg))
        dup = da * gate * sg
        du_ref[0] = dgate.astype(BF16)
        du_ref[1] = dup.astype(BF16)
        db_ref[0] += _rowsum(dgate)
        db_ref[1] += _rowsum(dup)

    cur, halo, w, b = _ffn_specs(ts)
    u4 = upre.reshape(2, 4, s, FB)
    du, db = pl.pallas_call(
        body, name="ffn_bwd_a", grid=(4, s // ts),
        out_shape=(jax.ShapeDtypeStruct((2, 4, s, FB), BF16), jax.ShapeDtypeStruct((2, 4, 1, FB), F32)),
        in_specs=[cur, halo, w, b, pl.BlockSpec((None, ts, FB), lambda j, i: (j, i, 0))], out_specs=(cur, b),
        scratch_shapes=[pltpu.VMEM((ts + HALO, FB), F32)], compiler_params=_cp("parallel", "arbitrary"),
    )(u4, u4, fcw.reshape(2, 4, 3, FB), fcb.reshape(2, 4, 1, FB), dact)
    return du.reshape(NDEV, s, FB), db.reshape(NDEV, 1, FB)


def _ffn_bwd_b(du, upre, fcw):
    s = upre.shape[1]
    ts = min(ROW_TILE, s)
    n = s // ts

    def body(du_ref, ndu_ref, u_ref, halo_ref, w_ref, dup_ref, dw_ref, ext, ext2):
        i = pl.program_id(1)

        @pl.when(i == 0)
        def _():
            dw_ref[...] = jnp.zeros_like(dw_ref)

        duv = du_ref[...].astype(F32)
        ext2[0:ts, :] = duv
        ext2[ts:ts + HALO, :] = ndu_ref[...].astype(F32) * (i < n - 1).astype(F32)
        dup_ref[...] = _conv3_t(ext2, w_ref, ts).astype(BF16)
        _ffn_stage(ext, u_ref, halo_ref, i, ts)
        for kk in range(3):
            dw_ref[kk:kk + 1, :] += _rowsum(duv * ext[pl.ds(HALO - 2 + kk, ts), :])

    cur = pl.BlockSpec((None, ts, FB), lambda j, i: (j, i, 0))
    w = pl.BlockSpec((None, 3, FB), lambda j, i: (j, 0, 0))
    return pl.pallas_call(
        body, name="ffn_bwd_b", grid=(NDEV, n),
        out_shape=(jax.ShapeDtypeStruct((NDEV, s, FB), BF16), jax.ShapeDtypeStruct((NDEV, 3, FB), F32)),
        in_specs=[cur, pl.BlockSpec((None, HALO, FB), lambda j, i: (j, _next_halo(i, ts, n), 0)), cur,
                  pl.BlockSpec((None, HALO, FB), lambda j, i: (j, _prev_halo(i, ts), 0)), w],
        out_specs=(cur, w), scratch_shapes=[pltpu.VMEM((ts + HALO, FB), F32), pltpu.VMEM((ts + HALO, FB), F32)],
        compiler_params=_cp("parallel", "arbitrary"),
    )(du, du, upre, upre, fcw)


def _softplus(v):
    e = jnp.exp(-jnp.abs(v))
    return jnp.maximum(v, 0.0) + jnp.where(e < 1e-4, e * (1.0 - 0.5 * e), jnp.log(1.0 + e))


def _ssd_consts():
    L = SSD_L
    r = lax.broadcasted_iota(jnp.int32, (L, L), 0)
    c = lax.broadcasted_iota(jnp.int32, (L, L), 1)
    tri = r >= c
    er = lax.broadcasted_iota(jnp.int32, (LANE, SSD_DIM), 0)
    ec = lax.broadcasted_iota(jnp.int32, (LANE, SSD_DIM), 1)
    expand = ((ec >= er * 64) & (ec < er * 64 + 64)).astype(F32)
    return tri, expand


def _ssd_conv4(ext, cw_ref, cb_ref):
    L = SSD_L
    pre = cb_ref[...] + cw_ref[0:1, :] * ext[pl.ds(HALO - 3, L), :]
    for kk in range(1, 4):
        pre = pre + cw_ref[kk:kk + 1, :] * ext[pl.ds(HALO - 3 + kk, L), :]
    return pre


def _ssd_common(xbc_ref, halo_ref, dt_ref, cw_ref, cb_ref, dtb_ref, alog_ref, ext, first):
    L = SSD_L
    tri, expand = _ssd_consts()
    ext[0:HALO, :] = halo_ref[...].astype(F32) * (1.0 - first.astype(F32))
    ext[HALO:HALO + L, :] = xbc_ref[...].astype(F32)
    pre = _ssd_conv4(ext, cw_ref, cb_ref)
    sg = _sig(pre)
    act = pre * sg
    lane = lax.broadcasted_iota(jnp.int32, (1, LANE), 1)
    m4 = lane < SSD_H
    raw = dt_ref[...].astype(F32) + dtb_ref[...]
    dtv = jnp.where(m4, _softplus(raw), 0.0)
    av = jnp.where(m4, -jnp.exp(alog_ref[...]), 0.0)
    adt = dtv * av
    acs = _dot(tri.astype(F32), adt, precision=HI)
    acs_b = _dot(acs, expand, precision=HI)
    dt_b = _dot(dtv, expand, precision=HI)
    return dict(tri=tri, expand=expand, pre=pre, sg=sg, act=act, raw=raw, dtv=dtv, av=av, m4=m4, acs=acs, acs_b=acs_b,
                dt_b=dt_b, lane=lane)


def _head_terms(cm, h):
    L = SSD_L
    acs, tri = cm["acs"], cm["tri"]
    lane_l = lax.broadcasted_iota(jnp.int32, (L, LANE), 1)
    sub_l = lax.broadcasted_iota(jnp.int32, (LANE, L), 0)
    col = jnp.sum(jnp.where(lane_l == h, acs, 0.0), axis=1, keepdims=True)
    row = jnp.sum(jnp.where(sub_l == h, acs.T, 0.0), axis=0, keepdims=True)
    dec = jnp.where(tri, jnp.exp(jnp.where(tri, col - row, NEG)), 0.0)
    rowi = lax.broadcasted_iota(jnp.int32, (L, 1), 0)
    last = jnp.sum(jnp.where(rowi == L - 1, col, 0.0), axis=0, keepdims=True)
    dte = jnp.exp(last - col)
    return col, dec, last, dte


def _ssd_fwd(proj, cw, cb, dtb, alog, dvec, nw):
    s = proj.shape[0]
    L = SSD_L
    nc = s // L

    def body(z_ref, xbc_ref, halo_ref, dt_ref, cw_ref, cb_ref, dtb_ref, alog_ref, d_ref, nw_ref, y_ref, ypre_ref, st_ref, ext, state):
        i = pl.program_id(0)

        @pl.when(i == 0)
        def _():
            state[...] = jnp.zeros_like(state)

        cm = _ssd_common(xbc_ref, halo_ref, dt_ref, cw_ref, cb_ref, dtb_ref, alog_ref, ext, i == 0)
        act = cm["act"]
        xs = act[:, 0:256]
        bm = (act[:, 256:384], act[:, 384:512])
        cmat = (act[:, 512:640].astype(BF16), act[:, 640:768].astype(BF16))
        xdt = xs * cm["dt_b"]
        prev = state[...]
        st_ref[...] = prev
        prev_bf = prev.astype(BF16)
        gm = [_dot(cmat[g], bm[g].astype(BF16), NT) for g in range(2)]
        lane2 = lax.broadcasted_iota(jnp.int32, (1, SSD_DIM), 1)
        rows2 = lax.broadcasted_iota(jnp.int32, (SSD_DIM, 1), 0)
        ydiag = jnp.zeros((L, SSD_DIM), F32)
        contrib = jnp.zeros((SSD_DIM, LANE), F32)
        cd_rows = jnp.zeros((SSD_DIM, 1), F32)
        for h in range(SSD_H):
            g = h // 2
            col, dec, last, dte = _head_terms(cm, h)
            mh = (lane2 >= 64 * h) & (lane2 < 64 * h + 64)
            xm = jnp.where(mh, xdt, 0.0).astype(BF16)
            ydiag += _dot((gm[g] * dec).astype(BF16), xm)
            contrib += _dot(xm, (bm[g] * dte).astype(BF16), TN)
            cd_rows += jnp.where((rows2 >= 64 * h) & (rows2 < 64 * h + 64), jnp.exp(last), 0.0)
        yo = jnp.where(lane2 < 128, _dot(cmat[0], prev_bf, NT), _dot(cmat[1], prev_bf, NT))
        y = ydiag + yo * jnp.exp(cm["acs_b"]) + xs * d_ref[...]
        state[...] = prev * cd_rows + contrib
        ypre_ref[...] = y
        zz = z_ref[...].astype(F32)
        gt = y * zz * _sig(zz)
        y_ref[...] = (gt * lax.rsqrt(jnp.mean(gt * gt, axis=-1, keepdims=True) + EPS) * nw_ref[...]).astype(BF16)

    def vec(w):
        return pl.BlockSpec((1, w), lambda i: (0, 0))

    return pl.pallas_call(
        body, name="ssd_fwd", grid=(nc,),
        out_shape=(jax.ShapeDtypeStruct((s, SSD_DIM), BF16), jax.ShapeDtypeStruct((s, SSD_DIM), F32),
                   jax.ShapeDtypeStruct((nc, SSD_DIM, LANE), F32)),
        in_specs=[pl.BlockSpec((L, SSD_DIM), lambda i: (i, 5)), pl.BlockSpec((L, SSD_CONV), lambda i: (i, 2)),
                  pl.BlockSpec((HALO, SSD_CONV), lambda i: (_prev_halo(i, L), 2)), pl.BlockSpec((L, LANE), lambda i: (i, 18)),
                  pl.BlockSpec((4, SSD_CONV), lambda i: (0, 0)), vec(SSD_CONV), vec(LANE), vec(LANE), vec(SSD_DIM), vec(SSD_DIM)],
        out_specs=(pl.BlockSpec((L, SSD_DIM), lambda i: (i, 0)), pl.BlockSpec((L, SSD_DIM), lambda i: (i, 0)),
                   pl.BlockSpec((None, SSD_DIM, LANE), lambda i: (i, 0, 0))),
        scratch_shapes=[pltpu.VMEM((L + HALO, SSD_CONV), F32), pltpu.VMEM((SSD_DIM, LANE), F32)], compiler_params=_cp("arbitrary"),
    )(proj, proj, proj, proj, cw, cb, dtb, alog, dvec, nw)


def _ssd_bwd(proj, dcat, ypre, states, cw, cb, dtb, alog, dvec, nw):
    s = proj.shape[0]
    L = SSD_L
    nc = s // L

    def body(z_ref, xbc_ref, halo_ref, dt_ref, dy_ref, ypre_ref, st_ref, cw_ref, cb_ref, dtb_ref, alog_ref, d_ref, nw_ref,
             dz_ref, dxbc_ref, ddt_ref, dcw_ref, dcb_ref, ddtb_ref, dalog_ref, dd_ref, dnw_ref, ext, ext2, carry, dstate, ddl):
        i = pl.program_id(0)
        r = nc - 1 - i

        @pl.when(i == 0)
        def _():
            for ref in (dcw_ref, dcb_ref, ddtb_ref, dalog_ref, dd_ref, dnw_ref, carry, dstate, ddl):
                ref[...] = jnp.zeros_like(ref)

        cm = _ssd_common(xbc_ref, halo_ref, dt_ref, cw_ref, cb_ref, dtb_ref, alog_ref, ext, r == 0)
        tri, expand, act = cm["tri"], cm["expand"], cm["act"]
        xs = act[:, 0:256]
        bm = (act[:, 256:384], act[:, 384:512])
        cmat = (act[:, 512:640], act[:, 640:768])
        bm_bf = [v.astype(BF16) for v in bm]
        cm_bf = [v.astype(BF16) for v in cmat]
        dt_b = cm["dt_b"]
        xdt = xs * dt_b
        xdt_bf = xdt.astype(BF16)
        ea_b = jnp.exp(cm["acs_b"])
        prev = st_ref[...]
        prev_bf = prev.astype(BF16)
        lane2 = lax.broadcasted_iota(jnp.int32, (1, SSD_DIM), 1)
        rows2 = lax.broadcasted_iota(jnp.int32, (SSD_DIM, 1), 0)
        lane_l = lax.broadcasted_iota(jnp.int32, (L, LANE), 1)
        rowi = lax.broadcasted_iota(jnp.int32, (L, 1), 0)

        y = ypre_ref[...]
        zz = z_ref[...].astype(F32)
        sz = _sig(zz)
        gt = y * zz * sz
        dgt, dwt = _rms_bwd_math(gt, nw_ref[...], dy_ref[...].astype(F32))
        dnw_ref[...] += _rowsum(dwt)
        dy = dgt * zz * sz
        dz_ref[...] = (dgt * y * sz * (1.0 + zz * (1.0 - sz))).astype(BF16)

        ddl[0:1, :] += _rowsum(dy * xs)
        dxs = dy * d_ref[...]

        yo = jnp.where(lane2 < 128, _dot(cm_bf[0], prev_bf, NT), _dot(cm_bf[1], prev_bf, NT))
        dacs_b = dy * yo * ea_b
        dyo = dy * ea_b
        dyo_g = (jnp.where(lane2 < 128, dyo, 0.0).astype(BF16), jnp.where(lane2 >= 128, dyo, 0.0).astype(BF16))
        dc = [_dot(dyo_g[g], prev_bf) for g in range(2)]
        dprev = _dot(dyo_g[0], cm_bf[0], TN) + _dot(dyo_g[1], cm_bf[1], TN)

        gm = [_dot(cm_bf[g], bm_bf[g], NT) for g in range(2)]
        dgm = [jnp.zeros((L, L), F32), jnp.zeros((L, L), F32)]
        db = [jnp.zeros((L, LANE), F32), jnp.zeros((L, LANE), F32)]
        dxdt = jnp.zeros((L, SSD_DIM), F32)
        dacs = jnp.zeros((L, LANE), F32)
        dlast = jnp.zeros((1, LANE), F32)
        cd_rows = jnp.zeros((SSD_DIM, 1), F32)
        dst = dstate[...]
        dst_bf = dst.astype(BF16)
        dsp = dst * prev
        ones = jnp.ones((L, LANE), F32)
        for h in range(SSD_H):
            g = h // 2
            col, dec, last, dte = _head_terms(cm, h)
            mh = (lane2 >= 64 * h) & (lane2 < 64 * h + 64)
            rh = (rows2 >= 64 * h) & (rows2 < 64 * h + 64)
            sc = gm[g] * dec
            xm = jnp.where(mh, xdt, 0.0).astype(BF16)
            dym = jnp.where(mh, dy, 0.0).astype(BF16)
            dsc = _dot(dym, xdt_bf, NT)
            dxdt += _dot(sc.astype(BF16), dym, TN)
            dgm[g] += dsc * dec
            dd = dsc * sc
            rs = jnp.sum(dd, axis=1, keepdims=True)
            cs = _dot(dd, ones, TN, precision=HI)
            dacs += jnp.where(lane_l == h, rs - cs, 0.0)
            bd = (bm[g] * dte).astype(BF16)
            dxdt += jnp.where(mh, _dot(bd, dst_bf, NT), 0.0)
            dbd = _dot(xm, dst_bf)
            db[g] += dbd * dte
            tt = jnp.sum(dbd * bm[g], axis=1, keepdims=True) * dte
            dacs += jnp.where(lane_l == h, -tt, 0.0)
            cdh = jnp.exp(last)
            dcd = jnp.sum(jnp.sum(jnp.where(rh, dsp, 0.0), axis=1, keepdims=True), axis=0, keepdims=True)
            dlast += jnp.where(cm["lane"] == h, jnp.sum(tt, axis=0, keepdims=True) + dcd * cdh, 0.0)
            cd_rows += jnp.where(rh, cdh, 0.0)
        dacs += jnp.where(rowi == L - 1, dlast, 0.0)
        dacs += _dot(dacs_b, expand, NT, precision=HI)
        dstate[...] = dprev + dst * cd_rows

        for g in range(2):
            dgb = dgm[g].astype(BF16)
            dc[g] += _dot(dgb, bm_bf[g])
            db[g] += _dot(dgb, cm_bf[g], TN)

        dadt = _dot(tri.astype(F32), dacs, TN, precision=HI)
        ddtv = dadt * cm["av"] + _dot(dxdt * xs, expand, NT, precision=HI)
        dalog_ref[...] += _rowsum(dadt * cm["dtv"]) * cm["av"]
        dxs += dxdt * dt_b
        draw = jnp.where(cm["m4"], ddtv * _sig(cm["raw"]), 0.0)
        ddtb_ref[...] += _rowsum(draw)
        ddt_ref[...] = draw.astype(BF16)

        dact = jnp.concatenate([dxs, db[0], db[1], dc[0], dc[1]], axis=1)
        sg, pre = cm["sg"], cm["pre"]
        dpre = dact * sg * (1.0 + pre * (1.0 - sg))
        dcb_ref[...] += _rowsum(dpre)
        for kk in range(4):
            dcw_ref[kk:kk + 1, :] += _rowsum(dpre * ext[pl.ds(HALO - 3 + kk, L), :])
        ext2[0:L, :] = dpre
        ext2[L:L + HALO, :] = carry[...]
        dx = cw_ref[3:4, :] * ext2[pl.ds(0, L), :]
        for kk in range(3):
            dx = dx + cw_ref[kk:kk + 1, :] * ext2[pl.ds(3 - kk, L), :]
        dxbc_ref[...] = dx.astype(BF16)
        carry[...] = dpre[0:HALO, :]

        @pl.when(i == nc - 1)
        def _():
            dd_ref[...] = _dot(ddl[...], expand, NT, precision=HI)

    def vec(w):
        return pl.BlockSpec((1, w), lambda i: (0, 0))

    def rv(i):
        return nc - 1 - i

    return pl.pallas_call(
        body, name="ssd_bwd", grid=(nc,),
        out_shape=(jax.ShapeDtypeStruct((s, SSD_DIM), BF16), jax.ShapeDtypeStruct((s, SSD_CONV), BF16), jax.ShapeDtypeStruct((s, LANE), BF16),
                   jax.ShapeDtypeStruct((4, SSD_CONV), F32), jax.ShapeDtypeStruct((1, SSD_CONV), F32), jax.ShapeDtypeStruct((1, LANE), F32),
                   jax.ShapeDtypeStruct((1, LANE), F32), jax.ShapeDtypeStruct((8, LANE), F32), jax.ShapeDtypeStruct((1, SSD_DIM), F32)),
        in_specs=[pl.BlockSpec((L, SSD_DIM), lambda i: (rv(i), 5)), pl.BlockSpec((L, SSD_CONV), lambda i: (rv(i), 2)),
                  pl.BlockSpec((HALO, SSD_CONV), lambda i: (_prev_halo(rv(i), L), 2)), pl.BlockSpec((L, LANE), lambda i: (rv(i), 18)),
                  pl.BlockSpec((L, SSD_DIM), lambda i: (rv(i), 5)), pl.BlockSpec((L, SSD_DIM), lambda i: (rv(i), 0)),
                  pl.BlockSpec((None, SSD_DIM, LANE), lambda i: (rv(i), 0, 0)),
                  pl.BlockSpec((4, SSD_CONV), lambda i: (0, 0)), vec(SSD_CONV), vec(LANE), vec(LANE), vec(SSD_DIM), vec(SSD_DIM)],
        out_specs=(pl.BlockSpec((L, SSD_DIM), lambda i: (rv(i), 0)), pl.BlockSpec((L, SSD_CONV), lambda i: (rv(i), 0)),
                   pl.BlockSpec((L, LANE), lambda i: (rv(i), 0)), pl.BlockSpec((4, SSD_CONV), lambda i: (0, 0)), vec(SSD_CONV),
                   vec(LANE), vec(LANE), pl.BlockSpec((8, LANE), lambda i: (0, 0)), vec(SSD_DIM)),
        scratch_shapes=[pltpu.VMEM((L + HALO, SSD_CONV), F32), pltpu.VMEM((L + HALO, SSD_CONV), F32), pltpu.VMEM((HALO, SSD_CONV), F32),
                        pltpu.VMEM((SSD_DIM, LANE), F32), pltpu.VMEM((8, SSD_DIM), F32)],
        compiler_params=_cp("arbitrary"),
    )(proj, proj, proj, proj, dcat, ypre, states, cw, cb, dtb, alog, dvec, nw)


def _adamw(parts, w, m, v, name):
    r, c = w.shape
    tr = r
    for cand in (256, 128, 64, 32, 16, 8):
        if r % cand == 0 and (cand * c * 4) <= 2 * 1024 * 1024:
            tr = cand
            break
    c1 = 1.0 - B1 ** STEP
    c2 = 1.0 - B2 ** STEP

    def body(p_ref, w_ref, m_ref, v_ref, g_ref, d_ref, nm_ref, nv_ref):
        g = p_ref[0].astype(F32)
        for dev in range(1, NDEV):
            g = g + p_ref[dev].astype(F32)
        mn = B1 * m_ref[...] + (1.0 - B1) * g
        vn = B2 * v_ref[...] + (1.0 - B2) * (g * g)
        g_ref[...] = g
        nm_ref[...] = mn
        nv_ref[...] = vn
        d_ref[...] = -LR * ((mn / c1) / (jnp.sqrt(vn / c2) + AEPS) + WD * w_ref[...])

    blk = pl.BlockSpec((tr, c), lambda i: (i, 0))
    out = jax.ShapeDtypeStruct((r, c), F32)
    return pl.pallas_call(
        body, name=name, grid=(r // tr,), out_shape=(out, out, out, out),
        in_specs=[pl.BlockSpec((NDEV, tr, c), lambda i: (0, i, 0)), blk, blk, blk], out_specs=(blk, blk, blk, blk),
        compiler_params=_cp("parallel"),
    )(parts, w, m, v)


def _pad_win(w):
    z = lambda n: jnp.zeros((w.shape[0], n), w.dtype)
    return jnp.concatenate([w[:, :384], z(64), w[:, 384:416], z(32), w[:, 416:], z(124)], axis=1)


def _unpad_win(g):
    return jnp.concatenate([g[:, :384], g[:, 448:480], g[:, 512:2308]], axis=1)


def _pad_wout(w):
    att = jnp.pad(w[:512].reshape(HEADS, 64, D), ((0, 0), (64, 0), (0, 0))).reshape(HEADS * LANE, D)
    return jnp.concatenate([att, w[512:]], axis=0)


def _unpad_wout(g):
    att = g[:HEADS * LANE].reshape(HEADS, LANE, D)[:, 64:, :].reshape(512, D)
    return jnp.concatenate([att, g[HEADS * LANE:]], axis=0)


def _lanes(v, n=LANE):
    return jnp.pad(v, (0, n - v.shape[0])).reshape(1, n)


def _prep_ffn(big):
    return {"wout": _pad_wout(big["w_out"].reshape(1024, D)), "wup": big["ffn_w_up"], "fcw": big["ffn_conv_w"].astype(F32),
            "wdown": big["ffn_w_down"].reshape(4, FB, D)}


def _prep_layer(big, small, l):
    p = _prep_ffn(big) if "w_out" in big else {}
    p["win"] = _pad_win(big["w_in"].reshape(D, 2212))
    p["wq"] = jnp.pad(big["mla_w_q_up"], ((0, 0), (0, 0), (0, LANE - 96)))
    p["wkv"] = big["mla_w_kv_up"]
    p["scw"] = big["sc_conv_w"].astype(F32).transpose(1, 0, 2).reshape(3, SC)
    p["ssdcw"] = big["ssd_conv_w"].astype(F32).transpose(1, 0, 2).reshape(4, SSD_CONV)
    for nm in ("norm_mix_pre", "norm_mix_post", "norm_ffn_pre", "norm_ffn_post", "mla_q_norm", "mla_kv_norm", "ssd_conv_b", "ssd_norm"):
        p[nm] = small[nm][l].reshape(1, -1)
    p["dtb"] = _lanes(small["ssd_dt_bias"][l])
    p["alog"] = _lanes(small["ssd_a_log"][l])
    p["dvec"] = jnp.repeat(small["ssd_d"][l], 64).reshape(1, SSD_DIM)
    p["fcb"] = small["ffn_conv_b"][l].reshape(NDEV, 1, FB)
    return p


def _rope_tables(positions):
    inv_freq = 1.0 / (ROPE_THETA ** (jnp.arange(0, ROPE, 2, dtype=F32) / ROPE))
    ang = positions.astype(F32)[:, None] * inv_freq
    cos, sin = jnp.cos(ang), jnp.sin(ang)
    s = positions.shape[0]
    z = lambda n: jnp.zeros((s, n), F32)
    tc = jnp.concatenate([jnp.ones((s, 64), F32), cos, cos, z(32)], axis=1)
    ta = jnp.concatenate([z(64), -sin, z(48)], axis=1)
    tb = jnp.concatenate([z(80), sin, z(32)], axis=1)
    return tc, ta, tb


def _layer_fwd(xv, p, tabs, prefetch=None, prep_rest=None):
    h = _rms(xv, p["norm_mix_pre"], BF16, "rms_pre")
    proj = _mm_rows("in_proj", h, p["win"], BF16, NN)
    q, k, kv = _mla_prep(proj, tabs, p["mla_q_norm"], p["mla_kv_norm"], p["wq"], p["wkv"])
    o, lse, gathered = _flash_fwd(q, k, kv, prefetch)
    if prep_rest is not None:
        p = {**p, **prep_rest(gathered)}
    yconv = _sconv_fwd(proj, p["scw"])
    yssd, ypre, states = _ssd_fwd(proj, p["ssdcw"], p["ssd_conv_b"], p["dtb"], p["alog"], p["dvec"], p["ssd_norm"])
    cat = jnp.concatenate([o, yconv, yssd], axis=1)
    mixed = _mm_rows("out_proj", cat, p["wout"], F32, NN)
    x1 = _add_rms(xv, mixed, p["norm_mix_post"], "add_rms")
    h2 = _rms(x1, p["norm_ffn_pre"], BF16, "rms_pre")
    upre = _mm_up(h2, p["wup"])
    act = _ffn_act(upre, p["fcw"], p["fcb"])
    f = _mm_down(act, p["wdown"])
    x2 = _add_rms(x1, f, p["norm_ffn_post"], "add_rms")
    saved = dict(x=xv, h=h, proj=proj, q=q, k=k, kv=kv, lse=lse, ypre=ypre, states=states, cat=cat, mixed=mixed, x1=x1, h2=h2,
                 upre=upre, act=act, f=f)
    return x2, saved, p, gathered


def _pack_grads(grads, group_ids):
    return [_group_pack(GROUPS[gi], lambda n: grads[n].reshape((NDEV,) + _rows2(n, True)), (NDEV,)) for gi in group_ids]


def _layer_bwd(dx2, sv, p, tabs, exchange=False, pending=None):
    df, g_nfpo = _rms_bwd(sv["f"], p["norm_ffn_post"], dx2, None, BF16, "rms_bwd_post")
    dact = _mm_dact(df, p["wdown"])
    g_wdown = _mm_dwdown(sv["act"], df)
    du, g_fcb = _ffn_bwd_a(sv["upre"], dact, p["fcw"], p["fcb"])
    dupre, g_fcw = _ffn_bwd_b(du, sv["upre"], p["fcw"])
    dh2 = _mm_dh2(dupre, p["wup"])
    g_wup = _mm_dwup(sv["h2"], dupre)
    dx1, g_nfp = _rms_bwd(sv["x1"], p["norm_ffn_pre"], dh2, dx2, F32, "rms_bwd_pre")
    dmixed, g_nmpo = _rms_bwd(sv["mixed"], p["norm_mix_post"], dx1, None, BF16, "rms_bwd_post")
    dcat = _mm_rows("dcat", dmixed, p["wout"], BF16, NT)
    g_wout = _mm_wgrad("dw_out", sv["cat"], dmixed, BF16)
    big = {
        "w_out": _unpad_wout(g_wout).reshape(NDEV, 128, D),
        "ffn_w_up": g_wup,
        "ffn_conv_w": g_fcw.astype(BF16),
        "ffn_w_down": g_wdown.reshape(NDEV, 352, D),
    }
    outgoing = _pack_grads(big, FFN_SIDE) + (pending or []) if exchange else None
    dq, dk, dv, received = _flash_bwd(sv["q"], sv["k"], sv["kv"], sv["cat"], dcat, sv["lse"], outgoing)
    dcq, dckv, dkr, g_wq, g_wkv, g_qn, g_kvn = _mla_prep_bwd(sv["proj"], tabs, p["mla_q_norm"], p["mla_kv_norm"], p["wq"], p["wkv"], dq, dk, dv)
    dscb, dscc, dsch, g_scw = _sconv_bwd(sv["proj"], dcat, p["scw"])
    dz, dxbc, ddt, g_cw, g_cb, g_dtb, g_alog, g_d, g_nw = _ssd_bwd(
        sv["proj"], dcat, sv["ypre"], sv["states"], p["ssdcw"], p["ssd_conv_b"], p["dtb"], p["alog"], p["dvec"], p["ssd_norm"])
    dproj = jnp.concatenate([dcq, dckv, dkr, dscb, dscc, dsch, dz, dxbc, ddt], axis=1)
    dh = _mm_rows("dh", dproj, p["win"], BF16, NT)
    g_win = _mm_wgrad("dw_in", sv["h"], dproj, BF16)
    dx, g_nmp = _rms_bwd(sv["x"], p["norm_mix_pre"], dh, dx1, F32, "rms_bwd_pre")
    big.update({
        "w_in": _unpad_win(g_win).reshape(NDEV, 128, 2212),
        "mla_w_q_up": g_wq[:, :, :96].astype(BF16),
        "mla_w_kv_up": g_wkv.astype(BF16),
        "sc_conv_w": g_scw.reshape(3, NDEV, 32).transpose(1, 0, 2).astype(BF16),
        "ssd_conv_w": g_cw.reshape(4, NDEV, 96).transpose(1, 0, 2).astype(BF16),
    })
    small = {
        "norm_mix_pre": g_nmp[0], "norm_mix_post": g_nmpo[0], "norm_ffn_pre": g_nfp[0], "norm_ffn_post": g_nfpo[0],
        "mla_q_norm": g_qn[0], "mla_kv_norm": g_kvn[0], "ssd_conv_b": g_cb[0], "ssd_dt_bias": g_dtb[0, :SSD_H],
        "ssd_a_log": g_alog[0, :SSD_H], "ssd_d": g_d[0, :SSD_H], "ssd_norm": g_nw[0], "ffn_conv_b": g_fcb.reshape(-1),
    }
    return dx, big, small, received


def _local_step(xv, positions, target, layers):
    tabs = _rope_tables(positions)
    saved = []
    for p in layers:
        xv, sv, _, _ = _layer_fwd(xv, p, tabs)
        saved.append(sv)
    loss, dx = _loss_head(xv, target)
    bigs, smalls = [None] * DEPTH, [None] * DEPTH
    for l in reversed(range(len(layers))):
        dx, bigs[l], smalls[l], _ = _layer_bwd(dx, saved[l], layers[l], tabs)
    return loss[0, 0], dx, bigs, smalls


def _pack_rows(flat, lead, width, mult):
    n = flat.shape[-1]
    rows = -(-n // (width * mult)) * mult
    pad = [(0, 0)] * (flat.ndim - 1) + [(0, rows * width - n)]
    return jnp.pad(flat, pad).reshape(lead + (rows, width))


def _rows2(n, layer=False):
    shape = SHAPES[n][1:] if layer else SHAPES[n]
    return (math.prod(shape[:-1]), shape[-1])


def _group_pack(group, get, lead):
    width, names = group
    pieces = []
    for n in names:
        rows, cols = _rows2(n, True)
        pad = [(0, 0)] * len(lead) + [(0, -rows % 16), (0, width - cols)]
        pieces.append(jnp.pad(get(n), pad))
    return pieces[0] if len(pieces) == 1 else jnp.concatenate(pieces, axis=len(lead))


def _group_unpack(group, buf, padded=False):
    _, names = group
    res, off = {}, 0
    for n in names:
        rows, cols = _rows2(n, True)
        rp = rows + (-rows % 16)
        res[n] = buf[:, off:off + (rp if padded else rows), :cols]
        off += rp
    return res


def kernel(x, positions, norm_mix_pre, norm_mix_post, norm_ffn_pre, norm_ffn_post, w_in, mla_q_norm, mla_w_q_up, mla_kv_norm, mla_w_kv_up, sc_conv_w, ssd_conv_w, ssd_conv_b, ssd_dt_bias, ssd_a_log, ssd_d, ssd_norm, w_out, ffn_w_up, ffn_conv_w, ffn_conv_b, ffn_w_down, loss_target, m_norm_mix_pre, m_norm_mix_post, m_norm_ffn_pre, m_norm_ffn_post, m_w_in, m_mla_q_norm, m_mla_w_q_up, m_mla_kv_norm, m_mla_w_kv_up, m_sc_conv_w, m_ssd_conv_w, m_ssd_conv_b, m_ssd_dt_bias, m_ssd_a_log, m_ssd_d, m_ssd_norm, m_w_out, m_ffn_w_up, m_ffn_conv_w, m_ffn_conv_b, m_ffn_w_down, v_norm_mix_pre, v_norm_mix_post, v_norm_ffn_pre, v_norm_ffn_post, v_w_in, v_mla_q_norm, v_mla_w_q_up, v_mla_kv_norm, v_mla_w_kv_up, v_sc_conv_w, v_ssd_conv_w, v_ssd_conv_b, v_ssd_dt_bias, v_ssd_a_log, v_ssd_d, v_ssd_norm, v_w_out, v_ffn_w_up, v_ffn_conv_w, v_ffn_conv_b, v_ffn_w_down):
    given = dict(locals())
    w = {n: given[n] for n in WEIGHTS}
    m = {n: given["m_" + n] for n in WEIGHTS}
    v = {n: given["v_" + n] for n in WEIGHTS}

    def shards(l, group_ids):
        return [_group_pack(GROUPS[gi], lambda n: w[n][l].astype(BF16).reshape(_rows2(n, True)), ()) for gi in group_ids]

    def unpacked(bufs, group_ids):
        big = {}
        for gi, buf in zip(group_ids, bufs):
            for n, piece in _group_unpack(GROUPS[gi], buf).items():
                big[n] = piece.reshape((NDEV,) + SHAPES[n][1:])
        return big

    small_w = {n: w[n] for n, _ in SMALL}
    tabs = _rope_tables(positions[0])
    xv, layers, saved = x[0], [], []
    att = _all_gather(shards(0, ATT_SIDE), "gather_weights")
    for l in range(DEPTH):
        prefetch = shards(l, FFN_SIDE) + (shards(l + 1, ATT_SIDE) if l + 1 < DEPTH else [])
        xv, sv, p, gathered = _layer_fwd(xv, _prep_layer(unpacked(att, ATT_SIDE), small_w, l), tabs, prefetch,
                                         lambda got: _prep_ffn(unpacked(got[:len(FFN_SIDE)], FFN_SIDE)))
        att = gathered[len(FFN_SIDE):]
        layers.append(p)
        saved.append(sv)
    loss, dx = _loss_head(xv, loss_target[0])
    loss = lax.psum(loss[0, 0], ("x", "y", "c"))

    smalls, pending = [None] * DEPTH, None
    recvs = [[None] * len(GROUPS) for _ in range(DEPTH)]
    for l in reversed(range(DEPTH)):
        dx, grads, smalls[l], received = _layer_bwd(dx, saved[l], layers[l], tabs, True, pending)
        for pos, gi in enumerate(FFN_SIDE):
            recvs[l][gi] = received[pos]
        if pending is not None:
            for pos, gi in enumerate(ATT_SIDE):
                recvs[l + 1][gi] = received[len(FFN_SIDE) + pos]
        pending = _pack_grads(grads, ATT_SIDE)
    for gi, buf in zip(ATT_SIDE, _all_to_all(pending, "exchange_grads")):
        recvs[0][gi] = buf
    out = {}
    for gi, g in enumerate(GROUPS):
        per_layer = [_group_unpack(g, recvs[l][gi], padded=True) for l in range(DEPTH)]
        for n in g[1]:
            r2 = _rows2(n)
            rows, cols = _rows2(n, True)
            parts = jnp.concatenate([per_layer[l][n] for l in range(DEPTH)], axis=1)
            parts = parts.reshape(NDEV, DEPTH, -1, cols)[:, :, :rows].reshape((NDEV,) + r2)
            res = _adamw(parts, w[n].reshape(r2), m[n].reshape(r2), v[n].reshape(r2), "adamw_" + n)
            out[n] = [a.reshape(SHAPES[n]) for a in res]

    sflat = jnp.concatenate([jnp.stack([smalls[l][n] for l in range(DEPTH)]).reshape(-1) for n, _ in SMALL])
    sparts = _all_gather([_pack_rows(sflat, (), LANE, 8)], "gather_small_grads")[0]
    pk = lambda d: _pack_rows(jnp.concatenate([d[n].reshape(-1) for n, _ in SMALL]), (), LANE, 8)
    res = _adamw(sparts, pk(w), pk(m), pk(v), "adamw_small")
    off = 0
    for n, width in SMALL:
        out[n] = [a.reshape(-1)[off:off + DEPTH * width].reshape(DEPTH, width) for a in res]
        off += DEPTH * width

    return (loss, dx[None], *[out[n][0] for n in WEIGHTS], *[out[n][1] for n in WEIGHTS],
            *[out[n][2] for n in WEIGHTS], *[out[n][3] for n in WEIGHTS])
```

```python
import functools
import math

import jax
import jax.numpy as jnp
from jax import lax
from jax.experimental import pallas as pl
from jax.experimental.pallas import tpu as pltpu

F32 = jnp.float32
BF16 = jnp.bfloat16

D = 1024
DEPTH = 4
NDEV = 8
HEADS = 8
QL = 256
KVL = 128
ROPE = 32
NOPE = 64
SC = 256
SSD_DIM = 256
SSD_CONV = 768
SSD_H = 4
SSD_L = 128
FFN = 2816
FB = 704
EPS = 1e-6
ROPE_THETA = 10000.0
ATT_SCALE = 96 ** -0.5
LR, B1, B2, AEPS, WD, STEP = 0.001, 0.9, 0.999, 1e-08, 0.01, 10

PW = 2432
CATW = 1536

ROW_TILE = 512
ROW_CHUNK = 16
MM_TILE = 1024
ATT_TILE = 256
FWD_HEADS = 4
BWD_HEADS = 2
HALO = 16
LANE = 128
NEG = -1e30
HI = lax.Precision.HIGHEST
NN = (((1,), (0,)), ((), ()))
NT = (((1,), (1,)), ((), ()))
TN = (((0,), (0,)), ((), ()))
VMEM_LIMIT = 56 * 1024 * 1024

SHARDED = (
    ("w_in", (4, 128, 2212)),
    ("mla_w_q_up", (4, 256, 96)),
    ("mla_w_kv_up", (4, 128, 128)),
    ("sc_conv_w", (4, 3, 32)),
    ("ssd_conv_w", (4, 4, 96)),
    ("w_out", (4, 128, 1024)),
    ("ffn_w_up", (4, 1024, 704)),
    ("ffn_conv_w", (4, 3, 704)),
    ("ffn_w_down", (4, 352, 1024)),
)
SHAPES = dict(SHARDED)
GROUPS = (
    (2212, ("w_in",)),
    (1024, ("w_out", "ffn_w_down")),
    (704, ("ffn_w_up", "ffn_conv_w")),
    (96, ("mla_w_q_up", "ssd_conv_w", "sc_conv_w")),
    (128, ("mla_w_kv_up",)),
)
ATT_SIDE = (0, 3, 4)
FFN_SIDE = (1, 2)
SMALL = (
    ("norm_mix_pre", 1024), ("norm_mix_post", 1024), ("norm_ffn_pre", 1024), ("norm_ffn_post", 1024),
    ("mla_q_norm", 256), ("mla_kv_norm", 128), ("ssd_conv_b", 768), ("ssd_dt_bias", 4), ("ssd_a_log", 4),
    ("ssd_d", 4), ("ssd_norm", 256), ("ffn_conv_b", 5632),
)
WEIGHTS = ("norm_mix_pre", "norm_mix_post", "norm_ffn_pre", "norm_ffn_post", "w_in", "mla_q_norm", "mla_w_q_up",
           "mla_kv_norm", "mla_w_kv_up", "sc_conv_w", "ssd_conv_w", "ssd_conv_b", "ssd_dt_bias", "ssd_a_log", "ssd_d",
           "ssd_norm", "w_out", "ffn_w_up", "ffn_conv_w", "ffn_conv_b", "ffn_w_down")


def _dot(a, b, dims=NN, precision=None):
    return lax.dot_general(a, b, dims, precision=precision, preferred_element_type=F32)


def _sig(v):
    return 1.0 / (1.0 + jnp.exp(-v))


def _cp(*sem):
    return pltpu.CompilerParams(dimension_semantics=sem, vmem_limit_bytes=VMEM_LIMIT)


def _rowsum(v):
    return jnp.sum(v, axis=0, keepdims=True)


def _prev_halo(i, ts):
    return jnp.maximum(i * (ts // HALO) - 1, 0)


def _next_halo(i, ts, n):
    return jnp.minimum((i + 1) * (ts // HALO), n * (ts // HALO) - 1)


def _gather_plan(x_refs, out_refs, send_sems, recv_sems, local_sems):
    n = len(x_refs)
    x, y, cc = lax.axis_index("x"), lax.axis_index("y"), lax.axis_index("c")
    me, sibling = (x, y, cc), (x, y, 1 - cc)
    chips = [(1 - x, y), (x, 1 - y), (1 - x, 1 - y)]

    def rows(t, px, py, pc):
        return out_refs[t].at[4 * px + 2 * py + pc]

    def copy(t, k, block, to, own=False):
        return pltpu.make_async_remote_copy(
            src_ref=x_refs[t] if own else rows(t, *block), dst_ref=rows(t, *block),
            send_sem=send_sems.at[7 * t + k], recv_sem=recv_sems.at[7 * t + k], device_id=to, device_id_type=pl.DeviceIdType.MESH)

    def local(t):
        return pltpu.make_async_copy(x_refs[t], rows(t, *me), local_sems.at[t])

    def start():
        for t in range(n):
            local(t).start()
            copy(t, 0, me, sibling, own=True).start()
            for j, chip in enumerate(chips):
                copy(t, 1 + j, me, (*chip, cc), own=True).start()

    def finish():
        for j, chip in enumerate(chips):
            for t in range(n):
                copy(t, 1 + j, (*chip, cc), me).wait_recv()
                copy(t, 4 + j, (*chip, cc), sibling).start()
        for t in range(n):
            copy(t, 0, sibling, me).wait_recv()
            for j, chip in enumerate(chips):
                copy(t, 4 + j, (*chip, 1 - cc), me).wait_recv()
        for t in range(n):
            copy(t, 0, me, sibling, own=True).wait_send()
            for j, chip in enumerate(chips):
                copy(t, 1 + j, me, (*chip, cc), own=True).wait_send()
                copy(t, 4 + j, (*chip, cc), sibling).wait_send()
            local(t).wait()

    return start, finish


def _exchange_plan(x_refs, out_refs, send_sems, recv_sems, local_sems):
    n = len(x_refs)
    x, y, cc = lax.axis_index("x"), lax.axis_index("y"), lax.axis_index("c")
    me = 4 * x + 2 * y + cc

    def copies():
        res = [pltpu.make_async_copy(x_refs[t].at[me], out_refs[t].at[me], local_sems.at[t]) for t in range(n)]
        for k in range(1, NDEV):
            px = 1 - x if k & 4 else x
            py = 1 - y if k & 2 else y
            pc = 1 - cc if k & 1 else cc
            peer = 4 * px + 2 * py + pc
            for t in range(n):
                res.append(pltpu.make_async_remote_copy(
                    src_ref=x_refs[t].at[peer], dst_ref=out_refs[t].at[me], send_sem=send_sems.at[7 * t + k - 1],
                    recv_sem=recv_sems.at[7 * t + k - 1], device_id=(px, py, pc), device_id_type=pl.DeviceIdType.MESH))
        return res

    def start():
        for cp in copies():
            cp.start()

    def finish():
        for cp in copies():
            cp.wait()

    return start, finish


def _comm_scratch(n):
    return [pltpu.SemaphoreType.DMA((7 * n,)), pltpu.SemaphoreType.DMA((7 * n,)), pltpu.SemaphoreType.DMA((n,))]


ANY = pl.BlockSpec(memory_space=pl.ANY)


def _all_gather(xs, name):
    n = len(xs)

    def body(*refs):
        start, finish = _gather_plan(refs[:n], refs[n:2 * n], *refs[2 * n:])
        start()
        finish()

    return pl.pallas_call(
        body, name=name, out_shape=[jax.ShapeDtypeStruct((NDEV,) + a.shape, a.dtype) for a in xs],
        in_specs=[ANY] * n, out_specs=[ANY] * n, scratch_shapes=_comm_scratch(n),
    )(*xs)


def _all_to_all(xs, name):
    n = len(xs)

    def body(*refs):
        start, finish = _exchange_plan(refs[:n], refs[n:2 * n], *refs[2 * n:])
        start()
        finish()

    return pl.pallas_call(
        body, name=name, out_shape=[jax.ShapeDtypeStruct(a.shape, a.dtype) for a in xs],
        in_specs=[ANY] * n, out_specs=[ANY] * n, scratch_shapes=_comm_scratch(n),
    )(*xs)


def _mm(name, a, b, out_shape, grid, a_spec, b_spec, o_spec, dims, acc_shape):
    nk = grid[2]

    def single(a_ref, b_ref, o_ref):
        o_ref[...] = _dot(a_ref[...], b_ref[...], dims).astype(o_ref.dtype)

    if nk == 1:
        return pl.pallas_call(
            single, name=name, grid=grid, out_shape=out_shape, in_specs=[a_spec, b_spec], out_specs=o_spec,
            compiler_params=_cp("parallel", "parallel", "arbitrary"),
        )(a, b)

    def body(a_ref, b_ref, o_ref, acc_ref):
        k = pl.program_id(2)

        @pl.when(k == 0)
        def _():
            acc_ref[...] = jnp.zeros_like(acc_ref)

        acc_ref[...] += _dot(a_ref[...], b_ref[...], dims)

        @pl.when(k == nk - 1)
        def _():
            o_ref[...] = acc_ref[...].astype(o_ref.dtype)

    return pl.pallas_call(
        body, name=name, grid=grid, out_shape=out_shape, in_specs=[a_spec, b_spec], out_specs=o_spec,
        scratch_shapes=[pltpu.VMEM(acc_shape, F32)], compiler_params=_cp("parallel", "parallel", "arbitrary"),
    )(a, b)


def _mm_rows(name, a, w, out_dtype, dims):
    s, k = a.shape
    n = w.shape[1] if dims == NN else w.shape[0]
    tm = min(MM_TILE, s)
    return _mm(name, a, w, jax.ShapeDtypeStruct((s, n), out_dtype), (s // tm, 1, 1),
               pl.BlockSpec((tm, k), lambda i, j, kk: (i, 0)), pl.BlockSpec(w.shape, lambda i, j, kk: (0, 0)),
               pl.BlockSpec((tm, n), lambda i, j, kk: (i, 0)), dims, (tm, n))


def _mm_wgrad(name, a, g, out_dtype):
    s, m = a.shape
    n = g.shape[1]
    tk = min(MM_TILE, s)
    return _mm(name, a, g, jax.ShapeDtypeStruct((m, n), out_dtype), (1, 1, s // tk),
               pl.BlockSpec((tk, m), lambda i, j, kk: (kk, 0)), pl.BlockSpec((tk, n), lambda i, j, kk: (kk, 0)),
               pl.BlockSpec((m, n), lambda i, j, kk: (0, 0)), TN, (m, n))


def _mm_up(h2, wup):
    s = h2.shape[0]
    tm = min(MM_TILE, s)
    return _mm("ffn_up", h2, wup, jax.ShapeDtypeStruct((NDEV, s, FB), BF16), (NDEV, s // tm, 1),
               pl.BlockSpec((tm, D), lambda j, i, kk: (i, 0)), pl.BlockSpec((None, D, FB), lambda j, i, kk: (j, 0, 0)),
               pl.BlockSpec((None, tm, FB), lambda j, i, kk: (j, i, 0)), NN, (tm, FB))


def _mm_down(act, wdown):
    s = act.shape[1]
    tm = min(MM_TILE, s)
    return _mm("ffn_down", act, wdown, jax.ShapeDtypeStruct((s, D), F32), (s // tm, 1, 4),
               pl.BlockSpec((None, tm, FB), lambda i, j, kk: (kk, i, 0)), pl.BlockSpec((None, FB, D), lambda i, j, kk: (kk, 0, 0)),
               pl.BlockSpec((tm, D), lambda i, j, kk: (i, 0)), NN, (tm, D))


def _mm_dact(df, wdown):
    s = df.shape[0]
    tm = min(MM_TILE, s)
    return _mm("ffn_dact", df, wdown, jax.ShapeDtypeStruct((4, s, FB), BF16), (4, s // tm, 1),
               pl.BlockSpec((tm, D), lambda j, i, kk: (i, 0)), pl.BlockSpec((None, FB, D), lambda j, i, kk: (j, 0, 0)),
               pl.BlockSpec((None, tm, FB), lambda j, i, kk: (j, i, 0)), NT, (tm, FB))


def _mm_dwdown(act, df):
    s = df.shape[0]
    tk = min(MM_TILE, s)
    return _mm("ffn_dwdown", act, df, jax.ShapeDtypeStruct((4, FB, D), BF16), (4, 1, s // tk),
               pl.BlockSpec((None, tk, FB), lambda j, i, kk: (j, kk, 0)), pl.BlockSpec((tk, D), lambda j, i, kk: (kk, 0)),
               pl.BlockSpec((None, FB, D), lambda j, i, kk: (j, 0, 0)), TN, (FB, D))


def _mm_dh2(dupre, wup):
    s = dupre.shape[1]
    tm = min(MM_TILE, s)
    return _mm("ffn_dh2", dupre, wup, jax.ShapeDtypeStruct((s, D), BF16), (s // tm, 1, NDEV),
               pl.BlockSpec((None, tm, FB), lambda i, j, kk: (kk, i, 0)), pl.BlockSpec((None, D, FB), lambda i, j, kk: (kk, 0, 0)),
               pl.BlockSpec((tm, D), lambda i, j, kk: (i, 0)), NT, (tm, D))


def _mm_dwup(h2, dupre):
    s = h2.shape[0]
    tk = min(MM_TILE, s)
    return _mm("ffn_dwup", h2, dupre, jax.ShapeDtypeStruct((NDEV, D, FB), BF16), (NDEV, 1, s // tk),
               pl.BlockSpec((tk, D), lambda j, i, kk: (kk, 0)), pl.BlockSpec((None, tk, FB), lambda j, i, kk: (j, kk, 0)),
               pl.BlockSpec((None, D, FB), lambda j, i, kk: (j, 0, 0)), TN, (D, FB))


def _rms(xv, w, out_dtype, name):
    s, d = xv.shape
    ts = min(ROW_TILE, s)

    def body(x_ref, w_ref, o_ref):
        xf = x_ref[...].astype(F32)
        r = lax.rsqrt(jnp.mean(xf * xf, axis=-1, keepdims=True) + EPS)
        o_ref[...] = (xf * r * w_ref[...]).astype(o_ref.dtype)

    return pl.pallas_call(
        body, name=name, grid=(s // ts,), out_shape=jax.ShapeDtypeStruct((s, d), out_dtype),
        in_specs=[pl.BlockSpec((ts, d), lambda i: (i, 0)), pl.BlockSpec((1, d), lambda i: (0, 0))],
        out_specs=pl.BlockSpec((ts, d), lambda i: (i, 0)), compiler_params=_cp("parallel"),
    )(xv, w)


def _add_rms(xv, mv, w, name):
    s, d = xv.shape
    ts = min(ROW_TILE, s)

    def body(x_ref, m_ref, w_ref, o_ref):
        mf = m_ref[...].astype(F32)
        r = lax.rsqrt(jnp.mean(mf * mf, axis=-1, keepdims=True) + EPS)
        o_ref[...] = x_ref[...] + mf * r * w_ref[...]

    return pl.pallas_call(
        body, name=name, grid=(s // ts,), out_shape=jax.ShapeDtypeStruct((s, d), F32),
        in_specs=[pl.BlockSpec((ts, d), lambda i: (i, 0)), pl.BlockSpec((ts, d), lambda i: (i, 0)), pl.BlockSpec((1, d), lambda i: (0, 0))],
        out_specs=pl.BlockSpec((ts, d), lambda i: (i, 0)), compiler_params=_cp("parallel"),
    )(xv, mv, w)


def _rms_bwd_math(xf, w, dy):
    r = lax.rsqrt(jnp.mean(xf * xf, axis=-1, keepdims=True) + EPS)
    xh = xf * r
    dxh = dy * w
    dx = r * (dxh - xh * jnp.mean(dxh * xh, axis=-1, keepdims=True))
    return dx, dy * xh


def _rms_bwd(xv, w, dy, dres, out_dtype, name):
    s, d = xv.shape
    ts = min(ROW_TILE, s)
    with_res = dres is not None

    def body(*refs):
        if with_res:
            x_ref, w_ref, dy_ref, dres_ref, dx_ref, dw_ref = refs
        else:
            x_ref, w_ref, dy_ref, dx_ref, dw_ref = refs
        dx, dwt = _rms_bwd_math(x_ref[...].astype(F32), w_ref[...], dy_ref[...].astype(F32))
        if with_res:
            dx = dx + dres_ref[...]
        dx_ref[...] = dx.astype(dx_ref.dtype)

        @pl.when(pl.program_id(0) == 0)
        def _():
            dw_ref[...] = jnp.zeros_like(dw_ref)

        dw_ref[...] += _rowsum(dwt)

    row = pl.BlockSpec((ts, d), lambda i: (i, 0))
    vec = pl.BlockSpec((1, d), lambda i: (0, 0))
    ins = [xv, w, dy] + ([dres] if with_res else [])
    return pl.pallas_call(
        body, name=name, grid=(s // ts,),
        out_shape=(jax.ShapeDtypeStruct((s, d), out_dtype), jax.ShapeDtypeStruct((1, d), F32)),
        in_specs=[row, vec, row] + ([row] if with_res else []), out_specs=(row, vec), compiler_params=_cp("arbitrary"),
    )(*ins)


def _loss_head(yv, tv):
    s, d = yv.shape
    ts = min(ROW_TILE, s)

    def body(y_ref, t_ref, l_ref, dy_ref):
        e = y_ref[...] - t_ref[...]
        dy_ref[...] = e * (1.0 / d)

        @pl.when(pl.program_id(0) == 0)
        def _():
            l_ref[...] = jnp.zeros_like(l_ref)

        tot = jnp.sum(jnp.sum(e * e, axis=1, keepdims=True), axis=0, keepdims=True)
        l_ref[...] += jnp.broadcast_to(tot * (0.5 / d), (8, LANE))

    row = pl.BlockSpec((ts, d), lambda i: (i, 0))
    return pl.pallas_call(
        body, name="loss_head", grid=(s // ts,),
        out_shape=(jax.ShapeDtypeStruct((8, LANE), F32), jax.ShapeDtypeStruct((s, d), F32)),
        in_specs=[row, row], out_specs=(pl.BlockSpec((8, LANE), lambda i: (0, 0)), row), compiler_params=_cp("arbitrary"),
    )(yv, tv)


def _rope(v, c, a, b):
    return v * c + pltpu.roll(v, LANE - 16, 1) * a + pltpu.roll(v, 16, 1) * b


def _rope_t(dv, c, a, b):
    return dv * c + pltpu.roll(dv * a, 16, 1) + pltpu.roll(dv * b, LANE - 16, 1)


def _mla_prep(proj, tabs, qnw, kvnw, wq, wkv):
    s = proj.shape[0]
    ts = min(ROW_TILE, s)
    tc, ta, tb = tabs

    def body(cq_ref, ckv_ref, kr_ref, c_ref, a_ref, b_ref, qnw_ref, kvnw_ref, wq_ref, wkv_ref, q_ref, k_ref, kv_ref):
        c, a, b = c_ref[...], a_ref[...], b_ref[...]
        cq = cq_ref[...].astype(F32)
        qn = (cq * lax.rsqrt(jnp.mean(cq * cq, axis=-1, keepdims=True) + EPS) * qnw_ref[...]).astype(BF16)
        ckv = ckv_ref[...].astype(F32)
        kvn = (ckv * lax.rsqrt(jnp.mean(ckv * ckv, axis=-1, keepdims=True) + EPS) * kvnw_ref[...]).astype(BF16)
        kr = _rope(kr_ref[...].astype(F32), c, a, b)
        lane = lax.broadcasted_iota(jnp.int32, (ts, LANE), 1)
        for h in range(HEADS):
            q_ref[h] = _rope(_dot(qn, wq_ref[h]), c, a, b).astype(BF16)
            kv = _dot(kvn, wkv_ref[h])
            kv_ref[h] = kv.astype(BF16)
            k_ref[h] = jnp.where(lane < NOPE, kv, kr).astype(BF16)

    tab = pl.BlockSpec((ts, LANE), lambda i: (i, 0))
    hd = pl.BlockSpec((HEADS, ts, LANE), lambda i: (0, i, 0))
    out = jax.ShapeDtypeStruct((HEADS, s, LANE), BF16)
    return pl.pallas_call(
        body, name="mla_prep", grid=(s // ts,), out_shape=(out, out, out),
        in_specs=[pl.BlockSpec((ts, QL), lambda i: (i, 0)), pl.BlockSpec((ts, LANE), lambda i: (i, 2)),
                  pl.BlockSpec((ts, LANE), lambda i: (i, 3)), tab, tab, tab,
                  pl.BlockSpec((1, QL), lambda i: (0, 0)), pl.BlockSpec((1, KVL), lambda i: (0, 0)),
                  pl.BlockSpec((HEADS, QL, LANE), lambda i: (0, 0, 0)), pl.BlockSpec((HEADS, KVL, LANE), lambda i: (0, 0, 0))],
        out_specs=(hd, hd, hd), compiler_params=_cp("parallel"),
    )(proj, proj, proj, tc, ta, tb, qnw, kvnw, wq, wkv)


def _mla_prep_bwd(proj, tabs, qnw, kvnw, wq, wkv, dq, dk, dv):
    s = proj.shape[0]
    ts = min(ROW_TILE, s)
    tc, ta, tb = tabs

    def body(cq_ref, ckv_ref, c_ref, a_ref, b_ref, qnw_ref, kvnw_ref, wq_ref, wkv_ref, dq_ref, dk_ref, dv_ref,
             dcq_ref, dckv_ref, dkr_ref, dwq_ref, dwkv_ref, dqnw_ref, dkvnw_ref):
        @pl.when(pl.program_id(0) == 0)
        def _():
            dwq_ref[...] = jnp.zeros_like(dwq_ref)
            dwkv_ref[...] = jnp.zeros_like(dwkv_ref)
            dqnw_ref[...] = jnp.zeros_like(dqnw_ref)
            dkvnw_ref[...] = jnp.zeros_like(dkvnw_ref)

        c, a, b = c_ref[...], a_ref[...], b_ref[...]
        cq = cq_ref[...].astype(F32)
        qn = (cq * lax.rsqrt(jnp.mean(cq * cq, axis=-1, keepdims=True) + EPS) * qnw_ref[...]).astype(BF16)
        ckv = ckv_ref[...].astype(F32)
        kvn = (ckv * lax.rsqrt(jnp.mean(ckv * ckv, axis=-1, keepdims=True) + EPS) * kvnw_ref[...]).astype(BF16)
        lane = lax.broadcasted_iota(jnp.int32, (ts, LANE), 1)
        dqn = jnp.zeros((ts, QL), F32)
        dkvn = jnp.zeros((ts, KVL), F32)
        dkr = jnp.zeros((ts, LANE), F32)
        for h in range(HEADS):
            dqh = _rope_t(dq_ref[h], c, a, b).astype(BF16)
            dwq_ref[h] += _dot(qn, dqh, TN)
            dqn += _dot(dqh, wq_ref[h], NT)
            dkh = dk_ref[h].astype(F32)
            dkvh = jnp.where(lane < NOPE, dkh, dv_ref[h].astype(F32)).astype(BF16)
            dkr += jnp.where(lane < NOPE, 0.0, dkh)
            dwkv_ref[h] += _dot(kvn, dkvh, TN)
            dkvn += _dot(dkvh, wkv_ref[h], NT)
        dkr_ref[...] = _rope_t(dkr, c, a, b).astype(BF16)
        dcq, dwt = _rms_bwd_math(cq, qnw_ref[...], dqn)
        dcq_ref[...] = dcq.astype(BF16)
        dqnw_ref[...] += _rowsum(dwt)
        dckv, dwt = _rms_bwd_math(ckv, kvnw_ref[...], dkvn)
        dckv_ref[...] = dckv.astype(BF16)
        dkvnw_ref[...] += _rowsum(dwt)

    tab = pl.BlockSpec((ts, LANE), lambda i: (i, 0))
    hd = pl.BlockSpec((HEADS, ts, LANE), lambda i: (0, i, 0))
    wq_spec = pl.BlockSpec((HEADS, QL, LANE), lambda i: (0, 0, 0))
    wkv_spec = pl.BlockSpec((HEADS, KVL, LANE), lambda i: (0, 0, 0))
    return pl.pallas_call(
        body, name="mla_prep_bwd", grid=(s // ts,),
        out_shape=(jax.ShapeDtypeStruct((s, QL), BF16), jax.ShapeDtypeStruct((s, KVL), BF16), jax.ShapeDtypeStruct((s, LANE), BF16),
                   jax.ShapeDtypeStruct((HEADS, QL, LANE), F32), jax.ShapeDtypeStruct((HEADS, KVL, LANE), F32),
                   jax.ShapeDtypeStruct((1, QL), F32), jax.ShapeDtypeStruct((1, KVL), F32)),
        in_specs=[pl.BlockSpec((ts, QL), lambda i: (i, 0)), pl.BlockSpec((ts, LANE), lambda i: (i, 2)), tab, tab, tab,
                  pl.BlockSpec((1, QL), lambda i: (0, 0)), pl.BlockSpec((1, KVL), lambda i: (0, 0)), wq_spec, wkv_spec, hd, hd, hd],
        out_specs=(pl.BlockSpec((ts, QL), lambda i: (i, 0)), pl.BlockSpec((ts, KVL), lambda i: (i, 0)), tab, wq_spec, wkv_spec,
                   pl.BlockSpec((1, QL), lambda i: (0, 0)), pl.BlockSpec((1, KVL), lambda i: (0, 0))),
        compiler_params=_cp("arbitrary"),
    )(proj, proj, tc, ta, tb, qnw, kvnw, wq, wkv, dq, dk, dv)


def _transpose_bf16(v):
    return v.astype(F32).T.astype(BF16)


def _flash_fwd(q, k, kv, prefetch=None):
    s = q.shape[1]
    t = min(ATT_TILE, s)
    n = s // t
    g = FWD_HEADS
    nx = len(prefetch) if prefetch else 0

    def body(*refs):
        q_ref, k_ref, kv_ref = refs[:3]
        o_ref, lse_ref = refs[3 + nx:5 + nx]
        kvt_sc = refs[5 + 2 * nx]
        step = pl.program_id(0) * n + pl.program_id(1)
        if nx:
            start, finish = _gather_plan(refs[3:3 + nx], refs[5 + nx:5 + 2 * nx], *refs[6 + 2 * nx:])
            pl.when(step == 0)(start)
        attend(q_ref, k_ref, kv_ref, o_ref, lse_ref, kvt_sc)
        if nx:
            pl.when(step == (HEADS // g) * n - 1)(finish)

    def attend(q_ref, k_ref, kv_ref, o_ref, lse_ref, kvt_sc):
        i = pl.program_id(1)

        @pl.when(i == 0)
        def _():
            for hh in range(g):
                kvt_sc[hh] = _transpose_bf16(kv_ref[hh])

        qt = [_transpose_bf16(q_ref[hh]) for hh in range(g)]
        kpos = lax.broadcasted_iota(jnp.int32, (t, t), 0)
        qpos = lax.broadcasted_iota(jnp.int32, (t, t), 1)

        def chunk(j, carry, diagonal):
            start = pl.multiple_of(j * t, t)
            scs = [_dot(k_ref[hh, pl.ds(start, t), :], qt[hh]) for hh in range(g)]
            soft = []
            for hh in range(g):
                m, l, _ = carry[hh]
                sc = scs[hh] * ATT_SCALE
                if diagonal:
                    sc = jnp.where(qpos >= kpos, sc, NEG)
                m_new = jnp.maximum(m, jnp.max(sc, axis=0, keepdims=True))
                alpha = jnp.exp(m - m_new)
                p = jnp.exp(sc - m_new)
                soft.append((m_new, alpha * l + jnp.sum(p, axis=0, keepdims=True), alpha, p.astype(BF16)))
            pvs = [_dot(kvt_sc[hh, :, pl.ds(start, t)], soft[hh][3]) for hh in range(g)]
            return tuple((soft[hh][0], soft[hh][1], soft[hh][2] * carry[hh][2] + pvs[hh]) for hh in range(g))

        init = tuple((jnp.full((1, t), NEG, F32), jnp.zeros((1, t), F32), jnp.zeros((LANE, t), F32)) for _ in range(g))
        carry = lax.fori_loop(0, i, lambda j, c: chunk(j, c, False), init)
        carry = chunk(i, carry, True)
        for hh in range(g):
            m, l, acc = carry[hh]
            o_ref[:, hh * LANE:(hh + 1) * LANE] = (acc / l).T.astype(BF16)
            lse_ref[hh] = m + jnp.log(l)

    whole = pl.BlockSpec((g, s, LANE), lambda h, i: (h, 0, 0))
    res = pl.pallas_call(
        body, name="flash_fwd_gather" if nx else "flash_fwd", grid=(HEADS // g, n),
        out_shape=[jax.ShapeDtypeStruct((s, HEADS * LANE), BF16), jax.ShapeDtypeStruct((HEADS, 1, s), F32)]
        + [jax.ShapeDtypeStruct((NDEV,) + a.shape, a.dtype) for a in (prefetch or [])],
        in_specs=[pl.BlockSpec((g, t, LANE), lambda h, i: (h, i, 0)), whole, whole] + [ANY] * nx,
        out_specs=[pl.BlockSpec((t, g * LANE), lambda h, i: (i, h)), pl.BlockSpec((g, 1, t), lambda h, i: (h, 0, i))] + [ANY] * nx,
        scratch_shapes=[pltpu.VMEM((g, LANE, s), BF16)] + (_comm_scratch(nx) if nx else []),
        compiler_params=_cp("arbitrary", "arbitrary"),
    )(q, k, kv, *(prefetch or []))
    return res[0], res[1], list(res[2:])


def _flash_bwd(q, k, kv, cat, dcat, lse, pending=None):
    s = q.shape[1]
    t = min(ATT_TILE, s)
    n = s // t
    g = BWD_HEADS
    nx = len(pending) if pending else 0

    def body(*refs):
        ins, outs, scr = refs[:6], refs[6 + nx:9 + nx], refs[9 + 2 * nx:13 + 2 * nx]
        step = pl.program_id(0) * n + pl.program_id(1)
        if nx:
            start, finish = _exchange_plan(refs[6:6 + nx], refs[9 + nx:9 + 2 * nx], *refs[13 + 2 * nx:])
            pl.when(step == 0)(start)
        attend(*ins, *outs, *scr)
        if nx:
            pl.when(step == (HEADS // g) * n - 1)(finish)

    def attend(q_ref, k_ref, kv_ref, o_ref, do_ref, lse_ref, dq_ref, dk_ref, dv_ref, qt_sc, dot_sc, delta_sc, dqt_sc):
        j = pl.program_id(1)

        @pl.when(j == 0)
        def _():
            for hh in range(g):
                lanes = slice(hh * LANE, (hh + 1) * LANE)
                qt_sc[hh] = _transpose_bf16(q_ref[hh])
                dof = do_ref[:, lanes].astype(F32)
                dot_sc[hh] = dof.T.astype(BF16)
                delta_sc[hh] = _dot(jnp.ones((8, LANE), F32), dof * o_ref[:, lanes].astype(F32), NT, precision=HI)
            dqt_sc[...] = jnp.zeros_like(dqt_sc)

        kjt = [_transpose_bf16(k_ref[hh]) for hh in range(g)]
        kpos = lax.broadcasted_iota(jnp.int32, (t, t), 0)
        qpos = lax.broadcasted_iota(jnp.int32, (t, t), 1)

        def chunk(i, carry, diagonal):
            start = pl.multiple_of(i * t, t)
            cols = pl.ds(start, t)
            scs = [_dot(k_ref[hh], qt_sc[hh, :, cols]) for hh in range(g)]
            dps = [_dot(kv_ref[hh], dot_sc[hh, :, cols]) for hh in range(g)]
            pds = []
            for hh in range(g):
                p = jnp.exp(scs[hh] * ATT_SCALE - lse_ref[hh, :, cols])
                if diagonal:
                    p = jnp.where(qpos >= kpos, p, 0.0)
                ds = (p * (dps[hh] - delta_sc[hh, 0:1, cols]) * ATT_SCALE).astype(BF16)
                pds.append((p.astype(BF16), ds))
            out = []
            for hh in range(g):
                dk, dv = carry[hh]
                dv = dv + _dot(pds[hh][0], do_ref[pl.ds(start, t), hh * LANE:(hh + 1) * LANE])
                dk = dk + _dot(pds[hh][1], q_ref[hh, pl.ds(start, t), :])
                dqt_sc[hh, :, cols] += _dot(kjt[hh], pds[hh][1])
                out.append((dk, dv))
            return tuple(out)

        zero = jnp.zeros((t, LANE), F32)
        carry = chunk(j, tuple((zero, zero) for _ in range(g)), True)
        carry = lax.fori_loop(j + 1, n, lambda i, c: chunk(i, c, False), carry)
        for hh in range(g):
            dk_ref[hh] = carry[hh][0].astype(BF16)
            dv_ref[hh] = carry[hh][1].astype(BF16)

        @pl.when(j == n - 1)
        def _():
            for hh in range(g):
                dq_ref[hh] = dqt_sc[hh].T

    whole = pl.BlockSpec((g, s, LANE), lambda h, j: (h, 0, 0))
    kspec = pl.BlockSpec((g, t, LANE), lambda h, j: (h, j, 0))
    ospec = pl.BlockSpec((s, g * LANE), lambda h, j: (0, h))
    res = pl.pallas_call(
        body, name="flash_bwd_exchange" if nx else "flash_bwd", grid=(HEADS // g, n),
        out_shape=[jax.ShapeDtypeStruct((HEADS, s, LANE), F32), jax.ShapeDtypeStruct((HEADS, s, LANE), BF16),
                   jax.ShapeDtypeStruct((HEADS, s, LANE), BF16)] + [jax.ShapeDtypeStruct(a.shape, a.dtype) for a in (pending or [])],
        in_specs=[whole, kspec, kspec, ospec, ospec, pl.BlockSpec((g, 1, s), lambda h, j: (h, 0, 0))] + [ANY] * nx,
        out_specs=[whole, kspec, kspec] + [ANY] * nx,
        scratch_shapes=[pltpu.VMEM((g, LANE, s), BF16), pltpu.VMEM((g, LANE, s), BF16), pltpu.VMEM((g, 8, s), F32),
                        pltpu.VMEM((g, LANE, s), F32)] + (_comm_scratch(nx) if nx else []),
        compiler_params=_cp("arbitrary", "arbitrary"),
    )(q, k, kv, cat, dcat, lse, *(pending or []))
    return res[0], res[1], res[2], list(res[3:])


def _conv3(ext, w_ref, ts):
    return (w_ref[0:1, :] * ext[pl.ds(HALO - 2, ts), :] + w_ref[1:2, :] * ext[pl.ds(HALO - 1, ts), :]
            + w_ref[2:3, :] * ext[pl.ds(HALO, ts), :])


def _conv3_rows(ext, w_ref, r):
    return (w_ref[0:1, :] * ext[pl.ds(HALO - 2 + r, ROW_CHUNK), :] + w_ref[1:2, :] * ext[pl.ds(HALO - 1 + r, ROW_CHUNK), :]
            + w_ref[2:3, :] * ext[pl.ds(HALO + r, ROW_CHUNK), :])


def _conv3_t(ext2, w_ref, ts):
    return (w_ref[0:1, :] * ext2[pl.ds(2, ts), :] + w_ref[1:2, :] * ext2[pl.ds(1, ts), :] + w_ref[2:3, :] * ext2[pl.ds(0, ts), :])


def _sconv_fwd(proj, w):
    s = proj.shape[0]
    ts = min(ROW_TILE, s)

    def body(b_ref, c_ref, h_ref, hc_ref, hh_ref, w_ref, o_ref, ext):
        i = pl.program_id(0)
        ext[0:HALO, :] = hc_ref[...].astype(F32) * hh_ref[...].astype(F32) * (i > 0).astype(F32)
        ext[HALO:HALO + ts, :] = c_ref[...].astype(F32) * h_ref[...].astype(F32)
        o_ref[...] = (b_ref[...].astype(F32) * _conv3(ext, w_ref, ts)).astype(BF16)

    def col(cb):
        return pl.BlockSpec((ts, SC), lambda i: (i, cb))

    def halo(cb):
        return pl.BlockSpec((HALO, SC), lambda i: (_prev_halo(i, ts), cb))

    return pl.pallas_call(
        body, name="sconv_fwd", grid=(s // ts,), out_shape=jax.ShapeDtypeStruct((s, SC), BF16),
        in_specs=[col(2), col(3), col(4), halo(3), halo(4), pl.BlockSpec((3, SC), lambda i: (0, 0))],
        out_specs=pl.BlockSpec((ts, SC), lambda i: (i, 0)), scratch_shapes=[pltpu.VMEM((ts + HALO, SC), F32)],
        compiler_params=_cp("parallel"),
    )(proj, proj, proj, proj, proj, w)


def _sconv_bwd(proj, dcat, w):
    s = proj.shape[0]
    ts = min(ROW_TILE, s)
    n = s // ts

    def body(b_ref, c_ref, h_ref, hc_ref, hh_ref, dy_ref, ndy_ref, nb_ref, w_ref, db_ref, dc_ref, dh_ref, dw_ref, ext, ext2):
        i = pl.program_id(0)

        @pl.when(i == 0)
        def _():
            dw_ref[...] = jnp.zeros_like(dw_ref)

        cv, hv, bv = c_ref[...].astype(F32), h_ref[...].astype(F32), b_ref[...].astype(F32)
        ext[0:HALO, :] = hc_ref[...].astype(F32) * hh_ref[...].astype(F32) * (i > 0).astype(F32)
        ext[HALO:HALO + ts, :] = cv * hv
        dy = dy_ref[...].astype(F32)
        db_ref[...] = (dy * _conv3(ext, w_ref, ts)).astype(BF16)
        dyb = dy * bv
        ext2[0:ts, :] = dyb
        ext2[ts:ts + HALO, :] = ndy_ref[...].astype(F32) * nb_ref[...].astype(F32) * (i < n - 1).astype(F32)
        dg = _conv3_t(ext2, w_ref, ts)
        dc_ref[...] = (dg * hv).astype(BF16)
        dh_ref[...] = (dg * cv).astype(BF16)
        for kk in range(3):
            dw_ref[kk:kk + 1, :] += _rowsum(dyb * ext[pl.ds(HALO - 2 + kk, ts), :])

    def col(cb):
        return pl.BlockSpec((ts, SC), lambda i: (i, cb))

    def halo(cb):
        return pl.BlockSpec((HALO, SC), lambda i: (_prev_halo(i, ts), cb))

    def nxt(cb):
        return pl.BlockSpec((HALO, SC), lambda i: (_next_halo(i, ts, n), cb))

    out = jax.ShapeDtypeStruct((s, SC), BF16)
    o0 = pl.BlockSpec((ts, SC), lambda i: (i, 0))
    return pl.pallas_call(
        body, name="sconv_bwd", grid=(n,), out_shape=(out, out, out, jax.ShapeDtypeStruct((3, SC), F32)),
        in_specs=[col(2), col(3), col(4), halo(3), halo(4), col(4), nxt(4), nxt(2), pl.BlockSpec((3, SC), lambda i: (0, 0))],
        out_specs=(o0, o0, o0, pl.BlockSpec((3, SC), lambda i: (0, 0))),
        scratch_shapes=[pltpu.VMEM((ts + HALO, SC), F32), pltpu.VMEM((ts + HALO, SC), F32)], compiler_params=_cp("arbitrary"),
    )(proj, proj, proj, proj, proj, dcat, dcat, proj, w)


def _ffn_stage(ext, u_ref, halo_ref, i, ts):
    ext[0:HALO, :] = halo_ref[...].astype(F32) * (i > 0).astype(F32)
    ext[HALO:HALO + ts, :] = u_ref[...].astype(F32)


def _ffn_specs(ts):
    cur = pl.BlockSpec((2, None, ts, FB), lambda j, i: (0, j, i, 0))
    halo = pl.BlockSpec((2, None, HALO, FB), lambda j, i: (0, j, _prev_halo(i, ts), 0))
    w = pl.BlockSpec((2, None, 3, FB), lambda j, i: (0, j, 0, 0))
    b = pl.BlockSpec((2, None, 1, FB), lambda j, i: (0, j, 0, 0))
    return cur, halo, w, b


def _ffn_act(upre, fcw, fcb):
    s = upre.shape[1]
    ts = min(ROW_TILE, s)

    def body(u_ref, halo_ref, w_ref, b_ref, o_ref, ext_g, ext_u):
        i = pl.program_id(1)
        _ffn_stage(ext_g, u_ref.at[0], halo_ref.at[0], i, ts)
        _ffn_stage(ext_u, u_ref.at[1], halo_ref.at[1], i, ts)
        for r in range(0, ts, ROW_CHUNK):
            gate = b_ref[0] + _conv3_rows(ext_g, w_ref.at[0], r)
            up = b_ref[1] + _conv3_rows(ext_u, w_ref.at[1], r)
            o_ref[pl.ds(r, ROW_CHUNK), :] = (gate * _sig(gate) * up).astype(BF16)

    cur, halo, w, b = _ffn_specs(ts)
    u4 = upre.reshape(2, 4, s, FB)
    return pl.pallas_call(
        body, name="ffn_act", grid=(4, s // ts), out_shape=jax.ShapeDtypeStruct((4, s, FB), BF16),
        in_specs=[cur, halo, w, b], out_specs=pl.BlockSpec((None, ts, FB), lambda j, i: (j, i, 0)),
        scratch_shapes=[pltpu.VMEM((ts + HALO, FB), F32), pltpu.VMEM((ts + HALO, FB), F32)], compiler_params=_cp("parallel", "parallel"),
    )(u4, u4, fcw.reshape(2, 4, 3, FB), fcb.reshape(2, 4, 1, FB))


def _ffn_bwd_a(upre, dact, fcw, fcb):
    s = upre.shape[1]
    ts = min(ROW_TILE, s)

    def body(u_ref, halo_ref, w_ref, b_ref, da_ref, du_ref, db_ref, ext_g, ext_u, acc):
        i = pl.program_id(1)

        @pl.when(i == 0)
        def _():
            db_ref[...] = jnp.zeros_like(db_ref)

        _ffn_stage(ext_g, u_ref.at[0], halo_ref.at[0], i, ts)
        _ffn_stage(ext_u, u_ref.at[1], halo_ref.at[1], i, ts)
        acc[...] = jnp.zeros_like(acc)
        for r in range(0, ts, ROW_CHUNK):
            rows = pl.ds(r, ROW_CHUNK)
            gate = b_ref[0] + _conv3_rows(ext_g, w_ref.at[0], r)
            up = b_ref[1] + _conv3_rows(ext_u, w_ref.at[1], r)
            sg = _sig(gate)
            da = da_ref[rows, :].astype(F32)
            dgate = da * up * sg * (1.0 + gate * (1.0 - sg))
            dup = da * gate * sg
            du_ref[0, rows, :] = dgate.astype(BF16)
            du_ref[1, rows, :] = dup.astype(BF16)
            acc[0] += dgate
            acc[1] += dup
        db_ref[0] += _rowsum(acc[0])
        db_ref[1] += _rowsum(acc[1])

    cur, halo, w, b = _ffn_specs(ts)
    u4 = upre.reshape(2, 4, s, FB)
    du, db = pl.pallas_call(
        body, name="ffn_bwd_a", grid=(4, s // ts),
        out_shape=(jax.ShapeDtypeStruct((2, 4, s, FB), BF16), jax.ShapeDtypeStruct((2, 4, 1, FB), F32)),
        in_specs=[cur, halo, w, b, pl.BlockSpec((None, ts, FB), lambda j, i: (j, i, 0))], out_specs=(cur, b),
        scratch_shapes=[pltpu.VMEM((ts + HALO, FB), F32), pltpu.VMEM((ts + HALO, FB), F32), pltpu.VMEM((2, ROW_CHUNK, FB), F32)],
        compiler_params=_cp("parallel", "arbitrary"),
    )(u4, u4, fcw.reshape(2, 4, 3, FB), fcb.reshape(2, 4, 1, FB), dact)
    return du.reshape(NDEV, s, FB), db.reshape(NDEV, 1, FB)


def _ffn_bwd_b(du, upre, fcw):
    s = upre.shape[1]
    ts = min(ROW_TILE, s)
    n = s // ts

    def body(du_ref, ndu_ref, u_ref, halo_ref, w_ref, dup_ref, dw_ref, ext, ext2, acc):
        i = pl.program_id(1)

        @pl.when(i == 0)
        def _():
            dw_ref[...] = jnp.zeros_like(dw_ref)

        ext2[0:ts, :] = du_ref[...].astype(F32)
        ext2[ts:ts + HALO, :] = ndu_ref[...].astype(F32) * (i < n - 1).astype(F32)
        _ffn_stage(ext, u_ref, halo_ref, i, ts)
        acc[...] = jnp.zeros_like(acc)
        for r in range(0, ts, ROW_CHUNK):
            d0 = ext2[pl.ds(r, ROW_CHUNK), :]
            dup = w_ref[2:3, :] * d0 + w_ref[1:2, :] * ext2[pl.ds(r + 1, ROW_CHUNK), :] + w_ref[0:1, :] * ext2[pl.ds(r + 2, ROW_CHUNK), :]
            dup_ref[pl.ds(r, ROW_CHUNK), :] = dup.astype(BF16)
            for kk in range(3):
                acc[kk] += d0 * ext[pl.ds(HALO - 2 + kk + r, ROW_CHUNK), :]
        for kk in range(3):
            dw_ref[kk:kk + 1, :] += _rowsum(acc[kk])

    cur = pl.BlockSpec((None, ts, FB), lambda j, i: (j, i, 0))
    w = pl.BlockSpec((None, 3, FB), lambda j, i: (j, 0, 0))
    return pl.pallas_call(
        body, name="ffn_bwd_b", grid=(NDEV, n),
        out_shape=(jax.ShapeDtypeStruct((NDEV, s, FB), BF16), jax.ShapeDtypeStruct((NDEV, 3, FB), F32)),
        in_specs=[cur, pl.BlockSpec((None, HALO, FB), lambda j, i: (j, _next_halo(i, ts, n), 0)), cur,
                  pl.BlockSpec((None, HALO, FB), lambda j, i: (j, _prev_halo(i, ts), 0)), w],
        out_specs=(cur, w),
        scratch_shapes=[pltpu.VMEM((ts + HALO, FB), F32), pltpu.VMEM((ts + HALO, FB), F32), pltpu.VMEM((3, ROW_CHUNK, FB), F32)],
        compiler_params=_cp("parallel", "arbitrary"),
    )(du, du, upre, upre, fcw)


def _softplus(v):
    e = jnp.exp(-jnp.abs(v))
    return jnp.maximum(v, 0.0) + jnp.where(e < 1e-4, e * (1.0 - 0.5 * e), jnp.log(1.0 + e))


def _ssd_consts():
    L = SSD_L
    r = lax.broadcasted_iota(jnp.int32, (L, L), 0)
    c = lax.broadcasted_iota(jnp.int32, (L, L), 1)
    tri = r >= c
    er = lax.broadcasted_iota(jnp.int32, (LANE, SSD_DIM), 0)
    ec = lax.broadcasted_iota(jnp.int32, (LANE, SSD_DIM), 1)
    expand = ((ec >= er * 64) & (ec < er * 64 + 64)).astype(F32)
    return tri, expand


def _ssd_conv4(ext, cw_ref, cb_ref):
    L = SSD_L
    pre = cb_ref[...] + cw_ref[0:1, :] * ext[pl.ds(HALO - 3, L), :]
    for kk in range(1, 4):
        pre = pre + cw_ref[kk:kk + 1, :] * ext[pl.ds(HALO - 3 + kk, L), :]
    return pre


def _ssd_common(xbc_ref, halo_ref, dt_ref, cw_ref, cb_ref, dtb_ref, alog_ref, ext, first):
    L = SSD_L
    tri, expand = _ssd_consts()
    ext[0:HALO, :] = halo_ref[...].astype(F32) * (1.0 - first.astype(F32))
    ext[HALO:HALO + L, :] = xbc_ref[...].astype(F32)
    pre = _ssd_conv4(ext, cw_ref, cb_ref)
    sg = _sig(pre)
    act = pre * sg
    lane = lax.broadcasted_iota(jnp.int32, (1, LANE), 1)
    m4 = lane < SSD_H
    raw = dt_ref[...].astype(F32) + dtb_ref[...]
    dtv = jnp.where(m4, _softplus(raw), 0.0)
    av = jnp.where(m4, -jnp.exp(alog_ref[...]), 0.0)
    adt = dtv * av
    acs = _dot(tri.astype(F32), adt, precision=HI)
    acs_b = _dot(acs, expand, precision=HI)
    dt_b = _dot(dtv, expand, precision=HI)
    return dict(tri=tri, expand=expand, pre=pre, sg=sg, act=act, raw=raw, dtv=dtv, av=av, m4=m4, acs=acs, acs_b=acs_b,
                dt_b=dt_b, lane=lane)


def _head_terms(cm, h):
    L = SSD_L
    acs, tri = cm["acs"], cm["tri"]
    lane_l = lax.broadcasted_iota(jnp.int32, (L, LANE), 1)
    sub_l = lax.broadcasted_iota(jnp.int32, (LANE, L), 0)
    col = jnp.sum(jnp.where(lane_l == h, acs, 0.0), axis=1, keepdims=True)
    row = jnp.sum(jnp.where(sub_l == h, acs.T, 0.0), axis=0, keepdims=True)
    dec = jnp.where(tri, jnp.exp(jnp.where(tri, col - row, NEG)), 0.0)
    rowi = lax.broadcasted_iota(jnp.int32, (L, 1), 0)
    last = jnp.sum(jnp.where(rowi == L - 1, col, 0.0), axis=0, keepdims=True)
    dte = jnp.exp(last - col)
    return col, dec, last, dte


def _ssd_fwd(proj, cw, cb, dtb, alog, dvec, nw):
    s = proj.shape[0]
    L = SSD_L
    nc = s // L

    def body(z_ref, xbc_ref, halo_ref, dt_ref, cw_ref, cb_ref, dtb_ref, alog_ref, d_ref, nw_ref, y_ref, ypre_ref, st_ref, ext, state):
        i = pl.program_id(0)

        @pl.when(i == 0)
        def _():
            state[...] = jnp.zeros_like(state)

        cm = _ssd_common(xbc_ref, halo_ref, dt_ref, cw_ref, cb_ref, dtb_ref, alog_ref, ext, i == 0)
        act = cm["act"]
        xs = act[:, 0:256]
        bm = (act[:, 256:384], act[:, 384:512])
        cmat = (act[:, 512:640].astype(BF16), act[:, 640:768].astype(BF16))
        xdt = xs * cm["dt_b"]
        prev = state[...]
        st_ref[...] = prev
        prev_bf = prev.astype(BF16)
        gm = [_dot(cmat[g], bm[g].astype(BF16), NT) for g in range(2)]
        lane2 = lax.broadcasted_iota(jnp.int32, (1, SSD_DIM), 1)
        rows2 = lax.broadcasted_iota(jnp.int32, (SSD_DIM, 1), 0)
        ydiag = jnp.zeros((L, SSD_DIM), F32)
        contrib = jnp.zeros((SSD_DIM, LANE), F32)
        cd_rows = jnp.zeros((SSD_DIM, 1), F32)
        for h in range(SSD_H):
            g = h // 2
            col, dec, last, dte = _head_terms(cm, h)
            mh = (lane2 >= 64 * h) & (lane2 < 64 * h + 64)
            xm = jnp.where(mh, xdt, 0.0).astype(BF16)
            ydiag += _dot((gm[g] * dec).astype(BF16), xm)
            contrib += _dot(xm, (bm[g] * dte).astype(BF16), TN)
            cd_rows += jnp.where((rows2 >= 64 * h) & (rows2 < 64 * h + 64), jnp.exp(last), 0.0)
        yo = jnp.where(lane2 < 128, _dot(cmat[0], prev_bf, NT), _dot(cmat[1], prev_bf, NT))
        y = ydiag + yo * jnp.exp(cm["acs_b"]) + xs * d_ref[...]
        state[...] = prev * cd_rows + contrib
        ypre_ref[...] = y
        zz = z_ref[...].astype(F32)
        gt = y * zz * _sig(zz)
        y_ref[...] = (gt * lax.rsqrt(jnp.mean(gt * gt, axis=-1, keepdims=True) + EPS) * nw_ref[...]).astype(BF16)

    def vec(w):
        return pl.BlockSpec((1, w), lambda i: (0, 0))

    return pl.pallas_call(
        body, name="ssd_fwd", grid=(nc,),
        out_shape=(jax.ShapeDtypeStruct((s, SSD_DIM), BF16), jax.ShapeDtypeStruct((s, SSD_DIM), F32),
                   jax.ShapeDtypeStruct((nc, SSD_DIM, LANE), F32)),
        in_specs=[pl.BlockSpec((L, SSD_DIM), lambda i: (i, 5)), pl.BlockSpec((L, SSD_CONV), lambda i: (i, 2)),
                  pl.BlockSpec((HALO, SSD_CONV), lambda i: (_prev_halo(i, L), 2)), pl.BlockSpec((L, LANE), lambda i: (i, 18)),
                  pl.BlockSpec((4, SSD_CONV), lambda i: (0, 0)), vec(SSD_CONV), vec(LANE), vec(LANE), vec(SSD_DIM), vec(SSD_DIM)],
        out_specs=(pl.BlockSpec((L, SSD_DIM), lambda i: (i, 0)), pl.BlockSpec((L, SSD_DIM), lambda i: (i, 0)),
                   pl.BlockSpec((None, SSD_DIM, LANE), lambda i: (i, 0, 0))),
        scratch_shapes=[pltpu.VMEM((L + HALO, SSD_CONV), F32), pltpu.VMEM((SSD_DIM, LANE), F32)], compiler_params=_cp("arbitrary"),
    )(proj, proj, proj, proj, cw, cb, dtb, alog, dvec, nw)


def _ssd_bwd(proj, dcat, ypre, states, cw, cb, dtb, alog, dvec, nw):
    s = proj.shape[0]
    L = SSD_L
    nc = s // L

    def body(z_ref, xbc_ref, halo_ref, dt_ref, dy_ref, ypre_ref, st_ref, cw_ref, cb_ref, dtb_ref, alog_ref, d_ref, nw_ref,
             dz_ref, dxbc_ref, ddt_ref, dcw_ref, dcb_ref, ddtb_ref, dalog_ref, dd_ref, dnw_ref, ext, ext2, carry, dstate, ddl):
        i = pl.program_id(0)
        r = nc - 1 - i

        @pl.when(i == 0)
        def _():
            for ref in (dcw_ref, dcb_ref, ddtb_ref, dalog_ref, dd_ref, dnw_ref, carry, dstate, ddl):
                ref[...] = jnp.zeros_like(ref)

        cm = _ssd_common(xbc_ref, halo_ref, dt_ref, cw_ref, cb_ref, dtb_ref, alog_ref, ext, r == 0)
        tri, expand, act = cm["tri"], cm["expand"], cm["act"]
        xs = act[:, 0:256]
        bm = (act[:, 256:384], act[:, 384:512])
        cmat = (act[:, 512:640], act[:, 640:768])
        bm_bf = [v.astype(BF16) for v in bm]
        cm_bf = [v.astype(BF16) for v in cmat]
        dt_b = cm["dt_b"]
        xdt = xs * dt_b
        xdt_bf = xdt.astype(BF16)
        ea_b = jnp.exp(cm["acs_b"])
        prev = st_ref[...]
        prev_bf = prev.astype(BF16)
        lane2 = lax.broadcasted_iota(jnp.int32, (1, SSD_DIM), 1)
        rows2 = lax.broadcasted_iota(jnp.int32, (SSD_DIM, 1), 0)
        lane_l = lax.broadcasted_iota(jnp.int32, (L, LANE), 1)
        rowi = lax.broadcasted_iota(jnp.int32, (L, 1), 0)

        y = ypre_ref[...]
        zz = z_ref[...].astype(F32)
        sz = _sig(zz)
        gt = y * zz * sz
        dgt, dwt = _rms_bwd_math(gt, nw_ref[...], dy_ref[...].astype(F32))
        dnw_ref[...] += _rowsum(dwt)
        dy = dgt * zz * sz
        dz_ref[...] = (dgt * y * sz * (1.0 + zz * (1.0 - sz))).astype(BF16)

        ddl[0:1, :] += _rowsum(dy * xs)
        dxs = dy * d_ref[...]

        yo = jnp.where(lane2 < 128, _dot(cm_bf[0], prev_bf, NT), _dot(cm_bf[1], prev_bf, NT))
        dacs_b = dy * yo * ea_b
        dyo = dy * ea_b
        dyo_g = (jnp.where(lane2 < 128, dyo, 0.0).astype(BF16), jnp.where(lane2 >= 128, dyo, 0.0).astype(BF16))
        dc = [_dot(dyo_g[g], prev_bf) for g in range(2)]
        dprev = _dot(dyo_g[0], cm_bf[0], TN) + _dot(dyo_g[1], cm_bf[1], TN)

        gm = [_dot(cm_bf[g], bm_bf[g], NT) for g in range(2)]
        dgm = [jnp.zeros((L, L), F32), jnp.zeros((L, L), F32)]
        db = [jnp.zeros((L, LANE), F32), jnp.zeros((L, LANE), F32)]
        dxdt = jnp.zeros((L, SSD_DIM), F32)
        dacs = jnp.zeros((L, LANE), F32)
        dlast = jnp.zeros((1, LANE), F32)
        cd_rows = jnp.zeros((SSD_DIM, 1), F32)
        dst = dstate[...]
        dst_bf = dst.astype(BF16)
        dsp = dst * prev
        ones = jnp.ones((L, LANE), F32)
        for h in range(SSD_H):
            g = h // 2
            col, dec, last, dte = _head_terms(cm, h)
            mh = (lane2 >= 64 * h) & (lane2 < 64 * h + 64)
            rh = (rows2 >= 64 * h) & (rows2 < 64 * h + 64)
            sc = gm[g] * dec
            xm = jnp.where(mh, xdt, 0.0).astype(BF16)
            dym = jnp.where(mh, dy, 0.0).astype(BF16)
            dsc = _dot(dym, xdt_bf, NT)
            dxdt += _dot(sc.astype(BF16), dym, TN)
            dgm[g] += dsc * dec
            dd = dsc * sc
            rs = jnp.sum(dd, axis=1, keepdims=True)
            cs = _dot(dd, ones, TN, precision=HI)
            dacs += jnp.where(lane_l == h, rs - cs, 0.0)
            bd = (bm[g] * dte).astype(BF16)
            dxdt += jnp.where(mh, _dot(bd, dst_bf, NT), 0.0)
            dbd = _dot(xm, dst_bf)
            db[g] += dbd * dte
            tt = jnp.sum(dbd * bm[g], axis=1, keepdims=True) * dte
            dacs += jnp.where(lane_l == h, -tt, 0.0)
            cdh = jnp.exp(last)
            dcd = jnp.sum(jnp.sum(jnp.where(rh, dsp, 0.0), axis=1, keepdims=True), axis=0, keepdims=True)
            dlast += jnp.where(cm["lane"] == h, jnp.sum(tt, axis=0, keepdims=True) + dcd * cdh, 0.0)
            cd_rows += jnp.where(rh, cdh, 0.0)
        dacs += jnp.where(rowi == L - 1, dlast, 0.0)
        dacs += _dot(dacs_b, expand, NT, precision=HI)
        dstate[...] = dprev + dst * cd_rows

        for g in range(2):
            dgb = dgm[g].astype(BF16)
            dc[g] += _dot(dgb, bm_bf[g])
            db[g] += _dot(dgb, cm_bf[g], TN)

        dadt = _dot(tri.astype(F32), dacs, TN, precision=HI)
        ddtv = dadt * cm["av"] + _dot(dxdt * xs, expand, NT, precision=HI)
        dalog_ref[...] += _rowsum(dadt * cm["dtv"]) * cm["av"]
        dxs += dxdt * dt_b
        draw = jnp.where(cm["m4"], ddtv * _sig(cm["raw"]), 0.0)
        ddtb_ref[...] += _rowsum(draw)
        ddt_ref[...] = draw.astype(BF16)

        dact = jnp.concatenate([dxs, db[0], db[1], dc[0], dc[1]], axis=1)
        sg, pre = cm["sg"], cm["pre"]
        dpre = dact * sg * (1.0 + pre * (1.0 - sg))
        dcb_ref[...] += _rowsum(dpre)
        for kk in range(4):
            dcw_ref[kk:kk + 1, :] += _rowsum(dpre * ext[pl.ds(HALO - 3 + kk, L), :])
        ext2[0:L, :] = dpre
        ext2[L:L + HALO, :] = carry[...]
        dx = cw_ref[3:4, :] * ext2[pl.ds(0, L), :]
        for kk in range(3):
            dx = dx + cw_ref[kk:kk + 1, :] * ext2[pl.ds(3 - kk, L), :]
        dxbc_ref[...] = dx.astype(BF16)
        carry[...] = dpre[0:HALO, :]

        @pl.when(i == nc - 1)
        def _():
            dd_ref[...] = _dot(ddl[...], expand, NT, precision=HI)

    def vec(w):
        return pl.BlockSpec((1, w), lambda i: (0, 0))

    def rv(i):
        return nc - 1 - i

    return pl.pallas_call(
        body, name="ssd_bwd", grid=(nc,),
        out_shape=(jax.ShapeDtypeStruct((s, SSD_DIM), BF16), jax.ShapeDtypeStruct((s, SSD_CONV), BF16), jax.ShapeDtypeStruct((s, LANE), BF16),
                   jax.ShapeDtypeStruct((4, SSD_CONV), F32), jax.ShapeDtypeStruct((1, SSD_CONV), F32), jax.ShapeDtypeStruct((1, LANE), F32),
                   jax.ShapeDtypeStruct((1, LANE), F32), jax.ShapeDtypeStruct((8, LANE), F32), jax.ShapeDtypeStruct((1, SSD_DIM), F32)),
        in_specs=[pl.BlockSpec((L, SSD_DIM), lambda i: (rv(i), 5)), pl.BlockSpec((L, SSD_CONV), lambda i: (rv(i), 2)),
                  pl.BlockSpec((HALO, SSD_CONV), lambda i: (_prev_halo(rv(i), L), 2)), pl.BlockSpec((L, LANE), lambda i: (rv(i), 18)),
                  pl.BlockSpec((L, SSD_DIM), lambda i: (rv(i), 5)), pl.BlockSpec((L, SSD_DIM), lambda i: (rv(i), 0)),
                  pl.BlockSpec((None, SSD_DIM, LANE), lambda i: (rv(i), 0, 0)),
                  pl.BlockSpec((4, SSD_CONV), lambda i: (0, 0)), vec(SSD_CONV), vec(LANE), vec(LANE), vec(SSD_DIM), vec(SSD_DIM)],
        out_specs=(pl.BlockSpec((L, SSD_DIM), lambda i: (rv(i), 0)), pl.BlockSpec((L, SSD_CONV), lambda i: (rv(i), 0)),
                   pl.BlockSpec((L, LANE), lambda i: (rv(i), 0)), pl.BlockSpec((4, SSD_CONV), lambda i: (0, 0)), vec(SSD_CONV),
                   vec(LANE), vec(LANE), pl.BlockSpec((8, LANE), lambda i: (0, 0)), vec(SSD_DIM)),
        scratch_shapes=[pltpu.VMEM((L + HALO, SSD_CONV), F32), pltpu.VMEM((L + HALO, SSD_CONV), F32), pltpu.VMEM((HALO, SSD_CONV), F32),
                        pltpu.VMEM((SSD_DIM, LANE), F32), pltpu.VMEM((8, SSD_DIM), F32)],
        compiler_params=_cp("arbitrary"),
    )(proj, proj, proj, proj, dcat, ypre, states, cw, cb, dtb, alog, dvec, nw)


def _adamw(parts, w, m, v, name):
    r, c = w.shape
    tr = r
    for cand in (256, 128, 64, 32, 16, 8):
        if r % cand == 0 and (cand * c * 4) <= 2 * 1024 * 1024:
            tr = cand
            break
    c1 = 1.0 - B1 ** STEP
    c2 = 1.0 - B2 ** STEP

    def body(p_ref, w_ref, m_ref, v_ref, g_ref, d_ref, nm_ref, nv_ref):
        g = p_ref[0].astype(F32)
        for dev in range(1, NDEV):
            g = g + p_ref[dev].astype(F32)
        mn = B1 * m_ref[...] + (1.0 - B1) * g
        vn = B2 * v_ref[...] + (1.0 - B2) * (g * g)
        g_ref[...] = g
        nm_ref[...] = mn
        nv_ref[...] = vn
        d_ref[...] = -LR * ((mn / c1) / (jnp.sqrt(vn / c2) + AEPS) + WD * w_ref[...])

    blk = pl.BlockSpec((tr, c), lambda i: (i, 0))
    out = jax.ShapeDtypeStruct((r, c), F32)
    return pl.pallas_call(
        body, name=name, grid=(r // tr,), out_shape=(out, out, out, out),
        in_specs=[pl.BlockSpec((NDEV, tr, c), lambda i: (0, i, 0)), blk, blk, blk], out_specs=(blk, blk, blk, blk),
        compiler_params=_cp("parallel"),
    )(parts, w, m, v)


def _pad_win(w):
    z = lambda n: jnp.zeros((w.shape[0], n), w.dtype)
    return jnp.concatenate([w[:, :384], z(64), w[:, 384:416], z(32), w[:, 416:], z(124)], axis=1)


def _unpad_win(g):
    return jnp.concatenate([g[:, :384], g[:, 448:480], g[:, 512:2308]], axis=1)


def _pad_wout(w):
    att = jnp.pad(w[:512].reshape(HEADS, 64, D), ((0, 0), (64, 0), (0, 0))).reshape(HEADS * LANE, D)
    return jnp.concatenate([att, w[512:]], axis=0)


def _unpad_wout(g):
    att = g[:HEADS * LANE].reshape(HEADS, LANE, D)[:, 64:, :].reshape(512, D)
    return jnp.concatenate([att, g[HEADS * LANE:]], axis=0)


def _lanes(v, n=LANE):
    return jnp.pad(v, (0, n - v.shape[0])).reshape(1, n)


def _prep_ffn(big):
    return {"wout": _pad_wout(big["w_out"].reshape(1024, D)), "wup": big["ffn_w_up"], "fcw": big["ffn_conv_w"].astype(F32),
            "wdown": big["ffn_w_down"].reshape(4, FB, D)}


def _prep_layer(big, small, l):
    p = _prep_ffn(big) if "w_out" in big else {}
    p["win"] = _pad_win(big["w_in"].reshape(D, 2212))
    p["wq"] = jnp.pad(big["mla_w_q_up"], ((0, 0), (0, 0), (0, LANE - 96)))
    p["wkv"] = big["mla_w_kv_up"]
    p["scw"] = big["sc_conv_w"].astype(F32).transpose(1, 0, 2).reshape(3, SC)
    p["ssdcw"] = big["ssd_conv_w"].astype(F32).transpose(1, 0, 2).reshape(4, SSD_CONV)
    for nm in ("norm_mix_pre", "norm_mix_post", "norm_ffn_pre", "norm_ffn_post", "mla_q_norm", "mla_kv_norm", "ssd_conv_b", "ssd_norm"):
        p[nm] = small[nm][l].reshape(1, -1)
    p["dtb"] = _lanes(small["ssd_dt_bias"][l])
    p["alog"] = _lanes(small["ssd_a_log"][l])
    p["dvec"] = jnp.repeat(small["ssd_d"][l], 64).reshape(1, SSD_DIM)
    p["fcb"] = small["ffn_conv_b"][l].reshape(NDEV, 1, FB)
    return p


def _rope_tables(positions):
    inv_freq = 1.0 / (ROPE_THETA ** (jnp.arange(0, ROPE, 2, dtype=F32) / ROPE))
    ang = positions.astype(F32)[:, None] * inv_freq
    cos, sin = jnp.cos(ang), jnp.sin(ang)
    s = positions.shape[0]
    z = lambda n: jnp.zeros((s, n), F32)
    tc = jnp.concatenate([jnp.ones((s, 64), F32), cos, cos, z(32)], axis=1)
    ta = jnp.concatenate([z(64), -sin, z(48)], axis=1)
    tb = jnp.concatenate([z(80), sin, z(32)], axis=1)
    return tc, ta, tb


def _layer_fwd(xv, p, tabs, prefetch=None, prep_rest=None):
    h = _rms(xv, p["norm_mix_pre"], BF16, "rms_pre")
    proj = _mm_rows("in_proj", h, p["win"], BF16, NN)
    q, k, kv = _mla_prep(proj, tabs, p["mla_q_norm"], p["mla_kv_norm"], p["wq"], p["wkv"])
    o, lse, gathered = _flash_fwd(q, k, kv, prefetch)
    if prep_rest is not None:
        p = {**p, **prep_rest(gathered)}
    yconv = _sconv_fwd(proj, p["scw"])
    yssd, ypre, states = _ssd_fwd(proj, p["ssdcw"], p["ssd_conv_b"], p["dtb"], p["alog"], p["dvec"], p["ssd_norm"])
    cat = jnp.concatenate([o, yconv, yssd], axis=1)
    mixed = _mm_rows("out_proj", cat, p["wout"], F32, NN)
    x1 = _add_rms(xv, mixed, p["norm_mix_post"], "add_rms")
    h2 = _rms(x1, p["norm_ffn_pre"], BF16, "rms_pre")
    upre = _mm_up(h2, p["wup"])
    act = _ffn_act(upre, p["fcw"], p["fcb"])
    f = _mm_down(act, p["wdown"])
    x2 = _add_rms(x1, f, p["norm_ffn_post"], "add_rms")
    saved = dict(x=xv, h=h, proj=proj, q=q, k=k, kv=kv, lse=lse, ypre=ypre, states=states, cat=cat, mixed=mixed, x1=x1, h2=h2,
                 upre=upre, act=act, f=f)
    return x2, saved, p, gathered


def _pack_grads(grads, group_ids):
    return [_group_pack(GROUPS[gi], lambda n: grads[n].reshape((NDEV,) + _rows2(n, True)), (NDEV,)) for gi in group_ids]


def _layer_bwd(dx2, sv, p, tabs, exchange=False, pending=None):
    df, g_nfpo = _rms_bwd(sv["f"], p["norm_ffn_post"], dx2, None, BF16, "rms_bwd_post")
    dact = _mm_dact(df, p["wdown"])
    g_wdown = _mm_dwdown(sv["act"], df)
    du, g_fcb = _ffn_bwd_a(sv["upre"], dact, p["fcw"], p["fcb"])
    dupre, g_fcw = _ffn_bwd_b(du, sv["upre"], p["fcw"])
    dh2 = _mm_dh2(dupre, p["wup"])
    g_wup = _mm_dwup(sv["h2"], dupre)
    dx1, g_nfp = _rms_bwd(sv["x1"], p["norm_ffn_pre"], dh2, dx2, F32, "rms_bwd_pre")
    dmixed, g_nmpo = _rms_bwd(sv["mixed"], p["norm_mix_post"], dx1, None, BF16, "rms_bwd_post")
    dcat = _mm_rows("dcat", dmixed, p["wout"], BF16, NT)
    g_wout = _mm_wgrad("dw_out", sv["cat"], dmixed, BF16)
    big = {
        "w_out": _unpad_wout(g_wout).reshape(NDEV, 128, D),
        "ffn_w_up": g_wup,
        "ffn_conv_w": g_fcw.astype(BF16),
        "ffn_w_down": g_wdown.reshape(NDEV, 352, D),
    }
    outgoing = _pack_grads(big, FFN_SIDE) + (pending or []) if exchange else None
    dq, dk, dv, received = _flash_bwd(sv["q"], sv["k"], sv["kv"], sv["cat"], dcat, sv["lse"], outgoing)
    dcq, dckv, dkr, g_wq, g_wkv, g_qn, g_kvn = _mla_prep_bwd(sv["proj"], tabs, p["mla_q_norm"], p["mla_kv_norm"], p["wq"], p["wkv"], dq, dk, dv)
    dscb, dscc, dsch, g_scw = _sconv_bwd(sv["proj"], dcat, p["scw"])
    dz, dxbc, ddt, g_cw, g_cb, g_dtb, g_alog, g_d, g_nw = _ssd_bwd(
        sv["proj"], dcat, sv["ypre"], sv["states"], p["ssdcw"], p["ssd_conv_b"], p["dtb"], p["alog"], p["dvec"], p["ssd_norm"])
    dproj = jnp.concatenate([dcq, dckv, dkr, dscb, dscc, dsch, dz, dxbc, ddt], axis=1)
    dh = _mm_rows("dh", dproj, p["win"], BF16, NT)
    g_win = _mm_wgrad("dw_in", sv["h"], dproj, BF16)
    dx, g_nmp = _rms_bwd(sv["x"], p["norm_mix_pre"], dh, dx1, F32, "rms_bwd_pre")
    big.update({
        "w_in": _unpad_win(g_win).reshape(NDEV, 128, 2212),
        "mla_w_q_up": g_wq[:, :, :96].astype(BF16),
        "mla_w_kv_up": g_wkv.astype(BF16),
        "sc_conv_w": g_scw.reshape(3, NDEV, 32).transpose(1, 0, 2).astype(BF16),
        "ssd_conv_w": g_cw.reshape(4, NDEV, 96).transpose(1, 0, 2).astype(BF16),
    })
    small = {
        "norm_mix_pre": g_nmp[0], "norm_mix_post": g_nmpo[0], "norm_ffn_pre": g_nfp[0], "norm_ffn_post": g_nfpo[0],
        "mla_q_norm": g_qn[0], "mla_kv_norm": g_kvn[0], "ssd_conv_b": g_cb[0], "ssd_dt_bias": g_dtb[0, :SSD_H],
        "ssd_a_log": g_alog[0, :SSD_H], "ssd_d": g_d[0, :SSD_H], "ssd_norm": g_nw[0], "ffn_conv_b": g_fcb.reshape(-1),
    }
    return dx, big, small, received


def _local_step(xv, positions, target, layers):
    tabs = _rope_tables(positions)
    saved = []
    for p in layers:
        xv, sv, _, _ = _layer_fwd(xv, p, tabs)
        saved.append(sv)
    loss, dx = _loss_head(xv, target)
    bigs, smalls = [None] * DEPTH, [None] * DEPTH
    for l in reversed(range(len(layers))):
        dx, bigs[l], smalls[l], _ = _layer_bwd(dx, saved[l], layers[l], tabs)
    return loss[0, 0], dx, bigs, smalls


def _pack_rows(flat, lead, width, mult):
    n = flat.shape[-1]
    rows = -(-n // (width * mult)) * mult
    pad = [(0, 0)] * (flat.ndim - 1) + [(0, rows * width - n)]
    return jnp.pad(flat, pad).reshape(lead + (rows, width))


def _rows2(n, layer=False):
    shape = SHAPES[n][1:] if layer else SHAPES[n]
    return (math.prod(shape[:-1]), shape[-1])


def _group_pack(group, get, lead):
    width, names = group
    pieces = []
    for n in names:
        rows, cols = _rows2(n, True)
        pad = [(0, 0)] * len(lead) + [(0, -rows % 16), (0, width - cols)]
        pieces.append(jnp.pad(get(n), pad))
    return pieces[0] if len(pieces) == 1 else jnp.concatenate(pieces, axis=len(lead))


def _group_unpack(group, buf, padded=False):
    _, names = group
    res, off = {}, 0
    for n in names:
        rows, cols = _rows2(n, True)
        rp = rows + (-rows % 16)
        res[n] = buf[:, off:off + (rp if padded else rows), :cols]
        off += rp
    return res


def kernel(x, positions, norm_mix_pre, norm_mix_post, norm_ffn_pre, norm_ffn_post, w_in, mla_q_norm, mla_w_q_up, mla_kv_norm, mla_w_kv_up, sc_conv_w, ssd_conv_w, ssd_conv_b, ssd_dt_bias, ssd_a_log, ssd_d, ssd_norm, w_out, ffn_w_up, ffn_conv_w, ffn_conv_b, ffn_w_down, loss_target, m_norm_mix_pre, m_norm_mix_post, m_norm_ffn_pre, m_norm_ffn_post, m_w_in, m_mla_q_norm, m_mla_w_q_up, m_mla_kv_norm, m_mla_w_kv_up, m_sc_conv_w, m_ssd_conv_w, m_ssd_conv_b, m_ssd_dt_bias, m_ssd_a_log, m_ssd_d, m_ssd_norm, m_w_out, m_ffn_w_up, m_ffn_conv_w, m_ffn_conv_b, m_ffn_w_down, v_norm_mix_pre, v_norm_mix_post, v_norm_ffn_pre, v_norm_ffn_post, v_w_in, v_mla_q_norm, v_mla_w_q_up, v_mla_kv_norm, v_mla_w_kv_up, v_sc_conv_w, v_ssd_conv_w, v_ssd_conv_b, v_ssd_dt_bias, v_ssd_a_log, v_ssd_d, v_ssd_norm, v_w_out, v_ffn_w_up, v_ffn_conv_w, v_ffn_conv_b, v_ffn_w_down):
    given = dict(locals())
    w = {n: given[n] for n in WEIGHTS}
    m = {n: given["m_" + n] for n in WEIGHTS}
    v = {n: given["v_" + n] for n in WEIGHTS}

    def shards(l, group_ids):
        return [_group_pack(GROUPS[gi], lambda n: w[n][l].astype(BF16).reshape(_rows2(n, True)), ()) for gi in group_ids]

    def unpacked(bufs, group_ids):
        big = {}
        for gi, buf in zip(group_ids, bufs):
            for n, piece in _group_unpack(GROUPS[gi], buf).items():
                big[n] = piece.reshape((NDEV,) + SHAPES[n][1:])
        return big

    small_w = {n: w[n] for n, _ in SMALL}
    tabs = _rope_tables(positions[0])
    xv, layers, saved = x[0], [], []
    att = _all_gather(shards(0, ATT_SIDE), "gather_weights")
    for l in range(DEPTH):
        prefetch = shards(l, FFN_SIDE) + (shards(l + 1, ATT_SIDE) if l + 1 < DEPTH else [])
        xv, sv, p, gathered = _layer_fwd(xv, _prep_layer(unpacked(att, ATT_SIDE), small_w, l), tabs, prefetch,
                                         lambda got: _prep_ffn(unpacked(got[:len(FFN_SIDE)], FFN_SIDE)))
        att = gathered[len(FFN_SIDE):]
        layers.append(p)
        saved.append(sv)
    loss, dx = _loss_head(xv, loss_target[0])
    loss = lax.psum(loss[0, 0], ("x", "y", "c"))

    smalls, pending = [None] * DEPTH, None
    recvs = [[None] * len(GROUPS) for _ in range(DEPTH)]
    for l in reversed(range(DEPTH)):
        dx, grads, smalls[l], received = _layer_bwd(dx, saved[l], layers[l], tabs, True, pending)
        for pos, gi in enumerate(FFN_SIDE):
            recvs[l][gi] = received[pos]
        if pending is not None:
            for pos, gi in enumerate(ATT_SIDE):
                recvs[l + 1][gi] = received[len(FFN_SIDE) + pos]
        pending = _pack_grads(grads, ATT_SIDE)
    for gi, buf in zip(ATT_SIDE, _all_to_all(pending, "exchange_grads")):
        recvs[0][gi] = buf
    out = {}
    for gi, g in enumerate(GROUPS):
        per_layer = [_group_unpack(g, recvs[l][gi], padded=True) for l in range(DEPTH)]
        for n in g[1]:
            r2 = _rows2(n)
            rows, cols = _rows2(n, True)
            parts = jnp.concatenate([per_layer[l][n] for l in range(DEPTH)], axis=1)
            parts = parts.reshape(NDEV, DEPTH, -1, cols)[:, :, :rows].reshape((NDEV,) + r2)
            res = _adamw(parts, w[n].reshape(r2), m[n].reshape(r2), v[n].reshape(r2), "adamw_" + n)
            out[n] = [a.reshape(SHAPES[n]) for a in res]

    sflat = jnp.concatenate([jnp.stack([smalls[l][n] for l in range(DEPTH)]).reshape(-1) for n, _ in SMALL])
    sparts = _all_gather([_pack_rows(sflat, (), LANE, 8)], "gather_small_grads")[0]
    pk = lambda d: _pack_rows(jnp.concatenate([d[n].reshape(-1) for n, _ in SMALL]), (), LANE, 8)
    res = _adamw(sparts, pk(w), pk(m), pk(v), "adamw_small")
    off = 0
    for n, width in SMALL:
        out[n] = [a.reshape(-1)[off:off + DEPTH * width].reshape(DEPTH, width) for a in res]
        off += DEPTH * width

    return (loss, dx[None], *[out[n][0] for n in WEIGHTS], *[out[n][1] for n in WEIGHTS],
            *[out[n][2] for n in WEIGHTS], *[out[n][3] for n in WEIGHTS])
```

```python
import functools
import math

import jax
import jax.numpy as jnp
from jax import lax
from jax.experimental import pallas as pl
from jax.experimental.pallas import tpu as pltpu

F32 = jnp.float32
BF16 = jnp.bfloat16

D = 1024
DEPTH = 4
NDEV = 8
HEADS = 8
QL = 256
KVL = 128
ROPE = 32
NOPE = 64
SC = 256
SSD_DIM = 256
SSD_CONV = 768
SSD_H = 4
SSD_L = 128
FFN = 2816
FB = 704
EPS = 1e-6
ROPE_THETA = 10000.0
ATT_SCALE = 96 ** -0.5
LOG2E = 1.4426950408889634
LR, B1, B2, AEPS, WD, STEP = 0.001, 0.9, 0.999, 1e-08, 0.01, 10

PW = 2432
CATW = 1536

ROW_TILE = 512
ROW_CHUNK = 16
NORM_CHUNK = 32
MM_TILE = 1024
ATT_TILE = 256
FWD_HEADS = 4
BWD_HEADS = 2
HALO = 16
LANE = 128
NEG = -1e30
HI = lax.Precision.HIGHEST
NN = (((1,), (0,)), ((), ()))
NT = (((1,), (1,)), ((), ()))
TN = (((0,), (0,)), ((), ()))
VMEM_LIMIT = 56 * 1024 * 1024

SHARDED = (
    ("w_in", (4, 128, 2212)),
    ("mla_w_q_up", (4, 256, 96)),
    ("mla_w_kv_up", (4, 128, 128)),
    ("sc_conv_w", (4, 3, 32)),
    ("ssd_conv_w", (4, 4, 96)),
    ("w_out", (4, 128, 1024)),
    ("ffn_w_up", (4, 1024, 704)),
    ("ffn_conv_w", (4, 3, 704)),
    ("ffn_w_down", (4, 352, 1024)),
)
SHAPES = dict(SHARDED)
GROUPS = (
    (2212, ("w_in",)),
    (1024, ("w_out", "ffn_w_down")),
    (704, ("ffn_w_up", "ffn_conv_w")),
    (96, ("mla_w_q_up", "ssd_conv_w", "sc_conv_w")),
    (128, ("mla_w_kv_up",)),
)
ATT_SIDE = (0, 3, 4)
FFN_SIDE = (1, 2)
SMALL = (
    ("norm_mix_pre", 1024), ("norm_mix_post", 1024), ("norm_ffn_pre", 1024), ("norm_ffn_post", 1024),
    ("mla_q_norm", 256), ("mla_kv_norm", 128), ("ssd_conv_b", 768), ("ssd_dt_bias", 4), ("ssd_a_log", 4),
    ("ssd_d", 4), ("ssd_norm", 256), ("ffn_conv_b", 5632),
)
WEIGHTS = ("norm_mix_pre", "norm_mix_post", "norm_ffn_pre", "norm_ffn_post", "w_in", "mla_q_norm", "mla_w_q_up",
           "mla_kv_norm", "mla_w_kv_up", "sc_conv_w", "ssd_conv_w", "ssd_conv_b", "ssd_dt_bias", "ssd_a_log", "ssd_d",
           "ssd_norm", "w_out", "ffn_w_up", "ffn_conv_w", "ffn_conv_b", "ffn_w_down")


def _dot(a, b, dims=NN, precision=None):
    return lax.dot_general(a, b, dims, precision=precision, preferred_element_type=F32)


def _sig(v):
    return 1.0 / (1.0 + jnp.exp(-v))


def _cp(*sem):
    return pltpu.CompilerParams(dimension_semantics=sem, vmem_limit_bytes=VMEM_LIMIT)


def _rowsum(v):
    return jnp.sum(v, axis=0, keepdims=True)


def _prev_halo(i, ts):
    return jnp.maximum(i * (ts // HALO) - 1, 0)


def _next_halo(i, ts, n):
    return jnp.minimum((i + 1) * (ts // HALO), n * (ts // HALO) - 1)


def _gather_plan(x_refs, out_refs, send_sems, recv_sems, local_sems):
    n = len(x_refs)
    x, y, cc = lax.axis_index("x"), lax.axis_index("y"), lax.axis_index("c")
    me, sibling = (x, y, cc), (x, y, 1 - cc)
    chips = [(1 - x, y), (x, 1 - y), (1 - x, 1 - y)]

    def rows(t, px, py, pc):
        return out_refs[t].at[4 * px + 2 * py + pc]

    def copy(t, k, block, to, own=False):
        return pltpu.make_async_remote_copy(
            src_ref=x_refs[t] if own else rows(t, *block), dst_ref=rows(t, *block),
            send_sem=send_sems.at[7 * t + k], recv_sem=recv_sems.at[7 * t + k], device_id=to, device_id_type=pl.DeviceIdType.MESH)

    def local(t):
        return pltpu.make_async_copy(x_refs[t], rows(t, *me), local_sems.at[t])

    def start():
        for t in range(n):
            local(t).start()
            copy(t, 0, me, sibling, own=True).start()
            for j, chip in enumerate(chips):
                copy(t, 1 + j, me, (*chip, cc), own=True).start()

    def finish():
        for j, chip in enumerate(chips):
            for t in range(n):
                copy(t, 1 + j, (*chip, cc), me).wait_recv()
                copy(t, 4 + j, (*chip, cc), sibling).start()
        for t in range(n):
            copy(t, 0, sibling, me).wait_recv()
            for j, chip in enumerate(chips):
                copy(t, 4 + j, (*chip, 1 - cc), me).wait_recv()
        for t in range(n):
            copy(t, 0, me, sibling, own=True).wait_send()
            for j, chip in enumerate(chips):
                copy(t, 1 + j, me, (*chip, cc), own=True).wait_send()
                copy(t, 4 + j, (*chip, cc), sibling).wait_send()
            local(t).wait()

    return start, finish


def _exchange_plan(x_refs, out_refs, send_sems, recv_sems, local_sems):
    n = len(x_refs)
    x, y, cc = lax.axis_index("x"), lax.axis_index("y"), lax.axis_index("c")
    me = 4 * x + 2 * y + cc

    def copies():
        res = [pltpu.make_async_copy(x_refs[t].at[me], out_refs[t].at[me], local_sems.at[t]) for t in range(n)]
        for k in range(1, NDEV):
            px = 1 - x if k & 4 else x
            py = 1 - y if k & 2 else y
            pc = 1 - cc if k & 1 else cc
            peer = 4 * px + 2 * py + pc
            for t in range(n):
                res.append(pltpu.make_async_remote_copy(
                    src_ref=x_refs[t].at[peer], dst_ref=out_refs[t].at[me], send_sem=send_sems.at[7 * t + k - 1],
                    recv_sem=recv_sems.at[7 * t + k - 1], device_id=(px, py, pc), device_id_type=pl.DeviceIdType.MESH))
        return res

    def start():
        for cp in copies():
            cp.start()

    def finish():
        for cp in copies():
            cp.wait()

    return start, finish


def _comm_scratch(n):
    return [pltpu.SemaphoreType.DMA((7 * n,)), pltpu.SemaphoreType.DMA((7 * n,)), pltpu.SemaphoreType.DMA((n,))]


ANY = pl.BlockSpec(memory_space=pl.ANY)


def _all_gather(xs, name):
    n = len(xs)

    def body(*refs):
        start, finish = _gather_plan(refs[:n], refs[n:2 * n], *refs[2 * n:])
        start()
        finish()

    return pl.pallas_call(
        body, name=name, out_shape=[jax.ShapeDtypeStruct((NDEV,) + a.shape, a.dtype) for a in xs],
        in_specs=[ANY] * n, out_specs=[ANY] * n, scratch_shapes=_comm_scratch(n),
    )(*xs)


def _all_to_all(xs, name):
    n = len(xs)

    def body(*refs):
        start, finish = _exchange_plan(refs[:n], refs[n:2 * n], *refs[2 * n:])
        start()
        finish()

    return pl.pallas_call(
        body, name=name, out_shape=[jax.ShapeDtypeStruct(a.shape, a.dtype) for a in xs],
        in_specs=[ANY] * n, out_specs=[ANY] * n, scratch_shapes=_comm_scratch(n),
    )(*xs)


def _mm(name, a, b, out_shape, grid, a_spec, b_spec, o_spec, dims, acc_shape):
    nk = grid[2]

    def single(a_ref, b_ref, o_ref):
        o_ref[...] = _dot(a_ref[...], b_ref[...], dims).astype(o_ref.dtype)

    if nk == 1:
        return pl.pallas_call(
            single, name=name, grid=grid, out_shape=out_shape, in_specs=[a_spec, b_spec], out_specs=o_spec,
            compiler_params=_cp("parallel", "parallel", "arbitrary"),
        )(a, b)

    def body(a_ref, b_ref, o_ref, acc_ref):
        k = pl.program_id(2)

        @pl.when(k == 0)
        def _():
            acc_ref[...] = jnp.zeros_like(acc_ref)

        acc_ref[...] += _dot(a_ref[...], b_ref[...], dims)

        @pl.when(k == nk - 1)
        def _():
            o_ref[...] = acc_ref[...].astype(o_ref.dtype)

    return pl.pallas_call(
        body, name=name, grid=grid, out_shape=out_shape, in_specs=[a_spec, b_spec], out_specs=o_spec,
        scratch_shapes=[pltpu.VMEM(acc_shape, F32)], compiler_params=_cp("parallel", "parallel", "arbitrary"),
    )(a, b)


def _mm_rows(name, a, w, out_dtype, dims):
    s, k = a.shape
    n = w.shape[1] if dims == NN else w.shape[0]
    tm = min(MM_TILE, s)
    return _mm(name, a, w, jax.ShapeDtypeStruct((s, n), out_dtype), (s // tm, 1, 1),
               pl.BlockSpec((tm, k), lambda i, j, kk: (i, 0)), pl.BlockSpec(w.shape, lambda i, j, kk: (0, 0)),
               pl.BlockSpec((tm, n), lambda i, j, kk: (i, 0)), dims, (tm, n))


def _mm_wgrad(name, a, g, out_dtype):
    s, m = a.shape
    n = g.shape[1]
    tk = min(MM_TILE, s)
    return _mm(name, a, g, jax.ShapeDtypeStruct((m, n), out_dtype), (1, 1, s // tk),
               pl.BlockSpec((tk, m), lambda i, j, kk: (kk, 0)), pl.BlockSpec((tk, n), lambda i, j, kk: (kk, 0)),
               pl.BlockSpec((m, n), lambda i, j, kk: (0, 0)), TN, (m, n))


def _mm_up(h2, wup):
    s = h2.shape[0]
    tm = min(MM_TILE, s)
    return _mm("ffn_up", h2, wup, jax.ShapeDtypeStruct((NDEV, s, FB), BF16), (NDEV, s // tm, 1),
               pl.BlockSpec((tm, D), lambda j, i, kk: (i, 0)), pl.BlockSpec((None, D, FB), lambda j, i, kk: (j, 0, 0)),
               pl.BlockSpec((None, tm, FB), lambda j, i, kk: (j, i, 0)), NN, (tm, FB))


def _mm_down(act, wdown):
    s = act.shape[1]
    tm = min(MM_TILE, s)
    return _mm("ffn_down", act, wdown, jax.ShapeDtypeStruct((s, D), F32), (s // tm, 1, 4),
               pl.BlockSpec((None, tm, FB), lambda i, j, kk: (kk, i, 0)), pl.BlockSpec((None, FB, D), lambda i, j, kk: (kk, 0, 0)),
               pl.BlockSpec((tm, D), lambda i, j, kk: (i, 0)), NN, (tm, D))


def _mm_dact(df, wdown):
    s = df.shape[0]
    tm = min(MM_TILE, s)
    return _mm("ffn_dact", df, wdown, jax.ShapeDtypeStruct((4, s, FB), BF16), (4, s // tm, 1),
               pl.BlockSpec((tm, D), lambda j, i, kk: (i, 0)), pl.BlockSpec((None, FB, D), lambda j, i, kk: (j, 0, 0)),
               pl.BlockSpec((None, tm, FB), lambda j, i, kk: (j, i, 0)), NT, (tm, FB))


def _mm_dwdown(act, df):
    s = df.shape[0]
    tk = min(MM_TILE, s)
    return _mm("ffn_dwdown", act, df, jax.ShapeDtypeStruct((4, FB, D), BF16), (4, 1, s // tk),
               pl.BlockSpec((None, tk, FB), lambda j, i, kk: (j, kk, 0)), pl.BlockSpec((tk, D), lambda j, i, kk: (kk, 0)),
               pl.BlockSpec((None, FB, D), lambda j, i, kk: (j, 0, 0)), TN, (FB, D))


def _mm_dh2(dupre, wup):
    s = dupre.shape[1]
    tm = min(MM_TILE, s)
    return _mm("ffn_dh2", dupre, wup, jax.ShapeDtypeStruct((s, D), BF16), (s // tm, 1, NDEV),
               pl.BlockSpec((None, tm, FB), lambda i, j, kk: (kk, i, 0)), pl.BlockSpec((None, D, FB), lambda i, j, kk: (kk, 0, 0)),
               pl.BlockSpec((tm, D), lambda i, j, kk: (i, 0)), NT, (tm, D))


def _mm_dwup(h2, dupre):
    s = h2.shape[0]
    tk = min(MM_TILE, s)
    return _mm("ffn_dwup", h2, dupre, jax.ShapeDtypeStruct((NDEV, D, FB), BF16), (NDEV, 1, s // tk),
               pl.BlockSpec((tk, D), lambda j, i, kk: (kk, 0)), pl.BlockSpec((None, tk, FB), lambda j, i, kk: (j, kk, 0)),
               pl.BlockSpec((None, D, FB), lambda j, i, kk: (j, 0, 0)), TN, (D, FB))


def _rms(xv, w, out_dtype, name):
    s, d = xv.shape
    ts = min(ROW_TILE, s)

    def body(x_ref, w_ref, o_ref):
        for r0 in range(0, ts, NORM_CHUNK):
            rows = pl.ds(r0, NORM_CHUNK)
            xf = x_ref[rows, :].astype(F32)
            r = lax.rsqrt(jnp.mean(xf * xf, axis=-1, keepdims=True) + EPS)
            o_ref[rows, :] = (xf * r * w_ref[...]).astype(o_ref.dtype)

    return pl.pallas_call(
        body, name=name, grid=(s // ts,), out_shape=jax.ShapeDtypeStruct((s, d), out_dtype),
        in_specs=[pl.BlockSpec((ts, d), lambda i: (i, 0)), pl.BlockSpec((1, d), lambda i: (0, 0))],
        out_specs=pl.BlockSpec((ts, d), lambda i: (i, 0)), compiler_params=_cp("parallel"),
    )(xv, w)


def _add_rms(xv, mv, w, name):
    s, d = xv.shape
    ts = min(ROW_TILE, s)

    def body(x_ref, m_ref, w_ref, o_ref):
        for r0 in range(0, ts, NORM_CHUNK):
            rows = pl.ds(r0, NORM_CHUNK)
            mf = m_ref[rows, :].astype(F32)
            r = lax.rsqrt(jnp.mean(mf * mf, axis=-1, keepdims=True) + EPS)
            o_ref[rows, :] = x_ref[rows, :] + mf * r * w_ref[...]

    return pl.pallas_call(
        body, name=name, grid=(s // ts,), out_shape=jax.ShapeDtypeStruct((s, d), F32),
        in_specs=[pl.BlockSpec((ts, d), lambda i: (i, 0)), pl.BlockSpec((ts, d), lambda i: (i, 0)), pl.BlockSpec((1, d), lambda i: (0, 0))],
        out_specs=pl.BlockSpec((ts, d), lambda i: (i, 0)), compiler_params=_cp("parallel"),
    )(xv, mv, w)


def _rms_bwd_math(xf, w, dy):
    r = lax.rsqrt(jnp.mean(xf * xf, axis=-1, keepdims=True) + EPS)
    xh = xf * r
    dxh = dy * w
    dx = r * (dxh - xh * jnp.mean(dxh * xh, axis=-1, keepdims=True))
    return dx, dy * xh


def _rms_bwd(xv, w, dy, dres, out_dtype, name):
    s, d = xv.shape
    ts = min(ROW_TILE, s)
    with_res = dres is not None

    def body(*refs):
        if with_res:
            x_ref, w_ref, dy_ref, dres_ref, dx_ref, dw_ref, acc = refs
        else:
            x_ref, w_ref, dy_ref, dx_ref, dw_ref, acc = refs
        acc[...] = jnp.zeros_like(acc)
        for r0 in range(0, ts, NORM_CHUNK):
            rows = pl.ds(r0, NORM_CHUNK)
            dx, dwt = _rms_bwd_math(x_ref[rows, :].astype(F32), w_ref[...], dy_ref[rows, :].astype(F32))
            if with_res:
                dx = dx + dres_ref[rows, :]
            dx_ref[rows, :] = dx.astype(dx_ref.dtype)
            acc[...] += dwt

        @pl.when(pl.program_id(0) == 0)
        def _():
            dw_ref[...] = jnp.zeros_like(dw_ref)

        dw_ref[...] += _rowsum(acc[...])

    row = pl.BlockSpec((ts, d), lambda i: (i, 0))
    vec = pl.BlockSpec((1, d), lambda i: (0, 0))
    ins = [xv, w, dy] + ([dres] if with_res else [])
    return pl.pallas_call(
        body, name=name, grid=(s // ts,),
        out_shape=(jax.ShapeDtypeStruct((s, d), out_dtype), jax.ShapeDtypeStruct((1, d), F32)),
        in_specs=[row, vec, row] + ([row] if with_res else []), out_specs=(row, vec),
        scratch_shapes=[pltpu.VMEM((NORM_CHUNK, d), F32)], compiler_params=_cp("arbitrary"),
    )(*ins)


def _loss_head(yv, tv):
    s, d = yv.shape
    ts = min(ROW_TILE, s)

    def body(y_ref, t_ref, l_ref, dy_ref):
        e = y_ref[...] - t_ref[...]
        dy_ref[...] = e * (1.0 / d)

        @pl.when(pl.program_id(0) == 0)
        def _():
            l_ref[...] = jnp.zeros_like(l_ref)

        tot = jnp.sum(jnp.sum(e * e, axis=1, keepdims=True), axis=0, keepdims=True)
        l_ref[...] += jnp.broadcast_to(tot * (0.5 / d), (8, LANE))

    row = pl.BlockSpec((ts, d), lambda i: (i, 0))
    return pl.pallas_call(
        body, name="loss_head", grid=(s // ts,),
        out_shape=(jax.ShapeDtypeStruct((8, LANE), F32), jax.ShapeDtypeStruct((s, d), F32)),
        in_specs=[row, row], out_specs=(pl.BlockSpec((8, LANE), lambda i: (0, 0)), row), compiler_params=_cp("arbitrary"),
    )(yv, tv)


def _rope(v, c, a, b):
    return v * c + pltpu.roll(v, LANE - 16, 1) * a + pltpu.roll(v, 16, 1) * b


def _rope_t(dv, c, a, b):
    return dv * c + pltpu.roll(dv * a, 16, 1) + pltpu.roll(dv * b, LANE - 16, 1)


def _mla_prep(proj, tabs, qnw, kvnw, wq, wkv):
    s = proj.shape[0]
    ts = min(ROW_TILE, s)
    tc, ta, tb = tabs

    def body(cq_ref, ckv_ref, kr_ref, c_ref, a_ref, b_ref, qnw_ref, kvnw_ref, wq_ref, wkv_ref, q_ref, k_ref, kv_ref):
        c, a, b = c_ref[...], a_ref[...], b_ref[...]
        cq = cq_ref[...].astype(F32)
        qn = (cq * lax.rsqrt(jnp.mean(cq * cq, axis=-1, keepdims=True) + EPS) * qnw_ref[...]).astype(BF16)
        ckv = ckv_ref[...].astype(F32)
        kvn = (ckv * lax.rsqrt(jnp.mean(ckv * ckv, axis=-1, keepdims=True) + EPS) * kvnw_ref[...]).astype(BF16)
        kr = _rope(kr_ref[...].astype(F32), c, a, b)
        lane = lax.broadcasted_iota(jnp.int32, (ts, LANE), 1)
        for h in range(HEADS):
            q_ref[h] = _rope(_dot(qn, wq_ref[h]), c, a, b).astype(BF16)
            kv = _dot(kvn, wkv_ref[h])
            kv_ref[h] = kv.astype(BF16)
            k_ref[h] = jnp.where(lane < NOPE, kv, kr).astype(BF16)

    tab = pl.BlockSpec((ts, LANE), lambda i: (i, 0))
    hd = pl.BlockSpec((HEADS, ts, LANE), lambda i: (0, i, 0))
    out = jax.ShapeDtypeStruct((HEADS, s, LANE), BF16)
    return pl.pallas_call(
        body, name="mla_prep", grid=(s // ts,), out_shape=(out, out, out),
        in_specs=[pl.BlockSpec((ts, QL), lambda i: (i, 0)), pl.BlockSpec((ts, LANE), lambda i: (i, 2)),
                  pl.BlockSpec((ts, LANE), lambda i: (i, 3)), tab, tab, tab,
                  pl.BlockSpec((1, QL), lambda i: (0, 0)), pl.BlockSpec((1, KVL), lambda i: (0, 0)),
                  pl.BlockSpec((HEADS, QL, LANE), lambda i: (0, 0, 0)), pl.BlockSpec((HEADS, KVL, LANE), lambda i: (0, 0, 0))],
        out_specs=(hd, hd, hd), compiler_params=_cp("parallel"),
    )(proj, proj, proj, tc, ta, tb, qnw, kvnw, wq, wkv)


def _mla_prep_bwd(proj, tabs, qnw, kvnw, wq, wkv, dq, dk, dv):
    s = proj.shape[0]
    ts = min(ROW_TILE, s)
    tc, ta, tb = tabs

    def body(cq_ref, ckv_ref, c_ref, a_ref, b_ref, qnw_ref, kvnw_ref, wq_ref, wkv_ref, dq_ref, dk_ref, dv_ref,
             dcq_ref, dckv_ref, dkr_ref, dwq_ref, dwkv_ref, dqnw_ref, dkvnw_ref):
        @pl.when(pl.program_id(0) == 0)
        def _():
            dwq_ref[...] = jnp.zeros_like(dwq_ref)
            dwkv_ref[...] = jnp.zeros_like(dwkv_ref)
            dqnw_ref[...] = jnp.zeros_like(dqnw_ref)
            dkvnw_ref[...] = jnp.zeros_like(dkvnw_ref)

        c, a, b = c_ref[...], a_ref[...], b_ref[...]
        cq = cq_ref[...].astype(F32)
        qn = (cq * lax.rsqrt(jnp.mean(cq * cq, axis=-1, keepdims=True) + EPS) * qnw_ref[...]).astype(BF16)
        ckv = ckv_ref[...].astype(F32)
        kvn = (ckv * lax.rsqrt(jnp.mean(ckv * ckv, axis=-1, keepdims=True) + EPS) * kvnw_ref[...]).astype(BF16)
        lane = lax.broadcasted_iota(jnp.int32, (ts, LANE), 1)
        dqn = jnp.zeros((ts, QL), F32)
        dkvn = jnp.zeros((ts, KVL), F32)
        dkr = jnp.zeros((ts, LANE), F32)
        for h in range(HEADS):
            dqh = _rope_t(dq_ref[h], c, a, b).astype(BF16)
            dwq_ref[h] += _dot(qn, dqh, TN)
            dqn += _dot(dqh, wq_ref[h], NT)
            dkh = dk_ref[h].astype(F32)
            dkvh = jnp.where(lane < NOPE, dkh, dv_ref[h].astype(F32)).astype(BF16)
            dkr += jnp.where(lane < NOPE, 0.0, dkh)
            dwkv_ref[h] += _dot(kvn, dkvh, TN)
            dkvn += _dot(dkvh, wkv_ref[h], NT)
        dkr_ref[...] = _rope_t(dkr, c, a, b).astype(BF16)
        dcq, dwt = _rms_bwd_math(cq, qnw_ref[...], dqn)
        dcq_ref[...] = dcq.astype(BF16)
        dqnw_ref[...] += _rowsum(dwt)
        dckv, dwt = _rms_bwd_math(ckv, kvnw_ref[...], dkvn)
        dckv_ref[...] = dckv.astype(BF16)
        dkvnw_ref[...] += _rowsum(dwt)

    tab = pl.BlockSpec((ts, LANE), lambda i: (i, 0))
    hd = pl.BlockSpec((HEADS, ts, LANE), lambda i: (0, i, 0))
    wq_spec = pl.BlockSpec((HEADS, QL, LANE), lambda i: (0, 0, 0))
    wkv_spec = pl.BlockSpec((HEADS, KVL, LANE), lambda i: (0, 0, 0))
    return pl.pallas_call(
        body, name="mla_prep_bwd", grid=(s // ts,),
        out_shape=(jax.ShapeDtypeStruct((s, QL), BF16), jax.ShapeDtypeStruct((s, KVL), BF16), jax.ShapeDtypeStruct((s, LANE), BF16),
                   jax.ShapeDtypeStruct((HEADS, QL, LANE), F32), jax.ShapeDtypeStruct((HEADS, KVL, LANE), F32),
                   jax.ShapeDtypeStruct((1, QL), F32), jax.ShapeDtypeStruct((1, KVL), F32)),
        in_specs=[pl.BlockSpec((ts, QL), lambda i: (i, 0)), pl.BlockSpec((ts, LANE), lambda i: (i, 2)), tab, tab, tab,
                  pl.BlockSpec((1, QL), lambda i: (0, 0)), pl.BlockSpec((1, KVL), lambda i: (0, 0)), wq_spec, wkv_spec, hd, hd, hd],
        out_specs=(pl.BlockSpec((ts, QL), lambda i: (i, 0)), pl.BlockSpec((ts, KVL), lambda i: (i, 0)), tab, wq_spec, wkv_spec,
                   pl.BlockSpec((1, QL), lambda i: (0, 0)), pl.BlockSpec((1, KVL), lambda i: (0, 0))),
        compiler_params=_cp("arbitrary"),
    )(proj, proj, tc, ta, tb, qnw, kvnw, wq, wkv, dq, dk, dv)


def _transpose_bf16(v):
    return v.astype(F32).T.astype(BF16)


def _flash_fwd(q, k, kv, prefetch=None):
    s = q.shape[1]
    t = min(ATT_TILE, s)
    n = s // t
    g = FWD_HEADS
    nx = len(prefetch) if prefetch else 0

    def body(*refs):
        q_ref, k_ref, kv_ref = refs[:3]
        o_ref, lse_ref = refs[3 + nx:5 + nx]
        kvt_sc = refs[5 + 2 * nx]
        step = pl.program_id(0) * n + pl.program_id(1)
        if nx:
            start, finish = _gather_plan(refs[3:3 + nx], refs[5 + nx:5 + 2 * nx], *refs[6 + 2 * nx:])
            pl.when(step == 0)(start)
        attend(q_ref, k_ref, kv_ref, o_ref, lse_ref, kvt_sc)
        if nx:
            pl.when(step == (HEADS // g) * n - 1)(finish)

    def attend(q_ref, k_ref, kv_ref, o_ref, lse_ref, kvt_sc):
        i = pl.program_id(1)

        @pl.when(i == 0)
        def _():
            ones_rows = lax.broadcasted_iota(jnp.int32, (LANE, s), 0) < NOPE
            for hh in range(g):
                kvt_sc[hh] = jnp.where(ones_rows, 1.0, kv_ref[hh].astype(F32).T).astype(BF16)

        qt = [(q_ref[hh].astype(F32) * (ATT_SCALE * LOG2E)).T.astype(BF16) for hh in range(g)]
        kpos = lax.broadcasted_iota(jnp.int32, (t, t), 0)
        qpos = lax.broadcasted_iota(jnp.int32, (t, t), 1)

        def chunk(j, carry, diagonal):
            start = pl.multiple_of(j * t, t)
            scs = [_dot(k_ref[hh, pl.ds(start, t), :], qt[hh]) for hh in range(g)]
            soft = []
            for hh in range(g):
                sc = scs[hh]
                if diagonal:
                    sc = jnp.where(qpos >= kpos, sc, NEG)
                m_new = jnp.maximum(carry[hh][0], jnp.max(sc, axis=0, keepdims=True))
                soft.append((m_new, jnp.exp2(carry[hh][0] - m_new), jnp.exp2(sc - m_new).astype(BF16)))
            pvs = [_dot(kvt_sc[hh, :, pl.ds(start, t)], soft[hh][2]) for hh in range(g)]
            return tuple((soft[hh][0], soft[hh][1] * carry[hh][1] + pvs[hh]) for hh in range(g))

        init = tuple((jnp.full((1, t), NEG, F32), jnp.zeros((LANE, t), F32)) for _ in range(g))
        carry = lax.fori_loop(0, i, lambda j, c: chunk(j, c, False), init)
        carry = chunk(i, carry, True)
        for hh in range(g):
            m, acc = carry[hh]
            l = acc[0:1, :]
            o_ref[:, hh * LANE:(hh + 1) * LANE] = (acc / l).T.astype(BF16)
            lse_ref[hh] = m + jnp.log2(l)

    whole = pl.BlockSpec((g, s, LANE), lambda h, i: (h, 0, 0))
    res = pl.pallas_call(
        body, name="flash_fwd_gather" if nx else "flash_fwd", grid=(HEADS // g, n),
        out_shape=[jax.ShapeDtypeStruct((s, HEADS * LANE), BF16), jax.ShapeDtypeStruct((HEADS, 1, s), F32)]
        + [jax.ShapeDtypeStruct((NDEV,) + a.shape, a.dtype) for a in (prefetch or [])],
        in_specs=[pl.BlockSpec((g, t, LANE), lambda h, i: (h, i, 0)), whole, whole] + [ANY] * nx,
        out_specs=[pl.BlockSpec((t, g * LANE), lambda h, i: (i, h)), pl.BlockSpec((g, 1, t), lambda h, i: (h, 0, i))] + [ANY] * nx,
        scratch_shapes=[pltpu.VMEM((g, LANE, s), BF16)] + (_comm_scratch(nx) if nx else []),
        compiler_params=_cp("arbitrary", "arbitrary"),
    )(q, k, kv, *(prefetch or []))
    return res[0], res[1], list(res[2:])


def _flash_bwd(q, k, kv, cat, dcat, lse, pending=None):
    s = q.shape[1]
    t = min(ATT_TILE, s)
    n = s // t
    g = BWD_HEADS
    nx = len(pending) if pending else 0

    def body(*refs):
        ins, outs, scr = refs[:6], refs[6 + nx:9 + nx], refs[9 + 2 * nx:14 + 2 * nx]
        step = pl.program_id(0) * n + pl.program_id(1)
        if nx:
            start, finish = _exchange_plan(refs[6:6 + nx], refs[9 + nx:9 + 2 * nx], *refs[14 + 2 * nx:])
            pl.when(step == 0)(start)
        attend(*ins, *outs, *scr)
        if nx:
            pl.when(step == (HEADS // g) * n - 1)(finish)

    def attend(q_ref, k_ref, kv_ref, o_ref, do_ref, lse_ref, dq_ref, dk_ref, dv_ref, qt_sc, dot_sc, delta_sc, dqt_sc, qs_sc):
        j = pl.program_id(1)

        @pl.when(j == 0)
        def _():
            for hh in range(g):
                lanes = slice(hh * LANE, (hh + 1) * LANE)
                qf = q_ref[hh].astype(F32)
                qt_sc[hh] = (qf * (ATT_SCALE * LOG2E)).T.astype(BF16)
                qs_sc[hh] = (qf * ATT_SCALE).astype(BF16)
                dof = do_ref[:, lanes].astype(F32)
                dot_sc[hh] = dof.T.astype(BF16)
                delta_sc[hh] = _dot(jnp.ones((8, LANE), F32), dof * o_ref[:, lanes].astype(F32), NT, precision=HI)
            dqt_sc[...] = jnp.zeros_like(dqt_sc)

        kjt = [_transpose_bf16(k_ref[hh]) for hh in range(g)]
        kpos = lax.broadcasted_iota(jnp.int32, (t, t), 0)
        qpos = lax.broadcasted_iota(jnp.int32, (t, t), 1)

        def chunk(i, carry, diagonal):
            start = pl.multiple_of(i * t, t)
            cols = pl.ds(start, t)
            scs = [_dot(k_ref[hh], qt_sc[hh, :, cols]) for hh in range(g)]
            dps = [_dot(kv_ref[hh], dot_sc[hh, :, cols]) for hh in range(g)]
            pds = []
            for hh in range(g):
                p = jnp.exp2(scs[hh] - lse_ref[hh, :, cols])
                if diagonal:
                    p = jnp.where(qpos >= kpos, p, 0.0)
                ds = (p * (dps[hh] - delta_sc[hh, 0:1, cols])).astype(BF16)
                pds.append((p.astype(BF16), ds))
            out = []
            for hh in range(g):
                dk, dv = carry[hh]
                dv = dv + _dot(pds[hh][0], do_ref[pl.ds(start, t), hh * LANE:(hh + 1) * LANE])
                dk = dk + _dot(pds[hh][1], qs_sc[hh, pl.ds(start, t), :])
                dqt_sc[hh, :, cols] += _dot(kjt[hh], pds[hh][1])
                out.append((dk, dv))
            return tuple(out)

        zero = jnp.zeros((t, LANE), F32)
        carry = chunk(j, tuple((zero, zero) for _ in range(g)), True)
        carry = lax.fori_loop(j + 1, n, lambda i, c: chunk(i, c, False), carry)
        for hh in range(g):
            dk_ref[hh] = carry[hh][0].astype(BF16)
            dv_ref[hh] = carry[hh][1].astype(BF16)

        @pl.when(j == n - 1)
        def _():
            for hh in range(g):
                dq_ref[hh] = (dqt_sc[hh] * ATT_SCALE).T

    whole = pl.BlockSpec((g, s, LANE), lambda h, j: (h, 0, 0))
    kspec = pl.BlockSpec((g, t, LANE), lambda h, j: (h, j, 0))
    ospec = pl.BlockSpec((s, g * LANE), lambda h, j: (0, h))
    res = pl.pallas_call(
        body, name="flash_bwd_exchange" if nx else "flash_bwd", grid=(HEADS // g, n),
        out_shape=[jax.ShapeDtypeStruct((HEADS, s, LANE), F32), jax.ShapeDtypeStruct((HEADS, s, LANE), BF16),
                   jax.ShapeDtypeStruct((HEADS, s, LANE), BF16)] + [jax.ShapeDtypeStruct(a.shape, a.dtype) for a in (pending or [])],
        in_specs=[whole, kspec, kspec, ospec, ospec, pl.BlockSpec((g, 1, s), lambda h, j: (h, 0, 0))] + [ANY] * nx,
        out_specs=[whole, kspec, kspec] + [ANY] * nx,
        scratch_shapes=[pltpu.VMEM((g, LANE, s), BF16), pltpu.VMEM((g, LANE, s), BF16), pltpu.VMEM((g, 8, s), F32),
                        pltpu.VMEM((g, LANE, s), F32), pltpu.VMEM((g, s, LANE), BF16)] + (_comm_scratch(nx) if nx else []),
        compiler_params=_cp("arbitrary", "arbitrary"),
    )(q, k, kv, cat, dcat, lse, *(pending or []))
    return res[0], res[1], res[2], list(res[3:])


def _conv3(ext, w_ref, ts):
    return (w_ref[0:1, :] * ext[pl.ds(HALO - 2, ts), :] + w_ref[1:2, :] * ext[pl.ds(HALO - 1, ts), :]
            + w_ref[2:3, :] * ext[pl.ds(HALO, ts), :])


def _conv3_rows(ext, w_ref, r):
    return (w_ref[0:1, :] * ext[pl.ds(HALO - 2 + r, ROW_CHUNK), :] + w_ref[1:2, :] * ext[pl.ds(HALO - 1 + r, ROW_CHUNK), :]
            + w_ref[2:3, :] * ext[pl.ds(HALO + r, ROW_CHUNK), :])


def _conv3_t(ext2, w_ref, ts):
    return (w_ref[0:1, :] * ext2[pl.ds(2, ts), :] + w_ref[1:2, :] * ext2[pl.ds(1, ts), :] + w_ref[2:3, :] * ext2[pl.ds(0, ts), :])


def _sconv_fwd(proj, w):
    s = proj.shape[0]
    ts = min(ROW_TILE, s)

    def body(b_ref, c_ref, h_ref, hc_ref, hh_ref, w_ref, o_ref, ext):
        i = pl.program_id(0)
        ext[0:HALO, :] = hc_ref[...].astype(F32) * hh_ref[...].astype(F32) * (i > 0).astype(F32)
        ext[HALO:HALO + ts, :] = c_ref[...].astype(F32) * h_ref[...].astype(F32)
        o_ref[...] = (b_ref[...].astype(F32) * _conv3(ext, w_ref, ts)).astype(BF16)

    def col(cb):
        return pl.BlockSpec((ts, SC), lambda i: (i, cb))

    def halo(cb):
        return pl.BlockSpec((HALO, SC), lambda i: (_prev_halo(i, ts), cb))

    return pl.pallas_call(
        body, name="sconv_fwd", grid=(s // ts,), out_shape=jax.ShapeDtypeStruct((s, SC), BF16),
        in_specs=[col(2), col(3), col(4), halo(3), halo(4), pl.BlockSpec((3, SC), lambda i: (0, 0))],
        out_specs=pl.BlockSpec((ts, SC), lambda i: (i, 0)), scratch_shapes=[pltpu.VMEM((ts + HALO, SC), F32)],
        compiler_params=_cp("parallel"),
    )(proj, proj, proj, proj, proj, w)


def _sconv_bwd(proj, dcat, w):
    s = proj.shape[0]
    ts = min(ROW_TILE, s)
    n = s // ts

    def body(b_ref, c_ref, h_ref, hc_ref, hh_ref, dy_ref, ndy_ref, nb_ref, w_ref, db_ref, dc_ref, dh_ref, dw_ref, ext, ext2):
        i = pl.program_id(0)

        @pl.when(i == 0)
        def _():
            dw_ref[...] = jnp.zeros_like(dw_ref)

        cv, hv, bv = c_ref[...].astype(F32), h_ref[...].astype(F32), b_ref[...].astype(F32)
        ext[0:HALO, :] = hc_ref[...].astype(F32) * hh_ref[...].astype(F32) * (i > 0).astype(F32)
        ext[HALO:HALO + ts, :] = cv * hv
        dy = dy_ref[...].astype(F32)
        db_ref[...] = (dy * _conv3(ext, w_ref, ts)).astype(BF16)
        dyb = dy * bv
        ext2[0:ts, :] = dyb
        ext2[ts:ts + HALO, :] = ndy_ref[...].astype(F32) * nb_ref[...].astype(F32) * (i < n - 1).astype(F32)
        dg = _conv3_t(ext2, w_ref, ts)
        dc_ref[...] = (dg * hv).astype(BF16)
        dh_ref[...] = (dg * cv).astype(BF16)
        for kk in range(3):
            dw_ref[kk:kk + 1, :] += _rowsum(dyb * ext[pl.ds(HALO - 2 + kk, ts), :])

    def col(cb):
        return pl.BlockSpec((ts, SC), lambda i: (i, cb))

    def halo(cb):
        return pl.BlockSpec((HALO, SC), lambda i: (_prev_halo(i, ts), cb))

    def nxt(cb):
        return pl.BlockSpec((HALO, SC), lambda i: (_next_halo(i, ts, n), cb))

    out = jax.ShapeDtypeStruct((s, SC), BF16)
    o0 = pl.BlockSpec((ts, SC), lambda i: (i, 0))
    return pl.pallas_call(
        body, name="sconv_bwd", grid=(n,), out_shape=(out, out, out, jax.ShapeDtypeStruct((3, SC), F32)),
        in_specs=[col(2), col(3), col(4), halo(3), halo(4), col(4), nxt(4), nxt(2), pl.BlockSpec((3, SC), lambda i: (0, 0))],
        out_specs=(o0, o0, o0, pl.BlockSpec((3, SC), lambda i: (0, 0))),
        scratch_shapes=[pltpu.VMEM((ts + HALO, SC), F32), pltpu.VMEM((ts + HALO, SC), F32)], compiler_params=_cp("arbitrary"),
    )(proj, proj, proj, proj, proj, dcat, dcat, proj, w)


def _ffn_stage(ext, u_ref, halo_ref, i, ts):
    ext[0:HALO, :] = halo_ref[...].astype(F32) * (i > 0).astype(F32)
    ext[HALO:HALO + ts, :] = u_ref[...].astype(F32)


def _ffn_specs(ts):
    cur = pl.BlockSpec((2, None, ts, FB), lambda j, i: (0, j, i, 0))
    halo = pl.BlockSpec((2, None, HALO, FB), lambda j, i: (0, j, _prev_halo(i, ts), 0))
    w = pl.BlockSpec((2, None, 3, FB), lambda j, i: (0, j, 0, 0))
    b = pl.BlockSpec((2, None, 1, FB), lambda j, i: (0, j, 0, 0))
    return cur, halo, w, b


def _ffn_act(upre, fcw, fcb):
    s = upre.shape[1]
    ts = min(ROW_TILE, s)

    def body(u_ref, halo_ref, w_ref, b_ref, o_ref, ext_g, ext_u):
        i = pl.program_id(1)
        _ffn_stage(ext_g, u_ref.at[0], halo_ref.at[0], i, ts)
        _ffn_stage(ext_u, u_ref.at[1], halo_ref.at[1], i, ts)
        for r in range(0, ts, ROW_CHUNK):
            gate = b_ref[0] + _conv3_rows(ext_g, w_ref.at[0], r)
            up = b_ref[1] + _conv3_rows(ext_u, w_ref.at[1], r)
            o_ref[pl.ds(r, ROW_CHUNK), :] = (gate * _sig(gate) * up).astype(BF16)

    cur, halo, w, b = _ffn_specs(ts)
    u4 = upre.reshape(2, 4, s, FB)
    return pl.pallas_call(
        body, name="ffn_act", grid=(4, s // ts), out_shape=jax.ShapeDtypeStruct((4, s, FB), BF16),
        in_specs=[cur, halo, w, b], out_specs=pl.BlockSpec((None, ts, FB), lambda j, i: (j, i, 0)),
        scratch_shapes=[pltpu.VMEM((ts + HALO, FB), F32), pltpu.VMEM((ts + HALO, FB), F32)], compiler_params=_cp("parallel", "parallel"),
    )(u4, u4, fcw.reshape(2, 4, 3, FB), fcb.reshape(2, 4, 1, FB))


def _ffn_bwd_a(upre, dact, fcw, fcb):
    s = upre.shape[1]
    ts = min(ROW_TILE, s)

    def body(u_ref, halo_ref, w_ref, b_ref, da_ref, du_ref, db_ref, ext_g, ext_u, acc):
        i = pl.program_id(1)

        @pl.when(i == 0)
        def _():
            db_ref[...] = jnp.zeros_like(db_ref)

        _ffn_stage(ext_g, u_ref.at[0], halo_ref.at[0], i, ts)
        _ffn_stage(ext_u, u_ref.at[1], halo_ref.at[1], i, ts)
        acc[...] = jnp.zeros_like(acc)
        for r in range(0, ts, ROW_CHUNK):
            rows = pl.ds(r, ROW_CHUNK)
            gate = b_ref[0] + _conv3_rows(ext_g, w_ref.at[0], r)
            up = b_ref[1] + _conv3_rows(ext_u, w_ref.at[1], r)
            sg = _sig(gate)
            da = da_ref[rows, :].astype(F32)
            dgate = da * up * sg * (1.0 + gate * (1.0 - sg))
            dup = da * gate * sg
            du_ref[0, rows, :] = dgate.astype(BF16)
            du_ref[1, rows, :] = dup.astype(BF16)
            acc[0] += dgate
            acc[1] += dup
        db_ref[0] += _rowsum(acc[0])
        db_ref[1] += _rowsum(acc[1])

    cur, halo, w, b = _ffn_specs(ts)
    u4 = upre.reshape(2, 4, s, FB)
    du, db = pl.pallas_call(
        body, name="ffn_bwd_a", grid=(4, s // ts),
        out_shape=(jax.ShapeDtypeStruct((2, 4, s, FB), BF16), jax.ShapeDtypeStruct((2, 4, 1, FB), F32)),
        in_specs=[cur, halo, w, b, pl.BlockSpec((None, ts, FB), lambda j, i: (j, i, 0))], out_specs=(cur, b),
        scratch_shapes=[pltpu.VMEM((ts + HALO, FB), F32), pltpu.VMEM((ts + HALO, FB), F32), pltpu.VMEM((2, ROW_CHUNK, FB), F32)],
        compiler_params=_cp("parallel", "arbitrary"),
    )(u4, u4, fcw.reshape(2, 4, 3, FB), fcb.reshape(2, 4, 1, FB), dact)
    return du.reshape(NDEV, s, FB), db.reshape(NDEV, 1, FB)


def _ffn_bwd_b(du, upre, fcw):
    s = upre.shape[1]
    ts = min(ROW_TILE, s)
    n = s // ts

    def body(du_ref, ndu_ref, u_ref, halo_ref, w_ref, dup_ref, dw_ref, ext, ext2, acc):
        i = pl.program_id(1)

        @pl.when(i == 0)
        def _():
            dw_ref[...] = jnp.zeros_like(dw_ref)

        ext2[0:ts, :] = du_ref[...].astype(F32)
        ext2[ts:ts + HALO, :] = ndu_ref[...].astype(F32) * (i < n - 1).astype(F32)
        _ffn_stage(ext, u_ref, halo_ref, i, ts)
        acc[...] = jnp.zeros_like(acc)
        for r in range(0, ts, ROW_CHUNK):
            d0 = ext2[pl.ds(r, ROW_CHUNK), :]
            dup = w_ref[2:3, :] * d0 + w_ref[1:2, :] * ext2[pl.ds(r + 1, ROW_CHUNK), :] + w_ref[0:1, :] * ext2[pl.ds(r + 2, ROW_CHUNK), :]
            dup_ref[pl.ds(r, ROW_CHUNK), :] = dup.astype(BF16)
            for kk in range(3):
                acc[kk] += d0 * ext[pl.ds(HALO - 2 + kk + r, ROW_CHUNK), :]
        for kk in range(3):
            dw_ref[kk:kk + 1, :] += _rowsum(acc[kk])

    cur = pl.BlockSpec((None, ts, FB), lambda j, i: (j, i, 0))
    w = pl.BlockSpec((None, 3, FB), lambda j, i: (j, 0, 0))
    return pl.pallas_call(
        body, name="ffn_bwd_b", grid=(NDEV, n),
        out_shape=(jax.ShapeDtypeStruct((NDEV, s, FB), BF16), jax.ShapeDtypeStruct((NDEV, 3, FB), F32)),
        in_specs=[cur, pl.BlockSpec((None, HALO, FB), lambda j, i: (j, _next_halo(i, ts, n), 0)), cur,
                  pl.BlockSpec((None, HALO, FB), lambda j, i: (j, _prev_halo(i, ts), 0)), w],
        out_specs=(cur, w),
        scratch_shapes=[pltpu.VMEM((ts + HALO, FB), F32), pltpu.VMEM((ts + HALO, FB), F32), pltpu.VMEM((3, ROW_CHUNK, FB), F32)],
        compiler_params=_cp("parallel", "arbitrary"),
    )(du, du, upre, upre, fcw)


def _softplus(v):
    e = jnp.exp(-jnp.abs(v))
    return jnp.maximum(v, 0.0) + jnp.where(e < 1e-4, e * (1.0 - 0.5 * e), jnp.log(1.0 + e))


def _ssd_consts():
    L = SSD_L
    r = lax.broadcasted_iota(jnp.int32, (L, L), 0)
    c = lax.broadcasted_iota(jnp.int32, (L, L), 1)
    tri = r >= c
    er = lax.broadcasted_iota(jnp.int32, (LANE, SSD_DIM), 0)
    ec = lax.broadcasted_iota(jnp.int32, (LANE, SSD_DIM), 1)
    expand = ((ec >= er * 64) & (ec < er * 64 + 64)).astype(F32)
    return tri, expand


def _ssd_conv4(ext, cw_ref, cb_ref):
    L = SSD_L
    pre = cb_ref[...] + cw_ref[0:1, :] * ext[pl.ds(HALO - 3, L), :]
    for kk in range(1, 4):
        pre = pre + cw_ref[kk:kk + 1, :] * ext[pl.ds(HALO - 3 + kk, L), :]
    return pre


def _ssd_common(xbc_ref, halo_ref, dt_ref, cw_ref, cb_ref, dtb_ref, alog_ref, ext, first):
    L = SSD_L
    tri, expand = _ssd_consts()
    ext[0:HALO, :] = halo_ref[...].astype(F32) * (1.0 - first.astype(F32))
    ext[HALO:HALO + L, :] = xbc_ref[...].astype(F32)
    pre = _ssd_conv4(ext, cw_ref, cb_ref)
    sg = _sig(pre)
    act = pre * sg
    lane = lax.broadcasted_iota(jnp.int32, (1, LANE), 1)
    m4 = lane < SSD_H
    raw = dt_ref[...].astype(F32) + dtb_ref[...]
    dtv = jnp.where(m4, _softplus(raw), 0.0)
    av = jnp.where(m4, -jnp.exp(alog_ref[...]), 0.0)
    adt = dtv * av
    acs = _dot(tri.astype(F32), adt, precision=HI)
    acs_b = _dot(acs, expand, precision=HI)
    dt_b = _dot(dtv, expand, precision=HI)
    return dict(tri=tri, expand=expand, pre=pre, sg=sg, act=act, raw=raw, dtv=dtv, av=av, m4=m4, acs=acs, acs_b=acs_b,
                dt_b=dt_b, lane=lane)


def _head_terms(cm, h):
    L = SSD_L
    acs, tri = cm["acs"], cm["tri"]
    lane_l = lax.broadcasted_iota(jnp.int32, (L, LANE), 1)
    sub_l = lax.broadcasted_iota(jnp.int32, (LANE, L), 0)
    col = jnp.sum(jnp.where(lane_l == h, acs, 0.0), axis=1, keepdims=True)
    row = jnp.sum(jnp.where(sub_l == h, acs.T, 0.0), axis=0, keepdims=True)
    dec = jnp.where(tri, jnp.exp(jnp.where(tri, col - row, NEG)), 0.0)
    rowi = lax.broadcasted_iota(jnp.int32, (L, 1), 0)
    last = jnp.sum(jnp.where(rowi == L - 1, col, 0.0), axis=0, keepdims=True)
    dte = jnp.exp(last - col)
    return col, dec, last, dte


def _ssd_fwd(proj, cw, cb, dtb, alog, dvec, nw):
    s = proj.shape[0]
    L = SSD_L
    nc = s // L

    def body(z_ref, xbc_ref, halo_ref, dt_ref, cw_ref, cb_ref, dtb_ref, alog_ref, d_ref, nw_ref, y_ref, ypre_ref, st_ref, ext, state):
        i = pl.program_id(0)

        @pl.when(i == 0)
        def _():
            state[...] = jnp.zeros_like(state)

        cm = _ssd_common(xbc_ref, halo_ref, dt_ref, cw_ref, cb_ref, dtb_ref, alog_ref, ext, i == 0)
        act = cm["act"]
        xs = act[:, 0:256]
        bm = (act[:, 256:384], act[:, 384:512])
        cmat = (act[:, 512:640].astype(BF16), act[:, 640:768].astype(BF16))
        xdt = xs * cm["dt_b"]
        prev = state[...]
        st_ref[...] = prev
        prev_bf = prev.astype(BF16)
        gm = [_dot(cmat[g], bm[g].astype(BF16), NT) for g in range(2)]
        lane2 = lax.broadcasted_iota(jnp.int32, (1, SSD_DIM), 1)
        rows2 = lax.broadcasted_iota(jnp.int32, (SSD_DIM, 1), 0)
        ydiag = jnp.zeros((L, SSD_DIM), F32)
        contrib = jnp.zeros((SSD_DIM, LANE), F32)
        cd_rows = jnp.zeros((SSD_DIM, 1), F32)
        for h in range(SSD_H):
            g = h // 2
            col, dec, last, dte = _head_terms(cm, h)
            mh = (lane2 >= 64 * h) & (lane2 < 64 * h + 64)
            xm = jnp.where(mh, xdt, 0.0).astype(BF16)
            ydiag += _dot((gm[g] * dec).astype(BF16), xm)
            contrib += _dot(xm, (bm[g] * dte).astype(BF16), TN)
            cd_rows += jnp.where((rows2 >= 64 * h) & (rows2 < 64 * h + 64), jnp.exp(last), 0.0)
        yo = jnp.where(lane2 < 128, _dot(cmat[0], prev_bf, NT), _dot(cmat[1], prev_bf, NT))
        y = ydiag + yo * jnp.exp(cm["acs_b"]) + xs * d_ref[...]
        state[...] = prev * cd_rows + contrib
        ypre_ref[...] = y
        zz = z_ref[...].astype(F32)
        gt = y * zz * _sig(zz)
        y_ref[...] = (gt * lax.rsqrt(jnp.mean(gt * gt, axis=-1, keepdims=True) + EPS) * nw_ref[...]).astype(BF16)

    def vec(w):
        return pl.BlockSpec((1, w), lambda i: (0, 0))

    return pl.pallas_call(
        body, name="ssd_fwd", grid=(nc,),
        out_shape=(jax.ShapeDtypeStruct((s, SSD_DIM), BF16), jax.ShapeDtypeStruct((s, SSD_DIM), F32),
                   jax.ShapeDtypeStruct((nc, SSD_DIM, LANE), F32)),
        in_specs=[pl.BlockSpec((L, SSD_DIM), lambda i: (i, 5)), pl.BlockSpec((L, SSD_CONV), lambda i: (i, 2)),
                  pl.BlockSpec((HALO, SSD_CONV), lambda i: (_prev_halo(i, L), 2)), pl.BlockSpec((L, LANE), lambda i: (i, 18)),
                  pl.BlockSpec((4, SSD_CONV), lambda i: (0, 0)), vec(SSD_CONV), vec(LANE), vec(LANE), vec(SSD_DIM), vec(SSD_DIM)],
        out_specs=(pl.BlockSpec((L, SSD_DIM), lambda i: (i, 0)), pl.BlockSpec((L, SSD_DIM), lambda i: (i, 0)),
                   pl.BlockSpec((None, SSD_DIM, LANE), lambda i: (i, 0, 0))),
        scratch_shapes=[pltpu.VMEM((L + HALO, SSD_CONV), F32), pltpu.VMEM((SSD_DIM, LANE), F32)], compiler_params=_cp("arbitrary"),
    )(proj, proj, proj, proj, cw, cb, dtb, alog, dvec, nw)


def _ssd_bwd(proj, dcat, ypre, states, cw, cb, dtb, alog, dvec, nw):
    s = proj.shape[0]
    L = SSD_L
    nc = s // L

    def body(z_ref, xbc_ref, halo_ref, dt_ref, dy_ref, ypre_ref, st_ref, cw_ref, cb_ref, dtb_ref, alog_ref, d_ref, nw_ref,
             dz_ref, dxbc_ref, ddt_ref, dcw_ref, dcb_ref, ddtb_ref, dalog_ref, dd_ref, dnw_ref, ext, ext2, carry, dstate, ddl):
        i = pl.program_id(0)
        r = nc - 1 - i

        @pl.when(i == 0)
        def _():
            for ref in (dcw_ref, dcb_ref, ddtb_ref, dalog_ref, dd_ref, dnw_ref, carry, dstate, ddl):
                ref[...] = jnp.zeros_like(ref)

        cm = _ssd_common(xbc_ref, halo_ref, dt_ref, cw_ref, cb_ref, dtb_ref, alog_ref, ext, r == 0)
        tri, expand, act = cm["tri"], cm["expand"], cm["act"]
        xs = act[:, 0:256]
        bm = (act[:, 256:384], act[:, 384:512])
        cmat = (act[:, 512:640], act[:, 640:768])
        bm_bf = [v.astype(BF16) for v in bm]
        cm_bf = [v.astype(BF16) for v in cmat]
        dt_b = cm["dt_b"]
        xdt = xs * dt_b
        xdt_bf = xdt.astype(BF16)
        ea_b = jnp.exp(cm["acs_b"])
        prev = st_ref[...]
        prev_bf = prev.astype(BF16)
        lane2 = lax.broadcasted_iota(jnp.int32, (1, SSD_DIM), 1)
        rows2 = lax.broadcasted_iota(jnp.int32, (SSD_DIM, 1), 0)
        lane_l = lax.broadcasted_iota(jnp.int32, (L, LANE), 1)
        rowi = lax.broadcasted_iota(jnp.int32, (L, 1), 0)

        y = ypre_ref[...]
        zz = z_ref[...].astype(F32)
        sz = _sig(zz)
        gt = y * zz * sz
        dgt, dwt = _rms_bwd_math(gt, nw_ref[...], dy_ref[...].astype(F32))
        dnw_ref[...] += _rowsum(dwt)
        dy = dgt * zz * sz
        dz_ref[...] = (dgt * y * sz * (1.0 + zz * (1.0 - sz))).astype(BF16)

        ddl[0:1, :] += _rowsum(dy * xs)
        dxs = dy * d_ref[...]

        yo = jnp.where(lane2 < 128, _dot(cm_bf[0], prev_bf, NT), _dot(cm_bf[1], prev_bf, NT))
        dacs_b = dy * yo * ea_b
        dyo = dy * ea_b
        dyo_g = (jnp.where(lane2 < 128, dyo, 0.0).astype(BF16), jnp.where(lane2 >= 128, dyo, 0.0).astype(BF16))
        dc = [_dot(dyo_g[g], prev_bf) for g in range(2)]
        dprev = _dot(dyo_g[0], cm_bf[0], TN) + _dot(dyo_g[1], cm_bf[1], TN)

        gm = [_dot(cm_bf[g], bm_bf[g], NT) for g in range(2)]
        dgm = [jnp.zeros((L, L), F32), jnp.zeros((L, L), F32)]
        db = [jnp.zeros((L, LANE), F32), jnp.zeros((L, LANE), F32)]
        dxdt = jnp.zeros((L, SSD_DIM), F32)
        dacs = jnp.zeros((L, LANE), F32)
        dlast = jnp.zeros((1, LANE), F32)
        cd_rows = jnp.zeros((SSD_DIM, 1), F32)
        dst = dstate[...]
        dst_bf = dst.astype(BF16)
        dsp = dst * prev
        ones = jnp.ones((L, LANE), F32)
        for h in range(SSD_H):
            g = h // 2
            col, dec, last, dte = _head_terms(cm, h)
            mh = (lane2 >= 64 * h) & (lane2 < 64 * h + 64)
            rh = (rows2 >= 64 * h) & (rows2 < 64 * h + 64)
            sc = gm[g] * dec
            xm = jnp.where(mh, xdt, 0.0).astype(BF16)
            dym = jnp.where(mh, dy, 0.0).astype(BF16)
            dsc = _dot(dym, xdt_bf, NT)
            dxdt += _dot(sc.astype(BF16), dym, TN)
            dgm[g] += dsc * dec
            dd = dsc * sc
            rs = jnp.sum(dd, axis=1, keepdims=True)
            cs = _dot(dd, ones, TN, precision=HI)
            dacs += jnp.where(lane_l == h, rs - cs, 0.0)
            bd = (bm[g] * dte).astype(BF16)
            dxdt += jnp.where(mh, _dot(bd, dst_bf, NT), 0.0)
            dbd = _dot(xm, dst_bf)
            db[g] += dbd * dte
            tt = jnp.sum(dbd * bm[g], axis=1, keepdims=True) * dte
            dacs += jnp.where(lane_l == h, -tt, 0.0)
            cdh = jnp.exp(last)
            dcd = jnp.sum(jnp.sum(jnp.where(rh, dsp, 0.0), axis=1, keepdims=True), axis=0, keepdims=True)
            dlast += jnp.where(cm["lane"] == h, jnp.sum(tt, axis=0, keepdims=True) + dcd * cdh, 0.0)
            cd_rows += jnp.where(rh, cdh, 0.0)
        dacs += jnp.where(rowi == L - 1, dlast, 0.0)
        dacs += _dot(dacs_b, expand, NT, precision=HI)
        dstate[...] = dprev + dst * cd_rows

        for g in range(2):
            dgb = dgm[g].astype(BF16)
            dc[g] += _dot(dgb, bm_bf[g])
            db[g] += _dot(dgb, cm_bf[g], TN)

        dadt = _dot(tri.astype(F32), dacs, TN, precision=HI)
        ddtv = dadt * cm["av"] + _dot(dxdt * xs, expand, NT, precision=HI)
        dalog_ref[...] += _rowsum(dadt * cm["dtv"]) * cm["av"]
        dxs += dxdt * dt_b
        draw = jnp.where(cm["m4"], ddtv * _sig(cm["raw"]), 0.0)
        ddtb_ref[...] += _rowsum(draw)
        ddt_ref[...] = draw.astype(BF16)

        dact = jnp.concatenate([dxs, db[0], db[1], dc[0], dc[1]], axis=1)
        sg, pre = cm["sg"], cm["pre"]
        dpre = dact * sg * (1.0 + pre * (1.0 - sg))
        dcb_ref[...] += _rowsum(dpre)
        for kk in range(4):
            dcw_ref[kk:kk + 1, :] += _rowsum(dpre * ext[pl.ds(HALO - 3 + kk, L), :])
        ext2[0:L, :] = dpre
        ext2[L:L + HALO, :] = carry[...]
        dx = cw_ref[3:4, :] * ext2[pl.ds(0, L), :]
        for kk in range(3):
            dx = dx + cw_ref[kk:kk + 1, :] * ext2[pl.ds(3 - kk, L), :]
        dxbc_ref[...] = dx.astype(BF16)
        carry[...] = dpre[0:HALO, :]

        @pl.when(i == nc - 1)
        def _():
            dd_ref[...] = _dot(ddl[...], expand, NT, precision=HI)

    def vec(w):
        return pl.BlockSpec((1, w), lambda i: (0, 0))

    def rv(i):
        return nc - 1 - i

    return pl.pallas_call(
        body, name="ssd_bwd", grid=(nc,),
        out_shape=(jax.ShapeDtypeStruct((s, SSD_DIM), BF16), jax.ShapeDtypeStruct((s, SSD_CONV), BF16), jax.ShapeDtypeStruct((s, LANE), BF16),
                   jax.ShapeDtypeStruct((4, SSD_CONV), F32), jax.ShapeDtypeStruct((1, SSD_CONV), F32), jax.ShapeDtypeStruct((1, LANE), F32),
                   jax.ShapeDtypeStruct((1, LANE), F32), jax.ShapeDtypeStruct((8, LANE), F32), jax.ShapeDtypeStruct((1, SSD_DIM), F32)),
        in_specs=[pl.BlockSpec((L, SSD_DIM), lambda i: (rv(i), 5)), pl.BlockSpec((L, SSD_CONV), lambda i: (rv(i), 2)),
                  pl.BlockSpec((HALO, SSD_CONV), lambda i: (_prev_halo(rv(i), L), 2)), pl.BlockSpec((L, LANE), lambda i: (rv(i), 18)),
                  pl.BlockSpec((L, SSD_DIM), lambda i: (rv(i), 5)), pl.BlockSpec((L, SSD_DIM), lambda i: (rv(i), 0)),
                  pl.BlockSpec((None, SSD_DIM, LANE), lambda i: (rv(i), 0, 0)),
                  pl.BlockSpec((4, SSD_CONV), lambda i: (0, 0)), vec(SSD_CONV), vec(LANE), vec(LANE), vec(SSD_DIM), vec(SSD_DIM)],
        out_specs=(pl.BlockSpec((L, SSD_DIM), lambda i: (rv(i), 0)), pl.BlockSpec((L, SSD_CONV), lambda i: (rv(i), 0)),
                   pl.BlockSpec((L, LANE), lambda i: (rv(i), 0)), pl.BlockSpec((4, SSD_CONV), lambda i: (0, 0)), vec(SSD_CONV),
                   vec(LANE), vec(LANE), pl.BlockSpec((8, LANE), lambda i: (0, 0)), vec(SSD_DIM)),
        scratch_shapes=[pltpu.VMEM((L + HALO, SSD_CONV), F32), pltpu.VMEM((L + HALO, SSD_CONV), F32), pltpu.VMEM((HALO, SSD_CONV), F32),
                        pltpu.VMEM((SSD_DIM, LANE), F32), pltpu.VMEM((8, SSD_DIM), F32)],
        compiler_params=_cp("arbitrary"),
    )(proj, proj, proj, proj, dcat, ypre, states, cw, cb, dtb, alog, dvec, nw)


def _adamw(parts, w, m, v, name):
    r, c = w.shape
    tr = r
    for cand in (256, 128, 64, 32, 16, 8):
        if r % cand == 0 and (cand * c * 4) <= 2 * 1024 * 1024:
            tr = cand
            break
    c1 = 1.0 - B1 ** STEP
    c2 = 1.0 - B2 ** STEP

    def body(p_ref, w_ref, m_ref, v_ref, g_ref, d_ref, nm_ref, nv_ref):
        g = p_ref[0].astype(F32)
        for dev in range(1, NDEV):
            g = g + p_ref[dev].astype(F32)
        mn = B1 * m_ref[...] + (1.0 - B1) * g
        vn = B2 * v_ref[...] + (1.0 - B2) * (g * g)
        g_ref[...] = g
        nm_ref[...] = mn
        nv_ref[...] = vn
        d_ref[...] = -LR * ((mn / c1) / (jnp.sqrt(vn / c2) + AEPS) + WD * w_ref[...])

    blk = pl.BlockSpec((tr, c), lambda i: (i, 0))
    out = jax.ShapeDtypeStruct((r, c), F32)
    return pl.pallas_call(
        body, name=name, grid=(r // tr,), out_shape=(out, out, out, out),
        in_specs=[pl.BlockSpec((NDEV, tr, c), lambda i: (0, i, 0)), blk, blk, blk], out_specs=(blk, blk, blk, blk),
        compiler_params=_cp("parallel"),
    )(parts, w, m, v)


def _pad_win(w):
    z = lambda n: jnp.zeros((w.shape[0], n), w.dtype)
    return jnp.concatenate([w[:, :384], z(64), w[:, 384:416], z(32), w[:, 416:], z(124)], axis=1)


def _unpad_win(g):
    return jnp.concatenate([g[:, :384], g[:, 448:480], g[:, 512:2308]], axis=1)


def _pad_wout(w):
    att = jnp.pad(w[:512].reshape(HEADS, 64, D), ((0, 0), (64, 0), (0, 0))).reshape(HEADS * LANE, D)
    return jnp.concatenate([att, w[512:]], axis=0)


def _unpad_wout(g):
    att = g[:HEADS * LANE].reshape(HEADS, LANE, D)[:, 64:, :].reshape(512, D)
    return jnp.concatenate([att, g[HEADS * LANE:]], axis=0)


def _lanes(v, n=LANE):
    return jnp.pad(v, (0, n - v.shape[0])).reshape(1, n)


def _prep_ffn(big):
    return {"wout": _pad_wout(big["w_out"].reshape(1024, D)), "wup": big["ffn_w_up"], "fcw": big["ffn_conv_w"].astype(F32),
            "wdown": big["ffn_w_down"].reshape(4, FB, D)}


def _prep_layer(big, small, l):
    p = _prep_ffn(big) if "w_out" in big else {}
    p["win"] = _pad_win(big["w_in"].reshape(D, 2212))
    p["wq"] = jnp.pad(big["mla_w_q_up"], ((0, 0), (0, 0), (0, LANE - 96)))
    p["wkv"] = big["mla_w_kv_up"]
    p["scw"] = big["sc_conv_w"].astype(F32).transpose(1, 0, 2).reshape(3, SC)
    p["ssdcw"] = big["ssd_conv_w"].astype(F32).transpose(1, 0, 2).reshape(4, SSD_CONV)
    for nm in ("norm_mix_pre", "norm_mix_post", "norm_ffn_pre", "norm_ffn_post", "mla_q_norm", "mla_kv_norm", "ssd_conv_b", "ssd_norm"):
        p[nm] = small[nm][l].reshape(1, -1)
    p["dtb"] = _lanes(small["ssd_dt_bias"][l])
    p["alog"] = _lanes(small["ssd_a_log"][l])
    p["dvec"] = jnp.repeat(small["ssd_d"][l], 64).reshape(1, SSD_DIM)
    p["fcb"] = small["ffn_conv_b"][l].reshape(NDEV, 1, FB)
    return p


def _rope_tables(positions):
    inv_freq = 1.0 / (ROPE_THETA ** (jnp.arange(0, ROPE, 2, dtype=F32) / ROPE))
    ang = positions.astype(F32)[:, None] * inv_freq
    cos, sin = jnp.cos(ang), jnp.sin(ang)
    s = positions.shape[0]
    z = lambda n: jnp.zeros((s, n), F32)
    tc = jnp.concatenate([jnp.ones((s, 64), F32), cos, cos, z(32)], axis=1)
    ta = jnp.concatenate([z(64), -sin, z(48)], axis=1)
    tb = jnp.concatenate([z(80), sin, z(32)], axis=1)
    return tc, ta, tb


def _layer_fwd(xv, p, tabs, prefetch=None, prep_rest=None):
    h = _rms(xv, p["norm_mix_pre"], BF16, "rms_pre")
    proj = _mm_rows("in_proj", h, p["win"], BF16, NN)
    q, k, kv = _mla_prep(proj, tabs, p["mla_q_norm"], p["mla_kv_norm"], p["wq"], p["wkv"])
    o, lse, gathered = _flash_fwd(q, k, kv, prefetch)
    if prep_rest is not None:
        p = {**p, **prep_rest(gathered)}
    yconv = _sconv_fwd(proj, p["scw"])
    yssd, ypre, states = _ssd_fwd(proj, p["ssdcw"], p["ssd_conv_b"], p["dtb"], p["alog"], p["dvec"], p["ssd_norm"])
    cat = jnp.concatenate([o, yconv, yssd], axis=1)
    mixed = _mm_rows("out_proj", cat, p["wout"], F32, NN)
    x1 = _add_rms(xv, mixed, p["norm_mix_post"], "add_rms")
    h2 = _rms(x1, p["norm_ffn_pre"], BF16, "rms_pre")
    upre = _mm_up(h2, p["wup"])
    act = _ffn_act(upre, p["fcw"], p["fcb"])
    f = _mm_down(act, p["wdown"])
    x2 = _add_rms(x1, f, p["norm_ffn_post"], "add_rms")
    saved = dict(x=xv, h=h, proj=proj, q=q, k=k, kv=kv, lse=lse, ypre=ypre, states=states, cat=cat, mixed=mixed, x1=x1, h2=h2,
                 upre=upre, act=act, f=f)
    return x2, saved, p, gathered


def _pack_grads(grads, group_ids):
    return [_group_pack(GROUPS[gi], lambda n: grads[n].reshape((NDEV,) + _rows2(n, True)), (NDEV,)) for gi in group_ids]


def _layer_bwd(dx2, sv, p, tabs, exchange=False, pending=None):
    df, g_nfpo = _rms_bwd(sv["f"], p["norm_ffn_post"], dx2, None, BF16, "rms_bwd_post")
    dact = _mm_dact(df, p["wdown"])
    g_wdown = _mm_dwdown(sv["act"], df)
    du, g_fcb = _ffn_bwd_a(sv["upre"], dact, p["fcw"], p["fcb"])
    dupre, g_fcw = _ffn_bwd_b(du, sv["upre"], p["fcw"])
    dh2 = _mm_dh2(dupre, p["wup"])
    g_wup = _mm_dwup(sv["h2"], dupre)
    dx1, g_nfp = _rms_bwd(sv["x1"], p["norm_ffn_pre"], dh2, dx2, F32, "rms_bwd_pre")
    dmixed, g_nmpo = _rms_bwd(sv["mixed"], p["norm_mix_post"], dx1, None, BF16, "rms_bwd_post")
    dcat = _mm_rows("dcat", dmixed, p["wout"], BF16, NT)
    g_wout = _mm_wgrad("dw_out", sv["cat"], dmixed, BF16)
    big = {
        "w_out": _unpad_wout(g_wout).reshape(NDEV, 128, D),
        "ffn_w_up": g_wup,
        "ffn_conv_w": g_fcw.astype(BF16),
        "ffn_w_down": g_wdown.reshape(NDEV, 352, D),
    }
    outgoing = _pack_grads(big, FFN_SIDE) + (pending or []) if exchange else None
    dq, dk, dv, received = _flash_bwd(sv["q"], sv["k"], sv["kv"], sv["cat"], dcat, sv["lse"], outgoing)
    dcq, dckv, dkr, g_wq, g_wkv, g_qn, g_kvn = _mla_prep_bwd(sv["proj"], tabs, p["mla_q_norm"], p["mla_kv_norm"], p["wq"], p["wkv"], dq, dk, dv)
    dscb, dscc, dsch, g_scw = _sconv_bwd(sv["proj"], dcat, p["scw"])
    dz, dxbc, ddt, g_cw, g_cb, g_dtb, g_alog, g_d, g_nw = _ssd_bwd(
        sv["proj"], dcat, sv["ypre"], sv["states"], p["ssdcw"], p["ssd_conv_b"], p["dtb"], p["alog"], p["dvec"], p["ssd_norm"])
    dproj = jnp.concatenate([dcq, dckv, dkr, dscb, dscc, dsch, dz, dxbc, ddt], axis=1)
    dh = _mm_rows("dh", dproj, p["win"], BF16, NT)
    g_win = _mm_wgrad("dw_in", sv["h"], dproj, BF16)
    dx, g_nmp = _rms_bwd(sv["x"], p["norm_mix_pre"], dh, dx1, F32, "rms_bwd_pre")
    big.update({
        "w_in": _unpad_win(g_win).reshape(NDEV, 128, 2212),
        "mla_w_q_up": g_wq[:, :, :96].astype(BF16),
        "mla_w_kv_up": g_wkv.astype(BF16),
        "sc_conv_w": g_scw.reshape(3, NDEV, 32).transpose(1, 0, 2).astype(BF16),
        "ssd_conv_w": g_cw.reshape(4, NDEV, 96).transpose(1, 0, 2).astype(BF16),
    })
    small = {
        "norm_mix_pre": g_nmp[0], "norm_mix_post": g_nmpo[0], "norm_ffn_pre": g_nfp[0], "norm_ffn_post": g_nfpo[0],
        "mla_q_norm": g_qn[0], "mla_kv_norm": g_kvn[0], "ssd_conv_b": g_cb[0], "ssd_dt_bias": g_dtb[0, :SSD_H],
        "ssd_a_log": g_alog[0, :SSD_H], "ssd_d": g_d[0, :SSD_H], "ssd_norm": g_nw[0], "ffn_conv_b": g_fcb.reshape(-1),
    }
    return dx, big, small, received


def _local_step(xv, positions, target, layers):
    tabs = _rope_tables(positions)
    saved = []
    for p in layers:
        xv, sv, _, _ = _layer_fwd(xv, p, tabs)
        saved.append(sv)
    loss, dx = _loss_head(xv, target)
    bigs, smalls = [None] * DEPTH, [None] * DEPTH
    for l in reversed(range(len(layers))):
        dx, bigs[l], smalls[l], _ = _layer_bwd(dx, saved[l], layers[l], tabs)
    return loss[0, 0], dx, bigs, smalls


def _pack_rows(flat, lead, width, mult):
    n = flat.shape[-1]
    rows = -(-n // (width * mult)) * mult
    pad = [(0, 0)] * (flat.ndim - 1) + [(0, rows * width - n)]
    return jnp.pad(flat, pad).reshape(lead + (rows, width))


def _rows2(n, layer=False):
    shape = SHAPES[n][1:] if layer else SHAPES[n]
    return (math.prod(shape[:-1]), shape[-1])


def _group_pack(group, get, lead):
    width, names = group
    pieces = []
    for n in names:
        rows, cols = _rows2(n, True)
        pad = [(0, 0)] * len(lead) + [(0, -rows % 16), (0, width - cols)]
        pieces.append(jnp.pad(get(n), pad))
    return pieces[0] if len(pieces) == 1 else jnp.concatenate(pieces, axis=len(lead))


def _group_unpack(group, buf, padded=False):
    _, names = group
    res, off = {}, 0
    for n in names:
        rows, cols = _rows2(n, True)
        rp = rows + (-rows % 16)
        res[n] = buf[:, off:off + (rp if padded else rows), :cols]
        off += rp
    return res


def kernel(x, positions, norm_mix_pre, norm_mix_post, norm_ffn_pre, norm_ffn_post, w_in, mla_q_norm, mla_w_q_up, mla_kv_norm, mla_w_kv_up, sc_conv_w, ssd_conv_w, ssd_conv_b, ssd_dt_bias, ssd_a_log, ssd_d, ssd_norm, w_out, ffn_w_up, ffn_conv_w, ffn_conv_b, ffn_w_down, loss_target, m_norm_mix_pre, m_norm_mix_post, m_norm_ffn_pre, m_norm_ffn_post, m_w_in, m_mla_q_norm, m_mla_w_q_up, m_mla_kv_norm, m_mla_w_kv_up, m_sc_conv_w, m_ssd_conv_w, m_ssd_conv_b, m_ssd_dt_bias, m_ssd_a_log, m_ssd_d, m_ssd_norm, m_w_out, m_ffn_w_up, m_ffn_conv_w, m_ffn_conv_b, m_ffn_w_down, v_norm_mix_pre, v_norm_mix_post, v_norm_ffn_pre, v_norm_ffn_post, v_w_in, v_mla_q_norm, v_mla_w_q_up, v_mla_kv_norm, v_mla_w_kv_up, v_sc_conv_w, v_ssd_conv_w, v_ssd_conv_b, v_ssd_dt_bias, v_ssd_a_log, v_ssd_d, v_ssd_norm, v_w_out, v_ffn_w_up, v_ffn_conv_w, v_ffn_conv_b, v_ffn_w_down):
    given = dict(locals())
    w = {n: given[n] for n in WEIGHTS}
    m = {n: given["m_" + n] for n in WEIGHTS}
    v = {n: given["v_" + n] for n in WEIGHTS}

    def shards(l, group_ids):
        return [_group_pack(GROUPS[gi], lambda n: w[n][l].astype(BF16).reshape(_rows2(n, True)), ()) for gi in group_ids]

    def unpacked(bufs, group_ids):
        big = {}
        for gi, buf in zip(group_ids, bufs):
            for n, piece in _group_unpack(GROUPS[gi], buf).items():
                big[n] = piece.reshape((NDEV,) + SHAPES[n][1:])
        return big

    small_w = {n: w[n] for n, _ in SMALL}
    tabs = _rope_tables(positions[0])
    xv, layers, saved = x[0], [], []
    att = _all_gather(shards(0, ATT_SIDE), "gather_weights")
    for l in range(DEPTH):
        prefetch = shards(l, FFN_SIDE) + (shards(l + 1, ATT_SIDE) if l + 1 < DEPTH else [])
        xv, sv, p, gathered = _layer_fwd(xv, _prep_layer(unpacked(att, ATT_SIDE), small_w, l), tabs, prefetch,
                                         lambda got: _prep_ffn(unpacked(got[:len(FFN_SIDE)], FFN_SIDE)))
        att = gathered[len(FFN_SIDE):]
        layers.append(p)
        saved.append(sv)
    loss, dx = _loss_head(xv, loss_target[0])
    loss = lax.psum(loss[0, 0], ("x", "y", "c"))

    smalls, pending = [None] * DEPTH, None
    recvs = [[None] * len(GROUPS) for _ in range(DEPTH)]
    for l in reversed(range(DEPTH)):
        dx, grads, smalls[l], received = _layer_bwd(dx, saved[l], layers[l], tabs, True, pending)
        for pos, gi in enumerate(FFN_SIDE):
            recvs[l][gi] = received[pos]
        if pending is not None:
            for pos, gi in enumerate(ATT_SIDE):
                recvs[l + 1][gi] = received[len(FFN_SIDE) + pos]
        pending = _pack_grads(grads, ATT_SIDE)
    for gi, buf in zip(ATT_SIDE, _all_to_all(pending, "exchange_grads")):
        recvs[0][gi] = buf
    out = {}
    for gi, g in enumerate(GROUPS):
        per_layer = [_group_unpack(g, recvs[l][gi], padded=True) for l in range(DEPTH)]
        for n in g[1]:
            r2 = _rows2(n)
            rows, cols = _rows2(n, True)
            parts = jnp.concatenate([per_layer[l][n] for l in range(DEPTH)], axis=1)
            parts = parts.reshape(NDEV, DEPTH, -1, cols)[:, :, :rows].reshape((NDEV,) + r2)
            res = _adamw(parts, w[n].reshape(r2), m[n].reshape(r2), v[n].reshape(r2), "adamw_" + n)
            out[n] = [a.reshape(SHAPES[n]) for a in res]

    sflat = jnp.concatenate([jnp.stack([smalls[l][n] for l in range(DEPTH)]).reshape(-1) for n, _ in SMALL])
    sparts = _all_gather([_pack_rows(sflat, (), LANE, 8)], "gather_small_grads")[0]
    pk = lambda d: _pack_rows(jnp.concatenate([d[n].reshape(-1) for n, _ in SMALL]), (), LANE, 8)
    res = _adamw(sparts, pk(w), pk(m), pk(v), "adamw_small")
    off = 0
    for n, width in SMALL:
        out[n] = [a.reshape(-1)[off:off + DEPTH * width].reshape(DEPTH, width) for a in res]
        off += DEPTH * width

    return (loss, dx[None], *[out[n][0] for n in WEIGHTS], *[out[n][1] for n in WEIGHTS],
            *[out[n][2] for n in WEIGHTS], *[out[n][3] for n in WEIGHTS])
```

```python
import functools
import math

import jax
import jax.numpy as jnp
from jax import lax
from jax.experimental import pallas as pl
from jax.experimental.pallas import tpu as pltpu

F32 = jnp.float32
BF16 = jnp.bfloat16

D = 1024
DEPTH = 4
NDEV = 8
HEADS = 8
QL = 256
KVL = 128
ROPE = 32
NOPE = 64
SC = 256
SSD_DIM = 256
SSD_CONV = 768
SSD_H = 4
SSD_L = 128
FFN = 2816
FB = 704
EPS = 1e-6
ROPE_THETA = 10000.0
ATT_SCALE = 96 ** -0.5
LOG2E = 1.4426950408889634
LR, B1, B2, AEPS, WD, STEP = 0.001, 0.9, 0.999, 1e-08, 0.01, 10

PW = 2432
CATW = 1536

ROW_TILE = 512
ROW_CHUNK = 16
NORM_CHUNK = 32
MM_TILE = 1024
ATT_TILE = 256
FWD_HEADS = 4
BWD_HEADS = 2
HALO = 16
LANE = 128
NEG = -1e30
HI = lax.Precision.HIGHEST
NN = (((1,), (0,)), ((), ()))
NT = (((1,), (1,)), ((), ()))
TN = (((0,), (0,)), ((), ()))
VMEM_LIMIT = 56 * 1024 * 1024

SHARDED = (
    ("w_in", (4, 128, 2212)),
    ("mla_w_q_up", (4, 256, 96)),
    ("mla_w_kv_up", (4, 128, 128)),
    ("sc_conv_w", (4, 3, 32)),
    ("ssd_conv_w", (4, 4, 96)),
    ("w_out", (4, 128, 1024)),
    ("ffn_w_up", (4, 1024, 704)),
    ("ffn_conv_w", (4, 3, 704)),
    ("ffn_w_down", (4, 352, 1024)),
)
SHAPES = dict(SHARDED)
GROUPS = (
    (2212, ("w_in",)),
    (1024, ("w_out", "ffn_w_down")),
    (704, ("ffn_w_up", "ffn_conv_w")),
    (96, ("mla_w_q_up", "ssd_conv_w", "sc_conv_w")),
    (128, ("mla_w_kv_up",)),
)
ATT_SIDE = (0, 3, 4)
FFN_SIDE = (1, 2)
SMALL = (
    ("norm_mix_pre", 1024), ("norm_mix_post", 1024), ("norm_ffn_pre", 1024), ("norm_ffn_post", 1024),
    ("mla_q_norm", 256), ("mla_kv_norm", 128), ("ssd_conv_b", 768), ("ssd_dt_bias", 4), ("ssd_a_log", 4),
    ("ssd_d", 4), ("ssd_norm", 256), ("ffn_conv_b", 5632),
)
WEIGHTS = ("norm_mix_pre", "norm_mix_post", "norm_ffn_pre", "norm_ffn_post", "w_in", "mla_q_norm", "mla_w_q_up",
           "mla_kv_norm", "mla_w_kv_up", "sc_conv_w", "ssd_conv_w", "ssd_conv_b", "ssd_dt_bias", "ssd_a_log", "ssd_d",
           "ssd_norm", "w_out", "ffn_w_up", "ffn_conv_w", "ffn_conv_b", "ffn_w_down")


def _dot(a, b, dims=NN, precision=None):
    return lax.dot_general(a, b, dims, precision=precision, preferred_element_type=F32)


def _sig(v):
    return 1.0 / (1.0 + jnp.exp(-v))


def _cp(*sem):
    return pltpu.CompilerParams(dimension_semantics=sem, vmem_limit_bytes=VMEM_LIMIT)


def _rowsum(v):
    return jnp.sum(v, axis=0, keepdims=True)


def _prev_halo(i, ts):
    return jnp.maximum(i * (ts // HALO) - 1, 0)


def _next_halo(i, ts, n):
    return jnp.minimum((i + 1) * (ts // HALO), n * (ts // HALO) - 1)


def _gather_plan(x_refs, out_refs, send_sems, recv_sems, local_sems):
    n = len(x_refs)
    x, y, cc = lax.axis_index("x"), lax.axis_index("y"), lax.axis_index("c")
    me, sibling = (x, y, cc), (x, y, 1 - cc)
    chips = [(1 - x, y), (x, 1 - y), (1 - x, 1 - y)]

    def rows(t, px, py, pc):
        return out_refs[t].at[4 * px + 2 * py + pc]

    def copy(t, k, block, to, own=False):
        return pltpu.make_async_remote_copy(
            src_ref=x_refs[t] if own else rows(t, *block), dst_ref=rows(t, *block),
            send_sem=send_sems.at[7 * t + k], recv_sem=recv_sems.at[7 * t + k], device_id=to, device_id_type=pl.DeviceIdType.MESH)

    def local(t):
        return pltpu.make_async_copy(x_refs[t], rows(t, *me), local_sems.at[t])

    def start():
        for t in range(n):
            local(t).start()
            copy(t, 0, me, sibling, own=True).start()
            for j, chip in enumerate(chips):
                copy(t, 1 + j, me, (*chip, cc), own=True).start()

    def finish():
        for j, chip in enumerate(chips):
            for t in range(n):
                copy(t, 1 + j, (*chip, cc), me).wait_recv()
                copy(t, 4 + j, (*chip, cc), sibling).start()
        for t in range(n):
            copy(t, 0, sibling, me).wait_recv()
            for j, chip in enumerate(chips):
                copy(t, 4 + j, (*chip, 1 - cc), me).wait_recv()
        for t in range(n):
            copy(t, 0, me, sibling, own=True).wait_send()
            for j, chip in enumerate(chips):
                copy(t, 1 + j, me, (*chip, cc), own=True).wait_send()
                copy(t, 4 + j, (*chip, cc), sibling).wait_send()
            local(t).wait()

    return start, finish


def _exchange_plan(x_refs, out_refs, send_sems, recv_sems, local_sems):
    n = len(x_refs)
    x, y, cc = lax.axis_index("x"), lax.axis_index("y"), lax.axis_index("c")
    me = 4 * x + 2 * y + cc

    def copies():
        res = [pltpu.make_async_copy(x_refs[t].at[me], out_refs[t].at[me], local_sems.at[t]) for t in range(n)]
        for k in range(1, NDEV):
            px = 1 - x if k & 4 else x
            py = 1 - y if k & 2 else y
            pc = 1 - cc if k & 1 else cc
            peer = 4 * px + 2 * py + pc
            for t in range(n):
                res.append(pltpu.make_async_remote_copy(
                    src_ref=x_refs[t].at[peer], dst_ref=out_refs[t].at[me], send_sem=send_sems.at[7 * t + k - 1],
                    recv_sem=recv_sems.at[7 * t + k - 1], device_id=(px, py, pc), device_id_type=pl.DeviceIdType.MESH))
        return res

    def start():
        for cp in copies():
            cp.start()

    def finish():
        for cp in copies():
            cp.wait()

    return start, finish


def _comm_scratch(n):
    return [pltpu.SemaphoreType.DMA((7 * n,)), pltpu.SemaphoreType.DMA((7 * n,)), pltpu.SemaphoreType.DMA((n,))]


ANY = pl.BlockSpec(memory_space=pl.ANY)


def _all_gather(xs, name):
    n = len(xs)

    def body(*refs):
        start, finish = _gather_plan(refs[:n], refs[n:2 * n], *refs[2 * n:])
        start()
        finish()

    return pl.pallas_call(
        body, name=name, out_shape=[jax.ShapeDtypeStruct((NDEV,) + a.shape, a.dtype) for a in xs],
        in_specs=[ANY] * n, out_specs=[ANY] * n, scratch_shapes=_comm_scratch(n),
    )(*xs)


def _all_to_all(xs, name):
    n = len(xs)

    def body(*refs):
        start, finish = _exchange_plan(refs[:n], refs[n:2 * n], *refs[2 * n:])
        start()
        finish()

    return pl.pallas_call(
        body, name=name, out_shape=[jax.ShapeDtypeStruct(a.shape, a.dtype) for a in xs],
        in_specs=[ANY] * n, out_specs=[ANY] * n, scratch_shapes=_comm_scratch(n),
    )(*xs)


def _mm(name, a, b, out_shape, grid, a_spec, b_spec, o_spec, dims, acc_shape):
    nk = grid[2]

    def single(a_ref, b_ref, o_ref):
        o_ref[...] = _dot(a_ref[...], b_ref[...], dims).astype(o_ref.dtype)

    if nk == 1:
        return pl.pallas_call(
            single, name=name, grid=grid, out_shape=out_shape, in_specs=[a_spec, b_spec], out_specs=o_spec,
            compiler_params=_cp("parallel", "parallel", "arbitrary"),
        )(a, b)

    def body(a_ref, b_ref, o_ref, acc_ref):
        k = pl.program_id(2)

        @pl.when(k == 0)
        def _():
            acc_ref[...] = jnp.zeros_like(acc_ref)

        acc_ref[...] += _dot(a_ref[...], b_ref[...], dims)

        @pl.when(k == nk - 1)
        def _():
            o_ref[...] = acc_ref[...].astype(o_ref.dtype)

    return pl.pallas_call(
        body, name=name, grid=grid, out_shape=out_shape, in_specs=[a_spec, b_spec], out_specs=o_spec,
        scratch_shapes=[pltpu.VMEM(acc_shape, F32)], compiler_params=_cp("parallel", "parallel", "arbitrary"),
    )(a, b)


def _mm_rows(name, a, w, out_dtype, dims):
    s, k = a.shape
    n = w.shape[1] if dims == NN else w.shape[0]
    tm = min(MM_TILE, s)
    return _mm(name, a, w, jax.ShapeDtypeStruct((s, n), out_dtype), (s // tm, 1, 1),
               pl.BlockSpec((tm, k), lambda i, j, kk: (i, 0)), pl.BlockSpec(w.shape, lambda i, j, kk: (0, 0)),
               pl.BlockSpec((tm, n), lambda i, j, kk: (i, 0)), dims, (tm, n))


def _mm_wgrad(name, a, g, out_dtype):
    s, m = a.shape
    n = g.shape[1]
    tk = min(MM_TILE, s)
    return _mm(name, a, g, jax.ShapeDtypeStruct((m, n), out_dtype), (1, 1, s // tk),
               pl.BlockSpec((tk, m), lambda i, j, kk: (kk, 0)), pl.BlockSpec((tk, n), lambda i, j, kk: (kk, 0)),
               pl.BlockSpec((m, n), lambda i, j, kk: (0, 0)), TN, (m, n))


def _mm_up(h2, wup):
    s = h2.shape[0]
    tm = min(MM_TILE, s)
    return _mm("ffn_up", h2, wup, jax.ShapeDtypeStruct((NDEV, s, FB), BF16), (NDEV, s // tm, 1),
               pl.BlockSpec((tm, D), lambda j, i, kk: (i, 0)), pl.BlockSpec((None, D, FB), lambda j, i, kk: (j, 0, 0)),
               pl.BlockSpec((None, tm, FB), lambda j, i, kk: (j, i, 0)), NN, (tm, FB))


def _mm_down(act, wdown):
    s = act.shape[1]
    tm = min(MM_TILE, s)
    return _mm("ffn_down", act, wdown, jax.ShapeDtypeStruct((s, D), BF16), (s // tm, 1, 4),
               pl.BlockSpec((None, tm, FB), lambda i, j, kk: (kk, i, 0)), pl.BlockSpec((None, FB, D), lambda i, j, kk: (kk, 0, 0)),
               pl.BlockSpec((tm, D), lambda i, j, kk: (i, 0)), NN, (tm, D))


def _mm_dact(df, wdown):
    s = df.shape[0]
    tm = min(MM_TILE, s)
    return _mm("ffn_dact", df, wdown, jax.ShapeDtypeStruct((4, s, FB), BF16), (4, s // tm, 1),
               pl.BlockSpec((tm, D), lambda j, i, kk: (i, 0)), pl.BlockSpec((None, FB, D), lambda j, i, kk: (j, 0, 0)),
               pl.BlockSpec((None, tm, FB), lambda j, i, kk: (j, i, 0)), NT, (tm, FB))


def _mm_dwdown(act, df):
    s = df.shape[0]
    tk = min(MM_TILE, s)
    return _mm("ffn_dwdown", act, df, jax.ShapeDtypeStruct((4, FB, D), BF16), (4, 1, s // tk),
               pl.BlockSpec((None, tk, FB), lambda j, i, kk: (j, kk, 0)), pl.BlockSpec((tk, D), lambda j, i, kk: (kk, 0)),
               pl.BlockSpec((None, FB, D), lambda j, i, kk: (j, 0, 0)), TN, (FB, D))


def _mm_dh2(dupre, wup):
    s = dupre.shape[1]
    tm = min(MM_TILE, s)
    return _mm("ffn_dh2", dupre, wup, jax.ShapeDtypeStruct((s, D), BF16), (s // tm, 1, NDEV),
               pl.BlockSpec((None, tm, FB), lambda i, j, kk: (kk, i, 0)), pl.BlockSpec((None, D, FB), lambda i, j, kk: (kk, 0, 0)),
               pl.BlockSpec((tm, D), lambda i, j, kk: (i, 0)), NT, (tm, D))


def _mm_dwup(h2, dupre):
    s = h2.shape[0]
    tk = min(MM_TILE, s)
    return _mm("ffn_dwup", h2, dupre, jax.ShapeDtypeStruct((NDEV, D, FB), BF16), (NDEV, 1, s // tk),
               pl.BlockSpec((tk, D), lambda j, i, kk: (kk, 0)), pl.BlockSpec((None, tk, FB), lambda j, i, kk: (j, kk, 0)),
               pl.BlockSpec((None, D, FB), lambda j, i, kk: (j, 0, 0)), TN, (D, FB))


def _rms(xv, w, out_dtype, name):
    s, d = xv.shape
    ts = min(ROW_TILE, s)

    def body(x_ref, w_ref, o_ref):
        for r0 in range(0, ts, NORM_CHUNK):
            rows = pl.ds(r0, NORM_CHUNK)
            xf = x_ref[rows, :].astype(F32)
            r = lax.rsqrt(jnp.mean(xf * xf, axis=-1, keepdims=True) + EPS)
            o_ref[rows, :] = (xf * r * w_ref[...]).astype(o_ref.dtype)

    return pl.pallas_call(
        body, name=name, grid=(s // ts,), out_shape=jax.ShapeDtypeStruct((s, d), out_dtype),
        in_specs=[pl.BlockSpec((ts, d), lambda i: (i, 0)), pl.BlockSpec((1, d), lambda i: (0, 0))],
        out_specs=pl.BlockSpec((ts, d), lambda i: (i, 0)), compiler_params=_cp("parallel"),
    )(xv, w)


def _add_rms(xv, mv, w, name, w_next=None):
    s, d = xv.shape
    ts = min(ROW_TILE, s)
    both = w_next is not None

    def body(*refs):
        x_ref, m_ref, w_ref = refs[:3]
        o_ref = refs[4] if both else refs[3]
        for r0 in range(0, ts, NORM_CHUNK):
            rows = pl.ds(r0, NORM_CHUNK)
            mf = m_ref[rows, :].astype(F32)
            r = lax.rsqrt(jnp.mean(mf * mf, axis=-1, keepdims=True) + EPS)
            y = x_ref[rows, :] + mf * r * w_ref[...]
            o_ref[rows, :] = y
            if both:
                r2 = lax.rsqrt(jnp.mean(y * y, axis=-1, keepdims=True) + EPS)
                refs[5][rows, :] = (y * r2 * refs[3][...]).astype(BF16)

    row = pl.BlockSpec((ts, d), lambda i: (i, 0))
    vec = pl.BlockSpec((1, d), lambda i: (0, 0))
    if both:
        return pl.pallas_call(
            body, name=name + "_rms", grid=(s // ts,),
            out_shape=(jax.ShapeDtypeStruct((s, d), F32), jax.ShapeDtypeStruct((s, d), BF16)),
            in_specs=[row, row, vec, vec], out_specs=(row, row), compiler_params=_cp("parallel"),
        )(xv, mv, w, w_next)
    return pl.pallas_call(
        body, name=name, grid=(s // ts,), out_shape=jax.ShapeDtypeStruct((s, d), F32),
        in_specs=[row, row, vec], out_specs=row, compiler_params=_cp("parallel"),
    )(xv, mv, w)


def _rms_bwd_math(xf, w, dy):
    r = lax.rsqrt(jnp.mean(xf * xf, axis=-1, keepdims=True) + EPS)
    xh = xf * r
    dxh = dy * w
    dx = r * (dxh - xh * jnp.mean(dxh * xh, axis=-1, keepdims=True))
    return dx, dy * xh


def _rms_bwd(xv, w, dy, dres, out_dtype, name):
    s, d = xv.shape
    ts = min(ROW_TILE, s)
    with_res = dres is not None

    def body(*refs):
        if with_res:
            x_ref, w_ref, dy_ref, dres_ref, dx_ref, dw_ref, acc = refs
        else:
            x_ref, w_ref, dy_ref, dx_ref, dw_ref, acc = refs
        acc[...] = jnp.zeros_like(acc)
        for r0 in range(0, ts, NORM_CHUNK):
            rows = pl.ds(r0, NORM_CHUNK)
            dx, dwt = _rms_bwd_math(x_ref[rows, :].astype(F32), w_ref[...], dy_ref[rows, :].astype(F32))
            if with_res:
                dx = dx + dres_ref[rows, :]
            dx_ref[rows, :] = dx.astype(dx_ref.dtype)
            acc[...] += dwt

        @pl.when(pl.program_id(0) == 0)
        def _():
            dw_ref[...] = jnp.zeros_like(dw_ref)

        dw_ref[...] += _rowsum(acc[...])

    row = pl.BlockSpec((ts, d), lambda i: (i, 0))
    vec = pl.BlockSpec((1, d), lambda i: (0, 0))
    ins = [xv, w, dy] + ([dres] if with_res else [])
    return pl.pallas_call(
        body, name=name, grid=(s // ts,),
        out_shape=(jax.ShapeDtypeStruct((s, d), out_dtype), jax.ShapeDtypeStruct((1, d), F32)),
        in_specs=[row, vec, row] + ([row] if with_res else []), out_specs=(row, vec),
        scratch_shapes=[pltpu.VMEM((NORM_CHUNK, d), F32)], compiler_params=_cp("arbitrary"),
    )(*ins)


def _loss_head(yv, tv):
    s, d = yv.shape
    ts = min(ROW_TILE, s)

    def body(y_ref, t_ref, l_ref, dy_ref):
        e = y_ref[...] - t_ref[...]
        dy_ref[...] = e * (1.0 / d)

        @pl.when(pl.program_id(0) == 0)
        def _():
            l_ref[...] = jnp.zeros_like(l_ref)

        tot = jnp.sum(jnp.sum(e * e, axis=1, keepdims=True), axis=0, keepdims=True)
        l_ref[...] += jnp.broadcast_to(tot * (0.5 / d), (8, LANE))

    row = pl.BlockSpec((ts, d), lambda i: (i, 0))
    return pl.pallas_call(
        body, name="loss_head", grid=(s // ts,),
        out_shape=(jax.ShapeDtypeStruct((8, LANE), F32), jax.ShapeDtypeStruct((s, d), F32)),
        in_specs=[row, row], out_specs=(pl.BlockSpec((8, LANE), lambda i: (0, 0)), row), compiler_params=_cp("arbitrary"),
    )(yv, tv)


def _rope(v, c, a, b):
    return v * c + pltpu.roll(v, LANE - 16, 1) * a + pltpu.roll(v, 16, 1) * b


def _rope_t(dv, c, a, b):
    return dv * c + pltpu.roll(dv * a, 16, 1) + pltpu.roll(dv * b, LANE - 16, 1)


def _mla_prep(proj, tabs, qnw, kvnw, wq, wkv):
    s = proj.shape[0]
    ts = min(ROW_TILE, s)
    tc, ta, tb = tabs

    def body(cq_ref, ckv_ref, kr_ref, c_ref, a_ref, b_ref, qnw_ref, kvnw_ref, wq_ref, wkv_ref, q_ref, k_ref, kv_ref):
        c, a, b = c_ref[...], a_ref[...], b_ref[...]
        cq = cq_ref[...].astype(F32)
        qn = (cq * lax.rsqrt(jnp.mean(cq * cq, axis=-1, keepdims=True) + EPS) * qnw_ref[...]).astype(BF16)
        ckv = ckv_ref[...].astype(F32)
        kvn = (ckv * lax.rsqrt(jnp.mean(ckv * ckv, axis=-1, keepdims=True) + EPS) * kvnw_ref[...]).astype(BF16)
        kr = _rope(kr_ref[...].astype(F32), c, a, b)
        lane = lax.broadcasted_iota(jnp.int32, (ts, LANE), 1)
        for h in range(HEADS):
            q_ref[h] = _rope(_dot(qn, wq_ref[h]), c, a, b).astype(BF16)
            kv = _dot(kvn, wkv_ref[h])
            kv_ref[h] = kv.astype(BF16)
            k_ref[h] = jnp.where(lane < NOPE, kv, kr).astype(BF16)

    tab = pl.BlockSpec((ts, LANE), lambda i: (i, 0))
    hd = pl.BlockSpec((HEADS, ts, LANE), lambda i: (0, i, 0))
    out = jax.ShapeDtypeStruct((HEADS, s, LANE), BF16)
    return pl.pallas_call(
        body, name="mla_prep", grid=(s // ts,), out_shape=(out, out, out),
        in_specs=[pl.BlockSpec((ts, QL), lambda i: (i, 0)), pl.BlockSpec((ts, LANE), lambda i: (i, 2)),
                  pl.BlockSpec((ts, LANE), lambda i: (i, 3)), tab, tab, tab,
                  pl.BlockSpec((1, QL), lambda i: (0, 0)), pl.BlockSpec((1, KVL), lambda i: (0, 0)),
                  pl.BlockSpec((HEADS, QL, LANE), lambda i: (0, 0, 0)), pl.BlockSpec((HEADS, KVL, LANE), lambda i: (0, 0, 0))],
        out_specs=(hd, hd, hd), compiler_params=_cp("parallel"),
    )(proj, proj, proj, tc, ta, tb, qnw, kvnw, wq, wkv)


def _mla_prep_bwd(proj, tabs, qnw, kvnw, wq, wkv, dq, dk, dv):
    s = proj.shape[0]
    ts = min(ROW_TILE, s)
    tc, ta, tb = tabs

    def body(cq_ref, ckv_ref, c_ref, a_ref, b_ref, qnw_ref, kvnw_ref, wq_ref, wkv_ref, dq_ref, dk_ref, dv_ref,
             dcq_ref, dckv_ref, dkr_ref, dwq_ref, dwkv_ref, dqnw_ref, dkvnw_ref):
        @pl.when(pl.program_id(0) == 0)
        def _():
            dwq_ref[...] = jnp.zeros_like(dwq_ref)
            dwkv_ref[...] = jnp.zeros_like(dwkv_ref)
            dqnw_ref[...] = jnp.zeros_like(dqnw_ref)
            dkvnw_ref[...] = jnp.zeros_like(dkvnw_ref)

        c, a, b = c_ref[...], a_ref[...], b_ref[...]
        cq = cq_ref[...].astype(F32)
        qn = (cq * lax.rsqrt(jnp.mean(cq * cq, axis=-1, keepdims=True) + EPS) * qnw_ref[...]).astype(BF16)
        ckv = ckv_ref[...].astype(F32)
        kvn = (ckv * lax.rsqrt(jnp.mean(ckv * ckv, axis=-1, keepdims=True) + EPS) * kvnw_ref[...]).astype(BF16)
        lane = lax.broadcasted_iota(jnp.int32, (ts, LANE), 1)
        dqn = jnp.zeros((ts, QL), F32)
        dkvn = jnp.zeros((ts, KVL), F32)
        dkr = jnp.zeros((ts, LANE), F32)
        for h in range(HEADS):
            dqh = _rope_t(dq_ref[h], c, a, b).astype(BF16)
            dwq_ref[h] += _dot(qn, dqh, TN)
            dqn += _dot(dqh, wq_ref[h], NT)
            dkh = dk_ref[h].astype(F32)
            dkvh = jnp.where(lane < NOPE, dkh, dv_ref[h].astype(F32)).astype(BF16)
            dkr += jnp.where(lane < NOPE, 0.0, dkh)
            dwkv_ref[h] += _dot(kvn, dkvh, TN)
            dkvn += _dot(dkvh, wkv_ref[h], NT)
        dkr_ref[...] = _rope_t(dkr, c, a, b).astype(BF16)
        dcq, dwt = _rms_bwd_math(cq, qnw_ref[...], dqn)
        dcq_ref[...] = dcq.astype(BF16)
        dqnw_ref[...] += _rowsum(dwt)
        dckv, dwt = _rms_bwd_math(ckv, kvnw_ref[...], dkvn)
        dckv_ref[...] = dckv.astype(BF16)
        dkvnw_ref[...] += _rowsum(dwt)

    tab = pl.BlockSpec((ts, LANE), lambda i: (i, 0))
    hd = pl.BlockSpec((HEADS, ts, LANE), lambda i: (0, i, 0))
    wq_spec = pl.BlockSpec((HEADS, QL, LANE), lambda i: (0, 0, 0))
    wkv_spec = pl.BlockSpec((HEADS, KVL, LANE), lambda i: (0, 0, 0))
    return pl.pallas_call(
        body, name="mla_prep_bwd", grid=(s // ts,),
        out_shape=(jax.ShapeDtypeStruct((s, QL), BF16), jax.ShapeDtypeStruct((s, KVL), BF16), jax.ShapeDtypeStruct((s, LANE), BF16),
                   jax.ShapeDtypeStruct((HEADS, QL, LANE), F32), jax.ShapeDtypeStruct((HEADS, KVL, LANE), F32),
                   jax.ShapeDtypeStruct((1, QL), F32), jax.ShapeDtypeStruct((1, KVL), F32)),
        in_specs=[pl.BlockSpec((ts, QL), lambda i: (i, 0)), pl.BlockSpec((ts, LANE), lambda i: (i, 2)), tab, tab, tab,
                  pl.BlockSpec((1, QL), lambda i: (0, 0)), pl.BlockSpec((1, KVL), lambda i: (0, 0)), wq_spec, wkv_spec, hd, hd, hd],
        out_specs=(pl.BlockSpec((ts, QL), lambda i: (i, 0)), pl.BlockSpec((ts, KVL), lambda i: (i, 0)), tab, wq_spec, wkv_spec,
                   pl.BlockSpec((1, QL), lambda i: (0, 0)), pl.BlockSpec((1, KVL), lambda i: (0, 0))),
        compiler_params=_cp("arbitrary"),
    )(proj, proj, tc, ta, tb, qnw, kvnw, wq, wkv, dq, dk, dv)


def _transpose_bf16(v):
    return v.astype(F32).T.astype(BF16)


def _flash_fwd(q, k, kv, prefetch=None):
    s = q.shape[1]
    t = min(ATT_TILE, s)
    n = s // t
    g = FWD_HEADS
    nx = len(prefetch) if prefetch else 0

    def body(*refs):
        q_ref, k_ref, kv_ref = refs[:3]
        o_ref, lse_ref = refs[3 + nx:5 + nx]
        kvt_sc = refs[5 + 2 * nx]
        step = pl.program_id(0) * n + pl.program_id(1)
        if nx:
            start, finish = _gather_plan(refs[3:3 + nx], refs[5 + nx:5 + 2 * nx], *refs[6 + 2 * nx:])
            pl.when(step == 0)(start)
        attend(q_ref, k_ref, kv_ref, o_ref, lse_ref, kvt_sc)
        if nx:
            pl.when(step == (HEADS // g) * n - 1)(finish)

    def attend(q_ref, k_ref, kv_ref, o_ref, lse_ref, kvt_sc):
        i = pl.program_id(1)

        @pl.when(i == 0)
        def _():
            ones_rows = lax.broadcasted_iota(jnp.int32, (LANE, s), 0) < NOPE
            for hh in range(g):
                kvt_sc[hh] = jnp.where(ones_rows, 1.0, kv_ref[hh].astype(F32).T).astype(BF16)

        qt = [(q_ref[hh].astype(F32) * (ATT_SCALE * LOG2E)).T.astype(BF16) for hh in range(g)]
        kpos = lax.broadcasted_iota(jnp.int32, (t, t), 0)
        qpos = lax.broadcasted_iota(jnp.int32, (t, t), 1)

        def chunk(j, carry, diagonal):
            start = pl.multiple_of(j * t, t)
            scs = [_dot(k_ref[hh, pl.ds(start, t), :], qt[hh]) for hh in range(g)]
            soft = []
            for hh in range(g):
                sc = scs[hh]
                if diagonal:
                    sc = jnp.where(qpos >= kpos, sc, NEG)
                m_new = jnp.maximum(carry[hh][0], jnp.max(sc, axis=0, keepdims=True))
                soft.append((m_new, jnp.exp2(carry[hh][0] - m_new), jnp.exp2(sc - m_new).astype(BF16)))
            pvs = [_dot(kvt_sc[hh, :, pl.ds(start, t)], soft[hh][2]) for hh in range(g)]
            return tuple((soft[hh][0], soft[hh][1] * carry[hh][1] + pvs[hh]) for hh in range(g))

        init = tuple((jnp.full((1, t), NEG, F32), jnp.zeros((LANE, t), F32)) for _ in range(g))
        carry = lax.fori_loop(0, i, lambda j, c: chunk(j, c, False), init)
        carry = chunk(i, carry, True)
        for hh in range(g):
            m, acc = carry[hh]
            l = acc[0:1, :]
            o_ref[:, hh * LANE:(hh + 1) * LANE] = (acc / l).T.astype(BF16)
            lse_ref[hh] = m + jnp.log2(l)

    whole = pl.BlockSpec((g, s, LANE), lambda h, i: (h, 0, 0))
    res = pl.pallas_call(
        body, name="flash_fwd_gather" if nx else "flash_fwd", grid=(HEADS // g, n),
        out_shape=[jax.ShapeDtypeStruct((s, HEADS * LANE), BF16), jax.ShapeDtypeStruct((HEADS, 1, s), F32)]
        + [jax.ShapeDtypeStruct((NDEV,) + a.shape, a.dtype) for a in (prefetch or [])],
        in_specs=[pl.BlockSpec((g, t, LANE), lambda h, i: (h, i, 0)), whole, whole] + [ANY] * nx,
        out_specs=[pl.BlockSpec((t, g * LANE), lambda h, i: (i, h)), pl.BlockSpec((g, 1, t), lambda h, i: (h, 0, i))] + [ANY] * nx,
        scratch_shapes=[pltpu.VMEM((g, LANE, s), BF16)] + (_comm_scratch(nx) if nx else []),
        compiler_params=_cp("arbitrary", "arbitrary"),
    )(q, k, kv, *(prefetch or []))
    return res[0], res[1], list(res[2:])


def _flash_bwd(q, k, kv, cat, dcat, lse, pending=None):
    s = q.shape[1]
    t = min(ATT_TILE, s)
    n = s // t
    g = BWD_HEADS
    nx = len(pending) if pending else 0

    def body(*refs):
        ins, outs, scr = refs[:6], refs[6 + nx:9 + nx], refs[9 + 2 * nx:14 + 2 * nx]
        step = pl.program_id(0) * n + pl.program_id(1)
        if nx:
            start, finish = _exchange_plan(refs[6:6 + nx], refs[9 + nx:9 + 2 * nx], *refs[14 + 2 * nx:])
            pl.when(step == 0)(start)
        attend(*ins, *outs, *scr)
        if nx:
            pl.when(step == (HEADS // g) * n - 1)(finish)

    def attend(q_ref, k_ref, kv_ref, o_ref, do_ref, lse_ref, dq_ref, dk_ref, dv_ref, qt_sc, dot_sc, delta_sc, dqt_sc, qs_sc):
        j = pl.program_id(1)

        @pl.when(j == 0)
        def _():
            for hh in range(g):
                lanes = slice(hh * LANE, (hh + 1) * LANE)
                qf = q_ref[hh].astype(F32)
                qt_sc[hh] = (qf * (ATT_SCALE * LOG2E)).T.astype(BF16)
                qs_sc[hh] = (qf * ATT_SCALE).astype(BF16)
                dof = do_ref[:, lanes].astype(F32)
                dot_sc[hh] = dof.T.astype(BF16)
                delta_sc[hh] = _dot(jnp.ones((8, LANE), F32), dof * o_ref[:, lanes].astype(F32), NT, precision=HI)
            dqt_sc[...] = jnp.zeros_like(dqt_sc)

        kjt = [_transpose_bf16(k_ref[hh]) for hh in range(g)]
        kpos = lax.broadcasted_iota(jnp.int32, (t, t), 0)
        qpos = lax.broadcasted_iota(jnp.int32, (t, t), 1)

        def chunk(i, carry, diagonal):
            start = pl.multiple_of(i * t, t)
            cols = pl.ds(start, t)
            scs = [_dot(k_ref[hh], qt_sc[hh, :, cols]) for hh in range(g)]
            dps = [_dot(kv_ref[hh], dot_sc[hh, :, cols]) for hh in range(g)]
            pds = []
            for hh in range(g):
                p = jnp.exp2(scs[hh] - lse_ref[hh, :, cols])
                if diagonal:
                    p = jnp.where(qpos >= kpos, p, 0.0)
                ds = (p * (dps[hh] - delta_sc[hh, 0:1, cols])).astype(BF16)
                pds.append((p.astype(BF16), ds))
            out = []
            for hh in range(g):
                dk, dv = carry[hh]
                dv = dv + _dot(pds[hh][0], do_ref[pl.ds(start, t), hh * LANE:(hh + 1) * LANE])
                dk = dk + _dot(pds[hh][1], qs_sc[hh, pl.ds(start, t), :])
                dqt_sc[hh, :, cols] += _dot(kjt[hh], pds[hh][1])
                out.append((dk, dv))
            return tuple(out)

        zero = jnp.zeros((t, LANE), F32)
        carry = chunk(j, tuple((zero, zero) for _ in range(g)), True)
        carry = lax.fori_loop(j + 1, n, lambda i, c: chunk(i, c, False), carry)
        for hh in range(g):
            dk_ref[hh] = carry[hh][0].astype(BF16)
            dv_ref[hh] = carry[hh][1].astype(BF16)

        @pl.when(j == n - 1)
        def _():
            for hh in range(g):
                dq_ref[hh] = (dqt_sc[hh] * ATT_SCALE).T

    whole = pl.BlockSpec((g, s, LANE), lambda h, j: (h, 0, 0))
    kspec = pl.BlockSpec((g, t, LANE), lambda h, j: (h, j, 0))
    ospec = pl.BlockSpec((s, g * LANE), lambda h, j: (0, h))
    res = pl.pallas_call(
        body, name="flash_bwd_exchange" if nx else "flash_bwd", grid=(HEADS // g, n),
        out_shape=[jax.ShapeDtypeStruct((HEADS, s, LANE), F32), jax.ShapeDtypeStruct((HEADS, s, LANE), BF16),
                   jax.ShapeDtypeStruct((HEADS, s, LANE), BF16)] + [jax.ShapeDtypeStruct(a.shape, a.dtype) for a in (pending or [])],
        in_specs=[whole, kspec, kspec, ospec, ospec, pl.BlockSpec((g, 1, s), lambda h, j: (h, 0, 0))] + [ANY] * nx,
        out_specs=[whole, kspec, kspec] + [ANY] * nx,
        scratch_shapes=[pltpu.VMEM((g, LANE, s), BF16), pltpu.VMEM((g, LANE, s), BF16), pltpu.VMEM((g, 8, s), F32),
                        pltpu.VMEM((g, LANE, s), F32), pltpu.VMEM((g, s, LANE), BF16)] + (_comm_scratch(nx) if nx else []),
        compiler_params=_cp("arbitrary", "arbitrary"),
    )(q, k, kv, cat, dcat, lse, *(pending or []))
    return res[0], res[1], res[2], list(res[3:])


def _conv3(ext, w_ref, ts):
    return (w_ref[0:1, :] * ext[pl.ds(HALO - 2, ts), :] + w_ref[1:2, :] * ext[pl.ds(HALO - 1, ts), :]
            + w_ref[2:3, :] * ext[pl.ds(HALO, ts), :])


def _conv3_rows(ext, w_ref, r):
    return (w_ref[0:1, :] * ext[pl.ds(HALO - 2 + r, ROW_CHUNK), :] + w_ref[1:2, :] * ext[pl.ds(HALO - 1 + r, ROW_CHUNK), :]
            + w_ref[2:3, :] * ext[pl.ds(HALO + r, ROW_CHUNK), :])


def _conv3_t(ext2, w_ref, ts):
    return (w_ref[0:1, :] * ext2[pl.ds(2, ts), :] + w_ref[1:2, :] * ext2[pl.ds(1, ts), :] + w_ref[2:3, :] * ext2[pl.ds(0, ts), :])


def _sconv_fwd(proj, w):
    s = proj.shape[0]
    ts = min(ROW_TILE, s)

    def body(b_ref, c_ref, h_ref, hc_ref, hh_ref, w_ref, o_ref, ext):
        i = pl.program_id(0)
        ext[0:HALO, :] = hc_ref[...].astype(F32) * hh_ref[...].astype(F32) * (i > 0).astype(F32)
        ext[HALO:HALO + ts, :] = c_ref[...].astype(F32) * h_ref[...].astype(F32)
        o_ref[...] = (b_ref[...].astype(F32) * _conv3(ext, w_ref, ts)).astype(BF16)

    def col(cb):
        return pl.BlockSpec((ts, SC), lambda i: (i, cb))

    def halo(cb):
        return pl.BlockSpec((HALO, SC), lambda i: (_prev_halo(i, ts), cb))

    return pl.pallas_call(
        body, name="sconv_fwd", grid=(s // ts,), out_shape=jax.ShapeDtypeStruct((s, SC), BF16),
        in_specs=[col(2), col(3), col(4), halo(3), halo(4), pl.BlockSpec((3, SC), lambda i: (0, 0))],
        out_specs=pl.BlockSpec((ts, SC), lambda i: (i, 0)), scratch_shapes=[pltpu.VMEM((ts + HALO, SC), F32)],
        compiler_params=_cp("parallel"),
    )(proj, proj, proj, proj, proj, w)


def _sconv_bwd(proj, dcat, w):
    s = proj.shape[0]
    ts = min(ROW_TILE, s)
    n = s // ts

    def body(b_ref, c_ref, h_ref, hc_ref, hh_ref, dy_ref, ndy_ref, nb_ref, w_ref, db_ref, dc_ref, dh_ref, dw_ref, ext, ext2):
        i = pl.program_id(0)

        @pl.when(i == 0)
        def _():
            dw_ref[...] = jnp.zeros_like(dw_ref)

        cv, hv, bv = c_ref[...].astype(F32), h_ref[...].astype(F32), b_ref[...].astype(F32)
        ext[0:HALO, :] = hc_ref[...].astype(F32) * hh_ref[...].astype(F32) * (i > 0).astype(F32)
        ext[HALO:HALO + ts, :] = cv * hv
        dy = dy_ref[...].astype(F32)
        db_ref[...] = (dy * _conv3(ext, w_ref, ts)).astype(BF16)
        dyb = dy * bv
        ext2[0:ts, :] = dyb
        ext2[ts:ts + HALO, :] = ndy_ref[...].astype(F32) * nb_ref[...].astype(F32) * (i < n - 1).astype(F32)
        dg = _conv3_t(ext2, w_ref, ts)
        dc_ref[...] = (dg * hv).astype(BF16)
        dh_ref[...] = (dg * cv).astype(BF16)
        for kk in range(3):
            dw_ref[kk:kk + 1, :] += _rowsum(dyb * ext[pl.ds(HALO - 2 + kk, ts), :])

    def col(cb):
        return pl.BlockSpec((ts, SC), lambda i: (i, cb))

    def halo(cb):
        return pl.BlockSpec((HALO, SC), lambda i: (_prev_halo(i, ts), cb))

    def nxt(cb):
        return pl.BlockSpec((HALO, SC), lambda i: (_next_halo(i, ts, n), cb))

    out = jax.ShapeDtypeStruct((s, SC), BF16)
    o0 = pl.BlockSpec((ts, SC), lambda i: (i, 0))
    return pl.pallas_call(
        body, name="sconv_bwd", grid=(n,), out_shape=(out, out, out, jax.ShapeDtypeStruct((3, SC), F32)),
        in_specs=[col(2), col(3), col(4), halo(3), halo(4), col(4), nxt(4), nxt(2), pl.BlockSpec((3, SC), lambda i: (0, 0))],
        out_specs=(o0, o0, o0, pl.BlockSpec((3, SC), lambda i: (0, 0))),
        scratch_shapes=[pltpu.VMEM((ts + HALO, SC), F32), pltpu.VMEM((ts + HALO, SC), F32)], compiler_params=_cp("arbitrary"),
    )(proj, proj, proj, proj, proj, dcat, dcat, proj, w)


def _ffn_stage(ext, u_ref, halo_ref, i, ts):
    ext[0:HALO, :] = halo_ref[...].astype(F32) * (i > 0).astype(F32)
    ext[HALO:HALO + ts, :] = u_ref[...].astype(F32)


def _ffn_specs(ts):
    cur = pl.BlockSpec((2, None, ts, FB), lambda j, i: (0, j, i, 0))
    halo = pl.BlockSpec((2, None, HALO, FB), lambda j, i: (0, j, _prev_halo(i, ts), 0))
    w = pl.BlockSpec((2, None, 3, FB), lambda j, i: (0, j, 0, 0))
    b = pl.BlockSpec((2, None, 1, FB), lambda j, i: (0, j, 0, 0))
    return cur, halo, w, b


def _ffn_act(upre, fcw, fcb):
    s = upre.shape[1]
    ts = min(ROW_TILE, s)

    def body(u_ref, halo_ref, w_ref, b_ref, o_ref, ext_g, ext_u):
        i = pl.program_id(1)
        _ffn_stage(ext_g, u_ref.at[0], halo_ref.at[0], i, ts)
        _ffn_stage(ext_u, u_ref.at[1], halo_ref.at[1], i, ts)
        for r in range(0, ts, ROW_CHUNK):
            gate = b_ref[0] + _conv3_rows(ext_g, w_ref.at[0], r)
            up = b_ref[1] + _conv3_rows(ext_u, w_ref.at[1], r)
            o_ref[pl.ds(r, ROW_CHUNK), :] = (gate * _sig(gate) * up).astype(BF16)

    cur, halo, w, b = _ffn_specs(ts)
    u4 = upre.reshape(2, 4, s, FB)
    return pl.pallas_call(
        body, name="ffn_act", grid=(4, s // ts), out_shape=jax.ShapeDtypeStruct((4, s, FB), BF16),
        in_specs=[cur, halo, w, b], out_specs=pl.BlockSpec((None, ts, FB), lambda j, i: (j, i, 0)),
        scratch_shapes=[pltpu.VMEM((ts + HALO, FB), F32), pltpu.VMEM((ts + HALO, FB), F32)], compiler_params=_cp("parallel", "parallel"),
    )(u4, u4, fcw.reshape(2, 4, 3, FB), fcb.reshape(2, 4, 1, FB))


def _ffn_bwd(upre, dact, fcw, fcb):
    s = upre.shape[1]
    ts = min(ROW_TILE, s)
    n = s // ts
    te = ts + HALO

    def body(u_ref, halo_ref, nxt_ref, w_ref, b_ref, da_ref, nda_ref, dup_ref, db_ref, dw_ref, ext_g, ext_u, ext_da, du_g, du_u, acc):
        i = pl.program_id(1)

        @pl.when(i == 0)
        def _():
            db_ref[...] = jnp.zeros_like(db_ref)
            dw_ref[...] = jnp.zeros_like(dw_ref)

        more = (i < n - 1).astype(F32)
        for idx, ext in ((0, ext_g), (1, ext_u)):
            _ffn_stage(ext, u_ref.at[idx], halo_ref.at[idx], i, ts)
            ext[HALO + ts:HALO + te, :] = nxt_ref[idx].astype(F32) * more
        ext_da[0:ts, :] = da_ref[...].astype(F32)
        ext_da[ts:te, :] = nda_ref[...].astype(F32) * more
        acc[...] = jnp.zeros_like(acc)
        for r in range(0, te, ROW_CHUNK):
            rows = pl.ds(r, ROW_CHUNK)
            gate = b_ref[0] + _conv3_rows(ext_g, w_ref.at[0], r)
            up = b_ref[1] + _conv3_rows(ext_u, w_ref.at[1], r)
            sg = _sig(gate)
            da = ext_da[rows, :]
            dgate = da * up * sg * (1.0 + gate * (1.0 - sg))
            dup = da * gate * sg
            du_g[rows, :] = dgate
            du_u[rows, :] = dup
            if r < ts:
                acc[0] += dgate
                acc[1] += dup
        for r in range(0, ts, ROW_CHUNK):
            for idx, du, ext in ((0, du_g, ext_g), (1, du_u, ext_u)):
                d0 = du[pl.ds(r, ROW_CHUNK), :]
                dupre = (w_ref[idx, 2:3, :] * d0 + w_ref[idx, 1:2, :] * du[pl.ds(r + 1, ROW_CHUNK), :]
                         + w_ref[idx, 0:1, :] * du[pl.ds(r + 2, ROW_CHUNK), :])
                dup_ref[idx, pl.ds(r, ROW_CHUNK), :] = dupre.astype(BF16)
                for kk in range(3):
                    acc[2 + 3 * idx + kk] += d0 * ext[pl.ds(HALO - 2 + kk + r, ROW_CHUNK), :]
        for idx in range(2):
            db_ref[idx] += _rowsum(acc[idx])
            for kk in range(3):
                dw_ref[idx, kk:kk + 1, :] += _rowsum(acc[2 + 3 * idx + kk])

    cur, halo, w, b = _ffn_specs(ts)
    nxt = pl.BlockSpec((2, None, HALO, FB), lambda j, i: (0, j, _next_halo(i, ts, n), 0))
    u4 = upre.reshape(2, 4, s, FB)
    dupre, db, dw = pl.pallas_call(
        body, name="ffn_bwd", grid=(4, n),
        out_shape=(jax.ShapeDtypeStruct((2, 4, s, FB), BF16), jax.ShapeDtypeStruct((2, 4, 1, FB), F32),
                   jax.ShapeDtypeStruct((2, 4, 3, FB), F32)),
        in_specs=[cur, halo, nxt, w, b, pl.BlockSpec((None, ts, FB), lambda j, i: (j, i, 0)),
                  pl.BlockSpec((None, HALO, FB), lambda j, i: (j, _next_halo(i, ts, n), 0))],
        out_specs=(cur, b, w),
        scratch_shapes=[pltpu.VMEM((te + HALO, FB), F32), pltpu.VMEM((te + HALO, FB), F32), pltpu.VMEM((te, FB), F32),
                        pltpu.VMEM((te, FB), F32), pltpu.VMEM((te, FB), F32), pltpu.VMEM((8, ROW_CHUNK, FB), F32)],
        compiler_params=_cp("parallel", "arbitrary"),
    )(u4, u4, u4, fcw.reshape(2, 4, 3, FB), fcb.reshape(2, 4, 1, FB), dact, dact)
    return dupre.reshape(NDEV, s, FB), db.reshape(NDEV, 1, FB), dw.reshape(NDEV, 3, FB)


def _softplus(v):
    e = jnp.exp(-jnp.abs(v))
    return jnp.maximum(v, 0.0) + jnp.where(e < 1e-4, e * (1.0 - 0.5 * e), jnp.log(1.0 + e))


def _ssd_consts():
    L = SSD_L
    r = lax.broadcasted_iota(jnp.int32, (L, L), 0)
    c = lax.broadcasted_iota(jnp.int32, (L, L), 1)
    tri = r >= c
    er = lax.broadcasted_iota(jnp.int32, (LANE, SSD_DIM), 0)
    ec = lax.broadcasted_iota(jnp.int32, (LANE, SSD_DIM), 1)
    expand = ((ec >= er * 64) & (ec < er * 64 + 64)).astype(F32)
    return tri, expand


def _ssd_conv4(ext, cw_ref, cb_ref):
    L = SSD_L
    pre = cb_ref[...] + cw_ref[0:1, :] * ext[pl.ds(HALO - 3, L), :]
    for kk in range(1, 4):
        pre = pre + cw_ref[kk:kk + 1, :] * ext[pl.ds(HALO - 3 + kk, L), :]
    return pre


def _ssd_common(xbc_ref, halo_ref, dt_ref, cw_ref, cb_ref, dtb_ref, alog_ref, ext, first):
    L = SSD_L
    tri, expand = _ssd_consts()
    ext[0:HALO, :] = halo_ref[...].astype(F32) * (1.0 - first.astype(F32))
    ext[HALO:HALO + L, :] = xbc_ref[...].astype(F32)
    pre = _ssd_conv4(ext, cw_ref, cb_ref)
    sg = _sig(pre)
    act = pre * sg
    lane = lax.broadcasted_iota(jnp.int32, (1, LANE), 1)
    m4 = lane < SSD_H
    raw = dt_ref[...].astype(F32) + dtb_ref[...]
    dtv = jnp.where(m4, _softplus(raw), 0.0)
    av = jnp.where(m4, -jnp.exp(alog_ref[...]), 0.0)
    adt = dtv * av
    acs = _dot(tri.astype(F32), adt, precision=HI)
    acs_b = _dot(acs, expand, precision=HI)
    dt_b = _dot(dtv, expand, precision=HI)
    return dict(tri=tri, expand=expand, pre=pre, sg=sg, act=act, raw=raw, dtv=dtv, av=av, m4=m4, acs=acs, acs_b=acs_b,
                dt_b=dt_b, lane=lane)


def _head_terms(cm, h):
    L = SSD_L
    acs, tri = cm["acs"], cm["tri"]
    lane_l = lax.broadcasted_iota(jnp.int32, (L, LANE), 1)
    sub_l = lax.broadcasted_iota(jnp.int32, (LANE, L), 0)
    col = jnp.sum(jnp.where(lane_l == h, acs, 0.0), axis=1, keepdims=True)
    row = jnp.sum(jnp.where(sub_l == h, acs.T, 0.0), axis=0, keepdims=True)
    dec = jnp.where(tri, jnp.exp(jnp.where(tri, col - row, NEG)), 0.0)
    rowi = lax.broadcasted_iota(jnp.int32, (L, 1), 0)
    last = jnp.sum(jnp.where(rowi == L - 1, col, 0.0), axis=0, keepdims=True)
    dte = jnp.exp(last - col)
    return col, dec, last, dte


def _ssd_fwd(proj, cw, cb, dtb, alog, dvec, nw):
    s = proj.shape[0]
    L = SSD_L
    nc = s // L

    def body(z_ref, xbc_ref, halo_ref, dt_ref, cw_ref, cb_ref, dtb_ref, alog_ref, d_ref, nw_ref, y_ref, ypre_ref, st_ref, ext, state):
        i = pl.program_id(0)

        @pl.when(i == 0)
        def _():
            state[...] = jnp.zeros_like(state)

        cm = _ssd_common(xbc_ref, halo_ref, dt_ref, cw_ref, cb_ref, dtb_ref, alog_ref, ext, i == 0)
        act = cm["act"]
        xs = act[:, 0:256]
        bm = (act[:, 256:384], act[:, 384:512])
        cmat = (act[:, 512:640].astype(BF16), act[:, 640:768].astype(BF16))
        xdt = xs * cm["dt_b"]
        prev = state[...]
        st_ref[...] = prev
        prev_bf = prev.astype(BF16)
        gm = [_dot(cmat[g], bm[g].astype(BF16), NT) for g in range(2)]
        lane2 = lax.broadcasted_iota(jnp.int32, (1, SSD_DIM), 1)
        rows2 = lax.broadcasted_iota(jnp.int32, (SSD_DIM, 1), 0)
        ydiag = jnp.zeros((L, SSD_DIM), F32)
        contrib = jnp.zeros((SSD_DIM, LANE), F32)
        cd_rows = jnp.zeros((SSD_DIM, 1), F32)
        for h in range(SSD_H):
            g = h // 2
            col, dec, last, dte = _head_terms(cm, h)
            mh = (lane2 >= 64 * h) & (lane2 < 64 * h + 64)
            xm = jnp.where(mh, xdt, 0.0).astype(BF16)
            ydiag += _dot((gm[g] * dec).astype(BF16), xm)
            contrib += _dot(xm, (bm[g] * dte).astype(BF16), TN)
            cd_rows += jnp.where((rows2 >= 64 * h) & (rows2 < 64 * h + 64), jnp.exp(last), 0.0)
        yo = jnp.where(lane2 < 128, _dot(cmat[0], prev_bf, NT), _dot(cmat[1], prev_bf, NT))
        y = ydiag + yo * jnp.exp(cm["acs_b"]) + xs * d_ref[...]
        state[...] = prev * cd_rows + contrib
        ypre_ref[...] = y
        zz = z_ref[...].astype(F32)
        gt = y * zz * _sig(zz)
        y_ref[...] = (gt * lax.rsqrt(jnp.mean(gt * gt, axis=-1, keepdims=True) + EPS) * nw_ref[...]).astype(BF16)

    def vec(w):
        return pl.BlockSpec((1, w), lambda i: (0, 0))

    return pl.pallas_call(
        body, name="ssd_fwd", grid=(nc,),
        out_shape=(jax.ShapeDtypeStruct((s, SSD_DIM), BF16), jax.ShapeDtypeStruct((s, SSD_DIM), F32),
                   jax.ShapeDtypeStruct((nc, SSD_DIM, LANE), F32)),
        in_specs=[pl.BlockSpec((L, SSD_DIM), lambda i: (i, 5)), pl.BlockSpec((L, SSD_CONV), lambda i: (i, 2)),
                  pl.BlockSpec((HALO, SSD_CONV), lambda i: (_prev_halo(i, L), 2)), pl.BlockSpec((L, LANE), lambda i: (i, 18)),
                  pl.BlockSpec((4, SSD_CONV), lambda i: (0, 0)), vec(SSD_CONV), vec(LANE), vec(LANE), vec(SSD_DIM), vec(SSD_DIM)],
        out_specs=(pl.BlockSpec((L, SSD_DIM), lambda i: (i, 0)), pl.BlockSpec((L, SSD_DIM), lambda i: (i, 0)),
                   pl.BlockSpec((None, SSD_DIM, LANE), lambda i: (i, 0, 0))),
        scratch_shapes=[pltpu.VMEM((L + HALO, SSD_CONV), F32), pltpu.VMEM((SSD_DIM, LANE), F32)], compiler_params=_cp("arbitrary"),
    )(proj, proj, proj, proj, cw, cb, dtb, alog, dvec, nw)


def _ssd_bwd(proj, dcat, ypre, states, cw, cb, dtb, alog, dvec, nw):
    s = proj.shape[0]
    L = SSD_L
    nc = s // L

    def body(z_ref, xbc_ref, halo_ref, dt_ref, dy_ref, ypre_ref, st_ref, cw_ref, cb_ref, dtb_ref, alog_ref, d_ref, nw_ref,
             dz_ref, dxbc_ref, ddt_ref, dcw_ref, dcb_ref, ddtb_ref, dalog_ref, dd_ref, dnw_ref, ext, ext2, carry, dstate, ddl):
        i = pl.program_id(0)
        r = nc - 1 - i

        @pl.when(i == 0)
        def _():
            for ref in (dcw_ref, dcb_ref, ddtb_ref, dalog_ref, dd_ref, dnw_ref, carry, dstate, ddl):
                ref[...] = jnp.zeros_like(ref)

        cm = _ssd_common(xbc_ref, halo_ref, dt_ref, cw_ref, cb_ref, dtb_ref, alog_ref, ext, r == 0)
        tri, expand, act = cm["tri"], cm["expand"], cm["act"]
        xs = act[:, 0:256]
        bm = (act[:, 256:384], act[:, 384:512])
        cmat = (act[:, 512:640], act[:, 640:768])
        bm_bf = [v.astype(BF16) for v in bm]
        cm_bf = [v.astype(BF16) for v in cmat]
        dt_b = cm["dt_b"]
        xdt = xs * dt_b
        xdt_bf = xdt.astype(BF16)
        ea_b = jnp.exp(cm["acs_b"])
        prev = st_ref[...]
        prev_bf = prev.astype(BF16)
        lane2 = lax.broadcasted_iota(jnp.int32, (1, SSD_DIM), 1)
        rows2 = lax.broadcasted_iota(jnp.int32, (SSD_DIM, 1), 0)
        lane_l = lax.broadcasted_iota(jnp.int32, (L, LANE), 1)
        rowi = lax.broadcasted_iota(jnp.int32, (L, 1), 0)

        y = ypre_ref[...]
        zz = z_ref[...].astype(F32)
        sz = _sig(zz)
        gt = y * zz * sz
        dgt, dwt = _rms_bwd_math(gt, nw_ref[...], dy_ref[...].astype(F32))
        dnw_ref[...] += _rowsum(dwt)
        dy = dgt * zz * sz
        dz_ref[...] = (dgt * y * sz * (1.0 + zz * (1.0 - sz))).astype(BF16)

        ddl[0:1, :] += _rowsum(dy * xs)
        dxs = dy * d_ref[...]

        yo = jnp.where(lane2 < 128, _dot(cm_bf[0], prev_bf, NT), _dot(cm_bf[1], prev_bf, NT))
        dacs_b = dy * yo * ea_b
        dyo = dy * ea_b
        dyo_g = (jnp.where(lane2 < 128, dyo, 0.0).astype(BF16), jnp.where(lane2 >= 128, dyo, 0.0).astype(BF16))
        dc = [_dot(dyo_g[g], prev_bf) for g in range(2)]
        dprev = _dot(dyo_g[0], cm_bf[0], TN) + _dot(dyo_g[1], cm_bf[1], TN)

        gm = [_dot(cm_bf[g], bm_bf[g], NT) for g in range(2)]
        dgm = [jnp.zeros((L, L), F32), jnp.zeros((L, L), F32)]
        db = [jnp.zeros((L, LANE), F32), jnp.zeros((L, LANE), F32)]
        dxdt = jnp.zeros((L, SSD_DIM), F32)
        dacs = jnp.zeros((L, LANE), F32)
        dlast = jnp.zeros((1, LANE), F32)
        cd_rows = jnp.zeros((SSD_DIM, 1), F32)
        dst = dstate[...]
        dst_bf = dst.astype(BF16)
        dsp = dst * prev
        ones = jnp.ones((L, LANE), F32)
        for h in range(SSD_H):
            g = h // 2
            col, dec, last, dte = _head_terms(cm, h)
            mh = (lane2 >= 64 * h) & (lane2 < 64 * h + 64)
            rh = (rows2 >= 64 * h) & (rows2 < 64 * h + 64)
            sc = gm[g] * dec
            xm = jnp.where(mh, xdt, 0.0).astype(BF16)
            dym = jnp.where(mh, dy, 0.0).astype(BF16)
            dsc = _dot(dym, xdt_bf, NT)
            dxdt += _dot(sc.astype(BF16), dym, TN)
            dgm[g] += dsc * dec
            dd = dsc * sc
            rs = jnp.sum(dd, axis=1, keepdims=True)
            cs = _dot(dd, ones, TN, precision=HI)
            dacs += jnp.where(lane_l == h, rs - cs, 0.0)
            bd = (bm[g] * dte).astype(BF16)
            dxdt += jnp.where(mh, _dot(bd, dst_bf, NT), 0.0)
            dbd = _dot(xm, dst_bf)
            db[g] += dbd * dte
            tt = jnp.sum(dbd * bm[g], axis=1, keepdims=True) * dte
            dacs += jnp.where(lane_l == h, -tt, 0.0)
            cdh = jnp.exp(last)
            dcd = jnp.sum(jnp.sum(jnp.where(rh, dsp, 0.0), axis=1, keepdims=True), axis=0, keepdims=True)
            dlast += jnp.where(cm["lane"] == h, jnp.sum(tt, axis=0, keepdims=True) + dcd * cdh, 0.0)
            cd_rows += jnp.where(rh, cdh, 0.0)
        dacs += jnp.where(rowi == L - 1, dlast, 0.0)
        dacs += _dot(dacs_b, expand, NT, precision=HI)
        dstate[...] = dprev + dst * cd_rows

        for g in range(2):
            dgb = dgm[g].astype(BF16)
            dc[g] += _dot(dgb, bm_bf[g])
            db[g] += _dot(dgb, cm_bf[g], TN)

        dadt = _dot(tri.astype(F32), dacs, TN, precision=HI)
        ddtv = dadt * cm["av"] + _dot(dxdt * xs, expand, NT, precision=HI)
        dalog_ref[...] += _rowsum(dadt * cm["dtv"]) * cm["av"]
        dxs += dxdt * dt_b
        draw = jnp.where(cm["m4"], ddtv * _sig(cm["raw"]), 0.0)
        ddtb_ref[...] += _rowsum(draw)
        ddt_ref[...] = draw.astype(BF16)

        dact = jnp.concatenate([dxs, db[0], db[1], dc[0], dc[1]], axis=1)
        sg, pre = cm["sg"], cm["pre"]
        dpre = dact * sg * (1.0 + pre * (1.0 - sg))
        dcb_ref[...] += _rowsum(dpre)
        for kk in range(4):
            dcw_ref[kk:kk + 1, :] += _rowsum(dpre * ext[pl.ds(HALO - 3 + kk, L), :])
        ext2[0:L, :] = dpre
        ext2[L:L + HALO, :] = carry[...]
        dx = cw_ref[3:4, :] * ext2[pl.ds(0, L), :]
        for kk in range(3):
            dx = dx + cw_ref[kk:kk + 1, :] * ext2[pl.ds(3 - kk, L), :]
        dxbc_ref[...] = dx.astype(BF16)
        carry[...] = dpre[0:HALO, :]

        @pl.when(i == nc - 1)
        def _():
            dd_ref[...] = _dot(ddl[...], expand, NT, precision=HI)

    def vec(w):
        return pl.BlockSpec((1, w), lambda i: (0, 0))

    def rv(i):
        return nc - 1 - i

    return pl.pallas_call(
        body, name="ssd_bwd", grid=(nc,),
        out_shape=(jax.ShapeDtypeStruct((s, SSD_DIM), BF16), jax.ShapeDtypeStruct((s, SSD_CONV), BF16), jax.ShapeDtypeStruct((s, LANE), BF16),
                   jax.ShapeDtypeStruct((4, SSD_CONV), F32), jax.ShapeDtypeStruct((1, SSD_CONV), F32), jax.ShapeDtypeStruct((1, LANE), F32),
                   jax.ShapeDtypeStruct((1, LANE), F32), jax.ShapeDtypeStruct((8, LANE), F32), jax.ShapeDtypeStruct((1, SSD_DIM), F32)),
        in_specs=[pl.BlockSpec((L, SSD_DIM), lambda i: (rv(i), 5)), pl.BlockSpec((L, SSD_CONV), lambda i: (rv(i), 2)),
                  pl.BlockSpec((HALO, SSD_CONV), lambda i: (_prev_halo(rv(i), L), 2)), pl.BlockSpec((L, LANE), lambda i: (rv(i), 18)),
                  pl.BlockSpec((L, SSD_DIM), lambda i: (rv(i), 5)), pl.BlockSpec((L, SSD_DIM), lambda i: (rv(i), 0)),
                  pl.BlockSpec((None, SSD_DIM, LANE), lambda i: (rv(i), 0, 0)),
                  pl.BlockSpec((4, SSD_CONV), lambda i: (0, 0)), vec(SSD_CONV), vec(LANE), vec(LANE), vec(SSD_DIM), vec(SSD_DIM)],
        out_specs=(pl.BlockSpec((L, SSD_DIM), lambda i: (rv(i), 0)), pl.BlockSpec((L, SSD_CONV), lambda i: (rv(i), 0)),
                   pl.BlockSpec((L, LANE), lambda i: (rv(i), 0)), pl.BlockSpec((4, SSD_CONV), lambda i: (0, 0)), vec(SSD_CONV),
                   vec(LANE), vec(LANE), pl.BlockSpec((8, LANE), lambda i: (0, 0)), vec(SSD_DIM)),
        scratch_shapes=[pltpu.VMEM((L + HALO, SSD_CONV), F32), pltpu.VMEM((L + HALO, SSD_CONV), F32), pltpu.VMEM((HALO, SSD_CONV), F32),
                        pltpu.VMEM((SSD_DIM, LANE), F32), pltpu.VMEM((8, SSD_DIM), F32)],
        compiler_params=_cp("arbitrary"),
    )(proj, proj, proj, proj, dcat, ypre, states, cw, cb, dtb, alog, dvec, nw)


def _adamw(parts, w, m, v, name):
    r, c = w.shape
    tr = r
    for cand in (256, 128, 64, 32, 16, 8):
        if r % cand == 0 and (cand * c * 4) <= 2 * 1024 * 1024:
            tr = cand
            break
    c1 = 1.0 - B1 ** STEP
    c2 = 1.0 - B2 ** STEP

    def body(p_ref, w_ref, m_ref, v_ref, g_ref, d_ref, nm_ref, nv_ref):
        g = p_ref[0].astype(F32)
        for dev in range(1, NDEV):
            g = g + p_ref[dev].astype(F32)
        mn = B1 * m_ref[...] + (1.0 - B1) * g
        vn = B2 * v_ref[...] + (1.0 - B2) * (g * g)
        g_ref[...] = g
        nm_ref[...] = mn
        nv_ref[...] = vn
        d_ref[...] = -LR * ((mn / c1) / (jnp.sqrt(vn / c2) + AEPS) + WD * w_ref[...])

    blk = pl.BlockSpec((tr, c), lambda i: (i, 0))
    out = jax.ShapeDtypeStruct((r, c), F32)
    return pl.pallas_call(
        body, name=name, grid=(r // tr,), out_shape=(out, out, out, out),
        in_specs=[pl.BlockSpec((NDEV, tr, c), lambda i: (0, i, 0)), blk, blk, blk], out_specs=(blk, blk, blk, blk),
        compiler_params=_cp("parallel"),
    )(parts, w, m, v)


def _pad_win(w):
    z = lambda n: jnp.zeros((w.shape[0], n), w.dtype)
    return jnp.concatenate([w[:, :384], z(64), w[:, 384:416], z(32), w[:, 416:], z(124)], axis=1)


def _unpad_win(g):
    return jnp.concatenate([g[:, :384], g[:, 448:480], g[:, 512:2308]], axis=1)


def _pad_wout(w):
    att = jnp.pad(w[:512].reshape(HEADS, 64, D), ((0, 0), (64, 0), (0, 0))).reshape(HEADS * LANE, D)
    return jnp.concatenate([att, w[512:]], axis=0)


def _unpad_wout(g):
    att = g[:HEADS * LANE].reshape(HEADS, LANE, D)[:, 64:, :].reshape(512, D)
    return jnp.concatenate([att, g[HEADS * LANE:]], axis=0)


def _lanes(v, n=LANE):
    return jnp.pad(v, (0, n - v.shape[0])).reshape(1, n)


def _prep_ffn(big):
    return {"wout": _pad_wout(big["w_out"].reshape(1024, D)), "wup": big["ffn_w_up"], "fcw": big["ffn_conv_w"].astype(F32),
            "wdown": big["ffn_w_down"].reshape(4, FB, D)}


def _prep_layer(big, small, l):
    p = _prep_ffn(big) if "w_out" in big else {}
    p["win"] = _pad_win(big["w_in"].reshape(D, 2212))
    p["wq"] = jnp.pad(big["mla_w_q_up"], ((0, 0), (0, 0), (0, LANE - 96)))
    p["wkv"] = big["mla_w_kv_up"]
    p["scw"] = big["sc_conv_w"].astype(F32).transpose(1, 0, 2).reshape(3, SC)
    p["ssdcw"] = big["ssd_conv_w"].astype(F32).transpose(1, 0, 2).reshape(4, SSD_CONV)
    for nm in ("norm_mix_pre", "norm_mix_post", "norm_ffn_pre", "norm_ffn_post", "mla_q_norm", "mla_kv_norm", "ssd_conv_b", "ssd_norm"):
        p[nm] = small[nm][l].reshape(1, -1)
    p["dtb"] = _lanes(small["ssd_dt_bias"][l])
    p["alog"] = _lanes(small["ssd_a_log"][l])
    p["dvec"] = jnp.repeat(small["ssd_d"][l], 64).reshape(1, SSD_DIM)
    p["fcb"] = small["ffn_conv_b"][l].reshape(NDEV, 1, FB)
    return p


def _rope_tables(positions):
    inv_freq = 1.0 / (ROPE_THETA ** (jnp.arange(0, ROPE, 2, dtype=F32) / ROPE))
    ang = positions.astype(F32)[:, None] * inv_freq
    cos, sin = jnp.cos(ang), jnp.sin(ang)
    s = positions.shape[0]
    z = lambda n: jnp.zeros((s, n), F32)
    tc = jnp.concatenate([jnp.ones((s, 64), F32), cos, cos, z(32)], axis=1)
    ta = jnp.concatenate([z(64), -sin, z(48)], axis=1)
    tb = jnp.concatenate([z(80), sin, z(32)], axis=1)
    return tc, ta, tb


def _layer_fwd(xv, p, tabs, prefetch=None, prep_rest=None, h=None, next_norm=None):
    if h is None:
        h = _rms(xv, p["norm_mix_pre"], BF16, "rms_pre")
    proj = _mm_rows("in_proj", h, p["win"], BF16, NN)
    q, k, kv = _mla_prep(proj, tabs, p["mla_q_norm"], p["mla_kv_norm"], p["wq"], p["wkv"])
    o, lse, gathered = _flash_fwd(q, k, kv, prefetch)
    if prep_rest is not None:
        p = {**p, **prep_rest(gathered)}
    yconv = _sconv_fwd(proj, p["scw"])
    yssd, ypre, states = _ssd_fwd(proj, p["ssdcw"], p["ssd_conv_b"], p["dtb"], p["alog"], p["dvec"], p["ssd_norm"])
    cat = jnp.concatenate([o, yconv, yssd], axis=1)
    mixed = _mm_rows("out_proj", cat, p["wout"], BF16, NN)
    x1, h2 = _add_rms(xv, mixed, p["norm_mix_post"], "add_rms", p["norm_ffn_pre"])
    upre = _mm_up(h2, p["wup"])
    act = _ffn_act(upre, p["fcw"], p["fcb"])
    f = _mm_down(act, p["wdown"])
    x2 = _add_rms(x1, f, p["norm_ffn_post"], "add_rms", next_norm)
    saved = dict(x=xv, h=h, proj=proj, q=q, k=k, kv=kv, lse=lse, ypre=ypre, states=states, cat=cat, mixed=mixed, x1=x1, h2=h2,
                 upre=upre, act=act, f=f)
    return x2, saved, p, gathered


def _pack_grads(grads, group_ids):
    return [_group_pack(GROUPS[gi], lambda n: grads[n].reshape((NDEV,) + _rows2(n, True)), (NDEV,)) for gi in group_ids]


def _layer_bwd(dx2, sv, p, tabs, exchange=False, pending=None):
    df, g_nfpo = _rms_bwd(sv["f"], p["norm_ffn_post"], dx2, None, BF16, "rms_bwd_post")
    dact = _mm_dact(df, p["wdown"])
    g_wdown = _mm_dwdown(sv["act"], df)
    dupre, g_fcb, g_fcw = _ffn_bwd(sv["upre"], dact, p["fcw"], p["fcb"])
    dh2 = _mm_dh2(dupre, p["wup"])
    g_wup = _mm_dwup(sv["h2"], dupre)
    dx1, g_nfp = _rms_bwd(sv["x1"], p["norm_ffn_pre"], dh2, dx2, F32, "rms_bwd_pre")
    dmixed, g_nmpo = _rms_bwd(sv["mixed"], p["norm_mix_post"], dx1, None, BF16, "rms_bwd_post")
    dcat = _mm_rows("dcat", dmixed, p["wout"], BF16, NT)
    g_wout = _mm_wgrad("dw_out", sv["cat"], dmixed, BF16)
    big = {
        "w_out": _unpad_wout(g_wout).reshape(NDEV, 128, D),
        "ffn_w_up": g_wup,
        "ffn_conv_w": g_fcw.astype(BF16),
        "ffn_w_down": g_wdown.reshape(NDEV, 352, D),
    }
    outgoing = _pack_grads(big, FFN_SIDE) + (pending or []) if exchange else None
    dq, dk, dv, received = _flash_bwd(sv["q"], sv["k"], sv["kv"], sv["cat"], dcat, sv["lse"], outgoing)
    dcq, dckv, dkr, g_wq, g_wkv, g_qn, g_kvn = _mla_prep_bwd(sv["proj"], tabs, p["mla_q_norm"], p["mla_kv_norm"], p["wq"], p["wkv"], dq, dk, dv)
    dscb, dscc, dsch, g_scw = _sconv_bwd(sv["proj"], dcat, p["scw"])
    dz, dxbc, ddt, g_cw, g_cb, g_dtb, g_alog, g_d, g_nw = _ssd_bwd(
        sv["proj"], dcat, sv["ypre"], sv["states"], p["ssdcw"], p["ssd_conv_b"], p["dtb"], p["alog"], p["dvec"], p["ssd_norm"])
    dproj = jnp.concatenate([dcq, dckv, dkr, dscb, dscc, dsch, dz, dxbc, ddt], axis=1)
    dh = _mm_rows("dh", dproj, p["win"], BF16, NT)
    g_win = _mm_wgrad("dw_in", sv["h"], dproj, BF16)
    dx, g_nmp = _rms_bwd(sv["x"], p["norm_mix_pre"], dh, dx1, F32, "rms_bwd_pre")
    big.update({
        "w_in": _unpad_win(g_win).reshape(NDEV, 128, 2212),
        "mla_w_q_up": g_wq[:, :, :96].astype(BF16),
        "mla_w_kv_up": g_wkv.astype(BF16),
        "sc_conv_w": g_scw.reshape(3, NDEV, 32).transpose(1, 0, 2).astype(BF16),
        "ssd_conv_w": g_cw.reshape(4, NDEV, 96).transpose(1, 0, 2).astype(BF16),
    })
    small = {
        "norm_mix_pre": g_nmp[0], "norm_mix_post": g_nmpo[0], "norm_ffn_pre": g_nfp[0], "norm_ffn_post": g_nfpo[0],
        "mla_q_norm": g_qn[0], "mla_kv_norm": g_kvn[0], "ssd_conv_b": g_cb[0], "ssd_dt_bias": g_dtb[0, :SSD_H],
        "ssd_a_log": g_alog[0, :SSD_H], "ssd_d": g_d[0, :SSD_H], "ssd_norm": g_nw[0], "ffn_conv_b": g_fcb.reshape(-1),
    }
    return dx, big, small, received


def _local_step(xv, positions, target, layers):
    tabs = _rope_tables(positions)
    saved = []
    for p in layers:
        xv, sv, _, _ = _layer_fwd(xv, p, tabs)
        saved.append(sv)
    loss, dx = _loss_head(xv, target)
    bigs, smalls = [None] * DEPTH, [None] * DEPTH
    for l in reversed(range(len(layers))):
        dx, bigs[l], smalls[l], _ = _layer_bwd(dx, saved[l], layers[l], tabs)
    return loss[0, 0], dx, bigs, smalls


def _pack_rows(flat, lead, width, mult):
    n = flat.shape[-1]
    rows = -(-n // (width * mult)) * mult
    pad = [(0, 0)] * (flat.ndim - 1) + [(0, rows * width - n)]
    return jnp.pad(flat, pad).reshape(lead + (rows, width))


def _rows2(n, layer=False):
    shape = SHAPES[n][1:] if layer else SHAPES[n]
    return (math.prod(shape[:-1]), shape[-1])


def _group_pack(group, get, lead):
    width, names = group
    pieces = []
    for n in names:
        rows, cols = _rows2(n, True)
        pad = [(0, 0)] * len(lead) + [(0, -rows % 16), (0, width - cols)]
        pieces.append(jnp.pad(get(n), pad))
    return pieces[0] if len(pieces) == 1 else jnp.concatenate(pieces, axis=len(lead))


def _group_unpack(group, buf, padded=False):
    _, names = group
    res, off = {}, 0
    for n in names:
        rows, cols = _rows2(n, True)
        rp = rows + (-rows % 16)
        res[n] = buf[:, off:off + (rp if padded else rows), :cols]
        off += rp
    return res


def kernel(x, positions, norm_mix_pre, norm_mix_post, norm_ffn_pre, norm_ffn_post, w_in, mla_q_norm, mla_w_q_up, mla_kv_norm, mla_w_kv_up, sc_conv_w, ssd_conv_w, ssd_conv_b, ssd_dt_bias, ssd_a_log, ssd_d, ssd_norm, w_out, ffn_w_up, ffn_conv_w, ffn_conv_b, ffn_w_down, loss_target, m_norm_mix_pre, m_norm_mix_post, m_norm_ffn_pre, m_norm_ffn_post, m_w_in, m_mla_q_norm, m_mla_w_q_up, m_mla_kv_norm, m_mla_w_kv_up, m_sc_conv_w, m_ssd_conv_w, m_ssd_conv_b, m_ssd_dt_bias, m_ssd_a_log, m_ssd_d, m_ssd_norm, m_w_out, m_ffn_w_up, m_ffn_conv_w, m_ffn_conv_b, m_ffn_w_down, v_norm_mix_pre, v_norm_mix_post, v_norm_ffn_pre, v_norm_ffn_post, v_w_in, v_mla_q_norm, v_mla_w_q_up, v_mla_kv_norm, v_mla_w_kv_up, v_sc_conv_w, v_ssd_conv_w, v_ssd_conv_b, v_ssd_dt_bias, v_ssd_a_log, v_ssd_d, v_ssd_norm, v_w_out, v_ffn_w_up, v_ffn_conv_w, v_ffn_conv_b, v_ffn_w_down):
    given = dict(locals())
    w = {n: given[n] for n in WEIGHTS}
    m = {n: given["m_" + n] for n in WEIGHTS}
    v = {n: given["v_" + n] for n in WEIGHTS}

    def shards(l, group_ids):
        return [_group_pack(GROUPS[gi], lambda n: w[n][l].astype(BF16).reshape(_rows2(n, True)), ()) for gi in group_ids]

    def unpacked(bufs, group_ids):
        big = {}
        for gi, buf in zip(group_ids, bufs):
            for n, piece in _group_unpack(GROUPS[gi], buf).items():
                big[n] = piece.reshape((NDEV,) + SHAPES[n][1:])
        return big

    small_w = {n: w[n] for n, _ in SMALL}
    tabs = _rope_tables(positions[0])
    xv, h, layers, saved = x[0], None, [], []
    att = _all_gather(shards(0, ATT_SIDE), "gather_weights")
    for l in range(DEPTH):
        prefetch = shards(l, FFN_SIDE) + (shards(l + 1, ATT_SIDE) if l + 1 < DEPTH else [])
        nxt = w["norm_mix_pre"][l + 1].reshape(1, D) if l + 1 < DEPTH else None
        xv, sv, p, gathered = _layer_fwd(xv, _prep_layer(unpacked(att, ATT_SIDE), small_w, l), tabs, prefetch,
                                         lambda got: _prep_ffn(unpacked(got[:len(FFN_SIDE)], FFN_SIDE)), h, nxt)
        xv, h = xv if nxt is not None else (xv, None)
        att = gathered[len(FFN_SIDE):]
        layers.append(p)
        saved.append(sv)
    loss, dx = _loss_head(xv, loss_target[0])
    loss = lax.psum(loss[0, 0], ("x", "y", "c"))

    smalls, pending = [None] * DEPTH, None
    recvs = [[None] * len(GROUPS) for _ in range(DEPTH)]
    for l in reversed(range(DEPTH)):
        dx, grads, smalls[l], received = _layer_bwd(dx, saved[l], layers[l], tabs, True, pending)
        for pos, gi in enumerate(FFN_SIDE):
            recvs[l][gi] = received[pos]
        if pending is not None:
            for pos, gi in enumerate(ATT_SIDE):
                recvs[l + 1][gi] = received[len(FFN_SIDE) + pos]
        pending = _pack_grads(grads, ATT_SIDE)
    for gi, buf in zip(ATT_SIDE, _all_to_all(pending, "exchange_grads")):
        recvs[0][gi] = buf
    out = {}
    for gi, g in enumerate(GROUPS):
        per_layer = [_group_unpack(g, recvs[l][gi], padded=True) for l in range(DEPTH)]
        for n in g[1]:
            r2 = _rows2(n)
            rows, cols = _rows2(n, True)
            parts = jnp.concatenate([per_layer[l][n] for l in range(DEPTH)], axis=1)
            parts = parts.reshape(NDEV, DEPTH, -1, cols)[:, :, :rows].reshape((NDEV,) + r2)
            res = _adamw(parts, w[n].reshape(r2), m[n].reshape(r2), v[n].reshape(r2), "adamw_" + n)
            out[n] = [a.reshape(SHAPES[n]) for a in res]

    sflat = jnp.concatenate([jnp.stack([smalls[l][n] for l in range(DEPTH)]).reshape(-1) for n, _ in SMALL])
    sparts = _all_gather([_pack_rows(sflat, (), LANE, 8)], "gather_small_grads")[0]
    pk = lambda d: _pack_rows(jnp.concatenate([d[n].reshape(-1) for n, _ in SMALL]), (), LANE, 8)
    res = _adamw(sparts, pk(w), pk(m), pk(v), "adamw_small")
    off = 0
    for n, width in SMALL:
        out[n] = [a.reshape(-1)[off:off + DEPTH * width].reshape(DEPTH, width) for a in res]
        off += DEPTH * width

    return (loss, dx[None], *[out[n][0] for n in WEIGHTS], *[out[n][1] for n in WEIGHTS],
            *[out[n][2] for n in WEIGHTS], *[out[n][3] for n in WEIGHTS])
```

```python
import functools
import math

import jax
import jax.numpy as jnp
from jax import lax
from jax.experimental import pallas as pl
from jax.experimental.pallas import tpu as pltpu

F32 = jnp.float32
BF16 = jnp.bfloat16

D = 1024
DEPTH = 4
NDEV = 8
HEADS = 8
QL = 256
KVL = 128
ROPE = 32
NOPE = 64
SC = 256
SSD_DIM = 256
SSD_CONV = 768
SSD_H = 4
SSD_L = 128
FFN = 2816
FB = 704
EPS = 1e-6
ROPE_THETA = 10000.0
ATT_SCALE = 96 ** -0.5
LOG2E = 1.4426950408889634
LR, B1, B2, AEPS, WD, STEP = 0.001, 0.9, 0.999, 1e-08, 0.01, 10

PW = 2432
CATW = 1536

ROW_TILE = 512
ROW_CHUNK = 16
NORM_CHUNK = 32
MM_TILE = 1024
ATT_TILE = 256
ATT_QW = 2
FWD_HEADS = 4
BWD_HEADS = 2
HALO = 16
LANE = 128
NEG = -1e30
HI = lax.Precision.HIGHEST
NN = (((1,), (0,)), ((), ()))
NT = (((1,), (1,)), ((), ()))
TN = (((0,), (0,)), ((), ()))
VMEM_LIMIT = 56 * 1024 * 1024

SHARDED = (
    ("w_in", (4, 128, 2212)),
    ("mla_w_q_up", (4, 256, 96)),
    ("mla_w_kv_up", (4, 128, 128)),
    ("sc_conv_w", (4, 3, 32)),
    ("ssd_conv_w", (4, 4, 96)),
    ("w_out", (4, 128, 1024)),
    ("ffn_w_up", (4, 1024, 704)),
    ("ffn_conv_w", (4, 3, 704)),
    ("ffn_w_down", (4, 352, 1024)),
)
SHAPES = dict(SHARDED)
GROUPS = (
    (2212, ("w_in",)),
    (1024, ("w_out",)),
    (704, ("ffn_w_up",)),
    (96, ("mla_w_q_up", "ssd_conv_w", "sc_conv_w")),
    (128, ("mla_w_kv_up",)),
    (1024, ("ffn_w_down",)),
    (704, ("ffn_conv_w",)),
)
ATT_SIDE = (0, 3, 4)
FFN_SIDE = (1, 2, 5, 6)
SMALL = (
    ("norm_mix_pre", 1024), ("norm_mix_post", 1024), ("norm_ffn_pre", 1024), ("norm_ffn_post", 1024),
    ("mla_q_norm", 256), ("mla_kv_norm", 128), ("ssd_conv_b", 768), ("ssd_dt_bias", 4), ("ssd_a_log", 4),
    ("ssd_d", 4), ("ssd_norm", 256), ("ffn_conv_b", 5632),
)
WEIGHTS = ("norm_mix_pre", "norm_mix_post", "norm_ffn_pre", "norm_ffn_post", "w_in", "mla_q_norm", "mla_w_q_up",
           "mla_kv_norm", "mla_w_kv_up", "sc_conv_w", "ssd_conv_w", "ssd_conv_b", "ssd_dt_bias", "ssd_a_log", "ssd_d",
           "ssd_norm", "w_out", "ffn_w_up", "ffn_conv_w", "ffn_conv_b", "ffn_w_down")


def _dot(a, b, dims=NN, precision=None):
    return lax.dot_general(a, b, dims, precision=precision, preferred_element_type=F32)


def _sig(v):
    return 1.0 / (1.0 + jnp.exp(-v))


def _cp(*sem):
    return pltpu.CompilerParams(dimension_semantics=sem, vmem_limit_bytes=VMEM_LIMIT)


def _rowsum(v):
    return jnp.sum(v, axis=0, keepdims=True)


def _prev_halo(i, ts):
    return jnp.maximum(i * (ts // HALO) - 1, 0)


def _next_halo(i, ts, n):
    return jnp.minimum((i + 1) * (ts // HALO), n * (ts // HALO) - 1)


def _gather_plan(x_refs, out_refs, send_sems, recv_sems, local_sems):
    n = len(x_refs)
    x, y, cc = lax.axis_index("x"), lax.axis_index("y"), lax.axis_index("c")
    me, sibling = (x, y, cc), (x, y, 1 - cc)
    chips = [(1 - x, y), (x, 1 - y), (1 - x, 1 - y)]

    def rows(t, px, py, pc):
        return out_refs[t].at[4 * px + 2 * py + pc]

    def copy(t, k, block, to, own=False):
        return pltpu.make_async_remote_copy(
            src_ref=x_refs[t] if own else rows(t, *block), dst_ref=rows(t, *block),
            send_sem=send_sems.at[7 * t + k], recv_sem=recv_sems.at[7 * t + k], device_id=to, device_id_type=pl.DeviceIdType.MESH)

    def local(t):
        return pltpu.make_async_copy(x_refs[t], rows(t, *me), local_sems.at[t])

    def start():
        for t in range(n):
            local(t).start()
            copy(t, 0, me, sibling, own=True).start()
            for j, chip in enumerate(chips):
                copy(t, 1 + j, me, (*chip, cc), own=True).start()

    def finish():
        for j, chip in enumerate(chips):
            for t in range(n):
                copy(t, 1 + j, (*chip, cc), me).wait_recv()
                copy(t, 4 + j, (*chip, cc), sibling).start()
        for t in range(n):
            copy(t, 0, sibling, me).wait_recv()
            for j, chip in enumerate(chips):
                copy(t, 4 + j, (*chip, 1 - cc), me).wait_recv()
        for t in range(n):
            copy(t, 0, me, sibling, own=True).wait_send()
            for j, chip in enumerate(chips):
                copy(t, 1 + j, me, (*chip, cc), own=True).wait_send()
                copy(t, 4 + j, (*chip, cc), sibling).wait_send()
            local(t).wait()

    return start, finish


def _exchange_plan(x_refs, out_refs, send_sems, recv_sems, local_sems):
    n = len(x_refs)
    x, y, cc = lax.axis_index("x"), lax.axis_index("y"), lax.axis_index("c")
    me = 4 * x + 2 * y + cc

    def copies():
        res = [pltpu.make_async_copy(x_refs[t].at[me], out_refs[t].at[me], local_sems.at[t]) for t in range(n)]
        for k in range(1, NDEV):
            px = 1 - x if k & 4 else x
            py = 1 - y if k & 2 else y
            pc = 1 - cc if k & 1 else cc
            peer = 4 * px + 2 * py + pc
            for t in range(n):
                res.append(pltpu.make_async_remote_copy(
                    src_ref=x_refs[t].at[peer], dst_ref=out_refs[t].at[me], send_sem=send_sems.at[7 * t + k - 1],
                    recv_sem=recv_sems.at[7 * t + k - 1], device_id=(px, py, pc), device_id_type=pl.DeviceIdType.MESH))
        return res

    def start():
        for cp in copies():
            cp.start()

    def finish():
        for cp in copies():
            cp.wait()

    return start, finish


def _comm_scratch(n):
    return [pltpu.SemaphoreType.DMA((7 * n,)), pltpu.SemaphoreType.DMA((7 * n,)), pltpu.SemaphoreType.DMA((n,))]


ANY = pl.BlockSpec(memory_space=pl.ANY)


def _all_gather(xs, name):
    n = len(xs)

    def body(*refs):
        start, finish = _gather_plan(refs[:n], refs[n:2 * n], *refs[2 * n:])
        start()
        finish()

    return pl.pallas_call(
        body, name=name, out_shape=[jax.ShapeDtypeStruct((NDEV,) + a.shape, a.dtype) for a in xs],
        in_specs=[ANY] * n, out_specs=[ANY] * n, scratch_shapes=_comm_scratch(n),
    )(*xs)


def _all_to_all(xs, name):
    n = len(xs)

    def body(*refs):
        start, finish = _exchange_plan(refs[:n], refs[n:2 * n], *refs[2 * n:])
        start()
        finish()

    return pl.pallas_call(
        body, name=name, out_shape=[jax.ShapeDtypeStruct(a.shape, a.dtype) for a in xs],
        in_specs=[ANY] * n, out_specs=[ANY] * n, scratch_shapes=_comm_scratch(n),
    )(*xs)


def _mm(name, a, b, out_shape, grid, a_spec, b_spec, o_spec, dims, acc_shape):
    nk = grid[2]

    def single(a_ref, b_ref, o_ref):
        o_ref[...] = _dot(a_ref[...], b_ref[...], dims).astype(o_ref.dtype)

    if nk == 1:
        return pl.pallas_call(
            single, name=name, grid=grid, out_shape=out_shape, in_specs=[a_spec, b_spec], out_specs=o_spec,
            compiler_params=_cp("parallel", "parallel", "arbitrary"),
        )(a, b)

    def body(a_ref, b_ref, o_ref, acc_ref):
        k = pl.program_id(2)

        @pl.when(k == 0)
        def _():
            acc_ref[...] = jnp.zeros_like(acc_ref)

        acc_ref[...] += _dot(a_ref[...], b_ref[...], dims)

        @pl.when(k == nk - 1)
        def _():
            o_ref[...] = acc_ref[...].astype(o_ref.dtype)

    return pl.pallas_call(
        body, name=name, grid=grid, out_shape=out_shape, in_specs=[a_spec, b_spec], out_specs=o_spec,
        scratch_shapes=[pltpu.VMEM(acc_shape, F32)], compiler_params=_cp("parallel", "parallel", "arbitrary"),
    )(a, b)


def _mm_rows(name, a, w, out_dtype, dims):
    s, k = a.shape
    n = w.shape[1] if dims == NN else w.shape[0]
    tm = min(MM_TILE, s)
    return _mm(name, a, w, jax.ShapeDtypeStruct((s, n), out_dtype), (s // tm, 1, 1),
               pl.BlockSpec((tm, k), lambda i, j, kk: (i, 0)), pl.BlockSpec(w.shape, lambda i, j, kk: (0, 0)),
               pl.BlockSpec((tm, n), lambda i, j, kk: (i, 0)), dims, (tm, n))


def _mm_wgrad(name, a, g, out_dtype):
    s, m = a.shape
    n = g.shape[1]
    tk = min(MM_TILE, s)
    return _mm(name, a, g, jax.ShapeDtypeStruct((m, n), out_dtype), (1, 1, s // tk),
               pl.BlockSpec((tk, m), lambda i, j, kk: (kk, 0)), pl.BlockSpec((tk, n), lambda i, j, kk: (kk, 0)),
               pl.BlockSpec((m, n), lambda i, j, kk: (0, 0)), TN, (m, n))


def _mm_up(h2, wup):
    s = h2.shape[0]
    tm = min(MM_TILE, s)
    return _mm("ffn_up", h2, wup, jax.ShapeDtypeStruct((NDEV, s, FB), BF16), (NDEV, s // tm, 1),
               pl.BlockSpec((tm, D), lambda j, i, kk: (i, 0)), pl.BlockSpec((None, D, FB), lambda j, i, kk: (j, 0, 0)),
               pl.BlockSpec((None, tm, FB), lambda j, i, kk: (j, i, 0)), NN, (tm, FB))


def _mm_down(act, wdown):
    s = act.shape[1]
    tm = min(MM_TILE, s)
    return _mm("ffn_down", act, wdown, jax.ShapeDtypeStruct((s, D), BF16), (s // tm, 1, 4),
               pl.BlockSpec((None, tm, FB), lambda i, j, kk: (kk, i, 0)), pl.BlockSpec((None, FB, D), lambda i, j, kk: (kk, 0, 0)),
               pl.BlockSpec((tm, D), lambda i, j, kk: (i, 0)), NN, (tm, D))


def _mm_dact(df, wdown):
    s = df.shape[0]
    tm = min(MM_TILE, s)
    return _mm("ffn_dact", df, wdown, jax.ShapeDtypeStruct((4, s, FB), BF16), (4, s // tm, 1),
               pl.BlockSpec((tm, D), lambda j, i, kk: (i, 0)), pl.BlockSpec((None, FB, D), lambda j, i, kk: (j, 0, 0)),
               pl.BlockSpec((None, tm, FB), lambda j, i, kk: (j, i, 0)), NT, (tm, FB))


def _mm_dwdown(act, df):
    s = df.shape[0]
    tk = min(MM_TILE, s)
    return _mm("ffn_dwdown", act, df, jax.ShapeDtypeStruct((4, FB, D), BF16), (4, 1, s // tk),
               pl.BlockSpec((None, tk, FB), lambda j, i, kk: (j, kk, 0)), pl.BlockSpec((tk, D), lambda j, i, kk: (kk, 0)),
               pl.BlockSpec((None, FB, D), lambda j, i, kk: (j, 0, 0)), TN, (FB, D))


def _mm_dh2(dupre, wup):
    s = dupre.shape[1]
    tm = min(MM_TILE, s)
    return _mm("ffn_dh2", dupre, wup, jax.ShapeDtypeStruct((s, D), BF16), (s // tm, 1, NDEV),
               pl.BlockSpec((None, tm, FB), lambda i, j, kk: (kk, i, 0)), pl.BlockSpec((None, D, FB), lambda i, j, kk: (kk, 0, 0)),
               pl.BlockSpec((tm, D), lambda i, j, kk: (i, 0)), NT, (tm, D))


def _mm_dwup(h2, dupre):
    s = h2.shape[0]
    tk = min(MM_TILE, s)
    return _mm("ffn_dwup", h2, dupre, jax.ShapeDtypeStruct((NDEV, D, FB), BF16), (NDEV, 1, s // tk),
               pl.BlockSpec((tk, D), lambda j, i, kk: (kk, 0)), pl.BlockSpec((None, tk, FB), lambda j, i, kk: (j, kk, 0)),
               pl.BlockSpec((None, D, FB), lambda j, i, kk: (j, 0, 0)), TN, (D, FB))


def _rms(xv, w, out_dtype, name):
    s, d = xv.shape
    ts = min(ROW_TILE, s)

    def body(x_ref, w_ref, o_ref):
        for r0 in range(0, ts, NORM_CHUNK):
            rows = pl.ds(r0, NORM_CHUNK)
            xf = x_ref[rows, :].astype(F32)
            r = lax.rsqrt(jnp.mean(xf * xf, axis=-1, keepdims=True) + EPS)
            o_ref[rows, :] = (xf * r * w_ref[...]).astype(o_ref.dtype)

    return pl.pallas_call(
        body, name=name, grid=(s // ts,), out_shape=jax.ShapeDtypeStruct((s, d), out_dtype),
        in_specs=[pl.BlockSpec((ts, d), lambda i: (i, 0)), pl.BlockSpec((1, d), lambda i: (0, 0))],
        out_specs=pl.BlockSpec((ts, d), lambda i: (i, 0)), compiler_params=_cp("parallel"),
    )(xv, w)


def _add_rms(xv, mv, w, name, w_next=None):
    s, d = xv.shape
    ts = min(ROW_TILE, s)
    both = w_next is not None

    def body(*refs):
        x_ref, m_ref, w_ref = refs[:3]
        o_ref = refs[4] if both else refs[3]
        for r0 in range(0, ts, NORM_CHUNK):
            rows = pl.ds(r0, NORM_CHUNK)
            mf = m_ref[rows, :].astype(F32)
            r = lax.rsqrt(jnp.mean(mf * mf, axis=-1, keepdims=True) + EPS)
            y = x_ref[rows, :] + mf * r * w_ref[...]
            o_ref[rows, :] = y
            if both:
                r2 = lax.rsqrt(jnp.mean(y * y, axis=-1, keepdims=True) + EPS)
                refs[5][rows, :] = (y * r2 * refs[3][...]).astype(BF16)

    row = pl.BlockSpec((ts, d), lambda i: (i, 0))
    vec = pl.BlockSpec((1, d), lambda i: (0, 0))
    if both:
        return pl.pallas_call(
            body, name=name + "_rms", grid=(s // ts,),
            out_shape=(jax.ShapeDtypeStruct((s, d), F32), jax.ShapeDtypeStruct((s, d), BF16)),
            in_specs=[row, row, vec, vec], out_specs=(row, row), compiler_params=_cp("parallel"),
        )(xv, mv, w, w_next)
    return pl.pallas_call(
        body, name=name, grid=(s // ts,), out_shape=jax.ShapeDtypeStruct((s, d), F32),
        in_specs=[row, row, vec], out_specs=row, compiler_params=_cp("parallel"),
    )(xv, mv, w)


def _rms_bwd_math(xf, w, dy):
    r = lax.rsqrt(jnp.mean(xf * xf, axis=-1, keepdims=True) + EPS)
    xh = xf * r
    dxh = dy * w
    dx = r * (dxh - xh * jnp.mean(dxh * xh, axis=-1, keepdims=True))
    return dx, dy * xh


def _rms_bwd(xv, w, dy, dres, out_dtype, name):
    s, d = xv.shape
    ts = min(ROW_TILE, s)
    with_res = dres is not None

    def body(*refs):
        if with_res:
            x_ref, w_ref, dy_ref, dres_ref, dx_ref, dw_ref, acc = refs
        else:
            x_ref, w_ref, dy_ref, dx_ref, dw_ref, acc = refs
        acc[...] = jnp.zeros_like(acc)
        for r0 in range(0, ts, NORM_CHUNK):
            rows = pl.ds(r0, NORM_CHUNK)
            dx, dwt = _rms_bwd_math(x_ref[rows, :].astype(F32), w_ref[...], dy_ref[rows, :].astype(F32))
            if with_res:
                dx = dx + dres_ref[rows, :]
            dx_ref[rows, :] = dx.astype(dx_ref.dtype)
            acc[...] += dwt

        @pl.when(pl.program_id(0) == 0)
        def _():
            dw_ref[...] = jnp.zeros_like(dw_ref)

        dw_ref[...] += _rowsum(acc[...])

    row = pl.BlockSpec((ts, d), lambda i: (i, 0))
    vec = pl.BlockSpec((1, d), lambda i: (0, 0))
    ins = [xv, w, dy] + ([dres] if with_res else [])
    return pl.pallas_call(
        body, name=name, grid=(s // ts,),
        out_shape=(jax.ShapeDtypeStruct((s, d), out_dtype), jax.ShapeDtypeStruct((1, d), F32)),
        in_specs=[row, vec, row] + ([row] if with_res else []), out_specs=(row, vec),
        scratch_shapes=[pltpu.VMEM((NORM_CHUNK, d), F32)], compiler_params=_cp("arbitrary"),
    )(*ins)


def _loss_head(yv, tv):
    s, d = yv.shape
    ts = min(ROW_TILE, s)

    def body(y_ref, t_ref, l_ref, dy_ref):
        e = y_ref[...] - t_ref[...]
        dy_ref[...] = e * (1.0 / d)

        @pl.when(pl.program_id(0) == 0)
        def _():
            l_ref[...] = jnp.zeros_like(l_ref)

        tot = jnp.sum(jnp.sum(e * e, axis=1, keepdims=True), axis=0, keepdims=True)
        l_ref[...] += jnp.broadcast_to(tot * (0.5 / d), (8, LANE))

    row = pl.BlockSpec((ts, d), lambda i: (i, 0))
    return pl.pallas_call(
        body, name="loss_head", grid=(s // ts,),
        out_shape=(jax.ShapeDtypeStruct((8, LANE), F32), jax.ShapeDtypeStruct((s, d), F32)),
        in_specs=[row, row], out_specs=(pl.BlockSpec((8, LANE), lambda i: (0, 0)), row), compiler_params=_cp("arbitrary"),
    )(yv, tv)


def _rope(v, c, a, b):
    return v * c + pltpu.roll(v, LANE - 16, 1) * a + pltpu.roll(v, 16, 1) * b


def _rope_t(dv, c, a, b):
    return dv * c + pltpu.roll(dv * a, 16, 1) + pltpu.roll(dv * b, LANE - 16, 1)


def _mla_prep(proj, tabs, qnw, kvnw, wq, wkv):
    s = proj.shape[0]
    ts = min(ROW_TILE, s)
    tc, ta, tb = tabs

    def body(cq_ref, ckv_ref, kr_ref, c_ref, a_ref, b_ref, qnw_ref, kvnw_ref, wq_ref, wkv_ref, q_ref, k_ref, kv_ref):
        c, a, b = c_ref[...], a_ref[...], b_ref[...]
        cq = cq_ref[...].astype(F32)
        qn = (cq * lax.rsqrt(jnp.mean(cq * cq, axis=-1, keepdims=True) + EPS) * qnw_ref[...]).astype(BF16)
        ckv = ckv_ref[...].astype(F32)
        kvn = (ckv * lax.rsqrt(jnp.mean(ckv * ckv, axis=-1, keepdims=True) + EPS) * kvnw_ref[...]).astype(BF16)
        kr = _rope(kr_ref[...].astype(F32), c, a, b)
        lane = lax.broadcasted_iota(jnp.int32, (ts, LANE), 1)
        for h in range(HEADS):
            q_ref[h] = _rope(_dot(qn, wq_ref[h]), c, a, b).astype(BF16)
            kv = _dot(kvn, wkv_ref[h])
            kv_ref[h] = kv.astype(BF16)
            k_ref[h] = jnp.where(lane < NOPE, kv, kr).astype(BF16)

    tab = pl.BlockSpec((ts, LANE), lambda i: (i, 0))
    hd = pl.BlockSpec((HEADS, ts, LANE), lambda i: (0, i, 0))
    out = jax.ShapeDtypeStruct((HEADS, s, LANE), BF16)
    return pl.pallas_call(
        body, name="mla_prep", grid=(s // ts,), out_shape=(out, out, out),
        in_specs=[pl.BlockSpec((ts, QL), lambda i: (i, 0)), pl.BlockSpec((ts, LANE), lambda i: (i, 2)),
                  pl.BlockSpec((ts, LANE), lambda i: (i, 3)), tab, tab, tab,
                  pl.BlockSpec((1, QL), lambda i: (0, 0)), pl.BlockSpec((1, KVL), lambda i: (0, 0)),
                  pl.BlockSpec((HEADS, QL, LANE), lambda i: (0, 0, 0)), pl.BlockSpec((HEADS, KVL, LANE), lambda i: (0, 0, 0))],
        out_specs=(hd, hd, hd), compiler_params=_cp("parallel"),
    )(proj, proj, proj, tc, ta, tb, qnw, kvnw, wq, wkv)


def _mla_prep_bwd(proj, tabs, qnw, kvnw, wq, wkv, dq, dk, dv):
    s = proj.shape[0]
    ts = min(ROW_TILE, s)
    tc, ta, tb = tabs

    def body(cq_ref, ckv_ref, c_ref, a_ref, b_ref, qnw_ref, kvnw_ref, wq_ref, wkv_ref, dq_ref, dk_ref, dv_ref,
             dcq_ref, dckv_ref, dkr_ref, dwq_ref, dwkv_ref, dqnw_ref, dkvnw_ref):
        @pl.when(pl.program_id(0) == 0)
        def _():
            dwq_ref[...] = jnp.zeros_like(dwq_ref)
            dwkv_ref[...] = jnp.zeros_like(dwkv_ref)
            dqnw_ref[...] = jnp.zeros_like(dqnw_ref)
            dkvnw_ref[...] = jnp.zeros_like(dkvnw_ref)

        c, a, b = c_ref[...], a_ref[...], b_ref[...]
        cq = cq_ref[...].astype(F32)
        qn = (cq * lax.rsqrt(jnp.mean(cq * cq, axis=-1, keepdims=True) + EPS) * qnw_ref[...]).astype(BF16)
        ckv = ckv_ref[...].astype(F32)
        kvn = (ckv * lax.rsqrt(jnp.mean(ckv * ckv, axis=-1, keepdims=True) + EPS) * kvnw_ref[...]).astype(BF16)
        lane = lax.broadcasted_iota(jnp.int32, (ts, LANE), 1)
        dqn = jnp.zeros((ts, QL), F32)
        dkvn = jnp.zeros((ts, KVL), F32)
        dkr = jnp.zeros((ts, LANE), F32)
        for h in range(HEADS):
            dqh = _rope_t(dq_ref[h], c, a, b).astype(BF16)
            dwq_ref[h] += _dot(qn, dqh, TN)
            dqn += _dot(dqh, wq_ref[h], NT)
            dkh = dk_ref[h].astype(F32)
            dkvh = jnp.where(lane < NOPE, dkh, dv_ref[h].astype(F32)).astype(BF16)
            dkr += jnp.where(lane < NOPE, 0.0, dkh)
            dwkv_ref[h] += _dot(kvn, dkvh, TN)
            dkvn += _dot(dkvh, wkv_ref[h], NT)
        dkr_ref[...] = _rope_t(dkr, c, a, b).astype(BF16)
        dcq, dwt = _rms_bwd_math(cq, qnw_ref[...], dqn)
        dcq_ref[...] = dcq.astype(BF16)
        dqnw_ref[...] += _rowsum(dwt)
        dckv, dwt = _rms_bwd_math(ckv, kvnw_ref[...], dkvn)
        dckv_ref[...] = dckv.astype(BF16)
        dkvnw_ref[...] += _rowsum(dwt)

    tab = pl.BlockSpec((ts, LANE), lambda i: (i, 0))
    hd = pl.BlockSpec((HEADS, ts, LANE), lambda i: (0, i, 0))
    wq_spec = pl.BlockSpec((HEADS, QL, LANE), lambda i: (0, 0, 0))
    wkv_spec = pl.BlockSpec((HEADS, KVL, LANE), lambda i: (0, 0, 0))
    return pl.pallas_call(
        body, name="mla_prep_bwd", grid=(s // ts,),
        out_shape=(jax.ShapeDtypeStruct((s, QL), BF16), jax.ShapeDtypeStruct((s, KVL), BF16), jax.ShapeDtypeStruct((s, LANE), BF16),
                   jax.ShapeDtypeStruct((HEADS, QL, LANE), F32), jax.ShapeDtypeStruct((HEADS, KVL, LANE), F32),
                   jax.ShapeDtypeStruct((1, QL), F32), jax.ShapeDtypeStruct((1, KVL), F32)),
        in_specs=[pl.BlockSpec((ts, QL), lambda i: (i, 0)), pl.BlockSpec((ts, LANE), lambda i: (i, 2)), tab, tab, tab,
                  pl.BlockSpec((1, QL), lambda i: (0, 0)), pl.BlockSpec((1, KVL), lambda i: (0, 0)), wq_spec, wkv_spec, hd, hd, hd],
        out_specs=(pl.BlockSpec((ts, QL), lambda i: (i, 0)), pl.BlockSpec((ts, KVL), lambda i: (i, 0)), tab, wq_spec, wkv_spec,
                   pl.BlockSpec((1, QL), lambda i: (0, 0)), pl.BlockSpec((1, KVL), lambda i: (0, 0))),
        compiler_params=_cp("arbitrary"),
    )(proj, proj, tc, ta, tb, qnw, kvnw, wq, wkv, dq, dk, dv)


def _transpose_bf16(v):
    return v.astype(F32).T.astype(BF16)


def _flash_fwd(q, k, kv, prefetch=None):
    s = q.shape[1]
    t = min(ATT_TILE, s)
    tq = ATT_QW * t
    n = s // tq
    g = FWD_HEADS
    nx = len(prefetch) if prefetch else 0

    def body(*refs):
        q_ref, k_ref, kv_ref = refs[:3]
        o_ref, lse_ref = refs[3 + nx:5 + nx]
        kvt_sc = refs[5 + 2 * nx]
        step = pl.program_id(0) * n + pl.program_id(1)
        if nx:
            start, finish = _gather_plan(refs[3:3 + nx], refs[5 + nx:5 + 2 * nx], *refs[6 + 2 * nx:])
            pl.when(step == 0)(start)
        attend(q_ref, k_ref, kv_ref, o_ref, lse_ref, kvt_sc)
        if nx:
            pl.when(step == (HEADS // g) * n - 1)(finish)

    def attend(q_ref, k_ref, kv_ref, o_ref, lse_ref, kvt_sc):
        i = pl.program_id(1)

        @pl.when(i == 0)
        def _():
            ones_rows = lax.broadcasted_iota(jnp.int32, (LANE, s), 0) < NOPE
            for hh in range(g):
                kvt_sc[hh] = jnp.where(ones_rows, 1.0, kv_ref[hh].astype(F32).T).astype(BF16)

        qt = [(q_ref[hh].astype(F32) * (ATT_SCALE * LOG2E)).T.astype(BF16) for hh in range(g)]
        kpos = lax.broadcasted_iota(jnp.int32, (t, tq), 0)
        qpos = lax.broadcasted_iota(jnp.int32, (t, tq), 1) + i * tq

        def chunk(j, carry, diagonal):
            start = pl.multiple_of(j * t, t)
            scs = [_dot(k_ref[hh, pl.ds(start, t), :], qt[hh]) for hh in range(g)]
            soft = []
            for hh in range(g):
                sc = scs[hh]
                if diagonal:
                    sc = jnp.where(qpos >= kpos + start, sc, NEG)
                m_new = jnp.maximum(carry[hh][0], jnp.max(sc, axis=0, keepdims=True))
                soft.append((m_new, jnp.exp2(carry[hh][0] - m_new), jnp.exp2(sc - m_new).astype(BF16)))
            pvs = [_dot(kvt_sc[hh, :, pl.ds(start, t)], soft[hh][2]) for hh in range(g)]
            return tuple((soft[hh][0], soft[hh][1] * carry[hh][1] + pvs[hh]) for hh in range(g))

        init = tuple((jnp.full((1, tq), NEG, F32), jnp.zeros((LANE, tq), F32)) for _ in range(g))
        carry = lax.fori_loop(0, ATT_QW * i, lambda j, c: chunk(j, c, False), init)
        for d in range(ATT_QW):
            carry = chunk(ATT_QW * i + d, carry, True)
        for hh in range(g):
            m, acc = carry[hh]
            l = acc[0:1, :]
            o_ref[:, hh * LANE:(hh + 1) * LANE] = (acc / l).T.astype(BF16)
            lse_ref[hh] = m + jnp.log2(l)

    whole = pl.BlockSpec((g, s, LANE), lambda h, i: (h, 0, 0))
    res = pl.pallas_call(
        body, name="flash_fwd_gather" if nx else "flash_fwd", grid=(HEADS // g, n),
        out_shape=[jax.ShapeDtypeStruct((s, HEADS * LANE), BF16), jax.ShapeDtypeStruct((HEADS, 1, s), F32)]
        + [jax.ShapeDtypeStruct((NDEV,) + a.shape, a.dtype) for a in (prefetch or [])],
        in_specs=[pl.BlockSpec((g, tq, LANE), lambda h, i: (h, i, 0)), whole, whole] + [ANY] * nx,
        out_specs=[pl.BlockSpec((tq, g * LANE), lambda h, i: (i, h)), pl.BlockSpec((g, 1, tq), lambda h, i: (h, 0, i))] + [ANY] * nx,
        scratch_shapes=[pltpu.VMEM((g, LANE, s), BF16)] + (_comm_scratch(nx) if nx else []),
        compiler_params=_cp("arbitrary", "arbitrary"),
    )(q, k, kv, *(prefetch or []))
    return res[0], res[1], list(res[2:])


def _flash_bwd(q, k, kv, cat, dcat, lse, pending=None):
    s = q.shape[1]
    t = min(ATT_TILE, s)
    tq = ATT_QW * t
    n = s // t
    g = BWD_HEADS
    nx = len(pending) if pending else 0

    def body(*refs):
        ins, outs, scr = refs[:6], refs[6 + nx:9 + nx], refs[9 + 2 * nx:14 + 2 * nx]
        step = pl.program_id(0) * n + pl.program_id(1)
        if nx:
            start, finish = _exchange_plan(refs[6:6 + nx], refs[9 + nx:9 + 2 * nx], *refs[14 + 2 * nx:])
            pl.when(step == 0)(start)
        attend(*ins, *outs, *scr)
        if nx:
            pl.when(step == (HEADS // g) * n - 1)(finish)

    def attend(q_ref, k_ref, kv_ref, o_ref, do_ref, lse_ref, dq_ref, dk_ref, dv_ref, qt_sc, dot_sc, delta_sc, dqt_sc, qs_sc):
        j = pl.program_id(1)

        @pl.when(j == 0)
        def _():
            for hh in range(g):
                lanes = slice(hh * LANE, (hh + 1) * LANE)
                qf = q_ref[hh].astype(F32)
                qt_sc[hh] = (qf * (ATT_SCALE * LOG2E)).T.astype(BF16)
                qs_sc[hh] = (qf * ATT_SCALE).astype(BF16)
                dof = do_ref[:, lanes].astype(F32)
                dot_sc[hh] = dof.T.astype(BF16)
                delta_sc[hh] = _dot(jnp.ones((8, LANE), F32), dof * o_ref[:, lanes].astype(F32), NT, precision=HI)
            dqt_sc[...] = jnp.zeros_like(dqt_sc)

        kjt = [_transpose_bf16(k_ref[hh]) for hh in range(g)]
        kpos = lax.broadcasted_iota(jnp.int32, (t, tq), 0) + j * t
        qpos = lax.broadcasted_iota(jnp.int32, (t, tq), 1)

        def chunk(i, carry, diagonal):
            start = pl.multiple_of(i * tq, tq)
            cols = pl.ds(start, tq)
            scs = [_dot(k_ref[hh], qt_sc[hh, :, cols]) for hh in range(g)]
            dps = [_dot(kv_ref[hh], dot_sc[hh, :, cols]) for hh in range(g)]
            pds = []
            for hh in range(g):
                p = jnp.exp2(scs[hh] - lse_ref[hh, :, cols])
                if diagonal:
                    p = jnp.where(qpos + start >= kpos, p, 0.0)
                ds = (p * (dps[hh] - delta_sc[hh, 0:1, cols])).astype(BF16)
                pds.append((p.astype(BF16), ds))
            out = []
            for hh in range(g):
                dk, dv = carry[hh]
                dv = dv + _dot(pds[hh][0], do_ref[pl.ds(start, tq), hh * LANE:(hh + 1) * LANE])
                dk = dk + _dot(pds[hh][1], qs_sc[hh, pl.ds(start, tq), :])
                dqt_sc[hh, :, cols] += _dot(kjt[hh], pds[hh][1])
                out.append((dk, dv))
            return tuple(out)

        zero = jnp.zeros((t, LANE), F32)
        first = j // ATT_QW
        carry = chunk(first, tuple((zero, zero) for _ in range(g)), True)
        carry = lax.fori_loop(first + 1, s // tq, lambda i, c: chunk(i, c, False), carry)
        for hh in range(g):
            dk_ref[hh] = carry[hh][0].astype(BF16)
            dv_ref[hh] = carry[hh][1].astype(BF16)

        @pl.when(j == n - 1)
        def _():
            for hh in range(g):
                dq_ref[hh] = (dqt_sc[hh] * ATT_SCALE).T

    whole = pl.BlockSpec((g, s, LANE), lambda h, j: (h, 0, 0))
    kspec = pl.BlockSpec((g, t, LANE), lambda h, j: (h, j, 0))
    ospec = pl.BlockSpec((s, g * LANE), lambda h, j: (0, h))
    res = pl.pallas_call(
        body, name="flash_bwd_exchange" if nx else "flash_bwd", grid=(HEADS // g, n),
        out_shape=[jax.ShapeDtypeStruct((HEADS, s, LANE), F32), jax.ShapeDtypeStruct((HEADS, s, LANE), BF16),
                   jax.ShapeDtypeStruct((HEADS, s, LANE), BF16)] + [jax.ShapeDtypeStruct(a.shape, a.dtype) for a in (pending or [])],
        in_specs=[whole, kspec, kspec, ospec, ospec, pl.BlockSpec((g, 1, s), lambda h, j: (h, 0, 0))] + [ANY] * nx,
        out_specs=[whole, kspec, kspec] + [ANY] * nx,
        scratch_shapes=[pltpu.VMEM((g, LANE, s), BF16), pltpu.VMEM((g, LANE, s), BF16), pltpu.VMEM((g, 8, s), F32),
                        pltpu.VMEM((g, LANE, s), F32), pltpu.VMEM((g, s, LANE), BF16)] + (_comm_scratch(nx) if nx else []),
        compiler_params=_cp("arbitrary", "arbitrary"),
    )(q, k, kv, cat, dcat, lse, *(pending or []))
    return res[0], res[1], res[2], list(res[3:])


def _conv3(ext, w_ref, ts):
    return (w_ref[0:1, :] * ext[pl.ds(HALO - 2, ts), :] + w_ref[1:2, :] * ext[pl.ds(HALO - 1, ts), :]
            + w_ref[2:3, :] * ext[pl.ds(HALO, ts), :])


def _conv3_rows(ext, w_ref, r):
    return (w_ref[0:1, :] * ext[pl.ds(HALO - 2 + r, ROW_CHUNK), :] + w_ref[1:2, :] * ext[pl.ds(HALO - 1 + r, ROW_CHUNK), :]
            + w_ref[2:3, :] * ext[pl.ds(HALO + r, ROW_CHUNK), :])


def _conv3_t(ext2, w_ref, ts):
    return (w_ref[0:1, :] * ext2[pl.ds(2, ts), :] + w_ref[1:2, :] * ext2[pl.ds(1, ts), :] + w_ref[2:3, :] * ext2[pl.ds(0, ts), :])


def _sconv_fwd(proj, w):
    s = proj.shape[0]
    ts = min(ROW_TILE, s)

    def body(b_ref, c_ref, h_ref, hc_ref, hh_ref, w_ref, o_ref, ext):
        i = pl.program_id(0)
        ext[0:HALO, :] = hc_ref[...].astype(F32) * hh_ref[...].astype(F32) * (i > 0).astype(F32)
        ext[HALO:HALO + ts, :] = c_ref[...].astype(F32) * h_ref[...].astype(F32)
        o_ref[...] = (b_ref[...].astype(F32) * _conv3(ext, w_ref, ts)).astype(BF16)

    def col(cb):
        return pl.BlockSpec((ts, SC), lambda i: (i, cb))

    def halo(cb):
        return pl.BlockSpec((HALO, SC), lambda i: (_prev_halo(i, ts), cb))

    return pl.pallas_call(
        body, name="sconv_fwd", grid=(s // ts,), out_shape=jax.ShapeDtypeStruct((s, SC), BF16),
        in_specs=[col(2), col(3), col(4), halo(3), halo(4), pl.BlockSpec((3, SC), lambda i: (0, 0))],
        out_specs=pl.BlockSpec((ts, SC), lambda i: (i, 0)), scratch_shapes=[pltpu.VMEM((ts + HALO, SC), F32)],
        compiler_params=_cp("parallel"),
    )(proj, proj, proj, proj, proj, w)


def _sconv_bwd(proj, dcat, w):
    s = proj.shape[0]
    ts = min(ROW_TILE, s)
    n = s // ts

    def body(b_ref, c_ref, h_ref, hc_ref, hh_ref, dy_ref, ndy_ref, nb_ref, w_ref, db_ref, dc_ref, dh_ref, dw_ref, ext, ext2):
        i = pl.program_id(0)

        @pl.when(i == 0)
        def _():
            dw_ref[...] = jnp.zeros_like(dw_ref)

        cv, hv, bv = c_ref[...].astype(F32), h_ref[...].astype(F32), b_ref[...].astype(F32)
        ext[0:HALO, :] = hc_ref[...].astype(F32) * hh_ref[...].astype(F32) * (i > 0).astype(F32)
        ext[HALO:HALO + ts, :] = cv * hv
        dy = dy_ref[...].astype(F32)
        db_ref[...] = (dy * _conv3(ext, w_ref, ts)).astype(BF16)
        dyb = dy * bv
        ext2[0:ts, :] = dyb
        ext2[ts:ts + HALO, :] = ndy_ref[...].astype(F32) * nb_ref[...].astype(F32) * (i < n - 1).astype(F32)
        dg = _conv3_t(ext2, w_ref, ts)
        dc_ref[...] = (dg * hv).astype(BF16)
        dh_ref[...] = (dg * cv).astype(BF16)
        for kk in range(3):
            dw_ref[kk:kk + 1, :] += _rowsum(dyb * ext[pl.ds(HALO - 2 + kk, ts), :])

    def col(cb):
        return pl.BlockSpec((ts, SC), lambda i: (i, cb))

    def halo(cb):
        return pl.BlockSpec((HALO, SC), lambda i: (_prev_halo(i, ts), cb))

    def nxt(cb):
        return pl.BlockSpec((HALO, SC), lambda i: (_next_halo(i, ts, n), cb))

    out = jax.ShapeDtypeStruct((s, SC), BF16)
    o0 = pl.BlockSpec((ts, SC), lambda i: (i, 0))
    return pl.pallas_call(
        body, name="sconv_bwd", grid=(n,), out_shape=(out, out, out, jax.ShapeDtypeStruct((3, SC), F32)),
        in_specs=[col(2), col(3), col(4), halo(3), halo(4), col(4), nxt(4), nxt(2), pl.BlockSpec((3, SC), lambda i: (0, 0))],
        out_specs=(o0, o0, o0, pl.BlockSpec((3, SC), lambda i: (0, 0))),
        scratch_shapes=[pltpu.VMEM((ts + HALO, SC), F32), pltpu.VMEM((ts + HALO, SC), F32)], compiler_params=_cp("arbitrary"),
    )(proj, proj, proj, proj, proj, dcat, dcat, proj, w)


def _ffn_stage(ext, u_ref, halo_ref, i, ts):
    ext[0:HALO, :] = halo_ref[...].astype(F32) * (i > 0).astype(F32)
    ext[HALO:HALO + ts, :] = u_ref[...].astype(F32)


def _ffn_specs(ts):
    cur = pl.BlockSpec((2, None, ts, FB), lambda j, i: (0, j, i, 0))
    halo = pl.BlockSpec((2, None, HALO, FB), lambda j, i: (0, j, _prev_halo(i, ts), 0))
    w = pl.BlockSpec((2, None, 3, FB), lambda j, i: (0, j, 0, 0))
    b = pl.BlockSpec((2, None, 1, FB), lambda j, i: (0, j, 0, 0))
    return cur, halo, w, b


def _ffn_act(upre, fcw, fcb):
    s = upre.shape[1]
    ts = min(ROW_TILE, s)

    def body(u_ref, halo_ref, w_ref, b_ref, o_ref, ext_g, ext_u):
        i = pl.program_id(1)
        _ffn_stage(ext_g, u_ref.at[0], halo_ref.at[0], i, ts)
        _ffn_stage(ext_u, u_ref.at[1], halo_ref.at[1], i, ts)
        for r in range(0, ts, ROW_CHUNK):
            gate = b_ref[0] + _conv3_rows(ext_g, w_ref.at[0], r)
            up = b_ref[1] + _conv3_rows(ext_u, w_ref.at[1], r)
            o_ref[pl.ds(r, ROW_CHUNK), :] = (gate * _sig(gate) * up).astype(BF16)

    cur, halo, w, b = _ffn_specs(ts)
    u4 = upre.reshape(2, 4, s, FB)
    return pl.pallas_call(
        body, name="ffn_act", grid=(4, s // ts), out_shape=jax.ShapeDtypeStruct((4, s, FB), BF16),
        in_specs=[cur, halo, w, b], out_specs=pl.BlockSpec((None, ts, FB), lambda j, i: (j, i, 0)),
        scratch_shapes=[pltpu.VMEM((ts + HALO, FB), F32), pltpu.VMEM((ts + HALO, FB), F32)], compiler_params=_cp("parallel", "parallel"),
    )(u4, u4, fcw.reshape(2, 4, 3, FB), fcb.reshape(2, 4, 1, FB))


def _ffn_bwd(upre, dact, fcw, fcb):
    s = upre.shape[1]
    ts = min(ROW_TILE, s)
    n = s // ts
    te = ts + HALO

    def body(u_ref, halo_ref, nxt_ref, w_ref, b_ref, da_ref, nda_ref, dup_ref, db_ref, dw_ref, ext_g, ext_u, ext_da, du_g, du_u, acc):
        i = pl.program_id(1)

        @pl.when(i == 0)
        def _():
            db_ref[...] = jnp.zeros_like(db_ref)
            dw_ref[...] = jnp.zeros_like(dw_ref)

        more = (i < n - 1).astype(F32)
        for idx, ext in ((0, ext_g), (1, ext_u)):
            _ffn_stage(ext, u_ref.at[idx], halo_ref.at[idx], i, ts)
            ext[HALO + ts:HALO + te, :] = nxt_ref[idx].astype(F32) * more
        ext_da[0:ts, :] = da_ref[...].astype(F32)
        ext_da[ts:te, :] = nda_ref[...].astype(F32) * more
        acc[...] = jnp.zeros_like(acc)
        for r in range(0, te, ROW_CHUNK):
            rows = pl.ds(r, ROW_CHUNK)
            gate = b_ref[0] + _conv3_rows(ext_g, w_ref.at[0], r)
            up = b_ref[1] + _conv3_rows(ext_u, w_ref.at[1], r)
            sg = _sig(gate)
            da = ext_da[rows, :]
            dgate = da * up * sg * (1.0 + gate * (1.0 - sg))
            dup = da * gate * sg
            du_g[rows, :] = dgate
            du_u[rows, :] = dup
            if r < ts:
                acc[0] += dgate
                acc[1] += dup
        for r in range(0, ts, ROW_CHUNK):
            for idx, du, ext in ((0, du_g, ext_g), (1, du_u, ext_u)):
                d0 = du[pl.ds(r, ROW_CHUNK), :]
                dupre = (w_ref[idx, 2:3, :] * d0 + w_ref[idx, 1:2, :] * du[pl.ds(r + 1, ROW_CHUNK), :]
                         + w_ref[idx, 0:1, :] * du[pl.ds(r + 2, ROW_CHUNK), :])
                dup_ref[idx, pl.ds(r, ROW_CHUNK), :] = dupre.astype(BF16)
                for kk in range(3):
                    acc[2 + 3 * idx + kk] += d0 * ext[pl.ds(HALO - 2 + kk + r, ROW_CHUNK), :]
        for idx in range(2):
            db_ref[idx] += _rowsum(acc[idx])
            for kk in range(3):
                dw_ref[idx, kk:kk + 1, :] += _rowsum(acc[2 + 3 * idx + kk])

    cur, halo, w, b = _ffn_specs(ts)
    nxt = pl.BlockSpec((2, None, HALO, FB), lambda j, i: (0, j, _next_halo(i, ts, n), 0))
    u4 = upre.reshape(2, 4, s, FB)
    dupre, db, dw = pl.pallas_call(
        body, name="ffn_bwd", grid=(4, n),
        out_shape=(jax.ShapeDtypeStruct((2, 4, s, FB), BF16), jax.ShapeDtypeStruct((2, 4, 1, FB), F32),
                   jax.ShapeDtypeStruct((2, 4, 3, FB), F32)),
        in_specs=[cur, halo, nxt, w, b, pl.BlockSpec((None, ts, FB), lambda j, i: (j, i, 0)),
                  pl.BlockSpec((None, HALO, FB), lambda j, i: (j, _next_halo(i, ts, n), 0))],
        out_specs=(cur, b, w),
        scratch_shapes=[pltpu.VMEM((te + HALO, FB), F32), pltpu.VMEM((te + HALO, FB), F32), pltpu.VMEM((te, FB), F32),
                        pltpu.VMEM((te, FB), F32), pltpu.VMEM((te, FB), F32), pltpu.VMEM((8, ROW_CHUNK, FB), F32)],
        compiler_params=_cp("parallel", "arbitrary"),
    )(u4, u4, u4, fcw.reshape(2, 4, 3, FB), fcb.reshape(2, 4, 1, FB), dact, dact)
    return dupre.reshape(NDEV, s, FB), db.reshape(NDEV, 1, FB), dw.reshape(NDEV, 3, FB)


def _softplus(v):
    e = jnp.exp(-jnp.abs(v))
    return jnp.maximum(v, 0.0) + jnp.where(e < 1e-4, e * (1.0 - 0.5 * e), jnp.log(1.0 + e))


def _ssd_consts():
    L = SSD_L
    r = lax.broadcasted_iota(jnp.int32, (L, L), 0)
    c = lax.broadcasted_iota(jnp.int32, (L, L), 1)
    tri = r >= c
    er = lax.broadcasted_iota(jnp.int32, (LANE, SSD_DIM), 0)
    ec = lax.broadcasted_iota(jnp.int32, (LANE, SSD_DIM), 1)
    expand = ((ec >= er * 64) & (ec < er * 64 + 64)).astype(F32)
    return tri, expand


def _ssd_conv4(ext, cw_ref, cb_ref):
    L = SSD_L
    pre = cb_ref[...] + cw_ref[0:1, :] * ext[pl.ds(HALO - 3, L), :]
    for kk in range(1, 4):
        pre = pre + cw_ref[kk:kk + 1, :] * ext[pl.ds(HALO - 3 + kk, L), :]
    return pre


def _ssd_common(xbc_ref, halo_ref, dt_ref, cw_ref, cb_ref, dtb_ref, alog_ref, ext, first):
    L = SSD_L
    tri, expand = _ssd_consts()
    ext[0:HALO, :] = halo_ref[...].astype(F32) * (1.0 - first.astype(F32))
    ext[HALO:HALO + L, :] = xbc_ref[...].astype(F32)
    pre = _ssd_conv4(ext, cw_ref, cb_ref)
    sg = _sig(pre)
    act = pre * sg
    lane = lax.broadcasted_iota(jnp.int32, (1, LANE), 1)
    m4 = lane < SSD_H
    raw = dt_ref[...].astype(F32) + dtb_ref[...]
    dtv = jnp.where(m4, _softplus(raw), 0.0)
    av = jnp.where(m4, -jnp.exp(alog_ref[...]), 0.0)
    adt = dtv * av
    acs = _dot(tri.astype(F32), adt, precision=HI)
    acs_b = _dot(acs, expand, precision=HI)
    dt_b = _dot(dtv, expand, precision=HI)
    return dict(tri=tri, expand=expand, pre=pre, sg=sg, act=act, raw=raw, dtv=dtv, av=av, m4=m4, acs=acs, acs_b=acs_b,
                dt_b=dt_b, lane=lane)


def _head_terms(cm, h):
    L = SSD_L
    acs, tri = cm["acs"], cm["tri"]
    lane_l = lax.broadcasted_iota(jnp.int32, (L, LANE), 1)
    sub_l = lax.broadcasted_iota(jnp.int32, (LANE, L), 0)
    col = jnp.sum(jnp.where(lane_l == h, acs, 0.0), axis=1, keepdims=True)
    row = jnp.sum(jnp.where(sub_l == h, acs.T, 0.0), axis=0, keepdims=True)
    dec = jnp.where(tri, jnp.exp(jnp.where(tri, col - row, NEG)), 0.0)
    rowi = lax.broadcasted_iota(jnp.int32, (L, 1), 0)
    last = jnp.sum(jnp.where(rowi == L - 1, col, 0.0), axis=0, keepdims=True)
    dte = jnp.exp(last - col)
    return col, dec, last, dte


def _ssd_fwd(proj, cw, cb, dtb, alog, dvec, nw):
    s = proj.shape[0]
    L = SSD_L
    nc = s // L

    def body(z_ref, xbc_ref, halo_ref, dt_ref, cw_ref, cb_ref, dtb_ref, alog_ref, d_ref, nw_ref, y_ref, ypre_ref, st_ref, ext, state):
        i = pl.program_id(0)

        @pl.when(i == 0)
        def _():
            state[...] = jnp.zeros_like(state)

        cm = _ssd_common(xbc_ref, halo_ref, dt_ref, cw_ref, cb_ref, dtb_ref, alog_ref, ext, i == 0)
        act = cm["act"]
        xs = act[:, 0:256]
        bm = (act[:, 256:384], act[:, 384:512])
        cmat = (act[:, 512:640].astype(BF16), act[:, 640:768].astype(BF16))
        xdt = xs * cm["dt_b"]
        prev = state[...]
        st_ref[...] = prev
        prev_bf = prev.astype(BF16)
        gm = [_dot(cmat[g], bm[g].astype(BF16), NT) for g in range(2)]
        lane2 = lax.broadcasted_iota(jnp.int32, (1, SSD_DIM), 1)
        rows2 = lax.broadcasted_iota(jnp.int32, (SSD_DIM, 1), 0)
        ydiag = jnp.zeros((L, SSD_DIM), F32)
        contrib = jnp.zeros((SSD_DIM, LANE), F32)
        cd_rows = jnp.zeros((SSD_DIM, 1), F32)
        for h in range(SSD_H):
            g = h // 2
            col, dec, last, dte = _head_terms(cm, h)
            mh = (lane2 >= 64 * h) & (lane2 < 64 * h + 64)
            xm = jnp.where(mh, xdt, 0.0).astype(BF16)
            ydiag += _dot((gm[g] * dec).astype(BF16), xm)
            contrib += _dot(xm, (bm[g] * dte).astype(BF16), TN)
            cd_rows += jnp.where((rows2 >= 64 * h) & (rows2 < 64 * h + 64), jnp.exp(last), 0.0)
        yo = jnp.where(lane2 < 128, _dot(cmat[0], prev_bf, NT), _dot(cmat[1], prev_bf, NT))
        y = ydiag + yo * jnp.exp(cm["acs_b"]) + xs * d_ref[...]
        state[...] = prev * cd_rows + contrib
        ypre_ref[...] = y
        zz = z_ref[...].astype(F32)
        gt = y * zz * _sig(zz)
        y_ref[...] = (gt * lax.rsqrt(jnp.mean(gt * gt, axis=-1, keepdims=True) + EPS) * nw_ref[...]).astype(BF16)

    def vec(w):
        return pl.BlockSpec((1, w), lambda i: (0, 0))

    return pl.pallas_call(
        body, name="ssd_fwd", grid=(nc,),
        out_shape=(jax.ShapeDtypeStruct((s, SSD_DIM), BF16), jax.ShapeDtypeStruct((s, SSD_DIM), F32),
                   jax.ShapeDtypeStruct((nc, SSD_DIM, LANE), F32)),
        in_specs=[pl.BlockSpec((L, SSD_DIM), lambda i: (i, 5)), pl.BlockSpec((L, SSD_CONV), lambda i: (i, 2)),
                  pl.BlockSpec((HALO, SSD_CONV), lambda i: (_prev_halo(i, L), 2)), pl.BlockSpec((L, LANE), lambda i: (i, 18)),
                  pl.BlockSpec((4, SSD_CONV), lambda i: (0, 0)), vec(SSD_CONV), vec(LANE), vec(LANE), vec(SSD_DIM), vec(SSD_DIM)],
        out_specs=(pl.BlockSpec((L, SSD_DIM), lambda i: (i, 0)), pl.BlockSpec((L, SSD_DIM), lambda i: (i, 0)),
                   pl.BlockSpec((None, SSD_DIM, LANE), lambda i: (i, 0, 0))),
        scratch_shapes=[pltpu.VMEM((L + HALO, SSD_CONV), F32), pltpu.VMEM((SSD_DIM, LANE), F32)], compiler_params=_cp("arbitrary"),
    )(proj, proj, proj, proj, cw, cb, dtb, alog, dvec, nw)


def _ssd_bwd(proj, dcat, ypre, states, cw, cb, dtb, alog, dvec, nw):
    s = proj.shape[0]
    L = SSD_L
    nc = s // L

    def body(z_ref, xbc_ref, halo_ref, dt_ref, dy_ref, ypre_ref, st_ref, cw_ref, cb_ref, dtb_ref, alog_ref, d_ref, nw_ref,
             dz_ref, dxbc_ref, ddt_ref, dcw_ref, dcb_ref, ddtb_ref, dalog_ref, dd_ref, dnw_ref, ext, ext2, carry, dstate, ddl):
        i = pl.program_id(0)
        r = nc - 1 - i

        @pl.when(i == 0)
        def _():
            for ref in (dcw_ref, dcb_ref, ddtb_ref, dalog_ref, dd_ref, dnw_ref, carry, dstate, ddl):
                ref[...] = jnp.zeros_like(ref)

        cm = _ssd_common(xbc_ref, halo_ref, dt_ref, cw_ref, cb_ref, dtb_ref, alog_ref, ext, r == 0)
        tri, expand, act = cm["tri"], cm["expand"], cm["act"]
        xs = act[:, 0:256]
        bm = (act[:, 256:384], act[:, 384:512])
        cmat = (act[:, 512:640], act[:, 640:768])
        bm_bf = [v.astype(BF16) for v in bm]
        cm_bf = [v.astype(BF16) for v in cmat]
        dt_b = cm["dt_b"]
        xdt = xs * dt_b
        xdt_bf = xdt.astype(BF16)
        ea_b = jnp.exp(cm["acs_b"])
        prev = st_ref[...]
        prev_bf = prev.astype(BF16)
        lane2 = lax.broadcasted_iota(jnp.int32, (1, SSD_DIM), 1)
        rows2 = lax.broadcasted_iota(jnp.int32, (SSD_DIM, 1), 0)
        lane_l = lax.broadcasted_iota(jnp.int32, (L, LANE), 1)
        rowi = lax.broadcasted_iota(jnp.int32, (L, 1), 0)

        y = ypre_ref[...]
        zz = z_ref[...].astype(F32)
        sz = _sig(zz)
        gt = y * zz * sz
        dgt, dwt = _rms_bwd_math(gt, nw_ref[...], dy_ref[...].astype(F32))
        dnw_ref[...] += _rowsum(dwt)
        dy = dgt * zz * sz
        dz_ref[...] = (dgt * y * sz * (1.0 + zz * (1.0 - sz))).astype(BF16)

        ddl[0:1, :] += _rowsum(dy * xs)
        dxs = dy * d_ref[...]

        yo = jnp.where(lane2 < 128, _dot(cm_bf[0], prev_bf, NT), _dot(cm_bf[1], prev_bf, NT))
        dacs_b = dy * yo * ea_b
        dyo = dy * ea_b
        dyo_g = (jnp.where(lane2 < 128, dyo, 0.0).astype(BF16), jnp.where(lane2 >= 128, dyo, 0.0).astype(BF16))
        dc = [_dot(dyo_g[g], prev_bf) for g in range(2)]
        dprev = _dot(dyo_g[0], cm_bf[0], TN) + _dot(dyo_g[1], cm_bf[1], TN)

        gm = [_dot(cm_bf[g], bm_bf[g], NT) for g in range(2)]
        dgm = [jnp.zeros((L, L), F32), jnp.zeros((L, L), F32)]
        db = [jnp.zeros((L, LANE), F32), jnp.zeros((L, LANE), F32)]
        dxdt = jnp.zeros((L, SSD_DIM), F32)
        dacs = jnp.zeros((L, LANE), F32)
        dlast = jnp.zeros((1, LANE), F32)
        cd_rows = jnp.zeros((SSD_DIM, 1), F32)
        dst = dstate[...]
        dst_bf = dst.astype(BF16)
        dsp = dst * prev
        ones = jnp.ones((L, LANE), F32)
        for h in range(SSD_H):
            g = h // 2
            col, dec, last, dte = _head_terms(cm, h)
            mh = (lane2 >= 64 * h) & (lane2 < 64 * h + 64)
            rh = (rows2 >= 64 * h) & (rows2 < 64 * h + 64)
            sc = gm[g] * dec
            xm = jnp.where(mh, xdt, 0.0).astype(BF16)
            dym = jnp.where(mh, dy, 0.0).astype(BF16)
            dsc = _dot(dym, xdt_bf, NT)
            dxdt += _dot(sc.astype(BF16), dym, TN)
            dgm[g] += dsc * dec
            dd = dsc * sc
            rs = jnp.sum(dd, axis=1, keepdims=True)
            cs = _dot(dd, ones, TN, precision=HI)
            dacs += jnp.where(lane_l == h, rs - cs, 0.0)
            bd = (bm[g] * dte).astype(BF16)
            dxdt += jnp.where(mh, _dot(bd, dst_bf, NT), 0.0)
            dbd = _dot(xm, dst_bf)
            db[g] += dbd * dte
            tt = jnp.sum(dbd * bm[g], axis=1, keepdims=True) * dte
            dacs += jnp.where(lane_l == h, -tt, 0.0)
            cdh = jnp.exp(last)
            dcd = jnp.sum(jnp.sum(jnp.where(rh, dsp, 0.0), axis=1, keepdims=True), axis=0, keepdims=True)
            dlast += jnp.where(cm["lane"] == h, jnp.sum(tt, axis=0, keepdims=True) + dcd * cdh, 0.0)
            cd_rows += jnp.where(rh, cdh, 0.0)
        dacs += jnp.where(rowi == L - 1, dlast, 0.0)
        dacs += _dot(dacs_b, expand, NT, precision=HI)
        dstate[...] = dprev + dst * cd_rows

        for g in range(2):
            dgb = dgm[g].astype(BF16)
            dc[g] += _dot(dgb, bm_bf[g])
            db[g] += _dot(dgb, cm_bf[g], TN)

        dadt = _dot(tri.astype(F32), dacs, TN, precision=HI)
        ddtv = dadt * cm["av"] + _dot(dxdt * xs, expand, NT, precision=HI)
        dalog_ref[...] += _rowsum(dadt * cm["dtv"]) * cm["av"]
        dxs += dxdt * dt_b
        draw = jnp.where(cm["m4"], ddtv * _sig(cm["raw"]), 0.0)
        ddtb_ref[...] += _rowsum(draw)
        ddt_ref[...] = draw.astype(BF16)

        dact = jnp.concatenate([dxs, db[0], db[1], dc[0], dc[1]], axis=1)
        sg, pre = cm["sg"], cm["pre"]
        dpre = dact * sg * (1.0 + pre * (1.0 - sg))
        dcb_ref[...] += _rowsum(dpre)
        for kk in range(4):
            dcw_ref[kk:kk + 1, :] += _rowsum(dpre * ext[pl.ds(HALO - 3 + kk, L), :])
        ext2[0:L, :] = dpre
        ext2[L:L + HALO, :] = carry[...]
        dx = cw_ref[3:4, :] * ext2[pl.ds(0, L), :]
        for kk in range(3):
            dx = dx + cw_ref[kk:kk + 1, :] * ext2[pl.ds(3 - kk, L), :]
        dxbc_ref[...] = dx.astype(BF16)
        carry[...] = dpre[0:HALO, :]

        @pl.when(i == nc - 1)
        def _():
            dd_ref[...] = _dot(ddl[...], expand, NT, precision=HI)

    def vec(w):
        return pl.BlockSpec((1, w), lambda i: (0, 0))

    def rv(i):
        return nc - 1 - i

    return pl.pallas_call(
        body, name="ssd_bwd", grid=(nc,),
        out_shape=(jax.ShapeDtypeStruct((s, SSD_DIM), BF16), jax.ShapeDtypeStruct((s, SSD_CONV), BF16), jax.ShapeDtypeStruct((s, LANE), BF16),
                   jax.ShapeDtypeStruct((4, SSD_CONV), F32), jax.ShapeDtypeStruct((1, SSD_CONV), F32), jax.ShapeDtypeStruct((1, LANE), F32),
                   jax.ShapeDtypeStruct((1, LANE), F32), jax.ShapeDtypeStruct((8, LANE), F32), jax.ShapeDtypeStruct((1, SSD_DIM), F32)),
        in_specs=[pl.BlockSpec((L, SSD_DIM), lambda i: (rv(i), 5)), pl.BlockSpec((L, SSD_CONV), lambda i: (rv(i), 2)),
                  pl.BlockSpec((HALO, SSD_CONV), lambda i: (_prev_halo(rv(i), L), 2)), pl.BlockSpec((L, LANE), lambda i: (rv(i), 18)),
                  pl.BlockSpec((L, SSD_DIM), lambda i: (rv(i), 5)), pl.BlockSpec((L, SSD_DIM), lambda i: (rv(i), 0)),
                  pl.BlockSpec((None, SSD_DIM, LANE), lambda i: (rv(i), 0, 0)),
                  pl.BlockSpec((4, SSD_CONV), lambda i: (0, 0)), vec(SSD_CONV), vec(LANE), vec(LANE), vec(SSD_DIM), vec(SSD_DIM)],
        out_specs=(pl.BlockSpec((L, SSD_DIM), lambda i: (rv(i), 0)), pl.BlockSpec((L, SSD_CONV), lambda i: (rv(i), 0)),
                   pl.BlockSpec((L, LANE), lambda i: (rv(i), 0)), pl.BlockSpec((4, SSD_CONV), lambda i: (0, 0)), vec(SSD_CONV),
                   vec(LANE), vec(LANE), pl.BlockSpec((8, LANE), lambda i: (0, 0)), vec(SSD_DIM)),
        scratch_shapes=[pltpu.VMEM((L + HALO, SSD_CONV), F32), pltpu.VMEM((L + HALO, SSD_CONV), F32), pltpu.VMEM((HALO, SSD_CONV), F32),
                        pltpu.VMEM((SSD_DIM, LANE), F32), pltpu.VMEM((8, SSD_DIM), F32)],
        compiler_params=_cp("arbitrary"),
    )(proj, proj, proj, proj, dcat, ypre, states, cw, cb, dtb, alog, dvec, nw)


def _adamw(parts, w, m, v, name):
    r, c = w.shape
    tr = r
    for cand in (256, 128, 64, 32, 16, 8):
        if r % cand == 0 and (cand * c * 4) <= 2 * 1024 * 1024:
            tr = cand
            break
    c1 = 1.0 - B1 ** STEP
    c2 = 1.0 - B2 ** STEP

    def body(p_ref, w_ref, m_ref, v_ref, g_ref, d_ref, nm_ref, nv_ref):
        g = p_ref[0].astype(F32)
        for dev in range(1, NDEV):
            g = g + p_ref[dev].astype(F32)
        mn = B1 * m_ref[...] + (1.0 - B1) * g
        vn = B2 * v_ref[...] + (1.0 - B2) * (g * g)
        g_ref[...] = g
        nm_ref[...] = mn
        nv_ref[...] = vn
        d_ref[...] = -LR * ((mn / c1) / (jnp.sqrt(vn / c2) + AEPS) + WD * w_ref[...])

    blk = pl.BlockSpec((tr, c), lambda i: (i, 0))
    out = jax.ShapeDtypeStruct((r, c), F32)
    return pl.pallas_call(
        body, name=name, grid=(r // tr,), out_shape=(out, out, out, out),
        in_specs=[pl.BlockSpec((NDEV, tr, c), lambda i: (0, i, 0)), blk, blk, blk], out_specs=(blk, blk, blk, blk),
        compiler_params=_cp("parallel"),
    )(parts, w, m, v)


def _pad_win(w):
    z = lambda n: jnp.zeros((w.shape[0], n), w.dtype)
    return jnp.concatenate([w[:, :384], z(64), w[:, 384:416], z(32), w[:, 416:], z(124)], axis=1)


def _unpad_win(g):
    return jnp.concatenate([g[:, :384], g[:, 448:480], g[:, 512:2308]], axis=1)


def _pad_wout(w):
    att = jnp.pad(w[:512].reshape(HEADS, 64, D), ((0, 0), (64, 0), (0, 0))).reshape(HEADS * LANE, D)
    return jnp.concatenate([att, w[512:]], axis=0)


def _unpad_wout(g):
    att = g[:HEADS * LANE].reshape(HEADS, LANE, D)[:, 64:, :].reshape(512, D)
    return jnp.concatenate([att, g[HEADS * LANE:]], axis=0)


def _lanes(v, n=LANE):
    return jnp.pad(v, (0, n - v.shape[0])).reshape(1, n)


def _prep_ffn(big):
    return {"wout": _pad_wout(big["w_out"].reshape(1024, D)), "wup": big["ffn_w_up"], "fcw": big["ffn_conv_w"].astype(F32),
            "wdown": big["ffn_w_down"].reshape(4, FB, D)}


def _prep_layer(big, small, l):
    p = _prep_ffn(big) if "w_out" in big else {}
    p["win"] = _pad_win(big["w_in"].reshape(D, 2212))
    p["wq"] = jnp.pad(big["mla_w_q_up"], ((0, 0), (0, 0), (0, LANE - 96)))
    p["wkv"] = big["mla_w_kv_up"]
    p["scw"] = big["sc_conv_w"].astype(F32).transpose(1, 0, 2).reshape(3, SC)
    p["ssdcw"] = big["ssd_conv_w"].astype(F32).transpose(1, 0, 2).reshape(4, SSD_CONV)
    for nm in ("norm_mix_pre", "norm_mix_post", "norm_ffn_pre", "norm_ffn_post", "mla_q_norm", "mla_kv_norm", "ssd_conv_b", "ssd_norm"):
        p[nm] = small[nm][l].reshape(1, -1)
    p["dtb"] = _lanes(small["ssd_dt_bias"][l])
    p["alog"] = _lanes(small["ssd_a_log"][l])
    p["dvec"] = jnp.repeat(small["ssd_d"][l], 64).reshape(1, SSD_DIM)
    p["fcb"] = small["ffn_conv_b"][l].reshape(NDEV, 1, FB)
    return p


def _rope_tables(positions):
    inv_freq = 1.0 / (ROPE_THETA ** (jnp.arange(0, ROPE, 2, dtype=F32) / ROPE))
    ang = positions.astype(F32)[:, None] * inv_freq
    cos, sin = jnp.cos(ang), jnp.sin(ang)
    s = positions.shape[0]
    z = lambda n: jnp.zeros((s, n), F32)
    tc = jnp.concatenate([jnp.ones((s, 64), F32), cos, cos, z(32)], axis=1)
    ta = jnp.concatenate([z(64), -sin, z(48)], axis=1)
    tb = jnp.concatenate([z(80), sin, z(32)], axis=1)
    return tc, ta, tb


def _layer_fwd(xv, p, tabs, prefetch=None, prep_rest=None, h=None, next_norm=None):
    if h is None:
        h = _rms(xv, p["norm_mix_pre"], BF16, "rms_pre")
    proj = _mm_rows("in_proj", h, p["win"], BF16, NN)
    q, k, kv = _mla_prep(proj, tabs, p["mla_q_norm"], p["mla_kv_norm"], p["wq"], p["wkv"])
    o, lse, gathered = _flash_fwd(q, k, kv, prefetch)
    if prep_rest is not None:
        p = {**p, **prep_rest(gathered)}
    yconv = _sconv_fwd(proj, p["scw"])
    yssd, ypre, states = _ssd_fwd(proj, p["ssdcw"], p["ssd_conv_b"], p["dtb"], p["alog"], p["dvec"], p["ssd_norm"])
    cat = jnp.concatenate([o, yconv, yssd], axis=1)
    mixed = _mm_rows("out_proj", cat, p["wout"], BF16, NN)
    x1, h2 = _add_rms(xv, mixed, p["norm_mix_post"], "add_rms", p["norm_ffn_pre"])
    upre = _mm_up(h2, p["wup"])
    act = _ffn_act(upre, p["fcw"], p["fcb"])
    f = _mm_down(act, p["wdown"])
    x2 = _add_rms(x1, f, p["norm_ffn_post"], "add_rms", next_norm)
    saved = dict(x=xv, h=h, proj=proj, q=q, k=k, kv=kv, lse=lse, ypre=ypre, states=states, cat=cat, mixed=mixed, x1=x1, h2=h2,
                 upre=upre, act=act, f=f)
    return x2, saved, p, gathered


def _pack_grads(grads, group_ids):
    return [_group_pack(GROUPS[gi], lambda n: grads[n].reshape((NDEV,) + _rows2(n, True)), (NDEV,)) for gi in group_ids]


def _layer_bwd(dx2, sv, p, tabs, exchange=False, pending=None):
    df, g_nfpo = _rms_bwd(sv["f"], p["norm_ffn_post"], dx2, None, BF16, "rms_bwd_post")
    dact = _mm_dact(df, p["wdown"])
    g_wdown = _mm_dwdown(sv["act"], df)
    dupre, g_fcb, g_fcw = _ffn_bwd(sv["upre"], dact, p["fcw"], p["fcb"])
    dh2 = _mm_dh2(dupre, p["wup"])
    g_wup = _mm_dwup(sv["h2"], dupre)
    dx1, g_nfp = _rms_bwd(sv["x1"], p["norm_ffn_pre"], dh2, dx2, F32, "rms_bwd_pre")
    dmixed, g_nmpo = _rms_bwd(sv["mixed"], p["norm_mix_post"], dx1, None, BF16, "rms_bwd_post")
    dcat = _mm_rows("dcat", dmixed, p["wout"], BF16, NT)
    g_wout = _mm_wgrad("dw_out", sv["cat"], dmixed, BF16)
    big = {
        "w_out": _unpad_wout(g_wout).reshape(NDEV, 128, D),
        "ffn_w_up": g_wup,
        "ffn_conv_w": g_fcw.astype(BF16),
        "ffn_w_down": g_wdown.reshape(NDEV, 352, D),
    }
    outgoing = _pack_grads(big, FFN_SIDE) + (pending or []) if exchange else None
    dq, dk, dv, received = _flash_bwd(sv["q"], sv["k"], sv["kv"], sv["cat"], dcat, sv["lse"], outgoing)
    dcq, dckv, dkr, g_wq, g_wkv, g_qn, g_kvn = _mla_prep_bwd(sv["proj"], tabs, p["mla_q_norm"], p["mla_kv_norm"], p["wq"], p["wkv"], dq, dk, dv)
    dscb, dscc, dsch, g_scw = _sconv_bwd(sv["proj"], dcat, p["scw"])
    dz, dxbc, ddt, g_cw, g_cb, g_dtb, g_alog, g_d, g_nw = _ssd_bwd(
        sv["proj"], dcat, sv["ypre"], sv["states"], p["ssdcw"], p["ssd_conv_b"], p["dtb"], p["alog"], p["dvec"], p["ssd_norm"])
    dproj = jnp.concatenate([dcq, dckv, dkr, dscb, dscc, dsch, dz, dxbc, ddt], axis=1)
    dh = _mm_rows("dh", dproj, p["win"], BF16, NT)
    g_win = _mm_wgrad("dw_in", sv["h"], dproj, BF16)
    dx, g_nmp = _rms_bwd(sv["x"], p["norm_mix_pre"], dh, dx1, F32, "rms_bwd_pre")
    big.update({
        "w_in": _unpad_win(g_win).reshape(NDEV, 128, 2212),
        "mla_w_q_up": g_wq[:, :, :96].astype(BF16),
        "mla_w_kv_up": g_wkv.astype(BF16),
        "sc_conv_w": g_scw.reshape(3, NDEV, 32).transpose(1, 0, 2).astype(BF16),
        "ssd_conv_w": g_cw.reshape(4, NDEV, 96).transpose(1, 0, 2).astype(BF16),
    })
    small = {
        "norm_mix_pre": g_nmp[0], "norm_mix_post": g_nmpo[0], "norm_ffn_pre": g_nfp[0], "norm_ffn_post": g_nfpo[0],
        "mla_q_norm": g_qn[0], "mla_kv_norm": g_kvn[0], "ssd_conv_b": g_cb[0], "ssd_dt_bias": g_dtb[0, :SSD_H],
        "ssd_a_log": g_alog[0, :SSD_H], "ssd_d": g_d[0, :SSD_H], "ssd_norm": g_nw[0], "ffn_conv_b": g_fcb.reshape(-1),
    }
    return dx, big, small, received


def _local_step(xv, positions, target, layers):
    tabs = _rope_tables(positions)
    saved = []
    for p in layers:
        xv, sv, _, _ = _layer_fwd(xv, p, tabs)
        saved.append(sv)
    loss, dx = _loss_head(xv, target)
    bigs, smalls = [None] * DEPTH, [None] * DEPTH
    for l in reversed(range(len(layers))):
        dx, bigs[l], smalls[l], _ = _layer_bwd(dx, saved[l], layers[l], tabs)
    return loss[0, 0], dx, bigs, smalls


def _pack_rows(flat, lead, width, mult):
    n = flat.shape[-1]
    rows = -(-n // (width * mult)) * mult
    pad = [(0, 0)] * (flat.ndim - 1) + [(0, rows * width - n)]
    return jnp.pad(flat, pad).reshape(lead + (rows, width))


def _rows2(n, layer=False):
    shape = SHAPES[n][1:] if layer else SHAPES[n]
    return (math.prod(shape[:-1]), shape[-1])


def _group_pack(group, get, lead):
    width, names = group
    pieces = []
    for n in names:
        rows, cols = _rows2(n, True)
        pad = [(0, 0)] * len(lead) + [(0, -rows % 16), (0, width - cols)]
        pieces.append(jnp.pad(get(n), pad))
    return pieces[0] if len(pieces) == 1 else jnp.concatenate(pieces, axis=len(lead))


def _group_unpack(group, buf, padded=False):
    _, names = group
    res, off = {}, 0
    for n in names:
        rows, cols = _rows2(n, True)
        rp = rows + (-rows % 16)
        res[n] = buf[:, off:off + (rp if padded else rows), :cols]
        off += rp
    return res


def kernel(x, positions, norm_mix_pre, norm_mix_post, norm_ffn_pre, norm_ffn_post, w_in, mla_q_norm, mla_w_q_up, mla_kv_norm, mla_w_kv_up, sc_conv_w, ssd_conv_w, ssd_conv_b, ssd_dt_bias, ssd_a_log, ssd_d, ssd_norm, w_out, ffn_w_up, ffn_conv_w, ffn_conv_b, ffn_w_down, loss_target, m_norm_mix_pre, m_norm_mix_post, m_norm_ffn_pre, m_norm_ffn_post, m_w_in, m_mla_q_norm, m_mla_w_q_up, m_mla_kv_norm, m_mla_w_kv_up, m_sc_conv_w, m_ssd_conv_w, m_ssd_conv_b, m_ssd_dt_bias, m_ssd_a_log, m_ssd_d, m_ssd_norm, m_w_out, m_ffn_w_up, m_ffn_conv_w, m_ffn_conv_b, m_ffn_w_down, v_norm_mix_pre, v_norm_mix_post, v_norm_ffn_pre, v_norm_ffn_post, v_w_in, v_mla_q_norm, v_mla_w_q_up, v_mla_kv_norm, v_mla_w_kv_up, v_sc_conv_w, v_ssd_conv_w, v_ssd_conv_b, v_ssd_dt_bias, v_ssd_a_log, v_ssd_d, v_ssd_norm, v_w_out, v_ffn_w_up, v_ffn_conv_w, v_ffn_conv_b, v_ffn_w_down):
    given = dict(locals())
    w = {n: given[n] for n in WEIGHTS}
    m = {n: given["m_" + n] for n in WEIGHTS}
    v = {n: given["v_" + n] for n in WEIGHTS}

    def shards(l, group_ids):
        return [_group_pack(GROUPS[gi], lambda n: w[n][l].astype(BF16).reshape(_rows2(n, True)), ()) for gi in group_ids]

    def unpacked(bufs, group_ids):
        big = {}
        for gi, buf in zip(group_ids, bufs):
            for n, piece in _group_unpack(GROUPS[gi], buf).items():
                big[n] = piece.reshape((NDEV,) + SHAPES[n][1:])
        return big

    small_w = {n: w[n] for n, _ in SMALL}
    tabs = _rope_tables(positions[0])
    xv, h, layers, saved = x[0], None, [], []
    att = _all_gather(shards(0, ATT_SIDE), "gather_weights")
    for l in range(DEPTH):
        prefetch = shards(l, FFN_SIDE) + (shards(l + 1, ATT_SIDE) if l + 1 < DEPTH else [])
        nxt = w["norm_mix_pre"][l + 1].reshape(1, D) if l + 1 < DEPTH else None
        xv, sv, p, gathered = _layer_fwd(xv, _prep_layer(unpacked(att, ATT_SIDE), small_w, l), tabs, prefetch,
                                         lambda got: _prep_ffn(unpacked(got[:len(FFN_SIDE)], FFN_SIDE)), h, nxt)
        xv, h = xv if nxt is not None else (xv, None)
        att = gathered[len(FFN_SIDE):]
        layers.append(p)
        saved.append(sv)
    loss, dx = _loss_head(xv, loss_target[0])
    loss = lax.psum(loss[0, 0], ("x", "y", "c"))

    smalls, pending = [None] * DEPTH, None
    recvs = [[None] * len(GROUPS) for _ in range(DEPTH)]
    for l in reversed(range(DEPTH)):
        dx, grads, smalls[l], received = _layer_bwd(dx, saved[l], layers[l], tabs, True, pending)
        for pos, gi in enumerate(FFN_SIDE):
            recvs[l][gi] = received[pos]
        if pending is not None:
            for pos, gi in enumerate(ATT_SIDE):
                recvs[l + 1][gi] = received[len(FFN_SIDE) + pos]
        pending = _pack_grads(grads, ATT_SIDE)
    for gi, buf in zip(ATT_SIDE, _all_to_all(pending, "exchange_grads")):
        recvs[0][gi] = buf
    out = {}
    for gi, g in enumerate(GROUPS):
        per_layer = [_group_unpack(g, recvs[l][gi], padded=True) for l in range(DEPTH)]
        for n in g[1]:
            r2 = _rows2(n)
            rows, cols = _rows2(n, True)
            parts = jnp.concatenate([per_layer[l][n] for l in range(DEPTH)], axis=1)
            parts = parts.reshape(NDEV, DEPTH, -1, cols)[:, :, :rows].reshape((NDEV,) + r2)
            res = _adamw(parts, w[n].reshape(r2), m[n].reshape(r2), v[n].reshape(r2), "adamw_" + n)
            out[n] = [a.reshape(SHAPES[n]) for a in res]

    sflat = jnp.concatenate([jnp.stack([smalls[l][n] for l in range(DEPTH)]).reshape(-1) for n, _ in SMALL])
    sparts = _all_gather([_pack_rows(sflat, (), LANE, 8)], "gather_small_grads")[0]
    pk = lambda d: _pack_rows(jnp.concatenate([d[n].reshape(-1) for n, _ in SMALL]), (), LANE, 8)
    res = _adamw(sparts, pk(w), pk(m), pk(v), "adamw_small")
    off = 0
    for n, width in SMALL:
        out[n] = [a.reshape(-1)[off:off + DEPTH * width].reshape(DEPTH, width) for a in res]
        off += DEPTH * width

    return (loss, dx[None], *[out[n][0] for n in WEIGHTS], *[out[n][1] for n in WEIGHTS],
            *[out[n][2] for n in WEIGHTS], *[out[n][3] for n in WEIGHTS])
```

```python
import functools
import math

import jax
import jax.numpy as jnp
from jax import lax
from jax.experimental import pallas as pl
from jax.experimental.pallas import tpu as pltpu

F32 = jnp.float32
BF16 = jnp.bfloat16

D = 1024
DEPTH = 4
NDEV = 8
HEADS = 8
QL = 256
KVL = 128
ROPE = 32
NOPE = 64
SC = 256
SSD_DIM = 256
SSD_CONV = 768
SSD_H = 4
SSD_L = 128
FFN = 2816
FB = 704
EPS = 1e-6
ROPE_THETA = 10000.0
ATT_SCALE = 96 ** -0.5
LOG2E = 1.4426950408889634
LR, B1, B2, AEPS, WD, STEP = 0.001, 0.9, 0.999, 1e-08, 0.01, 10

PW = 2432
CATW = 1536

ROW_TILE = 512
ROW_CHUNK = 16
NORM_CHUNK = 32
MM_TILE = 1024
ATT_TILE = 256
ATT_QW = 2
FWD_HEADS = 4
BWD_HEADS = 2
HALO = 16
LANE = 128
NEG = -1e30
HI = lax.Precision.HIGHEST
NN = (((1,), (0,)), ((), ()))
NT = (((1,), (1,)), ((), ()))
TN = (((0,), (0,)), ((), ()))
VMEM_LIMIT = 56 * 1024 * 1024

SHARDED = (
    ("w_in", (4, 128, 2212)),
    ("mla_w_q_up", (4, 256, 96)),
    ("mla_w_kv_up", (4, 128, 128)),
    ("sc_conv_w", (4, 3, 32)),
    ("ssd_conv_w", (4, 4, 96)),
    ("w_out", (4, 128, 1024)),
    ("ffn_w_up", (4, 1024, 704)),
    ("ffn_conv_w", (4, 3, 704)),
    ("ffn_w_down", (4, 352, 1024)),
)
SHAPES = dict(SHARDED)
GROUPS = (
    (2212, ("w_in",)),
    (1024, ("w_out",)),
    (704, ("ffn_w_up",)),
    (96, ("mla_w_q_up", "ssd_conv_w", "sc_conv_w")),
    (128, ("mla_w_kv_up",)),
    (1024, ("ffn_w_down",)),
    (704, ("ffn_conv_w",)),
)
ATT_SIDE = (0, 3, 4)
FFN_SIDE = (1, 2, 5, 6)
SMALL = (
    ("norm_mix_pre", 1024), ("norm_mix_post", 1024), ("norm_ffn_pre", 1024), ("norm_ffn_post", 1024),
    ("mla_q_norm", 256), ("mla_kv_norm", 128), ("ssd_conv_b", 768), ("ssd_dt_bias", 4), ("ssd_a_log", 4),
    ("ssd_d", 4), ("ssd_norm", 256), ("ffn_conv_b", 5632),
)
WEIGHTS = ("norm_mix_pre", "norm_mix_post", "norm_ffn_pre", "norm_ffn_post", "w_in", "mla_q_norm", "mla_w_q_up",
           "mla_kv_norm", "mla_w_kv_up", "sc_conv_w", "ssd_conv_w", "ssd_conv_b", "ssd_dt_bias", "ssd_a_log", "ssd_d",
           "ssd_norm", "w_out", "ffn_w_up", "ffn_conv_w", "ffn_conv_b", "ffn_w_down")


def _dot(a, b, dims=NN, precision=None):
    return lax.dot_general(a, b, dims, precision=precision, preferred_element_type=F32)


def _sig(v):
    return 1.0 / (1.0 + jnp.exp(-v))


def _cp(*sem):
    return pltpu.CompilerParams(dimension_semantics=sem, vmem_limit_bytes=VMEM_LIMIT)


def _rowsum(v):
    return jnp.sum(v, axis=0, keepdims=True)


def _prev_halo(i, ts):
    return jnp.maximum(i * (ts // HALO) - 1, 0)


def _next_halo(i, ts, n):
    return jnp.minimum((i + 1) * (ts // HALO), n * (ts // HALO) - 1)


def _gather_plan(x_refs, out_refs, send_sems, recv_sems, local_sems):
    n = len(x_refs)
    x, y, cc = lax.axis_index("x"), lax.axis_index("y"), lax.axis_index("c")
    me, sibling = (x, y, cc), (x, y, 1 - cc)
    chips = [(1 - x, y), (x, 1 - y), (1 - x, 1 - y)]

    def rows(t, px, py, pc):
        return out_refs[t].at[4 * px + 2 * py + pc]

    def copy(t, k, block, to, own=False):
        return pltpu.make_async_remote_copy(
            src_ref=x_refs[t] if own else rows(t, *block), dst_ref=rows(t, *block),
            send_sem=send_sems.at[7 * t + k], recv_sem=recv_sems.at[7 * t + k], device_id=to, device_id_type=pl.DeviceIdType.MESH)

    def local(t):
        return pltpu.make_async_copy(x_refs[t], rows(t, *me), local_sems.at[t])

    def start():
        for t in range(n):
            local(t).start()
            copy(t, 0, me, sibling, own=True).start()
            for j, chip in enumerate(chips):
                copy(t, 1 + j, me, (*chip, cc), own=True).start()

    def finish():
        for j, chip in enumerate(chips):
            for t in range(n):
                copy(t, 1 + j, (*chip, cc), me).wait_recv()
                copy(t, 4 + j, (*chip, cc), sibling).start()
        for t in range(n):
            copy(t, 0, sibling, me).wait_recv()
            for j, chip in enumerate(chips):
                copy(t, 4 + j, (*chip, 1 - cc), me).wait_recv()
        for t in range(n):
            copy(t, 0, me, sibling, own=True).wait_send()
            for j, chip in enumerate(chips):
                copy(t, 1 + j, me, (*chip, cc), own=True).wait_send()
                copy(t, 4 + j, (*chip, cc), sibling).wait_send()
            local(t).wait()

    return start, finish


def _exchange_plan(x_refs, out_refs, send_sems, recv_sems, local_sems):
    n = len(x_refs)
    x, y, cc = lax.axis_index("x"), lax.axis_index("y"), lax.axis_index("c")
    me = 4 * x + 2 * y + cc

    def copies():
        res = [pltpu.make_async_copy(x_refs[t].at[me], out_refs[t].at[me], local_sems.at[t]) for t in range(n)]
        for k in range(1, NDEV):
            px = 1 - x if k & 4 else x
            py = 1 - y if k & 2 else y
            pc = 1 - cc if k & 1 else cc
            peer = 4 * px + 2 * py + pc
            for t in range(n):
                res.append(pltpu.make_async_remote_copy(
                    src_ref=x_refs[t].at[peer], dst_ref=out_refs[t].at[me], send_sem=send_sems.at[7 * t + k - 1],
                    recv_sem=recv_sems.at[7 * t + k - 1], device_id=(px, py, pc), device_id_type=pl.DeviceIdType.MESH))
        return res

    def start():
        for cp in copies():
            cp.start()

    def finish():
        for cp in copies():
            cp.wait()

    return start, finish


def _comm_scratch(n):
    return [pltpu.SemaphoreType.DMA((7 * n,)), pltpu.SemaphoreType.DMA((7 * n,)), pltpu.SemaphoreType.DMA((n,))]


ANY = pl.BlockSpec(memory_space=pl.ANY)


def _all_gather(xs, name):
    n = len(xs)

    def body(*refs):
        start, finish = _gather_plan(refs[:n], refs[n:2 * n], *refs[2 * n:])
        start()
        finish()

    return pl.pallas_call(
        body, name=name, out_shape=[jax.ShapeDtypeStruct((NDEV,) + a.shape, a.dtype) for a in xs],
        in_specs=[ANY] * n, out_specs=[ANY] * n, scratch_shapes=_comm_scratch(n),
    )(*xs)


def _all_to_all(xs, name):
    n = len(xs)

    def body(*refs):
        start, finish = _exchange_plan(refs[:n], refs[n:2 * n], *refs[2 * n:])
        start()
        finish()

    return pl.pallas_call(
        body, name=name, out_shape=[jax.ShapeDtypeStruct(a.shape, a.dtype) for a in xs],
        in_specs=[ANY] * n, out_specs=[ANY] * n, scratch_shapes=_comm_scratch(n),
    )(*xs)


def _mm(name, a, b, out_shape, grid, a_spec, b_spec, o_spec, dims, acc_shape):
    nk = grid[2]

    def single(a_ref, b_ref, o_ref):
        o_ref[...] = _dot(a_ref[...], b_ref[...], dims).astype(o_ref.dtype)

    if nk == 1:
        return pl.pallas_call(
            single, name=name, grid=grid, out_shape=out_shape, in_specs=[a_spec, b_spec], out_specs=o_spec,
            compiler_params=_cp("parallel", "parallel", "arbitrary"),
        )(a, b)

    def body(a_ref, b_ref, o_ref, acc_ref):
        k = pl.program_id(2)

        @pl.when(k == 0)
        def _():
            acc_ref[...] = jnp.zeros_like(acc_ref)

        acc_ref[...] += _dot(a_ref[...], b_ref[...], dims)

        @pl.when(k == nk - 1)
        def _():
            o_ref[...] = acc_ref[...].astype(o_ref.dtype)

    return pl.pallas_call(
        body, name=name, grid=grid, out_shape=out_shape, in_specs=[a_spec, b_spec], out_specs=o_spec,
        scratch_shapes=[pltpu.VMEM(acc_shape, F32)], compiler_params=_cp("parallel", "parallel", "arbitrary"),
    )(a, b)


def _mm_rows(name, a, w, out_dtype, dims):
    s, k = a.shape
    n = w.shape[1] if dims == NN else w.shape[0]
    tm = min(MM_TILE, s)
    return _mm(name, a, w, jax.ShapeDtypeStruct((s, n), out_dtype), (s // tm, 1, 1),
               pl.BlockSpec((tm, k), lambda i, j, kk: (i, 0)), pl.BlockSpec(w.shape, lambda i, j, kk: (0, 0)),
               pl.BlockSpec((tm, n), lambda i, j, kk: (i, 0)), dims, (tm, n))


def _mm_wgrad(name, a, g, out_dtype):
    s, m = a.shape
    n = g.shape[1]
    tk = min(MM_TILE, s)
    return _mm(name, a, g, jax.ShapeDtypeStruct((m, n), out_dtype), (1, 1, s // tk),
               pl.BlockSpec((tk, m), lambda i, j, kk: (kk, 0)), pl.BlockSpec((tk, n), lambda i, j, kk: (kk, 0)),
               pl.BlockSpec((m, n), lambda i, j, kk: (0, 0)), TN, (m, n))


def _mm_up(h2, wup):
    s = h2.shape[0]
    tm = min(MM_TILE, s)
    return _mm("ffn_up", h2, wup, jax.ShapeDtypeStruct((NDEV, s, FB), BF16), (NDEV, s // tm, 1),
               pl.BlockSpec((tm, D), lambda j, i, kk: (i, 0)), pl.BlockSpec((None, D, FB), lambda j, i, kk: (j, 0, 0)),
               pl.BlockSpec((None, tm, FB), lambda j, i, kk: (j, i, 0)), NN, (tm, FB))


def _mm_down(act, wdown):
    s = act.shape[1]
    tm = min(MM_TILE, s)
    return _mm("ffn_down", act, wdown, jax.ShapeDtypeStruct((s, D), BF16), (s // tm, 1, 4),
               pl.BlockSpec((None, tm, FB), lambda i, j, kk: (kk, i, 0)), pl.BlockSpec((None, FB, D), lambda i, j, kk: (kk, 0, 0)),
               pl.BlockSpec((tm, D), lambda i, j, kk: (i, 0)), NN, (tm, D))


def _mm_dact(df, wdown):
    s = df.shape[0]
    tm = min(MM_TILE, s)
    return _mm("ffn_dact", df, wdown, jax.ShapeDtypeStruct((4, s, FB), BF16), (4, s // tm, 1),
               pl.BlockSpec((tm, D), lambda j, i, kk: (i, 0)), pl.BlockSpec((None, FB, D), lambda j, i, kk: (j, 0, 0)),
               pl.BlockSpec((None, tm, FB), lambda j, i, kk: (j, i, 0)), NT, (tm, FB))


def _mm_dwdown(act, df):
    s = df.shape[0]
    tk = min(MM_TILE, s)
    return _mm("ffn_dwdown", act, df, jax.ShapeDtypeStruct((4, FB, D), BF16), (4, 1, s // tk),
               pl.BlockSpec((None, tk, FB), lambda j, i, kk: (j, kk, 0)), pl.BlockSpec((tk, D), lambda j, i, kk: (kk, 0)),
               pl.BlockSpec((None, FB, D), lambda j, i, kk: (j, 0, 0)), TN, (FB, D))


def _mm_dh2(dupre, wup):
    s = dupre.shape[1]
    tm = min(MM_TILE, s)
    return _mm("ffn_dh2", dupre, wup, jax.ShapeDtypeStruct((s, D), BF16), (s // tm, 1, NDEV),
               pl.BlockSpec((None, tm, FB), lambda i, j, kk: (kk, i, 0)), pl.BlockSpec((None, D, FB), lambda i, j, kk: (kk, 0, 0)),
               pl.BlockSpec((tm, D), lambda i, j, kk: (i, 0)), NT, (tm, D))


def _mm_dwup(h2, dupre):
    s = h2.shape[0]
    tk = min(MM_TILE, s)
    return _mm("ffn_dwup", h2, dupre, jax.ShapeDtypeStruct((NDEV, D, FB), BF16), (NDEV, 1, s // tk),
               pl.BlockSpec((tk, D), lambda j, i, kk: (kk, 0)), pl.BlockSpec((None, tk, FB), lambda j, i, kk: (j, kk, 0)),
               pl.BlockSpec((None, D, FB), lambda j, i, kk: (j, 0, 0)), TN, (D, FB))


def _rms(xv, w, out_dtype, name):
    s, d = xv.shape
    ts = min(ROW_TILE, s)

    def body(x_ref, w_ref, o_ref):
        for r0 in range(0, ts, NORM_CHUNK):
            rows = pl.ds(r0, NORM_CHUNK)
            xf = x_ref[rows, :].astype(F32)
            r = lax.rsqrt(jnp.mean(xf * xf, axis=-1, keepdims=True) + EPS)
            o_ref[rows, :] = (xf * r * w_ref[...]).astype(o_ref.dtype)

    return pl.pallas_call(
        body, name=name, grid=(s // ts,), out_shape=jax.ShapeDtypeStruct((s, d), out_dtype),
        in_specs=[pl.BlockSpec((ts, d), lambda i: (i, 0)), pl.BlockSpec((1, d), lambda i: (0, 0))],
        out_specs=pl.BlockSpec((ts, d), lambda i: (i, 0)), compiler_params=_cp("parallel"),
    )(xv, w)


def _add_rms(xv, mv, w, name, w_next=None):
    s, d = xv.shape
    ts = min(ROW_TILE, s)
    both = w_next is not None

    def body(*refs):
        x_ref, m_ref, w_ref = refs[:3]
        o_ref = refs[4] if both else refs[3]
        for r0 in range(0, ts, NORM_CHUNK):
            rows = pl.ds(r0, NORM_CHUNK)
            mf = m_ref[rows, :].astype(F32)
            r = lax.rsqrt(jnp.mean(mf * mf, axis=-1, keepdims=True) + EPS)
            y = x_ref[rows, :] + mf * r * w_ref[...]
            o_ref[rows, :] = y
            if both:
                r2 = lax.rsqrt(jnp.mean(y * y, axis=-1, keepdims=True) + EPS)
                refs[5][rows, :] = (y * r2 * refs[3][...]).astype(BF16)

    row = pl.BlockSpec((ts, d), lambda i: (i, 0))
    vec = pl.BlockSpec((1, d), lambda i: (0, 0))
    if both:
        return pl.pallas_call(
            body, name=name + "_rms", grid=(s // ts,),
            out_shape=(jax.ShapeDtypeStruct((s, d), F32), jax.ShapeDtypeStruct((s, d), BF16)),
            in_specs=[row, row, vec, vec], out_specs=(row, row), compiler_params=_cp("parallel"),
        )(xv, mv, w, w_next)
    return pl.pallas_call(
        body, name=name, grid=(s // ts,), out_shape=jax.ShapeDtypeStruct((s, d), F32),
        in_specs=[row, row, vec], out_specs=row, compiler_params=_cp("parallel"),
    )(xv, mv, w)


def _rms_bwd_math(xf, w, dy):
    r = lax.rsqrt(jnp.mean(xf * xf, axis=-1, keepdims=True) + EPS)
    xh = xf * r
    dxh = dy * w
    dx = r * (dxh - xh * jnp.mean(dxh * xh, axis=-1, keepdims=True))
    return dx, dy * xh


def _rms_bwd(xv, w, dy, dres, out_dtype, name):
    s, d = xv.shape
    ts = min(ROW_TILE, s)
    with_res = dres is not None

    def body(*refs):
        if with_res:
            x_ref, w_ref, dy_ref, dres_ref, dx_ref, dw_ref, acc = refs
        else:
            x_ref, w_ref, dy_ref, dx_ref, dw_ref, acc = refs
        acc[...] = jnp.zeros_like(acc)
        for r0 in range(0, ts, NORM_CHUNK):
            rows = pl.ds(r0, NORM_CHUNK)
            dx, dwt = _rms_bwd_math(x_ref[rows, :].astype(F32), w_ref[...], dy_ref[rows, :].astype(F32))
            if with_res:
                dx = dx + dres_ref[rows, :]
            dx_ref[rows, :] = dx.astype(dx_ref.dtype)
            acc[...] += dwt

        @pl.when(pl.program_id(0) == 0)
        def _():
            dw_ref[...] = jnp.zeros_like(dw_ref)

        dw_ref[...] += _rowsum(acc[...])

    row = pl.BlockSpec((ts, d), lambda i: (i, 0))
    vec = pl.BlockSpec((1, d), lambda i: (0, 0))
    ins = [xv, w, dy] + ([dres] if with_res else [])
    return pl.pallas_call(
        body, name=name, grid=(s // ts,),
        out_shape=(jax.ShapeDtypeStruct((s, d), out_dtype), jax.ShapeDtypeStruct((1, d), F32)),
        in_specs=[row, vec, row] + ([row] if with_res else []), out_specs=(row, vec),
        scratch_shapes=[pltpu.VMEM((NORM_CHUNK, d), F32)], compiler_params=_cp("arbitrary"),
    )(*ins)


def _rms_bwd2(xa, wa, dya, dres, xb, wb):
    s, d = xa.shape
    ts = min(ROW_TILE, s)

    def body(xa_ref, wa_ref, dya_ref, dres_ref, xb_ref, wb_ref, da_ref, db_ref, dwa_ref, dwb_ref, acc):
        acc[...] = jnp.zeros_like(acc)
        for r0 in range(0, ts, NORM_CHUNK):
            rows = pl.ds(r0, NORM_CHUNK)
            da, dwt = _rms_bwd_math(xa_ref[rows, :].astype(F32), wa_ref[...], dya_ref[rows, :].astype(F32))
            da = da + dres_ref[rows, :]
            da_ref[rows, :] = da
            acc[0] += dwt
            db, dwt = _rms_bwd_math(xb_ref[rows, :].astype(F32), wb_ref[...], da)
            db_ref[rows, :] = db.astype(BF16)
            acc[1] += dwt

        @pl.when(pl.program_id(0) == 0)
        def _():
            dwa_ref[...] = jnp.zeros_like(dwa_ref)
            dwb_ref[...] = jnp.zeros_like(dwb_ref)

        dwa_ref[...] += _rowsum(acc[0])
        dwb_ref[...] += _rowsum(acc[1])

    row = pl.BlockSpec((ts, d), lambda i: (i, 0))
    vec = pl.BlockSpec((1, d), lambda i: (0, 0))
    return pl.pallas_call(
        body, name="rms_bwd2", grid=(s // ts,),
        out_shape=(jax.ShapeDtypeStruct((s, d), F32), jax.ShapeDtypeStruct((s, d), BF16), jax.ShapeDtypeStruct((1, d), F32),
                   jax.ShapeDtypeStruct((1, d), F32)),
        in_specs=[row, vec, row, row, row, vec], out_specs=(row, row, vec, vec),
        scratch_shapes=[pltpu.VMEM((2, NORM_CHUNK, d), F32)], compiler_params=_cp("arbitrary"),
    )(xa, wa, dya, dres, xb, wb)


def _loss_head(yv, tv):
    s, d = yv.shape
    ts = min(ROW_TILE, s)

    def body(y_ref, t_ref, l_ref, dy_ref):
        e = y_ref[...] - t_ref[...]
        dy_ref[...] = e * (1.0 / d)

        @pl.when(pl.program_id(0) == 0)
        def _():
            l_ref[...] = jnp.zeros_like(l_ref)

        tot = jnp.sum(jnp.sum(e * e, axis=1, keepdims=True), axis=0, keepdims=True)
        l_ref[...] += jnp.broadcast_to(tot * (0.5 / d), (8, LANE))

    row = pl.BlockSpec((ts, d), lambda i: (i, 0))
    return pl.pallas_call(
        body, name="loss_head", grid=(s // ts,),
        out_shape=(jax.ShapeDtypeStruct((8, LANE), F32), jax.ShapeDtypeStruct((s, d), F32)),
        in_specs=[row, row], out_specs=(pl.BlockSpec((8, LANE), lambda i: (0, 0)), row), compiler_params=_cp("arbitrary"),
    )(yv, tv)


def _rope(v, c, a, b):
    return v * c + pltpu.roll(v, LANE - 16, 1) * a + pltpu.roll(v, 16, 1) * b


def _rope_t(dv, c, a, b):
    return dv * c + pltpu.roll(dv * a, 16, 1) + pltpu.roll(dv * b, LANE - 16, 1)


def _mla_prep(proj, tabs, qnw, kvnw, wq, wkv):
    s = proj.shape[0]
    ts = min(ROW_TILE, s)
    tc, ta, tb = tabs

    def body(cq_ref, ckv_ref, kr_ref, c_ref, a_ref, b_ref, qnw_ref, kvnw_ref, wq_ref, wkv_ref, q_ref, k_ref, kv_ref):
        c, a, b = c_ref[...], a_ref[...], b_ref[...]
        cq = cq_ref[...].astype(F32)
        qn = (cq * lax.rsqrt(jnp.mean(cq * cq, axis=-1, keepdims=True) + EPS) * qnw_ref[...]).astype(BF16)
        ckv = ckv_ref[...].astype(F32)
        kvn = (ckv * lax.rsqrt(jnp.mean(ckv * ckv, axis=-1, keepdims=True) + EPS) * kvnw_ref[...]).astype(BF16)
        kr = _rope(kr_ref[...].astype(F32), c, a, b)
        lane = lax.broadcasted_iota(jnp.int32, (ts, LANE), 1)
        for h in range(HEADS):
            q_ref[h] = _rope(_dot(qn, wq_ref[h]), c, a, b).astype(BF16)
            kv = _dot(kvn, wkv_ref[h])
            kv_ref[h] = kv.astype(BF16)
            k_ref[h] = jnp.where(lane < NOPE, kv, kr).astype(BF16)

    tab = pl.BlockSpec((ts, LANE), lambda i: (i, 0))
    hd = pl.BlockSpec((HEADS, ts, LANE), lambda i: (0, i, 0))
    out = jax.ShapeDtypeStruct((HEADS, s, LANE), BF16)
    return pl.pallas_call(
        body, name="mla_prep", grid=(s // ts,), out_shape=(out, out, out),
        in_specs=[pl.BlockSpec((ts, QL), lambda i: (i, 0)), pl.BlockSpec((ts, LANE), lambda i: (i, 2)),
                  pl.BlockSpec((ts, LANE), lambda i: (i, 3)), tab, tab, tab,
                  pl.BlockSpec((1, QL), lambda i: (0, 0)), pl.BlockSpec((1, KVL), lambda i: (0, 0)),
                  pl.BlockSpec((HEADS, QL, LANE), lambda i: (0, 0, 0)), pl.BlockSpec((HEADS, KVL, LANE), lambda i: (0, 0, 0))],
        out_specs=(hd, hd, hd), compiler_params=_cp("parallel"),
    )(proj, proj, proj, tc, ta, tb, qnw, kvnw, wq, wkv)


def _mla_prep_bwd(proj, tabs, qnw, kvnw, wq, wkv, dq, dk, dv):
    s = proj.shape[0]
    ts = min(ROW_TILE, s)
    tc, ta, tb = tabs

    def body(cq_ref, ckv_ref, c_ref, a_ref, b_ref, qnw_ref, kvnw_ref, wq_ref, wkv_ref, dq_ref, dk_ref, dv_ref,
             dcq_ref, dckv_ref, dkr_ref, dwq_ref, dwkv_ref, dqnw_ref, dkvnw_ref):
        @pl.when(pl.program_id(0) == 0)
        def _():
            dwq_ref[...] = jnp.zeros_like(dwq_ref)
            dwkv_ref[...] = jnp.zeros_like(dwkv_ref)
            dqnw_ref[...] = jnp.zeros_like(dqnw_ref)
            dkvnw_ref[...] = jnp.zeros_like(dkvnw_ref)

        c, a, b = c_ref[...], a_ref[...], b_ref[...]
        cq = cq_ref[...].astype(F32)
        qn = (cq * lax.rsqrt(jnp.mean(cq * cq, axis=-1, keepdims=True) + EPS) * qnw_ref[...]).astype(BF16)
        ckv = ckv_ref[...].astype(F32)
        kvn = (ckv * lax.rsqrt(jnp.mean(ckv * ckv, axis=-1, keepdims=True) + EPS) * kvnw_ref[...]).astype(BF16)
        lane = lax.broadcasted_iota(jnp.int32, (ts, LANE), 1)
        dqn = jnp.zeros((ts, QL), F32)
        dkvn = jnp.zeros((ts, KVL), F32)
        dkr = jnp.zeros((ts, LANE), F32)
        for h in range(HEADS):
            dqh = _rope_t(dq_ref[h], c, a, b).astype(BF16)
            dwq_ref[h] += _dot(qn, dqh, TN)
            dqn += _dot(dqh, wq_ref[h], NT)
            dkh = dk_ref[h].astype(F32)
            dkvh = jnp.where(lane < NOPE, dkh, dv_ref[h].astype(F32)).astype(BF16)
            dkr += jnp.where(lane < NOPE, 0.0, dkh)
            dwkv_ref[h] += _dot(kvn, dkvh, TN)
            dkvn += _dot(dkvh, wkv_ref[h], NT)
        dkr_ref[...] = _rope_t(dkr, c, a, b).astype(BF16)
        dcq, dwt = _rms_bwd_math(cq, qnw_ref[...], dqn)
        dcq_ref[...] = dcq.astype(BF16)
        dqnw_ref[...] += _rowsum(dwt)
        dckv, dwt = _rms_bwd_math(ckv, kvnw_ref[...], dkvn)
        dckv_ref[...] = dckv.astype(BF16)
        dkvnw_ref[...] += _rowsum(dwt)

    tab = pl.BlockSpec((ts, LANE), lambda i: (i, 0))
    hd = pl.BlockSpec((HEADS, ts, LANE), lambda i: (0, i, 0))
    wq_spec = pl.BlockSpec((HEADS, QL, LANE), lambda i: (0, 0, 0))
    wkv_spec = pl.BlockSpec((HEADS, KVL, LANE), lambda i: (0, 0, 0))
    return pl.pallas_call(
        body, name="mla_prep_bwd", grid=(s // ts,),
        out_shape=(jax.ShapeDtypeStruct((s, QL), BF16), jax.ShapeDtypeStruct((s, KVL), BF16), jax.ShapeDtypeStruct((s, LANE), BF16),
                   jax.ShapeDtypeStruct((HEADS, QL, LANE), F32), jax.ShapeDtypeStruct((HEADS, KVL, LANE), F32),
                   jax.ShapeDtypeStruct((1, QL), F32), jax.ShapeDtypeStruct((1, KVL), F32)),
        in_specs=[pl.BlockSpec((ts, QL), lambda i: (i, 0)), pl.BlockSpec((ts, LANE), lambda i: (i, 2)), tab, tab, tab,
                  pl.BlockSpec((1, QL), lambda i: (0, 0)), pl.BlockSpec((1, KVL), lambda i: (0, 0)), wq_spec, wkv_spec, hd, hd, hd],
        out_specs=(pl.BlockSpec((ts, QL), lambda i: (i, 0)), pl.BlockSpec((ts, KVL), lambda i: (i, 0)), tab, wq_spec, wkv_spec,
                   pl.BlockSpec((1, QL), lambda i: (0, 0)), pl.BlockSpec((1, KVL), lambda i: (0, 0))),
        compiler_params=_cp("arbitrary"),
    )(proj, proj, tc, ta, tb, qnw, kvnw, wq, wkv, dq, dk, dv)


def _transpose_bf16(v):
    return v.astype(F32).T.astype(BF16)


def _flash_fwd(q, k, kv, prefetch=None):
    s = q.shape[1]
    t = min(ATT_TILE, s)
    tq = ATT_QW * t
    n = s // tq
    g = FWD_HEADS
    nx = len(prefetch) if prefetch else 0

    def body(*refs):
        q_ref, k_ref, kv_ref = refs[:3]
        o_ref, lse_ref = refs[3 + nx:5 + nx]
        kvt_sc = refs[5 + 2 * nx]
        step = pl.program_id(0) * n + pl.program_id(1)
        if nx:
            start, finish = _gather_plan(refs[3:3 + nx], refs[5 + nx:5 + 2 * nx], *refs[6 + 2 * nx:])
            pl.when(step == 0)(start)
        attend(q_ref, k_ref, kv_ref, o_ref, lse_ref, kvt_sc)
        if nx:
            pl.when(step == (HEADS // g) * n - 1)(finish)

    def attend(q_ref, k_ref, kv_ref, o_ref, lse_ref, kvt_sc):
        i = pl.program_id(1)

        @pl.when(i == 0)
        def _():
            ones_rows = lax.broadcasted_iota(jnp.int32, (LANE, s), 0) < NOPE
            for hh in range(g):
                kvt_sc[hh] = jnp.where(ones_rows, 1.0, kv_ref[hh].astype(F32).T).astype(BF16)

        qt = [(q_ref[hh].astype(F32) * (ATT_SCALE * LOG2E)).T.astype(BF16) for hh in range(g)]
        kpos = lax.broadcasted_iota(jnp.int32, (t, tq), 0)
        qpos = lax.broadcasted_iota(jnp.int32, (t, tq), 1) + i * tq

        def chunk(j, carry, diagonal):
            start = pl.multiple_of(j * t, t)
            scs = [_dot(k_ref[hh, pl.ds(start, t), :], qt[hh]) for hh in range(g)]
            soft = []
            for hh in range(g):
                sc = scs[hh]
                if diagonal:
                    sc = jnp.where(qpos >= kpos + start, sc, NEG)
                m_new = jnp.maximum(carry[hh][0], jnp.max(sc, axis=0, keepdims=True))
                soft.append((m_new, jnp.exp2(carry[hh][0] - m_new), jnp.exp2(sc - m_new).astype(BF16)))
            pvs = [_dot(kvt_sc[hh, :, pl.ds(start, t)], soft[hh][2]) for hh in range(g)]
            return tuple((soft[hh][0], soft[hh][1] * carry[hh][1] + pvs[hh]) for hh in range(g))

        init = tuple((jnp.full((1, tq), NEG, F32), jnp.zeros((LANE, tq), F32)) for _ in range(g))
        carry = lax.fori_loop(0, ATT_QW * i, lambda j, c: chunk(j, c, False), init)
        for d in range(ATT_QW):
            carry = chunk(ATT_QW * i + d, carry, True)
        for hh in range(g):
            m, acc = carry[hh]
            l = acc[0:1, :]
            o_ref[:, hh * LANE:(hh + 1) * LANE] = (acc / l).T.astype(BF16)
            lse_ref[hh] = m + jnp.log2(l)

    whole = pl.BlockSpec((g, s, LANE), lambda h, i: (h, 0, 0))
    res = pl.pallas_call(
        body, name="flash_fwd_gather" if nx else "flash_fwd", grid=(HEADS // g, n),
        out_shape=[jax.ShapeDtypeStruct((s, HEADS * LANE), BF16), jax.ShapeDtypeStruct((HEADS, 1, s), F32)]
        + [jax.ShapeDtypeStruct((NDEV,) + a.shape, a.dtype) for a in (prefetch or [])],
        in_specs=[pl.BlockSpec((g, tq, LANE), lambda h, i: (h, i, 0)), whole, whole] + [ANY] * nx,
        out_specs=[pl.BlockSpec((tq, g * LANE), lambda h, i: (i, h)), pl.BlockSpec((g, 1, tq), lambda h, i: (h, 0, i))] + [ANY] * nx,
        scratch_shapes=[pltpu.VMEM((g, LANE, s), BF16)] + (_comm_scratch(nx) if nx else []),
        compiler_params=_cp("arbitrary", "arbitrary"),
    )(q, k, kv, *(prefetch or []))
    return res[0], res[1], list(res[2:])


def _flash_bwd(q, k, kv, cat, dcat, lse, pending=None):
    s = q.shape[1]
    t = min(ATT_TILE, s)
    tq = ATT_QW * t
    n = s // t
    g = BWD_HEADS
    nx = len(pending) if pending else 0

    def body(*refs):
        ins, outs, scr = refs[:6], refs[6 + nx:9 + nx], refs[9 + 2 * nx:14 + 2 * nx]
        step = pl.program_id(0) * n + pl.program_id(1)
        if nx:
            start, finish = _exchange_plan(refs[6:6 + nx], refs[9 + nx:9 + 2 * nx], *refs[14 + 2 * nx:])
            pl.when(step == 0)(start)
        attend(*ins, *outs, *scr)
        if nx:
            pl.when(step == (HEADS // g) * n - 1)(finish)

    def attend(q_ref, k_ref, kv_ref, o_ref, do_ref, lse_ref, dq_ref, dk_ref, dv_ref, qt_sc, dot_sc, delta_sc, dqt_sc, qs_sc):
        j = pl.program_id(1)

        @pl.when(j == 0)
        def _():
            for hh in range(g):
                lanes = slice(hh * LANE, (hh + 1) * LANE)
                qf = q_ref[hh].astype(F32)
                qt_sc[hh] = (qf * (ATT_SCALE * LOG2E)).T.astype(BF16)
                qs_sc[hh] = (qf * ATT_SCALE).astype(BF16)
                dof = do_ref[:, lanes].astype(F32)
                dot_sc[hh] = dof.T.astype(BF16)
                delta_sc[hh] = _dot(jnp.ones((8, LANE), F32), dof * o_ref[:, lanes].astype(F32), NT, precision=HI)
            dqt_sc[...] = jnp.zeros_like(dqt_sc)

        kjt = [_transpose_bf16(k_ref[hh]) for hh in range(g)]
        kpos = lax.broadcasted_iota(jnp.int32, (t, tq), 0) + j * t
        qpos = lax.broadcasted_iota(jnp.int32, (t, tq), 1)

        def chunk(i, carry, diagonal):
            start = pl.multiple_of(i * tq, tq)
            cols = pl.ds(start, tq)
            scs = [_dot(k_ref[hh], qt_sc[hh, :, cols]) for hh in range(g)]
            dps = [_dot(kv_ref[hh], dot_sc[hh, :, cols]) for hh in range(g)]
            pds = []
            for hh in range(g):
                p = jnp.exp2(scs[hh] - lse_ref[hh, :, cols])
                if diagonal:
                    p = jnp.where(qpos + start >= kpos, p, 0.0)
                ds = (p * (dps[hh] - delta_sc[hh, 0:1, cols])).astype(BF16)
                pds.append((p.astype(BF16), ds))
            out = []
            for hh in range(g):
                dk, dv = carry[hh]
                dv = dv + _dot(pds[hh][0], do_ref[pl.ds(start, tq), hh * LANE:(hh + 1) * LANE])
                dk = dk + _dot(pds[hh][1], qs_sc[hh, pl.ds(start, tq), :])
                dqt_sc[hh, :, cols] += _dot(kjt[hh], pds[hh][1])
                out.append((dk, dv))
            return tuple(out)

        zero = jnp.zeros((t, LANE), F32)
        first = lax.div(j, ATT_QW)
        carry = chunk(first, tuple((zero, zero) for _ in range(g)), True)
        carry = lax.fori_loop(first + 1, s // tq, lambda i, c: chunk(i, c, False), carry)
        for hh in range(g):
            dk_ref[hh] = carry[hh][0].astype(BF16)
            dv_ref[hh] = carry[hh][1].astype(BF16)

        @pl.when(j == n - 1)
        def _():
            for hh in range(g):
                dq_ref[hh] = (dqt_sc[hh] * ATT_SCALE).T

    whole = pl.BlockSpec((g, s, LANE), lambda h, j: (h, 0, 0))
    kspec = pl.BlockSpec((g, t, LANE), lambda h, j: (h, j, 0))
    ospec = pl.BlockSpec((s, g * LANE), lambda h, j: (0, h))
    res = pl.pallas_call(
        body, name="flash_bwd_exchange" if nx else "flash_bwd", grid=(HEADS // g, n),
        out_shape=[jax.ShapeDtypeStruct((HEADS, s, LANE), F32), jax.ShapeDtypeStruct((HEADS, s, LANE), BF16),
                   jax.ShapeDtypeStruct((HEADS, s, LANE), BF16)] + [jax.ShapeDtypeStruct(a.shape, a.dtype) for a in (pending or [])],
        in_specs=[whole, kspec, kspec, ospec, ospec, pl.BlockSpec((g, 1, s), lambda h, j: (h, 0, 0))] + [ANY] * nx,
        out_specs=[whole, kspec, kspec] + [ANY] * nx,
        scratch_shapes=[pltpu.VMEM((g, LANE, s), BF16), pltpu.VMEM((g, LANE, s), BF16), pltpu.VMEM((g, 8, s), F32),
                        pltpu.VMEM((g, LANE, s), F32), pltpu.VMEM((g, s, LANE), BF16)] + (_comm_scratch(nx) if nx else []),
        compiler_params=_cp("arbitrary", "arbitrary"),
    )(q, k, kv, cat, dcat, lse, *(pending or []))
    return res[0], res[1], res[2], list(res[3:])


def _conv3(ext, w_ref, ts):
    return (w_ref[0:1, :] * ext[pl.ds(HALO - 2, ts), :] + w_ref[1:2, :] * ext[pl.ds(HALO - 1, ts), :]
            + w_ref[2:3, :] * ext[pl.ds(HALO, ts), :])


def _conv3_rows(ext, w_ref, r):
    return (w_ref[0:1, :] * ext[pl.ds(HALO - 2 + r, ROW_CHUNK), :] + w_ref[1:2, :] * ext[pl.ds(HALO - 1 + r, ROW_CHUNK), :]
            + w_ref[2:3, :] * ext[pl.ds(HALO + r, ROW_CHUNK), :])


def _conv3_t(ext2, w_ref, ts):
    return (w_ref[0:1, :] * ext2[pl.ds(2, ts), :] + w_ref[1:2, :] * ext2[pl.ds(1, ts), :] + w_ref[2:3, :] * ext2[pl.ds(0, ts), :])


def _sconv_fwd(proj, w):
    s = proj.shape[0]
    ts = min(ROW_TILE, s)

    def body(b_ref, c_ref, h_ref, hc_ref, hh_ref, w_ref, o_ref, ext):
        i = pl.program_id(0)
        ext[0:HALO, :] = hc_ref[...].astype(F32) * hh_ref[...].astype(F32) * (i > 0).astype(F32)
        ext[HALO:HALO + ts, :] = c_ref[...].astype(F32) * h_ref[...].astype(F32)
        o_ref[...] = (b_ref[...].astype(F32) * _conv3(ext, w_ref, ts)).astype(BF16)

    def col(cb):
        return pl.BlockSpec((ts, SC), lambda i: (i, cb))

    def halo(cb):
        return pl.BlockSpec((HALO, SC), lambda i: (_prev_halo(i, ts), cb))

    return pl.pallas_call(
        body, name="sconv_fwd", grid=(s // ts,), out_shape=jax.ShapeDtypeStruct((s, SC), BF16),
        in_specs=[col(2), col(3), col(4), halo(3), halo(4), pl.BlockSpec((3, SC), lambda i: (0, 0))],
        out_specs=pl.BlockSpec((ts, SC), lambda i: (i, 0)), scratch_shapes=[pltpu.VMEM((ts + HALO, SC), F32)],
        compiler_params=_cp("parallel"),
    )(proj, proj, proj, proj, proj, w)


def _sconv_bwd(proj, dcat, w):
    s = proj.shape[0]
    ts = min(ROW_TILE, s)
    n = s // ts

    def body(b_ref, c_ref, h_ref, hc_ref, hh_ref, dy_ref, ndy_ref, nb_ref, w_ref, db_ref, dc_ref, dh_ref, dw_ref, ext, ext2):
        i = pl.program_id(0)

        @pl.when(i == 0)
        def _():
            dw_ref[...] = jnp.zeros_like(dw_ref)

        cv, hv, bv = c_ref[...].astype(F32), h_ref[...].astype(F32), b_ref[...].astype(F32)
        ext[0:HALO, :] = hc_ref[...].astype(F32) * hh_ref[...].astype(F32) * (i > 0).astype(F32)
        ext[HALO:HALO + ts, :] = cv * hv
        dy = dy_ref[...].astype(F32)
        db_ref[...] = (dy * _conv3(ext, w_ref, ts)).astype(BF16)
        dyb = dy * bv
        ext2[0:ts, :] = dyb
        ext2[ts:ts + HALO, :] = ndy_ref[...].astype(F32) * nb_ref[...].astype(F32) * (i < n - 1).astype(F32)
        dg = _conv3_t(ext2, w_ref, ts)
        dc_ref[...] = (dg * hv).astype(BF16)
        dh_ref[...] = (dg * cv).astype(BF16)
        for kk in range(3):
            dw_ref[kk:kk + 1, :] += _rowsum(dyb * ext[pl.ds(HALO - 2 + kk, ts), :])

    def col(cb):
        return pl.BlockSpec((ts, SC), lambda i: (i, cb))

    def halo(cb):
        return pl.BlockSpec((HALO, SC), lambda i: (_prev_halo(i, ts), cb))

    def nxt(cb):
        return pl.BlockSpec((HALO, SC), lambda i: (_next_halo(i, ts, n), cb))

    out = jax.ShapeDtypeStruct((s, SC), BF16)
    o0 = pl.BlockSpec((ts, SC), lambda i: (i, 0))
    return pl.pallas_call(
        body, name="sconv_bwd", grid=(n,), out_shape=(out, out, out, jax.ShapeDtypeStruct((3, SC), F32)),
        in_specs=[col(2), col(3), col(4), halo(3), halo(4), col(4), nxt(4), nxt(2), pl.BlockSpec((3, SC), lambda i: (0, 0))],
        out_specs=(o0, o0, o0, pl.BlockSpec((3, SC), lambda i: (0, 0))),
        scratch_shapes=[pltpu.VMEM((ts + HALO, SC), F32), pltpu.VMEM((ts + HALO, SC), F32)], compiler_params=_cp("arbitrary"),
    )(proj, proj, proj, proj, proj, dcat, dcat, proj, w)


def _ffn_stage(ext, u_ref, halo_ref, i, ts):
    ext[0:HALO, :] = halo_ref[...].astype(F32) * (i > 0).astype(F32)
    ext[HALO:HALO + ts, :] = u_ref[...].astype(F32)


def _ffn_specs(ts):
    cur = pl.BlockSpec((2, None, ts, FB), lambda j, i: (0, j, i, 0))
    halo = pl.BlockSpec((2, None, HALO, FB), lambda j, i: (0, j, _prev_halo(i, ts), 0))
    w = pl.BlockSpec((2, None, 3, FB), lambda j, i: (0, j, 0, 0))
    b = pl.BlockSpec((2, None, 1, FB), lambda j, i: (0, j, 0, 0))
    return cur, halo, w, b


def _ffn_act(upre, fcw, fcb):
    s = upre.shape[1]
    ts = min(ROW_TILE, s)

    def body(u_ref, halo_ref, w_ref, b_ref, o_ref, ext_g, ext_u):
        i = pl.program_id(1)
        _ffn_stage(ext_g, u_ref.at[0], halo_ref.at[0], i, ts)
        _ffn_stage(ext_u, u_ref.at[1], halo_ref.at[1], i, ts)
        for r in range(0, ts, ROW_CHUNK):
            gate = b_ref[0] + _conv3_rows(ext_g, w_ref.at[0], r)
            up = b_ref[1] + _conv3_rows(ext_u, w_ref.at[1], r)
            o_ref[pl.ds(r, ROW_CHUNK), :] = (gate * _sig(gate) * up).astype(BF16)

    cur, halo, w, b = _ffn_specs(ts)
    u4 = upre.reshape(2, 4, s, FB)
    return pl.pallas_call(
        body, name="ffn_act", grid=(4, s // ts), out_shape=jax.ShapeDtypeStruct((4, s, FB), BF16),
        in_specs=[cur, halo, w, b], out_specs=pl.BlockSpec((None, ts, FB), lambda j, i: (j, i, 0)),
        scratch_shapes=[pltpu.VMEM((ts + HALO, FB), F32), pltpu.VMEM((ts + HALO, FB), F32)], compiler_params=_cp("parallel", "parallel"),
    )(u4, u4, fcw.reshape(2, 4, 3, FB), fcb.reshape(2, 4, 1, FB))


def _ffn_bwd(upre, dact, fcw, fcb):
    s = upre.shape[1]
    ts = min(ROW_TILE, s)
    n = s // ts
    te = ts + HALO

    def body(u_ref, halo_ref, nxt_ref, w_ref, b_ref, da_ref, nda_ref, dup_ref, db_ref, dw_ref, ext_g, ext_u, ext_da, du_g, du_u, acc):
        i = pl.program_id(1)

        @pl.when(i == 0)
        def _():
            db_ref[...] = jnp.zeros_like(db_ref)
            dw_ref[...] = jnp.zeros_like(dw_ref)

        more = (i < n - 1).astype(F32)
        for idx, ext in ((0, ext_g), (1, ext_u)):
            _ffn_stage(ext, u_ref.at[idx], halo_ref.at[idx], i, ts)
            ext[HALO + ts:HALO + te, :] = nxt_ref[idx].astype(F32) * more
        ext_da[0:ts, :] = da_ref[...].astype(F32)
        ext_da[ts:te, :] = nda_ref[...].astype(F32) * more
        acc[...] = jnp.zeros_like(acc)
        for r in range(0, te, ROW_CHUNK):
            rows = pl.ds(r, ROW_CHUNK)
            gate = b_ref[0] + _conv3_rows(ext_g, w_ref.at[0], r)
            up = b_ref[1] + _conv3_rows(ext_u, w_ref.at[1], r)
            sg = _sig(gate)
            da = ext_da[rows, :]
            dgate = da * up * sg * (1.0 + gate * (1.0 - sg))
            dup = da * gate * sg
            du_g[rows, :] = dgate
            du_u[rows, :] = dup
            if r < ts:
                acc[0] += dgate
                acc[1] += dup
        for r in range(0, ts, ROW_CHUNK):
            for idx, du, ext in ((0, du_g, ext_g), (1, du_u, ext_u)):
                d0 = du[pl.ds(r, ROW_CHUNK), :]
                dupre = (w_ref[idx, 2:3, :] * d0 + w_ref[idx, 1:2, :] * du[pl.ds(r + 1, ROW_CHUNK), :]
                         + w_ref[idx, 0:1, :] * du[pl.ds(r + 2, ROW_CHUNK), :])
                dup_ref[idx, pl.ds(r, ROW_CHUNK), :] = dupre.astype(BF16)
                for kk in range(3):
                    acc[2 + 3 * idx + kk] += d0 * ext[pl.ds(HALO - 2 + kk + r, ROW_CHUNK), :]
        for idx in range(2):
            db_ref[idx] += _rowsum(acc[idx])
            for kk in range(3):
                dw_ref[idx, kk:kk + 1, :] += _rowsum(acc[2 + 3 * idx + kk])

    cur, halo, w, b = _ffn_specs(ts)
    nxt = pl.BlockSpec((2, None, HALO, FB), lambda j, i: (0, j, _next_halo(i, ts, n), 0))
    u4 = upre.reshape(2, 4, s, FB)
    dupre, db, dw = pl.pallas_call(
        body, name="ffn_bwd", grid=(4, n),
        out_shape=(jax.ShapeDtypeStruct((2, 4, s, FB), BF16), jax.ShapeDtypeStruct((2, 4, 1, FB), F32),
                   jax.ShapeDtypeStruct((2, 4, 3, FB), F32)),
        in_specs=[cur, halo, nxt, w, b, pl.BlockSpec((None, ts, FB), lambda j, i: (j, i, 0)),
                  pl.BlockSpec((None, HALO, FB), lambda j, i: (j, _next_halo(i, ts, n), 0))],
        out_specs=(cur, b, w),
        scratch_shapes=[pltpu.VMEM((te + HALO, FB), F32), pltpu.VMEM((te + HALO, FB), F32), pltpu.VMEM((te, FB), F32),
                        pltpu.VMEM((te, FB), F32), pltpu.VMEM((te, FB), F32), pltpu.VMEM((8, ROW_CHUNK, FB), F32)],
        compiler_params=_cp("parallel", "arbitrary"),
    )(u4, u4, u4, fcw.reshape(2, 4, 3, FB), fcb.reshape(2, 4, 1, FB), dact, dact)
    return dupre.reshape(NDEV, s, FB), db.reshape(NDEV, 1, FB), dw.reshape(NDEV, 3, FB)


def _softplus(v):
    e = jnp.exp(-jnp.abs(v))
    return jnp.maximum(v, 0.0) + jnp.where(e < 1e-4, e * (1.0 - 0.5 * e), jnp.log(1.0 + e))


def _ssd_consts():
    L = SSD_L
    r = lax.broadcasted_iota(jnp.int32, (L, L), 0)
    c = lax.broadcasted_iota(jnp.int32, (L, L), 1)
    tri = r >= c
    er = lax.broadcasted_iota(jnp.int32, (LANE, SSD_DIM), 0)
    ec = lax.broadcasted_iota(jnp.int32, (LANE, SSD_DIM), 1)
    expand = ((ec >= er * 64) & (ec < er * 64 + 64)).astype(F32)
    return tri, expand


def _ssd_conv4(ext, cw_ref, cb_ref):
    L = SSD_L
    pre = cb_ref[...] + cw_ref[0:1, :] * ext[pl.ds(HALO - 3, L), :]
    for kk in range(1, 4):
        pre = pre + cw_ref[kk:kk + 1, :] * ext[pl.ds(HALO - 3 + kk, L), :]
    return pre


def _ssd_common(xbc_ref, halo_ref, dt_ref, cw_ref, cb_ref, dtb_ref, alog_ref, ext, first):
    L = SSD_L
    tri, expand = _ssd_consts()
    ext[0:HALO, :] = halo_ref[...].astype(F32) * (1.0 - first.astype(F32))
    ext[HALO:HALO + L, :] = xbc_ref[...].astype(F32)
    pre = _ssd_conv4(ext, cw_ref, cb_ref)
    sg = _sig(pre)
    act = pre * sg
    lane = lax.broadcasted_iota(jnp.int32, (1, LANE), 1)
    m4 = lane < SSD_H
    raw = dt_ref[...].astype(F32) + dtb_ref[...]
    dtv = jnp.where(m4, _softplus(raw), 0.0)
    av = jnp.where(m4, -jnp.exp(alog_ref[...]), 0.0)
    adt = dtv * av
    acs = _dot(tri.astype(F32), adt, precision=HI)
    acs_b = _dot(acs, expand, precision=HI)
    dt_b = _dot(dtv, expand, precision=HI)
    return dict(tri=tri, expand=expand, pre=pre, sg=sg, act=act, raw=raw, dtv=dtv, av=av, m4=m4, acs=acs, acs_b=acs_b,
                dt_b=dt_b, lane=lane)


def _head_terms(cm, h):
    L = SSD_L
    acs, tri = cm["acs"], cm["tri"]
    lane_l = lax.broadcasted_iota(jnp.int32, (L, LANE), 1)
    sub_l = lax.broadcasted_iota(jnp.int32, (LANE, L), 0)
    col = jnp.sum(jnp.where(lane_l == h, acs, 0.0), axis=1, keepdims=True)
    row = jnp.sum(jnp.where(sub_l == h, acs.T, 0.0), axis=0, keepdims=True)
    dec = jnp.where(tri, jnp.exp(jnp.where(tri, col - row, NEG)), 0.0)
    rowi = lax.broadcasted_iota(jnp.int32, (L, 1), 0)
    last = jnp.sum(jnp.where(rowi == L - 1, col, 0.0), axis=0, keepdims=True)
    dte = jnp.exp(last - col)
    return col, dec, last, dte


def _ssd_fwd(proj, cw, cb, dtb, alog, dvec, nw):
    s = proj.shape[0]
    L = SSD_L
    nc = s // L

    def body(z_ref, xbc_ref, halo_ref, dt_ref, cw_ref, cb_ref, dtb_ref, alog_ref, d_ref, nw_ref, y_ref, ypre_ref, st_ref, ext, state):
        i = pl.program_id(0)

        @pl.when(i == 0)
        def _():
            state[...] = jnp.zeros_like(state)

        cm = _ssd_common(xbc_ref, halo_ref, dt_ref, cw_ref, cb_ref, dtb_ref, alog_ref, ext, i == 0)
        act = cm["act"]
        xs = act[:, 0:256]
        bm = (act[:, 256:384], act[:, 384:512])
        cmat = (act[:, 512:640].astype(BF16), act[:, 640:768].astype(BF16))
        xdt = xs * cm["dt_b"]
        prev = state[...]
        st_ref[...] = prev
        prev_bf = prev.astype(BF16)
        gm = [_dot(cmat[g], bm[g].astype(BF16), NT) for g in range(2)]
        lane2 = lax.broadcasted_iota(jnp.int32, (1, SSD_DIM), 1)
        rows2 = lax.broadcasted_iota(jnp.int32, (SSD_DIM, 1), 0)
        ydiag = jnp.zeros((L, SSD_DIM), F32)
        contrib = jnp.zeros((SSD_DIM, LANE), F32)
        cd_rows = jnp.zeros((SSD_DIM, 1), F32)
        for h in range(SSD_H):
            g = h // 2
            col, dec, last, dte = _head_terms(cm, h)
            mh = (lane2 >= 64 * h) & (lane2 < 64 * h + 64)
            xm = jnp.where(mh, xdt, 0.0).astype(BF16)
            ydiag += _dot((gm[g] * dec).astype(BF16), xm)
            contrib += _dot(xm, (bm[g] * dte).astype(BF16), TN)
            cd_rows += jnp.where((rows2 >= 64 * h) & (rows2 < 64 * h + 64), jnp.exp(last), 0.0)
        yo = jnp.where(lane2 < 128, _dot(cmat[0], prev_bf, NT), _dot(cmat[1], prev_bf, NT))
        y = ydiag + yo * jnp.exp(cm["acs_b"]) + xs * d_ref[...]
        state[...] = prev * cd_rows + contrib
        ypre_ref[...] = y
        zz = z_ref[...].astype(F32)
        gt = y * zz * _sig(zz)
        y_ref[...] = (gt * lax.rsqrt(jnp.mean(gt * gt, axis=-1, keepdims=True) + EPS) * nw_ref[...]).astype(BF16)

    def vec(w):
        return pl.BlockSpec((1, w), lambda i: (0, 0))

    return pl.pallas_call(
        body, name="ssd_fwd", grid=(nc,),
        out_shape=(jax.ShapeDtypeStruct((s, SSD_DIM), BF16), jax.ShapeDtypeStruct((s, SSD_DIM), F32),
                   jax.ShapeDtypeStruct((nc, SSD_DIM, LANE), F32)),
        in_specs=[pl.BlockSpec((L, SSD_DIM), lambda i: (i, 5)), pl.BlockSpec((L, SSD_CONV), lambda i: (i, 2)),
                  pl.BlockSpec((HALO, SSD_CONV), lambda i: (_prev_halo(i, L), 2)), pl.BlockSpec((L, LANE), lambda i: (i, 18)),
                  pl.BlockSpec((4, SSD_CONV), lambda i: (0, 0)), vec(SSD_CONV), vec(LANE), vec(LANE), vec(SSD_DIM), vec(SSD_DIM)],
        out_specs=(pl.BlockSpec((L, SSD_DIM), lambda i: (i, 0)), pl.BlockSpec((L, SSD_DIM), lambda i: (i, 0)),
                   pl.BlockSpec((None, SSD_DIM, LANE), lambda i: (i, 0, 0))),
        scratch_shapes=[pltpu.VMEM((L + HALO, SSD_CONV), F32), pltpu.VMEM((SSD_DIM, LANE), F32)], compiler_params=_cp("arbitrary"),
    )(proj, proj, proj, proj, cw, cb, dtb, alog, dvec, nw)


def _ssd_bwd(proj, dcat, ypre, states, cw, cb, dtb, alog, dvec, nw):
    s = proj.shape[0]
    L = SSD_L
    nc = s // L

    def body(z_ref, xbc_ref, halo_ref, dt_ref, dy_ref, ypre_ref, st_ref, cw_ref, cb_ref, dtb_ref, alog_ref, d_ref, nw_ref,
             dz_ref, dxbc_ref, ddt_ref, dcw_ref, dcb_ref, ddtb_ref, dalog_ref, dd_ref, dnw_ref, ext, ext2, carry, dstate, ddl):
        i = pl.program_id(0)
        r = nc - 1 - i

        @pl.when(i == 0)
        def _():
            for ref in (dcw_ref, dcb_ref, ddtb_ref, dalog_ref, dd_ref, dnw_ref, carry, dstate, ddl):
                ref[...] = jnp.zeros_like(ref)

        cm = _ssd_common(xbc_ref, halo_ref, dt_ref, cw_ref, cb_ref, dtb_ref, alog_ref, ext, r == 0)
        tri, expand, act = cm["tri"], cm["expand"], cm["act"]
        xs = act[:, 0:256]
        bm = (act[:, 256:384], act[:, 384:512])
        cmat = (act[:, 512:640], act[:, 640:768])
        bm_bf = [v.astype(BF16) for v in bm]
        cm_bf = [v.astype(BF16) for v in cmat]
        dt_b = cm["dt_b"]
        xdt = xs * dt_b
        xdt_bf = xdt.astype(BF16)
        ea_b = jnp.exp(cm["acs_b"])
        prev = st_ref[...]
        prev_bf = prev.astype(BF16)
        lane2 = lax.broadcasted_iota(jnp.int32, (1, SSD_DIM), 1)
        rows2 = lax.broadcasted_iota(jnp.int32, (SSD_DIM, 1), 0)
        lane_l = lax.broadcasted_iota(jnp.int32, (L, LANE), 1)
        rowi = lax.broadcasted_iota(jnp.int32, (L, 1), 0)

        y = ypre_ref[...]
        zz = z_ref[...].astype(F32)
        sz = _sig(zz)
        gt = y * zz * sz
        dgt, dwt = _rms_bwd_math(gt, nw_ref[...], dy_ref[...].astype(F32))
        dnw_ref[...] += _rowsum(dwt)
        dy = dgt * zz * sz
        dz_ref[...] = (dgt * y * sz * (1.0 + zz * (1.0 - sz))).astype(BF16)

        ddl[0:1, :] += _rowsum(dy * xs)
        dxs = dy * d_ref[...]

        yo = jnp.where(lane2 < 128, _dot(cm_bf[0], prev_bf, NT), _dot(cm_bf[1], prev_bf, NT))
        dacs_b = dy * yo * ea_b
        dyo = dy * ea_b
        dyo_g = (jnp.where(lane2 < 128, dyo, 0.0).astype(BF16), jnp.where(lane2 >= 128, dyo, 0.0).astype(BF16))
        dc = [_dot(dyo_g[g], prev_bf) for g in range(2)]
        dprev = _dot(dyo_g[0], cm_bf[0], TN) + _dot(dyo_g[1], cm_bf[1], TN)

        gm = [_dot(cm_bf[g], bm_bf[g], NT) for g in range(2)]
        dgm = [jnp.zeros((L, L), F32), jnp.zeros((L, L), F32)]
        db = [jnp.zeros((L, LANE), F32), jnp.zeros((L, LANE), F32)]
        dxdt = jnp.zeros((L, SSD_DIM), F32)
        dacs = jnp.zeros((L, LANE), F32)
        dlast = jnp.zeros((1, LANE), F32)
        cd_rows = jnp.zeros((SSD_DIM, 1), F32)
        dst = dstate[...]
        dst_bf = dst.astype(BF16)
        dsp = dst * prev
        ones = jnp.ones((L, LANE), F32)
        for h in range(SSD_H):
            g = h // 2
            col, dec, last, dte = _head_terms(cm, h)
            mh = (lane2 >= 64 * h) & (lane2 < 64 * h + 64)
            rh = (rows2 >= 64 * h) & (rows2 < 64 * h + 64)
            sc = gm[g] * dec
            xm = jnp.where(mh, xdt, 0.0).astype(BF16)
            dym = jnp.where(mh, dy, 0.0).astype(BF16)
            dsc = _dot(dym, xdt_bf, NT)
            dxdt += _dot(sc.astype(BF16), dym, TN)
            dgm[g] += dsc * dec
            dd = dsc * sc
            rs = jnp.sum(dd, axis=1, keepdims=True)
            cs = _dot(dd, ones, TN, precision=HI)
            dacs += jnp.where(lane_l == h, rs - cs, 0.0)
            bd = (bm[g] * dte).astype(BF16)
            dxdt += jnp.where(mh, _dot(bd, dst_bf, NT), 0.0)
            dbd = _dot(xm, dst_bf)
            db[g] += dbd * dte
            tt = jnp.sum(dbd * bm[g], axis=1, keepdims=True) * dte
            dacs += jnp.where(lane_l == h, -tt, 0.0)
            cdh = jnp.exp(last)
            dcd = jnp.sum(jnp.sum(jnp.where(rh, dsp, 0.0), axis=1, keepdims=True), axis=0, keepdims=True)
            dlast += jnp.where(cm["lane"] == h, jnp.sum(tt, axis=0, keepdims=True) + dcd * cdh, 0.0)
            cd_rows += jnp.where(rh, cdh, 0.0)
        dacs += jnp.where(rowi == L - 1, dlast, 0.0)
        dacs += _dot(dacs_b, expand, NT, precision=HI)
        dstate[...] = dprev + dst * cd_rows

        for g in range(2):
            dgb = dgm[g].astype(BF16)
            dc[g] += _dot(dgb, bm_bf[g])
            db[g] += _dot(dgb, cm_bf[g], TN)

        dadt = _dot(tri.astype(F32), dacs, TN, precision=HI)
        ddtv = dadt * cm["av"] + _dot(dxdt * xs, expand, NT, precision=HI)
        dalog_ref[...] += _rowsum(dadt * cm["dtv"]) * cm["av"]
        dxs += dxdt * dt_b
        draw = jnp.where(cm["m4"], ddtv * _sig(cm["raw"]), 0.0)
        ddtb_ref[...] += _rowsum(draw)
        ddt_ref[...] = draw.astype(BF16)

        dact = jnp.concatenate([dxs, db[0], db[1], dc[0], dc[1]], axis=1)
        sg, pre = cm["sg"], cm["pre"]
        dpre = dact * sg * (1.0 + pre * (1.0 - sg))
        dcb_ref[...] += _rowsum(dpre)
        for kk in range(4):
            dcw_ref[kk:kk + 1, :] += _rowsum(dpre * ext[pl.ds(HALO - 3 + kk, L), :])
        ext2[0:L, :] = dpre
        ext2[L:L + HALO, :] = carry[...]
        dx = cw_ref[3:4, :] * ext2[pl.ds(0, L), :]
        for kk in range(3):
            dx = dx + cw_ref[kk:kk + 1, :] * ext2[pl.ds(3 - kk, L), :]
        dxbc_ref[...] = dx.astype(BF16)
        carry[...] = dpre[0:HALO, :]

        @pl.when(i == nc - 1)
        def _():
            dd_ref[...] = _dot(ddl[...], expand, NT, precision=HI)

    def vec(w):
        return pl.BlockSpec((1, w), lambda i: (0, 0))

    def rv(i):
        return nc - 1 - i

    return pl.pallas_call(
        body, name="ssd_bwd", grid=(nc,),
        out_shape=(jax.ShapeDtypeStruct((s, SSD_DIM), BF16), jax.ShapeDtypeStruct((s, SSD_CONV), BF16), jax.ShapeDtypeStruct((s, LANE), BF16),
                   jax.ShapeDtypeStruct((4, SSD_CONV), F32), jax.ShapeDtypeStruct((1, SSD_CONV), F32), jax.ShapeDtypeStruct((1, LANE), F32),
                   jax.ShapeDtypeStruct((1, LANE), F32), jax.ShapeDtypeStruct((8, LANE), F32), jax.ShapeDtypeStruct((1, SSD_DIM), F32)),
        in_specs=[pl.BlockSpec((L, SSD_DIM), lambda i: (rv(i), 5)), pl.BlockSpec((L, SSD_CONV), lambda i: (rv(i), 2)),
                  pl.BlockSpec((HALO, SSD_CONV), lambda i: (_prev_halo(rv(i), L), 2)), pl.BlockSpec((L, LANE), lambda i: (rv(i), 18)),
                  pl.BlockSpec((L, SSD_DIM), lambda i: (rv(i), 5)), pl.BlockSpec((L, SSD_DIM), lambda i: (rv(i), 0)),
                  pl.BlockSpec((None, SSD_DIM, LANE), lambda i: (rv(i), 0, 0)),
                  pl.BlockSpec((4, SSD_CONV), lambda i: (0, 0)), vec(SSD_CONV), vec(LANE), vec(LANE), vec(SSD_DIM), vec(SSD_DIM)],
        out_specs=(pl.BlockSpec((L, SSD_DIM), lambda i: (rv(i), 0)), pl.BlockSpec((L, SSD_CONV), lambda i: (rv(i), 0)),
                   pl.BlockSpec((L, LANE), lambda i: (rv(i), 0)), pl.BlockSpec((4, SSD_CONV), lambda i: (0, 0)), vec(SSD_CONV),
                   vec(LANE), vec(LANE), pl.BlockSpec((8, LANE), lambda i: (0, 0)), vec(SSD_DIM)),
        scratch_shapes=[pltpu.VMEM((L + HALO, SSD_CONV), F32), pltpu.VMEM((L + HALO, SSD_CONV), F32), pltpu.VMEM((HALO, SSD_CONV), F32),
                        pltpu.VMEM((SSD_DIM, LANE), F32), pltpu.VMEM((8, SSD_DIM), F32)],
        compiler_params=_cp("arbitrary"),
    )(proj, proj, proj, proj, dcat, ypre, states, cw, cb, dtb, alog, dvec, nw)


def _adamw(parts, w, m, v, name):
    nl, r, c = w.shape
    tr = r
    for cand in (256, 128, 64, 32, 16, 8):
        if r % cand == 0 and (cand * c * 4) <= 2 * 1024 * 1024:
            tr = cand
            break
    c1 = 1.0 - B1 ** STEP
    c2 = 1.0 - B2 ** STEP

    def body(p_ref, w_ref, m_ref, v_ref, g_ref, d_ref, nm_ref, nv_ref):
        g = p_ref[0].astype(F32)
        for dev in range(1, NDEV):
            g = g + p_ref[dev].astype(F32)
        mn = B1 * m_ref[...] + (1.0 - B1) * g
        vn = B2 * v_ref[...] + (1.0 - B2) * (g * g)
        g_ref[...] = g
        nm_ref[...] = mn
        nv_ref[...] = vn
        d_ref[...] = -LR * ((mn / c1) / (jnp.sqrt(vn / c2) + AEPS) + WD * w_ref[...])

    blk = pl.BlockSpec((None, tr, c), lambda l, i: (l, i, 0))
    out = jax.ShapeDtypeStruct((nl, r, c), F32)
    return pl.pallas_call(
        body, name=name, grid=(nl, r // tr), out_shape=(out, out, out, out),
        in_specs=[pl.BlockSpec((NDEV, None, tr, c), lambda l, i: (0, l, i, 0)), blk, blk, blk], out_specs=(blk, blk, blk, blk),
        compiler_params=_cp("parallel", "parallel"),
    )(parts, w, m, v)


def _pad_win(w):
    z = lambda n: jnp.zeros((w.shape[0], n), w.dtype)
    return jnp.concatenate([w[:, :384], z(64), w[:, 384:416], z(32), w[:, 416:], z(124)], axis=1)


def _unpad_win(g):
    return jnp.concatenate([g[:, :384], g[:, 448:480], g[:, 512:2308]], axis=1)


def _pad_wout(w):
    att = jnp.pad(w[:512].reshape(HEADS, 64, D), ((0, 0), (64, 0), (0, 0))).reshape(HEADS * LANE, D)
    return jnp.concatenate([att, w[512:]], axis=0)


def _unpad_wout(g):
    att = g[:HEADS * LANE].reshape(HEADS, LANE, D)[:, 64:, :].reshape(512, D)
    return jnp.concatenate([att, g[HEADS * LANE:]], axis=0)


def _lanes(v, n=LANE):
    return jnp.pad(v, (0, n - v.shape[0])).reshape(1, n)


def _prep_ffn(big):
    return {"wout": _pad_wout(big["w_out"].reshape(1024, D)), "wup": big["ffn_w_up"], "fcw": big["ffn_conv_w"].astype(F32),
            "wdown": big["ffn_w_down"].reshape(4, FB, D)}


def _prep_layer(big, small, l):
    p = _prep_ffn(big) if "w_out" in big else {}
    p["win"] = _pad_win(big["w_in"].reshape(D, 2212))
    p["wq"] = jnp.pad(big["mla_w_q_up"], ((0, 0), (0, 0), (0, LANE - 96)))
    p["wkv"] = big["mla_w_kv_up"]
    p["scw"] = big["sc_conv_w"].astype(F32).transpose(1, 0, 2).reshape(3, SC)
    p["ssdcw"] = big["ssd_conv_w"].astype(F32).transpose(1, 0, 2).reshape(4, SSD_CONV)
    for nm in ("norm_mix_pre", "norm_mix_post", "norm_ffn_pre", "norm_ffn_post", "mla_q_norm", "mla_kv_norm", "ssd_conv_b", "ssd_norm"):
        p[nm] = small[nm][l].reshape(1, -1)
    p["dtb"] = _lanes(small["ssd_dt_bias"][l])
    p["alog"] = _lanes(small["ssd_a_log"][l])
    p["dvec"] = jnp.repeat(small["ssd_d"][l], 64).reshape(1, SSD_DIM)
    p["fcb"] = small["ffn_conv_b"][l].reshape(NDEV, 1, FB)
    return p


def _rope_tables(positions):
    inv_freq = 1.0 / (ROPE_THETA ** (jnp.arange(0, ROPE, 2, dtype=F32) / ROPE))
    ang = positions.astype(F32)[:, None] * inv_freq
    cos, sin = jnp.cos(ang), jnp.sin(ang)
    s = positions.shape[0]
    z = lambda n: jnp.zeros((s, n), F32)
    tc = jnp.concatenate([jnp.ones((s, 64), F32), cos, cos, z(32)], axis=1)
    ta = jnp.concatenate([z(64), -sin, z(48)], axis=1)
    tb = jnp.concatenate([z(80), sin, z(32)], axis=1)
    return tc, ta, tb


def _layer_fwd(xv, p, tabs, prefetch=None, prep_rest=None, h=None, next_norm=None):
    if h is None:
        h = _rms(xv, p["norm_mix_pre"], BF16, "rms_pre")
    proj = _mm_rows("in_proj", h, p["win"], BF16, NN)
    q, k, kv = _mla_prep(proj, tabs, p["mla_q_norm"], p["mla_kv_norm"], p["wq"], p["wkv"])
    o, lse, gathered = _flash_fwd(q, k, kv, prefetch)
    if prep_rest is not None:
        p = {**p, **prep_rest(gathered)}
    yconv = _sconv_fwd(proj, p["scw"])
    yssd, ypre, states = _ssd_fwd(proj, p["ssdcw"], p["ssd_conv_b"], p["dtb"], p["alog"], p["dvec"], p["ssd_norm"])
    cat = jnp.concatenate([o, yconv, yssd], axis=1)
    mixed = _mm_rows("out_proj", cat, p["wout"], BF16, NN)
    x1, h2 = _add_rms(xv, mixed, p["norm_mix_post"], "add_rms", p["norm_ffn_pre"])
    upre = _mm_up(h2, p["wup"])
    act = _ffn_act(upre, p["fcw"], p["fcb"])
    f = _mm_down(act, p["wdown"])
    x2 = _add_rms(x1, f, p["norm_ffn_post"], "add_rms", next_norm)
    saved = dict(x=xv, h=h, proj=proj, q=q, k=k, kv=kv, lse=lse, ypre=ypre, states=states, cat=cat, mixed=mixed, x1=x1, h2=h2,
                 upre=upre, act=act, f=f)
    return x2, saved, p, gathered


def _pack_grads(grads, group_ids):
    return [_group_pack(GROUPS[gi], lambda n: grads[n].reshape((NDEV,) + _rows2(n, True)), (NDEV,)) for gi in group_ids]


def _layer_bwd(dx2, sv, p, tabs, exchange=False, pending=None, head=None, below=None):
    df, g_nfpo = head if head is not None else _rms_bwd(sv["f"], p["norm_ffn_post"], dx2, None, BF16, "rms_bwd_post")
    dact = _mm_dact(df, p["wdown"])
    g_wdown = _mm_dwdown(sv["act"], df)
    dupre, g_fcb, g_fcw = _ffn_bwd(sv["upre"], dact, p["fcw"], p["fcb"])
    dh2 = _mm_dh2(dupre, p["wup"])
    g_wup = _mm_dwup(sv["h2"], dupre)
    dx1, dmixed, g_nfp, g_nmpo = _rms_bwd2(sv["x1"], p["norm_ffn_pre"], dh2, dx2, sv["mixed"], p["norm_mix_post"])
    dcat = _mm_rows("dcat", dmixed, p["wout"], BF16, NT)
    g_wout = _mm_wgrad("dw_out", sv["cat"], dmixed, BF16)
    big = {
        "w_out": _unpad_wout(g_wout).reshape(NDEV, 128, D),
        "ffn_w_up": g_wup,
        "ffn_conv_w": g_fcw.astype(BF16),
        "ffn_w_down": g_wdown.reshape(NDEV, 352, D),
    }
    outgoing = _pack_grads(big, FFN_SIDE) + (pending or []) if exchange else None
    dq, dk, dv, received = _flash_bwd(sv["q"], sv["k"], sv["kv"], sv["cat"], dcat, sv["lse"], outgoing)
    dcq, dckv, dkr, g_wq, g_wkv, g_qn, g_kvn = _mla_prep_bwd(sv["proj"], tabs, p["mla_q_norm"], p["mla_kv_norm"], p["wq"], p["wkv"], dq, dk, dv)
    dscb, dscc, dsch, g_scw = _sconv_bwd(sv["proj"], dcat, p["scw"])
    dz, dxbc, ddt, g_cw, g_cb, g_dtb, g_alog, g_d, g_nw = _ssd_bwd(
        sv["proj"], dcat, sv["ypre"], sv["states"], p["ssdcw"], p["ssd_conv_b"], p["dtb"], p["alog"], p["dvec"], p["ssd_norm"])
    dproj = jnp.concatenate([dcq, dckv, dkr, dscb, dscc, dsch, dz, dxbc, ddt], axis=1)
    dh = _mm_rows("dh", dproj, p["win"], BF16, NT)
    g_win = _mm_wgrad("dw_in", sv["h"], dproj, BF16)
    if below is not None:
        dx, df_below, g_nmp, g_below = _rms_bwd2(sv["x"], p["norm_mix_pre"], dh, dx1, *below)
        head_below = (df_below, g_below)
    else:
        (dx, g_nmp), head_below = _rms_bwd(sv["x"], p["norm_mix_pre"], dh, dx1, F32, "rms_bwd_pre"), None
    big.update({
        "w_in": _unpad_win(g_win).reshape(NDEV, 128, 2212),
        "mla_w_q_up": g_wq[:, :, :96].astype(BF16),
        "mla_w_kv_up": g_wkv.astype(BF16),
        "sc_conv_w": g_scw.reshape(3, NDEV, 32).transpose(1, 0, 2).astype(BF16),
        "ssd_conv_w": g_cw.reshape(4, NDEV, 96).transpose(1, 0, 2).astype(BF16),
    })
    small = {
        "norm_mix_pre": g_nmp[0], "norm_mix_post": g_nmpo[0], "norm_ffn_pre": g_nfp[0], "norm_ffn_post": g_nfpo[0],
        "mla_q_norm": g_qn[0], "mla_kv_norm": g_kvn[0], "ssd_conv_b": g_cb[0], "ssd_dt_bias": g_dtb[0, :SSD_H],
        "ssd_a_log": g_alog[0, :SSD_H], "ssd_d": g_d[0, :SSD_H], "ssd_norm": g_nw[0], "ffn_conv_b": g_fcb.reshape(-1),
    }
    return dx, big, small, received, head_below


def _local_step(xv, positions, target, layers):
    tabs = _rope_tables(positions)
    saved = []
    for p in layers:
        xv, sv, _, _ = _layer_fwd(xv, p, tabs)
        saved.append(sv)
    loss, dx = _loss_head(xv, target)
    bigs, smalls = [None] * DEPTH, [None] * DEPTH
    head = None
    for l in reversed(range(len(layers))):
        below = (saved[l - 1]["f"], layers[l - 1]["norm_ffn_post"]) if l > 0 else None
        dx, bigs[l], smalls[l], _, head = _layer_bwd(dx, saved[l], layers[l], tabs, head=head, below=below)
    return loss[0, 0], dx, bigs, smalls


def _pack_rows(flat, lead, width, mult):
    n = flat.shape[-1]
    rows = -(-n // (width * mult)) * mult
    pad = [(0, 0)] * (flat.ndim - 1) + [(0, rows * width - n)]
    return jnp.pad(flat, pad).reshape(lead + (rows, width))


def _rows2(n, layer=False):
    shape = SHAPES[n][1:] if layer else SHAPES[n]
    return (math.prod(shape[:-1]), shape[-1])


def _group_pack(group, get, lead):
    width, names = group
    pieces = []
    for n in names:
        rows, cols = _rows2(n, True)
        pad = [(0, 0)] * len(lead) + [(0, -rows % 16), (0, width - cols)]
        pieces.append(jnp.pad(get(n), pad))
    return pieces[0] if len(pieces) == 1 else jnp.concatenate(pieces, axis=len(lead))


def _group_unpack(group, buf):
    _, names = group
    res, off = {}, 0
    for n in names:
        rows, cols = _rows2(n, True)
        res[n] = buf[:, off:off + rows, :cols]
        off += rows + (-rows % 16)
    return res


def kernel(x, positions, norm_mix_pre, norm_mix_post, norm_ffn_pre, norm_ffn_post, w_in, mla_q_norm, mla_w_q_up, mla_kv_norm, mla_w_kv_up, sc_conv_w, ssd_conv_w, ssd_conv_b, ssd_dt_bias, ssd_a_log, ssd_d, ssd_norm, w_out, ffn_w_up, ffn_conv_w, ffn_conv_b, ffn_w_down, loss_target, m_norm_mix_pre, m_norm_mix_post, m_norm_ffn_pre, m_norm_ffn_post, m_w_in, m_mla_q_norm, m_mla_w_q_up, m_mla_kv_norm, m_mla_w_kv_up, m_sc_conv_w, m_ssd_conv_w, m_ssd_conv_b, m_ssd_dt_bias, m_ssd_a_log, m_ssd_d, m_ssd_norm, m_w_out, m_ffn_w_up, m_ffn_conv_w, m_ffn_conv_b, m_ffn_w_down, v_norm_mix_pre, v_norm_mix_post, v_norm_ffn_pre, v_norm_ffn_post, v_w_in, v_mla_q_norm, v_mla_w_q_up, v_mla_kv_norm, v_mla_w_kv_up, v_sc_conv_w, v_ssd_conv_w, v_ssd_conv_b, v_ssd_dt_bias, v_ssd_a_log, v_ssd_d, v_ssd_norm, v_w_out, v_ffn_w_up, v_ffn_conv_w, v_ffn_conv_b, v_ffn_w_down):
    given = dict(locals())
    w = {n: given[n] for n in WEIGHTS}
    m = {n: given["m_" + n] for n in WEIGHTS}
    v = {n: given["v_" + n] for n in WEIGHTS}

    def shards(l, group_ids):
        return [_group_pack(GROUPS[gi], lambda n: w[n][l].astype(BF16).reshape(_rows2(n, True)), ()) for gi in group_ids]

    def unpacked(bufs, group_ids):
        big = {}
        for gi, buf in zip(group_ids, bufs):
            for n, piece in _group_unpack(GROUPS[gi], buf).items():
                big[n] = piece.reshape((NDEV,) + SHAPES[n][1:])
        return big

    small_w = {n: w[n] for n, _ in SMALL}
    tabs = _rope_tables(positions[0])
    xv, h, layers, saved = x[0], None, [], []
    att = _all_gather(shards(0, ATT_SIDE), "gather_weights")
    for l in range(DEPTH):
        prefetch = shards(l, FFN_SIDE) + (shards(l + 1, ATT_SIDE) if l + 1 < DEPTH else [])
        nxt = w["norm_mix_pre"][l + 1].reshape(1, D) if l + 1 < DEPTH else None
        xv, sv, p, gathered = _layer_fwd(xv, _prep_layer(unpacked(att, ATT_SIDE), small_w, l), tabs, prefetch,
                                         lambda got: _prep_ffn(unpacked(got[:len(FFN_SIDE)], FFN_SIDE)), h, nxt)
        xv, h = xv if nxt is not None else (xv, None)
        att = gathered[len(FFN_SIDE):]
        layers.append(p)
        saved.append(sv)
    loss, dx = _loss_head(xv, loss_target[0])
    loss = lax.psum(loss[0, 0], ("x", "y", "c"))

    smalls, pending, head = [None] * DEPTH, None, None
    recvs = [[None] * len(GROUPS) for _ in range(DEPTH)]
    for l in reversed(range(DEPTH)):
        below = (saved[l - 1]["f"], layers[l - 1]["norm_ffn_post"]) if l > 0 else None
        dx, grads, smalls[l], received, head = _layer_bwd(dx, saved[l], layers[l], tabs, True, pending, head, below)
        for pos, gi in enumerate(FFN_SIDE):
            recvs[l][gi] = received[pos]
        if pending is not None:
            for pos, gi in enumerate(ATT_SIDE):
                recvs[l + 1][gi] = received[len(FFN_SIDE) + pos]
        pending = _pack_grads(grads, ATT_SIDE)
    for gi, buf in zip(ATT_SIDE, _all_to_all(pending, "exchange_grads")):
        recvs[0][gi] = buf
    out = {}
    for gi, g in enumerate(GROUPS):
        per_layer = [_group_unpack(g, recvs[l][gi]) for l in range(DEPTH)]
        for n in g[1]:
            parts = jnp.stack([per_layer[l][n] for l in range(DEPTH)], axis=1)
            out[n] = _adamw(parts, w[n], m[n], v[n], "adamw_" + n)

    sflat = jnp.concatenate([jnp.stack([smalls[l][n] for l in range(DEPTH)]).reshape(-1) for n, _ in SMALL])
    sparts = _all_gather([_pack_rows(sflat, (), LANE, 8)], "gather_small_grads")[0]
    pk = lambda d: _pack_rows(jnp.concatenate([d[n].reshape(-1) for n, _ in SMALL]), (), LANE, 8)
    res = _adamw(sparts[:, None], pk(w)[None], pk(m)[None], pk(v)[None], "adamw_small")
    off = 0
    for n, width in SMALL:
        out[n] = [a.reshape(-1)[off:off + DEPTH * width].reshape(DEPTH, width) for a in res]
        off += DEPTH * width

    return (loss, dx[None], *[out[n][0] for n in WEIGHTS], *[out[n][1] for n in WEIGHTS],
            *[out[n][2] for n in WEIGHTS], *[out[n][3] for n in WEIGHTS])
```

```python
import functools
import math

import jax
import jax.numpy as jnp
from jax import lax
from jax.experimental import pallas as pl
from jax.experimental.pallas import tpu as pltpu

F32 = jnp.float32
BF16 = jnp.bfloat16

D = 1024
DEPTH = 4
NDEV = 8
HEADS = 8
QL = 256
KVL = 128
ROPE = 32
NOPE = 64
SC = 256
SSD_DIM = 256
SSD_CONV = 768
SSD_H = 4
SSD_L = 128
FFN = 2816
FB = 704
EPS = 1e-6
ROPE_THETA = 10000.0
ATT_SCALE = 96 ** -0.5
LOG2E = 1.4426950408889634
LR, B1, B2, AEPS, WD, STEP = 0.001, 0.9, 0.999, 1e-08, 0.01, 10

PW = 2432
CATW = 1536

ROW_TILE = 512
ROW_CHUNK = 16
NORM_CHUNK = 32
NORM_TILE = 1024
MM_TILE = 1024
ATT_TILE = 256
ATT_QW = 2
FWD_HEADS = 4
BWD_HEADS = 2
HALO = 16
LANE = 128
NEG = -1e30
HI = lax.Precision.HIGHEST
NN = (((1,), (0,)), ((), ()))
NT = (((1,), (1,)), ((), ()))
TN = (((0,), (0,)), ((), ()))
VMEM_LIMIT = 56 * 1024 * 1024

SHARDED = (
    ("w_in", (4, 128, 2212)),
    ("mla_w_q_up", (4, 256, 96)),
    ("mla_w_kv_up", (4, 128, 128)),
    ("sc_conv_w", (4, 3, 32)),
    ("ssd_conv_w", (4, 4, 96)),
    ("w_out", (4, 128, 1024)),
    ("ffn_w_up", (4, 1024, 704)),
    ("ffn_conv_w", (4, 3, 704)),
    ("ffn_w_down", (4, 352, 1024)),
)
SHAPES = dict(SHARDED)
GROUPS = (
    (2212, ("w_in",)),
    (1024, ("w_out",)),
    (704, ("ffn_w_up",)),
    (96, ("mla_w_q_up", "ssd_conv_w", "sc_conv_w")),
    (128, ("mla_w_kv_up",)),
    (1024, ("ffn_w_down",)),
    (704, ("ffn_conv_w",)),
)
ATT_SIDE = (0, 3, 4)
FFN_SIDE = (1, 2, 5, 6)
SMALL = (
    ("norm_mix_pre", 1024), ("norm_mix_post", 1024), ("norm_ffn_pre", 1024), ("norm_ffn_post", 1024),
    ("mla_q_norm", 256), ("mla_kv_norm", 128), ("ssd_conv_b", 768), ("ssd_dt_bias", 4), ("ssd_a_log", 4),
    ("ssd_d", 4), ("ssd_norm", 256), ("ffn_conv_b", 5632),
)
WEIGHTS = ("norm_mix_pre", "norm_mix_post", "norm_ffn_pre", "norm_ffn_post", "w_in", "mla_q_norm", "mla_w_q_up",
           "mla_kv_norm", "mla_w_kv_up", "sc_conv_w", "ssd_conv_w", "ssd_conv_b", "ssd_dt_bias", "ssd_a_log", "ssd_d",
           "ssd_norm", "w_out", "ffn_w_up", "ffn_conv_w", "ffn_conv_b", "ffn_w_down")


def _dot(a, b, dims=NN, precision=None):
    return lax.dot_general(a, b, dims, precision=precision, preferred_element_type=F32)


def _sig(v):
    return 1.0 / (1.0 + jnp.exp(-v))


def _cp(*sem):
    return pltpu.CompilerParams(dimension_semantics=sem, vmem_limit_bytes=VMEM_LIMIT)


def _rowsum(v):
    return jnp.sum(v, axis=0, keepdims=True)


def _prev_halo(i, ts):
    return jnp.maximum(i * (ts // HALO) - 1, 0)


def _next_halo(i, ts, n):
    return jnp.minimum((i + 1) * (ts // HALO), n * (ts // HALO) - 1)


def _gather_plan(x_refs, out_refs, send_sems, recv_sems, local_sems):
    n = len(x_refs)
    x, y, cc = lax.axis_index("x"), lax.axis_index("y"), lax.axis_index("c")
    me, sibling = (x, y, cc), (x, y, 1 - cc)
    chips = [(1 - x, y), (x, 1 - y), (1 - x, 1 - y)]

    def rows(t, px, py, pc):
        return out_refs[t].at[4 * px + 2 * py + pc]

    def copy(t, k, block, to, own=False):
        return pltpu.make_async_remote_copy(
            src_ref=x_refs[t] if own else rows(t, *block), dst_ref=rows(t, *block),
            send_sem=send_sems.at[7 * t + k], recv_sem=recv_sems.at[7 * t + k], device_id=to, device_id_type=pl.DeviceIdType.MESH)

    def local(t):
        return pltpu.make_async_copy(x_refs[t], rows(t, *me), local_sems.at[t])

    def start():
        for t in range(n):
            local(t).start()
            copy(t, 0, me, sibling, own=True).start()
            for j, chip in enumerate(chips):
                copy(t, 1 + j, me, (*chip, cc), own=True).start()

    def finish():
        for j, chip in enumerate(chips):
            for t in range(n):
                copy(t, 1 + j, (*chip, cc), me).wait_recv()
                copy(t, 4 + j, (*chip, cc), sibling).start()
        for t in range(n):
            copy(t, 0, sibling, me).wait_recv()
            for j, chip in enumerate(chips):
                copy(t, 4 + j, (*chip, 1 - cc), me).wait_recv()
        for t in range(n):
            copy(t, 0, me, sibling, own=True).wait_send()
            for j, chip in enumerate(chips):
                copy(t, 1 + j, me, (*chip, cc), own=True).wait_send()
                copy(t, 4 + j, (*chip, cc), sibling).wait_send()
            local(t).wait()

    return start, finish


def _exchange_plan(x_refs, out_refs, send_sems, recv_sems, local_sems):
    n = len(x_refs)
    x, y, cc = lax.axis_index("x"), lax.axis_index("y"), lax.axis_index("c")
    me = 4 * x + 2 * y + cc

    def copies():
        res = [pltpu.make_async_copy(x_refs[t].at[me], out_refs[t].at[me], local_sems.at[t]) for t in range(n)]
        for k in range(1, NDEV):
            px = 1 - x if k & 4 else x
            py = 1 - y if k & 2 else y
            pc = 1 - cc if k & 1 else cc
            peer = 4 * px + 2 * py + pc
            for t in range(n):
                res.append(pltpu.make_async_remote_copy(
                    src_ref=x_refs[t].at[peer], dst_ref=out_refs[t].at[me], send_sem=send_sems.at[7 * t + k - 1],
                    recv_sem=recv_sems.at[7 * t + k - 1], device_id=(px, py, pc), device_id_type=pl.DeviceIdType.MESH))
        return res

    def start():
        for cp in copies():
            cp.start()

    def finish():
        for cp in copies():
            cp.wait()

    return start, finish


def _comm_scratch(n):
    return [pltpu.SemaphoreType.DMA((7 * n,)), pltpu.SemaphoreType.DMA((7 * n,)), pltpu.SemaphoreType.DMA((n,))]


ANY = pl.BlockSpec(memory_space=pl.ANY)


def _all_gather(xs, name):
    n = len(xs)

    def body(*refs):
        start, finish = _gather_plan(refs[:n], refs[n:2 * n], *refs[2 * n:])
        start()
        finish()

    return pl.pallas_call(
        body, name=name, out_shape=[jax.ShapeDtypeStruct((NDEV,) + a.shape, a.dtype) for a in xs],
        in_specs=[ANY] * n, out_specs=[ANY] * n, scratch_shapes=_comm_scratch(n),
    )(*xs)


def _mm(name, a, b, out_shape, grid, a_spec, b_spec, o_spec, dims, acc_shape):
    nk = grid[2]

    def single(a_ref, b_ref, o_ref):
        o_ref[...] = _dot(a_ref[...], b_ref[...], dims).astype(o_ref.dtype)

    if nk == 1:
        return pl.pallas_call(
            single, name=name, grid=grid, out_shape=out_shape, in_specs=[a_spec, b_spec], out_specs=o_spec,
            compiler_params=_cp("parallel", "parallel", "arbitrary"),
        )(a, b)

    def body(a_ref, b_ref, o_ref, acc_ref):
        k = pl.program_id(2)

        @pl.when(k == 0)
        def _():
            acc_ref[...] = jnp.zeros_like(acc_ref)

        acc_ref[...] += _dot(a_ref[...], b_ref[...], dims)

        @pl.when(k == nk - 1)
        def _():
            o_ref[...] = acc_ref[...].astype(o_ref.dtype)

    return pl.pallas_call(
        body, name=name, grid=grid, out_shape=out_shape, in_specs=[a_spec, b_spec], out_specs=o_spec,
        scratch_shapes=[pltpu.VMEM(acc_shape, F32)], compiler_params=_cp("parallel", "parallel", "arbitrary"),
    )(a, b)


def _mm_rows(name, a, w, out_dtype, dims):
    s, k = a.shape
    n = w.shape[1] if dims == NN else w.shape[0]
    tm = min(MM_TILE, s)
    return _mm(name, a, w, jax.ShapeDtypeStruct((s, n), out_dtype), (s // tm, 1, 1),
               pl.BlockSpec((tm, k), lambda i, j, kk: (i, 0)), pl.BlockSpec(w.shape, lambda i, j, kk: (0, 0)),
               pl.BlockSpec((tm, n), lambda i, j, kk: (i, 0)), dims, (tm, n))


def _mm_wgrad(name, a, g, out_dtype):
    s, m = a.shape
    n = g.shape[1]
    tk = min(MM_TILE, s)
    return _mm(name, a, g, jax.ShapeDtypeStruct((m, n), out_dtype), (1, 1, s // tk),
               pl.BlockSpec((tk, m), lambda i, j, kk: (kk, 0)), pl.BlockSpec((tk, n), lambda i, j, kk: (kk, 0)),
               pl.BlockSpec((m, n), lambda i, j, kk: (0, 0)), TN, (m, n))


def _mm_up(h2, wup):
    s = h2.shape[0]
    tm = min(MM_TILE, s)
    return _mm("ffn_up", h2, wup, jax.ShapeDtypeStruct((NDEV, s, FB), BF16), (NDEV, s // tm, 1),
               pl.BlockSpec((tm, D), lambda j, i, kk: (i, 0)), pl.BlockSpec((None, D, FB), lambda j, i, kk: (j, 0, 0)),
               pl.BlockSpec((None, tm, FB), lambda j, i, kk: (j, i, 0)), NN, (tm, FB))


def _mm_down(act, wdown):
    s = act.shape[1]
    tm = min(MM_TILE, s)
    return _mm("ffn_down", act, wdown, jax.ShapeDtypeStruct((s, D), BF16), (s // tm, 1, 4),
               pl.BlockSpec((None, tm, FB), lambda i, j, kk: (kk, i, 0)), pl.BlockSpec((None, FB, D), lambda i, j, kk: (kk, 0, 0)),
               pl.BlockSpec((tm, D), lambda i, j, kk: (i, 0)), NN, (tm, D))


def _mm_dact(df, wdown):
    s = df.shape[0]
    tm = min(MM_TILE, s)
    return _mm("ffn_dact", df, wdown, jax.ShapeDtypeStruct((4, s, FB), BF16), (4, s // tm, 1),
               pl.BlockSpec((tm, D), lambda j, i, kk: (i, 0)), pl.BlockSpec((None, FB, D), lambda j, i, kk: (j, 0, 0)),
               pl.BlockSpec((None, tm, FB), lambda j, i, kk: (j, i, 0)), NT, (tm, FB))


def _mm_dwdown(act, df):
    s = df.shape[0]
    tk = min(MM_TILE, s)
    return _mm("ffn_dwdown", act, df, jax.ShapeDtypeStruct((4, FB, D), BF16), (4, 1, s // tk),
               pl.BlockSpec((None, tk, FB), lambda j, i, kk: (j, kk, 0)), pl.BlockSpec((tk, D), lambda j, i, kk: (kk, 0)),
               pl.BlockSpec((None, FB, D), lambda j, i, kk: (j, 0, 0)), TN, (FB, D))


def _mm_dh2(dupre, wup):
    s = dupre.shape[1]
    tm = min(MM_TILE, s)
    return _mm("ffn_dh2", dupre, wup, jax.ShapeDtypeStruct((s, D), BF16), (s // tm, 1, NDEV),
               pl.BlockSpec((None, tm, FB), lambda i, j, kk: (kk, i, 0)), pl.BlockSpec((None, D, FB), lambda i, j, kk: (kk, 0, 0)),
               pl.BlockSpec((tm, D), lambda i, j, kk: (i, 0)), NT, (tm, D))


def _mm_dwup(h2, dupre):
    s = h2.shape[0]
    tk = min(MM_TILE, s)
    return _mm("ffn_dwup", h2, dupre, jax.ShapeDtypeStruct((NDEV, D, FB), BF16), (NDEV, 1, s // tk),
               pl.BlockSpec((tk, D), lambda j, i, kk: (kk, 0)), pl.BlockSpec((None, tk, FB), lambda j, i, kk: (j, kk, 0)),
               pl.BlockSpec((None, D, FB), lambda j, i, kk: (j, 0, 0)), TN, (D, FB))


def _rms(xv, w, out_dtype, name):
    s, d = xv.shape
    ts = min(ROW_TILE, s)

    def body(x_ref, w_ref, o_ref):
        for r0 in range(0, ts, NORM_CHUNK):
            rows = pl.ds(r0, NORM_CHUNK)
            xf = x_ref[rows, :].astype(F32)
            r = lax.rsqrt(jnp.mean(xf * xf, axis=-1, keepdims=True) + EPS)
            o_ref[rows, :] = (xf * r * w_ref[...]).astype(o_ref.dtype)

    return pl.pallas_call(
        body, name=name, grid=(s // ts,), out_shape=jax.ShapeDtypeStruct((s, d), out_dtype),
        in_specs=[pl.BlockSpec((ts, d), lambda i: (i, 0)), pl.BlockSpec((1, d), lambda i: (0, 0))],
        out_specs=pl.BlockSpec((ts, d), lambda i: (i, 0)), compiler_params=_cp("parallel"),
    )(xv, w)


def _add_rms(xv, mv, w, name, w_next=None):
    s, d = xv.shape
    ts = min(NORM_TILE, s)
    both = w_next is not None

    def body(*refs):
        x_ref, m_ref, w_ref = refs[:3]
        o_ref = refs[4] if both else refs[3]
        for r0 in range(0, ts, NORM_CHUNK):
            rows = pl.ds(r0, NORM_CHUNK)
            mf = m_ref[rows, :].astype(F32)
            r = lax.rsqrt(jnp.mean(mf * mf, axis=-1, keepdims=True) + EPS)
            y = x_ref[rows, :] + mf * r * w_ref[...]
            o_ref[rows, :] = y
            if both:
                r2 = lax.rsqrt(jnp.mean(y * y, axis=-1, keepdims=True) + EPS)
                refs[5][rows, :] = (y * r2 * refs[3][...]).astype(BF16)

    row = pl.BlockSpec((ts, d), lambda i: (i, 0))
    vec = pl.BlockSpec((1, d), lambda i: (0, 0))
    if both:
        return pl.pallas_call(
            body, name=name + "_rms", grid=(s // ts,),
            out_shape=(jax.ShapeDtypeStruct((s, d), F32), jax.ShapeDtypeStruct((s, d), BF16)),
            in_specs=[row, row, vec, vec], out_specs=(row, row), compiler_params=_cp("parallel"),
        )(xv, mv, w, w_next)
    return pl.pallas_call(
        body, name=name, grid=(s // ts,), out_shape=jax.ShapeDtypeStruct((s, d), F32),
        in_specs=[row, row, vec], out_specs=row, compiler_params=_cp("parallel"),
    )(xv, mv, w)


def _rms_bwd_math(xf, w, dy):
    r = lax.rsqrt(jnp.mean(xf * xf, axis=-1, keepdims=True) + EPS)
    xh = xf * r
    dxh = dy * w
    dx = r * (dxh - xh * jnp.mean(dxh * xh, axis=-1, keepdims=True))
    return dx, dy * xh


def _rms_bwd(xv, w, dy, dres, out_dtype, name, pending=None):
    s, d = xv.shape
    ts = min(NORM_TILE, s)
    with_res = dres is not None
    nin = 4 if with_res else 3
    nx = len(pending) if pending else 0

    def body(*refs):
        x_ref, w_ref, dy_ref = refs[:3]
        dres_ref = refs[3] if with_res else None
        dx_ref, dw_ref = refs[nin + nx:nin + nx + 2]
        acc = refs[nin + 2 * nx + 2]
        if nx:
            start, finish = _exchange_plan(refs[nin:nin + nx], refs[nin + nx + 2:nin + 2 * nx + 2], *refs[nin + 2 * nx + 3:])
            pl.when(pl.program_id(0) == 0)(start)
        acc[...] = jnp.zeros_like(acc)
        for r0 in range(0, ts, NORM_CHUNK):
            rows = pl.ds(r0, NORM_CHUNK)
            dx, dwt = _rms_bwd_math(x_ref[rows, :].astype(F32), w_ref[...], dy_ref[rows, :].astype(F32))
            if with_res:
                dx = dx + dres_ref[rows, :]
            dx_ref[rows, :] = dx.astype(dx_ref.dtype)
            acc[...] += dwt

        @pl.when(pl.program_id(0) == 0)
        def _():
            dw_ref[...] = jnp.zeros_like(dw_ref)

        dw_ref[...] += _rowsum(acc[...])
        if nx:
            pl.when(pl.program_id(0) == s // ts - 1)(finish)

    row = pl.BlockSpec((ts, d), lambda i: (i, 0))
    vec = pl.BlockSpec((1, d), lambda i: (0, 0))
    ins = [xv, w, dy] + ([dres] if with_res else []) + (pending or [])
    res = pl.pallas_call(
        body, name=name + "_exchange" if nx else name, grid=(s // ts,),
        out_shape=[jax.ShapeDtypeStruct((s, d), out_dtype), jax.ShapeDtypeStruct((1, d), F32)]
        + [jax.ShapeDtypeStruct(a.shape, a.dtype) for a in (pending or [])],
        in_specs=[row, vec, row] + ([row] if with_res else []) + [ANY] * nx, out_specs=[row, vec] + [ANY] * nx,
        scratch_shapes=[pltpu.VMEM((NORM_CHUNK, d), F32)] + (_comm_scratch(nx) if nx else []), compiler_params=_cp("arbitrary"),
    )(*ins)
    return (res[0], res[1], list(res[2:])) if nx else (res[0], res[1])


def _rms_bwd2(xa, wa, dya, dres, xb, wb):
    s, d = xa.shape
    ts = min(NORM_TILE, s)

    def body(xa_ref, wa_ref, dya_ref, dres_ref, xb_ref, wb_ref, da_ref, db_ref, dwa_ref, dwb_ref, acc):
        acc[...] = jnp.zeros_like(acc)
        for r0 in range(0, ts, NORM_CHUNK):
            rows = pl.ds(r0, NORM_CHUNK)
            da, dwt = _rms_bwd_math(xa_ref[rows, :].astype(F32), wa_ref[...], dya_ref[rows, :].astype(F32))
            da = da + dres_ref[rows, :]
            da_ref[rows, :] = da
            acc[0] += dwt
            db, dwt = _rms_bwd_math(xb_ref[rows, :].astype(F32), wb_ref[...], da)
            db_ref[rows, :] = db.astype(BF16)
            acc[1] += dwt

        @pl.when(pl.program_id(0) == 0)
        def _():
            dwa_ref[...] = jnp.zeros_like(dwa_ref)
            dwb_ref[...] = jnp.zeros_like(dwb_ref)

        dwa_ref[...] += _rowsum(acc[0])
        dwb_ref[...] += _rowsum(acc[1])

    row = pl.BlockSpec((ts, d), lambda i: (i, 0))
    vec = pl.BlockSpec((1, d), lambda i: (0, 0))
    return pl.pallas_call(
        body, name="rms_bwd2", grid=(s // ts,),
        out_shape=(jax.ShapeDtypeStruct((s, d), F32), jax.ShapeDtypeStruct((s, d), BF16), jax.ShapeDtypeStruct((1, d), F32),
                   jax.ShapeDtypeStruct((1, d), F32)),
        in_specs=[row, vec, row, row, row, vec], out_specs=(row, row, vec, vec),
        scratch_shapes=[pltpu.VMEM((2, NORM_CHUNK, d), F32)], compiler_params=_cp("arbitrary"),
    )(xa, wa, dya, dres, xb, wb)


def _loss_head(yv, tv):
    s, d = yv.shape
    ts = min(ROW_TILE, s)

    def body(y_ref, t_ref, l_ref, dy_ref):
        e = y_ref[...] - t_ref[...]
        dy_ref[...] = e * (1.0 / d)

        @pl.when(pl.program_id(0) == 0)
        def _():
            l_ref[...] = jnp.zeros_like(l_ref)

        tot = jnp.sum(jnp.sum(e * e, axis=1, keepdims=True), axis=0, keepdims=True)
        l_ref[...] += jnp.broadcast_to(tot * (0.5 / d), (8, LANE))

    row = pl.BlockSpec((ts, d), lambda i: (i, 0))
    return pl.pallas_call(
        body, name="loss_head", grid=(s // ts,),
        out_shape=(jax.ShapeDtypeStruct((8, LANE), F32), jax.ShapeDtypeStruct((s, d), F32)),
        in_specs=[row, row], out_specs=(pl.BlockSpec((8, LANE), lambda i: (0, 0)), row), compiler_params=_cp("arbitrary"),
    )(yv, tv)


def _rope(v, c, a, b):
    return v * c + pltpu.roll(v, LANE - 16, 1) * a + pltpu.roll(v, 16, 1) * b


def _rope_t(dv, c, a, b):
    return dv * c + pltpu.roll(dv * a, 16, 1) + pltpu.roll(dv * b, LANE - 16, 1)


def _mla_prep(proj, tabs, qnw, kvnw, wq, wkv):
    s = proj.shape[0]
    ts = min(ROW_TILE, s)
    tc, ta, tb = tabs

    def body(cq_ref, ckv_ref, kr_ref, c_ref, a_ref, b_ref, qnw_ref, kvnw_ref, wq_ref, wkv_ref, q_ref, k_ref, kv_ref):
        c, a, b = c_ref[...], a_ref[...], b_ref[...]
        cq = cq_ref[...].astype(F32)
        qn = (cq * lax.rsqrt(jnp.mean(cq * cq, axis=-1, keepdims=True) + EPS) * qnw_ref[...]).astype(BF16)
        ckv = ckv_ref[...].astype(F32)
        kvn = (ckv * lax.rsqrt(jnp.mean(ckv * ckv, axis=-1, keepdims=True) + EPS) * kvnw_ref[...]).astype(BF16)
        kr = _rope(kr_ref[...].astype(F32), c, a, b)
        lane = lax.broadcasted_iota(jnp.int32, (ts, LANE), 1)
        for h in range(HEADS):
            q_ref[h] = _rope(_dot(qn, wq_ref[h]), c, a, b).astype(BF16)
            kv = _dot(kvn, wkv_ref[h])
            kv_ref[h] = kv.astype(BF16)
            k_ref[h] = jnp.where(lane < NOPE, kv, kr).astype(BF16)

    tab = pl.BlockSpec((ts, LANE), lambda i: (i, 0))
    hd = pl.BlockSpec((HEADS, ts, LANE), lambda i: (0, i, 0))
    out = jax.ShapeDtypeStruct((HEADS, s, LANE), BF16)
    return pl.pallas_call(
        body, name="mla_prep", grid=(s // ts,), out_shape=(out, out, out),
        in_specs=[pl.BlockSpec((ts, QL), lambda i: (i, 0)), pl.BlockSpec((ts, LANE), lambda i: (i, 2)),
                  pl.BlockSpec((ts, LANE), lambda i: (i, 3)), tab, tab, tab,
                  pl.BlockSpec((1, QL), lambda i: (0, 0)), pl.BlockSpec((1, KVL), lambda i: (0, 0)),
                  pl.BlockSpec((HEADS, QL, LANE), lambda i: (0, 0, 0)), pl.BlockSpec((HEADS, KVL, LANE), lambda i: (0, 0, 0))],
        out_specs=(hd, hd, hd), compiler_params=_cp("parallel"),
    )(proj, proj, proj, tc, ta, tb, qnw, kvnw, wq, wkv)


def _mla_prep_bwd(proj, tabs, qnw, kvnw, wq, wkv, dq, dk, dv):
    s = proj.shape[0]
    ts = min(ROW_TILE, s)
    tc, ta, tb = tabs

    def body(cq_ref, ckv_ref, c_ref, a_ref, b_ref, qnw_ref, kvnw_ref, wq_ref, wkv_ref, dq_ref, dk_ref, dv_ref,
             dcq_ref, dckv_ref, dkr_ref, dwq_ref, dwkv_ref, dqnw_ref, dkvnw_ref):
        @pl.when(pl.program_id(0) == 0)
        def _():
            dwq_ref[...] = jnp.zeros_like(dwq_ref)
            dwkv_ref[...] = jnp.zeros_like(dwkv_ref)
            dqnw_ref[...] = jnp.zeros_like(dqnw_ref)
            dkvnw_ref[...] = jnp.zeros_like(dkvnw_ref)

        c, a, b = c_ref[...], a_ref[...], b_ref[...]
        cq = cq_ref[...].astype(F32)
        qn = (cq * lax.rsqrt(jnp.mean(cq * cq, axis=-1, keepdims=True) + EPS) * qnw_ref[...]).astype(BF16)
        ckv = ckv_ref[...].astype(F32)
        kvn = (ckv * lax.rsqrt(jnp.mean(ckv * ckv, axis=-1, keepdims=True) + EPS) * kvnw_ref[...]).astype(BF16)
        lane = lax.broadcasted_iota(jnp.int32, (ts, LANE), 1)
        dqn = jnp.zeros((ts, QL), F32)
        dkvn = jnp.zeros((ts, KVL), F32)
        dkr = jnp.zeros((ts, LANE), F32)
        for h in range(HEADS):
            dqh = _rope_t(dq_ref[h], c, a, b).astype(BF16)
            dwq_ref[h] += _dot(qn, dqh, TN)
            dqn += _dot(dqh, wq_ref[h], NT)
            dkh = dk_ref[h].astype(F32)
            dkvh = jnp.where(lane < NOPE, dkh, dv_ref[h].astype(F32)).astype(BF16)
            dkr += jnp.where(lane < NOPE, 0.0, dkh)
            dwkv_ref[h] += _dot(kvn, dkvh, TN)
            dkvn += _dot(dkvh, wkv_ref[h], NT)
        dkr_ref[...] = _rope_t(dkr, c, a, b).astype(BF16)
        dcq, dwt = _rms_bwd_math(cq, qnw_ref[...], dqn)
        dcq_ref[...] = dcq.astype(BF16)
        dqnw_ref[...] += _rowsum(dwt)
        dckv, dwt = _rms_bwd_math(ckv, kvnw_ref[...], dkvn)
        dckv_ref[...] = dckv.astype(BF16)
        dkvnw_ref[...] += _rowsum(dwt)

    tab = pl.BlockSpec((ts, LANE), lambda i: (i, 0))
    hd = pl.BlockSpec((HEADS, ts, LANE), lambda i: (0, i, 0))
    wq_spec = pl.BlockSpec((HEADS, QL, LANE), lambda i: (0, 0, 0))
    wkv_spec = pl.BlockSpec((HEADS, KVL, LANE), lambda i: (0, 0, 0))
    return pl.pallas_call(
        body, name="mla_prep_bwd", grid=(s // ts,),
        out_shape=(jax.ShapeDtypeStruct((s, QL), BF16), jax.ShapeDtypeStruct((s, KVL), BF16), jax.ShapeDtypeStruct((s, LANE), BF16),
                   jax.ShapeDtypeStruct((HEADS, QL, LANE), F32), jax.ShapeDtypeStruct((HEADS, KVL, LANE), F32),
                   jax.ShapeDtypeStruct((1, QL), F32), jax.ShapeDtypeStruct((1, KVL), F32)),
        in_specs=[pl.BlockSpec((ts, QL), lambda i: (i, 0)), pl.BlockSpec((ts, LANE), lambda i: (i, 2)), tab, tab, tab,
                  pl.BlockSpec((1, QL), lambda i: (0, 0)), pl.BlockSpec((1, KVL), lambda i: (0, 0)), wq_spec, wkv_spec, hd, hd, hd],
        out_specs=(pl.BlockSpec((ts, QL), lambda i: (i, 0)), pl.BlockSpec((ts, KVL), lambda i: (i, 0)), tab, wq_spec, wkv_spec,
                   pl.BlockSpec((1, QL), lambda i: (0, 0)), pl.BlockSpec((1, KVL), lambda i: (0, 0))),
        compiler_params=_cp("arbitrary"),
    )(proj, proj, tc, ta, tb, qnw, kvnw, wq, wkv, dq, dk, dv)


def _transpose_bf16(v):
    return v.astype(F32).T.astype(BF16)


def _flash_fwd(q, k, kv, prefetch=None):
    s = q.shape[1]
    t = min(ATT_TILE, s)
    tq = ATT_QW * t
    n = s // tq
    g = FWD_HEADS
    nx = len(prefetch) if prefetch else 0

    def body(*refs):
        q_ref, k_ref, kv_ref = refs[:3]
        o_ref, lse_ref = refs[3 + nx:5 + nx]
        kvt_sc = refs[5 + 2 * nx]
        step = pl.program_id(0) * n + pl.program_id(1)
        if nx:
            start, finish = _gather_plan(refs[3:3 + nx], refs[5 + nx:5 + 2 * nx], *refs[6 + 2 * nx:])
            pl.when(step == 0)(start)
        attend(q_ref, k_ref, kv_ref, o_ref, lse_ref, kvt_sc)
        if nx:
            pl.when(step == (HEADS // g) * n - 1)(finish)

    def attend(q_ref, k_ref, kv_ref, o_ref, lse_ref, kvt_sc):
        i = pl.program_id(1)

        @pl.when(i == 0)
        def _():
            ones_rows = lax.broadcasted_iota(jnp.int32, (LANE, s), 0) < NOPE
            for hh in range(g):
                kvt_sc[hh] = jnp.where(ones_rows, 1.0, kv_ref[hh].astype(F32).T).astype(BF16)

        qt = [(q_ref[hh].astype(F32) * (ATT_SCALE * LOG2E)).T.astype(BF16) for hh in range(g)]
        kpos = lax.broadcasted_iota(jnp.int32, (t, tq), 0)
        qpos = lax.broadcasted_iota(jnp.int32, (t, tq), 1) + i * tq

        def chunk(j, carry, diagonal):
            start = pl.multiple_of(j * t, t)
            scs = [_dot(k_ref[hh, pl.ds(start, t), :], qt[hh]) for hh in range(g)]
            soft = []
            for hh in range(g):
                sc = scs[hh]
                if diagonal:
                    sc = jnp.where(qpos >= kpos + start, sc, NEG)
                m_new = jnp.maximum(carry[hh][0], jnp.max(sc, axis=0, keepdims=True))
                soft.append((m_new, jnp.exp2(carry[hh][0] - m_new), jnp.exp2(sc - m_new).astype(BF16)))
            pvs = [_dot(kvt_sc[hh, :, pl.ds(start, t)], soft[hh][2]) for hh in range(g)]
            return tuple((soft[hh][0], soft[hh][1] * carry[hh][1] + pvs[hh]) for hh in range(g))

        init = tuple((jnp.full((1, tq), NEG, F32), jnp.zeros((LANE, tq), F32)) for _ in range(g))
        carry = lax.fori_loop(0, ATT_QW * i, lambda j, c: chunk(j, c, False), init)
        for d in range(ATT_QW):
            carry = chunk(ATT_QW * i + d, carry, True)
        for hh in range(g):
            m, acc = carry[hh]
            l = acc[0:1, :]
            o_ref[:, hh * LANE:(hh + 1) * LANE] = (acc / l).T.astype(BF16)
            lse_ref[hh] = m + jnp.log2(l)

    whole = pl.BlockSpec((g, s, LANE), lambda h, i: (h, 0, 0))
    res = pl.pallas_call(
        body, name="flash_fwd_gather" if nx else "flash_fwd", grid=(HEADS // g, n),
        out_shape=[jax.ShapeDtypeStruct((s, HEADS * LANE), BF16), jax.ShapeDtypeStruct((HEADS, 1, s), F32)]
        + [jax.ShapeDtypeStruct((NDEV,) + a.shape, a.dtype) for a in (prefetch or [])],
        in_specs=[pl.BlockSpec((g, tq, LANE), lambda h, i: (h, i, 0)), whole, whole] + [ANY] * nx,
        out_specs=[pl.BlockSpec((tq, g * LANE), lambda h, i: (i, h)), pl.BlockSpec((g, 1, tq), lambda h, i: (h, 0, i))] + [ANY] * nx,
        scratch_shapes=[pltpu.VMEM((g, LANE, s), BF16)] + (_comm_scratch(nx) if nx else []),
        compiler_params=_cp("arbitrary", "arbitrary"),
    )(q, k, kv, *(prefetch or []))
    return res[0], res[1], list(res[2:])


def _flash_bwd(q, k, kv, cat, dcat, lse, pending=None):
    s = q.shape[1]
    t = min(ATT_TILE, s)
    tq = ATT_QW * t
    n = s // t
    g = BWD_HEADS
    nx = len(pending) if pending else 0

    def body(*refs):
        ins, outs, scr = refs[:6], refs[6 + nx:9 + nx], refs[9 + 2 * nx:14 + 2 * nx]
        step = pl.program_id(0) * n + pl.program_id(1)
        if nx:
            start, finish = _exchange_plan(refs[6:6 + nx], refs[9 + nx:9 + 2 * nx], *refs[14 + 2 * nx:])
            pl.when(step == 0)(start)
        attend(*ins, *outs, *scr)
        if nx:
            pl.when(step == (HEADS // g) * n - 1)(finish)

    def attend(q_ref, k_ref, kv_ref, o_ref, do_ref, lse_ref, dq_ref, dk_ref, dv_ref, qt_sc, dot_sc, delta_sc, dqt_sc, qs_sc):
        j = pl.program_id(1)

        @pl.when(j == 0)
        def _():
            for hh in range(g):
                lanes = slice(hh * LANE, (hh + 1) * LANE)
                qf = q_ref[hh].astype(F32)
                qt_sc[hh] = (qf * (ATT_SCALE * LOG2E)).T.astype(BF16)
                qs_sc[hh] = (qf * ATT_SCALE).astype(BF16)
                dof = do_ref[:, lanes].astype(F32)
                dot_sc[hh] = dof.T.astype(BF16)
                delta_sc[hh] = _dot(jnp.ones((8, LANE), F32), dof * o_ref[:, lanes].astype(F32), NT, precision=HI)
            dqt_sc[...] = jnp.zeros_like(dqt_sc)

        kjt = [_transpose_bf16(k_ref[hh]) for hh in range(g)]
        kpos = lax.broadcasted_iota(jnp.int32, (t, tq), 0) + j * t
        qpos = lax.broadcasted_iota(jnp.int32, (t, tq), 1)

        def chunk(i, carry, diagonal):
            start = pl.multiple_of(i * tq, tq)
            cols = pl.ds(start, tq)
            scs = [_dot(k_ref[hh], qt_sc[hh, :, cols]) for hh in range(g)]
            dps = [_dot(kv_ref[hh], dot_sc[hh, :, cols]) for hh in range(g)]
            pds = []
            for hh in range(g):
                p = jnp.exp2(scs[hh] - lse_ref[hh, :, cols])
                if diagonal:
                    p = jnp.where(qpos + start >= kpos, p, 0.0)
                ds = (p * (dps[hh] - delta_sc[hh, 0:1, cols])).astype(BF16)
                pds.append((p.astype(BF16), ds))
            out = []
            for hh in range(g):
                dk, dv = carry[hh]
                dv = dv + _dot(pds[hh][0], do_ref[pl.ds(start, tq), hh * LANE:(hh + 1) * LANE])
                dk = dk + _dot(pds[hh][1], qs_sc[hh, pl.ds(start, tq), :])
                dqt_sc[hh, :, cols] += _dot(kjt[hh], pds[hh][1])
                out.append((dk, dv))
            return tuple(out)

        zero = jnp.zeros((t, LANE), F32)
        first = lax.div(j, ATT_QW)
        carry = chunk(first, tuple((zero, zero) for _ in range(g)), True)
        carry = lax.fori_loop(first + 1, s // tq, lambda i, c: chunk(i, c, False), carry)
        for hh in range(g):
            dk_ref[hh] = carry[hh][0].astype(BF16)
            dv_ref[hh] = carry[hh][1].astype(BF16)

        @pl.when(j == n - 1)
        def _():
            for hh in range(g):
                dq_ref[hh] = (dqt_sc[hh] * ATT_SCALE).T

    whole = pl.BlockSpec((g, s, LANE), lambda h, j: (h, 0, 0))
    kspec = pl.BlockSpec((g, t, LANE), lambda h, j: (h, j, 0))
    ospec = pl.BlockSpec((s, g * LANE), lambda h, j: (0, h))
    res = pl.pallas_call(
        body, name="flash_bwd_exchange" if nx else "flash_bwd", grid=(HEADS // g, n),
        out_shape=[jax.ShapeDtypeStruct((HEADS, s, LANE), F32), jax.ShapeDtypeStruct((HEADS, s, LANE), BF16),
                   jax.ShapeDtypeStruct((HEADS, s, LANE), BF16)] + [jax.ShapeDtypeStruct(a.shape, a.dtype) for a in (pending or [])],
        in_specs=[whole, kspec, kspec, ospec, ospec, pl.BlockSpec((g, 1, s), lambda h, j: (h, 0, 0))] + [ANY] * nx,
        out_specs=[whole, kspec, kspec] + [ANY] * nx,
        scratch_shapes=[pltpu.VMEM((g, LANE, s), BF16), pltpu.VMEM((g, LANE, s), BF16), pltpu.VMEM((g, 8, s), F32),
                        pltpu.VMEM((g, LANE, s), F32), pltpu.VMEM((g, s, LANE), BF16)] + (_comm_scratch(nx) if nx else []),
        compiler_params=_cp("arbitrary", "arbitrary"),
    )(q, k, kv, cat, dcat, lse, *(pending or []))
    return res[0], res[1], res[2], list(res[3:])


def _conv3(ext, w_ref, ts):
    return (w_ref[0:1, :] * ext[pl.ds(HALO - 2, ts), :] + w_ref[1:2, :] * ext[pl.ds(HALO - 1, ts), :]
            + w_ref[2:3, :] * ext[pl.ds(HALO, ts), :])


def _conv3_rows(ext, w_ref, r):
    return (w_ref[0:1, :] * ext[pl.ds(HALO - 2 + r, ROW_CHUNK), :] + w_ref[1:2, :] * ext[pl.ds(HALO - 1 + r, ROW_CHUNK), :]
            + w_ref[2:3, :] * ext[pl.ds(HALO + r, ROW_CHUNK), :])


def _conv3_t(ext2, w_ref, ts):
    return (w_ref[0:1, :] * ext2[pl.ds(2, ts), :] + w_ref[1:2, :] * ext2[pl.ds(1, ts), :] + w_ref[2:3, :] * ext2[pl.ds(0, ts), :])


def _sconv_fwd(proj, w):
    s = proj.shape[0]
    ts = min(ROW_TILE, s)

    def body(b_ref, c_ref, h_ref, hc_ref, hh_ref, w_ref, o_ref, ext):
        i = pl.program_id(0)
        ext[0:HALO, :] = hc_ref[...].astype(F32) * hh_ref[...].astype(F32) * (i > 0).astype(F32)
        ext[HALO:HALO + ts, :] = c_ref[...].astype(F32) * h_ref[...].astype(F32)
        o_ref[...] = (b_ref[...].astype(F32) * _conv3(ext, w_ref, ts)).astype(BF16)

    def col(cb):
        return pl.BlockSpec((ts, SC), lambda i: (i, cb))

    def halo(cb):
        return pl.BlockSpec((HALO, SC), lambda i: (_prev_halo(i, ts), cb))

    return pl.pallas_call(
        body, name="sconv_fwd", grid=(s // ts,), out_shape=jax.ShapeDtypeStruct((s, SC), BF16),
        in_specs=[col(2), col(3), col(4), halo(3), halo(4), pl.BlockSpec((3, SC), lambda i: (0, 0))],
        out_specs=pl.BlockSpec((ts, SC), lambda i: (i, 0)), scratch_shapes=[pltpu.VMEM((ts + HALO, SC), F32)],
        compiler_params=_cp("parallel"),
    )(proj, proj, proj, proj, proj, w)


def _sconv_bwd(proj, dcat, w):
    s = proj.shape[0]
    ts = min(ROW_TILE, s)
    n = s // ts

    def body(b_ref, c_ref, h_ref, hc_ref, hh_ref, dy_ref, ndy_ref, nb_ref, w_ref, db_ref, dc_ref, dh_ref, dw_ref, ext, ext2):
        i = pl.program_id(0)

        @pl.when(i == 0)
        def _():
            dw_ref[...] = jnp.zeros_like(dw_ref)

        cv, hv, bv = c_ref[...].astype(F32), h_ref[...].astype(F32), b_ref[...].astype(F32)
        ext[0:HALO, :] = hc_ref[...].astype(F32) * hh_ref[...].astype(F32) * (i > 0).astype(F32)
        ext[HALO:HALO + ts, :] = cv * hv
        dy = dy_ref[...].astype(F32)
        db_ref[...] = (dy * _conv3(ext, w_ref, ts)).astype(BF16)
        dyb = dy * bv
        ext2[0:ts, :] = dyb
        ext2[ts:ts + HALO, :] = ndy_ref[...].astype(F32) * nb_ref[...].astype(F32) * (i < n - 1).astype(F32)
        dg = _conv3_t(ext2, w_ref, ts)
        dc_ref[...] = (dg * hv).astype(BF16)
        dh_ref[...] = (dg * cv).astype(BF16)
        for kk in range(3):
            dw_ref[kk:kk + 1, :] += _rowsum(dyb * ext[pl.ds(HALO - 2 + kk, ts), :])

    def col(cb):
        return pl.BlockSpec((ts, SC), lambda i: (i, cb))

    def halo(cb):
        return pl.BlockSpec((HALO, SC), lambda i: (_prev_halo(i, ts), cb))

    def nxt(cb):
        return pl.BlockSpec((HALO, SC), lambda i: (_next_halo(i, ts, n), cb))

    out = jax.ShapeDtypeStruct((s, SC), BF16)
    o0 = pl.BlockSpec((ts, SC), lambda i: (i, 0))
    return pl.pallas_call(
        body, name="sconv_bwd", grid=(n,), out_shape=(out, out, out, jax.ShapeDtypeStruct((3, SC), F32)),
        in_specs=[col(2), col(3), col(4), halo(3), halo(4), col(4), nxt(4), nxt(2), pl.BlockSpec((3, SC), lambda i: (0, 0))],
        out_specs=(o0, o0, o0, pl.BlockSpec((3, SC), lambda i: (0, 0))),
        scratch_shapes=[pltpu.VMEM((ts + HALO, SC), F32), pltpu.VMEM((ts + HALO, SC), F32)], compiler_params=_cp("arbitrary"),
    )(proj, proj, proj, proj, proj, dcat, dcat, proj, w)


def _ffn_stage(ext, u_ref, halo_ref, i, ts):
    ext[0:HALO, :] = halo_ref[...].astype(F32) * (i > 0).astype(F32)
    ext[HALO:HALO + ts, :] = u_ref[...].astype(F32)


def _ffn_specs(ts):
    cur = pl.BlockSpec((2, None, ts, FB), lambda j, i: (0, j, i, 0))
    halo = pl.BlockSpec((2, None, HALO, FB), lambda j, i: (0, j, _prev_halo(i, ts), 0))
    w = pl.BlockSpec((2, None, 3, FB), lambda j, i: (0, j, 0, 0))
    b = pl.BlockSpec((2, None, 1, FB), lambda j, i: (0, j, 0, 0))
    return cur, halo, w, b


def _ffn_act(upre, fcw, fcb):
    s = upre.shape[1]
    ts = min(ROW_TILE, s)

    def body(u_ref, halo_ref, w_ref, b_ref, o_ref, ext_g, ext_u):
        i = pl.program_id(1)
        _ffn_stage(ext_g, u_ref.at[0], halo_ref.at[0], i, ts)
        _ffn_stage(ext_u, u_ref.at[1], halo_ref.at[1], i, ts)
        for r in range(0, ts, ROW_CHUNK):
            gate = b_ref[0] + _conv3_rows(ext_g, w_ref.at[0], r)
            up = b_ref[1] + _conv3_rows(ext_u, w_ref.at[1], r)
            o_ref[pl.ds(r, ROW_CHUNK), :] = (gate * _sig(gate) * up).astype(BF16)

    cur, halo, w, b = _ffn_specs(ts)
    u4 = upre.reshape(2, 4, s, FB)
    return pl.pallas_call(
        body, name="ffn_act", grid=(4, s // ts), out_shape=jax.ShapeDtypeStruct((4, s, FB), BF16),
        in_specs=[cur, halo, w, b], out_specs=pl.BlockSpec((None, ts, FB), lambda j, i: (j, i, 0)),
        scratch_shapes=[pltpu.VMEM((ts + HALO, FB), F32), pltpu.VMEM((ts + HALO, FB), F32)], compiler_params=_cp("parallel", "parallel"),
    )(u4, u4, fcw.reshape(2, 4, 3, FB), fcb.reshape(2, 4, 1, FB))


def _ffn_bwd(upre, dact, fcw, fcb):
    s = upre.shape[1]
    ts = min(ROW_TILE, s)
    n = s // ts
    te = ts + HALO

    def body(u_ref, halo_ref, nxt_ref, w_ref, b_ref, da_ref, nda_ref, dup_ref, db_ref, dw_ref, ext_g, ext_u, ext_da, du_g, du_u, acc):
        i = pl.program_id(1)

        @pl.when(i == 0)
        def _():
            db_ref[...] = jnp.zeros_like(db_ref)
            dw_ref[...] = jnp.zeros_like(dw_ref)

        more = (i < n - 1).astype(F32)
        for idx, ext in ((0, ext_g), (1, ext_u)):
            _ffn_stage(ext, u_ref.at[idx], halo_ref.at[idx], i, ts)
            ext[HALO + ts:HALO + te, :] = nxt_ref[idx].astype(F32) * more
        ext_da[0:ts, :] = da_ref[...].astype(F32)
        ext_da[ts:te, :] = nda_ref[...].astype(F32) * more
        acc[...] = jnp.zeros_like(acc)
        for r in range(0, te, ROW_CHUNK):
            rows = pl.ds(r, ROW_CHUNK)
            gate = b_ref[0] + _conv3_rows(ext_g, w_ref.at[0], r)
            up = b_ref[1] + _conv3_rows(ext_u, w_ref.at[1], r)
            sg = _sig(gate)
            da = ext_da[rows, :]
            dgate = da * up * sg * (1.0 + gate * (1.0 - sg))
            dup = da * gate * sg
            du_g[rows, :] = dgate
            du_u[rows, :] = dup
            if r < ts:
                acc[0] += dgate
                acc[1] += dup
        for r in range(0, ts, ROW_CHUNK):
            for idx, du, ext in ((0, du_g, ext_g), (1, du_u, ext_u)):
                d0 = du[pl.ds(r, ROW_CHUNK), :]
                dupre = (w_ref[idx, 2:3, :] * d0 + w_ref[idx, 1:2, :] * du[pl.ds(r + 1, ROW_CHUNK), :]
                         + w_ref[idx, 0:1, :] * du[pl.ds(r + 2, ROW_CHUNK), :])
                dup_ref[idx, pl.ds(r, ROW_CHUNK), :] = dupre.astype(BF16)
                for kk in range(3):
                    acc[2 + 3 * idx + kk] += d0 * ext[pl.ds(HALO - 2 + kk + r, ROW_CHUNK), :]
        for idx in range(2):
            db_ref[idx] += _rowsum(acc[idx])
            for kk in range(3):
                dw_ref[idx, kk:kk + 1, :] += _rowsum(acc[2 + 3 * idx + kk])

    cur, halo, w, b = _ffn_specs(ts)
    nxt = pl.BlockSpec((2, None, HALO, FB), lambda j, i: (0, j, _next_halo(i, ts, n), 0))
    u4 = upre.reshape(2, 4, s, FB)
    dupre, db, dw = pl.pallas_call(
        body, name="ffn_bwd", grid=(4, n),
        out_shape=(jax.ShapeDtypeStruct((2, 4, s, FB), BF16), jax.ShapeDtypeStruct((2, 4, 1, FB), F32),
                   jax.ShapeDtypeStruct((2, 4, 3, FB), F32)),
        in_specs=[cur, halo, nxt, w, b, pl.BlockSpec((None, ts, FB), lambda j, i: (j, i, 0)),
                  pl.BlockSpec((None, HALO, FB), lambda j, i: (j, _next_halo(i, ts, n), 0))],
        out_specs=(cur, b, w),
        scratch_shapes=[pltpu.VMEM((te + HALO, FB), F32), pltpu.VMEM((te + HALO, FB), F32), pltpu.VMEM((te, FB), F32),
                        pltpu.VMEM((te, FB), F32), pltpu.VMEM((te, FB), F32), pltpu.VMEM((8, ROW_CHUNK, FB), F32)],
        compiler_params=_cp("parallel", "arbitrary"),
    )(u4, u4, u4, fcw.reshape(2, 4, 3, FB), fcb.reshape(2, 4, 1, FB), dact, dact)
    return dupre.reshape(NDEV, s, FB), db.reshape(NDEV, 1, FB), dw.reshape(NDEV, 3, FB)


def _softplus(v):
    e = jnp.exp(-jnp.abs(v))
    return jnp.maximum(v, 0.0) + jnp.where(e < 1e-4, e * (1.0 - 0.5 * e), jnp.log(1.0 + e))


def _ssd_consts():
    L = SSD_L
    r = lax.broadcasted_iota(jnp.int32, (L, L), 0)
    c = lax.broadcasted_iota(jnp.int32, (L, L), 1)
    tri = r >= c
    er = lax.broadcasted_iota(jnp.int32, (LANE, SSD_DIM), 0)
    ec = lax.broadcasted_iota(jnp.int32, (LANE, SSD_DIM), 1)
    expand = ((ec >= er * 64) & (ec < er * 64 + 64)).astype(F32)
    return tri, expand


def _ssd_conv4(ext, cw_ref, cb_ref):
    L = SSD_L
    pre = cb_ref[...] + cw_ref[0:1, :] * ext[pl.ds(HALO - 3, L), :]
    for kk in range(1, 4):
        pre = pre + cw_ref[kk:kk + 1, :] * ext[pl.ds(HALO - 3 + kk, L), :]
    return pre


def _ssd_common(xbc_ref, halo_ref, dt_ref, cw_ref, cb_ref, dtb_ref, alog_ref, ext, first):
    L = SSD_L
    tri, expand = _ssd_consts()
    ext[0:HALO, :] = halo_ref[...].astype(F32) * (1.0 - first.astype(F32))
    ext[HALO:HALO + L, :] = xbc_ref[...].astype(F32)
    pre = _ssd_conv4(ext, cw_ref, cb_ref)
    sg = _sig(pre)
    act = pre * sg
    lane = lax.broadcasted_iota(jnp.int32, (1, LANE), 1)
    m4 = lane < SSD_H
    raw = dt_ref[...].astype(F32) + dtb_ref[...]
    dtv = jnp.where(m4, _softplus(raw), 0.0)
    av = jnp.where(m4, -jnp.exp(alog_ref[...]), 0.0)
    adt = dtv * av
    acs = _dot(tri.astype(F32), adt, precision=HI)
    acs_b = _dot(acs, expand, precision=HI)
    dt_b = _dot(dtv, expand, precision=HI)
    return dict(tri=tri, expand=expand, pre=pre, sg=sg, act=act, raw=raw, dtv=dtv, av=av, m4=m4, acs=acs, acs_b=acs_b,
                dt_b=dt_b, lane=lane)


def _head_terms(cm, h):
    L = SSD_L
    acs, tri = cm["acs"], cm["tri"]
    lane_l = lax.broadcasted_iota(jnp.int32, (L, LANE), 1)
    sub_l = lax.broadcasted_iota(jnp.int32, (LANE, L), 0)
    col = jnp.sum(jnp.where(lane_l == h, acs, 0.0), axis=1, keepdims=True)
    row = jnp.sum(jnp.where(sub_l == h, acs.T, 0.0), axis=0, keepdims=True)
    dec = jnp.where(tri, jnp.exp(jnp.where(tri, col - row, NEG)), 0.0)
    rowi = lax.broadcasted_iota(jnp.int32, (L, 1), 0)
    last = jnp.sum(jnp.where(rowi == L - 1, col, 0.0), axis=0, keepdims=True)
    dte = jnp.exp(last - col)
    return col, dec, last, dte


def _ssd_fwd(proj, cw, cb, dtb, alog, dvec, nw):
    s = proj.shape[0]
    L = SSD_L
    nc = s // L

    def body(z_ref, xbc_ref, halo_ref, dt_ref, cw_ref, cb_ref, dtb_ref, alog_ref, d_ref, nw_ref, y_ref, ypre_ref, st_ref, ext, state):
        i = pl.program_id(0)

        @pl.when(i == 0)
        def _():
            state[...] = jnp.zeros_like(state)

        cm = _ssd_common(xbc_ref, halo_ref, dt_ref, cw_ref, cb_ref, dtb_ref, alog_ref, ext, i == 0)
        act = cm["act"]
        xs = act[:, 0:256]
        bm = (act[:, 256:384], act[:, 384:512])
        cmat = (act[:, 512:640].astype(BF16), act[:, 640:768].astype(BF16))
        xdt = xs * cm["dt_b"]
        prev = state[...]
        st_ref[...] = prev
        prev_bf = prev.astype(BF16)
        gm = [_dot(cmat[g], bm[g].astype(BF16), NT) for g in range(2)]
        lane2 = lax.broadcasted_iota(jnp.int32, (1, SSD_DIM), 1)
        rows2 = lax.broadcasted_iota(jnp.int32, (SSD_DIM, 1), 0)
        ydiag = jnp.zeros((L, SSD_DIM), F32)
        contrib = jnp.zeros((SSD_DIM, LANE), F32)
        cd_rows = jnp.zeros((SSD_DIM, 1), F32)
        for h in range(SSD_H):
            g = h // 2
            col, dec, last, dte = _head_terms(cm, h)
            mh = (lane2 >= 64 * h) & (lane2 < 64 * h + 64)
            xm = jnp.where(mh, xdt, 0.0).astype(BF16)
            ydiag += _dot((gm[g] * dec).astype(BF16), xm)
            contrib += _dot(xm, (bm[g] * dte).astype(BF16), TN)
            cd_rows += jnp.where((rows2 >= 64 * h) & (rows2 < 64 * h + 64), jnp.exp(last), 0.0)
        yo = jnp.where(lane2 < 128, _dot(cmat[0], prev_bf, NT), _dot(cmat[1], prev_bf, NT))
        y = ydiag + yo * jnp.exp(cm["acs_b"]) + xs * d_ref[...]
        state[...] = prev * cd_rows + contrib
        ypre_ref[...] = y
        zz = z_ref[...].astype(F32)
        gt = y * zz * _sig(zz)
        y_ref[...] = (gt * lax.rsqrt(jnp.mean(gt * gt, axis=-1, keepdims=True) + EPS) * nw_ref[...]).astype(BF16)

    def vec(w):
        return pl.BlockSpec((1, w), lambda i: (0, 0))

    return pl.pallas_call(
        body, name="ssd_fwd", grid=(nc,),
        out_shape=(jax.ShapeDtypeStruct((s, SSD_DIM), BF16), jax.ShapeDtypeStruct((s, SSD_DIM), F32),
                   jax.ShapeDtypeStruct((nc, SSD_DIM, LANE), F32)),
        in_specs=[pl.BlockSpec((L, SSD_DIM), lambda i: (i, 5)), pl.BlockSpec((L, SSD_CONV), lambda i: (i, 2)),
                  pl.BlockSpec((HALO, SSD_CONV), lambda i: (_prev_halo(i, L), 2)), pl.BlockSpec((L, LANE), lambda i: (i, 18)),
                  pl.BlockSpec((4, SSD_CONV), lambda i: (0, 0)), vec(SSD_CONV), vec(LANE), vec(LANE), vec(SSD_DIM), vec(SSD_DIM)],
        out_specs=(pl.BlockSpec((L, SSD_DIM), lambda i: (i, 0)), pl.BlockSpec((L, SSD_DIM), lambda i: (i, 0)),
                   pl.BlockSpec((None, SSD_DIM, LANE), lambda i: (i, 0, 0))),
        scratch_shapes=[pltpu.VMEM((L + HALO, SSD_CONV), F32), pltpu.VMEM((SSD_DIM, LANE), F32)], compiler_params=_cp("arbitrary"),
    )(proj, proj, proj, proj, cw, cb, dtb, alog, dvec, nw)


def _ssd_bwd(proj, dcat, ypre, states, cw, cb, dtb, alog, dvec, nw):
    s = proj.shape[0]
    L = SSD_L
    nc = s // L

    def body(z_ref, xbc_ref, halo_ref, dt_ref, dy_ref, ypre_ref, st_ref, cw_ref, cb_ref, dtb_ref, alog_ref, d_ref, nw_ref,
             dz_ref, dxbc_ref, ddt_ref, dcw_ref, dcb_ref, ddtb_ref, dalog_ref, dd_ref, dnw_ref, ext, ext2, carry, dstate, ddl):
        i = pl.program_id(0)
        r = nc - 1 - i

        @pl.when(i == 0)
        def _():
            for ref in (dcw_ref, dcb_ref, ddtb_ref, dalog_ref, dd_ref, dnw_ref, carry, dstate, ddl):
                ref[...] = jnp.zeros_like(ref)

        cm = _ssd_common(xbc_ref, halo_ref, dt_ref, cw_ref, cb_ref, dtb_ref, alog_ref, ext, r == 0)
        tri, expand, act = cm["tri"], cm["expand"], cm["act"]
        xs = act[:, 0:256]
        bm = (act[:, 256:384], act[:, 384:512])
        cmat = (act[:, 512:640], act[:, 640:768])
        bm_bf = [v.astype(BF16) for v in bm]
        cm_bf = [v.astype(BF16) for v in cmat]
        dt_b = cm["dt_b"]
        xdt = xs * dt_b
        xdt_bf = xdt.astype(BF16)
        ea_b = jnp.exp(cm["acs_b"])
        prev = st_ref[...]
        prev_bf = prev.astype(BF16)
        lane2 = lax.broadcasted_iota(jnp.int32, (1, SSD_DIM), 1)
        rows2 = lax.broadcasted_iota(jnp.int32, (SSD_DIM, 1), 0)
        lane_l = lax.broadcasted_iota(jnp.int32, (L, LANE), 1)
        rowi = lax.broadcasted_iota(jnp.int32, (L, 1), 0)

        y = ypre_ref[...]
        zz = z_ref[...].astype(F32)
        sz = _sig(zz)
        gt = y * zz * sz
        dgt, dwt = _rms_bwd_math(gt, nw_ref[...], dy_ref[...].astype(F32))
        dnw_ref[...] += _rowsum(dwt)
        dy = dgt * zz * sz
        dz_ref[...] = (dgt * y * sz * (1.0 + zz * (1.0 - sz))).astype(BF16)

        ddl[0:1, :] += _rowsum(dy * xs)
        dxs = dy * d_ref[...]

        yo = jnp.where(lane2 < 128, _dot(cm_bf[0], prev_bf, NT), _dot(cm_bf[1], prev_bf, NT))
        dacs_b = dy * yo * ea_b
        dyo = dy * ea_b
        dyo_g = (jnp.where(lane2 < 128, dyo, 0.0).astype(BF16), jnp.where(lane2 >= 128, dyo, 0.0).astype(BF16))
        dc = [_dot(dyo_g[g], prev_bf) for g in range(2)]
        dprev = _dot(dyo_g[0], cm_bf[0], TN) + _dot(dyo_g[1], cm_bf[1], TN)

        gm = [_dot(cm_bf[g], bm_bf[g], NT) for g in range(2)]
        dgm = [jnp.zeros((L, L), F32), jnp.zeros((L, L), F32)]
        db = [jnp.zeros((L, LANE), F32), jnp.zeros((L, LANE), F32)]
        dxdt = jnp.zeros((L, SSD_DIM), F32)
        dacs = jnp.zeros((L, LANE), F32)
        dlast = jnp.zeros((1, LANE), F32)
        cd_rows = jnp.zeros((SSD_DIM, 1), F32)
        dst = dstate[...]
        dst_bf = dst.astype(BF16)
        dsp = dst * prev
        ones = jnp.ones((L, LANE), F32)
        for h in range(SSD_H):
            g = h // 2
            col, dec, last, dte = _head_terms(cm, h)
            mh = (lane2 >= 64 * h) & (lane2 < 64 * h + 64)
            rh = (rows2 >= 64 * h) & (rows2 < 64 * h + 64)
            sc = gm[g] * dec
            xm = jnp.where(mh, xdt, 0.0).astype(BF16)
            dym = jnp.where(mh, dy, 0.0).astype(BF16)
            dsc = _dot(dym, xdt_bf, NT)
            dxdt += _dot(sc.astype(BF16), dym, TN)
            dgm[g] += dsc * dec
            dd = dsc * sc
            rs = jnp.sum(dd, axis=1, keepdims=True)
            cs = _dot(dd, ones, TN, precision=HI)
            dacs += jnp.where(lane_l == h, rs - cs, 0.0)
            bd = (bm[g] * dte).astype(BF16)
            dxdt += jnp.where(mh, _dot(bd, dst_bf, NT), 0.0)
            dbd = _dot(xm, dst_bf)
            db[g] += dbd * dte
            tt = jnp.sum(dbd * bm[g], axis=1, keepdims=True) * dte
            dacs += jnp.where(lane_l == h, -tt, 0.0)
            cdh = jnp.exp(last)
            dcd = jnp.sum(jnp.sum(jnp.where(rh, dsp, 0.0), axis=1, keepdims=True), axis=0, keepdims=True)
            dlast += jnp.where(cm["lane"] == h, jnp.sum(tt, axis=0, keepdims=True) + dcd * cdh, 0.0)
            cd_rows += jnp.where(rh, cdh, 0.0)
        dacs += jnp.where(rowi == L - 1, dlast, 0.0)
        dacs += _dot(dacs_b, expand, NT, precision=HI)
        dstate[...] = dprev + dst * cd_rows

        for g in range(2):
            dgb = dgm[g].astype(BF16)
            dc[g] += _dot(dgb, bm_bf[g])
            db[g] += _dot(dgb, cm_bf[g], TN)

        dadt = _dot(tri.astype(F32), dacs, TN, precision=HI)
        ddtv = dadt * cm["av"] + _dot(dxdt * xs, expand, NT, precision=HI)
        dalog_ref[...] += _rowsum(dadt * cm["dtv"]) * cm["av"]
        dxs += dxdt * dt_b
        draw = jnp.where(cm["m4"], ddtv * _sig(cm["raw"]), 0.0)
        ddtb_ref[...] += _rowsum(draw)
        ddt_ref[...] = draw.astype(BF16)

        dact = jnp.concatenate([dxs, db[0], db[1], dc[0], dc[1]], axis=1)
        sg, pre = cm["sg"], cm["pre"]
        dpre = dact * sg * (1.0 + pre * (1.0 - sg))
        dcb_ref[...] += _rowsum(dpre)
        for kk in range(4):
            dcw_ref[kk:kk + 1, :] += _rowsum(dpre * ext[pl.ds(HALO - 3 + kk, L), :])
        ext2[0:L, :] = dpre
        ext2[L:L + HALO, :] = carry[...]
        dx = cw_ref[3:4, :] * ext2[pl.ds(0, L), :]
        for kk in range(3):
            dx = dx + cw_ref[kk:kk + 1, :] * ext2[pl.ds(3 - kk, L), :]
        dxbc_ref[...] = dx.astype(BF16)
        carry[...] = dpre[0:HALO, :]

        @pl.when(i == nc - 1)
        def _():
            dd_ref[...] = _dot(ddl[...], expand, NT, precision=HI)

    def vec(w):
        return pl.BlockSpec((1, w), lambda i: (0, 0))

    def rv(i):
        return nc - 1 - i

    return pl.pallas_call(
        body, name="ssd_bwd", grid=(nc,),
        out_shape=(jax.ShapeDtypeStruct((s, SSD_DIM), BF16), jax.ShapeDtypeStruct((s, SSD_CONV), BF16), jax.ShapeDtypeStruct((s, LANE), BF16),
                   jax.ShapeDtypeStruct((4, SSD_CONV), F32), jax.ShapeDtypeStruct((1, SSD_CONV), F32), jax.ShapeDtypeStruct((1, LANE), F32),
                   jax.ShapeDtypeStruct((1, LANE), F32), jax.ShapeDtypeStruct((8, LANE), F32), jax.ShapeDtypeStruct((1, SSD_DIM), F32)),
        in_specs=[pl.BlockSpec((L, SSD_DIM), lambda i: (rv(i), 5)), pl.BlockSpec((L, SSD_CONV), lambda i: (rv(i), 2)),
                  pl.BlockSpec((HALO, SSD_CONV), lambda i: (_prev_halo(rv(i), L), 2)), pl.BlockSpec((L, LANE), lambda i: (rv(i), 18)),
                  pl.BlockSpec((L, SSD_DIM), lambda i: (rv(i), 5)), pl.BlockSpec((L, SSD_DIM), lambda i: (rv(i), 0)),
                  pl.BlockSpec((None, SSD_DIM, LANE), lambda i: (rv(i), 0, 0)),
                  pl.BlockSpec((4, SSD_CONV), lambda i: (0, 0)), vec(SSD_CONV), vec(LANE), vec(LANE), vec(SSD_DIM), vec(SSD_DIM)],
        out_specs=(pl.BlockSpec((L, SSD_DIM), lambda i: (rv(i), 0)), pl.BlockSpec((L, SSD_CONV), lambda i: (rv(i), 0)),
                   pl.BlockSpec((L, LANE), lambda i: (rv(i), 0)), pl.BlockSpec((4, SSD_CONV), lambda i: (0, 0)), vec(SSD_CONV),
                   vec(LANE), vec(LANE), pl.BlockSpec((8, LANE), lambda i: (0, 0)), vec(SSD_DIM)),
        scratch_shapes=[pltpu.VMEM((L + HALO, SSD_CONV), F32), pltpu.VMEM((L + HALO, SSD_CONV), F32), pltpu.VMEM((HALO, SSD_CONV), F32),
                        pltpu.VMEM((SSD_DIM, LANE), F32), pltpu.VMEM((8, SSD_DIM), F32)],
        compiler_params=_cp("arbitrary"),
    )(proj, proj, proj, proj, dcat, ypre, states, cw, cb, dtb, alog, dvec, nw)


def _adamw(parts, w, m, v, name):
    nl, r, c = w.shape
    tr = r
    for cand in (256, 128, 64, 32, 16, 8):
        if r % cand == 0 and (cand * c * 4) <= 2 * 1024 * 1024:
            tr = cand
            break
    c1 = 1.0 - B1 ** STEP
    c2 = 1.0 - B2 ** STEP

    def body(p_ref, w_ref, m_ref, v_ref, g_ref, d_ref, nm_ref, nv_ref):
        g = p_ref[0].astype(F32)
        for dev in range(1, NDEV):
            g = g + p_ref[dev].astype(F32)
        mn = B1 * m_ref[...] + (1.0 - B1) * g
        vn = B2 * v_ref[...] + (1.0 - B2) * (g * g)
        g_ref[...] = g
        nm_ref[...] = mn
        nv_ref[...] = vn
        d_ref[...] = -LR * ((mn / c1) / (jnp.sqrt(vn / c2) + AEPS) + WD * w_ref[...])

    blk = pl.BlockSpec((None, tr, c), lambda l, i: (l, i, 0))
    out = jax.ShapeDtypeStruct((nl, r, c), F32)
    return pl.pallas_call(
        body, name=name, grid=(nl, r // tr), out_shape=(out, out, out, out),
        in_specs=[pl.BlockSpec((NDEV, None, tr, c), lambda l, i: (0, l, i, 0)), blk, blk, blk], out_specs=(blk, blk, blk, blk),
        compiler_params=_cp("parallel", "parallel"),
    )(parts, w, m, v)


def _pad_win(w):
    z = lambda n: jnp.zeros((w.shape[0], n), w.dtype)
    return jnp.concatenate([w[:, :384], z(64), w[:, 384:416], z(32), w[:, 416:], z(124)], axis=1)


def _unpad_win(g):
    return jnp.concatenate([g[:, :384], g[:, 448:480], g[:, 512:2308]], axis=1)


def _pad_wout(w):
    att = jnp.pad(w[:512].reshape(HEADS, 64, D), ((0, 0), (64, 0), (0, 0))).reshape(HEADS * LANE, D)
    return jnp.concatenate([att, w[512:]], axis=0)


def _unpad_wout(g):
    att = g[:HEADS * LANE].reshape(HEADS, LANE, D)[:, 64:, :].reshape(512, D)
    return jnp.concatenate([att, g[HEADS * LANE:]], axis=0)


def _lanes(v, n=LANE):
    return jnp.pad(v, (0, n - v.shape[0])).reshape(1, n)


def _prep_ffn(big):
    return {"wout": _pad_wout(big["w_out"].reshape(1024, D)), "wup": big["ffn_w_up"], "fcw": big["ffn_conv_w"].astype(F32),
            "wdown": big["ffn_w_down"].reshape(4, FB, D)}


def _prep_layer(big, small, l):
    p = _prep_ffn(big) if "w_out" in big else {}
    p["win"] = _pad_win(big["w_in"].reshape(D, 2212))
    p["wq"] = jnp.pad(big["mla_w_q_up"], ((0, 0), (0, 0), (0, LANE - 96)))
    p["wkv"] = big["mla_w_kv_up"]
    p["scw"] = big["sc_conv_w"].astype(F32).transpose(1, 0, 2).reshape(3, SC)
    p["ssdcw"] = big["ssd_conv_w"].astype(F32).transpose(1, 0, 2).reshape(4, SSD_CONV)
    for nm in ("norm_mix_pre", "norm_mix_post", "norm_ffn_pre", "norm_ffn_post", "mla_q_norm", "mla_kv_norm", "ssd_conv_b", "ssd_norm"):
        p[nm] = small[nm][l].reshape(1, -1)
    p["dtb"] = _lanes(small["ssd_dt_bias"][l])
    p["alog"] = _lanes(small["ssd_a_log"][l])
    p["dvec"] = jnp.repeat(small["ssd_d"][l], 64).reshape(1, SSD_DIM)
    p["fcb"] = small["ffn_conv_b"][l].reshape(NDEV, 1, FB)
    return p


def _rope_tables(positions):
    inv_freq = 1.0 / (ROPE_THETA ** (jnp.arange(0, ROPE, 2, dtype=F32) / ROPE))
    ang = positions.astype(F32)[:, None] * inv_freq
    cos, sin = jnp.cos(ang), jnp.sin(ang)
    s = positions.shape[0]
    z = lambda n: jnp.zeros((s, n), F32)
    tc = jnp.concatenate([jnp.ones((s, 64), F32), cos, cos, z(32)], axis=1)
    ta = jnp.concatenate([z(64), -sin, z(48)], axis=1)
    tb = jnp.concatenate([z(80), sin, z(32)], axis=1)
    return tc, ta, tb


def _layer_fwd(xv, p, tabs, prefetch=None, prep_rest=None, h=None, next_norm=None):
    if h is None:
        h = _rms(xv, p["norm_mix_pre"], BF16, "rms_pre")
    proj = _mm_rows("in_proj", h, p["win"], BF16, NN)
    q, k, kv = _mla_prep(proj, tabs, p["mla_q_norm"], p["mla_kv_norm"], p["wq"], p["wkv"])
    o, lse, gathered = _flash_fwd(q, k, kv, prefetch)
    if prep_rest is not None:
        p = {**p, **prep_rest(gathered)}
    yconv = _sconv_fwd(proj, p["scw"])
    yssd, ypre, states = _ssd_fwd(proj, p["ssdcw"], p["ssd_conv_b"], p["dtb"], p["alog"], p["dvec"], p["ssd_norm"])
    cat = jnp.concatenate([o, yconv, yssd], axis=1)
    mixed = _mm_rows("out_proj", cat, p["wout"], BF16, NN)
    x1, h2 = _add_rms(xv, mixed, p["norm_mix_post"], "add_rms", p["norm_ffn_pre"])
    upre = _mm_up(h2, p["wup"])
    act = _ffn_act(upre, p["fcw"], p["fcb"])
    f = _mm_down(act, p["wdown"])
    x2 = _add_rms(x1, f, p["norm_ffn_post"], "add_rms", next_norm)
    saved = dict(x=xv, h=h, proj=proj, q=q, k=k, kv=kv, lse=lse, ypre=ypre, states=states, cat=cat, mixed=mixed, x1=x1, h2=h2,
                 upre=upre, act=act, f=f)
    return x2, saved, p, gathered


def _pack_grads(grads, group_ids):
    return [_group_pack(GROUPS[gi], lambda n: grads[n].reshape((NDEV,) + _rows2(n, True)), (NDEV,)) for gi in group_ids]


def _layer_bwd(dx2, sv, p, tabs, exchange=False, pending=None, head=None, below=None):
    df, g_nfpo = head if head is not None else _rms_bwd(sv["f"], p["norm_ffn_post"], dx2, None, BF16, "rms_bwd_post")
    dact = _mm_dact(df, p["wdown"])
    g_wdown = _mm_dwdown(sv["act"], df)
    dupre, g_fcb, g_fcw = _ffn_bwd(sv["upre"], dact, p["fcw"], p["fcb"])
    dh2 = _mm_dh2(dupre, p["wup"])
    g_wup = _mm_dwup(sv["h2"], dupre)
    dx1, dmixed, g_nfp, g_nmpo = _rms_bwd2(sv["x1"], p["norm_ffn_pre"], dh2, dx2, sv["mixed"], p["norm_mix_post"])
    dcat = _mm_rows("dcat", dmixed, p["wout"], BF16, NT)
    g_wout = _mm_wgrad("dw_out", sv["cat"], dmixed, BF16)
    big = {
        "w_out": _unpad_wout(g_wout).reshape(NDEV, 128, D),
        "ffn_w_up": g_wup,
        "ffn_conv_w": g_fcw.astype(BF16),
        "ffn_w_down": g_wdown.reshape(NDEV, 352, D),
    }
    outgoing = _pack_grads(big, FFN_SIDE) + (pending or []) if exchange else None
    dq, dk, dv, received = _flash_bwd(sv["q"], sv["k"], sv["kv"], sv["cat"], dcat, sv["lse"], outgoing)
    dcq, dckv, dkr, g_wq, g_wkv, g_qn, g_kvn = _mla_prep_bwd(sv["proj"], tabs, p["mla_q_norm"], p["mla_kv_norm"], p["wq"], p["wkv"], dq, dk, dv)
    dscb, dscc, dsch, g_scw = _sconv_bwd(sv["proj"], dcat, p["scw"])
    dz, dxbc, ddt, g_cw, g_cb, g_dtb, g_alog, g_d, g_nw = _ssd_bwd(
        sv["proj"], dcat, sv["ypre"], sv["states"], p["ssdcw"], p["ssd_conv_b"], p["dtb"], p["alog"], p["dvec"], p["ssd_norm"])
    dproj = jnp.concatenate([dcq, dckv, dkr, dscb, dscc, dsch, dz, dxbc, ddt], axis=1)
    dh = _mm_rows("dh", dproj, p["win"], BF16, NT)
    g_win = _mm_wgrad("dw_in", sv["h"], dproj, BF16)
    big.update({
        "w_in": _unpad_win(g_win).reshape(NDEV, 128, 2212),
        "mla_w_q_up": g_wq[:, :, :96].astype(BF16),
        "mla_w_kv_up": g_wkv.astype(BF16),
        "sc_conv_w": g_scw.reshape(3, NDEV, 32).transpose(1, 0, 2).astype(BF16),
        "ssd_conv_w": g_cw.reshape(4, NDEV, 96).transpose(1, 0, 2).astype(BF16),
    })
    head_below, last_received = None, None
    if below is not None:
        dx, df_below, g_nmp, g_below = _rms_bwd2(sv["x"], p["norm_mix_pre"], dh, dx1, *below)
        head_below = (df_below, g_below)
    elif exchange:
        dx, g_nmp, last_received = _rms_bwd(sv["x"], p["norm_mix_pre"], dh, dx1, F32, "rms_bwd_pre", _pack_grads(big, ATT_SIDE))
    else:
        dx, g_nmp = _rms_bwd(sv["x"], p["norm_mix_pre"], dh, dx1, F32, "rms_bwd_pre")
    small = {
        "norm_mix_pre": g_nmp[0], "norm_mix_post": g_nmpo[0], "norm_ffn_pre": g_nfp[0], "norm_ffn_post": g_nfpo[0],
        "mla_q_norm": g_qn[0], "mla_kv_norm": g_kvn[0], "ssd_conv_b": g_cb[0], "ssd_dt_bias": g_dtb[0, :SSD_H],
        "ssd_a_log": g_alog[0, :SSD_H], "ssd_d": g_d[0, :SSD_H], "ssd_norm": g_nw[0], "ffn_conv_b": g_fcb.reshape(-1),
    }
    return dx, big, small, received, head_below, last_received


def _local_step(xv, positions, target, layers):
    tabs = _rope_tables(positions)
    saved = []
    for p in layers:
        xv, sv, _, _ = _layer_fwd(xv, p, tabs)
        saved.append(sv)
    loss, dx = _loss_head(xv, target)
    bigs, smalls = [None] * DEPTH, [None] * DEPTH
    head = None
    for l in reversed(range(len(layers))):
        below = (saved[l - 1]["f"], layers[l - 1]["norm_ffn_post"]) if l > 0 else None
        dx, bigs[l], smalls[l], _, head, _ = _layer_bwd(dx, saved[l], layers[l], tabs, head=head, below=below)
    return loss[0, 0], dx, bigs, smalls


def _rows2(n, layer=False):
    shape = SHAPES[n][1:] if layer else SHAPES[n]
    return (math.prod(shape[:-1]), shape[-1])


def _group_pack(group, get, lead):
    width, names = group
    pieces = []
    for n in names:
        rows, cols = _rows2(n, True)
        pad = [(0, 0)] * len(lead) + [(0, -rows % 16), (0, width - cols)]
        pieces.append(jnp.pad(get(n), pad))
    return pieces[0] if len(pieces) == 1 else jnp.concatenate(pieces, axis=len(lead))


def _group_unpack(group, buf):
    _, names = group
    res, off = {}, 0
    for n in names:
        rows, cols = _rows2(n, True)
        res[n] = buf[:, off:off + rows, :cols]
        off += rows + (-rows % 16)
    return res


def kernel(x, positions, norm_mix_pre, norm_mix_post, norm_ffn_pre, norm_ffn_post, w_in, mla_q_norm, mla_w_q_up, mla_kv_norm, mla_w_kv_up, sc_conv_w, ssd_conv_w, ssd_conv_b, ssd_dt_bias, ssd_a_log, ssd_d, ssd_norm, w_out, ffn_w_up, ffn_conv_w, ffn_conv_b, ffn_w_down, loss_target, m_norm_mix_pre, m_norm_mix_post, m_norm_ffn_pre, m_norm_ffn_post, m_w_in, m_mla_q_norm, m_mla_w_q_up, m_mla_kv_norm, m_mla_w_kv_up, m_sc_conv_w, m_ssd_conv_w, m_ssd_conv_b, m_ssd_dt_bias, m_ssd_a_log, m_ssd_d, m_ssd_norm, m_w_out, m_ffn_w_up, m_ffn_conv_w, m_ffn_conv_b, m_ffn_w_down, v_norm_mix_pre, v_norm_mix_post, v_norm_ffn_pre, v_norm_ffn_post, v_w_in, v_mla_q_norm, v_mla_w_q_up, v_mla_kv_norm, v_mla_w_kv_up, v_sc_conv_w, v_ssd_conv_w, v_ssd_conv_b, v_ssd_dt_bias, v_ssd_a_log, v_ssd_d, v_ssd_norm, v_w_out, v_ffn_w_up, v_ffn_conv_w, v_ffn_conv_b, v_ffn_w_down):
    given = dict(locals())
    w = {n: given[n] for n in WEIGHTS}
    m = {n: given["m_" + n] for n in WEIGHTS}
    v = {n: given["v_" + n] for n in WEIGHTS}

    def shards(l, group_ids):
        return [_group_pack(GROUPS[gi], lambda n: w[n][l].astype(BF16).reshape(_rows2(n, True)), ()) for gi in group_ids]

    def unpacked(bufs, group_ids):
        big = {}
        for gi, buf in zip(group_ids, bufs):
            for n, piece in _group_unpack(GROUPS[gi], buf).items():
                big[n] = piece.reshape((NDEV,) + SHAPES[n][1:])
        return big

    small_w = {n: w[n] for n, _ in SMALL}
    tabs = _rope_tables(positions[0])
    xv, h, layers, saved = x[0], None, [], []
    att = _all_gather(shards(0, ATT_SIDE), "gather_weights")
    for l in range(DEPTH):
        prefetch = shards(l, FFN_SIDE) + (shards(l + 1, ATT_SIDE) if l + 1 < DEPTH else [])
        nxt = w["norm_mix_pre"][l + 1].reshape(1, D) if l + 1 < DEPTH else None
        xv, sv, p, gathered = _layer_fwd(xv, _prep_layer(unpacked(att, ATT_SIDE), small_w, l), tabs, prefetch,
                                         lambda got: _prep_ffn(unpacked(got[:len(FFN_SIDE)], FFN_SIDE)), h, nxt)
        xv, h = xv if nxt is not None else (xv, None)
        att = gathered[len(FFN_SIDE):]
        layers.append(p)
        saved.append(sv)
    loss, dx = _loss_head(xv, loss_target[0])
    loss = lax.psum(loss[0, 0], ("x", "y", "c"))

    smalls, pending, head = [None] * DEPTH, None, None
    recvs = [[None] * len(GROUPS) for _ in range(DEPTH)]
    for l in reversed(range(DEPTH)):
        below = (saved[l - 1]["f"], layers[l - 1]["norm_ffn_post"]) if l > 0 else None
        dx, grads, smalls[l], received, head, last = _layer_bwd(dx, saved[l], layers[l], tabs, True, pending, head, below)
        for pos, gi in enumerate(FFN_SIDE):
            recvs[l][gi] = received[pos]
        if pending is not None:
            for pos, gi in enumerate(ATT_SIDE):
                recvs[l + 1][gi] = received[len(FFN_SIDE) + pos]
        pending = _pack_grads(grads, ATT_SIDE) if l > 0 else None
    for gi, buf in zip(ATT_SIDE, last):
        recvs[0][gi] = buf
    out = {}
    for gi, g in enumerate(GROUPS):
        per_layer = [_group_unpack(g, recvs[l][gi]) for l in range(DEPTH)]
        for n in g[1]:
            parts = jnp.stack([per_layer[l][n] for l in range(DEPTH)], axis=1)
            out[n] = _adamw(parts, w[n], m[n], v[n], "adamw_" + n)

    total = sum(width for _, width in SMALL)
    padded = -(-total // (8 * LANE)) * 8 * LANE
    pk = lambda a: jnp.pad(a, ((0, 0), (0, padded - total))).reshape(1, DEPTH * padded // LANE, LANE)
    sflat = jnp.stack([jnp.concatenate([smalls[l][n] for n, _ in SMALL]) for l in range(DEPTH)])
    sparts = _all_gather([pk(sflat)[0]], "gather_small_grads")[0]
    pw = lambda d: pk(jnp.concatenate([d[n] for n, _ in SMALL], axis=1))
    res = _adamw(sparts[:, None], pw(w), pw(m), pw(v), "adamw_small")
    off = 0
    for n, width in SMALL:
        out[n] = [a.reshape(DEPTH, padded)[:, off:off + width] for a in res]
        off += width

    return (loss, dx[None], *[out[n][0] for n in WEIGHTS], *[out[n][1] for n in WEIGHTS],
            *[out[n][2] for n in WEIGHTS], *[out[n][3] for n in WEIGHTS])
```

```python
import functools
import math

import jax
import jax.numpy as jnp
from jax import lax
from jax.experimental import pallas as pl
from jax.experimental.pallas import tpu as pltpu

F32 = jnp.float32
BF16 = jnp.bfloat16

D = 1024
DEPTH = 4
NDEV = 8
HEADS = 8
QL = 256
KVL = 128
ROPE = 32
NOPE = 64
SC = 256
SSD_DIM = 256
SSD_CONV = 768
SSD_H = 4
SSD_L = 128
FFN = 2816
FB = 704
EPS = 1e-6
ROPE_THETA = 10000.0
ATT_SCALE = 96 ** -0.5
LOG2E = 1.4426950408889634
LR, B1, B2, AEPS, WD, STEP = 0.001, 0.9, 0.999, 1e-08, 0.01, 10

PW = 2432
CATW = 1536

ROW_TILE = 512
ROW_CHUNK = 16
NORM_CHUNK = 32
NORM_TILE = 512
MM_TILE = 1024
ATT_TILE = 256
ATT_QW = 2
FWD_HEADS = 4
BWD_HEADS = 2
HALO = 16
LANE = 128
NEG = -1e30
NN = (((1,), (0,)), ((), ()))
NT = (((1,), (1,)), ((), ()))
TN = (((0,), (0,)), ((), ()))
VMEM_LIMIT = 56 * 1024 * 1024

SHARDED = (
    ("w_in", (4, 128, 2212)),
    ("mla_w_q_up", (4, 256, 96)),
    ("mla_w_kv_up", (4, 128, 128)),
    ("sc_conv_w", (4, 3, 32)),
    ("ssd_conv_w", (4, 4, 96)),
    ("w_out", (4, 128, 1024)),
    ("ffn_w_up", (4, 1024, 704)),
    ("ffn_conv_w", (4, 3, 704)),
    ("ffn_w_down", (4, 352, 1024)),
)
SHAPES = dict(SHARDED)
GROUPS = (
    (2212, ("w_in",)),
    (1024, ("w_out",)),
    (704, ("ffn_w_up",)),
    (96, ("mla_w_q_up", "ssd_conv_w", "sc_conv_w")),
    (128, ("mla_w_kv_up",)),
    (1024, ("ffn_w_down",)),
    (704, ("ffn_conv_w",)),
)
ATT_SIDE = (0, 3, 4)
FFN_SIDE = (1, 2, 5, 6)
SMALL = (
    ("norm_mix_pre", 1024), ("norm_mix_post", 1024), ("norm_ffn_pre", 1024), ("norm_ffn_post", 1024),
    ("mla_q_norm", 256), ("mla_kv_norm", 128), ("ssd_conv_b", 768), ("ssd_dt_bias", 4), ("ssd_a_log", 4),
    ("ssd_d", 4), ("ssd_norm", 256), ("ffn_conv_b", 5632),
)
WEIGHTS = ("norm_mix_pre", "norm_mix_post", "norm_ffn_pre", "norm_ffn_post", "w_in", "mla_q_norm", "mla_w_q_up",
           "mla_kv_norm", "mla_w_kv_up", "sc_conv_w", "ssd_conv_w", "ssd_conv_b", "ssd_dt_bias", "ssd_a_log", "ssd_d",
           "ssd_norm", "w_out", "ffn_w_up", "ffn_conv_w", "ffn_conv_b", "ffn_w_down")


def _dot(a, b, dims=NN, precision=None):
    return lax.dot_general(a, b, dims, precision=precision, preferred_element_type=F32)


def _dot01(x, c, dims=NN, x_first=True):
    cb = c.astype(BF16)
    hi = x.astype(BF16)
    rest = x - hi.astype(F32)
    mid = rest.astype(BF16)
    lo = (rest - mid.astype(F32)).astype(BF16)
    one = (lambda t: _dot(t, cb, dims)) if x_first else (lambda t: _dot(cb, t, dims))
    return one(hi) + one(mid) + one(lo)


def _sig(v):
    return 1.0 / (1.0 + jnp.exp(-v))


def _cp(*sem):
    return pltpu.CompilerParams(dimension_semantics=sem, vmem_limit_bytes=VMEM_LIMIT)


def _rowsum(v):
    return jnp.sum(v, axis=0, keepdims=True)


def _prev_halo(i, ts):
    return jnp.maximum(i * (ts // HALO) - 1, 0)


def _next_halo(i, ts, n):
    return jnp.minimum((i + 1) * (ts // HALO), n * (ts // HALO) - 1)


def _gather_plan(x_refs, out_refs, send_sems, recv_sems, local_sems):
    n = len(x_refs)
    x, y, cc = lax.axis_index("x"), lax.axis_index("y"), lax.axis_index("c")
    me, sibling = (x, y, cc), (x, y, 1 - cc)
    chips = [(1 - x, y), (x, 1 - y), (1 - x, 1 - y)]

    def rows(t, px, py, pc):
        return out_refs[t].at[4 * px + 2 * py + pc]

    def copy(t, k, block, to, own=False):
        return pltpu.make_async_remote_copy(
            src_ref=x_refs[t] if own else rows(t, *block), dst_ref=rows(t, *block),
            send_sem=send_sems.at[7 * t + k], recv_sem=recv_sems.at[7 * t + k], device_id=to, device_id_type=pl.DeviceIdType.MESH)

    def local(t):
        return pltpu.make_async_copy(x_refs[t], rows(t, *me), local_sems.at[t])

    def start():
        for t in range(n):
            local(t).start()
            copy(t, 0, me, sibling, own=True).start()
            for j, chip in enumerate(chips):
                copy(t, 1 + j, me, (*chip, cc), own=True).start()

    def finish():
        for j, chip in enumerate(chips):
            for t in range(n):
                copy(t, 1 + j, (*chip, cc), me).wait_recv()
                copy(t, 4 + j, (*chip, cc), sibling).start()
        for t in range(n):
            copy(t, 0, sibling, me).wait_recv()
            for j, chip in enumerate(chips):
                copy(t, 4 + j, (*chip, 1 - cc), me).wait_recv()
        for t in range(n):
            copy(t, 0, me, sibling, own=True).wait_send()
            for j, chip in enumerate(chips):
                copy(t, 1 + j, me, (*chip, cc), own=True).wait_send()
                copy(t, 4 + j, (*chip, cc), sibling).wait_send()
            local(t).wait()

    return start, finish


def _exchange_plan(x_refs, out_refs, send_sems, recv_sems, local_sems):
    n = len(x_refs)
    x, y, cc = lax.axis_index("x"), lax.axis_index("y"), lax.axis_index("c")
    me = 4 * x + 2 * y + cc

    def copies():
        res = [pltpu.make_async_copy(x_refs[t].at[me], out_refs[t].at[me], local_sems.at[t]) for t in range(n)]
        for k in range(1, NDEV):
            px = 1 - x if k & 4 else x
            py = 1 - y if k & 2 else y
            pc = 1 - cc if k & 1 else cc
            peer = 4 * px + 2 * py + pc
            for t in range(n):
                res.append(pltpu.make_async_remote_copy(
                    src_ref=x_refs[t].at[peer], dst_ref=out_refs[t].at[me], send_sem=send_sems.at[7 * t + k - 1],
                    recv_sem=recv_sems.at[7 * t + k - 1], device_id=(px, py, pc), device_id_type=pl.DeviceIdType.MESH))
        return res

    def start():
        for cp in copies():
            cp.start()

    def finish():
        for cp in copies():
            cp.wait()

    return start, finish


def _comm_scratch(n):
    return [pltpu.SemaphoreType.DMA((7 * n,)), pltpu.SemaphoreType.DMA((7 * n,)), pltpu.SemaphoreType.DMA((n,))]


ANY = pl.BlockSpec(memory_space=pl.ANY)


def _all_gather(xs, name):
    n = len(xs)

    def body(*refs):
        start, finish = _gather_plan(refs[:n], refs[n:2 * n], *refs[2 * n:])
        start()
        finish()

    return pl.pallas_call(
        body, name=name, out_shape=[jax.ShapeDtypeStruct((NDEV,) + a.shape, a.dtype) for a in xs],
        in_specs=[ANY] * n, out_specs=[ANY] * n, scratch_shapes=_comm_scratch(n),
    )(*xs)


def _mm(name, a, b, out_shape, grid, a_spec, b_spec, o_spec, dims, acc_shape):
    nk = grid[2]

    def single(a_ref, b_ref, o_ref):
        o_ref[...] = _dot(a_ref[...], b_ref[...], dims).astype(o_ref.dtype)

    if nk == 1:
        return pl.pallas_call(
            single, name=name, grid=grid, out_shape=out_shape, in_specs=[a_spec, b_spec], out_specs=o_spec,
            compiler_params=_cp("parallel", "parallel", "arbitrary"),
        )(a, b)

    def body(a_ref, b_ref, o_ref, acc_ref):
        k = pl.program_id(2)

        @pl.when(k == 0)
        def _():
            acc_ref[...] = jnp.zeros_like(acc_ref)

        acc_ref[...] += _dot(a_ref[...], b_ref[...], dims)

        @pl.when(k == nk - 1)
        def _():
            o_ref[...] = acc_ref[...].astype(o_ref.dtype)

    return pl.pallas_call(
        body, name=name, grid=grid, out_shape=out_shape, in_specs=[a_spec, b_spec], out_specs=o_spec,
        scratch_shapes=[pltpu.VMEM(acc_shape, F32)], compiler_params=_cp("parallel", "parallel", "arbitrary"),
    )(a, b)


def _mm_rows(name, a, w, out_dtype, dims):
    s, k = a.shape
    n = w.shape[1] if dims == NN else w.shape[0]
    tm = min(MM_TILE, s)
    return _mm(name, a, w, jax.ShapeDtypeStruct((s, n), out_dtype), (s // tm, 1, 1),
               pl.BlockSpec((tm, k), lambda i, j, kk: (i, 0)), pl.BlockSpec(w.shape, lambda i, j, kk: (0, 0)),
               pl.BlockSpec((tm, n), lambda i, j, kk: (i, 0)), dims, (tm, n))


def _mm_wgrad(name, a, g, out_dtype):
    s, m = a.shape
    n = g.shape[1]
    tk = min(MM_TILE, s)
    return _mm(name, a, g, jax.ShapeDtypeStruct((m, n), out_dtype), (1, 1, s // tk),
               pl.BlockSpec((tk, m), lambda i, j, kk: (kk, 0)), pl.BlockSpec((tk, n), lambda i, j, kk: (kk, 0)),
               pl.BlockSpec((m, n), lambda i, j, kk: (0, 0)), TN, (m, n))


def _mm_up(h2, wup):
    s = h2.shape[0]
    tm = min(MM_TILE, s)
    return _mm("ffn_up", h2, wup, jax.ShapeDtypeStruct((NDEV, s, FB), BF16), (NDEV, s // tm, 1),
               pl.BlockSpec((tm, D), lambda j, i, kk: (i, 0)), pl.BlockSpec((None, D, FB), lambda j, i, kk: (j, 0, 0)),
               pl.BlockSpec((None, tm, FB), lambda j, i, kk: (j, i, 0)), NN, (tm, FB))


def _mm_down(act, wdown):
    s = act.shape[1]
    tm = min(MM_TILE, s)
    return _mm("ffn_down", act, wdown, jax.ShapeDtypeStruct((s, D), BF16), (s // tm, 1, 4),
               pl.BlockSpec((None, tm, FB), lambda i, j, kk: (kk, i, 0)), pl.BlockSpec((None, FB, D), lambda i, j, kk: (kk, 0, 0)),
               pl.BlockSpec((tm, D), lambda i, j, kk: (i, 0)), NN, (tm, D))


def _mm_dact(df, wdown):
    s = df.shape[0]
    tm = min(MM_TILE, s)
    return _mm("ffn_dact", df, wdown, jax.ShapeDtypeStruct((4, s, FB), BF16), (4, s // tm, 1),
               pl.BlockSpec((tm, D), lambda j, i, kk: (i, 0)), pl.BlockSpec((None, FB, D), lambda j, i, kk: (j, 0, 0)),
               pl.BlockSpec((None, tm, FB), lambda j, i, kk: (j, i, 0)), NT, (tm, FB))


def _mm_dwdown(act, df):
    s = df.shape[0]
    tk = min(MM_TILE, s)
    return _mm("ffn_dwdown", act, df, jax.ShapeDtypeStruct((4, FB, D), BF16), (4, 1, s // tk),
               pl.BlockSpec((None, tk, FB), lambda j, i, kk: (j, kk, 0)), pl.BlockSpec((tk, D), lambda j, i, kk: (kk, 0)),
               pl.BlockSpec((None, FB, D), lambda j, i, kk: (j, 0, 0)), TN, (FB, D))


def _mm_dh2(dupre, wup):
    s = dupre.shape[1]
    tm = min(MM_TILE, s)
    return _mm("ffn_dh2", dupre, wup, jax.ShapeDtypeStruct((s, D), BF16), (s // tm, 1, NDEV),
               pl.BlockSpec((None, tm, FB), lambda i, j, kk: (kk, i, 0)), pl.BlockSpec((None, D, FB), lambda i, j, kk: (kk, 0, 0)),
               pl.BlockSpec((tm, D), lambda i, j, kk: (i, 0)), NT, (tm, D))


def _mm_dwup(h2, dupre):
    s = h2.shape[0]
    tk = min(MM_TILE, s)
    return _mm("ffn_dwup", h2, dupre, jax.ShapeDtypeStruct((NDEV, D, FB), BF16), (NDEV, 1, s // tk),
               pl.BlockSpec((tk, D), lambda j, i, kk: (kk, 0)), pl.BlockSpec((None, tk, FB), lambda j, i, kk: (j, kk, 0)),
               pl.BlockSpec((None, D, FB), lambda j, i, kk: (j, 0, 0)), TN, (D, FB))


def _rms(xv, w, out_dtype, name):
    s, d = xv.shape
    ts = min(ROW_TILE, s)

    def body(x_ref, w_ref, o_ref):
        for r0 in range(0, ts, NORM_CHUNK):
            rows = pl.ds(r0, NORM_CHUNK)
            xf = x_ref[rows, :].astype(F32)
            r = lax.rsqrt(jnp.mean(xf * xf, axis=-1, keepdims=True) + EPS)
            o_ref[rows, :] = (xf * r * w_ref[...]).astype(o_ref.dtype)

    return pl.pallas_call(
        body, name=name, grid=(s // ts,), out_shape=jax.ShapeDtypeStruct((s, d), out_dtype),
        in_specs=[pl.BlockSpec((ts, d), lambda i: (i, 0)), pl.BlockSpec((1, d), lambda i: (0, 0))],
        out_specs=pl.BlockSpec((ts, d), lambda i: (i, 0)), compiler_params=_cp("parallel"),
    )(xv, w)


def _add_rms(xv, mv, w, name, w_next=None):
    s, d = xv.shape
    ts = min(NORM_TILE, s)
    both = w_next is not None

    def body(*refs):
        x_ref, m_ref, w_ref = refs[:3]
        o_ref = refs[4] if both else refs[3]
        for r0 in range(0, ts, NORM_CHUNK):
            rows = pl.ds(r0, NORM_CHUNK)
            mf = m_ref[rows, :].astype(F32)
            r = lax.rsqrt(jnp.mean(mf * mf, axis=-1, keepdims=True) + EPS)
            y = x_ref[rows, :] + mf * r * w_ref[...]
            o_ref[rows, :] = y
            if both:
                r2 = lax.rsqrt(jnp.mean(y * y, axis=-1, keepdims=True) + EPS)
                refs[5][rows, :] = (y * r2 * refs[3][...]).astype(BF16)

    row = pl.BlockSpec((ts, d), lambda i: (i, 0))
    vec = pl.BlockSpec((1, d), lambda i: (0, 0))
    if both:
        return pl.pallas_call(
            body, name=name + "_rms", grid=(s // ts,),
            out_shape=(jax.ShapeDtypeStruct((s, d), F32), jax.ShapeDtypeStruct((s, d), BF16)),
            in_specs=[row, row, vec, vec], out_specs=(row, row), compiler_params=_cp("parallel"),
        )(xv, mv, w, w_next)
    return pl.pallas_call(
        body, name=name, grid=(s // ts,), out_shape=jax.ShapeDtypeStruct((s, d), F32),
        in_specs=[row, row, vec], out_specs=row, compiler_params=_cp("parallel"),
    )(xv, mv, w)


def _rms_bwd_math(xf, w, dy):
    r = lax.rsqrt(jnp.mean(xf * xf, axis=-1, keepdims=True) + EPS)
    xh = xf * r
    dxh = dy * w
    dx = r * (dxh - xh * jnp.mean(dxh * xh, axis=-1, keepdims=True))
    return dx, dy * xh


def _rms_bwd(xv, w, dy, dres, out_dtype, name, pending=None):
    s, d = xv.shape
    ts = min(NORM_TILE, s)
    with_res = dres is not None
    nin = 4 if with_res else 3
    nx = len(pending) if pending else 0

    def body(*refs):
        x_ref, w_ref, dy_ref = refs[:3]
        dres_ref = refs[3] if with_res else None
        dx_ref, dw_ref = refs[nin + nx:nin + nx + 2]
        acc = refs[nin + 2 * nx + 2]
        if nx:
            start, finish = _exchange_plan(refs[nin:nin + nx], refs[nin + nx + 2:nin + 2 * nx + 2], *refs[nin + 2 * nx + 3:])
            pl.when(pl.program_id(0) == 0)(start)
        acc[...] = jnp.zeros_like(acc)
        for r0 in range(0, ts, NORM_CHUNK):
            rows = pl.ds(r0, NORM_CHUNK)
            dx, dwt = _rms_bwd_math(x_ref[rows, :].astype(F32), w_ref[...], dy_ref[rows, :].astype(F32))
            if with_res:
                dx = dx + dres_ref[rows, :]
            dx_ref[rows, :] = dx.astype(dx_ref.dtype)
            acc[...] += dwt

        @pl.when(pl.program_id(0) == 0)
        def _():
            dw_ref[...] = jnp.zeros_like(dw_ref)

        dw_ref[...] += _rowsum(acc[...])
        if nx:
            pl.when(pl.program_id(0) == s // ts - 1)(finish)

    row = pl.BlockSpec((ts, d), lambda i: (i, 0))
    vec = pl.BlockSpec((1, d), lambda i: (0, 0))
    ins = [xv, w, dy] + ([dres] if with_res else []) + (pending or [])
    res = pl.pallas_call(
        body, name=name + "_exchange" if nx else name, grid=(s // ts,),
        out_shape=[jax.ShapeDtypeStruct((s, d), out_dtype), jax.ShapeDtypeStruct((1, d), F32)]
        + [jax.ShapeDtypeStruct(a.shape, a.dtype) for a in (pending or [])],
        in_specs=[row, vec, row] + ([row] if with_res else []) + [ANY] * nx, out_specs=[row, vec] + [ANY] * nx,
        scratch_shapes=[pltpu.VMEM((NORM_CHUNK, d), F32)] + (_comm_scratch(nx) if nx else []), compiler_params=_cp("arbitrary"),
    )(*ins)
    return (res[0], res[1], list(res[2:])) if nx else (res[0], res[1])


def _rms_bwd2(xa, wa, dya, dres, xb, wb):
    s, d = xa.shape
    ts = min(NORM_TILE, s)

    def body(xa_ref, wa_ref, dya_ref, dres_ref, xb_ref, wb_ref, da_ref, db_ref, dwa_ref, dwb_ref, acc):
        acc[...] = jnp.zeros_like(acc)
        for r0 in range(0, ts, NORM_CHUNK):
            rows = pl.ds(r0, NORM_CHUNK)
            da, dwt = _rms_bwd_math(xa_ref[rows, :].astype(F32), wa_ref[...], dya_ref[rows, :].astype(F32))
            da = da + dres_ref[rows, :]
            da_ref[rows, :] = da
            acc[0] += dwt
            db, dwt = _rms_bwd_math(xb_ref[rows, :].astype(F32), wb_ref[...], da)
            db_ref[rows, :] = db.astype(BF16)
            acc[1] += dwt

        @pl.when(pl.program_id(0) == 0)
        def _():
            dwa_ref[...] = jnp.zeros_like(dwa_ref)
            dwb_ref[...] = jnp.zeros_like(dwb_ref)

        dwa_ref[...] += _rowsum(acc[0])
        dwb_ref[...] += _rowsum(acc[1])

    row = pl.BlockSpec((ts, d), lambda i: (i, 0))
    vec = pl.BlockSpec((1, d), lambda i: (0, 0))
    return pl.pallas_call(
        body, name="rms_bwd2", grid=(s // ts,),
        out_shape=(jax.ShapeDtypeStruct((s, d), F32), jax.ShapeDtypeStruct((s, d), BF16), jax.ShapeDtypeStruct((1, d), F32),
                   jax.ShapeDtypeStruct((1, d), F32)),
        in_specs=[row, vec, row, row, row, vec], out_specs=(row, row, vec, vec),
        scratch_shapes=[pltpu.VMEM((2, NORM_CHUNK, d), F32)], compiler_params=_cp("arbitrary"),
    )(xa, wa, dya, dres, xb, wb)


def _loss_head(yv, tv):
    s, d = yv.shape
    ts = min(ROW_TILE, s)

    def body(y_ref, t_ref, l_ref, dy_ref):
        e = y_ref[...] - t_ref[...]
        dy_ref[...] = e * (1.0 / d)

        @pl.when(pl.program_id(0) == 0)
        def _():
            l_ref[...] = jnp.zeros_like(l_ref)

        tot = jnp.sum(jnp.sum(e * e, axis=1, keepdims=True), axis=0, keepdims=True)
        l_ref[...] += jnp.broadcast_to(tot * (0.5 / d), (8, LANE))

    row = pl.BlockSpec((ts, d), lambda i: (i, 0))
    return pl.pallas_call(
        body, name="loss_head", grid=(s // ts,),
        out_shape=(jax.ShapeDtypeStruct((8, LANE), F32), jax.ShapeDtypeStruct((s, d), F32)),
        in_specs=[row, row], out_specs=(pl.BlockSpec((8, LANE), lambda i: (0, 0)), row), compiler_params=_cp("arbitrary"),
    )(yv, tv)


def _rope(v, c, a, b):
    return v * c + pltpu.roll(v, LANE - 16, 1) * a + pltpu.roll(v, 16, 1) * b


def _rope_t(dv, c, a, b):
    return dv * c + pltpu.roll(dv * a, 16, 1) + pltpu.roll(dv * b, LANE - 16, 1)


def _mla_prep(proj, tabs, qnw, kvnw, wq, wkv):
    s = proj.shape[0]
    ts = min(ROW_TILE, s)
    tc, ta, tb = tabs

    def body(cq_ref, ckv_ref, kr_ref, c_ref, a_ref, b_ref, qnw_ref, kvnw_ref, wq_ref, wkv_ref, q_ref, k_ref, kv_ref):
        c, a, b = c_ref[...], a_ref[...], b_ref[...]
        cq = cq_ref[...].astype(F32)
        qn = (cq * lax.rsqrt(jnp.mean(cq * cq, axis=-1, keepdims=True) + EPS) * qnw_ref[...]).astype(BF16)
        ckv = ckv_ref[...].astype(F32)
        kvn = (ckv * lax.rsqrt(jnp.mean(ckv * ckv, axis=-1, keepdims=True) + EPS) * kvnw_ref[...]).astype(BF16)
        kr = _rope(kr_ref[...].astype(F32), c, a, b)
        lane = lax.broadcasted_iota(jnp.int32, (ts, LANE), 1)
        for h in range(HEADS):
            q_ref[h] = _rope(_dot(qn, wq_ref[h]), c, a, b).astype(BF16)
            kv = _dot(kvn, wkv_ref[h])
            kv_ref[h] = kv.astype(BF16)
            k_ref[h] = jnp.where(lane < NOPE, kv, kr).astype(BF16)

    tab = pl.BlockSpec((ts, LANE), lambda i: (i, 0))
    hd = pl.BlockSpec((HEADS, ts, LANE), lambda i: (0, i, 0))
    out = jax.ShapeDtypeStruct((HEADS, s, LANE), BF16)
    return pl.pallas_call(
        body, name="mla_prep", grid=(s // ts,), out_shape=(out, out, out),
        in_specs=[pl.BlockSpec((ts, QL), lambda i: (i, 0)), pl.BlockSpec((ts, LANE), lambda i: (i, 2)),
                  pl.BlockSpec((ts, LANE), lambda i: (i, 3)), tab, tab, tab,
                  pl.BlockSpec((1, QL), lambda i: (0, 0)), pl.BlockSpec((1, KVL), lambda i: (0, 0)),
                  pl.BlockSpec((HEADS, QL, LANE), lambda i: (0, 0, 0)), pl.BlockSpec((HEADS, KVL, LANE), lambda i: (0, 0, 0))],
        out_specs=(hd, hd, hd), compiler_params=_cp("parallel"),
    )(proj, proj, proj, tc, ta, tb, qnw, kvnw, wq, wkv)


def _mla_prep_bwd(proj, tabs, qnw, kvnw, wq, wkv, dq, dk, dv):
    s = proj.shape[0]
    ts = min(ROW_TILE, s)
    tc, ta, tb = tabs

    def body(cq_ref, ckv_ref, c_ref, a_ref, b_ref, qnw_ref, kvnw_ref, wq_ref, wkv_ref, dq_ref, dk_ref, dv_ref,
             dcq_ref, dckv_ref, dkr_ref, dwq_ref, dwkv_ref, dqnw_ref, dkvnw_ref):
        @pl.when(pl.program_id(0) == 0)
        def _():
            dwq_ref[...] = jnp.zeros_like(dwq_ref)
            dwkv_ref[...] = jnp.zeros_like(dwkv_ref)
            dqnw_ref[...] = jnp.zeros_like(dqnw_ref)
            dkvnw_ref[...] = jnp.zeros_like(dkvnw_ref)

        c, a, b = c_ref[...], a_ref[...], b_ref[...]
        cq = cq_ref[...].astype(F32)
        qn = (cq * lax.rsqrt(jnp.mean(cq * cq, axis=-1, keepdims=True) + EPS) * qnw_ref[...]).astype(BF16)
        ckv = ckv_ref[...].astype(F32)
        kvn = (ckv * lax.rsqrt(jnp.mean(ckv * ckv, axis=-1, keepdims=True) + EPS) * kvnw_ref[...]).astype(BF16)
        lane = lax.broadcasted_iota(jnp.int32, (ts, LANE), 1)
        dqn = jnp.zeros((ts, QL), F32)
        dkvn = jnp.zeros((ts, KVL), F32)
        dkr = jnp.zeros((ts, LANE), F32)
        for h in range(HEADS):
            dqh = _rope_t(dq_ref[h], c, a, b).astype(BF16)
            dwq_ref[h] += _dot(qn, dqh, TN)
            dqn += _dot(dqh, wq_ref[h], NT)
            dkh = dk_ref[h].astype(F32)
            dkvh = jnp.where(lane < NOPE, dkh, dv_ref[h].astype(F32)).astype(BF16)
            dkr += jnp.where(lane < NOPE, 0.0, dkh)
            dwkv_ref[h] += _dot(kvn, dkvh, TN)
            dkvn += _dot(dkvh, wkv_ref[h], NT)
        dkr_ref[...] = _rope_t(dkr, c, a, b).astype(BF16)
        dcq, dwt = _rms_bwd_math(cq, qnw_ref[...], dqn)
        dcq_ref[...] = dcq.astype(BF16)
        dqnw_ref[...] += _rowsum(dwt)
        dckv, dwt = _rms_bwd_math(ckv, kvnw_ref[...], dkvn)
        dckv_ref[...] = dckv.astype(BF16)
        dkvnw_ref[...] += _rowsum(dwt)

    tab = pl.BlockSpec((ts, LANE), lambda i: (i, 0))
    hd = pl.BlockSpec((HEADS, ts, LANE), lambda i: (0, i, 0))
    wq_spec = pl.BlockSpec((HEADS, QL, LANE), lambda i: (0, 0, 0))
    wkv_spec = pl.BlockSpec((HEADS, KVL, LANE), lambda i: (0, 0, 0))
    return pl.pallas_call(
        body, name="mla_prep_bwd", grid=(s // ts,),
        out_shape=(jax.ShapeDtypeStruct((s, QL), BF16), jax.ShapeDtypeStruct((s, KVL), BF16), jax.ShapeDtypeStruct((s, LANE), BF16),
                   jax.ShapeDtypeStruct((HEADS, QL, LANE), F32), jax.ShapeDtypeStruct((HEADS, KVL, LANE), F32),
                   jax.ShapeDtypeStruct((1, QL), F32), jax.ShapeDtypeStruct((1, KVL), F32)),
        in_specs=[pl.BlockSpec((ts, QL), lambda i: (i, 0)), pl.BlockSpec((ts, LANE), lambda i: (i, 2)), tab, tab, tab,
                  pl.BlockSpec((1, QL), lambda i: (0, 0)), pl.BlockSpec((1, KVL), lambda i: (0, 0)), wq_spec, wkv_spec, hd, hd, hd],
        out_specs=(pl.BlockSpec((ts, QL), lambda i: (i, 0)), pl.BlockSpec((ts, KVL), lambda i: (i, 0)), tab, wq_spec, wkv_spec,
                   pl.BlockSpec((1, QL), lambda i: (0, 0)), pl.BlockSpec((1, KVL), lambda i: (0, 0))),
        compiler_params=_cp("arbitrary"),
    )(proj, proj, tc, ta, tb, qnw, kvnw, wq, wkv, dq, dk, dv)


def _transpose_bf16(v):
    return v.astype(F32).T.astype(BF16)


def _flash_fwd(q, k, kv, prefetch=None):
    s = q.shape[1]
    t = min(ATT_TILE, s)
    tq = ATT_QW * t
    n = s // tq
    g = FWD_HEADS
    nx = len(prefetch) if prefetch else 0

    def body(*refs):
        q_ref, k_ref, kv_ref = refs[:3]
        o_ref, lse_ref = refs[3 + nx:5 + nx]
        kvt_sc = refs[5 + 2 * nx]
        step = pl.program_id(0) * n + pl.program_id(1)
        if nx:
            start, finish = _gather_plan(refs[3:3 + nx], refs[5 + nx:5 + 2 * nx], *refs[6 + 2 * nx:])
            pl.when(step == 0)(start)
        attend(q_ref, k_ref, kv_ref, o_ref, lse_ref, kvt_sc)
        if nx:
            pl.when(step == (HEADS // g) * n - 1)(finish)

    def attend(q_ref, k_ref, kv_ref, o_ref, lse_ref, kvt_sc):
        i = pl.program_id(1)

        @pl.when(i == 0)
        def _():
            ones_rows = lax.broadcasted_iota(jnp.int32, (LANE, s), 0) < NOPE
            for hh in range(g):
                kvt_sc[hh] = jnp.where(ones_rows, 1.0, kv_ref[hh].astype(F32).T).astype(BF16)

        qt = [(q_ref[hh].astype(F32) * (ATT_SCALE * LOG2E)).T.astype(BF16) for hh in range(g)]
        kpos = lax.broadcasted_iota(jnp.int32, (t, tq), 0)
        qpos = lax.broadcasted_iota(jnp.int32, (t, tq), 1) + i * tq

        def chunk(j, carry, diagonal):
            start = pl.multiple_of(j * t, t)
            scs = [_dot(k_ref[hh, pl.ds(start, t), :], qt[hh]) for hh in range(g)]
            soft = []
            for hh in range(g):
                sc = scs[hh]
                if diagonal:
                    sc = jnp.where(qpos >= kpos + start, sc, NEG)
                m_new = jnp.maximum(carry[hh][0], jnp.max(sc, axis=0, keepdims=True))
                soft.append((m_new, jnp.exp2(carry[hh][0] - m_new), jnp.exp2(sc - m_new).astype(BF16)))
            pvs = [_dot(kvt_sc[hh, :, pl.ds(start, t)], soft[hh][2]) for hh in range(g)]
            return tuple((soft[hh][0], soft[hh][1] * carry[hh][1] + pvs[hh]) for hh in range(g))

        init = tuple((jnp.full((1, tq), NEG, F32), jnp.zeros((LANE, tq), F32)) for _ in range(g))
        carry = lax.fori_loop(0, ATT_QW * i, lambda j, c: chunk(j, c, False), init)
        for d in range(ATT_QW):
            carry = chunk(ATT_QW * i + d, carry, True)
        for hh in range(g):
            m, acc = carry[hh]
            l = acc[0:1, :]
            o_ref[:, hh * LANE:(hh + 1) * LANE] = (acc / l).T.astype(BF16)
            lse_ref[hh] = m + jnp.log2(l)

    whole = pl.BlockSpec((g, s, LANE), lambda h, i: (h, 0, 0))
    res = pl.pallas_call(
        body, name="flash_fwd_gather" if nx else "flash_fwd", grid=(HEADS // g, n),
        out_shape=[jax.ShapeDtypeStruct((s, HEADS * LANE), BF16), jax.ShapeDtypeStruct((HEADS, 1, s), F32)]
        + [jax.ShapeDtypeStruct((NDEV,) + a.shape, a.dtype) for a in (prefetch or [])],
        in_specs=[pl.BlockSpec((g, tq, LANE), lambda h, i: (h, i, 0)), whole, whole] + [ANY] * nx,
        out_specs=[pl.BlockSpec((tq, g * LANE), lambda h, i: (i, h)), pl.BlockSpec((g, 1, tq), lambda h, i: (h, 0, i))] + [ANY] * nx,
        scratch_shapes=[pltpu.VMEM((g, LANE, s), BF16)] + (_comm_scratch(nx) if nx else []),
        compiler_params=_cp("arbitrary", "arbitrary"),
    )(q, k, kv, *(prefetch or []))
    return res[0], res[1], list(res[2:])


def _flash_bwd(q, k, kv, cat, dcat, lse, pending=None):
    s = q.shape[1]
    t = min(ATT_TILE, s)
    tq = ATT_QW * t
    n = s // t
    g = BWD_HEADS
    nx = len(pending) if pending else 0

    def body(*refs):
        ins, outs, scr = refs[:6], refs[6 + nx:9 + nx], refs[9 + 2 * nx:14 + 2 * nx]
        step = pl.program_id(0) * n + pl.program_id(1)
        if nx:
            start, finish = _exchange_plan(refs[6:6 + nx], refs[9 + nx:9 + 2 * nx], *refs[14 + 2 * nx:])
            pl.when(step == 0)(start)
        attend(*ins, *outs, *scr)
        if nx:
            pl.when(step == (HEADS // g) * n - 1)(finish)

    def attend(q_ref, k_ref, kv_ref, o_ref, do_ref, lse_ref, dq_ref, dk_ref, dv_ref, qt_sc, dot_sc, delta_sc, dqt_sc, qs_sc):
        j = pl.program_id(1)

        @pl.when(j == 0)
        def _():
            for hh in range(g):
                lanes = slice(hh * LANE, (hh + 1) * LANE)
                qf = q_ref[hh].astype(F32)
                qt_sc[hh] = (qf * (ATT_SCALE * LOG2E)).T.astype(BF16)
                qs_sc[hh] = (qf * ATT_SCALE).astype(BF16)
                dof = do_ref[:, lanes].astype(F32)
                dot_sc[hh] = dof.T.astype(BF16)
                delta_sc[hh] = _dot01(dof * o_ref[:, lanes].astype(F32), jnp.ones((8, LANE), F32), NT, x_first=False)
            dqt_sc[...] = jnp.zeros_like(dqt_sc)

        kjt = [_transpose_bf16(k_ref[hh]) for hh in range(g)]
        kpos = lax.broadcasted_iota(jnp.int32, (t, tq), 0) + j * t
        qpos = lax.broadcasted_iota(jnp.int32, (t, tq), 1)

        def chunk(i, carry, diagonal):
            start = pl.multiple_of(i * tq, tq)
            cols = pl.ds(start, tq)
            scs = [_dot(k_ref[hh], qt_sc[hh, :, cols]) for hh in range(g)]
            dps = [_dot(kv_ref[hh], dot_sc[hh, :, cols]) for hh in range(g)]
            pds = []
            for hh in range(g):
                p = jnp.exp2(scs[hh] - lse_ref[hh, :, cols])
                if diagonal:
                    p = jnp.where(qpos + start >= kpos, p, 0.0)
                ds = (p * (dps[hh] - delta_sc[hh, 0:1, cols])).astype(BF16)
                pds.append((p.astype(BF16), ds))
            out = []
            for hh in range(g):
                dk, dv = carry[hh]
                dv = dv + _dot(pds[hh][0], do_ref[pl.ds(start, tq), hh * LANE:(hh + 1) * LANE])
                dk = dk + _dot(pds[hh][1], qs_sc[hh, pl.ds(start, tq), :])
                dqt_sc[hh, :, cols] += _dot(kjt[hh], pds[hh][1])
                out.append((dk, dv))
            return tuple(out)

        zero = jnp.zeros((t, LANE), F32)
        first = lax.div(j, ATT_QW)
        carry = chunk(first, tuple((zero, zero) for _ in range(g)), True)
        carry = lax.fori_loop(first + 1, s // tq, lambda i, c: chunk(i, c, False), carry)
        for hh in range(g):
            dk_ref[hh] = carry[hh][0].astype(BF16)
            dv_ref[hh] = carry[hh][1].astype(BF16)

        @pl.when(j == n - 1)
        def _():
            for hh in range(g):
                dq_ref[hh] = (dqt_sc[hh] * ATT_SCALE).T

    whole = pl.BlockSpec((g, s, LANE), lambda h, j: (h, 0, 0))
    kspec = pl.BlockSpec((g, t, LANE), lambda h, j: (h, j, 0))
    ospec = pl.BlockSpec((s, g * LANE), lambda h, j: (0, h))
    res = pl.pallas_call(
        body, name="flash_bwd_exchange" if nx else "flash_bwd", grid=(HEADS // g, n),
        out_shape=[jax.ShapeDtypeStruct((HEADS, s, LANE), F32), jax.ShapeDtypeStruct((HEADS, s, LANE), BF16),
                   jax.ShapeDtypeStruct((HEADS, s, LANE), BF16)] + [jax.ShapeDtypeStruct(a.shape, a.dtype) for a in (pending or [])],
        in_specs=[whole, kspec, kspec, ospec, ospec, pl.BlockSpec((g, 1, s), lambda h, j: (h, 0, 0))] + [ANY] * nx,
        out_specs=[whole, kspec, kspec] + [ANY] * nx,
        scratch_shapes=[pltpu.VMEM((g, LANE, s), BF16), pltpu.VMEM((g, LANE, s), BF16), pltpu.VMEM((g, 8, s), F32),
                        pltpu.VMEM((g, LANE, s), F32), pltpu.VMEM((g, s, LANE), BF16)] + (_comm_scratch(nx) if nx else []),
        compiler_params=_cp("arbitrary", "arbitrary"),
    )(q, k, kv, cat, dcat, lse, *(pending or []))
    return res[0], res[1], res[2], list(res[3:])


def _conv3(ext, w_ref, ts):
    return (w_ref[0:1, :] * ext[pl.ds(HALO - 2, ts), :] + w_ref[1:2, :] * ext[pl.ds(HALO - 1, ts), :]
            + w_ref[2:3, :] * ext[pl.ds(HALO, ts), :])


def _conv3_rows(ext, w_ref, r):
    return (w_ref[0:1, :] * ext[pl.ds(HALO - 2 + r, ROW_CHUNK), :] + w_ref[1:2, :] * ext[pl.ds(HALO - 1 + r, ROW_CHUNK), :]
            + w_ref[2:3, :] * ext[pl.ds(HALO + r, ROW_CHUNK), :])


def _conv3_t(ext2, w_ref, ts):
    return (w_ref[0:1, :] * ext2[pl.ds(2, ts), :] + w_ref[1:2, :] * ext2[pl.ds(1, ts), :] + w_ref[2:3, :] * ext2[pl.ds(0, ts), :])


def _sconv_fwd(proj, w):
    s = proj.shape[0]
    ts = min(ROW_TILE, s)

    def body(b_ref, c_ref, h_ref, hc_ref, hh_ref, w_ref, o_ref, ext):
        i = pl.program_id(0)
        ext[0:HALO, :] = hc_ref[...].astype(F32) * hh_ref[...].astype(F32) * (i > 0).astype(F32)
        ext[HALO:HALO + ts, :] = c_ref[...].astype(F32) * h_ref[...].astype(F32)
        o_ref[...] = (b_ref[...].astype(F32) * _conv3(ext, w_ref, ts)).astype(BF16)

    def col(cb):
        return pl.BlockSpec((ts, SC), lambda i: (i, cb))

    def halo(cb):
        return pl.BlockSpec((HALO, SC), lambda i: (_prev_halo(i, ts), cb))

    return pl.pallas_call(
        body, name="sconv_fwd", grid=(s // ts,), out_shape=jax.ShapeDtypeStruct((s, SC), BF16),
        in_specs=[col(2), col(3), col(4), halo(3), halo(4), pl.BlockSpec((3, SC), lambda i: (0, 0))],
        out_specs=pl.BlockSpec((ts, SC), lambda i: (i, 0)), scratch_shapes=[pltpu.VMEM((ts + HALO, SC), F32)],
        compiler_params=_cp("parallel"),
    )(proj, proj, proj, proj, proj, w)


def _sconv_bwd(proj, dcat, w):
    s = proj.shape[0]
    ts = min(ROW_TILE, s)
    n = s // ts

    def body(b_ref, c_ref, h_ref, hc_ref, hh_ref, dy_ref, ndy_ref, nb_ref, w_ref, db_ref, dc_ref, dh_ref, dw_ref, ext, ext2):
        i = pl.program_id(0)

        @pl.when(i == 0)
        def _():
            dw_ref[...] = jnp.zeros_like(dw_ref)

        cv, hv, bv = c_ref[...].astype(F32), h_ref[...].astype(F32), b_ref[...].astype(F32)
        ext[0:HALO, :] = hc_ref[...].astype(F32) * hh_ref[...].astype(F32) * (i > 0).astype(F32)
        ext[HALO:HALO + ts, :] = cv * hv
        dy = dy_ref[...].astype(F32)
        db_ref[...] = (dy * _conv3(ext, w_ref, ts)).astype(BF16)
        dyb = dy * bv
        ext2[0:ts, :] = dyb
        ext2[ts:ts + HALO, :] = ndy_ref[...].astype(F32) * nb_ref[...].astype(F32) * (i < n - 1).astype(F32)
        dg = _conv3_t(ext2, w_ref, ts)
        dc_ref[...] = (dg * hv).astype(BF16)
        dh_ref[...] = (dg * cv).astype(BF16)
        for kk in range(3):
            dw_ref[kk:kk + 1, :] += _rowsum(dyb * ext[pl.ds(HALO - 2 + kk, ts), :])

    def col(cb):
        return pl.BlockSpec((ts, SC), lambda i: (i, cb))

    def halo(cb):
        return pl.BlockSpec((HALO, SC), lambda i: (_prev_halo(i, ts), cb))

    def nxt(cb):
        return pl.BlockSpec((HALO, SC), lambda i: (_next_halo(i, ts, n), cb))

    out = jax.ShapeDtypeStruct((s, SC), BF16)
    o0 = pl.BlockSpec((ts, SC), lambda i: (i, 0))
    return pl.pallas_call(
        body, name="sconv_bwd", grid=(n,), out_shape=(out, out, out, jax.ShapeDtypeStruct((3, SC), F32)),
        in_specs=[col(2), col(3), col(4), halo(3), halo(4), col(4), nxt(4), nxt(2), pl.BlockSpec((3, SC), lambda i: (0, 0))],
        out_specs=(o0, o0, o0, pl.BlockSpec((3, SC), lambda i: (0, 0))),
        scratch_shapes=[pltpu.VMEM((ts + HALO, SC), F32), pltpu.VMEM((ts + HALO, SC), F32)], compiler_params=_cp("arbitrary"),
    )(proj, proj, proj, proj, proj, dcat, dcat, proj, w)


def _ffn_stage(ext, u_ref, halo_ref, i, ts):
    ext[0:HALO, :] = halo_ref[...].astype(F32) * (i > 0).astype(F32)
    ext[HALO:HALO + ts, :] = u_ref[...].astype(F32)


def _ffn_specs(ts):
    cur = pl.BlockSpec((2, None, ts, FB), lambda j, i: (0, j, i, 0))
    halo = pl.BlockSpec((2, None, HALO, FB), lambda j, i: (0, j, _prev_halo(i, ts), 0))
    w = pl.BlockSpec((2, None, 3, FB), lambda j, i: (0, j, 0, 0))
    b = pl.BlockSpec((2, None, 1, FB), lambda j, i: (0, j, 0, 0))
    return cur, halo, w, b


def _ffn_act(upre, fcw, fcb):
    s = upre.shape[1]
    ts = min(ROW_TILE, s)

    def body(u_ref, halo_ref, w_ref, b_ref, o_ref, ext_g, ext_u):
        i = pl.program_id(1)
        _ffn_stage(ext_g, u_ref.at[0], halo_ref.at[0], i, ts)
        _ffn_stage(ext_u, u_ref.at[1], halo_ref.at[1], i, ts)
        for r in range(0, ts, ROW_CHUNK):
            gate = b_ref[0] + _conv3_rows(ext_g, w_ref.at[0], r)
            up = b_ref[1] + _conv3_rows(ext_u, w_ref.at[1], r)
            o_ref[pl.ds(r, ROW_CHUNK), :] = (gate * _sig(gate) * up).astype(BF16)

    cur, halo, w, b = _ffn_specs(ts)
    u4 = upre.reshape(2, 4, s, FB)
    return pl.pallas_call(
        body, name="ffn_act", grid=(4, s // ts), out_shape=jax.ShapeDtypeStruct((4, s, FB), BF16),
        in_specs=[cur, halo, w, b], out_specs=pl.BlockSpec((None, ts, FB), lambda j, i: (j, i, 0)),
        scratch_shapes=[pltpu.VMEM((ts + HALO, FB), F32), pltpu.VMEM((ts + HALO, FB), F32)], compiler_params=_cp("parallel", "parallel"),
    )(u4, u4, fcw.reshape(2, 4, 3, FB), fcb.reshape(2, 4, 1, FB))


def _ffn_bwd(upre, dact, fcw, fcb):
    s = upre.shape[1]
    ts = min(ROW_TILE, s)
    n = s // ts
    te = ts + HALO

    def body(u_ref, halo_ref, nxt_ref, w_ref, b_ref, da_ref, nda_ref, dup_ref, db_ref, dw_ref, ext_g, ext_u, ext_da, du_g, du_u, acc):
        i = pl.program_id(1)

        @pl.when(i == 0)
        def _():
            db_ref[...] = jnp.zeros_like(db_ref)
            dw_ref[...] = jnp.zeros_like(dw_ref)

        more = (i < n - 1).astype(F32)
        for idx, ext in ((0, ext_g), (1, ext_u)):
            _ffn_stage(ext, u_ref.at[idx], halo_ref.at[idx], i, ts)
            ext[HALO + ts:HALO + te, :] = nxt_ref[idx].astype(F32) * more
        ext_da[0:ts, :] = da_ref[...].astype(F32)
        ext_da[ts:te, :] = nda_ref[...].astype(F32) * more
        acc[...] = jnp.zeros_like(acc)
        for r in range(0, te, ROW_CHUNK):
            rows = pl.ds(r, ROW_CHUNK)
            gate = b_ref[0] + _conv3_rows(ext_g, w_ref.at[0], r)
            up = b_ref[1] + _conv3_rows(ext_u, w_ref.at[1], r)
            sg = _sig(gate)
            da = ext_da[rows, :]
            dgate = da * up * sg * (1.0 + gate * (1.0 - sg))
            dup = da * gate * sg
            du_g[rows, :] = dgate
            du_u[rows, :] = dup
            if r < ts:
                acc[0] += dgate
                acc[1] += dup
        for r in range(0, ts, ROW_CHUNK):
            for idx, du, ext in ((0, du_g, ext_g), (1, du_u, ext_u)):
                d0 = du[pl.ds(r, ROW_CHUNK), :]
                dupre = (w_ref[idx, 2:3, :] * d0 + w_ref[idx, 1:2, :] * du[pl.ds(r + 1, ROW_CHUNK), :]
                         + w_ref[idx, 0:1, :] * du[pl.ds(r + 2, ROW_CHUNK), :])
                dup_ref[idx, pl.ds(r, ROW_CHUNK), :] = dupre.astype(BF16)
                for kk in range(3):
                    acc[2 + 3 * idx + kk] += d0 * ext[pl.ds(HALO - 2 + kk + r, ROW_CHUNK), :]
        for idx in range(2):
            db_ref[idx] += _rowsum(acc[idx])
            for kk in range(3):
                dw_ref[idx, kk:kk + 1, :] += _rowsum(acc[2 + 3 * idx + kk])

    cur, halo, w, b = _ffn_specs(ts)
    nxt = pl.BlockSpec((2, None, HALO, FB), lambda j, i: (0, j, _next_halo(i, ts, n), 0))
    u4 = upre.reshape(2, 4, s, FB)
    dupre, db, dw = pl.pallas_call(
        body, name="ffn_bwd", grid=(4, n),
        out_shape=(jax.ShapeDtypeStruct((2, 4, s, FB), BF16), jax.ShapeDtypeStruct((2, 4, 1, FB), F32),
                   jax.ShapeDtypeStruct((2, 4, 3, FB), F32)),
        in_specs=[cur, halo, nxt, w, b, pl.BlockSpec((None, ts, FB), lambda j, i: (j, i, 0)),
                  pl.BlockSpec((None, HALO, FB), lambda j, i: (j, _next_halo(i, ts, n), 0))],
        out_specs=(cur, b, w),
        scratch_shapes=[pltpu.VMEM((te + HALO, FB), F32), pltpu.VMEM((te + HALO, FB), F32), pltpu.VMEM((te, FB), F32),
                        pltpu.VMEM((te, FB), F32), pltpu.VMEM((te, FB), F32), pltpu.VMEM((8, ROW_CHUNK, FB), F32)],
        compiler_params=_cp("parallel", "arbitrary"),
    )(u4, u4, u4, fcw.reshape(2, 4, 3, FB), fcb.reshape(2, 4, 1, FB), dact, dact)
    return dupre.reshape(NDEV, s, FB), db.reshape(NDEV, 1, FB), dw.reshape(NDEV, 3, FB)


def _softplus(v):
    e = jnp.exp(-jnp.abs(v))
    return jnp.maximum(v, 0.0) + jnp.where(e < 1e-4, e * (1.0 - 0.5 * e), jnp.log(1.0 + e))


def _ssd_consts():
    L = SSD_L
    r = lax.broadcasted_iota(jnp.int32, (L, L), 0)
    c = lax.broadcasted_iota(jnp.int32, (L, L), 1)
    tri = r >= c
    er = lax.broadcasted_iota(jnp.int32, (LANE, SSD_DIM), 0)
    ec = lax.broadcasted_iota(jnp.int32, (LANE, SSD_DIM), 1)
    expand = ((ec >= er * 64) & (ec < er * 64 + 64)).astype(F32)
    return tri, expand


def _ssd_conv4(ext, cw_ref, cb_ref):
    L = SSD_L
    pre = cb_ref[...] + cw_ref[0:1, :] * ext[pl.ds(HALO - 3, L), :]
    for kk in range(1, 4):
        pre = pre + cw_ref[kk:kk + 1, :] * ext[pl.ds(HALO - 3 + kk, L), :]
    return pre


def _ssd_common(xbc_ref, halo_ref, dt_ref, cw_ref, cb_ref, dtb_ref, alog_ref, ext, first):
    L = SSD_L
    tri, expand = _ssd_consts()
    ext[0:HALO, :] = halo_ref[...].astype(F32) * (1.0 - first.astype(F32))
    ext[HALO:HALO + L, :] = xbc_ref[...].astype(F32)
    pre = _ssd_conv4(ext, cw_ref, cb_ref)
    sg = _sig(pre)
    act = pre * sg
    lane = lax.broadcasted_iota(jnp.int32, (1, LANE), 1)
    m4 = lane < SSD_H
    raw = dt_ref[...].astype(F32) + dtb_ref[...]
    dtv = jnp.where(m4, _softplus(raw), 0.0)
    av = jnp.where(m4, -jnp.exp(alog_ref[...]), 0.0)
    adt = dtv * av
    acs = _dot01(adt, tri, x_first=False)
    acs_b = _dot01(acs, expand)
    dt_b = _dot01(dtv, expand)
    return dict(tri=tri, expand=expand, pre=pre, sg=sg, act=act, raw=raw, dtv=dtv, av=av, m4=m4, acs=acs, acs_b=acs_b,
                dt_b=dt_b, lane=lane)


def _head_terms(cm, h):
    L = SSD_L
    acs, tri = cm["acs"], cm["tri"]
    lane_l = lax.broadcasted_iota(jnp.int32, (L, LANE), 1)
    sub_l = lax.broadcasted_iota(jnp.int32, (LANE, L), 0)
    col = jnp.sum(jnp.where(lane_l == h, acs, 0.0), axis=1, keepdims=True)
    row = jnp.sum(jnp.where(sub_l == h, acs.T, 0.0), axis=0, keepdims=True)
    dec = jnp.where(tri, jnp.exp(jnp.where(tri, col - row, NEG)), 0.0)
    rowi = lax.broadcasted_iota(jnp.int32, (L, 1), 0)
    last = jnp.sum(jnp.where(rowi == L - 1, col, 0.0), axis=0, keepdims=True)
    dte = jnp.exp(last - col)
    return col, dec, last, dte


def _ssd_fwd(proj, cw, cb, dtb, alog, dvec, nw):
    s = proj.shape[0]
    L = SSD_L
    nc = s // L

    def body(z_ref, xbc_ref, halo_ref, dt_ref, cw_ref, cb_ref, dtb_ref, alog_ref, d_ref, nw_ref, y_ref, ypre_ref, st_ref, ext, state):
        i = pl.program_id(0)

        @pl.when(i == 0)
        def _():
            state[...] = jnp.zeros_like(state)

        cm = _ssd_common(xbc_ref, halo_ref, dt_ref, cw_ref, cb_ref, dtb_ref, alog_ref, ext, i == 0)
        act = cm["act"]
        xs = act[:, 0:256]
        bm = (act[:, 256:384], act[:, 384:512])
        cmat = (act[:, 512:640].astype(BF16), act[:, 640:768].astype(BF16))
        xdt = xs * cm["dt_b"]
        prev = state[...]
        st_ref[...] = prev
        prev_bf = prev.astype(BF16)
        gm = [_dot(cmat[g], bm[g].astype(BF16), NT) for g in range(2)]
        lane2 = lax.broadcasted_iota(jnp.int32, (1, SSD_DIM), 1)
        rows2 = lax.broadcasted_iota(jnp.int32, (SSD_DIM, 1), 0)
        ydiag = jnp.zeros((L, SSD_DIM), F32)
        contrib = jnp.zeros((SSD_DIM, LANE), F32)
        cd_rows = jnp.zeros((SSD_DIM, 1), F32)
        for h in range(SSD_H):
            g = h // 2
            col, dec, last, dte = _head_terms(cm, h)
            mh = (lane2 >= 64 * h) & (lane2 < 64 * h + 64)
            xm = jnp.where(mh, xdt, 0.0).astype(BF16)
            ydiag += _dot((gm[g] * dec).astype(BF16), xm)
            contrib += _dot(xm, (bm[g] * dte).astype(BF16), TN)
            cd_rows += jnp.where((rows2 >= 64 * h) & (rows2 < 64 * h + 64), jnp.exp(last), 0.0)
        yo = jnp.where(lane2 < 128, _dot(cmat[0], prev_bf, NT), _dot(cmat[1], prev_bf, NT))
        y = ydiag + yo * jnp.exp(cm["acs_b"]) + xs * d_ref[...]
        state[...] = prev * cd_rows + contrib
        ypre_ref[...] = y
        zz = z_ref[...].astype(F32)
        gt = y * zz * _sig(zz)
        y_ref[...] = (gt * lax.rsqrt(jnp.mean(gt * gt, axis=-1, keepdims=True) + EPS) * nw_ref[...]).astype(BF16)

    def vec(w):
        return pl.BlockSpec((1, w), lambda i: (0, 0))

    return pl.pallas_call(
        body, name="ssd_fwd", grid=(nc,),
        out_shape=(jax.ShapeDtypeStruct((s, SSD_DIM), BF16), jax.ShapeDtypeStruct((s, SSD_DIM), F32),
                   jax.ShapeDtypeStruct((nc, SSD_DIM, LANE), F32)),
        in_specs=[pl.BlockSpec((L, SSD_DIM), lambda i: (i, 5)), pl.BlockSpec((L, SSD_CONV), lambda i: (i, 2)),
                  pl.BlockSpec((HALO, SSD_CONV), lambda i: (_prev_halo(i, L), 2)), pl.BlockSpec((L, LANE), lambda i: (i, 18)),
                  pl.BlockSpec((4, SSD_CONV), lambda i: (0, 0)), vec(SSD_CONV), vec(LANE), vec(LANE), vec(SSD_DIM), vec(SSD_DIM)],
        out_specs=(pl.BlockSpec((L, SSD_DIM), lambda i: (i, 0)), pl.BlockSpec((L, SSD_DIM), lambda i: (i, 0)),
                   pl.BlockSpec((None, SSD_DIM, LANE), lambda i: (i, 0, 0))),
        scratch_shapes=[pltpu.VMEM((L + HALO, SSD_CONV), F32), pltpu.VMEM((SSD_DIM, LANE), F32)], compiler_params=_cp("arbitrary"),
    )(proj, proj, proj, proj, cw, cb, dtb, alog, dvec, nw)


def _ssd_bwd(proj, dcat, ypre, states, cw, cb, dtb, alog, dvec, nw):
    s = proj.shape[0]
    L = SSD_L
    nc = s // L

    def body(z_ref, xbc_ref, halo_ref, dt_ref, dy_ref, ypre_ref, st_ref, cw_ref, cb_ref, dtb_ref, alog_ref, d_ref, nw_ref,
             dz_ref, dxbc_ref, ddt_ref, dcw_ref, dcb_ref, ddtb_ref, dalog_ref, dd_ref, dnw_ref, ext, ext2, carry, dstate, ddl):
        i = pl.program_id(0)
        r = nc - 1 - i

        @pl.when(i == 0)
        def _():
            for ref in (dcw_ref, dcb_ref, ddtb_ref, dalog_ref, dd_ref, dnw_ref, carry, dstate, ddl):
                ref[...] = jnp.zeros_like(ref)

        cm = _ssd_common(xbc_ref, halo_ref, dt_ref, cw_ref, cb_ref, dtb_ref, alog_ref, ext, r == 0)
        tri, expand, act = cm["tri"], cm["expand"], cm["act"]
        xs = act[:, 0:256]
        bm = (act[:, 256:384], act[:, 384:512])
        cmat = (act[:, 512:640], act[:, 640:768])
        bm_bf = [v.astype(BF16) for v in bm]
        cm_bf = [v.astype(BF16) for v in cmat]
        dt_b = cm["dt_b"]
        xdt = xs * dt_b
        xdt_bf = xdt.astype(BF16)
        ea_b = jnp.exp(cm["acs_b"])
        prev = st_ref[...]
        prev_bf = prev.astype(BF16)
        lane2 = lax.broadcasted_iota(jnp.int32, (1, SSD_DIM), 1)
        rows2 = lax.broadcasted_iota(jnp.int32, (SSD_DIM, 1), 0)
        lane_l = lax.broadcasted_iota(jnp.int32, (L, LANE), 1)
        rowi = lax.broadcasted_iota(jnp.int32, (L, 1), 0)

        y = ypre_ref[...]
        zz = z_ref[...].astype(F32)
        sz = _sig(zz)
        gt = y * zz * sz
        dgt, dwt = _rms_bwd_math(gt, nw_ref[...], dy_ref[...].astype(F32))
        dnw_ref[...] += _rowsum(dwt)
        dy = dgt * zz * sz
        dz_ref[...] = (dgt * y * sz * (1.0 + zz * (1.0 - sz))).astype(BF16)

        ddl[0:1, :] += _rowsum(dy * xs)
        dxs = dy * d_ref[...]

        yo = jnp.where(lane2 < 128, _dot(cm_bf[0], prev_bf, NT), _dot(cm_bf[1], prev_bf, NT))
        dacs_b = dy * yo * ea_b
        dyo = dy * ea_b
        dyo_g = (jnp.where(lane2 < 128, dyo, 0.0).astype(BF16), jnp.where(lane2 >= 128, dyo, 0.0).astype(BF16))
        dc = [_dot(dyo_g[g], prev_bf) for g in range(2)]
        dprev = _dot(dyo_g[0], cm_bf[0], TN) + _dot(dyo_g[1], cm_bf[1], TN)

        gm = [_dot(cm_bf[g], bm_bf[g], NT) for g in range(2)]
        dgm = [jnp.zeros((L, L), F32), jnp.zeros((L, L), F32)]
        db = [jnp.zeros((L, LANE), F32), jnp.zeros((L, LANE), F32)]
        dxdt = jnp.zeros((L, SSD_DIM), F32)
        dacs = jnp.zeros((L, LANE), F32)
        dlast = jnp.zeros((1, LANE), F32)
        cd_rows = jnp.zeros((SSD_DIM, 1), F32)
        dst = dstate[...]
        dst_bf = dst.astype(BF16)
        dsp = dst * prev
        ones = jnp.ones((L, LANE), F32)
        for h in range(SSD_H):
            g = h // 2
            col, dec, last, dte = _head_terms(cm, h)
            mh = (lane2 >= 64 * h) & (lane2 < 64 * h + 64)
            rh = (rows2 >= 64 * h) & (rows2 < 64 * h + 64)
            sc = gm[g] * dec
            xm = jnp.where(mh, xdt, 0.0).astype(BF16)
            dym = jnp.where(mh, dy, 0.0).astype(BF16)
            dsc = _dot(dym, xdt_bf, NT)
            dxdt += _dot(sc.astype(BF16), dym, TN)
            dgm[g] += dsc * dec
            dd = dsc * sc
            rs = jnp.sum(dd, axis=1, keepdims=True)
            cs = _dot01(dd, ones, TN)
            dacs += jnp.where(lane_l == h, rs - cs, 0.0)
            bd = (bm[g] * dte).astype(BF16)
            dxdt += jnp.where(mh, _dot(bd, dst_bf, NT), 0.0)
            dbd = _dot(xm, dst_bf)
            db[g] += dbd * dte
            tt = jnp.sum(dbd * bm[g], axis=1, keepdims=True) * dte
            dacs += jnp.where(lane_l == h, -tt, 0.0)
            cdh = jnp.exp(last)
            dcd = jnp.sum(jnp.sum(jnp.where(rh, dsp, 0.0), axis=1, keepdims=True), axis=0, keepdims=True)
            dlast += jnp.where(cm["lane"] == h, jnp.sum(tt, axis=0, keepdims=True) + dcd * cdh, 0.0)
            cd_rows += jnp.where(rh, cdh, 0.0)
        dacs += jnp.where(rowi == L - 1, dlast, 0.0)
        dacs += _dot01(dacs_b, expand, NT)
        dstate[...] = dprev + dst * cd_rows

        for g in range(2):
            dgb = dgm[g].astype(BF16)
            dc[g] += _dot(dgb, bm_bf[g])
            db[g] += _dot(dgb, cm_bf[g], TN)

        dadt = _dot01(dacs, tri, TN, x_first=False)
        ddtv = dadt * cm["av"] + _dot01(dxdt * xs, expand, NT)
        dalog_ref[...] += _rowsum(dadt * cm["dtv"]) * cm["av"]
        dxs += dxdt * dt_b
        draw = jnp.where(cm["m4"], ddtv * _sig(cm["raw"]), 0.0)
        ddtb_ref[...] += _rowsum(draw)
        ddt_ref[...] = draw.astype(BF16)

        dact = jnp.concatenate([dxs, db[0], db[1], dc[0], dc[1]], axis=1)
        sg, pre = cm["sg"], cm["pre"]
        dpre = dact * sg * (1.0 + pre * (1.0 - sg))
        dcb_ref[...] += _rowsum(dpre)
        for kk in range(4):
            dcw_ref[kk:kk + 1, :] += _rowsum(dpre * ext[pl.ds(HALO - 3 + kk, L), :])
        ext2[0:L, :] = dpre
        ext2[L:L + HALO, :] = carry[...]
        dx = cw_ref[3:4, :] * ext2[pl.ds(0, L), :]
        for kk in range(3):
            dx = dx + cw_ref[kk:kk + 1, :] * ext2[pl.ds(3 - kk, L), :]
        dxbc_ref[...] = dx.astype(BF16)
        carry[...] = dpre[0:HALO, :]

        @pl.when(i == nc - 1)
        def _():
            dd_ref[...] = _dot01(ddl[...], expand, NT)

    def vec(w):
        return pl.BlockSpec((1, w), lambda i: (0, 0))

    def rv(i):
        return nc - 1 - i

    return pl.pallas_call(
        body, name="ssd_bwd", grid=(nc,),
        out_shape=(jax.ShapeDtypeStruct((s, SSD_DIM), BF16), jax.ShapeDtypeStruct((s, SSD_CONV), BF16), jax.ShapeDtypeStruct((s, LANE), BF16),
                   jax.ShapeDtypeStruct((4, SSD_CONV), F32), jax.ShapeDtypeStruct((1, SSD_CONV), F32), jax.ShapeDtypeStruct((1, LANE), F32),
                   jax.ShapeDtypeStruct((1, LANE), F32), jax.ShapeDtypeStruct((8, LANE), F32), jax.ShapeDtypeStruct((1, SSD_DIM), F32)),
        in_specs=[pl.BlockSpec((L, SSD_DIM), lambda i: (rv(i), 5)), pl.BlockSpec((L, SSD_CONV), lambda i: (rv(i), 2)),
                  pl.BlockSpec((HALO, SSD_CONV), lambda i: (_prev_halo(rv(i), L), 2)), pl.BlockSpec((L, LANE), lambda i: (rv(i), 18)),
                  pl.BlockSpec((L, SSD_DIM), lambda i: (rv(i), 5)), pl.BlockSpec((L, SSD_DIM), lambda i: (rv(i), 0)),
                  pl.BlockSpec((None, SSD_DIM, LANE), lambda i: (rv(i), 0, 0)),
                  pl.BlockSpec((4, SSD_CONV), lambda i: (0, 0)), vec(SSD_CONV), vec(LANE), vec(LANE), vec(SSD_DIM), vec(SSD_DIM)],
        out_specs=(pl.BlockSpec((L, SSD_DIM), lambda i: (rv(i), 0)), pl.BlockSpec((L, SSD_CONV), lambda i: (rv(i), 0)),
                   pl.BlockSpec((L, LANE), lambda i: (rv(i), 0)), pl.BlockSpec((4, SSD_CONV), lambda i: (0, 0)), vec(SSD_CONV),
                   vec(LANE), vec(LANE), pl.BlockSpec((8, LANE), lambda i: (0, 0)), vec(SSD_DIM)),
        scratch_shapes=[pltpu.VMEM((L + HALO, SSD_CONV), F32), pltpu.VMEM((L + HALO, SSD_CONV), F32), pltpu.VMEM((HALO, SSD_CONV), F32),
                        pltpu.VMEM((SSD_DIM, LANE), F32), pltpu.VMEM((8, SSD_DIM), F32)],
        compiler_params=_cp("arbitrary"),
    )(proj, proj, proj, proj, dcat, ypre, states, cw, cb, dtb, alog, dvec, nw)


def _adamw(parts, w, m, v, name):
    nl, r, c = w.shape
    tr = r
    for cand in (256, 128, 64, 32, 16, 8):
        if r % cand == 0 and (cand * c * 4) <= 2 * 1024 * 1024:
            tr = cand
            break
    c1 = 1.0 - B1 ** STEP
    c2 = 1.0 - B2 ** STEP

    def body(p_ref, w_ref, m_ref, v_ref, g_ref, d_ref, nm_ref, nv_ref):
        g = p_ref[0].astype(F32)
        for dev in range(1, NDEV):
            g = g + p_ref[dev].astype(F32)
        mn = B1 * m_ref[...] + (1.0 - B1) * g
        vn = B2 * v_ref[...] + (1.0 - B2) * (g * g)
        g_ref[...] = g
        nm_ref[...] = mn
        nv_ref[...] = vn
        d_ref[...] = -LR * ((mn / c1) / (jnp.sqrt(vn / c2) + AEPS) + WD * w_ref[...])

    blk = pl.BlockSpec((None, tr, c), lambda l, i: (l, i, 0))
    out = jax.ShapeDtypeStruct((nl, r, c), F32)
    return pl.pallas_call(
        body, name=name, grid=(nl, r // tr), out_shape=(out, out, out, out),
        in_specs=[pl.BlockSpec((NDEV, None, tr, c), lambda l, i: (0, l, i, 0)), blk, blk, blk], out_specs=(blk, blk, blk, blk),
        compiler_params=_cp("parallel", "parallel"),
    )(parts, w, m, v)


def _pad_win(w):
    z = lambda n: jnp.zeros((w.shape[0], n), w.dtype)
    return jnp.concatenate([w[:, :384], z(64), w[:, 384:416], z(32), w[:, 416:], z(124)], axis=1)


def _unpad_win(g):
    return jnp.concatenate([g[:, :384], g[:, 448:480], g[:, 512:2308]], axis=1)


def _pad_wout(w):
    att = jnp.pad(w[:512].reshape(HEADS, 64, D), ((0, 0), (64, 0), (0, 0))).reshape(HEADS * LANE, D)
    return jnp.concatenate([att, w[512:]], axis=0)


def _unpad_wout(g):
    att = g[:HEADS * LANE].reshape(HEADS, LANE, D)[:, 64:, :].reshape(512, D)
    return jnp.concatenate([att, g[HEADS * LANE:]], axis=0)


def _lanes(v, n=LANE):
    return jnp.pad(v, (0, n - v.shape[0])).reshape(1, n)


def _prep_ffn(big):
    return {"wout": _pad_wout(big["w_out"].reshape(1024, D)), "wup": big["ffn_w_up"], "fcw": big["ffn_conv_w"].astype(F32),
            "wdown": big["ffn_w_down"].reshape(4, FB, D)}


def _prep_layer(big, small, l):
    p = _prep_ffn(big) if "w_out" in big else {}
    p["win"] = _pad_win(big["w_in"].reshape(D, 2212))
    p["wq"] = jnp.pad(big["mla_w_q_up"], ((0, 0), (0, 0), (0, LANE - 96)))
    p["wkv"] = big["mla_w_kv_up"]
    p["scw"] = big["sc_conv_w"].astype(F32).transpose(1, 0, 2).reshape(3, SC)
    p["ssdcw"] = big["ssd_conv_w"].astype(F32).transpose(1, 0, 2).reshape(4, SSD_CONV)
    for nm in ("norm_mix_pre", "norm_mix_post", "norm_ffn_pre", "norm_ffn_post", "mla_q_norm", "mla_kv_norm", "ssd_conv_b", "ssd_norm"):
        p[nm] = small[nm][l].reshape(1, -1)
    p["dtb"] = _lanes(small["ssd_dt_bias"][l])
    p["alog"] = _lanes(small["ssd_a_log"][l])
    p["dvec"] = jnp.repeat(small["ssd_d"][l], 64).reshape(1, SSD_DIM)
    p["fcb"] = small["ffn_conv_b"][l].reshape(NDEV, 1, FB)
    return p


def _rope_tables(positions):
    inv_freq = 1.0 / (ROPE_THETA ** (jnp.arange(0, ROPE, 2, dtype=F32) / ROPE))
    ang = positions.astype(F32)[:, None] * inv_freq
    cos, sin = jnp.cos(ang), jnp.sin(ang)
    s = positions.shape[0]
    z = lambda n: jnp.zeros((s, n), F32)
    tc = jnp.concatenate([jnp.ones((s, 64), F32), cos, cos, z(32)], axis=1)
    ta = jnp.concatenate([z(64), -sin, z(48)], axis=1)
    tb = jnp.concatenate([z(80), sin, z(32)], axis=1)
    return tc, ta, tb


def _layer_fwd(xv, p, tabs, prefetch=None, prep_rest=None, h=None, next_norm=None):
    if h is None:
        h = _rms(xv, p["norm_mix_pre"], BF16, "rms_pre")
    proj = _mm_rows("in_proj", h, p["win"], BF16, NN)
    q, k, kv = _mla_prep(proj, tabs, p["mla_q_norm"], p["mla_kv_norm"], p["wq"], p["wkv"])
    o, lse, gathered = _flash_fwd(q, k, kv, prefetch)
    if prep_rest is not None:
        p = {**p, **prep_rest(gathered)}
    yconv = _sconv_fwd(proj, p["scw"])
    yssd, ypre, states = _ssd_fwd(proj, p["ssdcw"], p["ssd_conv_b"], p["dtb"], p["alog"], p["dvec"], p["ssd_norm"])
    cat = jnp.concatenate([o, yconv, yssd], axis=1)
    mixed = _mm_rows("out_proj", cat, p["wout"], BF16, NN)
    x1, h2 = _add_rms(xv, mixed, p["norm_mix_post"], "add_rms", p["norm_ffn_pre"])
    upre = _mm_up(h2, p["wup"])
    act = _ffn_act(upre, p["fcw"], p["fcb"])
    f = _mm_down(act, p["wdown"])
    x2 = _add_rms(x1, f, p["norm_ffn_post"], "add_rms", next_norm)
    saved = dict(x=xv, h=h, proj=proj, q=q, k=k, kv=kv, lse=lse, ypre=ypre, states=states, cat=cat, mixed=mixed, x1=x1, h2=h2,
                 upre=upre, act=act, f=f)
    return x2, saved, p, gathered


def _pack_grads(grads, group_ids):
    return [_group_pack(GROUPS[gi], lambda n: grads[n].reshape((NDEV,) + _rows2(n, True)), (NDEV,)) for gi in group_ids]


def _layer_bwd(dx2, sv, p, tabs, exchange=False, pending=None, head=None, below=None):
    df, g_nfpo = head if head is not None else _rms_bwd(sv["f"], p["norm_ffn_post"], dx2, None, BF16, "rms_bwd_post")
    dact = _mm_dact(df, p["wdown"])
    g_wdown = _mm_dwdown(sv["act"], df)
    dupre, g_fcb, g_fcw = _ffn_bwd(sv["upre"], dact, p["fcw"], p["fcb"])
    dh2 = _mm_dh2(dupre, p["wup"])
    g_wup = _mm_dwup(sv["h2"], dupre)
    dx1, dmixed, g_nfp, g_nmpo = _rms_bwd2(sv["x1"], p["norm_ffn_pre"], dh2, dx2, sv["mixed"], p["norm_mix_post"])
    dcat = _mm_rows("dcat", dmixed, p["wout"], BF16, NT)
    g_wout = _mm_wgrad("dw_out", sv["cat"], dmixed, BF16)
    big = {
        "w_out": _unpad_wout(g_wout).reshape(NDEV, 128, D),
        "ffn_w_up": g_wup,
        "ffn_conv_w": g_fcw.astype(BF16),
        "ffn_w_down": g_wdown.reshape(NDEV, 352, D),
    }
    outgoing = _pack_grads(big, FFN_SIDE) + (pending or []) if exchange else None
    dq, dk, dv, received = _flash_bwd(sv["q"], sv["k"], sv["kv"], sv["cat"], dcat, sv["lse"], outgoing)
    dcq, dckv, dkr, g_wq, g_wkv, g_qn, g_kvn = _mla_prep_bwd(sv["proj"], tabs, p["mla_q_norm"], p["mla_kv_norm"], p["wq"], p["wkv"], dq, dk, dv)
    dscb, dscc, dsch, g_scw = _sconv_bwd(sv["proj"], dcat, p["scw"])
    dz, dxbc, ddt, g_cw, g_cb, g_dtb, g_alog, g_d, g_nw = _ssd_bwd(
        sv["proj"], dcat, sv["ypre"], sv["states"], p["ssdcw"], p["ssd_conv_b"], p["dtb"], p["alog"], p["dvec"], p["ssd_norm"])
    dproj = jnp.concatenate([dcq, dckv, dkr, dscb, dscc, dsch, dz, dxbc, ddt], axis=1)
    dh = _mm_rows("dh", dproj, p["win"], BF16, NT)
    g_win = _mm_wgrad("dw_in", sv["h"], dproj, BF16)
    big.update({
        "w_in": _unpad_win(g_win).reshape(NDEV, 128, 2212),
        "mla_w_q_up": g_wq[:, :, :96].astype(BF16),
        "mla_w_kv_up": g_wkv.astype(BF16),
        "sc_conv_w": g_scw.reshape(3, NDEV, 32).transpose(1, 0, 2).astype(BF16),
        "ssd_conv_w": g_cw.reshape(4, NDEV, 96).transpose(1, 0, 2).astype(BF16),
    })
    head_below, last_received = None, None
    if below is not None:
        dx, df_below, g_nmp, g_below = _rms_bwd2(sv["x"], p["norm_mix_pre"], dh, dx1, *below)
        head_below = (df_below, g_below)
    elif exchange:
        dx, g_nmp, last_received = _rms_bwd(sv["x"], p["norm_mix_pre"], dh, dx1, F32, "rms_bwd_pre", _pack_grads(big, ATT_SIDE))
    else:
        dx, g_nmp = _rms_bwd(sv["x"], p["norm_mix_pre"], dh, dx1, F32, "rms_bwd_pre")
    small = {
        "norm_mix_pre": g_nmp[0], "norm_mix_post": g_nmpo[0], "norm_ffn_pre": g_nfp[0], "norm_ffn_post": g_nfpo[0],
        "mla_q_norm": g_qn[0], "mla_kv_norm": g_kvn[0], "ssd_conv_b": g_cb[0], "ssd_dt_bias": g_dtb[0, :SSD_H],
        "ssd_a_log": g_alog[0, :SSD_H], "ssd_d": g_d[0, :SSD_H], "ssd_norm": g_nw[0], "ffn_conv_b": g_fcb.reshape(-1),
    }
    return dx, big, small, received, head_below, last_received


def _local_step(xv, positions, target, layers):
    tabs = _rope_tables(positions)
    saved = []
    for p in layers:
        xv, sv, _, _ = _layer_fwd(xv, p, tabs)
        saved.append(sv)
    loss, dx = _loss_head(xv, target)
    bigs, smalls = [None] * DEPTH, [None] * DEPTH
    head = None
    for l in reversed(range(len(layers))):
        below = (saved[l - 1]["f"], layers[l - 1]["norm_ffn_post"]) if l > 0 else None
        dx, bigs[l], smalls[l], _, head, _ = _layer_bwd(dx, saved[l], layers[l], tabs, head=head, below=below)
    return loss[0, 0], dx, bigs, smalls


def _rows2(n, layer=False):
    shape = SHAPES[n][1:] if layer else SHAPES[n]
    return (math.prod(shape[:-1]), shape[-1])


def _group_pack(group, get, lead):
    width, names = group
    pieces = []
    for n in names:
        rows, cols = _rows2(n, True)
        pad = [(0, 0)] * len(lead) + [(0, -rows % 16), (0, width - cols)]
        pieces.append(jnp.pad(get(n), pad))
    return pieces[0] if len(pieces) == 1 else jnp.concatenate(pieces, axis=len(lead))


def _group_unpack(group, buf):
    _, names = group
    res, off = {}, 0
    for n in names:
        rows, cols = _rows2(n, True)
        res[n] = buf[:, off:off + rows, :cols]
        off += rows + (-rows % 16)
    return res


def kernel(x, positions, norm_mix_pre, norm_mix_post, norm_ffn_pre, norm_ffn_post, w_in, mla_q_norm, mla_w_q_up, mla_kv_norm, mla_w_kv_up, sc_conv_w, ssd_conv_w, ssd_conv_b, ssd_dt_bias, ssd_a_log, ssd_d, ssd_norm, w_out, ffn_w_up, ffn_conv_w, ffn_conv_b, ffn_w_down, loss_target, m_norm_mix_pre, m_norm_mix_post, m_norm_ffn_pre, m_norm_ffn_post, m_w_in, m_mla_q_norm, m_mla_w_q_up, m_mla_kv_norm, m_mla_w_kv_up, m_sc_conv_w, m_ssd_conv_w, m_ssd_conv_b, m_ssd_dt_bias, m_ssd_a_log, m_ssd_d, m_ssd_norm, m_w_out, m_ffn_w_up, m_ffn_conv_w, m_ffn_conv_b, m_ffn_w_down, v_norm_mix_pre, v_norm_mix_post, v_norm_ffn_pre, v_norm_ffn_post, v_w_in, v_mla_q_norm, v_mla_w_q_up, v_mla_kv_norm, v_mla_w_kv_up, v_sc_conv_w, v_ssd_conv_w, v_ssd_conv_b, v_ssd_dt_bias, v_ssd_a_log, v_ssd_d, v_ssd_norm, v_w_out, v_ffn_w_up, v_ffn_conv_w, v_ffn_conv_b, v_ffn_w_down):
    given = dict(locals())
    w = {n: given[n] for n in WEIGHTS}
    m = {n: given["m_" + n] for n in WEIGHTS}
    v = {n: given["v_" + n] for n in WEIGHTS}

    def shards(l, group_ids):
        return [_group_pack(GROUPS[gi], lambda n: w[n][l].astype(BF16).reshape(_rows2(n, True)), ()) for gi in group_ids]

    def unpacked(bufs, group_ids):
        big = {}
        for gi, buf in zip(group_ids, bufs):
            for n, piece in _group_unpack(GROUPS[gi], buf).items():
                big[n] = piece.reshape((NDEV,) + SHAPES[n][1:])
        return big

    small_w = {n: w[n] for n, _ in SMALL}
    tabs = _rope_tables(positions[0])
    xv, h, layers, saved = x[0], None, [], []
    att = _all_gather(shards(0, ATT_SIDE), "gather_weights")
    for l in range(DEPTH):
        prefetch = shards(l, FFN_SIDE) + (shards(l + 1, ATT_SIDE) if l + 1 < DEPTH else [])
        nxt = w["norm_mix_pre"][l + 1].reshape(1, D) if l + 1 < DEPTH else None
        xv, sv, p, gathered = _layer_fwd(xv, _prep_layer(unpacked(att, ATT_SIDE), small_w, l), tabs, prefetch,
                                         lambda got: _prep_ffn(unpacked(got[:len(FFN_SIDE)], FFN_SIDE)), h, nxt)
        xv, h = xv if nxt is not None else (xv, None)
        att = gathered[len(FFN_SIDE):]
        layers.append(p)
        saved.append(sv)
    loss, dx = _loss_head(xv, loss_target[0])
    loss = lax.psum(loss[0, 0], ("x", "y", "c"))

    smalls, pending, head = [None] * DEPTH, None, None
    recvs = [[None] * len(GROUPS) for _ in range(DEPTH)]
    for l in reversed(range(DEPTH)):
        below = (saved[l - 1]["f"], layers[l - 1]["norm_ffn_post"]) if l > 0 else None
        dx, grads, smalls[l], received, head, last = _layer_bwd(dx, saved[l], layers[l], tabs, True, pending, head, below)
        for pos, gi in enumerate(FFN_SIDE):
            recvs[l][gi] = received[pos]
        if pending is not None:
            for pos, gi in enumerate(ATT_SIDE):
                recvs[l + 1][gi] = received[len(FFN_SIDE) + pos]
        pending = _pack_grads(grads, ATT_SIDE) if l > 0 else None
    for gi, buf in zip(ATT_SIDE, last):
        recvs[0][gi] = buf
    out = {}
    for gi, g in enumerate(GROUPS):
        per_layer = [_group_unpack(g, recvs[l][gi]) for l in range(DEPTH)]
        for n in g[1]:
            parts = jnp.stack([per_layer[l][n] for l in range(DEPTH)], axis=1)
            out[n] = _adamw(parts, w[n], m[n], v[n], "adamw_" + n)

    total = sum(width for _, width in SMALL)
    padded = -(-total // (8 * LANE)) * 8 * LANE
    pk = lambda a: jnp.pad(a, ((0, 0), (0, padded - total))).reshape(1, DEPTH * padded // LANE, LANE)
    sflat = jnp.stack([jnp.concatenate([smalls[l][n] for n, _ in SMALL]) for l in range(DEPTH)])
    sparts = _all_gather([pk(sflat)[0]], "gather_small_grads")[0]
    pw = lambda d: pk(jnp.concatenate([d[n] for n, _ in SMALL], axis=1))
    res = _adamw(sparts[:, None], pw(w), pw(m), pw(v), "adamw_small")
    off = 0
    for n, width in SMALL:
        out[n] = [a.reshape(DEPTH, padded)[:, off:off + width] for a in res]
        off += width

    return (loss, dx[None], *[out[n][0] for n in WEIGHTS], *[out[n][1] for n in WEIGHTS],
            *[out[n][2] for n in WEIGHTS], *[out[n][3] for n in WEIGHTS])
```

```python
import functools
import math

import jax
import jax.numpy as jnp
from jax import lax
from jax.experimental import pallas as pl
from jax.experimental.pallas import tpu as pltpu

F32 = jnp.float32
BF16 = jnp.bfloat16

D = 1024
DEPTH = 4
NDEV = 8
HEADS = 8
QL = 256
KVL = 128
ROPE = 32
NOPE = 64
SC = 256
SSD_DIM = 256
SSD_CONV = 768
SSD_H = 4
SSD_L = 128
FFN = 2816
FB = 704
EPS = 1e-6
ROPE_THETA = 10000.0
ATT_SCALE = 96 ** -0.5
LOG2E = 1.4426950408889634
LR, B1, B2, AEPS, WD, STEP = 0.001, 0.9, 0.999, 1e-08, 0.01, 10

PW = 2432
CATW = 1536

ROW_TILE = 512
ROW_CHUNK = 16
NORM_CHUNK = 32
NORM_TILE = 512
MM_TILE = 1024
ATT_TILE = 512
ATT_QW = 1
BWD_TILE = 512
BWD_QW = 1
FWD_HEADS = 4
BWD_HEADS = 2
HALO = 16
LANE = 128
NEG = -1e30
NN = (((1,), (0,)), ((), ()))
NT = (((1,), (1,)), ((), ()))
TN = (((0,), (0,)), ((), ()))
VMEM_LIMIT = 56 * 1024 * 1024

SHARDED = (
    ("w_in", (4, 128, 2212)),
    ("mla_w_q_up", (4, 256, 96)),
    ("mla_w_kv_up", (4, 128, 128)),
    ("sc_conv_w", (4, 3, 32)),
    ("ssd_conv_w", (4, 4, 96)),
    ("w_out", (4, 128, 1024)),
    ("ffn_w_up", (4, 1024, 704)),
    ("ffn_conv_w", (4, 3, 704)),
    ("ffn_w_down", (4, 352, 1024)),
)
SHAPES = dict(SHARDED)
GROUPS = (
    (2212, ("w_in",)),
    (1024, ("w_out",)),
    (704, ("ffn_w_up",)),
    (96, ("mla_w_q_up", "ssd_conv_w", "sc_conv_w")),
    (128, ("mla_w_kv_up",)),
    (1024, ("ffn_w_down",)),
    (704, ("ffn_conv_w",)),
)
ATT_SIDE = (0, 3, 4)
FFN_SIDE = (1, 2, 5, 6)
SMALL = (
    ("norm_mix_pre", 1024), ("norm_mix_post", 1024), ("norm_ffn_pre", 1024), ("norm_ffn_post", 1024),
    ("mla_q_norm", 256), ("mla_kv_norm", 128), ("ssd_conv_b", 768), ("ssd_dt_bias", 4), ("ssd_a_log", 4),
    ("ssd_d", 4), ("ssd_norm", 256), ("ffn_conv_b", 5632),
)
WEIGHTS = ("norm_mix_pre", "norm_mix_post", "norm_ffn_pre", "norm_ffn_post", "w_in", "mla_q_norm", "mla_w_q_up",
           "mla_kv_norm", "mla_w_kv_up", "sc_conv_w", "ssd_conv_w", "ssd_conv_b", "ssd_dt_bias", "ssd_a_log", "ssd_d",
           "ssd_norm", "w_out", "ffn_w_up", "ffn_conv_w", "ffn_conv_b", "ffn_w_down")


def _dot(a, b, dims=NN, precision=None):
    return lax.dot_general(a, b, dims, precision=precision, preferred_element_type=F32)


def _dot01(x, c, dims=NN, x_first=True):
    cb = c.astype(BF16)
    hi = x.astype(BF16)
    rest = x - hi.astype(F32)
    mid = rest.astype(BF16)
    lo = (rest - mid.astype(F32)).astype(BF16)
    one = (lambda t: _dot(t, cb, dims)) if x_first else (lambda t: _dot(cb, t, dims))
    return one(hi) + one(mid) + one(lo)


def _sig(v):
    return 1.0 / (1.0 + jnp.exp(-v))


def _cp(*sem):
    return pltpu.CompilerParams(dimension_semantics=sem, vmem_limit_bytes=VMEM_LIMIT)


def _rowsum(v):
    return jnp.sum(v, axis=0, keepdims=True)


def _prev_halo(i, ts):
    return jnp.maximum(i * (ts // HALO) - 1, 0)


def _next_halo(i, ts, n):
    return jnp.minimum((i + 1) * (ts // HALO), n * (ts // HALO) - 1)


def _gather_plan(x_refs, out_refs, send_sems, recv_sems, local_sems):
    n = len(x_refs)
    x, y, cc = lax.axis_index("x"), lax.axis_index("y"), lax.axis_index("c")
    me, sibling = (x, y, cc), (x, y, 1 - cc)
    chips = [(1 - x, y), (x, 1 - y), (1 - x, 1 - y)]

    def rows(t, px, py, pc):
        return out_refs[t].at[4 * px + 2 * py + pc]

    def copy(t, k, block, to, own=False):
        return pltpu.make_async_remote_copy(
            src_ref=x_refs[t] if own else rows(t, *block), dst_ref=rows(t, *block),
            send_sem=send_sems.at[7 * t + k], recv_sem=recv_sems.at[7 * t + k], device_id=to, device_id_type=pl.DeviceIdType.MESH)

    def local(t):
        return pltpu.make_async_copy(x_refs[t], rows(t, *me), local_sems.at[t])

    def start():
        for t in range(n):
            local(t).start()
            copy(t, 0, me, sibling, own=True).start()
            for j, chip in enumerate(chips):
                copy(t, 1 + j, me, (*chip, cc), own=True).start()

    def finish():
        for j, chip in enumerate(chips):
            for t in range(n):
                copy(t, 1 + j, (*chip, cc), me).wait_recv()
                copy(t, 4 + j, (*chip, cc), sibling).start()
        for t in range(n):
            copy(t, 0, sibling, me).wait_recv()
            for j, chip in enumerate(chips):
                copy(t, 4 + j, (*chip, 1 - cc), me).wait_recv()
        for t in range(n):
            copy(t, 0, me, sibling, own=True).wait_send()
            for j, chip in enumerate(chips):
                copy(t, 1 + j, me, (*chip, cc), own=True).wait_send()
                copy(t, 4 + j, (*chip, cc), sibling).wait_send()
            local(t).wait()

    return start, finish


def _exchange_plan(x_refs, out_refs, send_sems, recv_sems, local_sems):
    n = len(x_refs)
    x, y, cc = lax.axis_index("x"), lax.axis_index("y"), lax.axis_index("c")
    me = 4 * x + 2 * y + cc

    def copies():
        res = [pltpu.make_async_copy(x_refs[t].at[me], out_refs[t].at[me], local_sems.at[t]) for t in range(n)]
        for k in range(1, NDEV):
            px = 1 - x if k & 4 else x
            py = 1 - y if k & 2 else y
            pc = 1 - cc if k & 1 else cc
            peer = 4 * px + 2 * py + pc
            for t in range(n):
                res.append(pltpu.make_async_remote_copy(
                    src_ref=x_refs[t].at[peer], dst_ref=out_refs[t].at[me], send_sem=send_sems.at[7 * t + k - 1],
                    recv_sem=recv_sems.at[7 * t + k - 1], device_id=(px, py, pc), device_id_type=pl.DeviceIdType.MESH))
        return res

    def start():
        for cp in copies():
            cp.start()

    def finish():
        for cp in copies():
            cp.wait()

    return start, finish


def _comm_scratch(n):
    return [pltpu.SemaphoreType.DMA((7 * n,)), pltpu.SemaphoreType.DMA((7 * n,)), pltpu.SemaphoreType.DMA((n,))]


ANY = pl.BlockSpec(memory_space=pl.ANY)


def _all_gather(xs, name):
    n = len(xs)

    def body(*refs):
        start, finish = _gather_plan(refs[:n], refs[n:2 * n], *refs[2 * n:])
        start()
        finish()

    return pl.pallas_call(
        body, name=name, out_shape=[jax.ShapeDtypeStruct((NDEV,) + a.shape, a.dtype) for a in xs],
        in_specs=[ANY] * n, out_specs=[ANY] * n, scratch_shapes=_comm_scratch(n),
    )(*xs)


def _mm(name, a, b, out_shape, grid, a_spec, b_spec, o_spec, dims, acc_shape):
    nk = grid[2]

    def single(a_ref, b_ref, o_ref):
        o_ref[...] = _dot(a_ref[...], b_ref[...], dims).astype(o_ref.dtype)

    if nk == 1:
        return pl.pallas_call(
            single, name=name, grid=grid, out_shape=out_shape, in_specs=[a_spec, b_spec], out_specs=o_spec,
            compiler_params=_cp("parallel", "parallel", "arbitrary"),
        )(a, b)

    def body(a_ref, b_ref, o_ref, acc_ref):
        k = pl.program_id(2)

        @pl.when(k == 0)
        def _():
            acc_ref[...] = jnp.zeros_like(acc_ref)

        acc_ref[...] += _dot(a_ref[...], b_ref[...], dims)

        @pl.when(k == nk - 1)
        def _():
            o_ref[...] = acc_ref[...].astype(o_ref.dtype)

    return pl.pallas_call(
        body, name=name, grid=grid, out_shape=out_shape, in_specs=[a_spec, b_spec], out_specs=o_spec,
        scratch_shapes=[pltpu.VMEM(acc_shape, F32)], compiler_params=_cp("parallel", "parallel", "arbitrary"),
    )(a, b)


def _mm_rows(name, a, w, out_dtype, dims):
    s, k = a.shape
    n = w.shape[1] if dims == NN else w.shape[0]
    tm = min(MM_TILE, s)
    return _mm(name, a, w, jax.ShapeDtypeStruct((s, n), out_dtype), (s // tm, 1, 1),
               pl.BlockSpec((tm, k), lambda i, j, kk: (i, 0)), pl.BlockSpec(w.shape, lambda i, j, kk: (0, 0)),
               pl.BlockSpec((tm, n), lambda i, j, kk: (i, 0)), dims, (tm, n))


def _mm_wgrad(name, a, g, out_dtype):
    s, m = a.shape
    n = g.shape[1]
    tk = min(MM_TILE, s)
    return _mm(name, a, g, jax.ShapeDtypeStruct((m, n), out_dtype), (1, 1, s // tk),
               pl.BlockSpec((tk, m), lambda i, j, kk: (kk, 0)), pl.BlockSpec((tk, n), lambda i, j, kk: (kk, 0)),
               pl.BlockSpec((m, n), lambda i, j, kk: (0, 0)), TN, (m, n))


def _mm_up(h2, wup):
    s = h2.shape[0]
    tm = min(MM_TILE, s)
    return _mm("ffn_up", h2, wup, jax.ShapeDtypeStruct((NDEV, s, FB), BF16), (NDEV, s // tm, 1),
               pl.BlockSpec((tm, D), lambda j, i, kk: (i, 0)), pl.BlockSpec((None, D, FB), lambda j, i, kk: (j, 0, 0)),
               pl.BlockSpec((None, tm, FB), lambda j, i, kk: (j, i, 0)), NN, (tm, FB))


def _mm_down(act, wdown):
    s = act.shape[1]
    tm = min(MM_TILE, s)
    return _mm("ffn_down", act, wdown, jax.ShapeDtypeStruct((s, D), BF16), (s // tm, 1, 4),
               pl.BlockSpec((None, tm, FB), lambda i, j, kk: (kk, i, 0)), pl.BlockSpec((None, FB, D), lambda i, j, kk: (kk, 0, 0)),
               pl.BlockSpec((tm, D), lambda i, j, kk: (i, 0)), NN, (tm, D))


def _mm_dact(df, wdown):
    s = df.shape[0]
    tm = min(MM_TILE, s)
    return _mm("ffn_dact", df, wdown, jax.ShapeDtypeStruct((4, s, FB), BF16), (4, s // tm, 1),
               pl.BlockSpec((tm, D), lambda j, i, kk: (i, 0)), pl.BlockSpec((None, FB, D), lambda j, i, kk: (j, 0, 0)),
               pl.BlockSpec((None, tm, FB), lambda j, i, kk: (j, i, 0)), NT, (tm, FB))


def _mm_dwdown(act, df):
    s = df.shape[0]
    tk = min(MM_TILE, s)
    return _mm("ffn_dwdown", act, df, jax.ShapeDtypeStruct((4, FB, D), BF16), (4, 1, s // tk),
               pl.BlockSpec((None, tk, FB), lambda j, i, kk: (j, kk, 0)), pl.BlockSpec((tk, D), lambda j, i, kk: (kk, 0)),
               pl.BlockSpec((None, FB, D), lambda j, i, kk: (j, 0, 0)), TN, (FB, D))


def _mm_dh2(dupre, wup):
    s = dupre.shape[1]
    tm = min(MM_TILE, s)
    return _mm("ffn_dh2", dupre, wup, jax.ShapeDtypeStruct((s, D), BF16), (s // tm, 1, NDEV),
               pl.BlockSpec((None, tm, FB), lambda i, j, kk: (kk, i, 0)), pl.BlockSpec((None, D, FB), lambda i, j, kk: (kk, 0, 0)),
               pl.BlockSpec((tm, D), lambda i, j, kk: (i, 0)), NT, (tm, D))


def _mm_dwup(h2, dupre):
    s = h2.shape[0]
    tk = min(MM_TILE, s)
    return _mm("ffn_dwup", h2, dupre, jax.ShapeDtypeStruct((NDEV, D, FB), BF16), (NDEV, 1, s // tk),
               pl.BlockSpec((tk, D), lambda j, i, kk: (kk, 0)), pl.BlockSpec((None, tk, FB), lambda j, i, kk: (j, kk, 0)),
               pl.BlockSpec((None, D, FB), lambda j, i, kk: (j, 0, 0)), TN, (D, FB))


def _rms(xv, w, out_dtype, name):
    s, d = xv.shape
    ts = min(ROW_TILE, s)

    def body(x_ref, w_ref, o_ref):
        for r0 in range(0, ts, NORM_CHUNK):
            rows = pl.ds(r0, NORM_CHUNK)
            xf = x_ref[rows, :].astype(F32)
            r = lax.rsqrt(jnp.mean(xf * xf, axis=-1, keepdims=True) + EPS)
            o_ref[rows, :] = (xf * r * w_ref[...]).astype(o_ref.dtype)

    return pl.pallas_call(
        body, name=name, grid=(s // ts,), out_shape=jax.ShapeDtypeStruct((s, d), out_dtype),
        in_specs=[pl.BlockSpec((ts, d), lambda i: (i, 0)), pl.BlockSpec((1, d), lambda i: (0, 0))],
        out_specs=pl.BlockSpec((ts, d), lambda i: (i, 0)), compiler_params=_cp("parallel"),
    )(xv, w)


def _add_rms(xv, mv, w, name, w_next=None):
    s, d = xv.shape
    ts = min(NORM_TILE, s)
    both = w_next is not None

    def body(*refs):
        x_ref, m_ref, w_ref = refs[:3]
        o_ref = refs[4] if both else refs[3]
        for r0 in range(0, ts, NORM_CHUNK):
            rows = pl.ds(r0, NORM_CHUNK)
            mf = m_ref[rows, :].astype(F32)
            r = lax.rsqrt(jnp.mean(mf * mf, axis=-1, keepdims=True) + EPS)
            y = x_ref[rows, :] + mf * r * w_ref[...]
            o_ref[rows, :] = y
            if both:
                r2 = lax.rsqrt(jnp.mean(y * y, axis=-1, keepdims=True) + EPS)
                refs[5][rows, :] = (y * r2 * refs[3][...]).astype(BF16)

    row = pl.BlockSpec((ts, d), lambda i: (i, 0))
    vec = pl.BlockSpec((1, d), lambda i: (0, 0))
    if both:
        return pl.pallas_call(
            body, name=name + "_rms", grid=(s // ts,),
            out_shape=(jax.ShapeDtypeStruct((s, d), F32), jax.ShapeDtypeStruct((s, d), BF16)),
            in_specs=[row, row, vec, vec], out_specs=(row, row), compiler_params=_cp("parallel"),
        )(xv, mv, w, w_next)
    return pl.pallas_call(
        body, name=name, grid=(s // ts,), out_shape=jax.ShapeDtypeStruct((s, d), F32),
        in_specs=[row, row, vec], out_specs=row, compiler_params=_cp("parallel"),
    )(xv, mv, w)


def _rms_bwd_math(xf, w, dy):
    r = lax.rsqrt(jnp.mean(xf * xf, axis=-1, keepdims=True) + EPS)
    xh = xf * r
    dxh = dy * w
    dx = r * (dxh - xh * jnp.mean(dxh * xh, axis=-1, keepdims=True))
    return dx, dy * xh


def _rms_bwd(xv, w, dy, dres, out_dtype, name, pending=None):
    s, d = xv.shape
    ts = min(NORM_TILE, s)
    with_res = dres is not None
    nin = 4 if with_res else 3
    nx = len(pending) if pending else 0

    def body(*refs):
        x_ref, w_ref, dy_ref = refs[:3]
        dres_ref = refs[3] if with_res else None
        dx_ref, dw_ref = refs[nin + nx:nin + nx + 2]
        acc = refs[nin + 2 * nx + 2]
        if nx:
            start, finish = _exchange_plan(refs[nin:nin + nx], refs[nin + nx + 2:nin + 2 * nx + 2], *refs[nin + 2 * nx + 3:])
            pl.when(pl.program_id(0) == 0)(start)
        acc[...] = jnp.zeros_like(acc)
        for r0 in range(0, ts, NORM_CHUNK):
            rows = pl.ds(r0, NORM_CHUNK)
            dx, dwt = _rms_bwd_math(x_ref[rows, :].astype(F32), w_ref[...], dy_ref[rows, :].astype(F32))
            if with_res:
                dx = dx + dres_ref[rows, :]
            dx_ref[rows, :] = dx.astype(dx_ref.dtype)
            acc[...] += dwt

        @pl.when(pl.program_id(0) == 0)
        def _():
            dw_ref[...] = jnp.zeros_like(dw_ref)

        dw_ref[...] += _rowsum(acc[...])
        if nx:
            pl.when(pl.program_id(0) == s // ts - 1)(finish)

    row = pl.BlockSpec((ts, d), lambda i: (i, 0))
    vec = pl.BlockSpec((1, d), lambda i: (0, 0))
    ins = [xv, w, dy] + ([dres] if with_res else []) + (pending or [])
    res = pl.pallas_call(
        body, name=name + "_exchange" if nx else name, grid=(s // ts,),
        out_shape=[jax.ShapeDtypeStruct((s, d), out_dtype), jax.ShapeDtypeStruct((1, d), F32)]
        + [jax.ShapeDtypeStruct(a.shape, a.dtype) for a in (pending or [])],
        in_specs=[row, vec, row] + ([row] if with_res else []) + [ANY] * nx, out_specs=[row, vec] + [ANY] * nx,
        scratch_shapes=[pltpu.VMEM((NORM_CHUNK, d), F32)] + (_comm_scratch(nx) if nx else []), compiler_params=_cp("arbitrary"),
    )(*ins)
    return (res[0], res[1], list(res[2:])) if nx else (res[0], res[1])


def _rms_bwd2(xa, wa, dya, dres, xb, wb):
    s, d = xa.shape
    ts = min(NORM_TILE, s)

    def body(xa_ref, wa_ref, dya_ref, dres_ref, xb_ref, wb_ref, da_ref, db_ref, dwa_ref, dwb_ref, acc):
        acc[...] = jnp.zeros_like(acc)
        for r0 in range(0, ts, NORM_CHUNK):
            rows = pl.ds(r0, NORM_CHUNK)
            da, dwt = _rms_bwd_math(xa_ref[rows, :].astype(F32), wa_ref[...], dya_ref[rows, :].astype(F32))
            da = da + dres_ref[rows, :]
            da_ref[rows, :] = da
            acc[0] += dwt
            db, dwt = _rms_bwd_math(xb_ref[rows, :].astype(F32), wb_ref[...], da)
            db_ref[rows, :] = db.astype(BF16)
            acc[1] += dwt

        @pl.when(pl.program_id(0) == 0)
        def _():
            dwa_ref[...] = jnp.zeros_like(dwa_ref)
            dwb_ref[...] = jnp.zeros_like(dwb_ref)

        dwa_ref[...] += _rowsum(acc[0])
        dwb_ref[...] += _rowsum(acc[1])

    row = pl.BlockSpec((ts, d), lambda i: (i, 0))
    vec = pl.BlockSpec((1, d), lambda i: (0, 0))
    return pl.pallas_call(
        body, name="rms_bwd2", grid=(s // ts,),
        out_shape=(jax.ShapeDtypeStruct((s, d), F32), jax.ShapeDtypeStruct((s, d), BF16), jax.ShapeDtypeStruct((1, d), F32),
                   jax.ShapeDtypeStruct((1, d), F32)),
        in_specs=[row, vec, row, row, row, vec], out_specs=(row, row, vec, vec),
        scratch_shapes=[pltpu.VMEM((2, NORM_CHUNK, d), F32)], compiler_params=_cp("arbitrary"),
    )(xa, wa, dya, dres, xb, wb)


def _loss_head(yv, tv):
    s, d = yv.shape
    ts = min(ROW_TILE, s)

    def body(y_ref, t_ref, l_ref, dy_ref):
        e = y_ref[...] - t_ref[...]
        dy_ref[...] = e * (1.0 / d)

        @pl.when(pl.program_id(0) == 0)
        def _():
            l_ref[...] = jnp.zeros_like(l_ref)

        tot = jnp.sum(jnp.sum(e * e, axis=1, keepdims=True), axis=0, keepdims=True)
        l_ref[...] += jnp.broadcast_to(tot * (0.5 / d), (8, LANE))

    row = pl.BlockSpec((ts, d), lambda i: (i, 0))
    return pl.pallas_call(
        body, name="loss_head", grid=(s // ts,),
        out_shape=(jax.ShapeDtypeStruct((8, LANE), F32), jax.ShapeDtypeStruct((s, d), F32)),
        in_specs=[row, row], out_specs=(pl.BlockSpec((8, LANE), lambda i: (0, 0)), row), compiler_params=_cp("arbitrary"),
    )(yv, tv)


def _rope(v, c, a, b):
    return v * c + pltpu.roll(v, LANE - 16, 1) * a + pltpu.roll(v, 16, 1) * b


def _rope_t(dv, c, a, b):
    return dv * c + pltpu.roll(dv * a, 16, 1) + pltpu.roll(dv * b, LANE - 16, 1)


def _mla_prep(proj, tabs, qnw, kvnw, wq, wkv):
    s = proj.shape[0]
    ts = min(ROW_TILE, s)
    tc, ta, tb = tabs

    def body(cq_ref, ckv_ref, kr_ref, c_ref, a_ref, b_ref, qnw_ref, kvnw_ref, wq_ref, wkv_ref, q_ref, k_ref, kv_ref):
        c, a, b = c_ref[...], a_ref[...], b_ref[...]
        cq = cq_ref[...].astype(F32)
        qn = (cq * lax.rsqrt(jnp.mean(cq * cq, axis=-1, keepdims=True) + EPS) * qnw_ref[...]).astype(BF16)
        ckv = ckv_ref[...].astype(F32)
        kvn = (ckv * lax.rsqrt(jnp.mean(ckv * ckv, axis=-1, keepdims=True) + EPS) * kvnw_ref[...]).astype(BF16)
        kr = _rope(kr_ref[...].astype(F32), c, a, b)
        lane = lax.broadcasted_iota(jnp.int32, (ts, LANE), 1)
        for h in range(HEADS):
            q_ref[h] = _rope(_dot(qn, wq_ref[h]), c, a, b).astype(BF16)
            kv = _dot(kvn, wkv_ref[h])
            kv_ref[h] = kv.astype(BF16)
            k_ref[h] = jnp.where(lane < NOPE, kv, kr).astype(BF16)

    tab = pl.BlockSpec((ts, LANE), lambda i: (i, 0))
    hd = pl.BlockSpec((HEADS, ts, LANE), lambda i: (0, i, 0))
    out = jax.ShapeDtypeStruct((HEADS, s, LANE), BF16)
    return pl.pallas_call(
        body, name="mla_prep", grid=(s // ts,), out_shape=(out, out, out),
        in_specs=[pl.BlockSpec((ts, QL), lambda i: (i, 0)), pl.BlockSpec((ts, LANE), lambda i: (i, 2)),
                  pl.BlockSpec((ts, LANE), lambda i: (i, 3)), tab, tab, tab,
                  pl.BlockSpec((1, QL), lambda i: (0, 0)), pl.BlockSpec((1, KVL), lambda i: (0, 0)),
                  pl.BlockSpec((HEADS, QL, LANE), lambda i: (0, 0, 0)), pl.BlockSpec((HEADS, KVL, LANE), lambda i: (0, 0, 0))],
        out_specs=(hd, hd, hd), compiler_params=_cp("parallel"),
    )(proj, proj, proj, tc, ta, tb, qnw, kvnw, wq, wkv)


def _mla_prep_bwd(proj, tabs, qnw, kvnw, wq, wkv, dq, dk, dv):
    s = proj.shape[0]
    ts = min(ROW_TILE, s)
    tc, ta, tb = tabs

    def body(cq_ref, ckv_ref, c_ref, a_ref, b_ref, qnw_ref, kvnw_ref, wq_ref, wkv_ref, dq_ref, dk_ref, dv_ref,
             dcq_ref, dckv_ref, dkr_ref, dwq_ref, dwkv_ref, dqnw_ref, dkvnw_ref):
        @pl.when(pl.program_id(0) == 0)
        def _():
            dwq_ref[...] = jnp.zeros_like(dwq_ref)
            dwkv_ref[...] = jnp.zeros_like(dwkv_ref)
            dqnw_ref[...] = jnp.zeros_like(dqnw_ref)
            dkvnw_ref[...] = jnp.zeros_like(dkvnw_ref)

        c, a, b = c_ref[...], a_ref[...], b_ref[...]
        cq = cq_ref[...].astype(F32)
        qn = (cq * lax.rsqrt(jnp.mean(cq * cq, axis=-1, keepdims=True) + EPS) * qnw_ref[...]).astype(BF16)
        ckv = ckv_ref[...].astype(F32)
        kvn = (ckv * lax.rsqrt(jnp.mean(ckv * ckv, axis=-1, keepdims=True) + EPS) * kvnw_ref[...]).astype(BF16)
        lane = lax.broadcasted_iota(jnp.int32, (ts, LANE), 1)
        dqn = jnp.zeros((ts, QL), F32)
        dkvn = jnp.zeros((ts, KVL), F32)
        dkr = jnp.zeros((ts, LANE), F32)
        for h in range(HEADS):
            dqh = _rope_t(dq_ref[h], c, a, b).astype(BF16)
            dwq_ref[h] += _dot(qn, dqh, TN)
            dqn += _dot(dqh, wq_ref[h], NT)
            dkh = dk_ref[h].astype(F32)
            dkvh = jnp.where(lane < NOPE, dkh, dv_ref[h].astype(F32)).astype(BF16)
            dkr += jnp.where(lane < NOPE, 0.0, dkh)
            dwkv_ref[h] += _dot(kvn, dkvh, TN)
            dkvn += _dot(dkvh, wkv_ref[h], NT)
        dkr_ref[...] = _rope_t(dkr, c, a, b).astype(BF16)
        dcq, dwt = _rms_bwd_math(cq, qnw_ref[...], dqn)
        dcq_ref[...] = dcq.astype(BF16)
        dqnw_ref[...] += _rowsum(dwt)
        dckv, dwt = _rms_bwd_math(ckv, kvnw_ref[...], dkvn)
        dckv_ref[...] = dckv.astype(BF16)
        dkvnw_ref[...] += _rowsum(dwt)

    tab = pl.BlockSpec((ts, LANE), lambda i: (i, 0))
    hd = pl.BlockSpec((HEADS, ts, LANE), lambda i: (0, i, 0))
    wq_spec = pl.BlockSpec((HEADS, QL, LANE), lambda i: (0, 0, 0))
    wkv_spec = pl.BlockSpec((HEADS, KVL, LANE), lambda i: (0, 0, 0))
    return pl.pallas_call(
        body, name="mla_prep_bwd", grid=(s // ts,),
        out_shape=(jax.ShapeDtypeStruct((s, QL), BF16), jax.ShapeDtypeStruct((s, KVL), BF16), jax.ShapeDtypeStruct((s, LANE), BF16),
                   jax.ShapeDtypeStruct((HEADS, QL, LANE), F32), jax.ShapeDtypeStruct((HEADS, KVL, LANE), F32),
                   jax.ShapeDtypeStruct((1, QL), F32), jax.ShapeDtypeStruct((1, KVL), F32)),
        in_specs=[pl.BlockSpec((ts, QL), lambda i: (i, 0)), pl.BlockSpec((ts, LANE), lambda i: (i, 2)), tab, tab, tab,
                  pl.BlockSpec((1, QL), lambda i: (0, 0)), pl.BlockSpec((1, KVL), lambda i: (0, 0)), wq_spec, wkv_spec, hd, hd, hd],
        out_specs=(pl.BlockSpec((ts, QL), lambda i: (i, 0)), pl.BlockSpec((ts, KVL), lambda i: (i, 0)), tab, wq_spec, wkv_spec,
                   pl.BlockSpec((1, QL), lambda i: (0, 0)), pl.BlockSpec((1, KVL), lambda i: (0, 0))),
        compiler_params=_cp("arbitrary"),
    )(proj, proj, tc, ta, tb, qnw, kvnw, wq, wkv, dq, dk, dv)


def _transpose_bf16(v):
    return v.astype(F32).T.astype(BF16)


def _flash_fwd(q, k, kv, prefetch=None):
    s = q.shape[1]
    t = min(ATT_TILE, s)
    tq = ATT_QW * t
    n = s // tq
    g = FWD_HEADS
    nx = len(prefetch) if prefetch else 0

    def body(*refs):
        q_ref, k_ref, kv_ref = refs[:3]
        o_ref, lse_ref = refs[3 + nx:5 + nx]
        kvt_sc = refs[5 + 2 * nx]
        step = pl.program_id(0) * n + pl.program_id(1)
        if nx:
            start, finish = _gather_plan(refs[3:3 + nx], refs[5 + nx:5 + 2 * nx], *refs[6 + 2 * nx:])
            pl.when(step == 0)(start)
        attend(q_ref, k_ref, kv_ref, o_ref, lse_ref, kvt_sc)
        if nx:
            pl.when(step == (HEADS // g) * n - 1)(finish)

    def attend(q_ref, k_ref, kv_ref, o_ref, lse_ref, kvt_sc):
        i = pl.program_id(1)

        @pl.when(i == 0)
        def _():
            ones_rows = lax.broadcasted_iota(jnp.int32, (LANE, s), 0) < NOPE
            for hh in range(g):
                kvt_sc[hh] = jnp.where(ones_rows, 1.0, kv_ref[hh].astype(F32).T).astype(BF16)

        qt = [(q_ref[hh].astype(F32) * (ATT_SCALE * LOG2E)).T.astype(BF16) for hh in range(g)]
        kpos = lax.broadcasted_iota(jnp.int32, (t, tq), 0)
        qpos = lax.broadcasted_iota(jnp.int32, (t, tq), 1) + i * tq

        def chunk(j, carry, diagonal):
            start = pl.multiple_of(j * t, t)
            scs = [_dot(k_ref[hh, pl.ds(start, t), :], qt[hh]) for hh in range(g)]
            soft = []
            for hh in range(g):
                sc = scs[hh]
                if diagonal:
                    sc = jnp.where(qpos >= kpos + start, sc, NEG)
                m_new = jnp.maximum(carry[hh][0], jnp.max(sc, axis=0, keepdims=True))
                soft.append((m_new, jnp.exp2(carry[hh][0] - m_new), jnp.exp2(sc - m_new).astype(BF16)))
            pvs = [_dot(kvt_sc[hh, :, pl.ds(start, t)], soft[hh][2]) for hh in range(g)]
            return tuple((soft[hh][0], soft[hh][1] * carry[hh][1] + pvs[hh]) for hh in range(g))

        init = tuple((jnp.full((1, tq), NEG, F32), jnp.zeros((LANE, tq), F32)) for _ in range(g))
        carry = lax.fori_loop(0, ATT_QW * i, lambda j, c: chunk(j, c, False), init)
        for d in range(ATT_QW):
            carry = chunk(ATT_QW * i + d, carry, True)
        for hh in range(g):
            m, acc = carry[hh]
            l = acc[0:1, :]
            o_ref[:, hh * LANE:(hh + 1) * LANE] = (acc / l).T.astype(BF16)
            lse_ref[hh] = m + jnp.log2(l)

    whole = pl.BlockSpec((g, s, LANE), lambda h, i: (h, 0, 0))
    res = pl.pallas_call(
        body, name="flash_fwd_gather" if nx else "flash_fwd", grid=(HEADS // g, n),
        out_shape=[jax.ShapeDtypeStruct((s, HEADS * LANE), BF16), jax.ShapeDtypeStruct((HEADS, 1, s), F32)]
        + [jax.ShapeDtypeStruct((NDEV,) + a.shape, a.dtype) for a in (prefetch or [])],
        in_specs=[pl.BlockSpec((g, tq, LANE), lambda h, i: (h, i, 0)), whole, whole] + [ANY] * nx,
        out_specs=[pl.BlockSpec((tq, g * LANE), lambda h, i: (i, h)), pl.BlockSpec((g, 1, tq), lambda h, i: (h, 0, i))] + [ANY] * nx,
        scratch_shapes=[pltpu.VMEM((g, LANE, s), BF16)] + (_comm_scratch(nx) if nx else []),
        compiler_params=_cp("arbitrary", "arbitrary"),
    )(q, k, kv, *(prefetch or []))
    return res[0], res[1], list(res[2:])


def _flash_bwd(q, k, kv, cat, dcat, lse, pending=None):
    s = q.shape[1]
    t = min(BWD_TILE, s)
    tq = BWD_QW * t
    n = s // t
    g = BWD_HEADS
    nx = len(pending) if pending else 0

    def body(*refs):
        ins, outs, scr = refs[:6], refs[6 + nx:9 + nx], refs[9 + 2 * nx:14 + 2 * nx]
        step = pl.program_id(0) * n + pl.program_id(1)
        if nx:
            start, finish = _exchange_plan(refs[6:6 + nx], refs[9 + nx:9 + 2 * nx], *refs[14 + 2 * nx:])
            pl.when(step == 0)(start)
        attend(*ins, *outs, *scr)
        if nx:
            pl.when(step == (HEADS // g) * n - 1)(finish)

    def attend(q_ref, k_ref, kv_ref, o_ref, do_ref, lse_ref, dq_ref, dk_ref, dv_ref, qt_sc, dot_sc, delta_sc, dqt_sc, qs_sc):
        j = pl.program_id(1)

        @pl.when(j == 0)
        def _():
            for hh in range(g):
                lanes = slice(hh * LANE, (hh + 1) * LANE)
                qf = q_ref[hh].astype(F32)
                qt_sc[hh] = (qf * (ATT_SCALE * LOG2E)).T.astype(BF16)
                qs_sc[hh] = (qf * ATT_SCALE).astype(BF16)
                dof = do_ref[:, lanes].astype(F32)
                dot_sc[hh] = dof.T.astype(BF16)
                delta_sc[hh] = _dot01(dof * o_ref[:, lanes].astype(F32), jnp.ones((8, LANE), F32), NT, x_first=False)
            dqt_sc[...] = jnp.zeros_like(dqt_sc)

        kjt = [_transpose_bf16(k_ref[hh]) for hh in range(g)]
        kpos = lax.broadcasted_iota(jnp.int32, (t, tq), 0) + j * t
        qpos = lax.broadcasted_iota(jnp.int32, (t, tq), 1)

        def chunk(i, carry, diagonal):
            start = pl.multiple_of(i * tq, tq)
            cols = pl.ds(start, tq)
            scs = [_dot(k_ref[hh], qt_sc[hh, :, cols]) for hh in range(g)]
            dps = [_dot(kv_ref[hh], dot_sc[hh, :, cols]) for hh in range(g)]
            pds = []
            for hh in range(g):
                p = jnp.exp2(scs[hh] - lse_ref[hh, :, cols])
                if diagonal:
                    p = jnp.where(qpos + start >= kpos, p, 0.0)
                ds = (p * (dps[hh] - delta_sc[hh, 0:1, cols])).astype(BF16)
                pds.append((p.astype(BF16), ds))
            out = []
            for hh in range(g):
                dk, dv = carry[hh]
                dv = dv + _dot(pds[hh][0], do_ref[pl.ds(start, tq), hh * LANE:(hh + 1) * LANE])
                dk = dk + _dot(pds[hh][1], qs_sc[hh, pl.ds(start, tq), :])
                dqt_sc[hh, :, cols] += _dot(kjt[hh], pds[hh][1])
                out.append((dk, dv))
            return tuple(out)

        zero = jnp.zeros((t, LANE), F32)
        first = lax.div(j, BWD_QW)
        carry = chunk(first, tuple((zero, zero) for _ in range(g)), True)
        carry = lax.fori_loop(first + 1, s // tq, lambda i, c: chunk(i, c, False), carry)
        for hh in range(g):
            dk_ref[hh] = carry[hh][0].astype(BF16)
            dv_ref[hh] = carry[hh][1].astype(BF16)

        @pl.when(j == n - 1)
        def _():
            for hh in range(g):
                dq_ref[hh] = (dqt_sc[hh] * ATT_SCALE).T

    whole = pl.BlockSpec((g, s, LANE), lambda h, j: (h, 0, 0))
    kspec = pl.BlockSpec((g, t, LANE), lambda h, j: (h, j, 0))
    ospec = pl.BlockSpec((s, g * LANE), lambda h, j: (0, h))
    res = pl.pallas_call(
        body, name="flash_bwd_exchange" if nx else "flash_bwd", grid=(HEADS // g, n),
        out_shape=[jax.ShapeDtypeStruct((HEADS, s, LANE), F32), jax.ShapeDtypeStruct((HEADS, s, LANE), BF16),
                   jax.ShapeDtypeStruct((HEADS, s, LANE), BF16)] + [jax.ShapeDtypeStruct(a.shape, a.dtype) for a in (pending or [])],
        in_specs=[whole, kspec, kspec, ospec, ospec, pl.BlockSpec((g, 1, s), lambda h, j: (h, 0, 0))] + [ANY] * nx,
        out_specs=[whole, kspec, kspec] + [ANY] * nx,
        scratch_shapes=[pltpu.VMEM((g, LANE, s), BF16), pltpu.VMEM((g, LANE, s), BF16), pltpu.VMEM((g, 8, s), F32),
                        pltpu.VMEM((g, LANE, s), F32), pltpu.VMEM((g, s, LANE), BF16)] + (_comm_scratch(nx) if nx else []),
        compiler_params=_cp("arbitrary", "arbitrary"),
    )(q, k, kv, cat, dcat, lse, *(pending or []))
    return res[0], res[1], res[2], list(res[3:])


def _conv3(ext, w_ref, ts):
    return (w_ref[0:1, :] * ext[pl.ds(HALO - 2, ts), :] + w_ref[1:2, :] * ext[pl.ds(HALO - 1, ts), :]
            + w_ref[2:3, :] * ext[pl.ds(HALO, ts), :])


def _conv3_rows(ext, w_ref, r):
    return (w_ref[0:1, :] * ext[pl.ds(HALO - 2 + r, ROW_CHUNK), :] + w_ref[1:2, :] * ext[pl.ds(HALO - 1 + r, ROW_CHUNK), :]
            + w_ref[2:3, :] * ext[pl.ds(HALO + r, ROW_CHUNK), :])


def _conv3_t(ext2, w_ref, ts):
    return (w_ref[0:1, :] * ext2[pl.ds(2, ts), :] + w_ref[1:2, :] * ext2[pl.ds(1, ts), :] + w_ref[2:3, :] * ext2[pl.ds(0, ts), :])


def _sconv_fwd(proj, w):
    s = proj.shape[0]
    ts = min(ROW_TILE, s)

    def body(b_ref, c_ref, h_ref, hc_ref, hh_ref, w_ref, o_ref, ext):
        i = pl.program_id(0)
        ext[0:HALO, :] = hc_ref[...].astype(F32) * hh_ref[...].astype(F32) * (i > 0).astype(F32)
        ext[HALO:HALO + ts, :] = c_ref[...].astype(F32) * h_ref[...].astype(F32)
        o_ref[...] = (b_ref[...].astype(F32) * _conv3(ext, w_ref, ts)).astype(BF16)

    def col(cb):
        return pl.BlockSpec((ts, SC), lambda i: (i, cb))

    def halo(cb):
        return pl.BlockSpec((HALO, SC), lambda i: (_prev_halo(i, ts), cb))

    return pl.pallas_call(
        body, name="sconv_fwd", grid=(s // ts,), out_shape=jax.ShapeDtypeStruct((s, SC), BF16),
        in_specs=[col(2), col(3), col(4), halo(3), halo(4), pl.BlockSpec((3, SC), lambda i: (0, 0))],
        out_specs=pl.BlockSpec((ts, SC), lambda i: (i, 0)), scratch_shapes=[pltpu.VMEM((ts + HALO, SC), F32)],
        compiler_params=_cp("parallel"),
    )(proj, proj, proj, proj, proj, w)


def _sconv_bwd(proj, dcat, w):
    s = proj.shape[0]
    ts = min(ROW_TILE, s)
    n = s // ts

    def body(b_ref, c_ref, h_ref, hc_ref, hh_ref, dy_ref, ndy_ref, nb_ref, w_ref, db_ref, dc_ref, dh_ref, dw_ref, ext, ext2):
        i = pl.program_id(0)

        @pl.when(i == 0)
        def _():
            dw_ref[...] = jnp.zeros_like(dw_ref)

        cv, hv, bv = c_ref[...].astype(F32), h_ref[...].astype(F32), b_ref[...].astype(F32)
        ext[0:HALO, :] = hc_ref[...].astype(F32) * hh_ref[...].astype(F32) * (i > 0).astype(F32)
        ext[HALO:HALO + ts, :] = cv * hv
        dy = dy_ref[...].astype(F32)
        db_ref[...] = (dy * _conv3(ext, w_ref, ts)).astype(BF16)
        dyb = dy * bv
        ext2[0:ts, :] = dyb
        ext2[ts:ts + HALO, :] = ndy_ref[...].astype(F32) * nb_ref[...].astype(F32) * (i < n - 1).astype(F32)
        dg = _conv3_t(ext2, w_ref, ts)
        dc_ref[...] = (dg * hv).astype(BF16)
        dh_ref[...] = (dg * cv).astype(BF16)
        for kk in range(3):
            dw_ref[kk:kk + 1, :] += _rowsum(dyb * ext[pl.ds(HALO - 2 + kk, ts), :])

    def col(cb):
        return pl.BlockSpec((ts, SC), lambda i: (i, cb))

    def halo(cb):
        return pl.BlockSpec((HALO, SC), lambda i: (_prev_halo(i, ts), cb))

    def nxt(cb):
        return pl.BlockSpec((HALO, SC), lambda i: (_next_halo(i, ts, n), cb))

    out = jax.ShapeDtypeStruct((s, SC), BF16)
    o0 = pl.BlockSpec((ts, SC), lambda i: (i, 0))
    return pl.pallas_call(
        body, name="sconv_bwd", grid=(n,), out_shape=(out, out, out, jax.ShapeDtypeStruct((3, SC), F32)),
        in_specs=[col(2), col(3), col(4), halo(3), halo(4), col(4), nxt(4), nxt(2), pl.BlockSpec((3, SC), lambda i: (0, 0))],
        out_specs=(o0, o0, o0, pl.BlockSpec((3, SC), lambda i: (0, 0))),
        scratch_shapes=[pltpu.VMEM((ts + HALO, SC), F32), pltpu.VMEM((ts + HALO, SC), F32)], compiler_params=_cp("arbitrary"),
    )(proj, proj, proj, proj, proj, dcat, dcat, proj, w)


def _ffn_stage(ext, u_ref, halo_ref, i, ts):
    ext[0:HALO, :] = halo_ref[...].astype(F32) * (i > 0).astype(F32)
    ext[HALO:HALO + ts, :] = u_ref[...].astype(F32)


def _ffn_specs(ts):
    cur = pl.BlockSpec((2, None, ts, FB), lambda j, i: (0, j, i, 0))
    halo = pl.BlockSpec((2, None, HALO, FB), lambda j, i: (0, j, _prev_halo(i, ts), 0))
    w = pl.BlockSpec((2, None, 3, FB), lambda j, i: (0, j, 0, 0))
    b = pl.BlockSpec((2, None, 1, FB), lambda j, i: (0, j, 0, 0))
    return cur, halo, w, b


def _ffn_act(upre, fcw, fcb):
    s = upre.shape[1]
    ts = min(ROW_TILE, s)

    def body(u_ref, halo_ref, w_ref, b_ref, o_ref, ext_g, ext_u):
        i = pl.program_id(1)
        _ffn_stage(ext_g, u_ref.at[0], halo_ref.at[0], i, ts)
        _ffn_stage(ext_u, u_ref.at[1], halo_ref.at[1], i, ts)
        for r in range(0, ts, ROW_CHUNK):
            gate = b_ref[0] + _conv3_rows(ext_g, w_ref.at[0], r)
            up = b_ref[1] + _conv3_rows(ext_u, w_ref.at[1], r)
            o_ref[pl.ds(r, ROW_CHUNK), :] = (gate * _sig(gate) * up).astype(BF16)

    cur, halo, w, b = _ffn_specs(ts)
    u4 = upre.reshape(2, 4, s, FB)
    return pl.pallas_call(
        body, name="ffn_act", grid=(4, s // ts), out_shape=jax.ShapeDtypeStruct((4, s, FB), BF16),
        in_specs=[cur, halo, w, b], out_specs=pl.BlockSpec((None, ts, FB), lambda j, i: (j, i, 0)),
        scratch_shapes=[pltpu.VMEM((ts + HALO, FB), F32), pltpu.VMEM((ts + HALO, FB), F32)], compiler_params=_cp("parallel", "parallel"),
    )(u4, u4, fcw.reshape(2, 4, 3, FB), fcb.reshape(2, 4, 1, FB))


def _ffn_bwd(upre, dact, fcw, fcb):
    s = upre.shape[1]
    ts = min(ROW_TILE, s)
    n = s // ts
    te = ts + HALO

    def body(u_ref, halo_ref, nxt_ref, w_ref, b_ref, da_ref, nda_ref, dup_ref, db_ref, dw_ref, ext_g, ext_u, ext_da, du_g, du_u, acc):
        i = pl.program_id(1)

        @pl.when(i == 0)
        def _():
            db_ref[...] = jnp.zeros_like(db_ref)
            dw_ref[...] = jnp.zeros_like(dw_ref)

        more = (i < n - 1).astype(F32)
        for idx, ext in ((0, ext_g), (1, ext_u)):
            _ffn_stage(ext, u_ref.at[idx], halo_ref.at[idx], i, ts)
            ext[HALO + ts:HALO + te, :] = nxt_ref[idx].astype(F32) * more
        ext_da[0:ts, :] = da_ref[...].astype(F32)
        ext_da[ts:te, :] = nda_ref[...].astype(F32) * more
        acc[...] = jnp.zeros_like(acc)
        for r in range(0, te, ROW_CHUNK):
            rows = pl.ds(r, ROW_CHUNK)
            gate = b_ref[0] + _conv3_rows(ext_g, w_ref.at[0], r)
            up = b_ref[1] + _conv3_rows(ext_u, w_ref.at[1], r)
            sg = _sig(gate)
            da = ext_da[rows, :]
            dgate = da * up * sg * (1.0 + gate * (1.0 - sg))
            dup = da * gate * sg
            du_g[rows, :] = dgate
            du_u[rows, :] = dup
            if r < ts:
                acc[0] += dgate
                acc[1] += dup
        for r in range(0, ts, ROW_CHUNK):
            for idx, du, ext in ((0, du_g, ext_g), (1, du_u, ext_u)):
                d0 = du[pl.ds(r, ROW_CHUNK), :]
                dupre = (w_ref[idx, 2:3, :] * d0 + w_ref[idx, 1:2, :] * du[pl.ds(r + 1, ROW_CHUNK), :]
                         + w_ref[idx, 0:1, :] * du[pl.ds(r + 2, ROW_CHUNK), :])
                dup_ref[idx, pl.ds(r, ROW_CHUNK), :] = dupre.astype(BF16)
                for kk in range(3):
                    acc[2 + 3 * idx + kk] += d0 * ext[pl.ds(HALO - 2 + kk + r, ROW_CHUNK), :]
        for idx in range(2):
            db_ref[idx] += _rowsum(acc[idx])
            for kk in range(3):
                dw_ref[idx, kk:kk + 1, :] += _rowsum(acc[2 + 3 * idx + kk])

    cur, halo, w, b = _ffn_specs(ts)
    nxt = pl.BlockSpec((2, None, HALO, FB), lambda j, i: (0, j, _next_halo(i, ts, n), 0))
    u4 = upre.reshape(2, 4, s, FB)
    dupre, db, dw = pl.pallas_call(
        body, name="ffn_bwd", grid=(4, n),
        out_shape=(jax.ShapeDtypeStruct((2, 4, s, FB), BF16), jax.ShapeDtypeStruct((2, 4, 1, FB), F32),
                   jax.ShapeDtypeStruct((2, 4, 3, FB), F32)),
        in_specs=[cur, halo, nxt, w, b, pl.BlockSpec((None, ts, FB), lambda j, i: (j, i, 0)),
                  pl.BlockSpec((None, HALO, FB), lambda j, i: (j, _next_halo(i, ts, n), 0))],
        out_specs=(cur, b, w),
        scratch_shapes=[pltpu.VMEM((te + HALO, FB), F32), pltpu.VMEM((te + HALO, FB), F32), pltpu.VMEM((te, FB), F32),
                        pltpu.VMEM((te, FB), F32), pltpu.VMEM((te, FB), F32), pltpu.VMEM((8, ROW_CHUNK, FB), F32)],
        compiler_params=_cp("parallel", "arbitrary"),
    )(u4, u4, u4, fcw.reshape(2, 4, 3, FB), fcb.reshape(2, 4, 1, FB), dact, dact)
    return dupre.reshape(NDEV, s, FB), db.reshape(NDEV, 1, FB), dw.reshape(NDEV, 3, FB)


def _softplus(v):
    e = jnp.exp(-jnp.abs(v))
    return jnp.maximum(v, 0.0) + jnp.where(e < 1e-4, e * (1.0 - 0.5 * e), jnp.log(1.0 + e))


def _ssd_consts():
    L = SSD_L
    r = lax.broadcasted_iota(jnp.int32, (L, L), 0)
    c = lax.broadcasted_iota(jnp.int32, (L, L), 1)
    tri = r >= c
    er = lax.broadcasted_iota(jnp.int32, (LANE, SSD_DIM), 0)
    ec = lax.broadcasted_iota(jnp.int32, (LANE, SSD_DIM), 1)
    expand = ((ec >= er * 64) & (ec < er * 64 + 64)).astype(F32)
    return tri, expand


def _ssd_conv4(ext, cw_ref, cb_ref):
    L = SSD_L
    pre = cb_ref[...] + cw_ref[0:1, :] * ext[pl.ds(HALO - 3, L), :]
    for kk in range(1, 4):
        pre = pre + cw_ref[kk:kk + 1, :] * ext[pl.ds(HALO - 3 + kk, L), :]
    return pre


def _ssd_common(xbc_ref, halo_ref, dt_ref, cw_ref, cb_ref, dtb_ref, alog_ref, ext, first):
    L = SSD_L
    tri, expand = _ssd_consts()
    ext[0:HALO, :] = halo_ref[...].astype(F32) * (1.0 - first.astype(F32))
    ext[HALO:HALO + L, :] = xbc_ref[...].astype(F32)
    pre = _ssd_conv4(ext, cw_ref, cb_ref)
    sg = _sig(pre)
    act = pre * sg
    lane = lax.broadcasted_iota(jnp.int32, (1, LANE), 1)
    m4 = lane < SSD_H
    raw = dt_ref[...].astype(F32) + dtb_ref[...]
    dtv = jnp.where(m4, _softplus(raw), 0.0)
    av = jnp.where(m4, -jnp.exp(alog_ref[...]), 0.0)
    adt = dtv * av
    acs = _dot01(adt, tri, x_first=False)
    acs_b = _dot01(acs, expand)
    dt_b = _dot01(dtv, expand)
    return dict(tri=tri, expand=expand, pre=pre, sg=sg, act=act, raw=raw, dtv=dtv, av=av, m4=m4, acs=acs, acs_b=acs_b,
                dt_b=dt_b, lane=lane)


def _head_terms(cm, h):
    L = SSD_L
    acs, tri = cm["acs"], cm["tri"]
    lane_l = lax.broadcasted_iota(jnp.int32, (L, LANE), 1)
    sub_l = lax.broadcasted_iota(jnp.int32, (LANE, L), 0)
    col = jnp.sum(jnp.where(lane_l == h, acs, 0.0), axis=1, keepdims=True)
    row = jnp.sum(jnp.where(sub_l == h, acs.T, 0.0), axis=0, keepdims=True)
    dec = jnp.where(tri, jnp.exp(jnp.where(tri, col - row, NEG)), 0.0)
    rowi = lax.broadcasted_iota(jnp.int32, (L, 1), 0)
    last = jnp.sum(jnp.where(rowi == L - 1, col, 0.0), axis=0, keepdims=True)
    dte = jnp.exp(last - col)
    return col, dec, last, dte


def _ssd_fwd(proj, cw, cb, dtb, alog, dvec, nw):
    s = proj.shape[0]
    L = SSD_L
    nc = s // L

    def body(z_ref, xbc_ref, halo_ref, dt_ref, cw_ref, cb_ref, dtb_ref, alog_ref, d_ref, nw_ref, y_ref, ypre_ref, st_ref, ext, state):
        i = pl.program_id(0)

        @pl.when(i == 0)
        def _():
            state[...] = jnp.zeros_like(state)

        cm = _ssd_common(xbc_ref, halo_ref, dt_ref, cw_ref, cb_ref, dtb_ref, alog_ref, ext, i == 0)
        act = cm["act"]
        xs = act[:, 0:256]
        bm = (act[:, 256:384], act[:, 384:512])
        cmat = (act[:, 512:640].astype(BF16), act[:, 640:768].astype(BF16))
        xdt = xs * cm["dt_b"]
        prev = state[...]
        st_ref[...] = prev
        prev_bf = prev.astype(BF16)
        gm = [_dot(cmat[g], bm[g].astype(BF16), NT) for g in range(2)]
        lane2 = lax.broadcasted_iota(jnp.int32, (1, SSD_DIM), 1)
        rows2 = lax.broadcasted_iota(jnp.int32, (SSD_DIM, 1), 0)
        ydiag = jnp.zeros((L, SSD_DIM), F32)
        contrib = jnp.zeros((SSD_DIM, LANE), F32)
        cd_rows = jnp.zeros((SSD_DIM, 1), F32)
        for h in range(SSD_H):
            g = h // 2
            col, dec, last, dte = _head_terms(cm, h)
            mh = (lane2 >= 64 * h) & (lane2 < 64 * h + 64)
            xm = jnp.where(mh, xdt, 0.0).astype(BF16)
            ydiag += _dot((gm[g] * dec).astype(BF16), xm)
            contrib += _dot(xm, (bm[g] * dte).astype(BF16), TN)
            cd_rows += jnp.where((rows2 >= 64 * h) & (rows2 < 64 * h + 64), jnp.exp(last), 0.0)
        yo = jnp.where(lane2 < 128, _dot(cmat[0], prev_bf, NT), _dot(cmat[1], prev_bf, NT))
        y = ydiag + yo * jnp.exp(cm["acs_b"]) + xs * d_ref[...]
        state[...] = prev * cd_rows + contrib
        ypre_ref[...] = y
        zz = z_ref[...].astype(F32)
        gt = y * zz * _sig(zz)
        y_ref[...] = (gt * lax.rsqrt(jnp.mean(gt * gt, axis=-1, keepdims=True) + EPS) * nw_ref[...]).astype(BF16)

    def vec(w):
        return pl.BlockSpec((1, w), lambda i: (0, 0))

    return pl.pallas_call(
        body, name="ssd_fwd", grid=(nc,),
        out_shape=(jax.ShapeDtypeStruct((s, SSD_DIM), BF16), jax.ShapeDtypeStruct((s, SSD_DIM), F32),
                   jax.ShapeDtypeStruct((nc, SSD_DIM, LANE), F32)),
        in_specs=[pl.BlockSpec((L, SSD_DIM), lambda i: (i, 5)), pl.BlockSpec((L, SSD_CONV), lambda i: (i, 2)),
                  pl.BlockSpec((HALO, SSD_CONV), lambda i: (_prev_halo(i, L), 2)), pl.BlockSpec((L, LANE), lambda i: (i, 18)),
                  pl.BlockSpec((4, SSD_CONV), lambda i: (0, 0)), vec(SSD_CONV), vec(LANE), vec(LANE), vec(SSD_DIM), vec(SSD_DIM)],
        out_specs=(pl.BlockSpec((L, SSD_DIM), lambda i: (i, 0)), pl.BlockSpec((L, SSD_DIM), lambda i: (i, 0)),
                   pl.BlockSpec((None, SSD_DIM, LANE), lambda i: (i, 0, 0))),
        scratch_shapes=[pltpu.VMEM((L + HALO, SSD_CONV), F32), pltpu.VMEM((SSD_DIM, LANE), F32)], compiler_params=_cp("arbitrary"),
    )(proj, proj, proj, proj, cw, cb, dtb, alog, dvec, nw)


def _ssd_bwd(proj, dcat, ypre, states, cw, cb, dtb, alog, dvec, nw):
    s = proj.shape[0]
    L = SSD_L
    nc = s // L

    def body(z_ref, xbc_ref, halo_ref, dt_ref, dy_ref, ypre_ref, st_ref, cw_ref, cb_ref, dtb_ref, alog_ref, d_ref, nw_ref,
             dz_ref, dxbc_ref, ddt_ref, dcw_ref, dcb_ref, ddtb_ref, dalog_ref, dd_ref, dnw_ref, ext, ext2, carry, dstate, ddl):
        i = pl.program_id(0)
        r = nc - 1 - i

        @pl.when(i == 0)
        def _():
            for ref in (dcw_ref, dcb_ref, ddtb_ref, dalog_ref, dd_ref, dnw_ref, carry, dstate, ddl):
                ref[...] = jnp.zeros_like(ref)

        cm = _ssd_common(xbc_ref, halo_ref, dt_ref, cw_ref, cb_ref, dtb_ref, alog_ref, ext, r == 0)
        tri, expand, act = cm["tri"], cm["expand"], cm["act"]
        xs = act[:, 0:256]
        bm = (act[:, 256:384], act[:, 384:512])
        cmat = (act[:, 512:640], act[:, 640:768])
        bm_bf = [v.astype(BF16) for v in bm]
        cm_bf = [v.astype(BF16) for v in cmat]
        dt_b = cm["dt_b"]
        xdt = xs * dt_b
        xdt_bf = xdt.astype(BF16)
        ea_b = jnp.exp(cm["acs_b"])
        prev = st_ref[...]
        prev_bf = prev.astype(BF16)
        lane2 = lax.broadcasted_iota(jnp.int32, (1, SSD_DIM), 1)
        rows2 = lax.broadcasted_iota(jnp.int32, (SSD_DIM, 1), 0)
        lane_l = lax.broadcasted_iota(jnp.int32, (L, LANE), 1)
        rowi = lax.broadcasted_iota(jnp.int32, (L, 1), 0)

        y = ypre_ref[...]
        zz = z_ref[...].astype(F32)
        sz = _sig(zz)
        gt = y * zz * sz
        dgt, dwt = _rms_bwd_math(gt, nw_ref[...], dy_ref[...].astype(F32))
        dnw_ref[...] += _rowsum(dwt)
        dy = dgt * zz * sz
        dz_ref[...] = (dgt * y * sz * (1.0 + zz * (1.0 - sz))).astype(BF16)

        ddl[0:1, :] += _rowsum(dy * xs)
        dxs = dy * d_ref[...]

        yo = jnp.where(lane2 < 128, _dot(cm_bf[0], prev_bf, NT), _dot(cm_bf[1], prev_bf, NT))
        dacs_b = dy * yo * ea_b
        dyo = dy * ea_b
        dyo_g = (jnp.where(lane2 < 128, dyo, 0.0).astype(BF16), jnp.where(lane2 >= 128, dyo, 0.0).astype(BF16))
        dc = [_dot(dyo_g[g], prev_bf) for g in range(2)]
        dprev = _dot(dyo_g[0], cm_bf[0], TN) + _dot(dyo_g[1], cm_bf[1], TN)

        gm = [_dot(cm_bf[g], bm_bf[g], NT) for g in range(2)]
        dgm = [jnp.zeros((L, L), F32), jnp.zeros((L, L), F32)]
        db = [jnp.zeros((L, LANE), F32), jnp.zeros((L, LANE), F32)]
        dxdt = jnp.zeros((L, SSD_DIM), F32)
        dacs = jnp.zeros((L, LANE), F32)
        dlast = jnp.zeros((1, LANE), F32)
        cd_rows = jnp.zeros((SSD_DIM, 1), F32)
        dst = dstate[...]
        dst_bf = dst.astype(BF16)
        dsp = dst * prev
        ones = jnp.ones((L, LANE), F32)
        for h in range(SSD_H):
            g = h // 2
            col, dec, last, dte = _head_terms(cm, h)
            mh = (lane2 >= 64 * h) & (lane2 < 64 * h + 64)
            rh = (rows2 >= 64 * h) & (rows2 < 64 * h + 64)
            sc = gm[g] * dec
            xm = jnp.where(mh, xdt, 0.0).astype(BF16)
            dym = jnp.where(mh, dy, 0.0).astype(BF16)
            dsc = _dot(dym, xdt_bf, NT)
            dxdt += _dot(sc.astype(BF16), dym, TN)
            dgm[g] += dsc * dec
            dd = dsc * sc
            rs = jnp.sum(dd, axis=1, keepdims=True)
            cs = _dot01(dd, ones, TN)
            dacs += jnp.where(lane_l == h, rs - cs, 0.0)
            bd = (bm[g] * dte).astype(BF16)
            dxdt += jnp.where(mh, _dot(bd, dst_bf, NT), 0.0)
            dbd = _dot(xm, dst_bf)
            db[g] += dbd * dte
            tt = jnp.sum(dbd * bm[g], axis=1, keepdims=True) * dte
            dacs += jnp.where(lane_l == h, -tt, 0.0)
            cdh = jnp.exp(last)
            dcd = jnp.sum(jnp.sum(jnp.where(rh, dsp, 0.0), axis=1, keepdims=True), axis=0, keepdims=True)
            dlast += jnp.where(cm["lane"] == h, jnp.sum(tt, axis=0, keepdims=True) + dcd * cdh, 0.0)
            cd_rows += jnp.where(rh, cdh, 0.0)
        dacs += jnp.where(rowi == L - 1, dlast, 0.0)
        dacs += _dot01(dacs_b, expand, NT)
        dstate[...] = dprev + dst * cd_rows

        for g in range(2):
            dgb = dgm[g].astype(BF16)
            dc[g] += _dot(dgb, bm_bf[g])
            db[g] += _dot(dgb, cm_bf[g], TN)

        dadt = _dot01(dacs, tri, TN, x_first=False)
        ddtv = dadt * cm["av"] + _dot01(dxdt * xs, expand, NT)
        dalog_ref[...] += _rowsum(dadt * cm["dtv"]) * cm["av"]
        dxs += dxdt * dt_b
        draw = jnp.where(cm["m4"], ddtv * _sig(cm["raw"]), 0.0)
        ddtb_ref[...] += _rowsum(draw)
        ddt_ref[...] = draw.astype(BF16)

        dact = jnp.concatenate([dxs, db[0], db[1], dc[0], dc[1]], axis=1)
        sg, pre = cm["sg"], cm["pre"]
        dpre = dact * sg * (1.0 + pre * (1.0 - sg))
        dcb_ref[...] += _rowsum(dpre)
        for kk in range(4):
            dcw_ref[kk:kk + 1, :] += _rowsum(dpre * ext[pl.ds(HALO - 3 + kk, L), :])
        ext2[0:L, :] = dpre
        ext2[L:L + HALO, :] = carry[...]
        dx = cw_ref[3:4, :] * ext2[pl.ds(0, L), :]
        for kk in range(3):
            dx = dx + cw_ref[kk:kk + 1, :] * ext2[pl.ds(3 - kk, L), :]
        dxbc_ref[...] = dx.astype(BF16)
        carry[...] = dpre[0:HALO, :]

        @pl.when(i == nc - 1)
        def _():
            dd_ref[...] = _dot01(ddl[...], expand, NT)

    def vec(w):
        return pl.BlockSpec((1, w), lambda i: (0, 0))

    def rv(i):
        return nc - 1 - i

    return pl.pallas_call(
        body, name="ssd_bwd", grid=(nc,),
        out_shape=(jax.ShapeDtypeStruct((s, SSD_DIM), BF16), jax.ShapeDtypeStruct((s, SSD_CONV), BF16), jax.ShapeDtypeStruct((s, LANE), BF16),
                   jax.ShapeDtypeStruct((4, SSD_CONV), F32), jax.ShapeDtypeStruct((1, SSD_CONV), F32), jax.ShapeDtypeStruct((1, LANE), F32),
                   jax.ShapeDtypeStruct((1, LANE), F32), jax.ShapeDtypeStruct((8, LANE), F32), jax.ShapeDtypeStruct((1, SSD_DIM), F32)),
        in_specs=[pl.BlockSpec((L, SSD_DIM), lambda i: (rv(i), 5)), pl.BlockSpec((L, SSD_CONV), lambda i: (rv(i), 2)),
                  pl.BlockSpec((HALO, SSD_CONV), lambda i: (_prev_halo(rv(i), L), 2)), pl.BlockSpec((L, LANE), lambda i: (rv(i), 18)),
                  pl.BlockSpec((L, SSD_DIM), lambda i: (rv(i), 5)), pl.BlockSpec((L, SSD_DIM), lambda i: (rv(i), 0)),
                  pl.BlockSpec((None, SSD_DIM, LANE), lambda i: (rv(i), 0, 0)),
                  pl.BlockSpec((4, SSD_CONV), lambda i: (0, 0)), vec(SSD_CONV), vec(LANE), vec(LANE), vec(SSD_DIM), vec(SSD_DIM)],
        out_specs=(pl.BlockSpec((L, SSD_DIM), lambda i: (rv(i), 0)), pl.BlockSpec((L, SSD_CONV), lambda i: (rv(i), 0)),
                   pl.BlockSpec((L, LANE), lambda i: (rv(i), 0)), pl.BlockSpec((4, SSD_CONV), lambda i: (0, 0)), vec(SSD_CONV),
                   vec(LANE), vec(LANE), pl.BlockSpec((8, LANE), lambda i: (0, 0)), vec(SSD_DIM)),
        scratch_shapes=[pltpu.VMEM((L + HALO, SSD_CONV), F32), pltpu.VMEM((L + HALO, SSD_CONV), F32), pltpu.VMEM((HALO, SSD_CONV), F32),
                        pltpu.VMEM((SSD_DIM, LANE), F32), pltpu.VMEM((8, SSD_DIM), F32)],
        compiler_params=_cp("arbitrary"),
    )(proj, proj, proj, proj, dcat, ypre, states, cw, cb, dtb, alog, dvec, nw)


def _adamw(parts, w, m, v, name):
    nl, r, c = w.shape
    tr = r
    for cand in (256, 128, 64, 32, 16, 8):
        if r % cand == 0 and (cand * c * 4) <= 2 * 1024 * 1024:
            tr = cand
            break
    c1 = 1.0 - B1 ** STEP
    c2 = 1.0 - B2 ** STEP

    def body(p_ref, w_ref, m_ref, v_ref, g_ref, d_ref, nm_ref, nv_ref):
        g = p_ref[0].astype(F32)
        for dev in range(1, NDEV):
            g = g + p_ref[dev].astype(F32)
        mn = B1 * m_ref[...] + (1.0 - B1) * g
        vn = B2 * v_ref[...] + (1.0 - B2) * (g * g)
        g_ref[...] = g
        nm_ref[...] = mn
        nv_ref[...] = vn
        d_ref[...] = -LR * ((mn / c1) / (jnp.sqrt(vn / c2) + AEPS) + WD * w_ref[...])

    blk = pl.BlockSpec((None, tr, c), lambda l, i: (l, i, 0))
    out = jax.ShapeDtypeStruct((nl, r, c), F32)
    return pl.pallas_call(
        body, name=name, grid=(nl, r // tr), out_shape=(out, out, out, out),
        in_specs=[pl.BlockSpec((NDEV, None, tr, c), lambda l, i: (0, l, i, 0)), blk, blk, blk], out_specs=(blk, blk, blk, blk),
        compiler_params=_cp("parallel", "parallel"),
    )(parts, w, m, v)


def _pad_win(w):
    z = lambda n: jnp.zeros((w.shape[0], n), w.dtype)
    return jnp.concatenate([w[:, :384], z(64), w[:, 384:416], z(32), w[:, 416:], z(124)], axis=1)


def _unpad_win(g):
    return jnp.concatenate([g[:, :384], g[:, 448:480], g[:, 512:2308]], axis=1)


def _pad_wout(w):
    att = jnp.pad(w[:512].reshape(HEADS, 64, D), ((0, 0), (64, 0), (0, 0))).reshape(HEADS * LANE, D)
    return jnp.concatenate([att, w[512:]], axis=0)


def _unpad_wout(g):
    att = g[:HEADS * LANE].reshape(HEADS, LANE, D)[:, 64:, :].reshape(512, D)
    return jnp.concatenate([att, g[HEADS * LANE:]], axis=0)


def _lanes(v, n=LANE):
    return jnp.pad(v, (0, n - v.shape[0])).reshape(1, n)


def _prep_ffn(big):
    return {"wout": _pad_wout(big["w_out"].reshape(1024, D)), "wup": big["ffn_w_up"], "fcw": big["ffn_conv_w"].astype(F32),
            "wdown": big["ffn_w_down"].reshape(4, FB, D)}


def _prep_layer(big, small, l):
    p = _prep_ffn(big) if "w_out" in big else {}
    p["win"] = _pad_win(big["w_in"].reshape(D, 2212))
    p["wq"] = jnp.pad(big["mla_w_q_up"], ((0, 0), (0, 0), (0, LANE - 96)))
    p["wkv"] = big["mla_w_kv_up"]
    p["scw"] = big["sc_conv_w"].astype(F32).transpose(1, 0, 2).reshape(3, SC)
    p["ssdcw"] = big["ssd_conv_w"].astype(F32).transpose(1, 0, 2).reshape(4, SSD_CONV)
    for nm in ("norm_mix_pre", "norm_mix_post", "norm_ffn_pre", "norm_ffn_post", "mla_q_norm", "mla_kv_norm", "ssd_conv_b", "ssd_norm"):
        p[nm] = small[nm][l].reshape(1, -1)
    p["dtb"] = _lanes(small["ssd_dt_bias"][l])
    p["alog"] = _lanes(small["ssd_a_log"][l])
    p["dvec"] = jnp.repeat(small["ssd_d"][l], 64).reshape(1, SSD_DIM)
    p["fcb"] = small["ffn_conv_b"][l].reshape(NDEV, 1, FB)
    return p


def _rope_tables(positions):
    inv_freq = 1.0 / (ROPE_THETA ** (jnp.arange(0, ROPE, 2, dtype=F32) / ROPE))
    ang = positions.astype(F32)[:, None] * inv_freq
    cos, sin = jnp.cos(ang), jnp.sin(ang)
    s = positions.shape[0]
    z = lambda n: jnp.zeros((s, n), F32)
    tc = jnp.concatenate([jnp.ones((s, 64), F32), cos, cos, z(32)], axis=1)
    ta = jnp.concatenate([z(64), -sin, z(48)], axis=1)
    tb = jnp.concatenate([z(80), sin, z(32)], axis=1)
    return tc, ta, tb


def _layer_fwd(xv, p, tabs, prefetch=None, prep_rest=None, h=None, next_norm=None):
    if h is None:
        h = _rms(xv, p["norm_mix_pre"], BF16, "rms_pre")
    proj = _mm_rows("in_proj", h, p["win"], BF16, NN)
    q, k, kv = _mla_prep(proj, tabs, p["mla_q_norm"], p["mla_kv_norm"], p["wq"], p["wkv"])
    o, lse, gathered = _flash_fwd(q, k, kv, prefetch)
    if prep_rest is not None:
        p = {**p, **prep_rest(gathered)}
    yconv = _sconv_fwd(proj, p["scw"])
    yssd, ypre, states = _ssd_fwd(proj, p["ssdcw"], p["ssd_conv_b"], p["dtb"], p["alog"], p["dvec"], p["ssd_norm"])
    cat = jnp.concatenate([o, yconv, yssd], axis=1)
    mixed = _mm_rows("out_proj", cat, p["wout"], BF16, NN)
    x1, h2 = _add_rms(xv, mixed, p["norm_mix_post"], "add_rms", p["norm_ffn_pre"])
    upre = _mm_up(h2, p["wup"])
    act = _ffn_act(upre, p["fcw"], p["fcb"])
    f = _mm_down(act, p["wdown"])
    x2 = _add_rms(x1, f, p["norm_ffn_post"], "add_rms", next_norm)
    saved = dict(x=xv, h=h, proj=proj, q=q, k=k, kv=kv, lse=lse, ypre=ypre, states=states, cat=cat, mixed=mixed, x1=x1, h2=h2,
                 upre=upre, act=act, f=f)
    return x2, saved, p, gathered


def _pack_grads(grads, group_ids):
    return [_group_pack(GROUPS[gi], lambda n: grads[n].reshape((NDEV,) + _rows2(n, True)), (NDEV,)) for gi in group_ids]


def _layer_bwd(dx2, sv, p, tabs, exchange=False, pending=None, head=None, below=None):
    df, g_nfpo = head if head is not None else _rms_bwd(sv["f"], p["norm_ffn_post"], dx2, None, BF16, "rms_bwd_post")
    dact = _mm_dact(df, p["wdown"])
    g_wdown = _mm_dwdown(sv["act"], df)
    dupre, g_fcb, g_fcw = _ffn_bwd(sv["upre"], dact, p["fcw"], p["fcb"])
    dh2 = _mm_dh2(dupre, p["wup"])
    g_wup = _mm_dwup(sv["h2"], dupre)
    dx1, dmixed, g_nfp, g_nmpo = _rms_bwd2(sv["x1"], p["norm_ffn_pre"], dh2, dx2, sv["mixed"], p["norm_mix_post"])
    dcat = _mm_rows("dcat", dmixed, p["wout"], BF16, NT)
    g_wout = _mm_wgrad("dw_out", sv["cat"], dmixed, BF16)
    big = {
        "w_out": _unpad_wout(g_wout).reshape(NDEV, 128, D),
        "ffn_w_up": g_wup,
        "ffn_conv_w": g_fcw.astype(BF16),
        "ffn_w_down": g_wdown.reshape(NDEV, 352, D),
    }
    outgoing = _pack_grads(big, FFN_SIDE) + (pending or []) if exchange else None
    dq, dk, dv, received = _flash_bwd(sv["q"], sv["k"], sv["kv"], sv["cat"], dcat, sv["lse"], outgoing)
    dcq, dckv, dkr, g_wq, g_wkv, g_qn, g_kvn = _mla_prep_bwd(sv["proj"], tabs, p["mla_q_norm"], p["mla_kv_norm"], p["wq"], p["wkv"], dq, dk, dv)
    dscb, dscc, dsch, g_scw = _sconv_bwd(sv["proj"], dcat, p["scw"])
    dz, dxbc, ddt, g_cw, g_cb, g_dtb, g_alog, g_d, g_nw = _ssd_bwd(
        sv["proj"], dcat, sv["ypre"], sv["states"], p["ssdcw"], p["ssd_conv_b"], p["dtb"], p["alog"], p["dvec"], p["ssd_norm"])
    dproj = jnp.concatenate([dcq, dckv, dkr, dscb, dscc, dsch, dz, dxbc, ddt], axis=1)
    dh = _mm_rows("dh", dproj, p["win"], BF16, NT)
    g_win = _mm_wgrad("dw_in", sv["h"], dproj, BF16)
    big.update({
        "w_in": _unpad_win(g_win).reshape(NDEV, 128, 2212),
        "mla_w_q_up": g_wq[:, :, :96].astype(BF16),
        "mla_w_kv_up": g_wkv.astype(BF16),
        "sc_conv_w": g_scw.reshape(3, NDEV, 32).transpose(1, 0, 2).astype(BF16),
        "ssd_conv_w": g_cw.reshape(4, NDEV, 96).transpose(1, 0, 2).astype(BF16),
    })
    head_below, last_received = None, None
    if below is not None:
        dx, df_below, g_nmp, g_below = _rms_bwd2(sv["x"], p["norm_mix_pre"], dh, dx1, *below)
        head_below = (df_below, g_below)
    elif exchange:
        dx, g_nmp, last_received = _rms_bwd(sv["x"], p["norm_mix_pre"], dh, dx1, F32, "rms_bwd_pre", _pack_grads(big, ATT_SIDE))
    else:
        dx, g_nmp = _rms_bwd(sv["x"], p["norm_mix_pre"], dh, dx1, F32, "rms_bwd_pre")
    small = {
        "norm_mix_pre": g_nmp[0], "norm_mix_post": g_nmpo[0], "norm_ffn_pre": g_nfp[0], "norm_ffn_post": g_nfpo[0],
        "mla_q_norm": g_qn[0], "mla_kv_norm": g_kvn[0], "ssd_conv_b": g_cb[0], "ssd_dt_bias": g_dtb[0, :SSD_H],
        "ssd_a_log": g_alog[0, :SSD_H], "ssd_d": g_d[0, :SSD_H], "ssd_norm": g_nw[0], "ffn_conv_b": g_fcb.reshape(-1),
    }
    return dx, big, small, received, head_below, last_received


def _local_step(xv, positions, target, layers):
    tabs = _rope_tables(positions)
    saved = []
    for p in layers:
        xv, sv, _, _ = _layer_fwd(xv, p, tabs)
        saved.append(sv)
    loss, dx = _loss_head(xv, target)
    bigs, smalls = [None] * DEPTH, [None] * DEPTH
    head = None
    for l in reversed(range(len(layers))):
        below = (saved[l - 1]["f"], layers[l - 1]["norm_ffn_post"]) if l > 0 else None
        dx, bigs[l], smalls[l], _, head, _ = _layer_bwd(dx, saved[l], layers[l], tabs, head=head, below=below)
    return loss[0, 0], dx, bigs, smalls


def _rows2(n, layer=False):
    shape = SHAPES[n][1:] if layer else SHAPES[n]
    return (math.prod(shape[:-1]), shape[-1])


def _group_pack(group, get, lead):
    width, names = group
    pieces = []
    for n in names:
        rows, cols = _rows2(n, True)
        pad = [(0, 0)] * len(lead) + [(0, -rows % 16), (0, width - cols)]
        pieces.append(jnp.pad(get(n), pad))
    return pieces[0] if len(pieces) == 1 else jnp.concatenate(pieces, axis=len(lead))


def _group_unpack(group, buf):
    _, names = group
    res, off = {}, 0
    for n in names:
        rows, cols = _rows2(n, True)
        res[n] = buf[:, off:off + rows, :cols]
        off += rows + (-rows % 16)
    return res


def kernel(x, positions, norm_mix_pre, norm_mix_post, norm_ffn_pre, norm_ffn_post, w_in, mla_q_norm, mla_w_q_up, mla_kv_norm, mla_w_kv_up, sc_conv_w, ssd_conv_w, ssd_conv_b, ssd_dt_bias, ssd_a_log, ssd_d, ssd_norm, w_out, ffn_w_up, ffn_conv_w, ffn_conv_b, ffn_w_down, loss_target, m_norm_mix_pre, m_norm_mix_post, m_norm_ffn_pre, m_norm_ffn_post, m_w_in, m_mla_q_norm, m_mla_w_q_up, m_mla_kv_norm, m_mla_w_kv_up, m_sc_conv_w, m_ssd_conv_w, m_ssd_conv_b, m_ssd_dt_bias, m_ssd_a_log, m_ssd_d, m_ssd_norm, m_w_out, m_ffn_w_up, m_ffn_conv_w, m_ffn_conv_b, m_ffn_w_down, v_norm_mix_pre, v_norm_mix_post, v_norm_ffn_pre, v_norm_ffn_post, v_w_in, v_mla_q_norm, v_mla_w_q_up, v_mla_kv_norm, v_mla_w_kv_up, v_sc_conv_w, v_ssd_conv_w, v_ssd_conv_b, v_ssd_dt_bias, v_ssd_a_log, v_ssd_d, v_ssd_norm, v_w_out, v_ffn_w_up, v_ffn_conv_w, v_ffn_conv_b, v_ffn_w_down):
    given = dict(locals())
    w = {n: given[n] for n in WEIGHTS}
    m = {n: given["m_" + n] for n in WEIGHTS}
    v = {n: given["v_" + n] for n in WEIGHTS}

    def shards(l, group_ids):
        return [_group_pack(GROUPS[gi], lambda n: w[n][l].astype(BF16).reshape(_rows2(n, True)), ()) for gi in group_ids]

    def unpacked(bufs, group_ids):
        big = {}
        for gi, buf in zip(group_ids, bufs):
            for n, piece in _group_unpack(GROUPS[gi], buf).items():
                big[n] = piece.reshape((NDEV,) + SHAPES[n][1:])
        return big

    small_w = {n: w[n] for n, _ in SMALL}
    tabs = _rope_tables(positions[0])
    xv, h, layers, saved = x[0], None, [], []
    att = _all_gather(shards(0, ATT_SIDE), "gather_weights")
    for l in range(DEPTH):
        prefetch = shards(l, FFN_SIDE) + (shards(l + 1, ATT_SIDE) if l + 1 < DEPTH else [])
        nxt = w["norm_mix_pre"][l + 1].reshape(1, D) if l + 1 < DEPTH else None
        xv, sv, p, gathered = _layer_fwd(xv, _prep_layer(unpacked(att, ATT_SIDE), small_w, l), tabs, prefetch,
                                         lambda got: _prep_ffn(unpacked(got[:len(FFN_SIDE)], FFN_SIDE)), h, nxt)
        xv, h = xv if nxt is not None else (xv, None)
        att = gathered[len(FFN_SIDE):]
        layers.append(p)
        saved.append(sv)
    loss, dx = _loss_head(xv, loss_target[0])
    loss = lax.psum(loss[0, 0], ("x", "y", "c"))

    smalls, pending, head = [None] * DEPTH, None, None
    recvs = [[None] * len(GROUPS) for _ in range(DEPTH)]
    for l in reversed(range(DEPTH)):
        below = (saved[l - 1]["f"], layers[l - 1]["norm_ffn_post"]) if l > 0 else None
        dx, grads, smalls[l], received, head, last = _layer_bwd(dx, saved[l], layers[l], tabs, True, pending, head, below)
        for pos, gi in enumerate(FFN_SIDE):
            recvs[l][gi] = received[pos]
        if pending is not None:
            for pos, gi in enumerate(ATT_SIDE):
                recvs[l + 1][gi] = received[len(FFN_SIDE) + pos]
        pending = _pack_grads(grads, ATT_SIDE) if l > 0 else None
    for gi, buf in zip(ATT_SIDE, last):
        recvs[0][gi] = buf
    out = {}
    for gi, g in enumerate(GROUPS):
        per_layer = [_group_unpack(g, recvs[l][gi]) for l in range(DEPTH)]
        for n in g[1]:
            parts = jnp.stack([per_layer[l][n] for l in range(DEPTH)], axis=1)
            out[n] = _adamw(parts, w[n], m[n], v[n], "adamw_" + n)

    total = sum(width for _, width in SMALL)
    padded = -(-total // (8 * LANE)) * 8 * LANE
    pk = lambda a: jnp.pad(a, ((0, 0), (0, padded - total))).reshape(1, DEPTH * padded // LANE, LANE)
    sflat = jnp.stack([jnp.concatenate([smalls[l][n] for n, _ in SMALL]) for l in range(DEPTH)])
    sparts = _all_gather([pk(sflat)[0]], "gather_small_grads")[0]
    pw = lambda d: pk(jnp.concatenate([d[n] for n, _ in SMALL], axis=1))
    res = _adamw(sparts[:, None], pw(w), pw(m), pw(v), "adamw_small")
    off = 0
    for n, width in SMALL:
        out[n] = [a.reshape(DEPTH, padded)[:, off:off + width] for a in res]
        off += width

    return (loss, dx[None], *[out[n][0] for n in WEIGHTS], *[out[n][1] for n in WEIGHTS],
            *[out[n][2] for n in WEIGHTS], *[out[n][3] for n in WEIGHTS])
```

```python
import functools
import math

import jax
import jax.numpy as jnp
from jax import lax
from jax.experimental import pallas as pl
from jax.experimental.pallas import tpu as pltpu

F32 = jnp.float32
BF16 = jnp.bfloat16

D = 1024
DEPTH = 4
NDEV = 8
HEADS = 8
QL = 256
KVL = 128
ROPE = 32
NOPE = 64
SC = 256
SSD_DIM = 256
SSD_CONV = 768
SSD_H = 4
SSD_L = 128
FFN = 2816
FB = 704
EPS = 1e-6
ROPE_THETA = 10000.0
ATT_SCALE = 96 ** -0.5
LOG2E = 1.4426950408889634
LR, B1, B2, AEPS, WD, STEP = 0.001, 0.9, 0.999, 1e-08, 0.01, 10

PW = 2432
CATW = 1536

ROW_TILE = 512
ROW_CHUNK = 16
NORM_CHUNK = 32
NORM_TILE = 512
MM_TILE = 1024
ATT_TILE = 512
ATT_QW = 1
BWD_TILE = 512
BWD_QW = 1
FWD_HEADS = 4
BWD_HEADS = 2
HALO = 16
LANE = 128
NEG = -1e30
NN = (((1,), (0,)), ((), ()))
NT = (((1,), (1,)), ((), ()))
TN = (((0,), (0,)), ((), ()))
VMEM_LIMIT = 56 * 1024 * 1024

SHARDED = (
    ("w_in", (4, 128, 2212)),
    ("mla_w_q_up", (4, 256, 96)),
    ("mla_w_kv_up", (4, 128, 128)),
    ("sc_conv_w", (4, 3, 32)),
    ("ssd_conv_w", (4, 4, 96)),
    ("w_out", (4, 128, 1024)),
    ("ffn_w_up", (4, 1024, 704)),
    ("ffn_conv_w", (4, 3, 704)),
    ("ffn_w_down", (4, 352, 1024)),
)
SHAPES = dict(SHARDED)
GROUPS = (
    (2212, ("w_in",)),
    (1024, ("w_out",)),
    (704, ("ffn_w_up",)),
    (96, ("mla_w_q_up", "ssd_conv_w", "sc_conv_w")),
    (128, ("mla_w_kv_up",)),
    (1024, ("ffn_w_down",)),
    (704, ("ffn_conv_w",)),
)
ATT_SIDE = (0, 3, 4)
FFN_SIDE = (1, 2, 5, 6)
SMALL = (
    ("norm_mix_pre", 1024), ("norm_mix_post", 1024), ("norm_ffn_pre", 1024), ("norm_ffn_post", 1024),
    ("mla_q_norm", 256), ("mla_kv_norm", 128), ("ssd_conv_b", 768), ("ssd_dt_bias", 4), ("ssd_a_log", 4),
    ("ssd_d", 4), ("ssd_norm", 256), ("ffn_conv_b", 5632),
)
WEIGHTS = ("norm_mix_pre", "norm_mix_post", "norm_ffn_pre", "norm_ffn_post", "w_in", "mla_q_norm", "mla_w_q_up",
           "mla_kv_norm", "mla_w_kv_up", "sc_conv_w", "ssd_conv_w", "ssd_conv_b", "ssd_dt_bias", "ssd_a_log", "ssd_d",
           "ssd_norm", "w_out", "ffn_w_up", "ffn_conv_w", "ffn_conv_b", "ffn_w_down")


def _dot(a, b, dims=NN, precision=None):
    return lax.dot_general(a, b, dims, precision=precision, preferred_element_type=F32)


def _dot01(x, c, dims=NN, x_first=True):
    cb = c.astype(BF16)
    hi = x.astype(BF16)
    rest = x - hi.astype(F32)
    mid = rest.astype(BF16)
    lo = (rest - mid.astype(F32)).astype(BF16)
    one = (lambda t: _dot(t, cb, dims)) if x_first else (lambda t: _dot(cb, t, dims))
    return one(hi) + one(mid) + one(lo)


def _sig(v):
    return 0.5 * jnp.tanh(0.5 * v) + 0.5


def _cp(*sem):
    return pltpu.CompilerParams(dimension_semantics=sem, vmem_limit_bytes=VMEM_LIMIT)


def _rowsum(v):
    return jnp.sum(v, axis=0, keepdims=True)


def _prev_halo(i, ts):
    return jnp.maximum(i * (ts // HALO) - 1, 0)


def _next_halo(i, ts, n):
    return jnp.minimum((i + 1) * (ts // HALO), n * (ts // HALO) - 1)


def _gather_plan(x_refs, out_refs, send_sems, recv_sems, local_sems):
    n = len(x_refs)
    x, y, cc = lax.axis_index("x"), lax.axis_index("y"), lax.axis_index("c")
    me, sibling = (x, y, cc), (x, y, 1 - cc)
    chips = [(1 - x, y), (x, 1 - y), (1 - x, 1 - y)]

    def rows(t, px, py, pc):
        return out_refs[t].at[4 * px + 2 * py + pc]

    def copy(t, k, block, to, own=False):
        return pltpu.make_async_remote_copy(
            src_ref=x_refs[t] if own else rows(t, *block), dst_ref=rows(t, *block),
            send_sem=send_sems.at[7 * t + k], recv_sem=recv_sems.at[7 * t + k], device_id=to, device_id_type=pl.DeviceIdType.MESH)

    def local(t):
        return pltpu.make_async_copy(x_refs[t], rows(t, *me), local_sems.at[t])

    def start():
        for t in range(n):
            local(t).start()
            copy(t, 0, me, sibling, own=True).start()
            for j, chip in enumerate(chips):
                copy(t, 1 + j, me, (*chip, cc), own=True).start()

    def finish():
        for j, chip in enumerate(chips):
            for t in range(n):
                copy(t, 1 + j, (*chip, cc), me).wait_recv()
                copy(t, 4 + j, (*chip, cc), sibling).start()
        for t in range(n):
            copy(t, 0, sibling, me).wait_recv()
            for j, chip in enumerate(chips):
                copy(t, 4 + j, (*chip, 1 - cc), me).wait_recv()
        for t in range(n):
            copy(t, 0, me, sibling, own=True).wait_send()
            for j, chip in enumerate(chips):
                copy(t, 1 + j, me, (*chip, cc), own=True).wait_send()
                copy(t, 4 + j, (*chip, cc), sibling).wait_send()
            local(t).wait()

    return start, finish


def _exchange_plan(x_refs, out_refs, send_sems, recv_sems, local_sems):
    n = len(x_refs)
    x, y, cc = lax.axis_index("x"), lax.axis_index("y"), lax.axis_index("c")
    me = 4 * x + 2 * y + cc

    def copies():
        res = [pltpu.make_async_copy(x_refs[t].at[me], out_refs[t].at[me], local_sems.at[t]) for t in range(n)]
        for k in range(1, NDEV):
            px = 1 - x if k & 4 else x
            py = 1 - y if k & 2 else y
            pc = 1 - cc if k & 1 else cc
            peer = 4 * px + 2 * py + pc
            for t in range(n):
                res.append(pltpu.make_async_remote_copy(
                    src_ref=x_refs[t].at[peer], dst_ref=out_refs[t].at[me], send_sem=send_sems.at[7 * t + k - 1],
                    recv_sem=recv_sems.at[7 * t + k - 1], device_id=(px, py, pc), device_id_type=pl.DeviceIdType.MESH))
        return res

    def start():
        for cp in copies():
            cp.start()

    def finish():
        for cp in copies():
            cp.wait()

    return start, finish


def _comm_scratch(n):
    return [pltpu.SemaphoreType.DMA((7 * n,)), pltpu.SemaphoreType.DMA((7 * n,)), pltpu.SemaphoreType.DMA((n,))]


ANY = pl.BlockSpec(memory_space=pl.ANY)


def _all_gather(xs, name):
    n = len(xs)

    def body(*refs):
        start, finish = _gather_plan(refs[:n], refs[n:2 * n], *refs[2 * n:])
        start()
        finish()

    return pl.pallas_call(
        body, name=name, out_shape=[jax.ShapeDtypeStruct((NDEV,) + a.shape, a.dtype) for a in xs],
        in_specs=[ANY] * n, out_specs=[ANY] * n, scratch_shapes=_comm_scratch(n),
    )(*xs)


def _mm(name, a, b, out_shape, grid, a_spec, b_spec, o_spec, dims, acc_shape):
    nk = grid[2]

    def single(a_ref, b_ref, o_ref):
        o_ref[...] = _dot(a_ref[...], b_ref[...], dims).astype(o_ref.dtype)

    if nk == 1:
        return pl.pallas_call(
            single, name=name, grid=grid, out_shape=out_shape, in_specs=[a_spec, b_spec], out_specs=o_spec,
            compiler_params=_cp("parallel", "parallel", "arbitrary"),
        )(a, b)

    def body(a_ref, b_ref, o_ref, acc_ref):
        k = pl.program_id(2)

        @pl.when(k == 0)
        def _():
            acc_ref[...] = jnp.zeros_like(acc_ref)

        acc_ref[...] += _dot(a_ref[...], b_ref[...], dims)

        @pl.when(k == nk - 1)
        def _():
            o_ref[...] = acc_ref[...].astype(o_ref.dtype)

    return pl.pallas_call(
        body, name=name, grid=grid, out_shape=out_shape, in_specs=[a_spec, b_spec], out_specs=o_spec,
        scratch_shapes=[pltpu.VMEM(acc_shape, F32)], compiler_params=_cp("parallel", "parallel", "arbitrary"),
    )(a, b)


def _mm_rows(name, a, w, out_dtype, dims):
    s, k = a.shape
    n = w.shape[1] if dims == NN else w.shape[0]
    tm = min(MM_TILE, s)
    return _mm(name, a, w, jax.ShapeDtypeStruct((s, n), out_dtype), (s // tm, 1, 1),
               pl.BlockSpec((tm, k), lambda i, j, kk: (i, 0)), pl.BlockSpec(w.shape, lambda i, j, kk: (0, 0)),
               pl.BlockSpec((tm, n), lambda i, j, kk: (i, 0)), dims, (tm, n))


def _mm_wgrad(name, a, g, out_dtype):
    s, m = a.shape
    n = g.shape[1]
    tk = min(MM_TILE, s)
    return _mm(name, a, g, jax.ShapeDtypeStruct((m, n), out_dtype), (1, 1, s // tk),
               pl.BlockSpec((tk, m), lambda i, j, kk: (kk, 0)), pl.BlockSpec((tk, n), lambda i, j, kk: (kk, 0)),
               pl.BlockSpec((m, n), lambda i, j, kk: (0, 0)), TN, (m, n))


def _mm_dh(parts, wparts):
    s = parts[0].shape[0]
    tm = min(MM_TILE, s)
    n = len(parts)

    def body(*refs):
        acc = _dot(refs[0][...], refs[n][...], NT)
        for i in range(1, n):
            acc += _dot(refs[i][...], refs[n + i][...], NT)
        refs[2 * n][...] = acc.astype(BF16)

    return pl.pallas_call(
        body, name="dh", grid=(s // tm,), out_shape=jax.ShapeDtypeStruct((s, D), BF16),
        in_specs=[pl.BlockSpec((tm, a.shape[1]), lambda i: (i, 0)) for a in parts] + [pl.BlockSpec(w.shape, lambda i: (0, 0)) for w in wparts],
        out_specs=pl.BlockSpec((tm, D), lambda i: (i, 0)), compiler_params=_cp("parallel"),
    )(*parts, *wparts)


def _mm_dwin(h, parts):
    s = h.shape[0]
    tk = min(MM_TILE, s)
    nk = s // tk
    n = len(parts)

    def body(*refs):
        h_ref, g_refs, o_refs, accs = refs[0], refs[1:1 + n], refs[1 + n:1 + 2 * n], refs[1 + 2 * n:]
        k = pl.program_id(0)

        @pl.when(k == 0)
        def _():
            for acc in accs:
                acc[...] = jnp.zeros_like(acc)

        hv = h_ref[...]
        for i in range(n):
            accs[i][...] += _dot(hv, g_refs[i][...], TN)

        @pl.when(k == nk - 1)
        def _():
            for i in range(n):
                o_refs[i][...] = accs[i][...].astype(BF16)

    widths = [a.shape[1] for a in parts]
    return pl.pallas_call(
        body, name="dw_in", grid=(nk,), out_shape=[jax.ShapeDtypeStruct((D, w), BF16) for w in widths],
        in_specs=[pl.BlockSpec((tk, D), lambda k: (k, 0))] + [pl.BlockSpec((tk, w), lambda k: (k, 0)) for w in widths],
        out_specs=[pl.BlockSpec((D, w), lambda k: (0, 0)) for w in widths],
        scratch_shapes=[pltpu.VMEM((D, w), F32) for w in widths], compiler_params=_cp("arbitrary"),
    )(h, *parts)


def _mm_up(h2, wup):
    s = h2.shape[0]
    tm = min(MM_TILE, s)
    return _mm("ffn_up", h2, wup, jax.ShapeDtypeStruct((NDEV, s, FB), BF16), (NDEV, s // tm, 1),
               pl.BlockSpec((tm, D), lambda j, i, kk: (i, 0)), pl.BlockSpec((None, D, FB), lambda j, i, kk: (j, 0, 0)),
               pl.BlockSpec((None, tm, FB), lambda j, i, kk: (j, i, 0)), NN, (tm, FB))


def _mm_down(act, wdown):
    s = act.shape[1]
    tm = min(MM_TILE, s)
    return _mm("ffn_down", act, wdown, jax.ShapeDtypeStruct((s, D), BF16), (s // tm, 1, 4),
               pl.BlockSpec((None, tm, FB), lambda i, j, kk: (kk, i, 0)), pl.BlockSpec((None, FB, D), lambda i, j, kk: (kk, 0, 0)),
               pl.BlockSpec((tm, D), lambda i, j, kk: (i, 0)), NN, (tm, D))


def _mm_dact(df, wdown):
    s = df.shape[0]
    tm = min(MM_TILE, s)
    return _mm("ffn_dact", df, wdown, jax.ShapeDtypeStruct((4, s, FB), BF16), (4, s // tm, 1),
               pl.BlockSpec((tm, D), lambda j, i, kk: (i, 0)), pl.BlockSpec((None, FB, D), lambda j, i, kk: (j, 0, 0)),
               pl.BlockSpec((None, tm, FB), lambda j, i, kk: (j, i, 0)), NT, (tm, FB))


def _mm_dwdown(act, df):
    s = df.shape[0]
    tk = min(MM_TILE, s)
    return _mm("ffn_dwdown", act, df, jax.ShapeDtypeStruct((4, FB, D), BF16), (4, 1, s // tk),
               pl.BlockSpec((None, tk, FB), lambda j, i, kk: (j, kk, 0)), pl.BlockSpec((tk, D), lambda j, i, kk: (kk, 0)),
               pl.BlockSpec((None, FB, D), lambda j, i, kk: (j, 0, 0)), TN, (FB, D))


def _mm_dh2(dupre, wup):
    s = dupre.shape[1]
    tm = min(MM_TILE, s)
    return _mm("ffn_dh2", dupre, wup, jax.ShapeDtypeStruct((s, D), BF16), (s // tm, 1, NDEV),
               pl.BlockSpec((None, tm, FB), lambda i, j, kk: (kk, i, 0)), pl.BlockSpec((None, D, FB), lambda i, j, kk: (kk, 0, 0)),
               pl.BlockSpec((tm, D), lambda i, j, kk: (i, 0)), NT, (tm, D))


def _mm_dwup(h2, dupre):
    s = h2.shape[0]
    tk = min(MM_TILE, s)
    return _mm("ffn_dwup", h2, dupre, jax.ShapeDtypeStruct((NDEV, D, FB), BF16), (NDEV, 1, s // tk),
               pl.BlockSpec((tk, D), lambda j, i, kk: (kk, 0)), pl.BlockSpec((None, tk, FB), lambda j, i, kk: (j, kk, 0)),
               pl.BlockSpec((None, D, FB), lambda j, i, kk: (j, 0, 0)), TN, (D, FB))


def _rms(xv, w, out_dtype, name):
    s, d = xv.shape
    ts = min(ROW_TILE, s)

    def body(x_ref, w_ref, o_ref):
        for r0 in range(0, ts, NORM_CHUNK):
            rows = pl.ds(r0, NORM_CHUNK)
            xf = x_ref[rows, :].astype(F32)
            r = lax.rsqrt(jnp.mean(xf * xf, axis=-1, keepdims=True) + EPS)
            o_ref[rows, :] = (xf * r * w_ref[...]).astype(o_ref.dtype)

    return pl.pallas_call(
        body, name=name, grid=(s // ts,), out_shape=jax.ShapeDtypeStruct((s, d), out_dtype),
        in_specs=[pl.BlockSpec((ts, d), lambda i: (i, 0)), pl.BlockSpec((1, d), lambda i: (0, 0))],
        out_specs=pl.BlockSpec((ts, d), lambda i: (i, 0)), compiler_params=_cp("parallel"),
    )(xv, w)


def _add_rms(xv, mv, w, name, w_next=None):
    s, d = xv.shape
    ts = min(NORM_TILE, s)
    both = w_next is not None

    def body(*refs):
        x_ref, m_ref, w_ref = refs[:3]
        o_ref = refs[4] if both else refs[3]
        for r0 in range(0, ts, NORM_CHUNK):
            rows = pl.ds(r0, NORM_CHUNK)
            mf = m_ref[rows, :].astype(F32)
            r = lax.rsqrt(jnp.mean(mf * mf, axis=-1, keepdims=True) + EPS)
            y = x_ref[rows, :] + mf * r * w_ref[...]
            o_ref[rows, :] = y
            if both:
                r2 = lax.rsqrt(jnp.mean(y * y, axis=-1, keepdims=True) + EPS)
                refs[5][rows, :] = (y * r2 * refs[3][...]).astype(BF16)

    row = pl.BlockSpec((ts, d), lambda i: (i, 0))
    vec = pl.BlockSpec((1, d), lambda i: (0, 0))
    if both:
        return pl.pallas_call(
            body, name=name + "_rms", grid=(s // ts,),
            out_shape=(jax.ShapeDtypeStruct((s, d), F32), jax.ShapeDtypeStruct((s, d), BF16)),
            in_specs=[row, row, vec, vec], out_specs=(row, row), compiler_params=_cp("parallel"),
        )(xv, mv, w, w_next)
    return pl.pallas_call(
        body, name=name, grid=(s // ts,), out_shape=jax.ShapeDtypeStruct((s, d), F32),
        in_specs=[row, row, vec], out_specs=row, compiler_params=_cp("parallel"),
    )(xv, mv, w)


def _rms_bwd_math(xf, w, dy):
    r = lax.rsqrt(jnp.mean(xf * xf, axis=-1, keepdims=True) + EPS)
    xh = xf * r
    dxh = dy * w
    dx = r * (dxh - xh * jnp.mean(dxh * xh, axis=-1, keepdims=True))
    return dx, dy * xh


def _rms_bwd(xv, w, dy, dres, out_dtype, name, pending=None):
    s, d = xv.shape
    ts = min(NORM_TILE, s)
    with_res = dres is not None
    nin = 4 if with_res else 3
    nx = len(pending) if pending else 0

    def body(*refs):
        x_ref, w_ref, dy_ref = refs[:3]
        dres_ref = refs[3] if with_res else None
        dx_ref, dw_ref = refs[nin + nx:nin + nx + 2]
        acc = refs[nin + 2 * nx + 2]
        if nx:
            start, finish = _exchange_plan(refs[nin:nin + nx], refs[nin + nx + 2:nin + 2 * nx + 2], *refs[nin + 2 * nx + 3:])
            pl.when(pl.program_id(0) == 0)(start)
        acc[...] = jnp.zeros_like(acc)
        for r0 in range(0, ts, NORM_CHUNK):
            rows = pl.ds(r0, NORM_CHUNK)
            dx, dwt = _rms_bwd_math(x_ref[rows, :].astype(F32), w_ref[...], dy_ref[rows, :].astype(F32))
            if with_res:
                dx = dx + dres_ref[rows, :]
            dx_ref[rows, :] = dx.astype(dx_ref.dtype)
            acc[...] += dwt

        @pl.when(pl.program_id(0) == 0)
        def _():
            dw_ref[...] = jnp.zeros_like(dw_ref)

        dw_ref[...] += _rowsum(acc[...])
        if nx:
            pl.when(pl.program_id(0) == s // ts - 1)(finish)

    row = pl.BlockSpec((ts, d), lambda i: (i, 0))
    vec = pl.BlockSpec((1, d), lambda i: (0, 0))
    ins = [xv, w, dy] + ([dres] if with_res else []) + (pending or [])
    res = pl.pallas_call(
        body, name=name + "_exchange" if nx else name, grid=(s // ts,),
        out_shape=[jax.ShapeDtypeStruct((s, d), out_dtype), jax.ShapeDtypeStruct((1, d), F32)]
        + [jax.ShapeDtypeStruct(a.shape, a.dtype) for a in (pending or [])],
        in_specs=[row, vec, row] + ([row] if with_res else []) + [ANY] * nx, out_specs=[row, vec] + [ANY] * nx,
        scratch_shapes=[pltpu.VMEM((NORM_CHUNK, d), F32)] + (_comm_scratch(nx) if nx else []), compiler_params=_cp("arbitrary"),
    )(*ins)
    return (res[0], res[1], list(res[2:])) if nx else (res[0], res[1])


def _rms_bwd2(xa, wa, dya, dres, xb, wb):
    s, d = xa.shape
    ts = min(NORM_TILE, s)

    def body(xa_ref, wa_ref, dya_ref, dres_ref, xb_ref, wb_ref, da_ref, db_ref, dwa_ref, dwb_ref, acc):
        acc[...] = jnp.zeros_like(acc)
        for r0 in range(0, ts, NORM_CHUNK):
            rows = pl.ds(r0, NORM_CHUNK)
            da, dwt = _rms_bwd_math(xa_ref[rows, :].astype(F32), wa_ref[...], dya_ref[rows, :].astype(F32))
            da = da + dres_ref[rows, :]
            da_ref[rows, :] = da
            acc[0] += dwt
            db, dwt = _rms_bwd_math(xb_ref[rows, :].astype(F32), wb_ref[...], da)
            db_ref[rows, :] = db.astype(BF16)
            acc[1] += dwt

        @pl.when(pl.program_id(0) == 0)
        def _():
            dwa_ref[...] = jnp.zeros_like(dwa_ref)
            dwb_ref[...] = jnp.zeros_like(dwb_ref)

        dwa_ref[...] += _rowsum(acc[0])
        dwb_ref[...] += _rowsum(acc[1])

    row = pl.BlockSpec((ts, d), lambda i: (i, 0))
    vec = pl.BlockSpec((1, d), lambda i: (0, 0))
    return pl.pallas_call(
        body, name="rms_bwd2", grid=(s // ts,),
        out_shape=(jax.ShapeDtypeStruct((s, d), F32), jax.ShapeDtypeStruct((s, d), BF16), jax.ShapeDtypeStruct((1, d), F32),
                   jax.ShapeDtypeStruct((1, d), F32)),
        in_specs=[row, vec, row, row, row, vec], out_specs=(row, row, vec, vec),
        scratch_shapes=[pltpu.VMEM((2, NORM_CHUNK, d), F32)], compiler_params=_cp("arbitrary"),
    )(xa, wa, dya, dres, xb, wb)


def _loss_head(yv, tv):
    s, d = yv.shape
    ts = min(ROW_TILE, s)

    def body(y_ref, t_ref, l_ref, dy_ref):
        e = y_ref[...] - t_ref[...]
        dy_ref[...] = e * (1.0 / d)

        @pl.when(pl.program_id(0) == 0)
        def _():
            l_ref[...] = jnp.zeros_like(l_ref)

        tot = jnp.sum(jnp.sum(e * e, axis=1, keepdims=True), axis=0, keepdims=True)
        l_ref[...] += jnp.broadcast_to(tot * (0.5 / d), (8, LANE))

    row = pl.BlockSpec((ts, d), lambda i: (i, 0))
    return pl.pallas_call(
        body, name="loss_head", grid=(s // ts,),
        out_shape=(jax.ShapeDtypeStruct((8, LANE), F32), jax.ShapeDtypeStruct((s, d), F32)),
        in_specs=[row, row], out_specs=(pl.BlockSpec((8, LANE), lambda i: (0, 0)), row), compiler_params=_cp("arbitrary"),
    )(yv, tv)


def _rope(v, c, a, b):
    return v * c + pltpu.roll(v, LANE - 16, 1) * a + pltpu.roll(v, 16, 1) * b


def _rope_t(dv, c, a, b):
    return dv * c + pltpu.roll(dv * a, 16, 1) + pltpu.roll(dv * b, LANE - 16, 1)


def _mla_prep(proj, tabs, qnw, kvnw, wq, wkv):
    s = proj.shape[0]
    ts = min(ROW_TILE, s)
    tc, ta, tb = tabs

    def body(cq_ref, ckv_ref, kr_ref, c_ref, a_ref, b_ref, qnw_ref, kvnw_ref, wq_ref, wkv_ref, q_ref, k_ref, kv_ref):
        c, a, b = c_ref[...], a_ref[...], b_ref[...]
        cq = cq_ref[...].astype(F32)
        qn = (cq * lax.rsqrt(jnp.mean(cq * cq, axis=-1, keepdims=True) + EPS) * qnw_ref[...]).astype(BF16)
        ckv = ckv_ref[...].astype(F32)
        kvn = (ckv * lax.rsqrt(jnp.mean(ckv * ckv, axis=-1, keepdims=True) + EPS) * kvnw_ref[...]).astype(BF16)
        kr = _rope(kr_ref[...].astype(F32), c, a, b)
        lane = lax.broadcasted_iota(jnp.int32, (ts, LANE), 1)
        for h in range(HEADS):
            q_ref[h] = _rope(_dot(qn, wq_ref[h]), c, a, b).astype(BF16)
            kv = _dot(kvn, wkv_ref[h])
            kv_ref[h] = kv.astype(BF16)
            k_ref[h] = jnp.where(lane < NOPE, kv, kr).astype(BF16)

    tab = pl.BlockSpec((ts, LANE), lambda i: (i, 0))
    hd = pl.BlockSpec((HEADS, ts, LANE), lambda i: (0, i, 0))
    out = jax.ShapeDtypeStruct((HEADS, s, LANE), BF16)
    return pl.pallas_call(
        body, name="mla_prep", grid=(s // ts,), out_shape=(out, out, out),
        in_specs=[pl.BlockSpec((ts, QL), lambda i: (i, 0)), pl.BlockSpec((ts, LANE), lambda i: (i, 2)),
                  pl.BlockSpec((ts, LANE), lambda i: (i, 3)), tab, tab, tab,
                  pl.BlockSpec((1, QL), lambda i: (0, 0)), pl.BlockSpec((1, KVL), lambda i: (0, 0)),
                  pl.BlockSpec((HEADS, QL, LANE), lambda i: (0, 0, 0)), pl.BlockSpec((HEADS, KVL, LANE), lambda i: (0, 0, 0))],
        out_specs=(hd, hd, hd), compiler_params=_cp("parallel"),
    )(proj, proj, proj, tc, ta, tb, qnw, kvnw, wq, wkv)


def _mla_prep_bwd(proj, tabs, qnw, kvnw, wq, wkv, dq, dk, dv):
    s = proj.shape[0]
    ts = min(ROW_TILE, s)
    tc, ta, tb = tabs

    def body(cq_ref, ckv_ref, c_ref, a_ref, b_ref, qnw_ref, kvnw_ref, wq_ref, wkv_ref, dq_ref, dk_ref, dv_ref,
             d_ref, dwq_ref, dwkv_ref, dqnw_ref, dkvnw_ref):
        dcq_ref, dckv_ref, dkr_ref = d_ref.at[:, 0:QL], d_ref.at[:, QL:QL + KVL], d_ref.at[:, QL + KVL:QL + KVL + LANE]

        @pl.when(pl.program_id(0) == 0)
        def _():
            dwq_ref[...] = jnp.zeros_like(dwq_ref)
            dwkv_ref[...] = jnp.zeros_like(dwkv_ref)
            dqnw_ref[...] = jnp.zeros_like(dqnw_ref)
            dkvnw_ref[...] = jnp.zeros_like(dkvnw_ref)

        c, a, b = c_ref[...], a_ref[...], b_ref[...]
        cq = cq_ref[...].astype(F32)
        qn = (cq * lax.rsqrt(jnp.mean(cq * cq, axis=-1, keepdims=True) + EPS) * qnw_ref[...]).astype(BF16)
        ckv = ckv_ref[...].astype(F32)
        kvn = (ckv * lax.rsqrt(jnp.mean(ckv * ckv, axis=-1, keepdims=True) + EPS) * kvnw_ref[...]).astype(BF16)
        lane = lax.broadcasted_iota(jnp.int32, (ts, LANE), 1)
        dqn = jnp.zeros((ts, QL), F32)
        dkvn = jnp.zeros((ts, KVL), F32)
        dkr = jnp.zeros((ts, LANE), F32)
        for h in range(HEADS):
            dqh = _rope_t(dq_ref[h], c, a, b).astype(BF16)
            dwq_ref[h] += _dot(qn, dqh, TN)
            dqn += _dot(dqh, wq_ref[h], NT)
            dkh = dk_ref[h].astype(F32)
            dkvh = jnp.where(lane < NOPE, dkh, dv_ref[h].astype(F32)).astype(BF16)
            dkr += jnp.where(lane < NOPE, 0.0, dkh)
            dwkv_ref[h] += _dot(kvn, dkvh, TN)
            dkvn += _dot(dkvh, wkv_ref[h], NT)
        dkr_ref[...] = _rope_t(dkr, c, a, b).astype(BF16)
        dcq, dwt = _rms_bwd_math(cq, qnw_ref[...], dqn)
        dcq_ref[...] = dcq.astype(BF16)
        dqnw_ref[...] += _rowsum(dwt)
        dckv, dwt = _rms_bwd_math(ckv, kvnw_ref[...], dkvn)
        dckv_ref[...] = dckv.astype(BF16)
        dkvnw_ref[...] += _rowsum(dwt)

    tab = pl.BlockSpec((ts, LANE), lambda i: (i, 0))
    hd = pl.BlockSpec((HEADS, ts, LANE), lambda i: (0, i, 0))
    wq_spec = pl.BlockSpec((HEADS, QL, LANE), lambda i: (0, 0, 0))
    wkv_spec = pl.BlockSpec((HEADS, KVL, LANE), lambda i: (0, 0, 0))
    return pl.pallas_call(
        body, name="mla_prep_bwd", grid=(s // ts,),
        out_shape=(jax.ShapeDtypeStruct((s, QL + KVL + LANE), BF16),
                   jax.ShapeDtypeStruct((HEADS, QL, LANE), F32), jax.ShapeDtypeStruct((HEADS, KVL, LANE), F32),
                   jax.ShapeDtypeStruct((1, QL), F32), jax.ShapeDtypeStruct((1, KVL), F32)),
        in_specs=[pl.BlockSpec((ts, QL), lambda i: (i, 0)), pl.BlockSpec((ts, LANE), lambda i: (i, 2)), tab, tab, tab,
                  pl.BlockSpec((1, QL), lambda i: (0, 0)), pl.BlockSpec((1, KVL), lambda i: (0, 0)), wq_spec, wkv_spec, hd, hd, hd],
        out_specs=(pl.BlockSpec((ts, QL + KVL + LANE), lambda i: (i, 0)), wq_spec, wkv_spec,
                   pl.BlockSpec((1, QL), lambda i: (0, 0)), pl.BlockSpec((1, KVL), lambda i: (0, 0))),
        compiler_params=_cp("arbitrary"),
    )(proj, proj, tc, ta, tb, qnw, kvnw, wq, wkv, dq, dk, dv)


def _transpose_bf16(v):
    return v.astype(F32).T.astype(BF16)


def _flash_fwd(q, k, kv, prefetch=None):
    s = q.shape[1]
    t = min(ATT_TILE, s)
    tq = ATT_QW * t
    n = s // tq
    g = FWD_HEADS
    nx = len(prefetch) if prefetch else 0

    def body(*refs):
        q_ref, k_ref, kv_ref = refs[:3]
        o_ref, lse_ref = refs[3 + nx:5 + nx]
        kvt_sc = refs[5 + 2 * nx]
        step = pl.program_id(0) * n + pl.program_id(1)
        if nx:
            start, finish = _gather_plan(refs[3:3 + nx], refs[5 + nx:5 + 2 * nx], *refs[6 + 2 * nx:])
            pl.when(step == 0)(start)
        attend(q_ref, k_ref, kv_ref, o_ref, lse_ref, kvt_sc)
        if nx:
            pl.when(step == (HEADS // g) * n - 1)(finish)

    def attend(q_ref, k_ref, kv_ref, o_ref, lse_ref, kvt_sc):
        i = pl.program_id(1)

        @pl.when(i == 0)
        def _():
            ones_rows = lax.broadcasted_iota(jnp.int32, (LANE, s), 0) < NOPE
            for hh in range(g):
                kvt_sc[hh] = jnp.where(ones_rows, 1.0, kv_ref[hh].astype(F32).T).astype(BF16)

        qt = [(q_ref[hh].astype(F32) * (ATT_SCALE * LOG2E)).T.astype(BF16) for hh in range(g)]
        kpos = lax.broadcasted_iota(jnp.int32, (t, tq), 0)
        qpos = lax.broadcasted_iota(jnp.int32, (t, tq), 1) + i * tq

        def chunk(j, carry, diagonal):
            start = pl.multiple_of(j * t, t)
            scs = [_dot(k_ref[hh, pl.ds(start, t), :], qt[hh]) for hh in range(g)]
            soft = []
            for hh in range(g):
                sc = scs[hh]
                if diagonal:
                    sc = jnp.where(qpos >= kpos + start, sc, NEG)
                m_new = jnp.maximum(carry[hh][0], jnp.max(sc, axis=0, keepdims=True))
                soft.append((m_new, jnp.exp2(carry[hh][0] - m_new), jnp.exp2(sc - m_new).astype(BF16)))
            pvs = [_dot(kvt_sc[hh, :, pl.ds(start, t)], soft[hh][2]) for hh in range(g)]
            return tuple((soft[hh][0], soft[hh][1] * carry[hh][1] + pvs[hh]) for hh in range(g))

        init = tuple((jnp.full((1, tq), NEG, F32), jnp.zeros((LANE, tq), F32)) for _ in range(g))
        carry = lax.fori_loop(0, ATT_QW * i, lambda j, c: chunk(j, c, False), init)
        for d in range(ATT_QW):
            carry = chunk(ATT_QW * i + d, carry, True)
        for hh in range(g):
            m, acc = carry[hh]
            l = acc[0:1, :]
            o_ref[:, hh * LANE:(hh + 1) * LANE] = (acc / l).T.astype(BF16)
            lse_ref[hh] = m + jnp.log2(l)

    whole = pl.BlockSpec((g, s, LANE), lambda h, i: (h, 0, 0))
    res = pl.pallas_call(
        body, name="flash_fwd_gather" if nx else "flash_fwd", grid=(HEADS // g, n),
        out_shape=[jax.ShapeDtypeStruct((s, HEADS * LANE), BF16), jax.ShapeDtypeStruct((HEADS, 1, s), F32)]
        + [jax.ShapeDtypeStruct((NDEV,) + a.shape, a.dtype) for a in (prefetch or [])],
        in_specs=[pl.BlockSpec((g, tq, LANE), lambda h, i: (h, i, 0)), whole, whole] + [ANY] * nx,
        out_specs=[pl.BlockSpec((tq, g * LANE), lambda h, i: (i, h)), pl.BlockSpec((g, 1, tq), lambda h, i: (h, 0, i))] + [ANY] * nx,
        scratch_shapes=[pltpu.VMEM((g, LANE, s), BF16)] + (_comm_scratch(nx) if nx else []),
        compiler_params=_cp("arbitrary", "arbitrary"),
    )(q, k, kv, *(prefetch or []))
    return res[0], res[1], list(res[2:])


def _flash_bwd(q, k, kv, cat, dcat, lse, pending=None):
    s = q.shape[1]
    t = min(BWD_TILE, s)
    tq = BWD_QW * t
    n = s // t
    g = BWD_HEADS
    nx = len(pending) if pending else 0

    def body(*refs):
        ins, outs, scr = refs[:6], refs[6 + nx:9 + nx], refs[9 + 2 * nx:14 + 2 * nx]
        step = pl.program_id(0) * n + pl.program_id(1)
        if nx:
            start, finish = _exchange_plan(refs[6:6 + nx], refs[9 + nx:9 + 2 * nx], *refs[14 + 2 * nx:])
            pl.when(step == 0)(start)
        attend(*ins, *outs, *scr)
        if nx:
            pl.when(step == (HEADS // g) * n - 1)(finish)

    def attend(q_ref, k_ref, kv_ref, o_ref, do_ref, lse_ref, dq_ref, dk_ref, dv_ref, qt_sc, dot_sc, delta_sc, dqt_sc, qs_sc):
        j = pl.program_id(1)

        @pl.when(j == 0)
        def _():
            for hh in range(g):
                lanes = slice(hh * LANE, (hh + 1) * LANE)
                qf = q_ref[hh].astype(F32)
                qt_sc[hh] = (qf * (ATT_SCALE * LOG2E)).T.astype(BF16)
                qs_sc[hh] = (qf * ATT_SCALE).astype(BF16)
                dof = do_ref[:, lanes].astype(F32)
                dot_sc[hh] = dof.T.astype(BF16)
                delta_sc[hh] = _dot01(dof * o_ref[:, lanes].astype(F32), jnp.ones((8, LANE), F32), NT, x_first=False)
            dqt_sc[...] = jnp.zeros_like(dqt_sc)

        kjt = [_transpose_bf16(k_ref[hh]) for hh in range(g)]
        kpos = lax.broadcasted_iota(jnp.int32, (t, tq), 0) + j * t
        qpos = lax.broadcasted_iota(jnp.int32, (t, tq), 1)

        def chunk(i, carry, diagonal):
            start = pl.multiple_of(i * tq, tq)
            cols = pl.ds(start, tq)
            scs = [_dot(k_ref[hh], qt_sc[hh, :, cols]) for hh in range(g)]
            dps = [_dot(kv_ref[hh], dot_sc[hh, :, cols]) for hh in range(g)]
            pds = []
            for hh in range(g):
                p = jnp.exp2(scs[hh] - lse_ref[hh, :, cols])
                if diagonal:
                    p = jnp.where(qpos + start >= kpos, p, 0.0)
                ds = (p * (dps[hh] - delta_sc[hh, 0:1, cols])).astype(BF16)
                pds.append((p.astype(BF16), ds))
            out = []
            for hh in range(g):
                dk, dv = carry[hh]
                dv = dv + _dot(pds[hh][0], do_ref[pl.ds(start, tq), hh * LANE:(hh + 1) * LANE])
                dk = dk + _dot(pds[hh][1], qs_sc[hh, pl.ds(start, tq), :])
                dqt_sc[hh, :, cols] += _dot(kjt[hh], pds[hh][1])
                out.append((dk, dv))
            return tuple(out)

        zero = jnp.zeros((t, LANE), F32)
        first = lax.div(j, BWD_QW)
        carry = chunk(first, tuple((zero, zero) for _ in range(g)), True)
        carry = lax.fori_loop(first + 1, s // tq, lambda i, c: chunk(i, c, False), carry)
        for hh in range(g):
            dk_ref[hh] = carry[hh][0].astype(BF16)
            dv_ref[hh] = carry[hh][1].astype(BF16)

        @pl.when(j == n - 1)
        def _():
            for hh in range(g):
                dq_ref[hh] = (dqt_sc[hh] * ATT_SCALE).T

    whole = pl.BlockSpec((g, s, LANE), lambda h, j: (h, 0, 0))
    kspec = pl.BlockSpec((g, t, LANE), lambda h, j: (h, j, 0))
    ospec = pl.BlockSpec((s, g * LANE), lambda h, j: (0, h))
    res = pl.pallas_call(
        body, name="flash_bwd_exchange" if nx else "flash_bwd", grid=(HEADS // g, n),
        out_shape=[jax.ShapeDtypeStruct((HEADS, s, LANE), F32), jax.ShapeDtypeStruct((HEADS, s, LANE), BF16),
                   jax.ShapeDtypeStruct((HEADS, s, LANE), BF16)] + [jax.ShapeDtypeStruct(a.shape, a.dtype) for a in (pending or [])],
        in_specs=[whole, kspec, kspec, ospec, ospec, pl.BlockSpec((g, 1, s), lambda h, j: (h, 0, 0))] + [ANY] * nx,
        out_specs=[whole, kspec, kspec] + [ANY] * nx,
        scratch_shapes=[pltpu.VMEM((g, LANE, s), BF16), pltpu.VMEM((g, LANE, s), BF16), pltpu.VMEM((g, 8, s), F32),
                        pltpu.VMEM((g, LANE, s), F32), pltpu.VMEM((g, s, LANE), BF16)] + (_comm_scratch(nx) if nx else []),
        compiler_params=_cp("arbitrary", "arbitrary"),
    )(q, k, kv, cat, dcat, lse, *(pending or []))
    return res[0], res[1], res[2], list(res[3:])


def _conv3(ext, w_ref, ts):
    return (w_ref[0:1, :] * ext[pl.ds(HALO - 2, ts), :] + w_ref[1:2, :] * ext[pl.ds(HALO - 1, ts), :]
            + w_ref[2:3, :] * ext[pl.ds(HALO, ts), :])


def _conv3_rows(ext, w_ref, r):
    return (w_ref[0:1, :] * ext[pl.ds(HALO - 2 + r, ROW_CHUNK), :] + w_ref[1:2, :] * ext[pl.ds(HALO - 1 + r, ROW_CHUNK), :]
            + w_ref[2:3, :] * ext[pl.ds(HALO + r, ROW_CHUNK), :])


def _conv3_t(ext2, w_ref, ts):
    return (w_ref[0:1, :] * ext2[pl.ds(2, ts), :] + w_ref[1:2, :] * ext2[pl.ds(1, ts), :] + w_ref[2:3, :] * ext2[pl.ds(0, ts), :])


def _sconv_fwd(proj, w):
    s = proj.shape[0]
    ts = min(ROW_TILE, s)

    def body(b_ref, c_ref, h_ref, hc_ref, hh_ref, w_ref, o_ref, ext):
        i = pl.program_id(0)
        ext[0:HALO, :] = hc_ref[...].astype(F32) * hh_ref[...].astype(F32) * (i > 0).astype(F32)
        ext[HALO:HALO + ts, :] = c_ref[...].astype(F32) * h_ref[...].astype(F32)
        o_ref[...] = (b_ref[...].astype(F32) * _conv3(ext, w_ref, ts)).astype(BF16)

    def col(cb):
        return pl.BlockSpec((ts, SC), lambda i: (i, cb))

    def halo(cb):
        return pl.BlockSpec((HALO, SC), lambda i: (_prev_halo(i, ts), cb))

    return pl.pallas_call(
        body, name="sconv_fwd", grid=(s // ts,), out_shape=jax.ShapeDtypeStruct((s, SC), BF16),
        in_specs=[col(2), col(3), col(4), halo(3), halo(4), pl.BlockSpec((3, SC), lambda i: (0, 0))],
        out_specs=pl.BlockSpec((ts, SC), lambda i: (i, 0)), scratch_shapes=[pltpu.VMEM((ts + HALO, SC), F32)],
        compiler_params=_cp("parallel"),
    )(proj, proj, proj, proj, proj, w)


def _sconv_bwd(proj, dcat, w):
    s = proj.shape[0]
    ts = min(ROW_TILE, s)
    n = s // ts

    def body(b_ref, c_ref, h_ref, hc_ref, hh_ref, dy_ref, ndy_ref, nb_ref, w_ref, d_ref, dw_ref, ext, ext2):
        i = pl.program_id(0)
        db_ref, dc_ref, dh_ref = d_ref.at[:, 0:SC], d_ref.at[:, SC:2 * SC], d_ref.at[:, 2 * SC:3 * SC]

        @pl.when(i == 0)
        def _():
            dw_ref[...] = jnp.zeros_like(dw_ref)

        cv, hv, bv = c_ref[...].astype(F32), h_ref[...].astype(F32), b_ref[...].astype(F32)
        ext[0:HALO, :] = hc_ref[...].astype(F32) * hh_ref[...].astype(F32) * (i > 0).astype(F32)
        ext[HALO:HALO + ts, :] = cv * hv
        dy = dy_ref[...].astype(F32)
        db_ref[...] = (dy * _conv3(ext, w_ref, ts)).astype(BF16)
        dyb = dy * bv
        ext2[0:ts, :] = dyb
        ext2[ts:ts + HALO, :] = ndy_ref[...].astype(F32) * nb_ref[...].astype(F32) * (i < n - 1).astype(F32)
        dg = _conv3_t(ext2, w_ref, ts)
        dc_ref[...] = (dg * hv).astype(BF16)
        dh_ref[...] = (dg * cv).astype(BF16)
        for kk in range(3):
            dw_ref[kk:kk + 1, :] += _rowsum(dyb * ext[pl.ds(HALO - 2 + kk, ts), :])

    def col(cb):
        return pl.BlockSpec((ts, SC), lambda i: (i, cb))

    def halo(cb):
        return pl.BlockSpec((HALO, SC), lambda i: (_prev_halo(i, ts), cb))

    def nxt(cb):
        return pl.BlockSpec((HALO, SC), lambda i: (_next_halo(i, ts, n), cb))

    out = jax.ShapeDtypeStruct((s, 3 * SC), BF16)
    o0 = pl.BlockSpec((ts, 3 * SC), lambda i: (i, 0))
    return pl.pallas_call(
        body, name="sconv_bwd", grid=(n,), out_shape=(out, jax.ShapeDtypeStruct((3, SC), F32)),
        in_specs=[col(2), col(3), col(4), halo(3), halo(4), col(4), nxt(4), nxt(2), pl.BlockSpec((3, SC), lambda i: (0, 0))],
        out_specs=(o0, pl.BlockSpec((3, SC), lambda i: (0, 0))),
        scratch_shapes=[pltpu.VMEM((ts + HALO, SC), F32), pltpu.VMEM((ts + HALO, SC), F32)], compiler_params=_cp("arbitrary"),
    )(proj, proj, proj, proj, proj, dcat, dcat, proj, w)


def _ffn_stage(ext, u_ref, halo_ref, i, ts):
    ext[0:HALO, :] = halo_ref[...].astype(F32) * (i > 0).astype(F32)
    ext[HALO:HALO + ts, :] = u_ref[...].astype(F32)


def _ffn_specs(ts):
    cur = pl.BlockSpec((2, None, ts, FB), lambda j, i: (0, j, i, 0))
    halo = pl.BlockSpec((2, None, HALO, FB), lambda j, i: (0, j, _prev_halo(i, ts), 0))
    w = pl.BlockSpec((2, None, 3, FB), lambda j, i: (0, j, 0, 0))
    b = pl.BlockSpec((2, None, 1, FB), lambda j, i: (0, j, 0, 0))
    return cur, halo, w, b


def _ffn_act(upre, fcw, fcb):
    s = upre.shape[1]
    ts = min(ROW_TILE, s)

    def body(u_ref, halo_ref, w_ref, b_ref, o_ref, ext_g, ext_u):
        i = pl.program_id(1)
        _ffn_stage(ext_g, u_ref.at[0], halo_ref.at[0], i, ts)
        _ffn_stage(ext_u, u_ref.at[1], halo_ref.at[1], i, ts)
        for r in range(0, ts, ROW_CHUNK):
            gate = b_ref[0] + _conv3_rows(ext_g, w_ref.at[0], r)
            up = b_ref[1] + _conv3_rows(ext_u, w_ref.at[1], r)
            o_ref[pl.ds(r, ROW_CHUNK), :] = (gate * _sig(gate) * up).astype(BF16)

    cur, halo, w, b = _ffn_specs(ts)
    u4 = upre.reshape(2, 4, s, FB)
    return pl.pallas_call(
        body, name="ffn_act", grid=(4, s // ts), out_shape=jax.ShapeDtypeStruct((4, s, FB), BF16),
        in_specs=[cur, halo, w, b], out_specs=pl.BlockSpec((None, ts, FB), lambda j, i: (j, i, 0)),
        scratch_shapes=[pltpu.VMEM((ts + HALO, FB), F32), pltpu.VMEM((ts + HALO, FB), F32)], compiler_params=_cp("parallel", "parallel"),
    )(u4, u4, fcw.reshape(2, 4, 3, FB), fcb.reshape(2, 4, 1, FB))


def _ffn_bwd(upre, dact, fcw, fcb):
    s = upre.shape[1]
    ts = min(ROW_TILE, s)
    n = s // ts
    te = ts + HALO

    def body(u_ref, halo_ref, nxt_ref, w_ref, b_ref, da_ref, nda_ref, dup_ref, db_ref, dw_ref, ext_g, ext_u, ext_da, du_g, du_u, acc):
        i = pl.program_id(1)

        @pl.when(i == 0)
        def _():
            db_ref[...] = jnp.zeros_like(db_ref)
            dw_ref[...] = jnp.zeros_like(dw_ref)

        more = (i < n - 1).astype(F32)
        for idx, ext in ((0, ext_g), (1, ext_u)):
            _ffn_stage(ext, u_ref.at[idx], halo_ref.at[idx], i, ts)
            ext[HALO + ts:HALO + te, :] = nxt_ref[idx].astype(F32) * more
        ext_da[0:ts, :] = da_ref[...].astype(F32)
        ext_da[ts:te, :] = nda_ref[...].astype(F32) * more
        acc[...] = jnp.zeros_like(acc)
        for r in range(0, te, ROW_CHUNK):
            rows = pl.ds(r, ROW_CHUNK)
            taps = [[ext[pl.ds(HALO - 2 + kk + r, ROW_CHUNK), :] for kk in range(3)] for ext in (ext_g, ext_u)]
            gate = b_ref[0] + w_ref[0, 0:1, :] * taps[0][0] + w_ref[0, 1:2, :] * taps[0][1] + w_ref[0, 2:3, :] * taps[0][2]
            up = b_ref[1] + w_ref[1, 0:1, :] * taps[1][0] + w_ref[1, 1:2, :] * taps[1][1] + w_ref[1, 2:3, :] * taps[1][2]
            sg = _sig(gate)
            da = ext_da[rows, :]
            dgate = da * up * sg * (1.0 + gate * (1.0 - sg))
            dup = da * gate * sg
            du_g[rows, :] = dgate
            du_u[rows, :] = dup
            if r < ts:
                acc[0] += dgate
                acc[1] += dup
                for kk in range(3):
                    acc[2 + kk] += dgate * taps[0][kk]
                    acc[5 + kk] += dup * taps[1][kk]
        for r in range(0, ts, ROW_CHUNK):
            for idx, du in ((0, du_g), (1, du_u)):
                dupre = (w_ref[idx, 2:3, :] * du[pl.ds(r, ROW_CHUNK), :] + w_ref[idx, 1:2, :] * du[pl.ds(r + 1, ROW_CHUNK), :]
                         + w_ref[idx, 0:1, :] * du[pl.ds(r + 2, ROW_CHUNK), :])
                dup_ref[idx, pl.ds(r, ROW_CHUNK), :] = dupre.astype(BF16)
        for idx in range(2):
            db_ref[idx] += _rowsum(acc[idx])
            for kk in range(3):
                dw_ref[idx, kk:kk + 1, :] += _rowsum(acc[2 + 3 * idx + kk])

    cur, halo, w, b = _ffn_specs(ts)
    nxt = pl.BlockSpec((2, None, HALO, FB), lambda j, i: (0, j, _next_halo(i, ts, n), 0))
    u4 = upre.reshape(2, 4, s, FB)
    dupre, db, dw = pl.pallas_call(
        body, name="ffn_bwd", grid=(4, n),
        out_shape=(jax.ShapeDtypeStruct((2, 4, s, FB), BF16), jax.ShapeDtypeStruct((2, 4, 1, FB), F32),
                   jax.ShapeDtypeStruct((2, 4, 3, FB), F32)),
        in_specs=[cur, halo, nxt, w, b, pl.BlockSpec((None, ts, FB), lambda j, i: (j, i, 0)),
                  pl.BlockSpec((None, HALO, FB), lambda j, i: (j, _next_halo(i, ts, n), 0))],
        out_specs=(cur, b, w),
        scratch_shapes=[pltpu.VMEM((te + HALO, FB), F32), pltpu.VMEM((te + HALO, FB), F32), pltpu.VMEM((te, FB), F32),
                        pltpu.VMEM((te, FB), F32), pltpu.VMEM((te, FB), F32), pltpu.VMEM((8, ROW_CHUNK, FB), F32)],
        compiler_params=_cp("parallel", "arbitrary"),
    )(u4, u4, u4, fcw.reshape(2, 4, 3, FB), fcb.reshape(2, 4, 1, FB), dact, dact)
    return dupre.reshape(NDEV, s, FB), db.reshape(NDEV, 1, FB), dw.reshape(NDEV, 3, FB)


def _softplus(v):
    e = jnp.exp(-jnp.abs(v))
    return jnp.maximum(v, 0.0) + jnp.where(e < 1e-4, e * (1.0 - 0.5 * e), jnp.log(1.0 + e))


def _ssd_consts():
    L = SSD_L
    r = lax.broadcasted_iota(jnp.int32, (L, L), 0)
    c = lax.broadcasted_iota(jnp.int32, (L, L), 1)
    tri = r >= c
    er = lax.broadcasted_iota(jnp.int32, (LANE, SSD_DIM), 0)
    ec = lax.broadcasted_iota(jnp.int32, (LANE, SSD_DIM), 1)
    expand = ((ec >= er * 64) & (ec < er * 64 + 64)).astype(F32)
    return tri, expand


def _ssd_conv4(ext, cw_ref, cb_ref):
    L = SSD_L
    pre = cb_ref[...] + cw_ref[0:1, :] * ext[pl.ds(HALO - 3, L), :]
    for kk in range(1, 4):
        pre = pre + cw_ref[kk:kk + 1, :] * ext[pl.ds(HALO - 3 + kk, L), :]
    return pre


def _ssd_common(xbc_ref, halo_ref, dt_ref, cw_ref, cb_ref, dtb_ref, alog_ref, ext, first):
    L = SSD_L
    tri, expand = _ssd_consts()
    ext[0:HALO, :] = halo_ref[...].astype(F32) * (1.0 - first.astype(F32))
    ext[HALO:HALO + L, :] = xbc_ref[...].astype(F32)
    pre = _ssd_conv4(ext, cw_ref, cb_ref)
    sg = _sig(pre)
    act = pre * sg
    lane = lax.broadcasted_iota(jnp.int32, (1, LANE), 1)
    m4 = lane < SSD_H
    raw = dt_ref[...].astype(F32) + dtb_ref[...]
    dtv = jnp.where(m4, _softplus(raw), 0.0)
    av = jnp.where(m4, -jnp.exp(alog_ref[...]), 0.0)
    adt = dtv * av
    acs = _dot01(adt, tri, x_first=False)
    acs_b = _dot01(acs, expand)
    dt_b = _dot01(dtv, expand)
    return dict(tri=tri, expand=expand, pre=pre, sg=sg, act=act, raw=raw, dtv=dtv, av=av, m4=m4, acs=acs, acs_b=acs_b,
                dt_b=dt_b, lane=lane)


def _head_terms(cm, h):
    L = SSD_L
    acs, tri = cm["acs"], cm["tri"]
    lane_l = lax.broadcasted_iota(jnp.int32, (L, LANE), 1)
    sub_l = lax.broadcasted_iota(jnp.int32, (LANE, L), 0)
    col = jnp.sum(jnp.where(lane_l == h, acs, 0.0), axis=1, keepdims=True)
    row = jnp.sum(jnp.where(sub_l == h, acs.T, 0.0), axis=0, keepdims=True)
    dec = jnp.where(tri, jnp.exp(jnp.where(tri, col - row, NEG)), 0.0)
    rowi = lax.broadcasted_iota(jnp.int32, (L, 1), 0)
    last = jnp.sum(jnp.where(rowi == L - 1, col, 0.0), axis=0, keepdims=True)
    dte = jnp.exp(last - col)
    return col, dec, last, dte


def _ssd_fwd(proj, cw, cb, dtb, alog, dvec, nw):
    s = proj.shape[0]
    L = SSD_L
    nc = s // L

    def body(z_ref, xbc_ref, halo_ref, dt_ref, cw_ref, cb_ref, dtb_ref, alog_ref, d_ref, nw_ref, y_ref, ypre_ref, st_ref, ext, state):
        i = pl.program_id(0)

        @pl.when(i == 0)
        def _():
            state[...] = jnp.zeros_like(state)

        cm = _ssd_common(xbc_ref, halo_ref, dt_ref, cw_ref, cb_ref, dtb_ref, alog_ref, ext, i == 0)
        act = cm["act"]
        xs = act[:, 0:256]
        bm = (act[:, 256:384], act[:, 384:512])
        cmat = (act[:, 512:640].astype(BF16), act[:, 640:768].astype(BF16))
        xdt = xs * cm["dt_b"]
        prev = state[...]
        st_ref[...] = prev
        prev_bf = prev.astype(BF16)
        gm = [_dot(cmat[g], bm[g].astype(BF16), NT) for g in range(2)]
        lane2 = lax.broadcasted_iota(jnp.int32, (1, SSD_DIM), 1)
        rows2 = lax.broadcasted_iota(jnp.int32, (SSD_DIM, 1), 0)
        ydiag = jnp.zeros((L, SSD_DIM), F32)
        contrib = jnp.zeros((SSD_DIM, LANE), F32)
        cd_rows = jnp.zeros((SSD_DIM, 1), F32)
        for h in range(SSD_H):
            g = h // 2
            col, dec, last, dte = _head_terms(cm, h)
            mh = (lane2 >= 64 * h) & (lane2 < 64 * h + 64)
            xm = jnp.where(mh, xdt, 0.0).astype(BF16)
            ydiag += _dot((gm[g] * dec).astype(BF16), xm)
            contrib += _dot(xm, (bm[g] * dte).astype(BF16), TN)
            cd_rows += jnp.where((rows2 >= 64 * h) & (rows2 < 64 * h + 64), jnp.exp(last), 0.0)
        yo = jnp.where(lane2 < 128, _dot(cmat[0], prev_bf, NT), _dot(cmat[1], prev_bf, NT))
        y = ydiag + yo * jnp.exp(cm["acs_b"]) + xs * d_ref[...]
        state[...] = prev * cd_rows + contrib
        ypre_ref[...] = y
        zz = z_ref[...].astype(F32)
        gt = y * zz * _sig(zz)
        y_ref[...] = (gt * lax.rsqrt(jnp.mean(gt * gt, axis=-1, keepdims=True) + EPS) * nw_ref[...]).astype(BF16)

    def vec(w):
        return pl.BlockSpec((1, w), lambda i: (0, 0))

    return pl.pallas_call(
        body, name="ssd_fwd", grid=(nc,),
        out_shape=(jax.ShapeDtypeStruct((s, SSD_DIM), BF16), jax.ShapeDtypeStruct((s, SSD_DIM), F32),
                   jax.ShapeDtypeStruct((nc, SSD_DIM, LANE), F32)),
        in_specs=[pl.BlockSpec((L, SSD_DIM), lambda i: (i, 5)), pl.BlockSpec((L, SSD_CONV), lambda i: (i, 2)),
                  pl.BlockSpec((HALO, SSD_CONV), lambda i: (_prev_halo(i, L), 2)), pl.BlockSpec((L, LANE), lambda i: (i, 18)),
                  pl.BlockSpec((4, SSD_CONV), lambda i: (0, 0)), vec(SSD_CONV), vec(LANE), vec(LANE), vec(SSD_DIM), vec(SSD_DIM)],
        out_specs=(pl.BlockSpec((L, SSD_DIM), lambda i: (i, 0)), pl.BlockSpec((L, SSD_DIM), lambda i: (i, 0)),
                   pl.BlockSpec((None, SSD_DIM, LANE), lambda i: (i, 0, 0))),
        scratch_shapes=[pltpu.VMEM((L + HALO, SSD_CONV), F32), pltpu.VMEM((SSD_DIM, LANE), F32)], compiler_params=_cp("arbitrary"),
    )(proj, proj, proj, proj, cw, cb, dtb, alog, dvec, nw)


def _ssd_bwd(proj, dcat, ypre, states, cw, cb, dtb, alog, dvec, nw):
    s = proj.shape[0]
    L = SSD_L
    nc = s // L

    def body(z_ref, xbc_ref, halo_ref, dt_ref, dy_ref, ypre_ref, st_ref, cw_ref, cb_ref, dtb_ref, alog_ref, d_ref, nw_ref,
             din_ref, dcw_ref, dcb_ref, ddtb_ref, dalog_ref, dd_ref, dnw_ref, ext, ext2, carry, dstate, ddl):
        i = pl.program_id(0)
        dz_ref, dxbc_ref = din_ref.at[:, 0:SSD_DIM], din_ref.at[:, SSD_DIM:SSD_DIM + SSD_CONV]
        ddt_ref = din_ref.at[:, SSD_DIM + SSD_CONV:SSD_DIM + SSD_CONV + LANE]
        r = nc - 1 - i

        @pl.when(i == 0)
        def _():
            for ref in (dcw_ref, dcb_ref, ddtb_ref, dalog_ref, dd_ref, dnw_ref, carry, dstate, ddl):
                ref[...] = jnp.zeros_like(ref)

        cm = _ssd_common(xbc_ref, halo_ref, dt_ref, cw_ref, cb_ref, dtb_ref, alog_ref, ext, r == 0)
        tri, expand, act = cm["tri"], cm["expand"], cm["act"]
        xs = act[:, 0:256]
        bm = (act[:, 256:384], act[:, 384:512])
        cmat = (act[:, 512:640], act[:, 640:768])
        bm_bf = [v.astype(BF16) for v in bm]
        cm_bf = [v.astype(BF16) for v in cmat]
        dt_b = cm["dt_b"]
        xdt = xs * dt_b
        xdt_bf = xdt.astype(BF16)
        ea_b = jnp.exp(cm["acs_b"])
        prev = st_ref[...]
        prev_bf = prev.astype(BF16)
        lane2 = lax.broadcasted_iota(jnp.int32, (1, SSD_DIM), 1)
        rows2 = lax.broadcasted_iota(jnp.int32, (SSD_DIM, 1), 0)
        lane_l = lax.broadcasted_iota(jnp.int32, (L, LANE), 1)
        rowi = lax.broadcasted_iota(jnp.int32, (L, 1), 0)

        y = ypre_ref[...]
        zz = z_ref[...].astype(F32)
        sz = _sig(zz)
        gt = y * zz * sz
        dgt, dwt = _rms_bwd_math(gt, nw_ref[...], dy_ref[...].astype(F32))
        dnw_ref[...] += _rowsum(dwt)
        dy = dgt * zz * sz
        dz_ref[...] = (dgt * y * sz * (1.0 + zz * (1.0 - sz))).astype(BF16)

        ddl[0:1, :] += _rowsum(dy * xs)
        dxs = dy * d_ref[...]

        yo = jnp.where(lane2 < 128, _dot(cm_bf[0], prev_bf, NT), _dot(cm_bf[1], prev_bf, NT))
        dacs_b = dy * yo * ea_b
        dyo = dy * ea_b
        dyo_g = (jnp.where(lane2 < 128, dyo, 0.0).astype(BF16), jnp.where(lane2 >= 128, dyo, 0.0).astype(BF16))
        dc = [_dot(dyo_g[g], prev_bf) for g in range(2)]
        dprev = _dot(dyo_g[0], cm_bf[0], TN) + _dot(dyo_g[1], cm_bf[1], TN)

        gm = [_dot(cm_bf[g], bm_bf[g], NT) for g in range(2)]
        dgm = [jnp.zeros((L, L), F32), jnp.zeros((L, L), F32)]
        db = [jnp.zeros((L, LANE), F32), jnp.zeros((L, LANE), F32)]
        dxdt = jnp.zeros((L, SSD_DIM), F32)
        dacs = jnp.zeros((L, LANE), F32)
        dlast = jnp.zeros((1, LANE), F32)
        cd_rows = jnp.zeros((SSD_DIM, 1), F32)
        dst = dstate[...]
        dst_bf = dst.astype(BF16)
        dsp = dst * prev
        ones = jnp.ones((L, LANE), F32)
        for h in range(SSD_H):
            g = h // 2
            col, dec, last, dte = _head_terms(cm, h)
            mh = (lane2 >= 64 * h) & (lane2 < 64 * h + 64)
            rh = (rows2 >= 64 * h) & (rows2 < 64 * h + 64)
            sc = gm[g] * dec
            xm = jnp.where(mh, xdt, 0.0).astype(BF16)
            dym = jnp.where(mh, dy, 0.0).astype(BF16)
            dsc = _dot(dym, xdt_bf, NT)
            dxdt += _dot(sc.astype(BF16), dym, TN)
            dgm[g] += dsc * dec
            dd = dsc * sc
            rs = jnp.sum(dd, axis=1, keepdims=True)
            cs = _dot01(dd, ones, TN)
            dacs += jnp.where(lane_l == h, rs - cs, 0.0)
            bd = (bm[g] * dte).astype(BF16)
            dxdt += jnp.where(mh, _dot(bd, dst_bf, NT), 0.0)
            dbd = _dot(xm, dst_bf)
            db[g] += dbd * dte
            tt = jnp.sum(dbd * bm[g], axis=1, keepdims=True) * dte
            dacs += jnp.where(lane_l == h, -tt, 0.0)
            cdh = jnp.exp(last)
            dcd = jnp.sum(jnp.sum(jnp.where(rh, dsp, 0.0), axis=1, keepdims=True), axis=0, keepdims=True)
            dlast += jnp.where(cm["lane"] == h, jnp.sum(tt, axis=0, keepdims=True) + dcd * cdh, 0.0)
            cd_rows += jnp.where(rh, cdh, 0.0)
        dacs += jnp.where(rowi == L - 1, dlast, 0.0)
        dacs += _dot01(dacs_b, expand, NT)
        dstate[...] = dprev + dst * cd_rows

        for g in range(2):
            dgb = dgm[g].astype(BF16)
            dc[g] += _dot(dgb, bm_bf[g])
            db[g] += _dot(dgb, cm_bf[g], TN)

        dadt = _dot01(dacs, tri, TN, x_first=False)
        ddtv = dadt * cm["av"] + _dot01(dxdt * xs, expand, NT)
        dalog_ref[...] += _rowsum(dadt * cm["dtv"]) * cm["av"]
        dxs += dxdt * dt_b
        draw = jnp.where(cm["m4"], ddtv * _sig(cm["raw"]), 0.0)
        ddtb_ref[...] += _rowsum(draw)
        ddt_ref[...] = draw.astype(BF16)

        dact = jnp.concatenate([dxs, db[0], db[1], dc[0], dc[1]], axis=1)
        sg, pre = cm["sg"], cm["pre"]
        dpre = dact * sg * (1.0 + pre * (1.0 - sg))
        dcb_ref[...] += _rowsum(dpre)
        for kk in range(4):
            dcw_ref[kk:kk + 1, :] += _rowsum(dpre * ext[pl.ds(HALO - 3 + kk, L), :])
        ext2[0:L, :] = dpre
        ext2[L:L + HALO, :] = carry[...]
        dx = cw_ref[3:4, :] * ext2[pl.ds(0, L), :]
        for kk in range(3):
            dx = dx + cw_ref[kk:kk + 1, :] * ext2[pl.ds(3 - kk, L), :]
        dxbc_ref[...] = dx.astype(BF16)
        carry[...] = dpre[0:HALO, :]

        @pl.when(i == nc - 1)
        def _():
            dd_ref[...] = _dot01(ddl[...], expand, NT)

    def vec(w):
        return pl.BlockSpec((1, w), lambda i: (0, 0))

    def rv(i):
        return nc - 1 - i

    return pl.pallas_call(
        body, name="ssd_bwd", grid=(nc,),
        out_shape=(jax.ShapeDtypeStruct((s, SSD_DIM + SSD_CONV + LANE), BF16),
                   jax.ShapeDtypeStruct((4, SSD_CONV), F32), jax.ShapeDtypeStruct((1, SSD_CONV), F32), jax.ShapeDtypeStruct((1, LANE), F32),
                   jax.ShapeDtypeStruct((1, LANE), F32), jax.ShapeDtypeStruct((8, LANE), F32), jax.ShapeDtypeStruct((1, SSD_DIM), F32)),
        in_specs=[pl.BlockSpec((L, SSD_DIM), lambda i: (rv(i), 5)), pl.BlockSpec((L, SSD_CONV), lambda i: (rv(i), 2)),
                  pl.BlockSpec((HALO, SSD_CONV), lambda i: (_prev_halo(rv(i), L), 2)), pl.BlockSpec((L, LANE), lambda i: (rv(i), 18)),
                  pl.BlockSpec((L, SSD_DIM), lambda i: (rv(i), 5)), pl.BlockSpec((L, SSD_DIM), lambda i: (rv(i), 0)),
                  pl.BlockSpec((None, SSD_DIM, LANE), lambda i: (rv(i), 0, 0)),
                  pl.BlockSpec((4, SSD_CONV), lambda i: (0, 0)), vec(SSD_CONV), vec(LANE), vec(LANE), vec(SSD_DIM), vec(SSD_DIM)],
        out_specs=(pl.BlockSpec((L, SSD_DIM + SSD_CONV + LANE), lambda i: (rv(i), 0)), pl.BlockSpec((4, SSD_CONV), lambda i: (0, 0)), vec(SSD_CONV),
                   vec(LANE), vec(LANE), pl.BlockSpec((8, LANE), lambda i: (0, 0)), vec(SSD_DIM)),
        scratch_shapes=[pltpu.VMEM((L + HALO, SSD_CONV), F32), pltpu.VMEM((L + HALO, SSD_CONV), F32), pltpu.VMEM((HALO, SSD_CONV), F32),
                        pltpu.VMEM((SSD_DIM, LANE), F32), pltpu.VMEM((8, SSD_DIM), F32)],
        compiler_params=_cp("arbitrary"),
    )(proj, proj, proj, proj, dcat, ypre, states, cw, cb, dtb, alog, dvec, nw)


def _adamw(parts, w, m, v, name):
    nl, r, c = w.shape
    tr = r
    for cand in (256, 128, 64, 32, 16, 8):
        if r % cand == 0 and (cand * c * 4) <= 2 * 1024 * 1024:
            tr = cand
            break
    c1 = 1.0 - B1 ** STEP
    c2 = 1.0 - B2 ** STEP

    def body(p_ref, w_ref, m_ref, v_ref, g_ref, d_ref, nm_ref, nv_ref):
        g = p_ref[0].astype(F32)
        for dev in range(1, NDEV):
            g = g + p_ref[dev].astype(F32)
        mn = B1 * m_ref[...] + (1.0 - B1) * g
        vn = B2 * v_ref[...] + (1.0 - B2) * (g * g)
        g_ref[...] = g
        nm_ref[...] = mn
        nv_ref[...] = vn
        d_ref[...] = -LR * ((mn / c1) / (jnp.sqrt(vn / c2) + AEPS) + WD * w_ref[...])

    blk = pl.BlockSpec((None, tr, c), lambda l, i: (l, i, 0))
    out = jax.ShapeDtypeStruct((nl, r, c), F32)
    return pl.pallas_call(
        body, name=name, grid=(nl, r // tr), out_shape=(out, out, out, out),
        in_specs=[pl.BlockSpec((NDEV, None, tr, c), lambda l, i: (0, l, i, 0)), blk, blk, blk], out_specs=(blk, blk, blk, blk),
        compiler_params=_cp("parallel", "parallel"),
    )(parts, w, m, v)


def _pad_win(w):
    z = lambda n: jnp.zeros((w.shape[0], n), w.dtype)
    return jnp.concatenate([w[:, :384], z(64), w[:, 384:416], z(32), w[:, 416:], z(124)], axis=1)


def _unpad_win(g_mla, g_sc, g_ssd):
    return jnp.concatenate([g_mla[:, :384], g_mla[:, 448:480], g_sc, g_ssd[:, :1028]], axis=1)


def _pad_wout(w):
    att = jnp.pad(w[:512].reshape(HEADS, 64, D), ((0, 0), (64, 0), (0, 0))).reshape(HEADS * LANE, D)
    return jnp.concatenate([att, w[512:]], axis=0)


def _unpad_wout(g):
    att = g[:HEADS * LANE].reshape(HEADS, LANE, D)[:, 64:, :].reshape(512, D)
    return jnp.concatenate([att, g[HEADS * LANE:]], axis=0)


def _lanes(v, n=LANE):
    return jnp.pad(v, (0, n - v.shape[0])).reshape(1, n)


def _prep_ffn(big):
    return {"wout": _pad_wout(big["w_out"].reshape(1024, D)), "wup": big["ffn_w_up"], "fcw": big["ffn_conv_w"].astype(F32),
            "wdown": big["ffn_w_down"].reshape(4, FB, D)}


def _prep_layer(big, small, l):
    p = _prep_ffn(big) if "w_out" in big else {}
    p["win"] = _pad_win(big["w_in"].reshape(D, 2212))
    p["wq"] = jnp.pad(big["mla_w_q_up"], ((0, 0), (0, 0), (0, LANE - 96)))
    p["wkv"] = big["mla_w_kv_up"]
    p["scw"] = big["sc_conv_w"].astype(F32).transpose(1, 0, 2).reshape(3, SC)
    p["ssdcw"] = big["ssd_conv_w"].astype(F32).transpose(1, 0, 2).reshape(4, SSD_CONV)
    for nm in ("norm_mix_pre", "norm_mix_post", "norm_ffn_pre", "norm_ffn_post", "mla_q_norm", "mla_kv_norm", "ssd_conv_b", "ssd_norm"):
        p[nm] = small[nm][l].reshape(1, -1)
    p["dtb"] = _lanes(small["ssd_dt_bias"][l])
    p["alog"] = _lanes(small["ssd_a_log"][l])
    p["dvec"] = jnp.repeat(small["ssd_d"][l], 64).reshape(1, SSD_DIM)
    p["fcb"] = small["ffn_conv_b"][l].reshape(NDEV, 1, FB)
    return p


def _rope_tables(positions):
    inv_freq = 1.0 / (ROPE_THETA ** (jnp.arange(0, ROPE, 2, dtype=F32) / ROPE))
    ang = positions.astype(F32)[:, None] * inv_freq
    cos, sin = jnp.cos(ang), jnp.sin(ang)
    s = positions.shape[0]
    z = lambda n: jnp.zeros((s, n), F32)
    tc = jnp.concatenate([jnp.ones((s, 64), F32), cos, cos, z(32)], axis=1)
    ta = jnp.concatenate([z(64), -sin, z(48)], axis=1)
    tb = jnp.concatenate([z(80), sin, z(32)], axis=1)
    return tc, ta, tb


def _layer_fwd(xv, p, tabs, prefetch=None, prep_rest=None, h=None, next_norm=None):
    if h is None:
        h = _rms(xv, p["norm_mix_pre"], BF16, "rms_pre")
    proj = _mm_rows("in_proj", h, p["win"], BF16, NN)
    q, k, kv = _mla_prep(proj, tabs, p["mla_q_norm"], p["mla_kv_norm"], p["wq"], p["wkv"])
    o, lse, gathered = _flash_fwd(q, k, kv, prefetch)
    if prep_rest is not None:
        p = {**p, **prep_rest(gathered)}
    yconv = _sconv_fwd(proj, p["scw"])
    yssd, ypre, states = _ssd_fwd(proj, p["ssdcw"], p["ssd_conv_b"], p["dtb"], p["alog"], p["dvec"], p["ssd_norm"])
    cat = jnp.concatenate([o, yconv, yssd], axis=1)
    mixed = _mm_rows("out_proj", cat, p["wout"], BF16, NN)
    x1, h2 = _add_rms(xv, mixed, p["norm_mix_post"], "add_rms", p["norm_ffn_pre"])
    upre = _mm_up(h2, p["wup"])
    act = _ffn_act(upre, p["fcw"], p["fcb"])
    f = _mm_down(act, p["wdown"])
    x2 = _add_rms(x1, f, p["norm_ffn_post"], "add_rms", next_norm)
    saved = dict(x=xv, h=h, proj=proj, q=q, k=k, kv=kv, lse=lse, ypre=ypre, states=states, cat=cat, mixed=mixed, x1=x1, h2=h2,
                 upre=upre, act=act, f=f)
    return x2, saved, p, gathered


def _pack_grads(grads, group_ids):
    return [_group_pack(GROUPS[gi], lambda n: grads[n].reshape((NDEV,) + _rows2(n, True)), (NDEV,)) for gi in group_ids]


def _layer_bwd(dx2, sv, p, tabs, exchange=False, pending=None, head=None, below=None):
    df, g_nfpo = head if head is not None else _rms_bwd(sv["f"], p["norm_ffn_post"], dx2, None, BF16, "rms_bwd_post")
    dact = _mm_dact(df, p["wdown"])
    g_wdown = _mm_dwdown(sv["act"], df)
    dupre, g_fcb, g_fcw = _ffn_bwd(sv["upre"], dact, p["fcw"], p["fcb"])
    dh2 = _mm_dh2(dupre, p["wup"])
    g_wup = _mm_dwup(sv["h2"], dupre)
    dx1, dmixed, g_nfp, g_nmpo = _rms_bwd2(sv["x1"], p["norm_ffn_pre"], dh2, dx2, sv["mixed"], p["norm_mix_post"])
    dcat = _mm_rows("dcat", dmixed, p["wout"], BF16, NT)
    g_wout = _mm_wgrad("dw_out", sv["cat"], dmixed, BF16)
    big = {
        "w_out": _unpad_wout(g_wout).reshape(NDEV, 128, D),
        "ffn_w_up": g_wup,
        "ffn_conv_w": g_fcw.astype(BF16),
        "ffn_w_down": g_wdown.reshape(NDEV, 352, D),
    }
    outgoing = _pack_grads(big, FFN_SIDE) + (pending or []) if exchange else None
    dq, dk, dv, received = _flash_bwd(sv["q"], sv["k"], sv["kv"], sv["cat"], dcat, sv["lse"], outgoing)
    d_mla, g_wq, g_wkv, g_qn, g_kvn = _mla_prep_bwd(sv["proj"], tabs, p["mla_q_norm"], p["mla_kv_norm"], p["wq"], p["wkv"], dq, dk, dv)
    d_sc, g_scw = _sconv_bwd(sv["proj"], dcat, p["scw"])
    d_ssd, g_cw, g_cb, g_dtb, g_alog, g_d, g_nw = _ssd_bwd(
        sv["proj"], dcat, sv["ypre"], sv["states"], p["ssdcw"], p["ssd_conv_b"], p["dtb"], p["alog"], p["dvec"], p["ssd_norm"])
    dparts = [d_mla, d_sc, d_ssd]
    dh = _mm_dh(dparts, [p["win"][:, 0:512], p["win"][:, 512:1280], p["win"][:, 1280:PW]])
    g_win = _mm_dwin(sv["h"], dparts)
    big.update({
        "w_in": _unpad_win(*g_win).reshape(NDEV, 128, 2212),
        "mla_w_q_up": g_wq[:, :, :96].astype(BF16),
        "mla_w_kv_up": g_wkv.astype(BF16),
        "sc_conv_w": g_scw.reshape(3, NDEV, 32).transpose(1, 0, 2).astype(BF16),
        "ssd_conv_w": g_cw.reshape(4, NDEV, 96).transpose(1, 0, 2).astype(BF16),
    })
    head_below, last_received = None, None
    if below is not None:
        dx, df_below, g_nmp, g_below = _rms_bwd2(sv["x"], p["norm_mix_pre"], dh, dx1, *below)
        head_below = (df_below, g_below)
    elif exchange:
        dx, g_nmp, last_received = _rms_bwd(sv["x"], p["norm_mix_pre"], dh, dx1, F32, "rms_bwd_pre", _pack_grads(big, ATT_SIDE))
    else:
        dx, g_nmp = _rms_bwd(sv["x"], p["norm_mix_pre"], dh, dx1, F32, "rms_bwd_pre")
    small = {
        "norm_mix_pre": g_nmp[0], "norm_mix_post": g_nmpo[0], "norm_ffn_pre": g_nfp[0], "norm_ffn_post": g_nfpo[0],
        "mla_q_norm": g_qn[0], "mla_kv_norm": g_kvn[0], "ssd_conv_b": g_cb[0], "ssd_dt_bias": g_dtb[0, :SSD_H],
        "ssd_a_log": g_alog[0, :SSD_H], "ssd_d": g_d[0, :SSD_H], "ssd_norm": g_nw[0], "ffn_conv_b": g_fcb.reshape(-1),
    }
    return dx, big, small, received, head_below, last_received


def _local_step(xv, positions, target, layers):
    tabs = _rope_tables(positions)
    saved = []
    for p in layers:
        xv, sv, _, _ = _layer_fwd(xv, p, tabs)
        saved.append(sv)
    loss, dx = _loss_head(xv, target)
    bigs, smalls = [None] * DEPTH, [None] * DEPTH
    head = None
    for l in reversed(range(len(layers))):
        below = (saved[l - 1]["f"], layers[l - 1]["norm_ffn_post"]) if l > 0 else None
        dx, bigs[l], smalls[l], _, head, _ = _layer_bwd(dx, saved[l], layers[l], tabs, head=head, below=below)
    return loss[0, 0], dx, bigs, smalls


def _rows2(n, layer=False):
    shape = SHAPES[n][1:] if layer else SHAPES[n]
    return (math.prod(shape[:-1]), shape[-1])


def _group_pack(group, get, lead):
    width, names = group
    pieces = []
    for n in names:
        rows, cols = _rows2(n, True)
        pad = [(0, 0)] * len(lead) + [(0, -rows % 16), (0, width - cols)]
        pieces.append(jnp.pad(get(n), pad))
    return pieces[0] if len(pieces) == 1 else jnp.concatenate(pieces, axis=len(lead))


def _group_unpack(group, buf):
    _, names = group
    res, off = {}, 0
    for n in names:
        rows, cols = _rows2(n, True)
        res[n] = buf[:, off:off + rows, :cols]
        off += rows + (-rows % 16)
    return res


def kernel(x, positions, norm_mix_pre, norm_mix_post, norm_ffn_pre, norm_ffn_post, w_in, mla_q_norm, mla_w_q_up, mla_kv_norm, mla_w_kv_up, sc_conv_w, ssd_conv_w, ssd_conv_b, ssd_dt_bias, ssd_a_log, ssd_d, ssd_norm, w_out, ffn_w_up, ffn_conv_w, ffn_conv_b, ffn_w_down, loss_target, m_norm_mix_pre, m_norm_mix_post, m_norm_ffn_pre, m_norm_ffn_post, m_w_in, m_mla_q_norm, m_mla_w_q_up, m_mla_kv_norm, m_mla_w_kv_up, m_sc_conv_w, m_ssd_conv_w, m_ssd_conv_b, m_ssd_dt_bias, m_ssd_a_log, m_ssd_d, m_ssd_norm, m_w_out, m_ffn_w_up, m_ffn_conv_w, m_ffn_conv_b, m_ffn_w_down, v_norm_mix_pre, v_norm_mix_post, v_norm_ffn_pre, v_norm_ffn_post, v_w_in, v_mla_q_norm, v_mla_w_q_up, v_mla_kv_norm, v_mla_w_kv_up, v_sc_conv_w, v_ssd_conv_w, v_ssd_conv_b, v_ssd_dt_bias, v_ssd_a_log, v_ssd_d, v_ssd_norm, v_w_out, v_ffn_w_up, v_ffn_conv_w, v_ffn_conv_b, v_ffn_w_down):
    given = dict(locals())
    w = {n: given[n] for n in WEIGHTS}
    m = {n: given["m_" + n] for n in WEIGHTS}
    v = {n: given["v_" + n] for n in WEIGHTS}

    def shards(l, group_ids):
        return [_group_pack(GROUPS[gi], lambda n: w[n][l].astype(BF16).reshape(_rows2(n, True)), ()) for gi in group_ids]

    def unpacked(bufs, group_ids):
        big = {}
        for gi, buf in zip(group_ids, bufs):
            for n, piece in _group_unpack(GROUPS[gi], buf).items():
                big[n] = piece.reshape((NDEV,) + SHAPES[n][1:])
        return big

    small_w = {n: w[n] for n, _ in SMALL}
    tabs = _rope_tables(positions[0])
    xv, h, layers, saved = x[0], None, [], []
    att = _all_gather(shards(0, ATT_SIDE), "gather_weights")
    for l in range(DEPTH):
        prefetch = shards(l, FFN_SIDE) + (shards(l + 1, ATT_SIDE) if l + 1 < DEPTH else [])
        nxt = w["norm_mix_pre"][l + 1].reshape(1, D) if l + 1 < DEPTH else None
        xv, sv, p, gathered = _layer_fwd(xv, _prep_layer(unpacked(att, ATT_SIDE), small_w, l), tabs, prefetch,
                                         lambda got: _prep_ffn(unpacked(got[:len(FFN_SIDE)], FFN_SIDE)), h, nxt)
        xv, h = xv if nxt is not None else (xv, None)
        att = gathered[len(FFN_SIDE):]
        layers.append(p)
        saved.append(sv)
    loss, dx = _loss_head(xv, loss_target[0])
    loss = lax.psum(loss[0, 0], ("x", "y", "c"))

    smalls, pending, head = [None] * DEPTH, None, None
    recvs = [[None] * len(GROUPS) for _ in range(DEPTH)]
    for l in reversed(range(DEPTH)):
        below = (saved[l - 1]["f"], layers[l - 1]["norm_ffn_post"]) if l > 0 else None
        dx, grads, smalls[l], received, head, last = _layer_bwd(dx, saved[l], layers[l], tabs, True, pending, head, below)
        for pos, gi in enumerate(FFN_SIDE):
            recvs[l][gi] = received[pos]
        if pending is not None:
            for pos, gi in enumerate(ATT_SIDE):
                recvs[l + 1][gi] = received[len(FFN_SIDE) + pos]
        pending = _pack_grads(grads, ATT_SIDE) if l > 0 else None
    for gi, buf in zip(ATT_SIDE, last):
        recvs[0][gi] = buf
    out = {}
    for gi, g in enumerate(GROUPS):
        per_layer = [_group_unpack(g, recvs[l][gi]) for l in range(DEPTH)]
        for n in g[1]:
            parts = jnp.stack([per_layer[l][n] for l in range(DEPTH)], axis=1)
            out[n] = _adamw(parts, w[n], m[n], v[n], "adamw_" + n)

    total = sum(width for _, width in SMALL)
    padded = -(-total // (8 * LANE)) * 8 * LANE
    pk = lambda a: jnp.pad(a, ((0, 0), (0, padded - total))).reshape(1, DEPTH * padded // LANE, LANE)
    sflat = jnp.stack([jnp.concatenate([smalls[l][n] for n, _ in SMALL]) for l in range(DEPTH)])
    sparts = _all_gather([pk(sflat)[0]], "gather_small_grads")[0]
    pw = lambda d: pk(jnp.concatenate([d[n] for n, _ in SMALL], axis=1))
    res = _adamw(sparts[:, None], pw(w), pw(m), pw(v), "adamw_small")
    off = 0
    for n, width in SMALL:
        out[n] = [a.reshape(DEPTH, padded)[:, off:off + width] for a in res]
        off += width

    return (loss, dx[None], *[out[n][0] for n in WEIGHTS], *[out[n][1] for n in WEIGHTS],
            *[out[n][2] for n in WEIGHTS], *[out[n][3] for n in WEIGHTS])
```

```python
import functools
import math

import jax
import jax.numpy as jnp
from jax import lax
from jax.experimental import pallas as pl
from jax.experimental.pallas import tpu as pltpu

F32 = jnp.float32
BF16 = jnp.bfloat16

D = 1024
DEPTH = 4
NDEV = 8
HEADS = 8
QL = 256
KVL = 128
ROPE = 32
NOPE = 64
SC = 256
SSD_DIM = 256
SSD_CONV = 768
SSD_H = 4
SSD_L = 128
FFN = 2816
FB = 704
EPS = 1e-6
ROPE_THETA = 10000.0
ATT_SCALE = 96 ** -0.5
LOG2E = 1.4426950408889634
LR, B1, B2, AEPS, WD, STEP = 0.001, 0.9, 0.999, 1e-08, 0.01, 10

PW = 2432
CATW = 1536

ROW_TILE = 512
ROW_CHUNK = 16
NORM_CHUNK = 32
NORM_TILE = 512
MM_TILE = 1024
ATT_TILE = 512
ATT_QW = 1
BWD_TILE = 512
BWD_QW = 1
FWD_HEADS = 4
BWD_HEADS = 2
HALO = 16
LANE = 128
NEG = -1e30
NN = (((1,), (0,)), ((), ()))
NT = (((1,), (1,)), ((), ()))
TN = (((0,), (0,)), ((), ()))
VMEM_LIMIT = 56 * 1024 * 1024

SHARDED = (
    ("w_in", (4, 128, 2212)),
    ("mla_w_q_up", (4, 256, 96)),
    ("mla_w_kv_up", (4, 128, 128)),
    ("sc_conv_w", (4, 3, 32)),
    ("ssd_conv_w", (4, 4, 96)),
    ("w_out", (4, 128, 1024)),
    ("ffn_w_up", (4, 1024, 704)),
    ("ffn_conv_w", (4, 3, 704)),
    ("ffn_w_down", (4, 352, 1024)),
)
SHAPES = dict(SHARDED)
GROUPS = (
    (2212, ("w_in",)),
    (1024, ("w_out",)),
    (704, ("ffn_w_up",)),
    (96, ("mla_w_q_up", "ssd_conv_w", "sc_conv_w")),
    (128, ("mla_w_kv_up",)),
    (1024, ("ffn_w_down",)),
    (704, ("ffn_conv_w",)),
)
ATT_SIDE = (0, 3, 4)
FFN_SIDE = (1, 2, 5, 6)
SMALL = (
    ("norm_mix_pre", 1024), ("norm_mix_post", 1024), ("norm_ffn_pre", 1024), ("norm_ffn_post", 1024),
    ("mla_q_norm", 256), ("mla_kv_norm", 128), ("ssd_conv_b", 768), ("ssd_dt_bias", 4), ("ssd_a_log", 4),
    ("ssd_d", 4), ("ssd_norm", 256), ("ffn_conv_b", 5632),
)
WEIGHTS = ("norm_mix_pre", "norm_mix_post", "norm_ffn_pre", "norm_ffn_post", "w_in", "mla_q_norm", "mla_w_q_up",
           "mla_kv_norm", "mla_w_kv_up", "sc_conv_w", "ssd_conv_w", "ssd_conv_b", "ssd_dt_bias", "ssd_a_log", "ssd_d",
           "ssd_norm", "w_out", "ffn_w_up", "ffn_conv_w", "ffn_conv_b", "ffn_w_down")


def _dot(a, b, dims=NN, precision=None):
    return lax.dot_general(a, b, dims, precision=precision, preferred_element_type=F32)


def _dot01(x, c, dims=NN, x_first=True):
    cb = c.astype(BF16)
    hi = x.astype(BF16)
    rest = x - hi.astype(F32)
    mid = rest.astype(BF16)
    lo = (rest - mid.astype(F32)).astype(BF16)
    one = (lambda t: _dot(t, cb, dims)) if x_first else (lambda t: _dot(cb, t, dims))
    return one(hi) + one(mid) + one(lo)


def _sig(v):
    return 0.5 * jnp.tanh(0.5 * v) + 0.5


def _cp(*sem):
    return pltpu.CompilerParams(dimension_semantics=sem, vmem_limit_bytes=VMEM_LIMIT)


def _rowsum(v):
    return jnp.sum(v, axis=0, keepdims=True)


def _prev_halo(i, ts):
    return jnp.maximum(i * (ts // HALO) - 1, 0)


def _next_halo(i, ts, n):
    return jnp.minimum((i + 1) * (ts // HALO), n * (ts // HALO) - 1)


def _gather_plan(x_refs, out_refs, send_sems, recv_sems, local_sems):
    n = len(x_refs)
    x, y, cc = lax.axis_index("x"), lax.axis_index("y"), lax.axis_index("c")
    me, sibling = (x, y, cc), (x, y, 1 - cc)
    chips = [(1 - x, y), (x, 1 - y), (1 - x, 1 - y)]

    def rows(t, px, py, pc):
        return out_refs[t].at[4 * px + 2 * py + pc]

    def copy(t, k, block, to, own=False):
        return pltpu.make_async_remote_copy(
            src_ref=x_refs[t] if own else rows(t, *block), dst_ref=rows(t, *block),
            send_sem=send_sems.at[7 * t + k], recv_sem=recv_sems.at[7 * t + k], device_id=to, device_id_type=pl.DeviceIdType.MESH)

    def local(t):
        return pltpu.make_async_copy(x_refs[t], rows(t, *me), local_sems.at[t])

    def start():
        for t in range(n):
            local(t).start()
            copy(t, 0, me, sibling, own=True).start()
            for j, chip in enumerate(chips):
                copy(t, 1 + j, me, (*chip, cc), own=True).start()

    def finish():
        for j, chip in enumerate(chips):
            for t in range(n):
                copy(t, 1 + j, (*chip, cc), me).wait_recv()
                copy(t, 4 + j, (*chip, cc), sibling).start()
        for t in range(n):
            copy(t, 0, sibling, me).wait_recv()
            for j, chip in enumerate(chips):
                copy(t, 4 + j, (*chip, 1 - cc), me).wait_recv()
        for t in range(n):
            copy(t, 0, me, sibling, own=True).wait_send()
            for j, chip in enumerate(chips):
                copy(t, 1 + j, me, (*chip, cc), own=True).wait_send()
                copy(t, 4 + j, (*chip, cc), sibling).wait_send()
            local(t).wait()

    return start, finish


def _exchange_plan(x_refs, out_refs, send_sems, recv_sems, local_sems):
    n = len(x_refs)
    x, y, cc = lax.axis_index("x"), lax.axis_index("y"), lax.axis_index("c")
    me = 4 * x + 2 * y + cc

    def copies():
        res = [pltpu.make_async_copy(x_refs[t].at[me], out_refs[t].at[me], local_sems.at[t]) for t in range(n)]
        for k in range(1, NDEV):
            px = 1 - x if k & 4 else x
            py = 1 - y if k & 2 else y
            pc = 1 - cc if k & 1 else cc
            peer = 4 * px + 2 * py + pc
            for t in range(n):
                res.append(pltpu.make_async_remote_copy(
                    src_ref=x_refs[t].at[peer], dst_ref=out_refs[t].at[me], send_sem=send_sems.at[7 * t + k - 1],
                    recv_sem=recv_sems.at[7 * t + k - 1], device_id=(px, py, pc), device_id_type=pl.DeviceIdType.MESH))
        return res

    def start():
        for cp in copies():
            cp.start()

    def finish():
        for cp in copies():
            cp.wait()

    return start, finish


def _comm_scratch(n):
    return [pltpu.SemaphoreType.DMA((7 * n,)), pltpu.SemaphoreType.DMA((7 * n,)), pltpu.SemaphoreType.DMA((n,))]


ANY = pl.BlockSpec(memory_space=pl.ANY)


def _all_gather(xs, name):
    n = len(xs)

    def body(*refs):
        start, finish = _gather_plan(refs[:n], refs[n:2 * n], *refs[2 * n:])
        start()
        finish()

    return pl.pallas_call(
        body, name=name, out_shape=[jax.ShapeDtypeStruct((NDEV,) + a.shape, a.dtype) for a in xs],
        in_specs=[ANY] * n, out_specs=[ANY] * n, scratch_shapes=_comm_scratch(n),
    )(*xs)


def _mm(name, a, b, out_shape, grid, a_spec, b_spec, o_spec, dims, acc_shape):
    nk = grid[2]

    def single(a_ref, b_ref, o_ref):
        o_ref[...] = _dot(a_ref[...], b_ref[...], dims).astype(o_ref.dtype)

    if nk == 1:
        return pl.pallas_call(
            single, name=name, grid=grid, out_shape=out_shape, in_specs=[a_spec, b_spec], out_specs=o_spec,
            compiler_params=_cp("parallel", "parallel", "arbitrary"),
        )(a, b)

    def body(a_ref, b_ref, o_ref, acc_ref):
        k = pl.program_id(2)

        @pl.when(k == 0)
        def _():
            acc_ref[...] = jnp.zeros_like(acc_ref)

        acc_ref[...] += _dot(a_ref[...], b_ref[...], dims)

        @pl.when(k == nk - 1)
        def _():
            o_ref[...] = acc_ref[...].astype(o_ref.dtype)

    return pl.pallas_call(
        body, name=name, grid=grid, out_shape=out_shape, in_specs=[a_spec, b_spec], out_specs=o_spec,
        scratch_shapes=[pltpu.VMEM(acc_shape, F32)], compiler_params=_cp("parallel", "parallel", "arbitrary"),
    )(a, b)


def _mm_rows(name, a, w, out_dtype, dims):
    s, k = a.shape
    n = w.shape[1] if dims == NN else w.shape[0]
    tm = min(MM_TILE, s)
    return _mm(name, a, w, jax.ShapeDtypeStruct((s, n), out_dtype), (s // tm, 1, 1),
               pl.BlockSpec((tm, k), lambda i, j, kk: (i, 0)), pl.BlockSpec(w.shape, lambda i, j, kk: (0, 0)),
               pl.BlockSpec((tm, n), lambda i, j, kk: (i, 0)), dims, (tm, n))


def _mm_wgrad(name, a, g, out_dtype):
    s, m = a.shape
    n = g.shape[1]
    tk = min(MM_TILE, s)
    return _mm(name, a, g, jax.ShapeDtypeStruct((m, n), out_dtype), (1, 1, s // tk),
               pl.BlockSpec((tk, m), lambda i, j, kk: (kk, 0)), pl.BlockSpec((tk, n), lambda i, j, kk: (kk, 0)),
               pl.BlockSpec((m, n), lambda i, j, kk: (0, 0)), TN, (m, n))


def _mm_sum(name, parts, wparts, dims):
    s = parts[0].shape[0]
    tm = min(MM_TILE, s)
    n = len(parts)

    def body(*refs):
        acc = _dot(refs[0][...], refs[n][...], dims)
        for i in range(1, n):
            acc += _dot(refs[i][...], refs[n + i][...], dims)
        refs[2 * n][...] = acc.astype(BF16)

    return pl.pallas_call(
        body, name=name, grid=(s // tm,), out_shape=jax.ShapeDtypeStruct((s, D), BF16),
        in_specs=[pl.BlockSpec((tm, a.shape[1]), lambda i: (i, 0)) for a in parts] + [pl.BlockSpec(w.shape, lambda i: (0, 0)) for w in wparts],
        out_specs=pl.BlockSpec((tm, D), lambda i: (i, 0)), compiler_params=_cp("parallel"),
    )(*parts, *wparts)


def _mm_dwin(h, parts):
    s = h.shape[0]
    tk = min(MM_TILE, s)
    nk = s // tk
    n = len(parts)

    def body(*refs):
        h_ref, g_refs, o_refs, accs = refs[0], refs[1:1 + n], refs[1 + n:1 + 2 * n], refs[1 + 2 * n:]
        k = pl.program_id(0)

        @pl.when(k == 0)
        def _():
            for acc in accs:
                acc[...] = jnp.zeros_like(acc)

        hv = h_ref[...]
        for i in range(n):
            accs[i][...] += _dot(hv, g_refs[i][...], TN)

        @pl.when(k == nk - 1)
        def _():
            for i in range(n):
                o_refs[i][...] = accs[i][...].astype(BF16)

    widths = [a.shape[1] for a in parts]
    return pl.pallas_call(
        body, name="dw_in", grid=(nk,), out_shape=[jax.ShapeDtypeStruct((D, w), BF16) for w in widths],
        in_specs=[pl.BlockSpec((tk, D), lambda k: (k, 0))] + [pl.BlockSpec((tk, w), lambda k: (k, 0)) for w in widths],
        out_specs=[pl.BlockSpec((D, w), lambda k: (0, 0)) for w in widths],
        scratch_shapes=[pltpu.VMEM((D, w), F32) for w in widths], compiler_params=_cp("arbitrary"),
    )(h, *parts)


def _mm_dwout(parts, g):
    s = g.shape[0]
    tk = min(MM_TILE, s)
    nk = s // tk
    n = len(parts)

    def body(*refs):
        a_refs, g_ref, o_refs, accs = refs[:n], refs[n], refs[n + 1:2 * n + 1], refs[2 * n + 1:]
        k = pl.program_id(0)

        @pl.when(k == 0)
        def _():
            for acc in accs:
                acc[...] = jnp.zeros_like(acc)

        gv = g_ref[...]
        for i in range(n):
            accs[i][...] += _dot(a_refs[i][...], gv, TN)

        @pl.when(k == nk - 1)
        def _():
            for i in range(n):
                o_refs[i][...] = accs[i][...].astype(BF16)

    widths = [a.shape[1] for a in parts]
    return pl.pallas_call(
        body, name="dw_out", grid=(nk,), out_shape=[jax.ShapeDtypeStruct((w, D), BF16) for w in widths],
        in_specs=[pl.BlockSpec((tk, w), lambda k: (k, 0)) for w in widths] + [pl.BlockSpec((tk, D), lambda k: (k, 0))],
        out_specs=[pl.BlockSpec((w, D), lambda k: (0, 0)) for w in widths],
        scratch_shapes=[pltpu.VMEM((w, D), F32) for w in widths], compiler_params=_cp("arbitrary"),
    )(*parts, g)


def _mm_up(h2, wup):
    s = h2.shape[0]
    tm = min(MM_TILE, s)
    return _mm("ffn_up", h2, wup, jax.ShapeDtypeStruct((NDEV, s, FB), BF16), (NDEV, s // tm, 1),
               pl.BlockSpec((tm, D), lambda j, i, kk: (i, 0)), pl.BlockSpec((None, D, FB), lambda j, i, kk: (j, 0, 0)),
               pl.BlockSpec((None, tm, FB), lambda j, i, kk: (j, i, 0)), NN, (tm, FB))


def _mm_down(act, wdown):
    s = act.shape[1]
    tm = min(MM_TILE, s)
    return _mm("ffn_down", act, wdown, jax.ShapeDtypeStruct((s, D), BF16), (s // tm, 1, 4),
               pl.BlockSpec((None, tm, FB), lambda i, j, kk: (kk, i, 0)), pl.BlockSpec((None, FB, D), lambda i, j, kk: (kk, 0, 0)),
               pl.BlockSpec((tm, D), lambda i, j, kk: (i, 0)), NN, (tm, D))


def _mm_dact(df, wdown):
    s = df.shape[0]
    tm = min(MM_TILE, s)
    return _mm("ffn_dact", df, wdown, jax.ShapeDtypeStruct((4, s, FB), BF16), (4, s // tm, 1),
               pl.BlockSpec((tm, D), lambda j, i, kk: (i, 0)), pl.BlockSpec((None, FB, D), lambda j, i, kk: (j, 0, 0)),
               pl.BlockSpec((None, tm, FB), lambda j, i, kk: (j, i, 0)), NT, (tm, FB))


def _mm_dwdown(act, df):
    s = df.shape[0]
    tk = min(MM_TILE, s)
    return _mm("ffn_dwdown", act, df, jax.ShapeDtypeStruct((4, FB, D), BF16), (4, 1, s // tk),
               pl.BlockSpec((None, tk, FB), lambda j, i, kk: (j, kk, 0)), pl.BlockSpec((tk, D), lambda j, i, kk: (kk, 0)),
               pl.BlockSpec((None, FB, D), lambda j, i, kk: (j, 0, 0)), TN, (FB, D))


def _mm_dh2(dupre, wup):
    s = dupre.shape[1]
    tm = min(MM_TILE, s)
    return _mm("ffn_dh2", dupre, wup, jax.ShapeDtypeStruct((s, D), BF16), (s // tm, 1, NDEV),
               pl.BlockSpec((None, tm, FB), lambda i, j, kk: (kk, i, 0)), pl.BlockSpec((None, D, FB), lambda i, j, kk: (kk, 0, 0)),
               pl.BlockSpec((tm, D), lambda i, j, kk: (i, 0)), NT, (tm, D))


def _mm_dwup(h2, dupre):
    s = h2.shape[0]
    tk = min(MM_TILE, s)
    return _mm("ffn_dwup", h2, dupre, jax.ShapeDtypeStruct((NDEV, D, FB), BF16), (NDEV, 1, s // tk),
               pl.BlockSpec((tk, D), lambda j, i, kk: (kk, 0)), pl.BlockSpec((None, tk, FB), lambda j, i, kk: (j, kk, 0)),
               pl.BlockSpec((None, D, FB), lambda j, i, kk: (j, 0, 0)), TN, (D, FB))


def _rms(xv, w, out_dtype, name):
    s, d = xv.shape
    ts = min(ROW_TILE, s)

    def body(x_ref, w_ref, o_ref):
        for r0 in range(0, ts, NORM_CHUNK):
            rows = pl.ds(r0, NORM_CHUNK)
            xf = x_ref[rows, :].astype(F32)
            r = lax.rsqrt(jnp.mean(xf * xf, axis=-1, keepdims=True) + EPS)
            o_ref[rows, :] = (xf * r * w_ref[...]).astype(o_ref.dtype)

    return pl.pallas_call(
        body, name=name, grid=(s // ts,), out_shape=jax.ShapeDtypeStruct((s, d), out_dtype),
        in_specs=[pl.BlockSpec((ts, d), lambda i: (i, 0)), pl.BlockSpec((1, d), lambda i: (0, 0))],
        out_specs=pl.BlockSpec((ts, d), lambda i: (i, 0)), compiler_params=_cp("parallel"),
    )(xv, w)


def _add_rms(xv, mv, w, name, w_next=None):
    s, d = xv.shape
    ts = min(NORM_TILE, s)
    both = w_next is not None

    def body(*refs):
        x_ref, m_ref, w_ref = refs[:3]
        o_ref = refs[4] if both else refs[3]
        for r0 in range(0, ts, NORM_CHUNK):
            rows = pl.ds(r0, NORM_CHUNK)
            mf = m_ref[rows, :].astype(F32)
            r = lax.rsqrt(jnp.mean(mf * mf, axis=-1, keepdims=True) + EPS)
            y = x_ref[rows, :] + mf * r * w_ref[...]
            o_ref[rows, :] = y
            if both:
                r2 = lax.rsqrt(jnp.mean(y * y, axis=-1, keepdims=True) + EPS)
                refs[5][rows, :] = (y * r2 * refs[3][...]).astype(BF16)

    row = pl.BlockSpec((ts, d), lambda i: (i, 0))
    vec = pl.BlockSpec((1, d), lambda i: (0, 0))
    if both:
        return pl.pallas_call(
            body, name=name + "_rms", grid=(s // ts,),
            out_shape=(jax.ShapeDtypeStruct((s, d), F32), jax.ShapeDtypeStruct((s, d), BF16)),
            in_specs=[row, row, vec, vec], out_specs=(row, row), compiler_params=_cp("parallel"),
        )(xv, mv, w, w_next)
    return pl.pallas_call(
        body, name=name, grid=(s // ts,), out_shape=jax.ShapeDtypeStruct((s, d), F32),
        in_specs=[row, row, vec], out_specs=row, compiler_params=_cp("parallel"),
    )(xv, mv, w)


def _rms_bwd_math(xf, w, dy):
    r = lax.rsqrt(jnp.mean(xf * xf, axis=-1, keepdims=True) + EPS)
    xh = xf * r
    dxh = dy * w
    dx = r * (dxh - xh * jnp.mean(dxh * xh, axis=-1, keepdims=True))
    return dx, dy * xh


def _rms_bwd(xv, w, dy, dres, out_dtype, name, pending=None):
    s, d = xv.shape
    ts = min(NORM_TILE, s)
    with_res = dres is not None
    nin = 4 if with_res else 3
    nx = len(pending) if pending else 0

    def body(*refs):
        x_ref, w_ref, dy_ref = refs[:3]
        dres_ref = refs[3] if with_res else None
        dx_ref, dw_ref = refs[nin + nx:nin + nx + 2]
        acc = refs[nin + 2 * nx + 2]
        if nx:
            start, finish = _exchange_plan(refs[nin:nin + nx], refs[nin + nx + 2:nin + 2 * nx + 2], *refs[nin + 2 * nx + 3:])
            pl.when(pl.program_id(0) == 0)(start)
        acc[...] = jnp.zeros_like(acc)
        for r0 in range(0, ts, NORM_CHUNK):
            rows = pl.ds(r0, NORM_CHUNK)
            dx, dwt = _rms_bwd_math(x_ref[rows, :].astype(F32), w_ref[...], dy_ref[rows, :].astype(F32))
            if with_res:
                dx = dx + dres_ref[rows, :]
            dx_ref[rows, :] = dx.astype(dx_ref.dtype)
            acc[...] += dwt

        @pl.when(pl.program_id(0) == 0)
        def _():
            dw_ref[...] = jnp.zeros_like(dw_ref)

        dw_ref[...] += _rowsum(acc[...])
        if nx:
            pl.when(pl.program_id(0) == s // ts - 1)(finish)

    row = pl.BlockSpec((ts, d), lambda i: (i, 0))
    vec = pl.BlockSpec((1, d), lambda i: (0, 0))
    ins = [xv, w, dy] + ([dres] if with_res else []) + (pending or [])
    res = pl.pallas_call(
        body, name=name + "_exchange" if nx else name, grid=(s // ts,),
        out_shape=[jax.ShapeDtypeStruct((s, d), out_dtype), jax.ShapeDtypeStruct((1, d), F32)]
        + [jax.ShapeDtypeStruct(a.shape, a.dtype) for a in (pending or [])],
        in_specs=[row, vec, row] + ([row] if with_res else []) + [ANY] * nx, out_specs=[row, vec] + [ANY] * nx,
        scratch_shapes=[pltpu.VMEM((NORM_CHUNK, d), F32)] + (_comm_scratch(nx) if nx else []), compiler_params=_cp("arbitrary"),
    )(*ins)
    return (res[0], res[1], list(res[2:])) if nx else (res[0], res[1])


def _rms_bwd2(xa, wa, dya, dres, xb, wb):
    s, d = xa.shape
    ts = min(NORM_TILE, s)

    def body(xa_ref, wa_ref, dya_ref, dres_ref, xb_ref, wb_ref, da_ref, db_ref, dwa_ref, dwb_ref, acc):
        acc[...] = jnp.zeros_like(acc)
        for r0 in range(0, ts, NORM_CHUNK):
            rows = pl.ds(r0, NORM_CHUNK)
            da, dwt = _rms_bwd_math(xa_ref[rows, :].astype(F32), wa_ref[...], dya_ref[rows, :].astype(F32))
            da = da + dres_ref[rows, :]
            da_ref[rows, :] = da
            acc[0] += dwt
            db, dwt = _rms_bwd_math(xb_ref[rows, :].astype(F32), wb_ref[...], da)
            db_ref[rows, :] = db.astype(BF16)
            acc[1] += dwt

        @pl.when(pl.program_id(0) == 0)
        def _():
            dwa_ref[...] = jnp.zeros_like(dwa_ref)
            dwb_ref[...] = jnp.zeros_like(dwb_ref)

        dwa_ref[...] += _rowsum(acc[0])
        dwb_ref[...] += _rowsum(acc[1])

    row = pl.BlockSpec((ts, d), lambda i: (i, 0))
    vec = pl.BlockSpec((1, d), lambda i: (0, 0))
    return pl.pallas_call(
        body, name="rms_bwd2", grid=(s // ts,),
        out_shape=(jax.ShapeDtypeStruct((s, d), F32), jax.ShapeDtypeStruct((s, d), BF16), jax.ShapeDtypeStruct((1, d), F32),
                   jax.ShapeDtypeStruct((1, d), F32)),
        in_specs=[row, vec, row, row, row, vec], out_specs=(row, row, vec, vec),
        scratch_shapes=[pltpu.VMEM((2, NORM_CHUNK, d), F32)], compiler_params=_cp("arbitrary"),
    )(xa, wa, dya, dres, xb, wb)


def _loss_head(yv, tv):
    s, d = yv.shape
    ts = min(ROW_TILE, s)

    def body(y_ref, t_ref, l_ref, dy_ref):
        e = y_ref[...] - t_ref[...]
        dy_ref[...] = e * (1.0 / d)

        @pl.when(pl.program_id(0) == 0)
        def _():
            l_ref[...] = jnp.zeros_like(l_ref)

        tot = jnp.sum(jnp.sum(e * e, axis=1, keepdims=True), axis=0, keepdims=True)
        l_ref[...] += jnp.broadcast_to(tot * (0.5 / d), (8, LANE))

    row = pl.BlockSpec((ts, d), lambda i: (i, 0))
    return pl.pallas_call(
        body, name="loss_head", grid=(s // ts,),
        out_shape=(jax.ShapeDtypeStruct((8, LANE), F32), jax.ShapeDtypeStruct((s, d), F32)),
        in_specs=[row, row], out_specs=(pl.BlockSpec((8, LANE), lambda i: (0, 0)), row), compiler_params=_cp("arbitrary"),
    )(yv, tv)


def _rope(v, c, a, b):
    return v * c + pltpu.roll(v, LANE - 16, 1) * a + pltpu.roll(v, 16, 1) * b


def _rope_t(dv, c, a, b):
    return dv * c + pltpu.roll(dv * a, 16, 1) + pltpu.roll(dv * b, LANE - 16, 1)


def _mla_prep(proj, tabs, qnw, kvnw, wq, wkv):
    s = proj.shape[0]
    ts = min(ROW_TILE, s)
    tc, ta, tb = tabs

    def body(cq_ref, ckv_ref, kr_ref, c_ref, a_ref, b_ref, qnw_ref, kvnw_ref, wq_ref, wkv_ref, q_ref, k_ref, kv_ref):
        c, a, b = c_ref[...], a_ref[...], b_ref[...]
        cq = cq_ref[...].astype(F32)
        qn = (cq * lax.rsqrt(jnp.mean(cq * cq, axis=-1, keepdims=True) + EPS) * qnw_ref[...]).astype(BF16)
        ckv = ckv_ref[...].astype(F32)
        kvn = (ckv * lax.rsqrt(jnp.mean(ckv * ckv, axis=-1, keepdims=True) + EPS) * kvnw_ref[...]).astype(BF16)
        kr = _rope(kr_ref[...].astype(F32), c, a, b)
        lane = lax.broadcasted_iota(jnp.int32, (ts, LANE), 1)
        for h in range(HEADS):
            q_ref[h] = _rope(_dot(qn, wq_ref[h]), c, a, b).astype(BF16)
            kv = _dot(kvn, wkv_ref[h])
            kv_ref[h] = kv.astype(BF16)
            k_ref[h] = jnp.where(lane < NOPE, kv, kr).astype(BF16)

    tab = pl.BlockSpec((ts, LANE), lambda i: (i, 0))
    hd = pl.BlockSpec((HEADS, ts, LANE), lambda i: (0, i, 0))
    out = jax.ShapeDtypeStruct((HEADS, s, LANE), BF16)
    return pl.pallas_call(
        body, name="mla_prep", grid=(s // ts,), out_shape=(out, out, out),
        in_specs=[pl.BlockSpec((ts, QL), lambda i: (i, 0)), pl.BlockSpec((ts, LANE), lambda i: (i, 2)),
                  pl.BlockSpec((ts, LANE), lambda i: (i, 3)), tab, tab, tab,
                  pl.BlockSpec((1, QL), lambda i: (0, 0)), pl.BlockSpec((1, KVL), lambda i: (0, 0)),
                  pl.BlockSpec((HEADS, QL, LANE), lambda i: (0, 0, 0)), pl.BlockSpec((HEADS, KVL, LANE), lambda i: (0, 0, 0))],
        out_specs=(hd, hd, hd), compiler_params=_cp("parallel"),
    )(proj, proj, proj, tc, ta, tb, qnw, kvnw, wq, wkv)


def _mla_prep_bwd(proj, tabs, qnw, kvnw, wq, wkv, dq, dk, dv):
    s = proj.shape[0]
    ts = min(ROW_TILE, s)
    tc, ta, tb = tabs

    def body(cq_ref, ckv_ref, c_ref, a_ref, b_ref, qnw_ref, kvnw_ref, wq_ref, wkv_ref, dq_ref, dk_ref, dv_ref,
             d_ref, dwq_ref, dwkv_ref, dqnw_ref, dkvnw_ref):
        dcq_ref, dckv_ref, dkr_ref = d_ref.at[:, 0:QL], d_ref.at[:, QL:QL + KVL], d_ref.at[:, QL + KVL:QL + KVL + LANE]

        @pl.when(pl.program_id(0) == 0)
        def _():
            dwq_ref[...] = jnp.zeros_like(dwq_ref)
            dwkv_ref[...] = jnp.zeros_like(dwkv_ref)
            dqnw_ref[...] = jnp.zeros_like(dqnw_ref)
            dkvnw_ref[...] = jnp.zeros_like(dkvnw_ref)

        c, a, b = c_ref[...], a_ref[...], b_ref[...]
        cq = cq_ref[...].astype(F32)
        qn = (cq * lax.rsqrt(jnp.mean(cq * cq, axis=-1, keepdims=True) + EPS) * qnw_ref[...]).astype(BF16)
        ckv = ckv_ref[...].astype(F32)
        kvn = (ckv * lax.rsqrt(jnp.mean(ckv * ckv, axis=-1, keepdims=True) + EPS) * kvnw_ref[...]).astype(BF16)
        lane = lax.broadcasted_iota(jnp.int32, (ts, LANE), 1)
        dqn = jnp.zeros((ts, QL), F32)
        dkvn = jnp.zeros((ts, KVL), F32)
        dkr = jnp.zeros((ts, LANE), F32)
        for h in range(HEADS):
            dqh = _rope_t(dq_ref[h], c, a, b).astype(BF16)
            dwq_ref[h] += _dot(qn, dqh, TN)
            dqn += _dot(dqh, wq_ref[h], NT)
            dkh = dk_ref[h].astype(F32)
            dkvh = jnp.where(lane < NOPE, dkh, dv_ref[h].astype(F32)).astype(BF16)
            dkr += jnp.where(lane < NOPE, 0.0, dkh)
            dwkv_ref[h] += _dot(kvn, dkvh, TN)
            dkvn += _dot(dkvh, wkv_ref[h], NT)
        dkr_ref[...] = _rope_t(dkr, c, a, b).astype(BF16)
        dcq, dwt = _rms_bwd_math(cq, qnw_ref[...], dqn)
        dcq_ref[...] = dcq.astype(BF16)
        dqnw_ref[...] += _rowsum(dwt)
        dckv, dwt = _rms_bwd_math(ckv, kvnw_ref[...], dkvn)
        dckv_ref[...] = dckv.astype(BF16)
        dkvnw_ref[...] += _rowsum(dwt)

    tab = pl.BlockSpec((ts, LANE), lambda i: (i, 0))
    hd = pl.BlockSpec((HEADS, ts, LANE), lambda i: (0, i, 0))
    wq_spec = pl.BlockSpec((HEADS, QL, LANE), lambda i: (0, 0, 0))
    wkv_spec = pl.BlockSpec((HEADS, KVL, LANE), lambda i: (0, 0, 0))
    return pl.pallas_call(
        body, name="mla_prep_bwd", grid=(s // ts,),
        out_shape=(jax.ShapeDtypeStruct((s, QL + KVL + LANE), BF16),
                   jax.ShapeDtypeStruct((HEADS, QL, LANE), F32), jax.ShapeDtypeStruct((HEADS, KVL, LANE), F32),
                   jax.ShapeDtypeStruct((1, QL), F32), jax.ShapeDtypeStruct((1, KVL), F32)),
        in_specs=[pl.BlockSpec((ts, QL), lambda i: (i, 0)), pl.BlockSpec((ts, LANE), lambda i: (i, 2)), tab, tab, tab,
                  pl.BlockSpec((1, QL), lambda i: (0, 0)), pl.BlockSpec((1, KVL), lambda i: (0, 0)), wq_spec, wkv_spec, hd, hd, hd],
        out_specs=(pl.BlockSpec((ts, QL + KVL + LANE), lambda i: (i, 0)), wq_spec, wkv_spec,
                   pl.BlockSpec((1, QL), lambda i: (0, 0)), pl.BlockSpec((1, KVL), lambda i: (0, 0))),
        compiler_params=_cp("arbitrary"),
    )(proj, proj, tc, ta, tb, qnw, kvnw, wq, wkv, dq, dk, dv)


def _transpose_bf16(v):
    return v.astype(F32).T.astype(BF16)


def _flash_fwd(q, k, kv, prefetch=None):
    s = q.shape[1]
    t = min(ATT_TILE, s)
    tq = ATT_QW * t
    n = s // tq
    g = FWD_HEADS
    nx = len(prefetch) if prefetch else 0

    def body(*refs):
        q_ref, k_ref, kv_ref = refs[:3]
        o_ref, lse_ref = refs[3 + nx:5 + nx]
        kvt_sc = refs[5 + 2 * nx]
        step = pl.program_id(0) * n + pl.program_id(1)
        if nx:
            start, finish = _gather_plan(refs[3:3 + nx], refs[5 + nx:5 + 2 * nx], *refs[6 + 2 * nx:])
            pl.when(step == 0)(start)
        attend(q_ref, k_ref, kv_ref, o_ref, lse_ref, kvt_sc)
        if nx:
            pl.when(step == (HEADS // g) * n - 1)(finish)

    def attend(q_ref, k_ref, kv_ref, o_ref, lse_ref, kvt_sc):
        i = pl.program_id(1)

        @pl.when(i == 0)
        def _():
            ones_rows = lax.broadcasted_iota(jnp.int32, (LANE, s), 0) < NOPE
            for hh in range(g):
                kvt_sc[hh] = jnp.where(ones_rows, 1.0, kv_ref[hh].astype(F32).T).astype(BF16)

        qt = [(q_ref[hh].astype(F32) * (ATT_SCALE * LOG2E)).T.astype(BF16) for hh in range(g)]
        kpos = lax.broadcasted_iota(jnp.int32, (t, tq), 0)
        qpos = lax.broadcasted_iota(jnp.int32, (t, tq), 1) + i * tq

        def chunk(j, carry, diagonal):
            start = pl.multiple_of(j * t, t)
            scs = [_dot(k_ref[hh, pl.ds(start, t), :], qt[hh]) for hh in range(g)]
            soft = []
            for hh in range(g):
                sc = scs[hh]
                if diagonal:
                    sc = jnp.where(qpos >= kpos + start, sc, NEG)
                m_new = jnp.maximum(carry[hh][0], jnp.max(sc, axis=0, keepdims=True))
                soft.append((m_new, jnp.exp2(carry[hh][0] - m_new), jnp.exp2(sc - m_new).astype(BF16)))
            pvs = [_dot(kvt_sc[hh, :, pl.ds(start, t)], soft[hh][2]) for hh in range(g)]
            return tuple((soft[hh][0], soft[hh][1] * carry[hh][1] + pvs[hh]) for hh in range(g))

        init = tuple((jnp.full((1, tq), NEG, F32), jnp.zeros((LANE, tq), F32)) for _ in range(g))
        carry = lax.fori_loop(0, ATT_QW * i, lambda j, c: chunk(j, c, False), init)
        for d in range(ATT_QW):
            carry = chunk(ATT_QW * i + d, carry, True)
        for hh in range(g):
            m, acc = carry[hh]
            l = acc[0:1, :]
            o_ref[:, hh * LANE:(hh + 1) * LANE] = (acc / l).T.astype(BF16)
            lse_ref[hh] = m + jnp.log2(l)

    whole = pl.BlockSpec((g, s, LANE), lambda h, i: (h, 0, 0))
    res = pl.pallas_call(
        body, name="flash_fwd_gather" if nx else "flash_fwd", grid=(HEADS // g, n),
        out_shape=[jax.ShapeDtypeStruct((s, HEADS * LANE), BF16), jax.ShapeDtypeStruct((HEADS, 1, s), F32)]
        + [jax.ShapeDtypeStruct((NDEV,) + a.shape, a.dtype) for a in (prefetch or [])],
        in_specs=[pl.BlockSpec((g, tq, LANE), lambda h, i: (h, i, 0)), whole, whole] + [ANY] * nx,
        out_specs=[pl.BlockSpec((tq, g * LANE), lambda h, i: (i, h)), pl.BlockSpec((g, 1, tq), lambda h, i: (h, 0, i))] + [ANY] * nx,
        scratch_shapes=[pltpu.VMEM((g, LANE, s), BF16)] + (_comm_scratch(nx) if nx else []),
        compiler_params=_cp("arbitrary", "arbitrary"),
    )(q, k, kv, *(prefetch or []))
    return res[0], res[1], list(res[2:])


def _flash_bwd(q, k, kv, cat, dcat, lse, pending=None):
    s = q.shape[1]
    t = min(BWD_TILE, s)
    tq = BWD_QW * t
    n = s // t
    g = BWD_HEADS
    nx = len(pending) if pending else 0

    def body(*refs):
        ins, outs, scr = refs[:6], refs[6 + nx:9 + nx], refs[9 + 2 * nx:14 + 2 * nx]
        step = pl.program_id(0) * n + pl.program_id(1)
        if nx:
            start, finish = _exchange_plan(refs[6:6 + nx], refs[9 + nx:9 + 2 * nx], *refs[14 + 2 * nx:])
            pl.when(step == 0)(start)
        attend(*ins, *outs, *scr)
        if nx:
            pl.when(step == (HEADS // g) * n - 1)(finish)

    def attend(q_ref, k_ref, kv_ref, o_ref, do_ref, lse_ref, dq_ref, dk_ref, dv_ref, qt_sc, dot_sc, delta_sc, dqt_sc, qs_sc):
        j = pl.program_id(1)

        @pl.when(j == 0)
        def _():
            for hh in range(g):
                lanes = slice(hh * LANE, (hh + 1) * LANE)
                qf = q_ref[hh].astype(F32)
                qt_sc[hh] = (qf * (ATT_SCALE * LOG2E)).T.astype(BF16)
                qs_sc[hh] = (qf * ATT_SCALE).astype(BF16)
                dof = do_ref[:, lanes].astype(F32)
                dot_sc[hh] = dof.T.astype(BF16)
                delta_sc[hh] = _dot01(dof * o_ref[:, lanes].astype(F32), jnp.ones((8, LANE), F32), NT, x_first=False)
            dqt_sc[...] = jnp.zeros_like(dqt_sc)

        kjt = [_transpose_bf16(k_ref[hh]) for hh in range(g)]
        kpos = lax.broadcasted_iota(jnp.int32, (t, tq), 0) + j * t
        qpos = lax.broadcasted_iota(jnp.int32, (t, tq), 1)

        def chunk(i, carry, diagonal):
            start = pl.multiple_of(i * tq, tq)
            cols = pl.ds(start, tq)
            scs = [_dot(k_ref[hh], qt_sc[hh, :, cols]) for hh in range(g)]
            dps = [_dot(kv_ref[hh], dot_sc[hh, :, cols]) for hh in range(g)]
            pds = []
            for hh in range(g):
                p = jnp.exp2(scs[hh] - lse_ref[hh, :, cols])
                if diagonal:
                    p = jnp.where(qpos + start >= kpos, p, 0.0)
                ds = (p * (dps[hh] - delta_sc[hh, 0:1, cols])).astype(BF16)
                pds.append((p.astype(BF16), ds))
            out = []
            for hh in range(g):
                dk, dv = carry[hh]
                dv = dv + _dot(pds[hh][0], do_ref[pl.ds(start, tq), hh * LANE:(hh + 1) * LANE])
                dk = dk + _dot(pds[hh][1], qs_sc[hh, pl.ds(start, tq), :])
                dqt_sc[hh, :, cols] += _dot(kjt[hh], pds[hh][1])
                out.append((dk, dv))
            return tuple(out)

        zero = jnp.zeros((t, LANE), F32)
        first = lax.div(j, BWD_QW)
        carry = chunk(first, tuple((zero, zero) for _ in range(g)), True)
        carry = lax.fori_loop(first + 1, s // tq, lambda i, c: chunk(i, c, False), carry)
        for hh in range(g):
            dk_ref[hh] = carry[hh][0].astype(BF16)
            dv_ref[hh] = carry[hh][1].astype(BF16)

        @pl.when(j == n - 1)
        def _():
            for hh in range(g):
                dq_ref[hh] = (dqt_sc[hh] * ATT_SCALE).T

    whole = pl.BlockSpec((g, s, LANE), lambda h, j: (h, 0, 0))
    kspec = pl.BlockSpec((g, t, LANE), lambda h, j: (h, j, 0))
    ospec = pl.BlockSpec((s, g * LANE), lambda h, j: (0, h))
    res = pl.pallas_call(
        body, name="flash_bwd_exchange" if nx else "flash_bwd", grid=(HEADS // g, n),
        out_shape=[jax.ShapeDtypeStruct((HEADS, s, LANE), F32), jax.ShapeDtypeStruct((HEADS, s, LANE), BF16),
                   jax.ShapeDtypeStruct((HEADS, s, LANE), BF16)] + [jax.ShapeDtypeStruct(a.shape, a.dtype) for a in (pending or [])],
        in_specs=[whole, kspec, kspec, ospec, ospec, pl.BlockSpec((g, 1, s), lambda h, j: (h, 0, 0))] + [ANY] * nx,
        out_specs=[whole, kspec, kspec] + [ANY] * nx,
        scratch_shapes=[pltpu.VMEM((g, LANE, s), BF16), pltpu.VMEM((g, LANE, s), BF16), pltpu.VMEM((g, 8, s), F32),
                        pltpu.VMEM((g, LANE, s), F32), pltpu.VMEM((g, s, LANE), BF16)] + (_comm_scratch(nx) if nx else []),
        compiler_params=_cp("arbitrary", "arbitrary"),
    )(q, k, kv, cat, dcat, lse, *(pending or []))
    return res[0], res[1], res[2], list(res[3:])


def _conv3(ext, w_ref, ts):
    return (w_ref[0:1, :] * ext[pl.ds(HALO - 2, ts), :] + w_ref[1:2, :] * ext[pl.ds(HALO - 1, ts), :]
            + w_ref[2:3, :] * ext[pl.ds(HALO, ts), :])


def _conv3_rows(ext, w_ref, r):
    return (w_ref[0:1, :] * ext[pl.ds(HALO - 2 + r, ROW_CHUNK), :] + w_ref[1:2, :] * ext[pl.ds(HALO - 1 + r, ROW_CHUNK), :]
            + w_ref[2:3, :] * ext[pl.ds(HALO + r, ROW_CHUNK), :])


def _conv3_t(ext2, w_ref, ts):
    return (w_ref[0:1, :] * ext2[pl.ds(2, ts), :] + w_ref[1:2, :] * ext2[pl.ds(1, ts), :] + w_ref[2:3, :] * ext2[pl.ds(0, ts), :])


def _sconv_fwd(proj, w):
    s = proj.shape[0]
    ts = min(ROW_TILE, s)

    def body(b_ref, c_ref, h_ref, hc_ref, hh_ref, w_ref, o_ref, ext):
        i = pl.program_id(0)
        ext[0:HALO, :] = hc_ref[...].astype(F32) * hh_ref[...].astype(F32) * (i > 0).astype(F32)
        ext[HALO:HALO + ts, :] = c_ref[...].astype(F32) * h_ref[...].astype(F32)
        o_ref[...] = (b_ref[...].astype(F32) * _conv3(ext, w_ref, ts)).astype(BF16)

    def col(cb):
        return pl.BlockSpec((ts, SC), lambda i: (i, cb))

    def halo(cb):
        return pl.BlockSpec((HALO, SC), lambda i: (_prev_halo(i, ts), cb))

    return pl.pallas_call(
        body, name="sconv_fwd", grid=(s // ts,), out_shape=jax.ShapeDtypeStruct((s, SC), BF16),
        in_specs=[col(2), col(3), col(4), halo(3), halo(4), pl.BlockSpec((3, SC), lambda i: (0, 0))],
        out_specs=pl.BlockSpec((ts, SC), lambda i: (i, 0)), scratch_shapes=[pltpu.VMEM((ts + HALO, SC), F32)],
        compiler_params=_cp("parallel"),
    )(proj, proj, proj, proj, proj, w)


def _sconv_bwd(proj, dcat, w):
    s = proj.shape[0]
    ts = min(ROW_TILE, s)
    n = s // ts

    def body(b_ref, c_ref, h_ref, hc_ref, hh_ref, dy_ref, ndy_ref, nb_ref, w_ref, d_ref, dw_ref, ext, ext2):
        i = pl.program_id(0)
        db_ref, dc_ref, dh_ref = d_ref.at[:, 0:SC], d_ref.at[:, SC:2 * SC], d_ref.at[:, 2 * SC:3 * SC]

        @pl.when(i == 0)
        def _():
            dw_ref[...] = jnp.zeros_like(dw_ref)

        cv, hv, bv = c_ref[...].astype(F32), h_ref[...].astype(F32), b_ref[...].astype(F32)
        ext[0:HALO, :] = hc_ref[...].astype(F32) * hh_ref[...].astype(F32) * (i > 0).astype(F32)
        ext[HALO:HALO + ts, :] = cv * hv
        dy = dy_ref[...].astype(F32)
        db_ref[...] = (dy * _conv3(ext, w_ref, ts)).astype(BF16)
        dyb = dy * bv
        ext2[0:ts, :] = dyb
        ext2[ts:ts + HALO, :] = ndy_ref[...].astype(F32) * nb_ref[...].astype(F32) * (i < n - 1).astype(F32)
        dg = _conv3_t(ext2, w_ref, ts)
        dc_ref[...] = (dg * hv).astype(BF16)
        dh_ref[...] = (dg * cv).astype(BF16)
        for kk in range(3):
            dw_ref[kk:kk + 1, :] += _rowsum(dyb * ext[pl.ds(HALO - 2 + kk, ts), :])

    def col(cb):
        return pl.BlockSpec((ts, SC), lambda i: (i, cb))

    def halo(cb):
        return pl.BlockSpec((HALO, SC), lambda i: (_prev_halo(i, ts), cb))

    def nxt(cb):
        return pl.BlockSpec((HALO, SC), lambda i: (_next_halo(i, ts, n), cb))

    out = jax.ShapeDtypeStruct((s, 3 * SC), BF16)
    o0 = pl.BlockSpec((ts, 3 * SC), lambda i: (i, 0))
    return pl.pallas_call(
        body, name="sconv_bwd", grid=(n,), out_shape=(out, jax.ShapeDtypeStruct((3, SC), F32)),
        in_specs=[col(2), col(3), col(4), halo(3), halo(4), col(4), nxt(4), nxt(2), pl.BlockSpec((3, SC), lambda i: (0, 0))],
        out_specs=(o0, pl.BlockSpec((3, SC), lambda i: (0, 0))),
        scratch_shapes=[pltpu.VMEM((ts + HALO, SC), F32), pltpu.VMEM((ts + HALO, SC), F32)], compiler_params=_cp("arbitrary"),
    )(proj, proj, proj, proj, proj, dcat, dcat, proj, w)


def _ffn_stage(ext, u_ref, halo_ref, i, ts):
    ext[0:HALO, :] = halo_ref[...].astype(F32) * (i > 0).astype(F32)
    ext[HALO:HALO + ts, :] = u_ref[...].astype(F32)


def _ffn_specs(ts):
    cur = pl.BlockSpec((2, None, ts, FB), lambda j, i: (0, j, i, 0))
    halo = pl.BlockSpec((2, None, HALO, FB), lambda j, i: (0, j, _prev_halo(i, ts), 0))
    w = pl.BlockSpec((2, None, 3, FB), lambda j, i: (0, j, 0, 0))
    b = pl.BlockSpec((2, None, 1, FB), lambda j, i: (0, j, 0, 0))
    return cur, halo, w, b


def _ffn_act(upre, fcw, fcb):
    s = upre.shape[1]
    ts = min(ROW_TILE, s)

    def body(u_ref, halo_ref, w_ref, b_ref, o_ref, ext_g, ext_u):
        i = pl.program_id(1)
        _ffn_stage(ext_g, u_ref.at[0], halo_ref.at[0], i, ts)
        _ffn_stage(ext_u, u_ref.at[1], halo_ref.at[1], i, ts)
        for r in range(0, ts, ROW_CHUNK):
            gate = b_ref[0] + _conv3_rows(ext_g, w_ref.at[0], r)
            up = b_ref[1] + _conv3_rows(ext_u, w_ref.at[1], r)
            o_ref[pl.ds(r, ROW_CHUNK), :] = (gate * _sig(gate) * up).astype(BF16)

    cur, halo, w, b = _ffn_specs(ts)
    u4 = upre.reshape(2, 4, s, FB)
    return pl.pallas_call(
        body, name="ffn_act", grid=(4, s // ts), out_shape=jax.ShapeDtypeStruct((4, s, FB), BF16),
        in_specs=[cur, halo, w, b], out_specs=pl.BlockSpec((None, ts, FB), lambda j, i: (j, i, 0)),
        scratch_shapes=[pltpu.VMEM((ts + HALO, FB), F32), pltpu.VMEM((ts + HALO, FB), F32)], compiler_params=_cp("parallel", "parallel"),
    )(u4, u4, fcw.reshape(2, 4, 3, FB), fcb.reshape(2, 4, 1, FB))


def _ffn_bwd(upre, dact, fcw, fcb):
    s = upre.shape[1]
    ts = min(ROW_TILE, s)
    n = s // ts
    te = ts + HALO

    def body(u_ref, halo_ref, nxt_ref, w_ref, b_ref, da_ref, nda_ref, dup_ref, db_ref, dw_ref, ext_g, ext_u, ext_da, du_g, du_u, acc):
        i = pl.program_id(1)

        @pl.when(i == 0)
        def _():
            db_ref[...] = jnp.zeros_like(db_ref)
            dw_ref[...] = jnp.zeros_like(dw_ref)

        more = (i < n - 1).astype(F32)
        for idx, ext in ((0, ext_g), (1, ext_u)):
            _ffn_stage(ext, u_ref.at[idx], halo_ref.at[idx], i, ts)
            ext[HALO + ts:HALO + te, :] = nxt_ref[idx].astype(F32) * more
        ext_da[0:ts, :] = da_ref[...].astype(F32)
        ext_da[ts:te, :] = nda_ref[...].astype(F32) * more
        acc[...] = jnp.zeros_like(acc)
        for r in range(0, te, ROW_CHUNK):
            rows = pl.ds(r, ROW_CHUNK)
            taps = [[ext[pl.ds(HALO - 2 + kk + r, ROW_CHUNK), :] for kk in range(3)] for ext in (ext_g, ext_u)]
            gate = b_ref[0] + w_ref[0, 0:1, :] * taps[0][0] + w_ref[0, 1:2, :] * taps[0][1] + w_ref[0, 2:3, :] * taps[0][2]
            up = b_ref[1] + w_ref[1, 0:1, :] * taps[1][0] + w_ref[1, 1:2, :] * taps[1][1] + w_ref[1, 2:3, :] * taps[1][2]
            sg = _sig(gate)
            da = ext_da[rows, :]
            dgate = da * up * sg * (1.0 + gate * (1.0 - sg))
            dup = da * gate * sg
            du_g[rows, :] = dgate
            du_u[rows, :] = dup
            if r < ts:
                acc[0] += dgate
                acc[1] += dup
                for kk in range(3):
                    acc[2 + kk] += dgate * taps[0][kk]
                    acc[5 + kk] += dup * taps[1][kk]
        for r in range(0, ts, ROW_CHUNK):
            for idx, du in ((0, du_g), (1, du_u)):
                dupre = (w_ref[idx, 2:3, :] * du[pl.ds(r, ROW_CHUNK), :] + w_ref[idx, 1:2, :] * du[pl.ds(r + 1, ROW_CHUNK), :]
                         + w_ref[idx, 0:1, :] * du[pl.ds(r + 2, ROW_CHUNK), :])
                dup_ref[idx, pl.ds(r, ROW_CHUNK), :] = dupre.astype(BF16)
        for idx in range(2):
            db_ref[idx] += _rowsum(acc[idx])
            for kk in range(3):
                dw_ref[idx, kk:kk + 1, :] += _rowsum(acc[2 + 3 * idx + kk])

    cur, halo, w, b = _ffn_specs(ts)
    nxt = pl.BlockSpec((2, None, HALO, FB), lambda j, i: (0, j, _next_halo(i, ts, n), 0))
    u4 = upre.reshape(2, 4, s, FB)
    dupre, db, dw = pl.pallas_call(
        body, name="ffn_bwd", grid=(4, n),
        out_shape=(jax.ShapeDtypeStruct((2, 4, s, FB), BF16), jax.ShapeDtypeStruct((2, 4, 1, FB), F32),
                   jax.ShapeDtypeStruct((2, 4, 3, FB), F32)),
        in_specs=[cur, halo, nxt, w, b, pl.BlockSpec((None, ts, FB), lambda j, i: (j, i, 0)),
                  pl.BlockSpec((None, HALO, FB), lambda j, i: (j, _next_halo(i, ts, n), 0))],
        out_specs=(cur, b, w),
        scratch_shapes=[pltpu.VMEM((te + HALO, FB), F32), pltpu.VMEM((te + HALO, FB), F32), pltpu.VMEM((te, FB), F32),
                        pltpu.VMEM((te, FB), F32), pltpu.VMEM((te, FB), F32), pltpu.VMEM((8, ROW_CHUNK, FB), F32)],
        compiler_params=_cp("parallel", "arbitrary"),
    )(u4, u4, u4, fcw.reshape(2, 4, 3, FB), fcb.reshape(2, 4, 1, FB), dact, dact)
    return dupre.reshape(NDEV, s, FB), db.reshape(NDEV, 1, FB), dw.reshape(NDEV, 3, FB)


def _softplus(v):
    e = jnp.exp(-jnp.abs(v))
    return jnp.maximum(v, 0.0) + jnp.where(e < 1e-4, e * (1.0 - 0.5 * e), jnp.log(1.0 + e))


def _ssd_consts():
    L = SSD_L
    r = lax.broadcasted_iota(jnp.int32, (L, L), 0)
    c = lax.broadcasted_iota(jnp.int32, (L, L), 1)
    tri = r >= c
    er = lax.broadcasted_iota(jnp.int32, (LANE, SSD_DIM), 0)
    ec = lax.broadcasted_iota(jnp.int32, (LANE, SSD_DIM), 1)
    expand = ((ec >= er * 64) & (ec < er * 64 + 64)).astype(F32)
    return tri, expand


def _ssd_conv4(ext, cw_ref, cb_ref):
    L = SSD_L
    pre = cb_ref[...] + cw_ref[0:1, :] * ext[pl.ds(HALO - 3, L), :]
    for kk in range(1, 4):
        pre = pre + cw_ref[kk:kk + 1, :] * ext[pl.ds(HALO - 3 + kk, L), :]
    return pre


def _ssd_common(xbc_ref, halo_ref, dt_ref, cw_ref, cb_ref, dtb_ref, alog_ref, ext, first):
    L = SSD_L
    tri, expand = _ssd_consts()
    ext[0:HALO, :] = halo_ref[...].astype(F32) * (1.0 - first.astype(F32))
    ext[HALO:HALO + L, :] = xbc_ref[...].astype(F32)
    pre = _ssd_conv4(ext, cw_ref, cb_ref)
    sg = _sig(pre)
    act = pre * sg
    lane = lax.broadcasted_iota(jnp.int32, (1, LANE), 1)
    m4 = lane < SSD_H
    raw = dt_ref[...].astype(F32) + dtb_ref[...]
    dtv = jnp.where(m4, _softplus(raw), 0.0)
    av = jnp.where(m4, -jnp.exp(alog_ref[...]), 0.0)
    adt = dtv * av
    acs = _dot01(adt, tri, x_first=False)
    acs_b = _dot01(acs, expand)
    dt_b = _dot01(dtv, expand)
    return dict(tri=tri, expand=expand, pre=pre, sg=sg, act=act, raw=raw, dtv=dtv, av=av, m4=m4, acs=acs, acs_b=acs_b,
                dt_b=dt_b, lane=lane)


def _head_terms(cm, h):
    L = SSD_L
    acs, tri = cm["acs"], cm["tri"]
    lane_l = lax.broadcasted_iota(jnp.int32, (L, LANE), 1)
    sub_l = lax.broadcasted_iota(jnp.int32, (LANE, L), 0)
    col = jnp.sum(jnp.where(lane_l == h, acs, 0.0), axis=1, keepdims=True)
    row = jnp.sum(jnp.where(sub_l == h, acs.T, 0.0), axis=0, keepdims=True)
    dec = jnp.where(tri, jnp.exp(jnp.where(tri, col - row, NEG)), 0.0)
    rowi = lax.broadcasted_iota(jnp.int32, (L, 1), 0)
    last = jnp.sum(jnp.where(rowi == L - 1, col, 0.0), axis=0, keepdims=True)
    dte = jnp.exp(last - col)
    return col, dec, last, dte


def _ssd_fwd(proj, cw, cb, dtb, alog, dvec, nw):
    s = proj.shape[0]
    L = SSD_L
    nc = s // L

    def body(z_ref, xbc_ref, halo_ref, dt_ref, cw_ref, cb_ref, dtb_ref, alog_ref, d_ref, nw_ref, y_ref, ypre_ref, st_ref, ext, state):
        i = pl.program_id(0)

        @pl.when(i == 0)
        def _():
            state[...] = jnp.zeros_like(state)

        cm = _ssd_common(xbc_ref, halo_ref, dt_ref, cw_ref, cb_ref, dtb_ref, alog_ref, ext, i == 0)
        act = cm["act"]
        xs = act[:, 0:256]
        bm = (act[:, 256:384], act[:, 384:512])
        cmat = (act[:, 512:640].astype(BF16), act[:, 640:768].astype(BF16))
        xdt = xs * cm["dt_b"]
        prev = state[...]
        st_ref[...] = prev
        prev_bf = prev.astype(BF16)
        gm = [_dot(cmat[g], bm[g].astype(BF16), NT) for g in range(2)]
        lane2 = lax.broadcasted_iota(jnp.int32, (1, SSD_DIM), 1)
        rows2 = lax.broadcasted_iota(jnp.int32, (SSD_DIM, 1), 0)
        ydiag = jnp.zeros((L, SSD_DIM), F32)
        contrib = jnp.zeros((SSD_DIM, LANE), F32)
        cd_rows = jnp.zeros((SSD_DIM, 1), F32)
        for h in range(SSD_H):
            g = h // 2
            col, dec, last, dte = _head_terms(cm, h)
            mh = (lane2 >= 64 * h) & (lane2 < 64 * h + 64)
            xm = jnp.where(mh, xdt, 0.0).astype(BF16)
            ydiag += _dot((gm[g] * dec).astype(BF16), xm)
            contrib += _dot(xm, (bm[g] * dte).astype(BF16), TN)
            cd_rows += jnp.where((rows2 >= 64 * h) & (rows2 < 64 * h + 64), jnp.exp(last), 0.0)
        yo = jnp.where(lane2 < 128, _dot(cmat[0], prev_bf, NT), _dot(cmat[1], prev_bf, NT))
        y = ydiag + yo * jnp.exp(cm["acs_b"]) + xs * d_ref[...]
        state[...] = prev * cd_rows + contrib
        ypre_ref[...] = y
        zz = z_ref[...].astype(F32)
        gt = y * zz * _sig(zz)
        y_ref[...] = (gt * lax.rsqrt(jnp.mean(gt * gt, axis=-1, keepdims=True) + EPS) * nw_ref[...]).astype(BF16)

    def vec(w):
        return pl.BlockSpec((1, w), lambda i: (0, 0))

    return pl.pallas_call(
        body, name="ssd_fwd", grid=(nc,),
        out_shape=(jax.ShapeDtypeStruct((s, SSD_DIM), BF16), jax.ShapeDtypeStruct((s, SSD_DIM), F32),
                   jax.ShapeDtypeStruct((nc, SSD_DIM, LANE), F32)),
        in_specs=[pl.BlockSpec((L, SSD_DIM), lambda i: (i, 5)), pl.BlockSpec((L, SSD_CONV), lambda i: (i, 2)),
                  pl.BlockSpec((HALO, SSD_CONV), lambda i: (_prev_halo(i, L), 2)), pl.BlockSpec((L, LANE), lambda i: (i, 18)),
                  pl.BlockSpec((4, SSD_CONV), lambda i: (0, 0)), vec(SSD_CONV), vec(LANE), vec(LANE), vec(SSD_DIM), vec(SSD_DIM)],
        out_specs=(pl.BlockSpec((L, SSD_DIM), lambda i: (i, 0)), pl.BlockSpec((L, SSD_DIM), lambda i: (i, 0)),
                   pl.BlockSpec((None, SSD_DIM, LANE), lambda i: (i, 0, 0))),
        scratch_shapes=[pltpu.VMEM((L + HALO, SSD_CONV), F32), pltpu.VMEM((SSD_DIM, LANE), F32)], compiler_params=_cp("arbitrary"),
    )(proj, proj, proj, proj, cw, cb, dtb, alog, dvec, nw)


def _ssd_bwd(proj, dcat, ypre, states, cw, cb, dtb, alog, dvec, nw):
    s = proj.shape[0]
    L = SSD_L
    nc = s // L

    def body(z_ref, xbc_ref, halo_ref, dt_ref, dy_ref, ypre_ref, st_ref, cw_ref, cb_ref, dtb_ref, alog_ref, d_ref, nw_ref,
             din_ref, dcw_ref, dcb_ref, ddtb_ref, dalog_ref, dd_ref, dnw_ref, ext, ext2, carry, dstate, ddl):
        i = pl.program_id(0)
        dz_ref, dxbc_ref = din_ref.at[:, 0:SSD_DIM], din_ref.at[:, SSD_DIM:SSD_DIM + SSD_CONV]
        ddt_ref = din_ref.at[:, SSD_DIM + SSD_CONV:SSD_DIM + SSD_CONV + LANE]
        r = nc - 1 - i

        @pl.when(i == 0)
        def _():
            for ref in (dcw_ref, dcb_ref, ddtb_ref, dalog_ref, dd_ref, dnw_ref, carry, dstate, ddl):
                ref[...] = jnp.zeros_like(ref)

        cm = _ssd_common(xbc_ref, halo_ref, dt_ref, cw_ref, cb_ref, dtb_ref, alog_ref, ext, r == 0)
        tri, expand, act = cm["tri"], cm["expand"], cm["act"]
        xs = act[:, 0:256]
        bm = (act[:, 256:384], act[:, 384:512])
        cmat = (act[:, 512:640], act[:, 640:768])
        bm_bf = [v.astype(BF16) for v in bm]
        cm_bf = [v.astype(BF16) for v in cmat]
        dt_b = cm["dt_b"]
        xdt = xs * dt_b
        xdt_bf = xdt.astype(BF16)
        ea_b = jnp.exp(cm["acs_b"])
        prev = st_ref[...]
        prev_bf = prev.astype(BF16)
        lane2 = lax.broadcasted_iota(jnp.int32, (1, SSD_DIM), 1)
        rows2 = lax.broadcasted_iota(jnp.int32, (SSD_DIM, 1), 0)
        lane_l = lax.broadcasted_iota(jnp.int32, (L, LANE), 1)
        rowi = lax.broadcasted_iota(jnp.int32, (L, 1), 0)

        y = ypre_ref[...]
        zz = z_ref[...].astype(F32)
        sz = _sig(zz)
        gt = y * zz * sz
        dgt, dwt = _rms_bwd_math(gt, nw_ref[...], dy_ref[...].astype(F32))
        dnw_ref[...] += _rowsum(dwt)
        dy = dgt * zz * sz
        dz_ref[...] = (dgt * y * sz * (1.0 + zz * (1.0 - sz))).astype(BF16)

        ddl[0:1, :] += _rowsum(dy * xs)
        dxs = dy * d_ref[...]

        yo = jnp.where(lane2 < 128, _dot(cm_bf[0], prev_bf, NT), _dot(cm_bf[1], prev_bf, NT))
        dacs_b = dy * yo * ea_b
        dyo = dy * ea_b
        dyo_g = (jnp.where(lane2 < 128, dyo, 0.0).astype(BF16), jnp.where(lane2 >= 128, dyo, 0.0).astype(BF16))
        dc = [_dot(dyo_g[g], prev_bf) for g in range(2)]
        dprev = _dot(dyo_g[0], cm_bf[0], TN) + _dot(dyo_g[1], cm_bf[1], TN)

        gm = [_dot(cm_bf[g], bm_bf[g], NT) for g in range(2)]
        dgm = [jnp.zeros((L, L), F32), jnp.zeros((L, L), F32)]
        db = [jnp.zeros((L, LANE), F32), jnp.zeros((L, LANE), F32)]
        dxdt = jnp.zeros((L, SSD_DIM), F32)
        dacs = jnp.zeros((L, LANE), F32)
        dlast = jnp.zeros((1, LANE), F32)
        cd_rows = jnp.zeros((SSD_DIM, 1), F32)
        dst = dstate[...]
        dst_bf = dst.astype(BF16)
        dsp = dst * prev
        ones = jnp.ones((L, LANE), F32)
        for h in range(SSD_H):
            g = h // 2
            col, dec, last, dte = _head_terms(cm, h)
            mh = (lane2 >= 64 * h) & (lane2 < 64 * h + 64)
            rh = (rows2 >= 64 * h) & (rows2 < 64 * h + 64)
            sc = gm[g] * dec
            xm = jnp.where(mh, xdt, 0.0).astype(BF16)
            dym = jnp.where(mh, dy, 0.0).astype(BF16)
            dsc = _dot(dym, xdt_bf, NT)
            dxdt += _dot(sc.astype(BF16), dym, TN)
            dgm[g] += dsc * dec
            dd = dsc * sc
            rs = jnp.sum(dd, axis=1, keepdims=True)
            cs = _dot01(dd, ones, TN)
            dacs += jnp.where(lane_l == h, rs - cs, 0.0)
            bd = (bm[g] * dte).astype(BF16)
            dxdt += jnp.where(mh, _dot(bd, dst_bf, NT), 0.0)
            dbd = _dot(xm, dst_bf)
            db[g] += dbd * dte
            tt = jnp.sum(dbd * bm[g], axis=1, keepdims=True) * dte
            dacs += jnp.where(lane_l == h, -tt, 0.0)
            cdh = jnp.exp(last)
            dcd = jnp.sum(jnp.sum(jnp.where(rh, dsp, 0.0), axis=1, keepdims=True), axis=0, keepdims=True)
            dlast += jnp.where(cm["lane"] == h, jnp.sum(tt, axis=0, keepdims=True) + dcd * cdh, 0.0)
            cd_rows += jnp.where(rh, cdh, 0.0)
        dacs += jnp.where(rowi == L - 1, dlast, 0.0)
        dacs += _dot01(dacs_b, expand, NT)
        dstate[...] = dprev + dst * cd_rows

        for g in range(2):
            dgb = dgm[g].astype(BF16)
            dc[g] += _dot(dgb, bm_bf[g])
            db[g] += _dot(dgb, cm_bf[g], TN)

        dadt = _dot01(dacs, tri, TN, x_first=False)
        ddtv = dadt * cm["av"] + _dot01(dxdt * xs, expand, NT)
        dalog_ref[...] += _rowsum(dadt * cm["dtv"]) * cm["av"]
        dxs += dxdt * dt_b
        draw = jnp.where(cm["m4"], ddtv * _sig(cm["raw"]), 0.0)
        ddtb_ref[...] += _rowsum(draw)
        ddt_ref[...] = draw.astype(BF16)

        dact = jnp.concatenate([dxs, db[0], db[1], dc[0], dc[1]], axis=1)
        sg, pre = cm["sg"], cm["pre"]
        dpre = dact * sg * (1.0 + pre * (1.0 - sg))
        dcb_ref[...] += _rowsum(dpre)
        for kk in range(4):
            dcw_ref[kk:kk + 1, :] += _rowsum(dpre * ext[pl.ds(HALO - 3 + kk, L), :])
        ext2[0:L, :] = dpre
        ext2[L:L + HALO, :] = carry[...]
        dx = cw_ref[3:4, :] * ext2[pl.ds(0, L), :]
        for kk in range(3):
            dx = dx + cw_ref[kk:kk + 1, :] * ext2[pl.ds(3 - kk, L), :]
        dxbc_ref[...] = dx.astype(BF16)
        carry[...] = dpre[0:HALO, :]

        @pl.when(i == nc - 1)
        def _():
            dd_ref[...] = _dot01(ddl[...], expand, NT)

    def vec(w):
        return pl.BlockSpec((1, w), lambda i: (0, 0))

    def rv(i):
        return nc - 1 - i

    return pl.pallas_call(
        body, name="ssd_bwd", grid=(nc,),
        out_shape=(jax.ShapeDtypeStruct((s, SSD_DIM + SSD_CONV + LANE), BF16),
                   jax.ShapeDtypeStruct((4, SSD_CONV), F32), jax.ShapeDtypeStruct((1, SSD_CONV), F32), jax.ShapeDtypeStruct((1, LANE), F32),
                   jax.ShapeDtypeStruct((1, LANE), F32), jax.ShapeDtypeStruct((8, LANE), F32), jax.ShapeDtypeStruct((1, SSD_DIM), F32)),
        in_specs=[pl.BlockSpec((L, SSD_DIM), lambda i: (rv(i), 5)), pl.BlockSpec((L, SSD_CONV), lambda i: (rv(i), 2)),
                  pl.BlockSpec((HALO, SSD_CONV), lambda i: (_prev_halo(rv(i), L), 2)), pl.BlockSpec((L, LANE), lambda i: (rv(i), 18)),
                  pl.BlockSpec((L, SSD_DIM), lambda i: (rv(i), 5)), pl.BlockSpec((L, SSD_DIM), lambda i: (rv(i), 0)),
                  pl.BlockSpec((None, SSD_DIM, LANE), lambda i: (rv(i), 0, 0)),
                  pl.BlockSpec((4, SSD_CONV), lambda i: (0, 0)), vec(SSD_CONV), vec(LANE), vec(LANE), vec(SSD_DIM), vec(SSD_DIM)],
        out_specs=(pl.BlockSpec((L, SSD_DIM + SSD_CONV + LANE), lambda i: (rv(i), 0)), pl.BlockSpec((4, SSD_CONV), lambda i: (0, 0)), vec(SSD_CONV),
                   vec(LANE), vec(LANE), pl.BlockSpec((8, LANE), lambda i: (0, 0)), vec(SSD_DIM)),
        scratch_shapes=[pltpu.VMEM((L + HALO, SSD_CONV), F32), pltpu.VMEM((L + HALO, SSD_CONV), F32), pltpu.VMEM((HALO, SSD_CONV), F32),
                        pltpu.VMEM((SSD_DIM, LANE), F32), pltpu.VMEM((8, SSD_DIM), F32)],
        compiler_params=_cp("arbitrary"),
    )(proj, proj, proj, proj, dcat, ypre, states, cw, cb, dtb, alog, dvec, nw)


def _adamw(parts, w, m, v, name):
    nl, r, c = w.shape
    tr = r
    for cand in (256, 128, 64, 32, 16, 8):
        if r % cand == 0 and (cand * c * 4) <= 2 * 1024 * 1024:
            tr = cand
            break
    c1 = 1.0 - B1 ** STEP
    c2 = 1.0 - B2 ** STEP

    def body(p_ref, w_ref, m_ref, v_ref, g_ref, d_ref, nm_ref, nv_ref):
        g = p_ref[0].astype(F32)
        for dev in range(1, NDEV):
            g = g + p_ref[dev].astype(F32)
        mn = B1 * m_ref[...] + (1.0 - B1) * g
        vn = B2 * v_ref[...] + (1.0 - B2) * (g * g)
        g_ref[...] = g
        nm_ref[...] = mn
        nv_ref[...] = vn
        d_ref[...] = -LR * ((mn / c1) / (jnp.sqrt(vn / c2) + AEPS) + WD * w_ref[...])

    blk = pl.BlockSpec((None, tr, c), lambda l, i: (l, i, 0))
    out = jax.ShapeDtypeStruct((nl, r, c), F32)
    return pl.pallas_call(
        body, name=name, grid=(nl, r // tr), out_shape=(out, out, out, out),
        in_specs=[pl.BlockSpec((NDEV, None, tr, c), lambda l, i: (0, l, i, 0)), blk, blk, blk], out_specs=(blk, blk, blk, blk),
        compiler_params=_cp("parallel", "parallel"),
    )(parts, w, m, v)


def _pad_win(w):
    z = lambda n: jnp.zeros((w.shape[0], n), w.dtype)
    return jnp.concatenate([w[:, :384], z(64), w[:, 384:416], z(32), w[:, 416:], z(124)], axis=1)


def _unpad_win(g_mla, g_sc, g_ssd):
    return jnp.concatenate([g_mla[:, :384], g_mla[:, 448:480], g_sc, g_ssd[:, :1028]], axis=1)


def _pad_wout(w):
    att = jnp.pad(w[:512].reshape(HEADS, 64, D), ((0, 0), (64, 0), (0, 0))).reshape(HEADS * LANE, D)
    return jnp.concatenate([att, w[512:]], axis=0)


def _unpad_wout(g_att, g_conv, g_ssd):
    att = g_att.reshape(HEADS, LANE, D)[:, 64:, :].reshape(512, D)
    return jnp.concatenate([att, g_conv, g_ssd], axis=0)


def _lanes(v, n=LANE):
    return jnp.pad(v, (0, n - v.shape[0])).reshape(1, n)


def _prep_ffn(big):
    return {"wout": _pad_wout(big["w_out"].reshape(1024, D)), "wup": big["ffn_w_up"], "fcw": big["ffn_conv_w"].astype(F32),
            "wdown": big["ffn_w_down"].reshape(4, FB, D)}


def _prep_layer(big, small, l):
    p = _prep_ffn(big) if "w_out" in big else {}
    p["win"] = _pad_win(big["w_in"].reshape(D, 2212))
    p["wq"] = jnp.pad(big["mla_w_q_up"], ((0, 0), (0, 0), (0, LANE - 96)))
    p["wkv"] = big["mla_w_kv_up"]
    p["scw"] = big["sc_conv_w"].astype(F32).transpose(1, 0, 2).reshape(3, SC)
    p["ssdcw"] = big["ssd_conv_w"].astype(F32).transpose(1, 0, 2).reshape(4, SSD_CONV)
    for nm in ("norm_mix_pre", "norm_mix_post", "norm_ffn_pre", "norm_ffn_post", "mla_q_norm", "mla_kv_norm", "ssd_conv_b", "ssd_norm"):
        p[nm] = small[nm][l].reshape(1, -1)
    p["dtb"] = _lanes(small["ssd_dt_bias"][l])
    p["alog"] = _lanes(small["ssd_a_log"][l])
    p["dvec"] = jnp.repeat(small["ssd_d"][l], 64).reshape(1, SSD_DIM)
    p["fcb"] = small["ffn_conv_b"][l].reshape(NDEV, 1, FB)
    return p


def _rope_tables(positions):
    inv_freq = 1.0 / (ROPE_THETA ** (jnp.arange(0, ROPE, 2, dtype=F32) / ROPE))
    ang = positions.astype(F32)[:, None] * inv_freq
    cos, sin = jnp.cos(ang), jnp.sin(ang)
    s = positions.shape[0]
    z = lambda n: jnp.zeros((s, n), F32)
    tc = jnp.concatenate([jnp.ones((s, 64), F32), cos, cos, z(32)], axis=1)
    ta = jnp.concatenate([z(64), -sin, z(48)], axis=1)
    tb = jnp.concatenate([z(80), sin, z(32)], axis=1)
    return tc, ta, tb


def _layer_fwd(xv, p, tabs, prefetch=None, prep_rest=None, h=None, next_norm=None):
    if h is None:
        h = _rms(xv, p["norm_mix_pre"], BF16, "rms_pre")
    proj = _mm_rows("in_proj", h, p["win"], BF16, NN)
    q, k, kv = _mla_prep(proj, tabs, p["mla_q_norm"], p["mla_kv_norm"], p["wq"], p["wkv"])
    o, lse, gathered = _flash_fwd(q, k, kv, prefetch)
    if prep_rest is not None:
        p = {**p, **prep_rest(gathered)}
    yconv = _sconv_fwd(proj, p["scw"])
    yssd, ypre, states = _ssd_fwd(proj, p["ssdcw"], p["ssd_conv_b"], p["dtb"], p["alog"], p["dvec"], p["ssd_norm"])
    mixed = _mm_sum("out_proj", [o, yconv, yssd], [p["wout"][:HEADS * LANE], p["wout"][HEADS * LANE:HEADS * LANE + SC],
                                                  p["wout"][HEADS * LANE + SC:]], NN)
    x1, h2 = _add_rms(xv, mixed, p["norm_mix_post"], "add_rms", p["norm_ffn_pre"])
    upre = _mm_up(h2, p["wup"])
    act = _ffn_act(upre, p["fcw"], p["fcb"])
    f = _mm_down(act, p["wdown"])
    x2 = _add_rms(x1, f, p["norm_ffn_post"], "add_rms", next_norm)
    saved = dict(x=xv, h=h, proj=proj, q=q, k=k, kv=kv, lse=lse, ypre=ypre, states=states, o=o, yconv=yconv, yssd=yssd, mixed=mixed, x1=x1, h2=h2,
                 upre=upre, act=act, f=f)
    return x2, saved, p, gathered


def _pack_grads(grads, group_ids):
    return [_group_pack(GROUPS[gi], lambda n: grads[n].reshape((NDEV,) + _rows2(n, True)), (NDEV,)) for gi in group_ids]


def _layer_bwd(dx2, sv, p, tabs, exchange=False, pending=None, head=None, below=None):
    df, g_nfpo = head if head is not None else _rms_bwd(sv["f"], p["norm_ffn_post"], dx2, None, BF16, "rms_bwd_post")
    dact = _mm_dact(df, p["wdown"])
    g_wdown = _mm_dwdown(sv["act"], df)
    dupre, g_fcb, g_fcw = _ffn_bwd(sv["upre"], dact, p["fcw"], p["fcb"])
    dh2 = _mm_dh2(dupre, p["wup"])
    g_wup = _mm_dwup(sv["h2"], dupre)
    dx1, dmixed, g_nfp, g_nmpo = _rms_bwd2(sv["x1"], p["norm_ffn_pre"], dh2, dx2, sv["mixed"], p["norm_mix_post"])
    dcat = _mm_rows("dcat", dmixed, p["wout"], BF16, NT)
    g_wout = _mm_dwout([sv["o"], sv["yconv"], sv["yssd"]], dmixed)
    big = {
        "w_out": _unpad_wout(*g_wout).reshape(NDEV, 128, D),
        "ffn_w_up": g_wup,
        "ffn_conv_w": g_fcw.astype(BF16),
        "ffn_w_down": g_wdown.reshape(NDEV, 352, D),
    }
    outgoing = _pack_grads(big, FFN_SIDE) + (pending or []) if exchange else None
    dq, dk, dv, received = _flash_bwd(sv["q"], sv["k"], sv["kv"], sv["o"], dcat, sv["lse"], outgoing)
    d_mla, g_wq, g_wkv, g_qn, g_kvn = _mla_prep_bwd(sv["proj"], tabs, p["mla_q_norm"], p["mla_kv_norm"], p["wq"], p["wkv"], dq, dk, dv)
    d_sc, g_scw = _sconv_bwd(sv["proj"], dcat, p["scw"])
    d_ssd, g_cw, g_cb, g_dtb, g_alog, g_d, g_nw = _ssd_bwd(
        sv["proj"], dcat, sv["ypre"], sv["states"], p["ssdcw"], p["ssd_conv_b"], p["dtb"], p["alog"], p["dvec"], p["ssd_norm"])
    dparts = [d_mla, d_sc, d_ssd]
    dh = _mm_sum("dh", dparts, [p["win"][:, 0:512], p["win"][:, 512:1280], p["win"][:, 1280:PW]], NT)
    g_win = _mm_dwin(sv["h"], dparts)
    big.update({
        "w_in": _unpad_win(*g_win).reshape(NDEV, 128, 2212),
        "mla_w_q_up": g_wq[:, :, :96].astype(BF16),
        "mla_w_kv_up": g_wkv.astype(BF16),
        "sc_conv_w": g_scw.reshape(3, NDEV, 32).transpose(1, 0, 2).astype(BF16),
        "ssd_conv_w": g_cw.reshape(4, NDEV, 96).transpose(1, 0, 2).astype(BF16),
    })
    head_below, last_received = None, None
    if below is not None:
        dx, df_below, g_nmp, g_below = _rms_bwd2(sv["x"], p["norm_mix_pre"], dh, dx1, *below)
        head_below = (df_below, g_below)
    elif exchange:
        dx, g_nmp, last_received = _rms_bwd(sv["x"], p["norm_mix_pre"], dh, dx1, F32, "rms_bwd_pre", _pack_grads(big, ATT_SIDE))
    else:
        dx, g_nmp = _rms_bwd(sv["x"], p["norm_mix_pre"], dh, dx1, F32, "rms_bwd_pre")
    small = {
        "norm_mix_pre": g_nmp[0], "norm_mix_post": g_nmpo[0], "norm_ffn_pre": g_nfp[0], "norm_ffn_post": g_nfpo[0],
        "mla_q_norm": g_qn[0], "mla_kv_norm": g_kvn[0], "ssd_conv_b": g_cb[0], "ssd_dt_bias": g_dtb[0, :SSD_H],
        "ssd_a_log": g_alog[0, :SSD_H], "ssd_d": g_d[0, :SSD_H], "ssd_norm": g_nw[0], "ffn_conv_b": g_fcb.reshape(-1),
    }
    return dx, big, small, received, head_below, last_received


def _local_step(xv, positions, target, layers):
    tabs = _rope_tables(positions)
    saved = []
    for p in layers:
        xv, sv, _, _ = _layer_fwd(xv, p, tabs)
        saved.append(sv)
    loss, dx = _loss_head(xv, target)
    bigs, smalls = [None] * DEPTH, [None] * DEPTH
    head = None
    for l in reversed(range(len(layers))):
        below = (saved[l - 1]["f"], layers[l - 1]["norm_ffn_post"]) if l > 0 else None
        dx, bigs[l], smalls[l], _, head, _ = _layer_bwd(dx, saved[l], layers[l], tabs, head=head, below=below)
    return loss[0, 0], dx, bigs, smalls


def _rows2(n, layer=False):
    shape = SHAPES[n][1:] if layer else SHAPES[n]
    return (math.prod(shape[:-1]), shape[-1])


def _group_pack(group, get, lead):
    width, names = group
    pieces = []
    for n in names:
        rows, cols = _rows2(n, True)
        pad = [(0, 0)] * len(lead) + [(0, -rows % 16), (0, width - cols)]
        pieces.append(jnp.pad(get(n), pad))
    return pieces[0] if len(pieces) == 1 else jnp.concatenate(pieces, axis=len(lead))


def _group_unpack(group, buf):
    _, names = group
    res, off = {}, 0
    for n in names:
        rows, cols = _rows2(n, True)
        res[n] = buf[:, off:off + rows, :cols]
        off += rows + (-rows % 16)
    return res


def kernel(x, positions, norm_mix_pre, norm_mix_post, norm_ffn_pre, norm_ffn_post, w_in, mla_q_norm, mla_w_q_up, mla_kv_norm, mla_w_kv_up, sc_conv_w, ssd_conv_w, ssd_conv_b, ssd_dt_bias, ssd_a_log, ssd_d, ssd_norm, w_out, ffn_w_up, ffn_conv_w, ffn_conv_b, ffn_w_down, loss_target, m_norm_mix_pre, m_norm_mix_post, m_norm_ffn_pre, m_norm_ffn_post, m_w_in, m_mla_q_norm, m_mla_w_q_up, m_mla_kv_norm, m_mla_w_kv_up, m_sc_conv_w, m_ssd_conv_w, m_ssd_conv_b, m_ssd_dt_bias, m_ssd_a_log, m_ssd_d, m_ssd_norm, m_w_out, m_ffn_w_up, m_ffn_conv_w, m_ffn_conv_b, m_ffn_w_down, v_norm_mix_pre, v_norm_mix_post, v_norm_ffn_pre, v_norm_ffn_post, v_w_in, v_mla_q_norm, v_mla_w_q_up, v_mla_kv_norm, v_mla_w_kv_up, v_sc_conv_w, v_ssd_conv_w, v_ssd_conv_b, v_ssd_dt_bias, v_ssd_a_log, v_ssd_d, v_ssd_norm, v_w_out, v_ffn_w_up, v_ffn_conv_w, v_ffn_conv_b, v_ffn_w_down):
    given = dict(locals())
    w = {n: given[n] for n in WEIGHTS}
    m = {n: given["m_" + n] for n in WEIGHTS}
    v = {n: given["v_" + n] for n in WEIGHTS}

    def shards(l, group_ids):
        return [_group_pack(GROUPS[gi], lambda n: w[n][l].astype(BF16).reshape(_rows2(n, True)), ()) for gi in group_ids]

    def unpacked(bufs, group_ids):
        big = {}
        for gi, buf in zip(group_ids, bufs):
            for n, piece in _group_unpack(GROUPS[gi], buf).items():
                big[n] = piece.reshape((NDEV,) + SHAPES[n][1:])
        return big

    small_w = {n: w[n] for n, _ in SMALL}
    tabs = _rope_tables(positions[0])
    xv, h, layers, saved = x[0], None, [], []
    att = _all_gather(shards(0, ATT_SIDE), "gather_weights")
    for l in range(DEPTH):
        prefetch = shards(l, FFN_SIDE) + (shards(l + 1, ATT_SIDE) if l + 1 < DEPTH else [])
        nxt = w["norm_mix_pre"][l + 1].reshape(1, D) if l + 1 < DEPTH else None
        xv, sv, p, gathered = _layer_fwd(xv, _prep_layer(unpacked(att, ATT_SIDE), small_w, l), tabs, prefetch,
                                         lambda got: _prep_ffn(unpacked(got[:len(FFN_SIDE)], FFN_SIDE)), h, nxt)
        xv, h = xv if nxt is not None else (xv, None)
        att = gathered[len(FFN_SIDE):]
        layers.append(p)
        saved.append(sv)
    loss, dx = _loss_head(xv, loss_target[0])
    loss = lax.psum(loss[0, 0], ("x", "y", "c"))

    smalls, pending, head = [None] * DEPTH, None, None
    recvs = [[None] * len(GROUPS) for _ in range(DEPTH)]
    for l in reversed(range(DEPTH)):
        below = (saved[l - 1]["f"], layers[l - 1]["norm_ffn_post"]) if l > 0 else None
        dx, grads, smalls[l], received, head, last = _layer_bwd(dx, saved[l], layers[l], tabs, True, pending, head, below)
        for pos, gi in enumerate(FFN_SIDE):
            recvs[l][gi] = received[pos]
        if pending is not None:
            for pos, gi in enumerate(ATT_SIDE):
                recvs[l + 1][gi] = received[len(FFN_SIDE) + pos]
        pending = _pack_grads(grads, ATT_SIDE) if l > 0 else None
    for gi, buf in zip(ATT_SIDE, last):
        recvs[0][gi] = buf
    out = {}
    for gi, g in enumerate(GROUPS):
        per_layer = [_group_unpack(g, recvs[l][gi]) for l in range(DEPTH)]
        for n in g[1]:
            parts = jnp.stack([per_layer[l][n] for l in range(DEPTH)], axis=1)
            out[n] = _adamw(parts, w[n], m[n], v[n], "adamw_" + n)

    total = sum(width for _, width in SMALL)
    padded = -(-total // (8 * LANE)) * 8 * LANE
    pk = lambda a: jnp.pad(a, ((0, 0), (0, padded - total))).reshape(1, DEPTH * padded // LANE, LANE)
    sflat = jnp.stack([jnp.concatenate([smalls[l][n] for n, _ in SMALL]) for l in range(DEPTH)])
    sparts = _all_gather([pk(sflat)[0]], "gather_small_grads")[0]
    pw = lambda d: pk(jnp.concatenate([d[n] for n, _ in SMALL], axis=1))
    res = _adamw(sparts[:, None], pw(w), pw(m), pw(v), "adamw_small")
    off = 0
    for n, width in SMALL:
        out[n] = [a.reshape(DEPTH, padded)[:, off:off + width] for a in res]
        off += width

    return (loss, dx[None], *[out[n][0] for n in WEIGHTS], *[out[n][1] for n in WEIGHTS],
            *[out[n][2] for n in WEIGHTS], *[out[n][3] for n in WEIGHTS])
```

```python
import functools
import math

import jax
import jax.numpy as jnp
from jax import lax
from jax.experimental import pallas as pl
from jax.experimental.pallas import tpu as pltpu

F32 = jnp.float32
BF16 = jnp.bfloat16

D = 1024
DEPTH = 4
NDEV = 8
HEADS = 8
QL = 256
KVL = 128
ROPE = 32
NOPE = 64
SC = 256
SSD_DIM = 256
SSD_CONV = 768
SSD_H = 4
SSD_L = 128
FFN = 2816
FB = 704
EPS = 1e-6
ROPE_THETA = 10000.0
ATT_SCALE = 96 ** -0.5
LOG2E = 1.4426950408889634
LR, B1, B2, AEPS, WD, STEP = 0.001, 0.9, 0.999, 1e-08, 0.01, 10

PW = 2432
CATW = 1536

ROW_TILE = 512
ROW_CHUNK = 16
NORM_CHUNK = 32
NORM_TILE = 512
MM_TILE = 1024
ATT_TILE = 512
ATT_QW = 1
BWD_TILE = 512
BWD_QW = 1
FWD_HEADS = 4
BWD_HEADS = 2
HALO = 16
LANE = 128
NEG = -1e30
NN = (((1,), (0,)), ((), ()))
NT = (((1,), (1,)), ((), ()))
TN = (((0,), (0,)), ((), ()))
VMEM_LIMIT = 56 * 1024 * 1024

SHARDED = (
    ("w_in", (4, 128, 2212)),
    ("mla_w_q_up", (4, 256, 96)),
    ("mla_w_kv_up", (4, 128, 128)),
    ("sc_conv_w", (4, 3, 32)),
    ("ssd_conv_w", (4, 4, 96)),
    ("w_out", (4, 128, 1024)),
    ("ffn_w_up", (4, 1024, 704)),
    ("ffn_conv_w", (4, 3, 704)),
    ("ffn_w_down", (4, 352, 1024)),
)
SHAPES = dict(SHARDED)
GROUPS = (
    (2212, ("w_in",)),
    (1024, ("w_out",)),
    (704, ("ffn_w_up",)),
    (96, ("mla_w_q_up", "ssd_conv_w", "sc_conv_w")),
    (128, ("mla_w_kv_up",)),
    (1024, ("ffn_w_down",)),
    (704, ("ffn_conv_w",)),
)
ATT_SIDE = (0, 3, 4)
FFN_SIDE = (1, 2, 5, 6)
SMALL = (
    ("norm_mix_pre", 1024), ("norm_mix_post", 1024), ("norm_ffn_pre", 1024), ("norm_ffn_post", 1024),
    ("mla_q_norm", 256), ("mla_kv_norm", 128), ("ssd_conv_b", 768), ("ssd_dt_bias", 4), ("ssd_a_log", 4),
    ("ssd_d", 4), ("ssd_norm", 256), ("ffn_conv_b", 5632),
)
WEIGHTS = ("norm_mix_pre", "norm_mix_post", "norm_ffn_pre", "norm_ffn_post", "w_in", "mla_q_norm", "mla_w_q_up",
           "mla_kv_norm", "mla_w_kv_up", "sc_conv_w", "ssd_conv_w", "ssd_conv_b", "ssd_dt_bias", "ssd_a_log", "ssd_d",
           "ssd_norm", "w_out", "ffn_w_up", "ffn_conv_w", "ffn_conv_b", "ffn_w_down")


def _dot(a, b, dims=NN, precision=None):
    return lax.dot_general(a, b, dims, precision=precision, preferred_element_type=F32)


def _dot01(x, c, dims=NN, x_first=True):
    cb = c.astype(BF16)
    hi = x.astype(BF16)
    rest = x - hi.astype(F32)
    mid = rest.astype(BF16)
    lo = (rest - mid.astype(F32)).astype(BF16)
    one = (lambda t: _dot(t, cb, dims)) if x_first else (lambda t: _dot(cb, t, dims))
    return one(hi) + one(mid) + one(lo)


def _sig(v):
    return 0.5 * jnp.tanh(0.5 * v) + 0.5


def _cp(*sem):
    return pltpu.CompilerParams(dimension_semantics=sem, vmem_limit_bytes=VMEM_LIMIT)


def _rowsum(v):
    return jnp.sum(v, axis=0, keepdims=True)


def _prev_halo(i, ts):
    return jnp.maximum(i * (ts // HALO) - 1, 0)


def _next_halo(i, ts, n):
    return jnp.minimum((i + 1) * (ts // HALO), n * (ts // HALO) - 1)


def _gather_plan(x_refs, out_refs, send_sems, recv_sems, local_sems):
    n = len(x_refs)
    x, y, cc = lax.axis_index("x"), lax.axis_index("y"), lax.axis_index("c")
    me, sibling = (x, y, cc), (x, y, 1 - cc)
    chips = [(1 - x, y), (x, 1 - y), (1 - x, 1 - y)]

    def rows(t, px, py, pc):
        return out_refs[t].at[4 * px + 2 * py + pc]

    def copy(t, k, block, to, own=False):
        return pltpu.make_async_remote_copy(
            src_ref=x_refs[t] if own else rows(t, *block), dst_ref=rows(t, *block),
            send_sem=send_sems.at[7 * t + k], recv_sem=recv_sems.at[7 * t + k], device_id=to, device_id_type=pl.DeviceIdType.MESH)

    def local(t):
        return pltpu.make_async_copy(x_refs[t], rows(t, *me), local_sems.at[t])

    def start():
        for t in range(n):
            local(t).start()
            copy(t, 0, me, sibling, own=True).start()
            for j, chip in enumerate(chips):
                copy(t, 1 + j, me, (*chip, cc), own=True).start()

    def finish():
        for j, chip in enumerate(chips):
            for t in range(n):
                copy(t, 1 + j, (*chip, cc), me).wait_recv()
                copy(t, 4 + j, (*chip, cc), sibling).start()
        for t in range(n):
            copy(t, 0, sibling, me).wait_recv()
            for j, chip in enumerate(chips):
                copy(t, 4 + j, (*chip, 1 - cc), me).wait_recv()
        for t in range(n):
            copy(t, 0, me, sibling, own=True).wait_send()
            for j, chip in enumerate(chips):
                copy(t, 1 + j, me, (*chip, cc), own=True).wait_send()
                copy(t, 4 + j, (*chip, cc), sibling).wait_send()
            local(t).wait()

    return start, finish


def _exchange_plan(x_refs, out_refs, send_sems, recv_sems, local_sems):
    n = len(x_refs)
    x, y, cc = lax.axis_index("x"), lax.axis_index("y"), lax.axis_index("c")
    me = 4 * x + 2 * y + cc

    def copies():
        res = [pltpu.make_async_copy(x_refs[t].at[me], out_refs[t].at[me], local_sems.at[t]) for t in range(n)]
        for k in range(1, NDEV):
            px = 1 - x if k & 4 else x
            py = 1 - y if k & 2 else y
            pc = 1 - cc if k & 1 else cc
            peer = 4 * px + 2 * py + pc
            for t in range(n):
                res.append(pltpu.make_async_remote_copy(
                    src_ref=x_refs[t].at[peer], dst_ref=out_refs[t].at[me], send_sem=send_sems.at[7 * t + k - 1],
                    recv_sem=recv_sems.at[7 * t + k - 1], device_id=(px, py, pc), device_id_type=pl.DeviceIdType.MESH))
        return res

    def start():
        for cp in copies():
            cp.start()

    def finish():
        for cp in copies():
            cp.wait()

    return start, finish


def _comm_scratch(n):
    return [pltpu.SemaphoreType.DMA((7 * n,)), pltpu.SemaphoreType.DMA((7 * n,)), pltpu.SemaphoreType.DMA((n,))]


ANY = pl.BlockSpec(memory_space=pl.ANY)


def _all_gather(xs, name):
    n = len(xs)

    def body(*refs):
        start, finish = _gather_plan(refs[:n], refs[n:2 * n], *refs[2 * n:])
        start()
        finish()

    return pl.pallas_call(
        body, name=name, out_shape=[jax.ShapeDtypeStruct((NDEV,) + a.shape, a.dtype) for a in xs],
        in_specs=[ANY] * n, out_specs=[ANY] * n, scratch_shapes=_comm_scratch(n),
    )(*xs)


def _mm(name, a, b, out_shape, grid, a_spec, b_spec, o_spec, dims, acc_shape):
    nk = grid[2]

    def single(a_ref, b_ref, o_ref):
        o_ref[...] = _dot(a_ref[...], b_ref[...], dims).astype(o_ref.dtype)

    if nk == 1:
        return pl.pallas_call(
            single, name=name, grid=grid, out_shape=out_shape, in_specs=[a_spec, b_spec], out_specs=o_spec,
            compiler_params=_cp("parallel", "parallel", "arbitrary"),
        )(a, b)

    def body(a_ref, b_ref, o_ref, acc_ref):
        k = pl.program_id(2)

        @pl.when(k == 0)
        def _():
            acc_ref[...] = jnp.zeros_like(acc_ref)

        acc_ref[...] += _dot(a_ref[...], b_ref[...], dims)

        @pl.when(k == nk - 1)
        def _():
            o_ref[...] = acc_ref[...].astype(o_ref.dtype)

    return pl.pallas_call(
        body, name=name, grid=grid, out_shape=out_shape, in_specs=[a_spec, b_spec], out_specs=o_spec,
        scratch_shapes=[pltpu.VMEM(acc_shape, F32)], compiler_params=_cp("parallel", "parallel", "arbitrary"),
    )(a, b)


def _mm_rows(name, a, w, out_dtype, dims):
    s, k = a.shape
    n = w.shape[1] if dims == NN else w.shape[0]
    tm = min(MM_TILE, s)
    return _mm(name, a, w, jax.ShapeDtypeStruct((s, n), out_dtype), (s // tm, 1, 1),
               pl.BlockSpec((tm, k), lambda i, j, kk: (i, 0)), pl.BlockSpec(w.shape, lambda i, j, kk: (0, 0)),
               pl.BlockSpec((tm, n), lambda i, j, kk: (i, 0)), dims, (tm, n))


def _mm_wgrad(name, a, g, out_dtype):
    s, m = a.shape
    n = g.shape[1]
    tk = min(MM_TILE, s)
    return _mm(name, a, g, jax.ShapeDtypeStruct((m, n), out_dtype), (1, 1, s // tk),
               pl.BlockSpec((tk, m), lambda i, j, kk: (kk, 0)), pl.BlockSpec((tk, n), lambda i, j, kk: (kk, 0)),
               pl.BlockSpec((m, n), lambda i, j, kk: (0, 0)), TN, (m, n))


def _mm_sum(name, parts, wparts, dims):
    s = parts[0].shape[0]
    tm = min(MM_TILE, s)
    n = len(parts)

    def body(*refs):
        acc = _dot(refs[0][...], refs[n][...], dims)
        for i in range(1, n):
            acc += _dot(refs[i][...], refs[n + i][...], dims)
        refs[2 * n][...] = acc.astype(BF16)

    return pl.pallas_call(
        body, name=name, grid=(s // tm,), out_shape=jax.ShapeDtypeStruct((s, D), BF16),
        in_specs=[pl.BlockSpec((tm, a.shape[1]), lambda i: (i, 0)) for a in parts] + [pl.BlockSpec(w.shape, lambda i: (0, 0)) for w in wparts],
        out_specs=pl.BlockSpec((tm, D), lambda i: (i, 0)), compiler_params=_cp("parallel"),
    )(*parts, *wparts)


def _mm_dwin(h, parts):
    s = h.shape[0]
    tk = min(MM_TILE, s)
    nk = s // tk
    n = len(parts)

    def body(*refs):
        h_ref, g_refs, o_refs, accs = refs[0], refs[1:1 + n], refs[1 + n:1 + 2 * n], refs[1 + 2 * n:]
        k = pl.program_id(0)

        @pl.when(k == 0)
        def _():
            for acc in accs:
                acc[...] = jnp.zeros_like(acc)

        hv = h_ref[...]
        for i in range(n):
            accs[i][...] += _dot(hv, g_refs[i][...], TN)

        @pl.when(k == nk - 1)
        def _():
            for i in range(n):
                o_refs[i][...] = accs[i][...].astype(BF16)

    widths = [a.shape[1] for a in parts]
    return pl.pallas_call(
        body, name="dw_in", grid=(nk,), out_shape=[jax.ShapeDtypeStruct((D, w), BF16) for w in widths],
        in_specs=[pl.BlockSpec((tk, D), lambda k: (k, 0))] + [pl.BlockSpec((tk, w), lambda k: (k, 0)) for w in widths],
        out_specs=[pl.BlockSpec((D, w), lambda k: (0, 0)) for w in widths],
        scratch_shapes=[pltpu.VMEM((D, w), F32) for w in widths], compiler_params=_cp("arbitrary"),
    )(h, *parts)


def _mm_dwout(parts, g):
    s = g.shape[0]
    tk = min(MM_TILE, s)
    nk = s // tk
    n = len(parts)

    def body(*refs):
        a_refs, g_ref, o_refs, accs = refs[:n], refs[n], refs[n + 1:2 * n + 1], refs[2 * n + 1:]
        k = pl.program_id(0)

        @pl.when(k == 0)
        def _():
            for acc in accs:
                acc[...] = jnp.zeros_like(acc)

        gv = g_ref[...]
        for i in range(n):
            accs[i][...] += _dot(a_refs[i][...], gv, TN)

        @pl.when(k == nk - 1)
        def _():
            for i in range(n):
                o_refs[i][...] = accs[i][...].astype(BF16)

    widths = [a.shape[1] for a in parts]
    return pl.pallas_call(
        body, name="dw_out", grid=(nk,), out_shape=[jax.ShapeDtypeStruct((w, D), BF16) for w in widths],
        in_specs=[pl.BlockSpec((tk, w), lambda k: (k, 0)) for w in widths] + [pl.BlockSpec((tk, D), lambda k: (k, 0))],
        out_specs=[pl.BlockSpec((w, D), lambda k: (0, 0)) for w in widths],
        scratch_shapes=[pltpu.VMEM((w, D), F32) for w in widths], compiler_params=_cp("arbitrary"),
    )(*parts, g)


def _mm_up(h2, wup):
    s = h2.shape[0]
    tm = min(MM_TILE, s)
    return _mm("ffn_up", h2, wup, jax.ShapeDtypeStruct((NDEV, s, FB), BF16), (NDEV, s // tm, 1),
               pl.BlockSpec((tm, D), lambda j, i, kk: (i, 0)), pl.BlockSpec((None, D, FB), lambda j, i, kk: (j, 0, 0)),
               pl.BlockSpec((None, tm, FB), lambda j, i, kk: (j, i, 0)), NN, (tm, FB))


def _mm_down(act, wdown):
    s = act.shape[1]
    tm = min(MM_TILE, s)
    return _mm("ffn_down", act, wdown, jax.ShapeDtypeStruct((s, D), BF16), (s // tm, 1, 4),
               pl.BlockSpec((None, tm, FB), lambda i, j, kk: (kk, i, 0)), pl.BlockSpec((None, FB, D), lambda i, j, kk: (kk, 0, 0)),
               pl.BlockSpec((tm, D), lambda i, j, kk: (i, 0)), NN, (tm, D))


def _mm_dact(df, wdown):
    s = df.shape[0]
    tm = min(MM_TILE, s)
    return _mm("ffn_dact", df, wdown, jax.ShapeDtypeStruct((4, s, FB), BF16), (4, s // tm, 1),
               pl.BlockSpec((tm, D), lambda j, i, kk: (i, 0)), pl.BlockSpec((None, FB, D), lambda j, i, kk: (j, 0, 0)),
               pl.BlockSpec((None, tm, FB), lambda j, i, kk: (j, i, 0)), NT, (tm, FB))


def _mm_dwdown(act, df):
    s = df.shape[0]
    tk = min(MM_TILE, s)
    return _mm("ffn_dwdown", act, df, jax.ShapeDtypeStruct((4, FB, D), BF16), (4, 1, s // tk),
               pl.BlockSpec((None, tk, FB), lambda j, i, kk: (j, kk, 0)), pl.BlockSpec((tk, D), lambda j, i, kk: (kk, 0)),
               pl.BlockSpec((None, FB, D), lambda j, i, kk: (j, 0, 0)), TN, (FB, D))


def _mm_dh2(dupre, wup):
    s = dupre.shape[1]
    tm = min(MM_TILE, s)
    return _mm("ffn_dh2", dupre, wup, jax.ShapeDtypeStruct((s, D), BF16), (s // tm, 1, NDEV),
               pl.BlockSpec((None, tm, FB), lambda i, j, kk: (kk, i, 0)), pl.BlockSpec((None, D, FB), lambda i, j, kk: (kk, 0, 0)),
               pl.BlockSpec((tm, D), lambda i, j, kk: (i, 0)), NT, (tm, D))


def _mm_dwup(h2, dupre):
    s = h2.shape[0]
    tk = min(MM_TILE, s)
    return _mm("ffn_dwup", h2, dupre, jax.ShapeDtypeStruct((NDEV, D, FB), BF16), (NDEV, 1, s // tk),
               pl.BlockSpec((tk, D), lambda j, i, kk: (kk, 0)), pl.BlockSpec((None, tk, FB), lambda j, i, kk: (j, kk, 0)),
               pl.BlockSpec((None, D, FB), lambda j, i, kk: (j, 0, 0)), TN, (D, FB))


def _rms(xv, w, out_dtype, name):
    s, d = xv.shape
    ts = min(ROW_TILE, s)

    def body(x_ref, w_ref, o_ref):
        for r0 in range(0, ts, NORM_CHUNK):
            rows = pl.ds(r0, NORM_CHUNK)
            xf = x_ref[rows, :].astype(F32)
            r = lax.rsqrt(jnp.mean(xf * xf, axis=-1, keepdims=True) + EPS)
            o_ref[rows, :] = (xf * r * w_ref[...]).astype(o_ref.dtype)

    return pl.pallas_call(
        body, name=name, grid=(s // ts,), out_shape=jax.ShapeDtypeStruct((s, d), out_dtype),
        in_specs=[pl.BlockSpec((ts, d), lambda i: (i, 0)), pl.BlockSpec((1, d), lambda i: (0, 0))],
        out_specs=pl.BlockSpec((ts, d), lambda i: (i, 0)), compiler_params=_cp("parallel"),
    )(xv, w)


def _add_rms(xv, mv, w, name, w_next=None):
    s, d = xv.shape
    ts = min(NORM_TILE, s)
    both = w_next is not None

    def body(*refs):
        x_ref, m_ref, w_ref = refs[:3]
        o_ref = refs[4] if both else refs[3]
        for r0 in range(0, ts, NORM_CHUNK):
            rows = pl.ds(r0, NORM_CHUNK)
            mf = m_ref[rows, :].astype(F32)
            r = lax.rsqrt(jnp.mean(mf * mf, axis=-1, keepdims=True) + EPS)
            y = x_ref[rows, :] + mf * r * w_ref[...]
            o_ref[rows, :] = y
            if both:
                r2 = lax.rsqrt(jnp.mean(y * y, axis=-1, keepdims=True) + EPS)
                refs[5][rows, :] = (y * r2 * refs[3][...]).astype(BF16)

    row = pl.BlockSpec((ts, d), lambda i: (i, 0))
    vec = pl.BlockSpec((1, d), lambda i: (0, 0))
    if both:
        return pl.pallas_call(
            body, name=name + "_rms", grid=(s // ts,),
            out_shape=(jax.ShapeDtypeStruct((s, d), F32), jax.ShapeDtypeStruct((s, d), BF16)),
            in_specs=[row, row, vec, vec], out_specs=(row, row), compiler_params=_cp("parallel"),
        )(xv, mv, w, w_next)
    return pl.pallas_call(
        body, name=name, grid=(s // ts,), out_shape=jax.ShapeDtypeStruct((s, d), F32),
        in_specs=[row, row, vec], out_specs=row, compiler_params=_cp("parallel"),
    )(xv, mv, w)


def _rms_bwd_math(xf, w, dy):
    r = lax.rsqrt(jnp.mean(xf * xf, axis=-1, keepdims=True) + EPS)
    xh = xf * r
    dxh = dy * w
    dx = r * (dxh - xh * jnp.mean(dxh * xh, axis=-1, keepdims=True))
    return dx, dy * xh


def _rms_bwd(xv, w, dy, dres, out_dtype, name, pending=None):
    s, d = xv.shape
    ts = min(NORM_TILE, s)
    with_res = dres is not None
    nin = 4 if with_res else 3
    nx = len(pending) if pending else 0

    def body(*refs):
        x_ref, w_ref, dy_ref = refs[:3]
        dres_ref = refs[3] if with_res else None
        dx_ref, dw_ref = refs[nin + nx:nin + nx + 2]
        acc = refs[nin + 2 * nx + 2]
        if nx:
            start, finish = _exchange_plan(refs[nin:nin + nx], refs[nin + nx + 2:nin + 2 * nx + 2], *refs[nin + 2 * nx + 3:])
            pl.when(pl.program_id(0) == 0)(start)
        acc[...] = jnp.zeros_like(acc)
        for r0 in range(0, ts, NORM_CHUNK):
            rows = pl.ds(r0, NORM_CHUNK)
            dx, dwt = _rms_bwd_math(x_ref[rows, :].astype(F32), w_ref[...], dy_ref[rows, :].astype(F32))
            if with_res:
                dx = dx + dres_ref[rows, :]
            dx_ref[rows, :] = dx.astype(dx_ref.dtype)
            acc[...] += dwt

        @pl.when(pl.program_id(0) == 0)
        def _():
            dw_ref[...] = jnp.zeros_like(dw_ref)

        dw_ref[...] += _rowsum(acc[...])
        if nx:
            pl.when(pl.program_id(0) == s // ts - 1)(finish)

    row = pl.BlockSpec((ts, d), lambda i: (i, 0))
    vec = pl.BlockSpec((1, d), lambda i: (0, 0))
    ins = [xv, w, dy] + ([dres] if with_res else []) + (pending or [])
    res = pl.pallas_call(
        body, name=name + "_exchange" if nx else name, grid=(s // ts,),
        out_shape=[jax.ShapeDtypeStruct((s, d), out_dtype), jax.ShapeDtypeStruct((1, d), F32)]
        + [jax.ShapeDtypeStruct(a.shape, a.dtype) for a in (pending or [])],
        in_specs=[row, vec, row] + ([row] if with_res else []) + [ANY] * nx, out_specs=[row, vec] + [ANY] * nx,
        scratch_shapes=[pltpu.VMEM((NORM_CHUNK, d), F32)] + (_comm_scratch(nx) if nx else []), compiler_params=_cp("arbitrary"),
    )(*ins)
    return (res[0], res[1], list(res[2:])) if nx else (res[0], res[1])


def _rms_bwd2(xa, wa, dya, dres, xb, wb):
    s, d = xa.shape
    ts = min(NORM_TILE, s)

    def body(xa_ref, wa_ref, dya_ref, dres_ref, xb_ref, wb_ref, da_ref, db_ref, dwa_ref, dwb_ref, acc):
        acc[...] = jnp.zeros_like(acc)
        for r0 in range(0, ts, NORM_CHUNK):
            rows = pl.ds(r0, NORM_CHUNK)
            da, dwt = _rms_bwd_math(xa_ref[rows, :].astype(F32), wa_ref[...], dya_ref[rows, :].astype(F32))
            da = da + dres_ref[rows, :]
            da_ref[rows, :] = da
            acc[0] += dwt
            db, dwt = _rms_bwd_math(xb_ref[rows, :].astype(F32), wb_ref[...], da)
            db_ref[rows, :] = db.astype(BF16)
            acc[1] += dwt

        @pl.when(pl.program_id(0) == 0)
        def _():
            dwa_ref[...] = jnp.zeros_like(dwa_ref)
            dwb_ref[...] = jnp.zeros_like(dwb_ref)

        dwa_ref[...] += _rowsum(acc[0])
        dwb_ref[...] += _rowsum(acc[1])

    row = pl.BlockSpec((ts, d), lambda i: (i, 0))
    vec = pl.BlockSpec((1, d), lambda i: (0, 0))
    return pl.pallas_call(
        body, name="rms_bwd2", grid=(s // ts,),
        out_shape=(jax.ShapeDtypeStruct((s, d), F32), jax.ShapeDtypeStruct((s, d), BF16), jax.ShapeDtypeStruct((1, d), F32),
                   jax.ShapeDtypeStruct((1, d), F32)),
        in_specs=[row, vec, row, row, row, vec], out_specs=(row, row, vec, vec),
        scratch_shapes=[pltpu.VMEM((2, NORM_CHUNK, d), F32)], compiler_params=_cp("arbitrary"),
    )(xa, wa, dya, dres, xb, wb)


def _loss_head(yv, tv):
    s, d = yv.shape
    ts = min(ROW_TILE, s)

    def body(y_ref, t_ref, l_ref, dy_ref):
        e = y_ref[...] - t_ref[...]
        dy_ref[...] = e * (1.0 / d)

        @pl.when(pl.program_id(0) == 0)
        def _():
            l_ref[...] = jnp.zeros_like(l_ref)

        tot = jnp.sum(jnp.sum(e * e, axis=1, keepdims=True), axis=0, keepdims=True)
        l_ref[...] += jnp.broadcast_to(tot * (0.5 / d), (8, LANE))

    row = pl.BlockSpec((ts, d), lambda i: (i, 0))
    return pl.pallas_call(
        body, name="loss_head", grid=(s // ts,),
        out_shape=(jax.ShapeDtypeStruct((8, LANE), F32), jax.ShapeDtypeStruct((s, d), F32)),
        in_specs=[row, row], out_specs=(pl.BlockSpec((8, LANE), lambda i: (0, 0)), row), compiler_params=_cp("arbitrary"),
    )(yv, tv)


def _rope(v, c, a, b):
    return v * c + pltpu.roll(v, LANE - 16, 1) * a + pltpu.roll(v, 16, 1) * b


def _rope_t(dv, c, a, b):
    return dv * c + pltpu.roll(dv * a, 16, 1) + pltpu.roll(dv * b, LANE - 16, 1)


def _mla_prep(proj, tabs, qnw, kvnw, wq, wkv):
    s = proj.shape[0]
    ts = min(ROW_TILE, s)
    tc, ta, tb = tabs

    def body(cq_ref, ckv_ref, kr_ref, c_ref, a_ref, b_ref, qnw_ref, kvnw_ref, wq_ref, wkv_ref, q_ref, k_ref, kv_ref):
        c, a, b = c_ref[...], a_ref[...], b_ref[...]
        cq = cq_ref[...].astype(F32)
        qn = (cq * lax.rsqrt(jnp.mean(cq * cq, axis=-1, keepdims=True) + EPS) * qnw_ref[...]).astype(BF16)
        ckv = ckv_ref[...].astype(F32)
        kvn = (ckv * lax.rsqrt(jnp.mean(ckv * ckv, axis=-1, keepdims=True) + EPS) * kvnw_ref[...]).astype(BF16)
        kr = _rope(kr_ref[...].astype(F32), c, a, b)
        lane = lax.broadcasted_iota(jnp.int32, (ts, LANE), 1)
        for h in range(HEADS):
            q_ref[h] = _rope(_dot(qn, wq_ref[h]), c, a, b).astype(BF16)
            kv = _dot(kvn, wkv_ref[h])
            kv_ref[h] = kv.astype(BF16)
            k_ref[h] = jnp.where(lane < NOPE, kv, kr).astype(BF16)

    tab = pl.BlockSpec((ts, LANE), lambda i: (i, 0))
    hd = pl.BlockSpec((HEADS, ts, LANE), lambda i: (0, i, 0))
    out = jax.ShapeDtypeStruct((HEADS, s, LANE), BF16)
    return pl.pallas_call(
        body, name="mla_prep", grid=(s // ts,), out_shape=(out, out, out),
        in_specs=[pl.BlockSpec((ts, QL), lambda i: (i, 0)), pl.BlockSpec((ts, LANE), lambda i: (i, 2)),
                  pl.BlockSpec((ts, LANE), lambda i: (i, 3)), tab, tab, tab,
                  pl.BlockSpec((1, QL), lambda i: (0, 0)), pl.BlockSpec((1, KVL), lambda i: (0, 0)),
                  pl.BlockSpec((HEADS, QL, LANE), lambda i: (0, 0, 0)), pl.BlockSpec((HEADS, KVL, LANE), lambda i: (0, 0, 0))],
        out_specs=(hd, hd, hd), compiler_params=_cp("parallel"),
    )(proj, proj, proj, tc, ta, tb, qnw, kvnw, wq, wkv)


def _mla_prep_bwd(proj, tabs, qnw, kvnw, wq, wkv, dq, dk, dv):
    s = proj.shape[0]
    ts = min(ROW_TILE, s)
    tc, ta, tb = tabs

    def body(cq_ref, ckv_ref, c_ref, a_ref, b_ref, qnw_ref, kvnw_ref, wq_ref, wkv_ref, dq_ref, dk_ref, dv_ref,
             d_ref, dwq_ref, dwkv_ref, dqnw_ref, dkvnw_ref):
        dcq_ref, dckv_ref, dkr_ref = d_ref.at[:, 0:QL], d_ref.at[:, QL:QL + KVL], d_ref.at[:, QL + KVL:QL + KVL + LANE]

        @pl.when(pl.program_id(0) == 0)
        def _():
            dwq_ref[...] = jnp.zeros_like(dwq_ref)
            dwkv_ref[...] = jnp.zeros_like(dwkv_ref)
            dqnw_ref[...] = jnp.zeros_like(dqnw_ref)
            dkvnw_ref[...] = jnp.zeros_like(dkvnw_ref)

        c, a, b = c_ref[...], a_ref[...], b_ref[...]
        cq = cq_ref[...].astype(F32)
        qn = (cq * lax.rsqrt(jnp.mean(cq * cq, axis=-1, keepdims=True) + EPS) * qnw_ref[...]).astype(BF16)
        ckv = ckv_ref[...].astype(F32)
        kvn = (ckv * lax.rsqrt(jnp.mean(ckv * ckv, axis=-1, keepdims=True) + EPS) * kvnw_ref[...]).astype(BF16)
        lane = lax.broadcasted_iota(jnp.int32, (ts, LANE), 1)
        dqn = jnp.zeros((ts, QL), F32)
        dkvn = jnp.zeros((ts, KVL), F32)
        dkr = jnp.zeros((ts, LANE), F32)
        for h in range(HEADS):
            dqh = _rope_t(dq_ref[h], c, a, b).astype(BF16)
            dwq_ref[h] += _dot(qn, dqh, TN)
            dqn += _dot(dqh, wq_ref[h], NT)
            dkh = dk_ref[h].astype(F32)
            dkvh = jnp.where(lane < NOPE, dkh, dv_ref[h].astype(F32)).astype(BF16)
            dkr += jnp.where(lane < NOPE, 0.0, dkh)
            dwkv_ref[h] += _dot(kvn, dkvh, TN)
            dkvn += _dot(dkvh, wkv_ref[h], NT)
        dkr_ref[...] = _rope_t(dkr, c, a, b).astype(BF16)
        dcq, dwt = _rms_bwd_math(cq, qnw_ref[...], dqn)
        dcq_ref[...] = dcq.astype(BF16)
        dqnw_ref[...] += _rowsum(dwt)
        dckv, dwt = _rms_bwd_math(ckv, kvnw_ref[...], dkvn)
        dckv_ref[...] = dckv.astype(BF16)
        dkvnw_ref[...] += _rowsum(dwt)

    tab = pl.BlockSpec((ts, LANE), lambda i: (i, 0))
    hd = pl.BlockSpec((HEADS, ts, LANE), lambda i: (0, i, 0))
    wq_spec = pl.BlockSpec((HEADS, QL, LANE), lambda i: (0, 0, 0))
    wkv_spec = pl.BlockSpec((HEADS, KVL, LANE), lambda i: (0, 0, 0))
    return pl.pallas_call(
        body, name="mla_prep_bwd", grid=(s // ts,),
        out_shape=(jax.ShapeDtypeStruct((s, QL + KVL + LANE), BF16),
                   jax.ShapeDtypeStruct((HEADS, QL, LANE), F32), jax.ShapeDtypeStruct((HEADS, KVL, LANE), F32),
                   jax.ShapeDtypeStruct((1, QL), F32), jax.ShapeDtypeStruct((1, KVL), F32)),
        in_specs=[pl.BlockSpec((ts, QL), lambda i: (i, 0)), pl.BlockSpec((ts, LANE), lambda i: (i, 2)), tab, tab, tab,
                  pl.BlockSpec((1, QL), lambda i: (0, 0)), pl.BlockSpec((1, KVL), lambda i: (0, 0)), wq_spec, wkv_spec, hd, hd, hd],
        out_specs=(pl.BlockSpec((ts, QL + KVL + LANE), lambda i: (i, 0)), wq_spec, wkv_spec,
                   pl.BlockSpec((1, QL), lambda i: (0, 0)), pl.BlockSpec((1, KVL), lambda i: (0, 0))),
        compiler_params=_cp("arbitrary"),
    )(proj, proj, tc, ta, tb, qnw, kvnw, wq, wkv, dq, dk, dv)


def _transpose_bf16(v):
    return v.astype(F32).T.astype(BF16)


def _flash_fwd(q, k, kv, prefetch=None):
    s = q.shape[1]
    t = min(ATT_TILE, s)
    tq = ATT_QW * t
    n = s // tq
    g = FWD_HEADS
    nx = len(prefetch) if prefetch else 0

    def body(*refs):
        q_ref, k_ref, kv_ref = refs[:3]
        o_ref, lse_ref = refs[3 + nx:5 + nx]
        kvt_sc = refs[5 + 2 * nx]
        step = pl.program_id(0) * n + pl.program_id(1)
        if nx:
            start, finish = _gather_plan(refs[3:3 + nx], refs[5 + nx:5 + 2 * nx], *refs[6 + 2 * nx:])
            pl.when(step == 0)(start)
        attend(q_ref, k_ref, kv_ref, o_ref, lse_ref, kvt_sc)
        if nx:
            pl.when(step == (HEADS // g) * n - 1)(finish)

    def attend(q_ref, k_ref, kv_ref, o_ref, lse_ref, kvt_sc):
        i = pl.program_id(1)

        @pl.when(i == 0)
        def _():
            ones_rows = lax.broadcasted_iota(jnp.int32, (LANE, s), 0) < NOPE
            for hh in range(g):
                kvt_sc[hh] = jnp.where(ones_rows, 1.0, kv_ref[hh].astype(F32).T).astype(BF16)

        qt = [(q_ref[hh].astype(F32) * (ATT_SCALE * LOG2E)).T.astype(BF16) for hh in range(g)]
        kpos = lax.broadcasted_iota(jnp.int32, (t, tq), 0)
        qpos = lax.broadcasted_iota(jnp.int32, (t, tq), 1) + i * tq

        def chunk(j, carry, diagonal):
            start = pl.multiple_of(j * t, t)
            scs = [_dot(k_ref[hh, pl.ds(start, t), :], qt[hh]) for hh in range(g)]
            soft = []
            for hh in range(g):
                sc = scs[hh]
                if diagonal:
                    sc = jnp.where(qpos >= kpos + start, sc, NEG)
                m_new = jnp.maximum(carry[hh][0], jnp.max(sc, axis=0, keepdims=True))
                soft.append((m_new, jnp.exp2(carry[hh][0] - m_new), jnp.exp2(sc - m_new).astype(BF16)))
            pvs = [_dot(kvt_sc[hh, :, pl.ds(start, t)], soft[hh][2]) for hh in range(g)]
            return tuple((soft[hh][0], soft[hh][1] * carry[hh][1] + pvs[hh]) for hh in range(g))

        init = tuple((jnp.full((1, tq), NEG, F32), jnp.zeros((LANE, tq), F32)) for _ in range(g))
        carry = lax.fori_loop(0, ATT_QW * i, lambda j, c: chunk(j, c, False), init)
        for d in range(ATT_QW):
            carry = chunk(ATT_QW * i + d, carry, True)
        for hh in range(g):
            m, acc = carry[hh]
            l = acc[0:1, :]
            o_ref[:, hh * LANE:(hh + 1) * LANE] = (acc / l).T.astype(BF16)
            lse_ref[hh] = m + jnp.log2(l)

    whole = pl.BlockSpec((g, s, LANE), lambda h, i: (h, 0, 0))
    res = pl.pallas_call(
        body, name="flash_fwd_gather" if nx else "flash_fwd", grid=(HEADS // g, n),
        out_shape=[jax.ShapeDtypeStruct((s, HEADS * LANE), BF16), jax.ShapeDtypeStruct((HEADS, 1, s), F32)]
        + [jax.ShapeDtypeStruct((NDEV,) + a.shape, a.dtype) for a in (prefetch or [])],
        in_specs=[pl.BlockSpec((g, tq, LANE), lambda h, i: (h, i, 0)), whole, whole] + [ANY] * nx,
        out_specs=[pl.BlockSpec((tq, g * LANE), lambda h, i: (i, h)), pl.BlockSpec((g, 1, tq), lambda h, i: (h, 0, i))] + [ANY] * nx,
        scratch_shapes=[pltpu.VMEM((g, LANE, s), BF16)] + (_comm_scratch(nx) if nx else []),
        compiler_params=_cp("arbitrary", "arbitrary"),
    )(q, k, kv, *(prefetch or []))
    return res[0], res[1], list(res[2:])


def _flash_bwd(q, k, kv, cat, dcat, lse, pending=None):
    s = q.shape[1]
    t = min(BWD_TILE, s)
    tq = BWD_QW * t
    n = s // t
    g = BWD_HEADS
    nx = len(pending) if pending else 0

    def body(*refs):
        ins, outs, scr = refs[:6], refs[6 + nx:9 + nx], refs[9 + 2 * nx:14 + 2 * nx]
        step = pl.program_id(0) * n + pl.program_id(1)
        if nx:
            start, finish = _exchange_plan(refs[6:6 + nx], refs[9 + nx:9 + 2 * nx], *refs[14 + 2 * nx:])
            pl.when(step == 0)(start)
        attend(*ins, *outs, *scr)
        if nx:
            pl.when(step == (HEADS // g) * n - 1)(finish)

    def attend(q_ref, k_ref, kv_ref, o_ref, do_ref, lse_ref, dq_ref, dk_ref, dv_ref, qt_sc, dot_sc, delta_sc, dqt_sc, qs_sc):
        j = pl.program_id(1)

        @pl.when(j == 0)
        def _():
            for hh in range(g):
                lanes = slice(hh * LANE, (hh + 1) * LANE)
                qf = q_ref[hh].astype(F32)
                qt_sc[hh] = (qf * (ATT_SCALE * LOG2E)).T.astype(BF16)
                qs_sc[hh] = (qf * ATT_SCALE).astype(BF16)
                dof = do_ref[:, lanes].astype(F32)
                dot_sc[hh] = dof.T.astype(BF16)
                delta_sc[hh] = _dot01(dof * o_ref[:, lanes].astype(F32), jnp.ones((8, LANE), F32), NT, x_first=False)
            dqt_sc[...] = jnp.zeros_like(dqt_sc)

        kjt = [_transpose_bf16(k_ref[hh]) for hh in range(g)]
        kpos = lax.broadcasted_iota(jnp.int32, (t, tq), 0) + j * t
        qpos = lax.broadcasted_iota(jnp.int32, (t, tq), 1)

        def chunk(i, carry, diagonal):
            start = pl.multiple_of(i * tq, tq)
            cols = pl.ds(start, tq)
            scs = [_dot(k_ref[hh], qt_sc[hh, :, cols]) for hh in range(g)]
            dps = [_dot(kv_ref[hh], dot_sc[hh, :, cols]) for hh in range(g)]
            pds = []
            for hh in range(g):
                p = jnp.exp2(scs[hh] - lse_ref[hh, :, cols])
                if diagonal:
                    p = jnp.where(qpos + start >= kpos, p, 0.0)
                ds = (p * (dps[hh] - delta_sc[hh, 0:1, cols])).astype(BF16)
                pds.append((p.astype(BF16), ds))
            out = []
            for hh in range(g):
                dk, dv = carry[hh]
                dv = dv + _dot(pds[hh][0], do_ref[pl.ds(start, tq), hh * LANE:(hh + 1) * LANE])
                dk = dk + _dot(pds[hh][1], qs_sc[hh, pl.ds(start, tq), :])
                dqt_sc[hh, :, cols] += _dot(kjt[hh], pds[hh][1])
                out.append((dk, dv))
            return tuple(out)

        zero = jnp.zeros((t, LANE), F32)
        first = lax.div(j, BWD_QW)
        carry = chunk(first, tuple((zero, zero) for _ in range(g)), True)
        carry = lax.fori_loop(first + 1, s // tq, lambda i, c: chunk(i, c, False), carry)
        for hh in range(g):
            dk_ref[hh] = carry[hh][0].astype(BF16)
            dv_ref[hh] = carry[hh][1].astype(BF16)

        @pl.when(j == n - 1)
        def _():
            for hh in range(g):
                dq_ref[hh] = (dqt_sc[hh] * ATT_SCALE).T

    whole = pl.BlockSpec((g, s, LANE), lambda h, j: (h, 0, 0))
    kspec = pl.BlockSpec((g, t, LANE), lambda h, j: (h, j, 0))
    ospec = pl.BlockSpec((s, g * LANE), lambda h, j: (0, h))
    res = pl.pallas_call(
        body, name="flash_bwd_exchange" if nx else "flash_bwd", grid=(HEADS // g, n),
        out_shape=[jax.ShapeDtypeStruct((HEADS, s, LANE), F32), jax.ShapeDtypeStruct((HEADS, s, LANE), BF16),
                   jax.ShapeDtypeStruct((HEADS, s, LANE), BF16)] + [jax.ShapeDtypeStruct(a.shape, a.dtype) for a in (pending or [])],
        in_specs=[whole, kspec, kspec, ospec, ospec, pl.BlockSpec((g, 1, s), lambda h, j: (h, 0, 0))] + [ANY] * nx,
        out_specs=[whole, kspec, kspec] + [ANY] * nx,
        scratch_shapes=[pltpu.VMEM((g, LANE, s), BF16), pltpu.VMEM((g, LANE, s), BF16), pltpu.VMEM((g, 8, s), F32),
                        pltpu.VMEM((g, LANE, s), F32), pltpu.VMEM((g, s, LANE), BF16)] + (_comm_scratch(nx) if nx else []),
        compiler_params=_cp("arbitrary", "arbitrary"),
    )(q, k, kv, cat, dcat, lse, *(pending or []))
    return res[0], res[1], res[2], list(res[3:])


def _conv3(ext, w_ref, ts):
    return (w_ref[0:1, :] * ext[pl.ds(HALO - 2, ts), :] + w_ref[1:2, :] * ext[pl.ds(HALO - 1, ts), :]
            + w_ref[2:3, :] * ext[pl.ds(HALO, ts), :])


def _conv3_rows(ext, w_ref, r):
    return (w_ref[0:1, :] * ext[pl.ds(HALO - 2 + r, ROW_CHUNK), :] + w_ref[1:2, :] * ext[pl.ds(HALO - 1 + r, ROW_CHUNK), :]
            + w_ref[2:3, :] * ext[pl.ds(HALO + r, ROW_CHUNK), :])


def _conv3_t(ext2, w_ref, ts):
    return (w_ref[0:1, :] * ext2[pl.ds(2, ts), :] + w_ref[1:2, :] * ext2[pl.ds(1, ts), :] + w_ref[2:3, :] * ext2[pl.ds(0, ts), :])


def _sconv_fwd(proj, w):
    s = proj.shape[0]
    ts = min(ROW_TILE, s)

    def body(b_ref, c_ref, h_ref, hc_ref, hh_ref, w_ref, o_ref, ext):
        i = pl.program_id(0)
        ext[0:HALO, :] = hc_ref[...].astype(F32) * hh_ref[...].astype(F32) * (i > 0).astype(F32)
        ext[HALO:HALO + ts, :] = c_ref[...].astype(F32) * h_ref[...].astype(F32)
        o_ref[...] = (b_ref[...].astype(F32) * _conv3(ext, w_ref, ts)).astype(BF16)

    def col(cb):
        return pl.BlockSpec((ts, SC), lambda i: (i, cb))

    def halo(cb):
        return pl.BlockSpec((HALO, SC), lambda i: (_prev_halo(i, ts), cb))

    return pl.pallas_call(
        body, name="sconv_fwd", grid=(s // ts,), out_shape=jax.ShapeDtypeStruct((s, SC), BF16),
        in_specs=[col(2), col(3), col(4), halo(3), halo(4), pl.BlockSpec((3, SC), lambda i: (0, 0))],
        out_specs=pl.BlockSpec((ts, SC), lambda i: (i, 0)), scratch_shapes=[pltpu.VMEM((ts + HALO, SC), F32)],
        compiler_params=_cp("parallel"),
    )(proj, proj, proj, proj, proj, w)


def _sconv_bwd(proj, dcat, w):
    s = proj.shape[0]
    ts = min(ROW_TILE, s)
    n = s // ts

    def body(b_ref, c_ref, h_ref, hc_ref, hh_ref, dy_ref, ndy_ref, nb_ref, w_ref, d_ref, dw_ref, ext, ext2):
        i = pl.program_id(0)
        db_ref, dc_ref, dh_ref = d_ref.at[:, 0:SC], d_ref.at[:, SC:2 * SC], d_ref.at[:, 2 * SC:3 * SC]

        @pl.when(i == 0)
        def _():
            dw_ref[...] = jnp.zeros_like(dw_ref)

        cv, hv, bv = c_ref[...].astype(F32), h_ref[...].astype(F32), b_ref[...].astype(F32)
        ext[0:HALO, :] = hc_ref[...].astype(F32) * hh_ref[...].astype(F32) * (i > 0).astype(F32)
        ext[HALO:HALO + ts, :] = cv * hv
        dy = dy_ref[...].astype(F32)
        db_ref[...] = (dy * _conv3(ext, w_ref, ts)).astype(BF16)
        dyb = dy * bv
        ext2[0:ts, :] = dyb
        ext2[ts:ts + HALO, :] = ndy_ref[...].astype(F32) * nb_ref[...].astype(F32) * (i < n - 1).astype(F32)
        dg = _conv3_t(ext2, w_ref, ts)
        dc_ref[...] = (dg * hv).astype(BF16)
        dh_ref[...] = (dg * cv).astype(BF16)
        for kk in range(3):
            dw_ref[kk:kk + 1, :] += _rowsum(dyb * ext[pl.ds(HALO - 2 + kk, ts), :])

    def col(cb):
        return pl.BlockSpec((ts, SC), lambda i: (i, cb))

    def halo(cb):
        return pl.BlockSpec((HALO, SC), lambda i: (_prev_halo(i, ts), cb))

    def nxt(cb):
        return pl.BlockSpec((HALO, SC), lambda i: (_next_halo(i, ts, n), cb))

    out = jax.ShapeDtypeStruct((s, 3 * SC), BF16)
    o0 = pl.BlockSpec((ts, 3 * SC), lambda i: (i, 0))
    return pl.pallas_call(
        body, name="sconv_bwd", grid=(n,), out_shape=(out, jax.ShapeDtypeStruct((3, SC), F32)),
        in_specs=[col(2), col(3), col(4), halo(3), halo(4), col(4), nxt(4), nxt(2), pl.BlockSpec((3, SC), lambda i: (0, 0))],
        out_specs=(o0, pl.BlockSpec((3, SC), lambda i: (0, 0))),
        scratch_shapes=[pltpu.VMEM((ts + HALO, SC), F32), pltpu.VMEM((ts + HALO, SC), F32)], compiler_params=_cp("arbitrary"),
    )(proj, proj, proj, proj, proj, dcat, dcat, proj, w)


def _ffn_stage(ext, u_ref, halo_ref, i, ts):
    ext[0:HALO, :] = halo_ref[...].astype(F32) * (i > 0).astype(F32)
    ext[HALO:HALO + ts, :] = u_ref[...].astype(F32)


def _ffn_specs(ts):
    cur = pl.BlockSpec((2, None, ts, FB), lambda j, i: (0, j, i, 0))
    halo = pl.BlockSpec((2, None, HALO, FB), lambda j, i: (0, j, _prev_halo(i, ts), 0))
    w = pl.BlockSpec((2, None, 3, FB), lambda j, i: (0, j, 0, 0))
    b = pl.BlockSpec((2, None, 1, FB), lambda j, i: (0, j, 0, 0))
    return cur, halo, w, b


def _ffn_act(upre, fcw, fcb):
    s = upre.shape[1]
    ts = min(ROW_TILE, s)

    def body(u_ref, halo_ref, w_ref, b_ref, o_ref, ext_g, ext_u):
        i = pl.program_id(1)
        _ffn_stage(ext_g, u_ref.at[0], halo_ref.at[0], i, ts)
        _ffn_stage(ext_u, u_ref.at[1], halo_ref.at[1], i, ts)
        for r in range(0, ts, ROW_CHUNK):
            gate = b_ref[0] + _conv3_rows(ext_g, w_ref.at[0], r)
            up = b_ref[1] + _conv3_rows(ext_u, w_ref.at[1], r)
            o_ref[pl.ds(r, ROW_CHUNK), :] = (gate * _sig(gate) * up).astype(BF16)

    cur, halo, w, b = _ffn_specs(ts)
    u4 = upre.reshape(2, 4, s, FB)
    return pl.pallas_call(
        body, name="ffn_act", grid=(4, s // ts), out_shape=jax.ShapeDtypeStruct((4, s, FB), BF16),
        in_specs=[cur, halo, w, b], out_specs=pl.BlockSpec((None, ts, FB), lambda j, i: (j, i, 0)),
        scratch_shapes=[pltpu.VMEM((ts + HALO, FB), F32), pltpu.VMEM((ts + HALO, FB), F32)], compiler_params=_cp("parallel", "parallel"),
    )(u4, u4, fcw.reshape(2, 4, 3, FB), fcb.reshape(2, 4, 1, FB))


def _ffn_bwd(upre, dact, fcw, fcb):
    s = upre.shape[1]
    ts = min(ROW_TILE, s)
    n = s // ts
    te = ts + HALO

    def body(u_ref, halo_ref, nxt_ref, w_ref, b_ref, da_ref, nda_ref, dup_ref, db_ref, dw_ref, ext_g, ext_u, ext_da, du_g, du_u, acc):
        i = pl.program_id(1)

        @pl.when(i == 0)
        def _():
            db_ref[...] = jnp.zeros_like(db_ref)
            dw_ref[...] = jnp.zeros_like(dw_ref)

        more = (i < n - 1).astype(F32)
        for idx, ext in ((0, ext_g), (1, ext_u)):
            _ffn_stage(ext, u_ref.at[idx], halo_ref.at[idx], i, ts)
            ext[HALO + ts:HALO + te, :] = nxt_ref[idx].astype(F32) * more
        ext_da[0:ts, :] = da_ref[...].astype(F32)
        ext_da[ts:te, :] = nda_ref[...].astype(F32) * more
        acc[...] = jnp.zeros_like(acc)
        for r in range(0, te, ROW_CHUNK):
            rows = pl.ds(r, ROW_CHUNK)
            taps = [[ext[pl.ds(HALO - 2 + kk + r, ROW_CHUNK), :] for kk in range(3)] for ext in (ext_g, ext_u)]
            gate = b_ref[0] + w_ref[0, 0:1, :] * taps[0][0] + w_ref[0, 1:2, :] * taps[0][1] + w_ref[0, 2:3, :] * taps[0][2]
            up = b_ref[1] + w_ref[1, 0:1, :] * taps[1][0] + w_ref[1, 1:2, :] * taps[1][1] + w_ref[1, 2:3, :] * taps[1][2]
            sg = _sig(gate)
            da = ext_da[rows, :]
            dgate = da * up * sg * (1.0 + gate * (1.0 - sg))
            dup = da * gate * sg
            du_g[rows, :] = dgate
            du_u[rows, :] = dup
            if r < ts:
                acc[0] += dgate
                acc[1] += dup
                for kk in range(3):
                    acc[2 + kk] += dgate * taps[0][kk]
                    acc[5 + kk] += dup * taps[1][kk]
        for r in range(0, ts, ROW_CHUNK):
            for idx, du in ((0, du_g), (1, du_u)):
                dupre = (w_ref[idx, 2:3, :] * du[pl.ds(r, ROW_CHUNK), :] + w_ref[idx, 1:2, :] * du[pl.ds(r + 1, ROW_CHUNK), :]
                         + w_ref[idx, 0:1, :] * du[pl.ds(r + 2, ROW_CHUNK), :])
                dup_ref[idx, pl.ds(r, ROW_CHUNK), :] = dupre.astype(BF16)
        for idx in range(2):
            db_ref[idx] += _rowsum(acc[idx])
            for kk in range(3):
                dw_ref[idx, kk:kk + 1, :] += _rowsum(acc[2 + 3 * idx + kk])

    cur, halo, w, b = _ffn_specs(ts)
    nxt = pl.BlockSpec((2, None, HALO, FB), lambda j, i: (0, j, _next_halo(i, ts, n), 0))
    u4 = upre.reshape(2, 4, s, FB)
    dupre, db, dw = pl.pallas_call(
        body, name="ffn_bwd", grid=(4, n),
        out_shape=(jax.ShapeDtypeStruct((2, 4, s, FB), BF16), jax.ShapeDtypeStruct((2, 4, 1, FB), F32),
                   jax.ShapeDtypeStruct((2, 4, 3, FB), F32)),
        in_specs=[cur, halo, nxt, w, b, pl.BlockSpec((None, ts, FB), lambda j, i: (j, i, 0)),
                  pl.BlockSpec((None, HALO, FB), lambda j, i: (j, _next_halo(i, ts, n), 0))],
        out_specs=(cur, b, w),
        scratch_shapes=[pltpu.VMEM((te + HALO, FB), F32), pltpu.VMEM((te + HALO, FB), F32), pltpu.VMEM((te, FB), F32),
                        pltpu.VMEM((te, FB), F32), pltpu.VMEM((te, FB), F32), pltpu.VMEM((8, ROW_CHUNK, FB), F32)],
        compiler_params=_cp("parallel", "arbitrary"),
    )(u4, u4, u4, fcw.reshape(2, 4, 3, FB), fcb.reshape(2, 4, 1, FB), dact, dact)
    return dupre.reshape(NDEV, s, FB), db.reshape(NDEV, 1, FB), dw.reshape(NDEV, 3, FB)


def _softplus(v):
    e = jnp.exp(-jnp.abs(v))
    return jnp.maximum(v, 0.0) + jnp.where(e < 1e-4, e * (1.0 - 0.5 * e), jnp.log(1.0 + e))


def _ssd_consts():
    L = SSD_L
    r = lax.broadcasted_iota(jnp.int32, (L, L), 0)
    c = lax.broadcasted_iota(jnp.int32, (L, L), 1)
    tri = r >= c
    er = lax.broadcasted_iota(jnp.int32, (LANE, SSD_DIM), 0)
    ec = lax.broadcasted_iota(jnp.int32, (LANE, SSD_DIM), 1)
    expand = ((ec >= er * 64) & (ec < er * 64 + 64)).astype(F32)
    return tri, expand


def _ssd_conv4(ext, cw_ref, cb_ref):
    L = SSD_L
    pre = cb_ref[...] + cw_ref[0:1, :] * ext[pl.ds(HALO - 3, L), :]
    for kk in range(1, 4):
        pre = pre + cw_ref[kk:kk + 1, :] * ext[pl.ds(HALO - 3 + kk, L), :]
    return pre


def _ssd_common(xbc_ref, halo_ref, dt_ref, cw_ref, cb_ref, dtb_ref, alog_ref, ext, first):
    L = SSD_L
    tri, expand = _ssd_consts()
    ext[0:HALO, :] = halo_ref[...].astype(F32) * (1.0 - first.astype(F32))
    ext[HALO:HALO + L, :] = xbc_ref[...].astype(F32)
    pre = _ssd_conv4(ext, cw_ref, cb_ref)
    sg = _sig(pre)
    act = pre * sg
    lane = lax.broadcasted_iota(jnp.int32, (1, LANE), 1)
    m4 = lane < SSD_H
    raw = dt_ref[...].astype(F32) + dtb_ref[...]
    dtv = jnp.where(m4, _softplus(raw), 0.0)
    av = jnp.where(m4, -jnp.exp(alog_ref[...]), 0.0)
    adt = dtv * av
    acs = _dot01(adt, tri, x_first=False)
    acs_b = _dot01(acs, expand)
    dt_b = _dot01(dtv, expand)
    return dict(tri=tri, expand=expand, pre=pre, sg=sg, act=act, raw=raw, dtv=dtv, av=av, m4=m4, acs=acs, acs_b=acs_b,
                dt_b=dt_b, lane=lane)


def _head_terms(cm, h):
    L = SSD_L
    acs, tri = cm["acs"], cm["tri"]
    lane_l = lax.broadcasted_iota(jnp.int32, (L, LANE), 1)
    sub_l = lax.broadcasted_iota(jnp.int32, (LANE, L), 0)
    col = jnp.sum(jnp.where(lane_l == h, acs, 0.0), axis=1, keepdims=True)
    row = jnp.sum(jnp.where(sub_l == h, acs.T, 0.0), axis=0, keepdims=True)
    dec = jnp.where(tri, jnp.exp(jnp.where(tri, col - row, NEG)), 0.0)
    rowi = lax.broadcasted_iota(jnp.int32, (L, 1), 0)
    last = jnp.sum(jnp.where(rowi == L - 1, col, 0.0), axis=0, keepdims=True)
    dte = jnp.exp(last - col)
    return col, dec, last, dte


def _ssd_fwd(proj, cw, cb, dtb, alog, dvec, nw):
    s = proj.shape[0]
    L = SSD_L
    nc = s // L

    def body(z_ref, xbc_ref, halo_ref, dt_ref, cw_ref, cb_ref, dtb_ref, alog_ref, d_ref, nw_ref, y_ref, ypre_ref, st_ref, ext, state):
        i = pl.program_id(0)

        @pl.when(i == 0)
        def _():
            state[...] = jnp.zeros_like(state)

        cm = _ssd_common(xbc_ref, halo_ref, dt_ref, cw_ref, cb_ref, dtb_ref, alog_ref, ext, i == 0)
        act = cm["act"]
        xs = act[:, 0:256]
        bm = (act[:, 256:384], act[:, 384:512])
        cmat = (act[:, 512:640].astype(BF16), act[:, 640:768].astype(BF16))
        xdt = xs * cm["dt_b"]
        prev = state[...]
        st_ref[...] = prev
        prev_bf = prev.astype(BF16)
        gm = [_dot(cmat[g], bm[g].astype(BF16), NT) for g in range(2)]
        lane2 = lax.broadcasted_iota(jnp.int32, (1, SSD_DIM), 1)
        rows2 = lax.broadcasted_iota(jnp.int32, (SSD_DIM, 1), 0)
        ydiag = jnp.zeros((L, SSD_DIM), F32)
        contrib = jnp.zeros((SSD_DIM, LANE), F32)
        cd_rows = jnp.zeros((SSD_DIM, 1), F32)
        for h in range(SSD_H):
            g = h // 2
            col, dec, last, dte = _head_terms(cm, h)
            mh = (lane2 >= 64 * h) & (lane2 < 64 * h + 64)
            xm = jnp.where(mh, xdt, 0.0).astype(BF16)
            ydiag += _dot((gm[g] * dec).astype(BF16), xm)
            contrib += _dot(xm, (bm[g] * dte).astype(BF16), TN)
            cd_rows += jnp.where((rows2 >= 64 * h) & (rows2 < 64 * h + 64), jnp.exp(last), 0.0)
        yo = jnp.where(lane2 < 128, _dot(cmat[0], prev_bf, NT), _dot(cmat[1], prev_bf, NT))
        y = ydiag + yo * jnp.exp(cm["acs_b"]) + xs * d_ref[...]
        state[...] = prev * cd_rows + contrib
        ypre_ref[...] = y
        zz = z_ref[...].astype(F32)
        gt = y * zz * _sig(zz)
        y_ref[...] = (gt * lax.rsqrt(jnp.mean(gt * gt, axis=-1, keepdims=True) + EPS) * nw_ref[...]).astype(BF16)

    def vec(w):
        return pl.BlockSpec((1, w), lambda i: (0, 0))

    return pl.pallas_call(
        body, name="ssd_fwd", grid=(nc,),
        out_shape=(jax.ShapeDtypeStruct((s, SSD_DIM), BF16), jax.ShapeDtypeStruct((s, SSD_DIM), F32),
                   jax.ShapeDtypeStruct((nc, SSD_DIM, LANE), F32)),
        in_specs=[pl.BlockSpec((L, SSD_DIM), lambda i: (i, 5)), pl.BlockSpec((L, SSD_CONV), lambda i: (i, 2)),
                  pl.BlockSpec((HALO, SSD_CONV), lambda i: (_prev_halo(i, L), 2)), pl.BlockSpec((L, LANE), lambda i: (i, 18)),
                  pl.BlockSpec((4, SSD_CONV), lambda i: (0, 0)), vec(SSD_CONV), vec(LANE), vec(LANE), vec(SSD_DIM), vec(SSD_DIM)],
        out_specs=(pl.BlockSpec((L, SSD_DIM), lambda i: (i, 0)), pl.BlockSpec((L, SSD_DIM), lambda i: (i, 0)),
                   pl.BlockSpec((None, SSD_DIM, LANE), lambda i: (i, 0, 0))),
        scratch_shapes=[pltpu.VMEM((L + HALO, SSD_CONV), F32), pltpu.VMEM((SSD_DIM, LANE), F32)], compiler_params=_cp("arbitrary"),
    )(proj, proj, proj, proj, cw, cb, dtb, alog, dvec, nw)


def _ssd_bwd(proj, dcat, ypre, states, cw, cb, dtb, alog, dvec, nw):
    s = proj.shape[0]
    L = SSD_L
    nc = s // L

    def body(z_ref, xbc_ref, halo_ref, dt_ref, dy_ref, ypre_ref, st_ref, cw_ref, cb_ref, dtb_ref, alog_ref, d_ref, nw_ref,
             din_ref, dcw_ref, dcb_ref, ddtb_ref, dalog_ref, dd_ref, dnw_ref, ext, ext2, carry, dstate, ddl):
        i = pl.program_id(0)
        dz_ref, dxbc_ref = din_ref.at[:, 0:SSD_DIM], din_ref.at[:, SSD_DIM:SSD_DIM + SSD_CONV]
        ddt_ref = din_ref.at[:, SSD_DIM + SSD_CONV:SSD_DIM + SSD_CONV + LANE]
        r = nc - 1 - i

        @pl.when(i == 0)
        def _():
            for ref in (dcw_ref, dcb_ref, ddtb_ref, dalog_ref, dd_ref, dnw_ref, carry, dstate, ddl):
                ref[...] = jnp.zeros_like(ref)

        cm = _ssd_common(xbc_ref, halo_ref, dt_ref, cw_ref, cb_ref, dtb_ref, alog_ref, ext, r == 0)
        tri, expand, act = cm["tri"], cm["expand"], cm["act"]
        xs = act[:, 0:256]
        bm = (act[:, 256:384], act[:, 384:512])
        cmat = (act[:, 512:640], act[:, 640:768])
        bm_bf = [v.astype(BF16) for v in bm]
        cm_bf = [v.astype(BF16) for v in cmat]
        dt_b = cm["dt_b"]
        xdt = xs * dt_b
        xdt_bf = xdt.astype(BF16)
        ea_b = jnp.exp(cm["acs_b"])
        prev = st_ref[...]
        prev_bf = prev.astype(BF16)
        lane2 = lax.broadcasted_iota(jnp.int32, (1, SSD_DIM), 1)
        rows2 = lax.broadcasted_iota(jnp.int32, (SSD_DIM, 1), 0)
        lane_l = lax.broadcasted_iota(jnp.int32, (L, LANE), 1)
        rowi = lax.broadcasted_iota(jnp.int32, (L, 1), 0)

        y = ypre_ref[...]
        zz = z_ref[...].astype(F32)
        sz = _sig(zz)
        gt = y * zz * sz
        dgt, dwt = _rms_bwd_math(gt, nw_ref[...], dy_ref[...].astype(F32))
        dnw_ref[...] += _rowsum(dwt)
        dy = dgt * zz * sz
        dz_ref[...] = (dgt * y * sz * (1.0 + zz * (1.0 - sz))).astype(BF16)

        ddl[0:1, :] += _rowsum(dy * xs)
        dxs = dy * d_ref[...]

        yo = jnp.where(lane2 < 128, _dot(cm_bf[0], prev_bf, NT), _dot(cm_bf[1], prev_bf, NT))
        dacs_b = dy * yo * ea_b
        dyo = dy * ea_b
        dyo_g = (jnp.where(lane2 < 128, dyo, 0.0).astype(BF16), jnp.where(lane2 >= 128, dyo, 0.0).astype(BF16))
        dc = [_dot(dyo_g[g], prev_bf) for g in range(2)]
        dprev = _dot(dyo_g[0], cm_bf[0], TN) + _dot(dyo_g[1], cm_bf[1], TN)

        gm = [_dot(cm_bf[g], bm_bf[g], NT) for g in range(2)]
        dgm = [jnp.zeros((L, L), F32), jnp.zeros((L, L), F32)]
        db = [jnp.zeros((L, LANE), F32), jnp.zeros((L, LANE), F32)]
        dxdt = jnp.zeros((L, SSD_DIM), F32)
        dacs = jnp.zeros((L, LANE), F32)
        dlast = jnp.zeros((1, LANE), F32)
        cd_rows = jnp.zeros((SSD_DIM, 1), F32)
        dst = dstate[...]
        dst_bf = dst.astype(BF16)
        dsp = dst * prev
        ones = jnp.ones((L, LANE), F32)
        for h in range(SSD_H):
            g = h // 2
            col, dec, last, dte = _head_terms(cm, h)
            mh = (lane2 >= 64 * h) & (lane2 < 64 * h + 64)
            rh = (rows2 >= 64 * h) & (rows2 < 64 * h + 64)
            sc = gm[g] * dec
            xm = jnp.where(mh, xdt, 0.0).astype(BF16)
            dym = jnp.where(mh, dy, 0.0).astype(BF16)
            dsc = _dot(dym, xdt_bf, NT)
            dxdt += _dot(sc.astype(BF16), dym, TN)
            dgm[g] += dsc * dec
            dd = dsc * sc
            rs = jnp.sum(dd, axis=1, keepdims=True)
            cs = _dot01(dd, ones, TN)
            dacs += jnp.where(lane_l == h, rs - cs, 0.0)
            bd = (bm[g] * dte).astype(BF16)
            dxdt += jnp.where(mh, _dot(bd, dst_bf, NT), 0.0)
            dbd = _dot(xm, dst_bf)
            db[g] += dbd * dte
            tt = jnp.sum(dbd * bm[g], axis=1, keepdims=True) * dte
            dacs += jnp.where(lane_l == h, -tt, 0.0)
            cdh = jnp.exp(last)
            dcd = jnp.sum(jnp.sum(jnp.where(rh, dsp, 0.0), axis=1, keepdims=True), axis=0, keepdims=True)
            dlast += jnp.where(cm["lane"] == h, jnp.sum(tt, axis=0, keepdims=True) + dcd * cdh, 0.0)
            cd_rows += jnp.where(rh, cdh, 0.0)
        dacs += jnp.where(rowi == L - 1, dlast, 0.0)
        dacs += _dot01(dacs_b, expand, NT)
        dstate[...] = dprev + dst * cd_rows

        for g in range(2):
            dgb = dgm[g].astype(BF16)
            dc[g] += _dot(dgb, bm_bf[g])
            db[g] += _dot(dgb, cm_bf[g], TN)

        dadt = _dot01(dacs, tri, TN, x_first=False)
        ddtv = dadt * cm["av"] + _dot01(dxdt * xs, expand, NT)
        dalog_ref[...] += _rowsum(dadt * cm["dtv"]) * cm["av"]
        dxs += dxdt * dt_b
        draw = jnp.where(cm["m4"], ddtv * _sig(cm["raw"]), 0.0)
        ddtb_ref[...] += _rowsum(draw)
        ddt_ref[...] = draw.astype(BF16)

        dact = jnp.concatenate([dxs, db[0], db[1], dc[0], dc[1]], axis=1)
        sg, pre = cm["sg"], cm["pre"]
        dpre = dact * sg * (1.0 + pre * (1.0 - sg))
        dcb_ref[...] += _rowsum(dpre)
        for kk in range(4):
            dcw_ref[kk:kk + 1, :] += _rowsum(dpre * ext[pl.ds(HALO - 3 + kk, L), :])
        ext2[0:L, :] = dpre
        ext2[L:L + HALO, :] = carry[...]
        dx = cw_ref[3:4, :] * ext2[pl.ds(0, L), :]
        for kk in range(3):
            dx = dx + cw_ref[kk:kk + 1, :] * ext2[pl.ds(3 - kk, L), :]
        dxbc_ref[...] = dx.astype(BF16)
        carry[...] = dpre[0:HALO, :]

        @pl.when(i == nc - 1)
        def _():
            dd_ref[...] = _dot01(ddl[...], expand, NT)

    def vec(w):
        return pl.BlockSpec((1, w), lambda i: (0, 0))

    def rv(i):
        return nc - 1 - i

    return pl.pallas_call(
        body, name="ssd_bwd", grid=(nc,),
        out_shape=(jax.ShapeDtypeStruct((s, SSD_DIM + SSD_CONV + LANE), BF16),
                   jax.ShapeDtypeStruct((4, SSD_CONV), F32), jax.ShapeDtypeStruct((1, SSD_CONV), F32), jax.ShapeDtypeStruct((1, LANE), F32),
                   jax.ShapeDtypeStruct((1, LANE), F32), jax.ShapeDtypeStruct((8, LANE), F32), jax.ShapeDtypeStruct((1, SSD_DIM), F32)),
        in_specs=[pl.BlockSpec((L, SSD_DIM), lambda i: (rv(i), 5)), pl.BlockSpec((L, SSD_CONV), lambda i: (rv(i), 2)),
                  pl.BlockSpec((HALO, SSD_CONV), lambda i: (_prev_halo(rv(i), L), 2)), pl.BlockSpec((L, LANE), lambda i: (rv(i), 18)),
                  pl.BlockSpec((L, SSD_DIM), lambda i: (rv(i), 5)), pl.BlockSpec((L, SSD_DIM), lambda i: (rv(i), 0)),
                  pl.BlockSpec((None, SSD_DIM, LANE), lambda i: (rv(i), 0, 0)),
                  pl.BlockSpec((4, SSD_CONV), lambda i: (0, 0)), vec(SSD_CONV), vec(LANE), vec(LANE), vec(SSD_DIM), vec(SSD_DIM)],
        out_specs=(pl.BlockSpec((L, SSD_DIM + SSD_CONV + LANE), lambda i: (rv(i), 0)), pl.BlockSpec((4, SSD_CONV), lambda i: (0, 0)), vec(SSD_CONV),
                   vec(LANE), vec(LANE), pl.BlockSpec((8, LANE), lambda i: (0, 0)), vec(SSD_DIM)),
        scratch_shapes=[pltpu.VMEM((L + HALO, SSD_CONV), F32), pltpu.VMEM((L + HALO, SSD_CONV), F32), pltpu.VMEM((HALO, SSD_CONV), F32),
                        pltpu.VMEM((SSD_DIM, LANE), F32), pltpu.VMEM((8, SSD_DIM), F32)],
        compiler_params=_cp("arbitrary"),
    )(proj, proj, proj, proj, dcat, ypre, states, cw, cb, dtb, alog, dvec, nw)


def _adamw(parts, w, m, v, name):
    nl, r, c = w.shape
    tr = r
    for cand in (256, 128, 64, 32, 16, 8):
        if r % cand == 0 and (cand * c * 4) <= 2 * 1024 * 1024:
            tr = cand
            break
    c1 = 1.0 - B1 ** STEP
    c2 = 1.0 - B2 ** STEP

    nrow = r // tr

    def update(p_ref, w_ref, m_ref, v_ref, g_ref, d_ref, nm_ref, nv_ref):
        g = p_ref[0].astype(F32)
        for dev in range(1, NDEV):
            g = g + p_ref[dev].astype(F32)
        mn = B1 * m_ref[...] + (1.0 - B1) * g
        vn = B2 * v_ref[...] + (1.0 - B2) * (g * g)
        g_ref[...] = g
        nm_ref[...] = mn
        nv_ref[...] = vn
        d_ref[...] = -LR * ((mn / c1) / (jnp.sqrt(vn / c2) + AEPS) + WD * w_ref[...])

    def body(*refs):
        for k in range(nl):
            pl.when(pl.program_id(0) == k)(functools.partial(update, refs[k], *refs[nl:]))

    def part_spec(k):
        return pl.BlockSpec((NDEV, tr, c), lambda l, i: (0, jnp.where(l == k, i, jnp.where(l < k, 0, nrow - 1)), 0))

    blk = pl.BlockSpec((None, tr, c), lambda l, i: (l, i, 0))
    out = jax.ShapeDtypeStruct((nl, r, c), F32)
    return pl.pallas_call(
        body, name=name, grid=(nl, nrow), out_shape=(out, out, out, out),
        in_specs=[part_spec(k) for k in range(nl)] + [blk, blk, blk], out_specs=(blk, blk, blk, blk),
        compiler_params=_cp("arbitrary", "arbitrary"),
    )(*parts, w, m, v)


def _pad_win(w):
    z = lambda n: jnp.zeros((w.shape[0], n), w.dtype)
    return jnp.concatenate([w[:, :384], z(64), w[:, 384:416], z(32), w[:, 416:], z(124)], axis=1)


def _unpad_win(g_mla, g_sc, g_ssd):
    return jnp.concatenate([g_mla[:, :384], g_mla[:, 448:480], g_sc, g_ssd[:, :1028]], axis=1)


def _pad_wout(w):
    att = jnp.pad(w[:512].reshape(HEADS, 64, D), ((0, 0), (64, 0), (0, 0))).reshape(HEADS * LANE, D)
    return jnp.concatenate([att, w[512:]], axis=0)


def _unpad_wout(g_att, g_conv, g_ssd):
    att = g_att.reshape(HEADS, LANE, D)[:, 64:, :].reshape(512, D)
    return jnp.concatenate([att, g_conv, g_ssd], axis=0)


def _lanes(v, n=LANE):
    return jnp.pad(v, (0, n - v.shape[0])).reshape(1, n)


def _prep_ffn(big):
    return {"wout": _pad_wout(big["w_out"].reshape(1024, D)), "wup": big["ffn_w_up"], "fcw": big["ffn_conv_w"].astype(F32),
            "wdown": big["ffn_w_down"].reshape(4, FB, D)}


def _prep_layer(big, small, l):
    p = _prep_ffn(big) if "w_out" in big else {}
    p["win"] = _pad_win(big["w_in"].reshape(D, 2212))
    p["wq"] = jnp.pad(big["mla_w_q_up"], ((0, 0), (0, 0), (0, LANE - 96)))
    p["wkv"] = big["mla_w_kv_up"]
    p["scw"] = big["sc_conv_w"].astype(F32).transpose(1, 0, 2).reshape(3, SC)
    p["ssdcw"] = big["ssd_conv_w"].astype(F32).transpose(1, 0, 2).reshape(4, SSD_CONV)
    for nm in ("norm_mix_pre", "norm_mix_post", "norm_ffn_pre", "norm_ffn_post", "mla_q_norm", "mla_kv_norm", "ssd_conv_b", "ssd_norm"):
        p[nm] = small[nm][l].reshape(1, -1)
    p["dtb"] = _lanes(small["ssd_dt_bias"][l])
    p["alog"] = _lanes(small["ssd_a_log"][l])
    p["dvec"] = jnp.repeat(small["ssd_d"][l], 64).reshape(1, SSD_DIM)
    p["fcb"] = small["ffn_conv_b"][l].reshape(NDEV, 1, FB)
    return p


def _rope_tables(positions):
    inv_freq = 1.0 / (ROPE_THETA ** (jnp.arange(0, ROPE, 2, dtype=F32) / ROPE))
    ang = positions.astype(F32)[:, None] * inv_freq
    cos, sin = jnp.cos(ang), jnp.sin(ang)
    s = positions.shape[0]
    z = lambda n: jnp.zeros((s, n), F32)
    tc = jnp.concatenate([jnp.ones((s, 64), F32), cos, cos, z(32)], axis=1)
    ta = jnp.concatenate([z(64), -sin, z(48)], axis=1)
    tb = jnp.concatenate([z(80), sin, z(32)], axis=1)
    return tc, ta, tb


def _layer_fwd(xv, p, tabs, prefetch=None, prep_rest=None, h=None, next_norm=None):
    if h is None:
        h = _rms(xv, p["norm_mix_pre"], BF16, "rms_pre")
    proj = _mm_rows("in_proj", h, p["win"], BF16, NN)
    q, k, kv = _mla_prep(proj, tabs, p["mla_q_norm"], p["mla_kv_norm"], p["wq"], p["wkv"])
    o, lse, gathered = _flash_fwd(q, k, kv, prefetch)
    if prep_rest is not None:
        p = {**p, **prep_rest(gathered)}
    yconv = _sconv_fwd(proj, p["scw"])
    yssd, ypre, states = _ssd_fwd(proj, p["ssdcw"], p["ssd_conv_b"], p["dtb"], p["alog"], p["dvec"], p["ssd_norm"])
    mixed = _mm_sum("out_proj", [o, yconv, yssd], [p["wout"][:HEADS * LANE], p["wout"][HEADS * LANE:HEADS * LANE + SC],
                                                  p["wout"][HEADS * LANE + SC:]], NN)
    x1, h2 = _add_rms(xv, mixed, p["norm_mix_post"], "add_rms", p["norm_ffn_pre"])
    upre = _mm_up(h2, p["wup"])
    act = _ffn_act(upre, p["fcw"], p["fcb"])
    f = _mm_down(act, p["wdown"])
    x2 = _add_rms(x1, f, p["norm_ffn_post"], "add_rms", next_norm)
    saved = dict(x=xv, h=h, proj=proj, q=q, k=k, kv=kv, lse=lse, ypre=ypre, states=states, o=o, yconv=yconv, yssd=yssd, mixed=mixed, x1=x1, h2=h2,
                 upre=upre, act=act, f=f)
    return x2, saved, p, gathered


def _pack_grads(grads, group_ids):
    return [_group_pack(GROUPS[gi], lambda n: grads[n].reshape((NDEV,) + _rows2(n, True)), (NDEV,)) for gi in group_ids]


def _layer_bwd(dx2, sv, p, tabs, exchange=False, pending=None, head=None, below=None):
    df, g_nfpo = head if head is not None else _rms_bwd(sv["f"], p["norm_ffn_post"], dx2, None, BF16, "rms_bwd_post")
    dact = _mm_dact(df, p["wdown"])
    g_wdown = _mm_dwdown(sv["act"], df)
    dupre, g_fcb, g_fcw = _ffn_bwd(sv["upre"], dact, p["fcw"], p["fcb"])
    dh2 = _mm_dh2(dupre, p["wup"])
    g_wup = _mm_dwup(sv["h2"], dupre)
    dx1, dmixed, g_nfp, g_nmpo = _rms_bwd2(sv["x1"], p["norm_ffn_pre"], dh2, dx2, sv["mixed"], p["norm_mix_post"])
    dcat = _mm_rows("dcat", dmixed, p["wout"], BF16, NT)
    g_wout = _mm_dwout([sv["o"], sv["yconv"], sv["yssd"]], dmixed)
    big = {
        "w_out": _unpad_wout(*g_wout).reshape(NDEV, 128, D),
        "ffn_w_up": g_wup,
        "ffn_conv_w": g_fcw.astype(BF16),
        "ffn_w_down": g_wdown.reshape(NDEV, 352, D),
    }
    outgoing = _pack_grads(big, FFN_SIDE) + (pending or []) if exchange else None
    dq, dk, dv, received = _flash_bwd(sv["q"], sv["k"], sv["kv"], sv["o"], dcat, sv["lse"], outgoing)
    d_mla, g_wq, g_wkv, g_qn, g_kvn = _mla_prep_bwd(sv["proj"], tabs, p["mla_q_norm"], p["mla_kv_norm"], p["wq"], p["wkv"], dq, dk, dv)
    d_sc, g_scw = _sconv_bwd(sv["proj"], dcat, p["scw"])
    d_ssd, g_cw, g_cb, g_dtb, g_alog, g_d, g_nw = _ssd_bwd(
        sv["proj"], dcat, sv["ypre"], sv["states"], p["ssdcw"], p["ssd_conv_b"], p["dtb"], p["alog"], p["dvec"], p["ssd_norm"])
    dparts = [d_mla, d_sc, d_ssd]
    dh = _mm_sum("dh", dparts, [p["win"][:, 0:512], p["win"][:, 512:1280], p["win"][:, 1280:PW]], NT)
    g_win = _mm_dwin(sv["h"], dparts)
    big.update({
        "w_in": _unpad_win(*g_win).reshape(NDEV, 128, 2212),
        "mla_w_q_up": g_wq[:, :, :96].astype(BF16),
        "mla_w_kv_up": g_wkv.astype(BF16),
        "sc_conv_w": g_scw.reshape(3, NDEV, 32).transpose(1, 0, 2).astype(BF16),
        "ssd_conv_w": g_cw.reshape(4, NDEV, 96).transpose(1, 0, 2).astype(BF16),
    })
    head_below, last_received = None, None
    if below is not None:
        dx, df_below, g_nmp, g_below = _rms_bwd2(sv["x"], p["norm_mix_pre"], dh, dx1, *below)
        head_below = (df_below, g_below)
    elif exchange:
        dx, g_nmp, last_received = _rms_bwd(sv["x"], p["norm_mix_pre"], dh, dx1, F32, "rms_bwd_pre", _pack_grads(big, ATT_SIDE))
    else:
        dx, g_nmp = _rms_bwd(sv["x"], p["norm_mix_pre"], dh, dx1, F32, "rms_bwd_pre")
    small = {
        "norm_mix_pre": g_nmp[0], "norm_mix_post": g_nmpo[0], "norm_ffn_pre": g_nfp[0], "norm_ffn_post": g_nfpo[0],
        "mla_q_norm": g_qn[0], "mla_kv_norm": g_kvn[0], "ssd_conv_b": g_cb[0], "ssd_dt_bias": g_dtb[0, :SSD_H],
        "ssd_a_log": g_alog[0, :SSD_H], "ssd_d": g_d[0, :SSD_H], "ssd_norm": g_nw[0], "ffn_conv_b": g_fcb.reshape(-1),
    }
    return dx, big, small, received, head_below, last_received


def _local_step(xv, positions, target, layers):
    tabs = _rope_tables(positions)
    saved = []
    for p in layers:
        xv, sv, _, _ = _layer_fwd(xv, p, tabs)
        saved.append(sv)
    loss, dx = _loss_head(xv, target)
    bigs, smalls = [None] * DEPTH, [None] * DEPTH
    head = None
    for l in reversed(range(len(layers))):
        below = (saved[l - 1]["f"], layers[l - 1]["norm_ffn_post"]) if l > 0 else None
        dx, bigs[l], smalls[l], _, head, _ = _layer_bwd(dx, saved[l], layers[l], tabs, head=head, below=below)
    return loss[0, 0], dx, bigs, smalls


def _rows2(n, layer=False):
    shape = SHAPES[n][1:] if layer else SHAPES[n]
    return (math.prod(shape[:-1]), shape[-1])


def _group_pack(group, get, lead):
    width, names = group
    pieces = []
    for n in names:
        rows, cols = _rows2(n, True)
        pad = [(0, 0)] * len(lead) + [(0, -rows % 16), (0, width - cols)]
        pieces.append(jnp.pad(get(n), pad))
    return pieces[0] if len(pieces) == 1 else jnp.concatenate(pieces, axis=len(lead))


def _group_unpack(group, buf):
    _, names = group
    res, off = {}, 0
    for n in names:
        rows, cols = _rows2(n, True)
        res[n] = buf[:, off:off + rows, :cols]
        off += rows + (-rows % 16)
    return res


def kernel(x, positions, norm_mix_pre, norm_mix_post, norm_ffn_pre, norm_ffn_post, w_in, mla_q_norm, mla_w_q_up, mla_kv_norm, mla_w_kv_up, sc_conv_w, ssd_conv_w, ssd_conv_b, ssd_dt_bias, ssd_a_log, ssd_d, ssd_norm, w_out, ffn_w_up, ffn_conv_w, ffn_conv_b, ffn_w_down, loss_target, m_norm_mix_pre, m_norm_mix_post, m_norm_ffn_pre, m_norm_ffn_post, m_w_in, m_mla_q_norm, m_mla_w_q_up, m_mla_kv_norm, m_mla_w_kv_up, m_sc_conv_w, m_ssd_conv_w, m_ssd_conv_b, m_ssd_dt_bias, m_ssd_a_log, m_ssd_d, m_ssd_norm, m_w_out, m_ffn_w_up, m_ffn_conv_w, m_ffn_conv_b, m_ffn_w_down, v_norm_mix_pre, v_norm_mix_post, v_norm_ffn_pre, v_norm_ffn_post, v_w_in, v_mla_q_norm, v_mla_w_q_up, v_mla_kv_norm, v_mla_w_kv_up, v_sc_conv_w, v_ssd_conv_w, v_ssd_conv_b, v_ssd_dt_bias, v_ssd_a_log, v_ssd_d, v_ssd_norm, v_w_out, v_ffn_w_up, v_ffn_conv_w, v_ffn_conv_b, v_ffn_w_down):
    given = dict(locals())
    w = {n: given[n] for n in WEIGHTS}
    m = {n: given["m_" + n] for n in WEIGHTS}
    v = {n: given["v_" + n] for n in WEIGHTS}

    def shards(l, group_ids):
        return [_group_pack(GROUPS[gi], lambda n: w[n][l].astype(BF16).reshape(_rows2(n, True)), ()) for gi in group_ids]

    def unpacked(bufs, group_ids):
        big = {}
        for gi, buf in zip(group_ids, bufs):
            for n, piece in _group_unpack(GROUPS[gi], buf).items():
                big[n] = piece.reshape((NDEV,) + SHAPES[n][1:])
        return big

    small_w = {n: w[n] for n, _ in SMALL}
    tabs = _rope_tables(positions[0])
    xv, h, layers, saved = x[0], None, [], []
    att = _all_gather(shards(0, ATT_SIDE), "gather_weights")
    for l in range(DEPTH):
        prefetch = shards(l, FFN_SIDE) + (shards(l + 1, ATT_SIDE) if l + 1 < DEPTH else [])
        nxt = w["norm_mix_pre"][l + 1].reshape(1, D) if l + 1 < DEPTH else None
        xv, sv, p, gathered = _layer_fwd(xv, _prep_layer(unpacked(att, ATT_SIDE), small_w, l), tabs, prefetch,
                                         lambda got: _prep_ffn(unpacked(got[:len(FFN_SIDE)], FFN_SIDE)), h, nxt)
        xv, h = xv if nxt is not None else (xv, None)
        att = gathered[len(FFN_SIDE):]
        layers.append(p)
        saved.append(sv)
    loss, dx = _loss_head(xv, loss_target[0])
    loss = lax.psum(loss[0, 0], ("x", "y", "c"))

    smalls, pending, head = [None] * DEPTH, None, None
    recvs = [[None] * len(GROUPS) for _ in range(DEPTH)]
    for l in reversed(range(DEPTH)):
        below = (saved[l - 1]["f"], layers[l - 1]["norm_ffn_post"]) if l > 0 else None
        dx, grads, smalls[l], received, head, last = _layer_bwd(dx, saved[l], layers[l], tabs, True, pending, head, below)
        for pos, gi in enumerate(FFN_SIDE):
            recvs[l][gi] = received[pos]
        if pending is not None:
            for pos, gi in enumerate(ATT_SIDE):
                recvs[l + 1][gi] = received[len(FFN_SIDE) + pos]
        pending = _pack_grads(grads, ATT_SIDE) if l > 0 else None
    for gi, buf in zip(ATT_SIDE, last):
        recvs[0][gi] = buf
    out = {}
    for gi, g in enumerate(GROUPS):
        per_layer = [_group_unpack(g, recvs[l][gi]) for l in range(DEPTH)]
        for n in g[1]:
            out[n] = _adamw([per_layer[l][n] for l in range(DEPTH)], w[n], m[n], v[n], "adamw_" + n)

    total = sum(width for _, width in SMALL)
    padded = -(-total // (8 * LANE)) * 8 * LANE
    pk = lambda a: jnp.pad(a, ((0, 0), (0, padded - total))).reshape(1, DEPTH * padded // LANE, LANE)
    sflat = jnp.stack([jnp.concatenate([smalls[l][n] for n, _ in SMALL]) for l in range(DEPTH)])
    sparts = _all_gather([pk(sflat)[0]], "gather_small_grads")[0]
    pw = lambda d: pk(jnp.concatenate([d[n] for n, _ in SMALL], axis=1))
    res = _adamw([sparts], pw(w), pw(m), pw(v), "adamw_small")
    off = 0
    for n, width in SMALL:
        out[n] = [a.reshape(DEPTH, padded)[:, off:off + width] for a in res]
        off += width

    return (loss, dx[None], *[out[n][0] for n in WEIGHTS], *[out[n][1] for n in WEIGHTS],
            *[out[n][2] for n in WEIGHTS], *[out[n][3] for n in WEIGHTS])
```

```python
import functools
import math

import jax
import jax.numpy as jnp
from jax import lax
from jax.experimental import pallas as pl
from jax.experimental.pallas import tpu as pltpu

F32 = jnp.float32
BF16 = jnp.bfloat16

D = 1024
DEPTH = 4
NDEV = 8
HEADS = 8
QL = 256
KVL = 128
ROPE = 32
NOPE = 64
SC = 256
SSD_DIM = 256
SSD_CONV = 768
SSD_H = 4
SSD_L = 128
FFN = 2816
FB = 704
EPS = 1e-6
ROPE_THETA = 10000.0
ATT_SCALE = 96 ** -0.5
LOG2E = 1.4426950408889634
LR, B1, B2, AEPS, WD, STEP = 0.001, 0.9, 0.999, 1e-08, 0.01, 10

PW = 2432
CATW = 1536

ROW_TILE = 512
ROW_CHUNK = 16
NORM_CHUNK = 32
NORM_TILE = 512
STREAM_BUFFERS = 3
MM_TILE = 1024
ATT_TILE = 512
ATT_QW = 1
BWD_TILE = 512
BWD_QW = 1
FWD_HEADS = 4
BWD_HEADS = 2
HALO = 16
LANE = 128
NEG = -1e30
NN = (((1,), (0,)), ((), ()))
NT = (((1,), (1,)), ((), ()))
TN = (((0,), (0,)), ((), ()))
VMEM_LIMIT = 56 * 1024 * 1024

SHARDED = (
    ("w_in", (4, 128, 2212)),
    ("mla_w_q_up", (4, 256, 96)),
    ("mla_w_kv_up", (4, 128, 128)),
    ("sc_conv_w", (4, 3, 32)),
    ("ssd_conv_w", (4, 4, 96)),
    ("w_out", (4, 128, 1024)),
    ("ffn_w_up", (4, 1024, 704)),
    ("ffn_conv_w", (4, 3, 704)),
    ("ffn_w_down", (4, 352, 1024)),
)
SHAPES = dict(SHARDED)
GROUPS = (
    (2212, ("w_in",)),
    (1024, ("w_out",)),
    (704, ("ffn_w_up",)),
    (96, ("mla_w_q_up", "ssd_conv_w", "sc_conv_w")),
    (128, ("mla_w_kv_up",)),
    (1024, ("ffn_w_down",)),
    (704, ("ffn_conv_w",)),
)
ATT_SIDE = (0, 3, 4)
FFN_SIDE = (1, 2, 5, 6)
SMALL = (
    ("norm_mix_pre", 1024), ("norm_mix_post", 1024), ("norm_ffn_pre", 1024), ("norm_ffn_post", 1024),
    ("mla_q_norm", 256), ("mla_kv_norm", 128), ("ssd_conv_b", 768), ("ssd_dt_bias", 4), ("ssd_a_log", 4),
    ("ssd_d", 4), ("ssd_norm", 256), ("ffn_conv_b", 5632),
)
WEIGHTS = ("norm_mix_pre", "norm_mix_post", "norm_ffn_pre", "norm_ffn_post", "w_in", "mla_q_norm", "mla_w_q_up",
           "mla_kv_norm", "mla_w_kv_up", "sc_conv_w", "ssd_conv_w", "ssd_conv_b", "ssd_dt_bias", "ssd_a_log", "ssd_d",
           "ssd_norm", "w_out", "ffn_w_up", "ffn_conv_w", "ffn_conv_b", "ffn_w_down")


def _dot(a, b, dims=NN, precision=None):
    return lax.dot_general(a, b, dims, precision=precision, preferred_element_type=F32)


def _dot01(x, c, dims=NN, x_first=True):
    cb = c.astype(BF16)
    hi = x.astype(BF16)
    rest = x - hi.astype(F32)
    mid = rest.astype(BF16)
    lo = (rest - mid.astype(F32)).astype(BF16)
    one = (lambda t: _dot(t, cb, dims)) if x_first else (lambda t: _dot(cb, t, dims))
    return one(hi) + one(mid) + one(lo)


def _sig(v):
    return 0.5 * jnp.tanh(0.5 * v) + 0.5


def _cp(*sem):
    return pltpu.CompilerParams(dimension_semantics=sem, vmem_limit_bytes=VMEM_LIMIT)


def _rowsum(v):
    return jnp.sum(v, axis=0, keepdims=True)


def _prev_halo(i, ts):
    return jnp.maximum(i * (ts // HALO) - 1, 0)


def _next_halo(i, ts, n):
    return jnp.minimum((i + 1) * (ts // HALO), n * (ts // HALO) - 1)


def _gather_plan(x_refs, out_refs, send_sems, recv_sems, local_sems):
    n = len(x_refs)
    x, y, cc = lax.axis_index("x"), lax.axis_index("y"), lax.axis_index("c")
    me, sibling = (x, y, cc), (x, y, 1 - cc)
    chips = [(1 - x, y), (x, 1 - y), (1 - x, 1 - y)]

    def rows(t, px, py, pc):
        return out_refs[t].at[4 * px + 2 * py + pc]

    def copy(t, k, block, to, own=False):
        return pltpu.make_async_remote_copy(
            src_ref=x_refs[t] if own else rows(t, *block), dst_ref=rows(t, *block),
            send_sem=send_sems.at[7 * t + k], recv_sem=recv_sems.at[7 * t + k], device_id=to, device_id_type=pl.DeviceIdType.MESH)

    def local(t):
        return pltpu.make_async_copy(x_refs[t], rows(t, *me), local_sems.at[t])

    def start():
        for t in range(n):
            local(t).start()
            copy(t, 0, me, sibling, own=True).start()
            for j, chip in enumerate(chips):
                copy(t, 1 + j, me, (*chip, cc), own=True).start()

    def finish():
        for j, chip in enumerate(chips):
            for t in range(n):
                copy(t, 1 + j, (*chip, cc), me).wait_recv()
                copy(t, 4 + j, (*chip, cc), sibling).start()
        for t in range(n):
            copy(t, 0, sibling, me).wait_recv()
            for j, chip in enumerate(chips):
                copy(t, 4 + j, (*chip, 1 - cc), me).wait_recv()
        for t in range(n):
            copy(t, 0, me, sibling, own=True).wait_send()
            for j, chip in enumerate(chips):
                copy(t, 1 + j, me, (*chip, cc), own=True).wait_send()
                copy(t, 4 + j, (*chip, cc), sibling).wait_send()
            local(t).wait()

    return start, finish


def _exchange_plan(x_refs, out_refs, send_sems, recv_sems, local_sems):
    n = len(x_refs)
    x, y, cc = lax.axis_index("x"), lax.axis_index("y"), lax.axis_index("c")
    me = 4 * x + 2 * y + cc

    def copies():
        res = [pltpu.make_async_copy(x_refs[t].at[me], out_refs[t].at[me], local_sems.at[t]) for t in range(n)]
        for k in range(1, NDEV):
            px = 1 - x if k & 4 else x
            py = 1 - y if k & 2 else y
            pc = 1 - cc if k & 1 else cc
            peer = 4 * px + 2 * py + pc
            for t in range(n):
                res.append(pltpu.make_async_remote_copy(
                    src_ref=x_refs[t].at[peer], dst_ref=out_refs[t].at[me], send_sem=send_sems.at[7 * t + k - 1],
                    recv_sem=recv_sems.at[7 * t + k - 1], device_id=(px, py, pc), device_id_type=pl.DeviceIdType.MESH))
        return res

    def start():
        for cp in copies():
            cp.start()

    def finish():
        for cp in copies():
            cp.wait()

    return start, finish


def _comm_scratch(n):
    return [pltpu.SemaphoreType.DMA((7 * n,)), pltpu.SemaphoreType.DMA((7 * n,)), pltpu.SemaphoreType.DMA((n,))]


ANY = pl.BlockSpec(memory_space=pl.ANY)


def _all_gather(xs, name):
    n = len(xs)

    def body(*refs):
        start, finish = _gather_plan(refs[:n], refs[n:2 * n], *refs[2 * n:])
        start()
        finish()

    return pl.pallas_call(
        body, name=name, out_shape=[jax.ShapeDtypeStruct((NDEV,) + a.shape, a.dtype) for a in xs],
        in_specs=[ANY] * n, out_specs=[ANY] * n, scratch_shapes=_comm_scratch(n),
    )(*xs)


def _mm(name, a, b, out_shape, grid, a_spec, b_spec, o_spec, dims, acc_shape):
    nk = grid[2]

    def single(a_ref, b_ref, o_ref):
        o_ref[...] = _dot(a_ref[...], b_ref[...], dims).astype(o_ref.dtype)

    if nk == 1:
        return pl.pallas_call(
            single, name=name, grid=grid, out_shape=out_shape, in_specs=[a_spec, b_spec], out_specs=o_spec,
            compiler_params=_cp("parallel", "parallel", "arbitrary"),
        )(a, b)

    def body(a_ref, b_ref, o_ref, acc_ref):
        k = pl.program_id(2)

        @pl.when(k == 0)
        def _():
            acc_ref[...] = jnp.zeros_like(acc_ref)

        acc_ref[...] += _dot(a_ref[...], b_ref[...], dims)

        @pl.when(k == nk - 1)
        def _():
            o_ref[...] = acc_ref[...].astype(o_ref.dtype)

    return pl.pallas_call(
        body, name=name, grid=grid, out_shape=out_shape, in_specs=[a_spec, b_spec], out_specs=o_spec,
        scratch_shapes=[pltpu.VMEM(acc_shape, F32)], compiler_params=_cp("parallel", "parallel", "arbitrary"),
    )(a, b)


def _mm_rows(name, a, w, out_dtype, dims):
    s, k = a.shape
    n = w.shape[1] if dims == NN else w.shape[0]
    tm = min(MM_TILE, s)
    return _mm(name, a, w, jax.ShapeDtypeStruct((s, n), out_dtype), (s // tm, 1, 1),
               pl.BlockSpec((tm, k), lambda i, j, kk: (i, 0)), pl.BlockSpec(w.shape, lambda i, j, kk: (0, 0)),
               pl.BlockSpec((tm, n), lambda i, j, kk: (i, 0)), dims, (tm, n))


def _mm_wgrad(name, a, g, out_dtype):
    s, m = a.shape
    n = g.shape[1]
    tk = min(MM_TILE, s)
    return _mm(name, a, g, jax.ShapeDtypeStruct((m, n), out_dtype), (1, 1, s // tk),
               pl.BlockSpec((tk, m), lambda i, j, kk: (kk, 0)), pl.BlockSpec((tk, n), lambda i, j, kk: (kk, 0)),
               pl.BlockSpec((m, n), lambda i, j, kk: (0, 0)), TN, (m, n))


def _mm_sum(name, parts, wparts, dims):
    s = parts[0].shape[0]
    tm = min(MM_TILE, s)
    n = len(parts)

    def body(*refs):
        acc = _dot(refs[0][...], refs[n][...], dims)
        for i in range(1, n):
            acc += _dot(refs[i][...], refs[n + i][...], dims)
        refs[2 * n][...] = acc.astype(BF16)

    return pl.pallas_call(
        body, name=name, grid=(s // tm,), out_shape=jax.ShapeDtypeStruct((s, D), BF16),
        in_specs=[pl.BlockSpec((tm, a.shape[1]), lambda i: (i, 0)) for a in parts] + [pl.BlockSpec(w.shape, lambda i: (0, 0)) for w in wparts],
        out_specs=pl.BlockSpec((tm, D), lambda i: (i, 0)), compiler_params=_cp("parallel"),
    )(*parts, *wparts)


def _mm_dwin(h, parts):
    s = h.shape[0]
    tk = min(MM_TILE, s)
    nk = s // tk
    n = len(parts)

    def body(*refs):
        h_ref, g_refs, o_refs, accs = refs[0], refs[1:1 + n], refs[1 + n:1 + 2 * n], refs[1 + 2 * n:]
        k = pl.program_id(0)

        @pl.when(k == 0)
        def _():
            for acc in accs:
                acc[...] = jnp.zeros_like(acc)

        hv = h_ref[...]
        for i in range(n):
            accs[i][...] += _dot(hv, g_refs[i][...], TN)

        @pl.when(k == nk - 1)
        def _():
            for i in range(n):
                o_refs[i][...] = accs[i][...].astype(BF16)

    widths = [a.shape[1] for a in parts]
    return pl.pallas_call(
        body, name="dw_in", grid=(nk,), out_shape=[jax.ShapeDtypeStruct((D, w), BF16) for w in widths],
        in_specs=[pl.BlockSpec((tk, D), lambda k: (k, 0))] + [pl.BlockSpec((tk, w), lambda k: (k, 0)) for w in widths],
        out_specs=[pl.BlockSpec((D, w), lambda k: (0, 0)) for w in widths],
        scratch_shapes=[pltpu.VMEM((D, w), F32) for w in widths], compiler_params=_cp("arbitrary"),
    )(h, *parts)


def _mm_dwout(parts, g):
    s = g.shape[0]
    tk = min(MM_TILE, s)
    nk = s // tk
    n = len(parts)

    def body(*refs):
        a_refs, g_ref, o_refs, accs = refs[:n], refs[n], refs[n + 1:2 * n + 1], refs[2 * n + 1:]
        k = pl.program_id(0)

        @pl.when(k == 0)
        def _():
            for acc in accs:
                acc[...] = jnp.zeros_like(acc)

        gv = g_ref[...]
        for i in range(n):
            accs[i][...] += _dot(a_refs[i][...], gv, TN)

        @pl.when(k == nk - 1)
        def _():
            for i in range(n):
                o_refs[i][...] = accs[i][...].astype(BF16)

    widths = [a.shape[1] for a in parts]
    return pl.pallas_call(
        body, name="dw_out", grid=(nk,), out_shape=[jax.ShapeDtypeStruct((w, D), BF16) for w in widths],
        in_specs=[pl.BlockSpec((tk, w), lambda k: (k, 0)) for w in widths] + [pl.BlockSpec((tk, D), lambda k: (k, 0))],
        out_specs=[pl.BlockSpec((w, D), lambda k: (0, 0)) for w in widths],
        scratch_shapes=[pltpu.VMEM((w, D), F32) for w in widths], compiler_params=_cp("arbitrary"),
    )(*parts, g)


def _mm_up(h2, wup):
    s = h2.shape[0]
    tm = min(MM_TILE, s)
    return _mm("ffn_up", h2, wup, jax.ShapeDtypeStruct((NDEV, s, FB), BF16), (NDEV, s // tm, 1),
               pl.BlockSpec((tm, D), lambda j, i, kk: (i, 0)), pl.BlockSpec((None, D, FB), lambda j, i, kk: (j, 0, 0)),
               pl.BlockSpec((None, tm, FB), lambda j, i, kk: (j, i, 0)), NN, (tm, FB))


def _mm_down(act, wdown):
    s = act.shape[1]
    tm = min(MM_TILE, s)
    return _mm("ffn_down", act, wdown, jax.ShapeDtypeStruct((s, D), BF16), (s // tm, 1, 4),
               pl.BlockSpec((None, tm, FB), lambda i, j, kk: (kk, i, 0)), pl.BlockSpec((None, FB, D), lambda i, j, kk: (kk, 0, 0)),
               pl.BlockSpec((tm, D), lambda i, j, kk: (i, 0)), NN, (tm, D))


def _mm_dact(df, wdown):
    s = df.shape[0]
    tm = min(MM_TILE, s)
    return _mm("ffn_dact", df, wdown, jax.ShapeDtypeStruct((4, s, FB), BF16), (4, s // tm, 1),
               pl.BlockSpec((tm, D), lambda j, i, kk: (i, 0)), pl.BlockSpec((None, FB, D), lambda j, i, kk: (j, 0, 0)),
               pl.BlockSpec((None, tm, FB), lambda j, i, kk: (j, i, 0)), NT, (tm, FB))


def _mm_dwdown(act, df):
    s = df.shape[0]
    tk = min(MM_TILE, s)
    return _mm("ffn_dwdown", act, df, jax.ShapeDtypeStruct((4, FB, D), BF16), (4, 1, s // tk),
               pl.BlockSpec((None, tk, FB), lambda j, i, kk: (j, kk, 0)), pl.BlockSpec((tk, D), lambda j, i, kk: (kk, 0)),
               pl.BlockSpec((None, FB, D), lambda j, i, kk: (j, 0, 0)), TN, (FB, D))


def _mm_dh2(dupre, wup):
    s = dupre.shape[1]
    tm = min(MM_TILE, s)
    return _mm("ffn_dh2", dupre, wup, jax.ShapeDtypeStruct((s, D), BF16), (s // tm, 1, NDEV),
               pl.BlockSpec((None, tm, FB), lambda i, j, kk: (kk, i, 0)), pl.BlockSpec((None, D, FB), lambda i, j, kk: (kk, 0, 0)),
               pl.BlockSpec((tm, D), lambda i, j, kk: (i, 0)), NT, (tm, D))


def _mm_dwup(h2, dupre):
    s = h2.shape[0]
    tk = min(MM_TILE, s)
    return _mm("ffn_dwup", h2, dupre, jax.ShapeDtypeStruct((NDEV, D, FB), BF16), (NDEV, 1, s // tk),
               pl.BlockSpec((tk, D), lambda j, i, kk: (kk, 0)), pl.BlockSpec((None, tk, FB), lambda j, i, kk: (j, kk, 0)),
               pl.BlockSpec((None, D, FB), lambda j, i, kk: (j, 0, 0)), TN, (D, FB))


def _rms(xv, w, out_dtype, name):
    s, d = xv.shape
    ts = min(ROW_TILE, s)

    def body(x_ref, w_ref, o_ref):
        for r0 in range(0, ts, NORM_CHUNK):
            rows = pl.ds(r0, NORM_CHUNK)
            xf = x_ref[rows, :].astype(F32)
            r = lax.rsqrt(jnp.mean(xf * xf, axis=-1, keepdims=True) + EPS)
            o_ref[rows, :] = (xf * r * w_ref[...]).astype(o_ref.dtype)

    return pl.pallas_call(
        body, name=name, grid=(s // ts,), out_shape=jax.ShapeDtypeStruct((s, d), out_dtype),
        in_specs=[pl.BlockSpec((ts, d), lambda i: (i, 0)), pl.BlockSpec((1, d), lambda i: (0, 0))],
        out_specs=pl.BlockSpec((ts, d), lambda i: (i, 0)), compiler_params=_cp("parallel"),
    )(xv, w)


def _add_rms(xv, mv, w, name, w_next=None):
    s, d = xv.shape
    ts = min(NORM_TILE, s)
    both = w_next is not None

    def body(*refs):
        x_ref, m_ref, w_ref = refs[:3]
        o_ref = refs[4] if both else refs[3]
        for r0 in range(0, ts, NORM_CHUNK):
            rows = pl.ds(r0, NORM_CHUNK)
            mf = m_ref[rows, :].astype(F32)
            r = lax.rsqrt(jnp.mean(mf * mf, axis=-1, keepdims=True) + EPS)
            y = x_ref[rows, :] + mf * r * w_ref[...]
            o_ref[rows, :] = y
            if both:
                r2 = lax.rsqrt(jnp.mean(y * y, axis=-1, keepdims=True) + EPS)
                refs[5][rows, :] = (y * r2 * refs[3][...]).astype(BF16)

    row = pl.BlockSpec((ts, d), lambda i: (i, 0))
    vec = pl.BlockSpec((1, d), lambda i: (0, 0))
    if both:
        return _add_rms_streamed(xv, mv, w, w_next, name + "_rms")
    return pl.pallas_call(
        body, name=name, grid=(s // ts,), out_shape=jax.ShapeDtypeStruct((s, d), F32),
        in_specs=[row, row, vec], out_specs=row, compiler_params=_cp("parallel"),
    )(xv, mv, w)


def _add_rms_streamed(xv, mv, w, w_next, name):
    s, d = xv.shape
    ts = min(NORM_TILE, s)
    n = s // ts
    nb = STREAM_BUFFERS

    def body(x_hbm, m_hbm, w_ref, w2_ref, o_hbm, h_hbm, xb, mb, ob, hb, sem_in, sem_out):
        def reads(i, slot):
            rows = pl.ds(pl.multiple_of(i * ts, ts), ts)
            return (pltpu.make_async_copy(x_hbm.at[rows], xb.at[slot], sem_in.at[0, slot]),
                    pltpu.make_async_copy(m_hbm.at[rows], mb.at[slot], sem_in.at[1, slot]))

        def writes(i, slot):
            rows = pl.ds(pl.multiple_of(i * ts, ts), ts)
            return (pltpu.make_async_copy(ob.at[slot], o_hbm.at[rows], sem_out.at[0, slot]),
                    pltpu.make_async_copy(hb.at[slot], h_hbm.at[rows], sem_out.at[1, slot]))

        for i in range(min(nb - 1, n)):
            for cp in reads(i, i):
                cp.start()

        def step(i, carry):
            slot = lax.rem(i, nb)
            for cp in reads(i, slot):
                cp.wait()

            @pl.when(i + nb - 1 < n)
            def _():
                for cp in reads(i + nb - 1, lax.rem(i + nb - 1, nb)):
                    cp.start()

            @pl.when(i >= nb)
            def _():
                for cp in writes(i - nb, slot):
                    cp.wait()

            for r0 in range(0, ts, NORM_CHUNK):
                rows = pl.ds(r0, NORM_CHUNK)
                mf = mb[slot, rows, :].astype(F32)
                r = lax.rsqrt(jnp.mean(mf * mf, axis=-1, keepdims=True) + EPS)
                y = xb[slot, rows, :] + mf * r * w_ref[...]
                ob[slot, rows, :] = y
                r2 = lax.rsqrt(jnp.mean(y * y, axis=-1, keepdims=True) + EPS)
                hb[slot, rows, :] = (y * r2 * w2_ref[...]).astype(BF16)
            for cp in writes(i, slot):
                cp.start()
            return carry

        lax.fori_loop(0, n, step, 0)
        for i in range(max(n - nb, 0), n):
            for cp in writes(i, i % nb):
                cp.wait()

    vmem = pl.BlockSpec(memory_space=pltpu.VMEM)
    return pl.pallas_call(
        body, name=name, out_shape=(jax.ShapeDtypeStruct((s, d), F32), jax.ShapeDtypeStruct((s, d), BF16)),
        in_specs=[ANY, ANY, vmem, vmem], out_specs=(ANY, ANY),
        scratch_shapes=[pltpu.VMEM((nb, ts, d), F32), pltpu.VMEM((nb, ts, d), BF16), pltpu.VMEM((nb, ts, d), F32),
                        pltpu.VMEM((nb, ts, d), BF16), pltpu.SemaphoreType.DMA((2, nb)), pltpu.SemaphoreType.DMA((2, nb))],
        compiler_params=pltpu.CompilerParams(vmem_limit_bytes=VMEM_LIMIT),
    )(xv, mv, w, w_next)


def _rms_bwd_math(xf, w, dy):
    r = lax.rsqrt(jnp.mean(xf * xf, axis=-1, keepdims=True) + EPS)
    xh = xf * r
    dxh = dy * w
    dx = r * (dxh - xh * jnp.mean(dxh * xh, axis=-1, keepdims=True))
    return dx, dy * xh


def _rms_bwd(xv, w, dy, dres, out_dtype, name, pending=None):
    s, d = xv.shape
    ts = min(NORM_TILE, s)
    with_res = dres is not None
    nin = 4 if with_res else 3
    nx = len(pending) if pending else 0

    def body(*refs):
        x_ref, w_ref, dy_ref = refs[:3]
        dres_ref = refs[3] if with_res else None
        dx_ref, dw_ref = refs[nin + nx:nin + nx + 2]
        acc = refs[nin + 2 * nx + 2]
        if nx:
            start, finish = _exchange_plan(refs[nin:nin + nx], refs[nin + nx + 2:nin + 2 * nx + 2], *refs[nin + 2 * nx + 3:])
            pl.when(pl.program_id(0) == 0)(start)
        acc[...] = jnp.zeros_like(acc)
        for r0 in range(0, ts, NORM_CHUNK):
            rows = pl.ds(r0, NORM_CHUNK)
            dx, dwt = _rms_bwd_math(x_ref[rows, :].astype(F32), w_ref[...], dy_ref[rows, :].astype(F32))
            if with_res:
                dx = dx + dres_ref[rows, :]
            dx_ref[rows, :] = dx.astype(dx_ref.dtype)
            acc[...] += dwt

        @pl.when(pl.program_id(0) == 0)
        def _():
            dw_ref[...] = jnp.zeros_like(dw_ref)

        dw_ref[...] += _rowsum(acc[...])
        if nx:
            pl.when(pl.program_id(0) == s // ts - 1)(finish)

    row = pl.BlockSpec((ts, d), lambda i: (i, 0))
    vec = pl.BlockSpec((1, d), lambda i: (0, 0))
    ins = [xv, w, dy] + ([dres] if with_res else []) + (pending or [])
    res = pl.pallas_call(
        body, name=name + "_exchange" if nx else name, grid=(s // ts,),
        out_shape=[jax.ShapeDtypeStruct((s, d), out_dtype), jax.ShapeDtypeStruct((1, d), F32)]
        + [jax.ShapeDtypeStruct(a.shape, a.dtype) for a in (pending or [])],
        in_specs=[row, vec, row] + ([row] if with_res else []) + [ANY] * nx, out_specs=[row, vec] + [ANY] * nx,
        scratch_shapes=[pltpu.VMEM((NORM_CHUNK, d), F32)] + (_comm_scratch(nx) if nx else []), compiler_params=_cp("arbitrary"),
    )(*ins)
    return (res[0], res[1], list(res[2:])) if nx else (res[0], res[1])


def _rms_bwd2(xa, wa, dya, dres, xb, wb):
    s, d = xa.shape
    ts = min(NORM_TILE, s)

    def body(xa_ref, wa_ref, dya_ref, dres_ref, xb_ref, wb_ref, da_ref, db_ref, dwa_ref, dwb_ref, acc):
        acc[...] = jnp.zeros_like(acc)
        for r0 in range(0, ts, NORM_CHUNK):
            rows = pl.ds(r0, NORM_CHUNK)
            da, dwt = _rms_bwd_math(xa_ref[rows, :].astype(F32), wa_ref[...], dya_ref[rows, :].astype(F32))
            da = da + dres_ref[rows, :]
            da_ref[rows, :] = da
            acc[0] += dwt
            db, dwt = _rms_bwd_math(xb_ref[rows, :].astype(F32), wb_ref[...], da)
            db_ref[rows, :] = db.astype(BF16)
            acc[1] += dwt

        @pl.when(pl.program_id(0) == 0)
        def _():
            dwa_ref[...] = jnp.zeros_like(dwa_ref)
            dwb_ref[...] = jnp.zeros_like(dwb_ref)

        dwa_ref[...] += _rowsum(acc[0])
        dwb_ref[...] += _rowsum(acc[1])

    row = pl.BlockSpec((ts, d), lambda i: (i, 0))
    vec = pl.BlockSpec((1, d), lambda i: (0, 0))
    return pl.pallas_call(
        body, name="rms_bwd2", grid=(s // ts,),
        out_shape=(jax.ShapeDtypeStruct((s, d), F32), jax.ShapeDtypeStruct((s, d), BF16), jax.ShapeDtypeStruct((1, d), F32),
                   jax.ShapeDtypeStruct((1, d), F32)),
        in_specs=[row, vec, row, row, row, vec], out_specs=(row, row, vec, vec),
        scratch_shapes=[pltpu.VMEM((2, NORM_CHUNK, d), F32)], compiler_params=_cp("arbitrary"),
    )(xa, wa, dya, dres, xb, wb)


def _loss_head(yv, tv):
    s, d = yv.shape
    ts = min(ROW_TILE, s)

    def body(y_ref, t_ref, l_ref, dy_ref):
        e = y_ref[...] - t_ref[...]
        dy_ref[...] = e * (1.0 / d)

        @pl.when(pl.program_id(0) == 0)
        def _():
            l_ref[...] = jnp.zeros_like(l_ref)

        tot = jnp.sum(jnp.sum(e * e, axis=1, keepdims=True), axis=0, keepdims=True)
        l_ref[...] += jnp.broadcast_to(tot * (0.5 / d), (8, LANE))

    row = pl.BlockSpec((ts, d), lambda i: (i, 0))
    return pl.pallas_call(
        body, name="loss_head", grid=(s // ts,),
        out_shape=(jax.ShapeDtypeStruct((8, LANE), F32), jax.ShapeDtypeStruct((s, d), F32)),
        in_specs=[row, row], out_specs=(pl.BlockSpec((8, LANE), lambda i: (0, 0)), row), compiler_params=_cp("arbitrary"),
    )(yv, tv)


def _rope(v, c, a, b):
    return v * c + pltpu.roll(v, LANE - 16, 1) * a + pltpu.roll(v, 16, 1) * b


def _rope_t(dv, c, a, b):
    return dv * c + pltpu.roll(dv * a, 16, 1) + pltpu.roll(dv * b, LANE - 16, 1)


def _mla_prep(proj, tabs, qnw, kvnw, wq, wkv):
    s = proj.shape[0]
    ts = min(ROW_TILE, s)
    tc, ta, tb = tabs

    def body(cq_ref, ckv_ref, kr_ref, c_ref, a_ref, b_ref, qnw_ref, kvnw_ref, wq_ref, wkv_ref, q_ref, k_ref, kv_ref):
        c, a, b = c_ref[...], a_ref[...], b_ref[...]
        cq = cq_ref[...].astype(F32)
        qn = (cq * lax.rsqrt(jnp.mean(cq * cq, axis=-1, keepdims=True) + EPS) * qnw_ref[...]).astype(BF16)
        ckv = ckv_ref[...].astype(F32)
        kvn = (ckv * lax.rsqrt(jnp.mean(ckv * ckv, axis=-1, keepdims=True) + EPS) * kvnw_ref[...]).astype(BF16)
        kr = _rope(kr_ref[...].astype(F32), c, a, b)
        lane = lax.broadcasted_iota(jnp.int32, (ts, LANE), 1)
        for h in range(HEADS):
            q_ref[h] = _rope(_dot(qn, wq_ref[h]), c, a, b).astype(BF16)
            kv = _dot(kvn, wkv_ref[h])
            kv_ref[h] = kv.astype(BF16)
            k_ref[h] = jnp.where(lane < NOPE, kv, kr).astype(BF16)

    tab = pl.BlockSpec((ts, LANE), lambda i: (i, 0))
    hd = pl.BlockSpec((HEADS, ts, LANE), lambda i: (0, i, 0))
    out = jax.ShapeDtypeStruct((HEADS, s, LANE), BF16)
    return pl.pallas_call(
        body, name="mla_prep", grid=(s // ts,), out_shape=(out, out, out),
        in_specs=[pl.BlockSpec((ts, QL), lambda i: (i, 0)), pl.BlockSpec((ts, LANE), lambda i: (i, 2)),
                  pl.BlockSpec((ts, LANE), lambda i: (i, 3)), tab, tab, tab,
                  pl.BlockSpec((1, QL), lambda i: (0, 0)), pl.BlockSpec((1, KVL), lambda i: (0, 0)),
                  pl.BlockSpec((HEADS, QL, LANE), lambda i: (0, 0, 0)), pl.BlockSpec((HEADS, KVL, LANE), lambda i: (0, 0, 0))],
        out_specs=(hd, hd, hd), compiler_params=_cp("parallel"),
    )(proj, proj, proj, tc, ta, tb, qnw, kvnw, wq, wkv)


def _mla_prep_bwd(proj, tabs, qnw, kvnw, wq, wkv, dq, dk, dv):
    s = proj.shape[0]
    ts = min(ROW_TILE, s)
    tc, ta, tb = tabs

    def body(cq_ref, ckv_ref, c_ref, a_ref, b_ref, qnw_ref, kvnw_ref, wq_ref, wkv_ref, dq_ref, dk_ref, dv_ref,
             d_ref, dwq_ref, dwkv_ref, dqnw_ref, dkvnw_ref):
        dcq_ref, dckv_ref, dkr_ref = d_ref.at[:, 0:QL], d_ref.at[:, QL:QL + KVL], d_ref.at[:, QL + KVL:QL + KVL + LANE]

        @pl.when(pl.program_id(0) == 0)
        def _():
            dwq_ref[...] = jnp.zeros_like(dwq_ref)
            dwkv_ref[...] = jnp.zeros_like(dwkv_ref)
            dqnw_ref[...] = jnp.zeros_like(dqnw_ref)
            dkvnw_ref[...] = jnp.zeros_like(dkvnw_ref)

        c, a, b = c_ref[...], a_ref[...], b_ref[...]
        cq = cq_ref[...].astype(F32)
        qn = (cq * lax.rsqrt(jnp.mean(cq * cq, axis=-1, keepdims=True) + EPS) * qnw_ref[...]).astype(BF16)
        ckv = ckv_ref[...].astype(F32)
        kvn = (ckv * lax.rsqrt(jnp.mean(ckv * ckv, axis=-1, keepdims=True) + EPS) * kvnw_ref[...]).astype(BF16)
        lane = lax.broadcasted_iota(jnp.int32, (ts, LANE), 1)
        dqn = jnp.zeros((ts, QL), F32)
        dkvn = jnp.zeros((ts, KVL), F32)
        dkr = jnp.zeros((ts, LANE), F32)
        for h in range(HEADS):
            dqh = _rope_t(dq_ref[h], c, a, b).astype(BF16)
            dwq_ref[h] += _dot(qn, dqh, TN)
            dqn += _dot(dqh, wq_ref[h], NT)
            dkh = dk_ref[h].astype(F32)
            dkvh = jnp.where(lane < NOPE, dkh, dv_ref[h].astype(F32)).astype(BF16)
            dkr += jnp.where(lane < NOPE, 0.0, dkh)
            dwkv_ref[h] += _dot(kvn, dkvh, TN)
            dkvn += _dot(dkvh, wkv_ref[h], NT)
        dkr_ref[...] = _rope_t(dkr, c, a, b).astype(BF16)
        dcq, dwt = _rms_bwd_math(cq, qnw_ref[...], dqn)
        dcq_ref[...] = dcq.astype(BF16)
        dqnw_ref[...] += _rowsum(dwt)
        dckv, dwt = _rms_bwd_math(ckv, kvnw_ref[...], dkvn)
        dckv_ref[...] = dckv.astype(BF16)
        dkvnw_ref[...] += _rowsum(dwt)

    tab = pl.BlockSpec((ts, LANE), lambda i: (i, 0))
    hd = pl.BlockSpec((HEADS, ts, LANE), lambda i: (0, i, 0))
    wq_spec = pl.BlockSpec((HEADS, QL, LANE), lambda i: (0, 0, 0))
    wkv_spec = pl.BlockSpec((HEADS, KVL, LANE), lambda i: (0, 0, 0))
    return pl.pallas_call(
        body, name="mla_prep_bwd", grid=(s // ts,),
        out_shape=(jax.ShapeDtypeStruct((s, QL + KVL + LANE), BF16),
                   jax.ShapeDtypeStruct((HEADS, QL, LANE), F32), jax.ShapeDtypeStruct((HEADS, KVL, LANE), F32),
                   jax.ShapeDtypeStruct((1, QL), F32), jax.ShapeDtypeStruct((1, KVL), F32)),
        in_specs=[pl.BlockSpec((ts, QL), lambda i: (i, 0)), pl.BlockSpec((ts, LANE), lambda i: (i, 2)), tab, tab, tab,
                  pl.BlockSpec((1, QL), lambda i: (0, 0)), pl.BlockSpec((1, KVL), lambda i: (0, 0)), wq_spec, wkv_spec, hd, hd, hd],
        out_specs=(pl.BlockSpec((ts, QL + KVL + LANE), lambda i: (i, 0)), wq_spec, wkv_spec,
                   pl.BlockSpec((1, QL), lambda i: (0, 0)), pl.BlockSpec((1, KVL), lambda i: (0, 0))),
        compiler_params=_cp("arbitrary"),
    )(proj, proj, tc, ta, tb, qnw, kvnw, wq, wkv, dq, dk, dv)


def _transpose_bf16(v):
    return v.astype(F32).T.astype(BF16)


def _flash_fwd(q, k, kv, prefetch=None):
    s = q.shape[1]
    t = min(ATT_TILE, s)
    tq = ATT_QW * t
    n = s // tq
    g = FWD_HEADS
    nx = len(prefetch) if prefetch else 0

    def body(*refs):
        q_ref, k_ref, kv_ref = refs[:3]
        o_ref, lse_ref = refs[3 + nx:5 + nx]
        kvt_sc = refs[5 + 2 * nx]
        step = pl.program_id(0) * n + pl.program_id(1)
        if nx:
            start, finish = _gather_plan(refs[3:3 + nx], refs[5 + nx:5 + 2 * nx], *refs[6 + 2 * nx:])
            pl.when(step == 0)(start)
        attend(q_ref, k_ref, kv_ref, o_ref, lse_ref, kvt_sc)
        if nx:
            pl.when(step == (HEADS // g) * n - 1)(finish)

    def attend(q_ref, k_ref, kv_ref, o_ref, lse_ref, kvt_sc):
        i = pl.program_id(1)

        @pl.when(i == 0)
        def _():
            ones_rows = lax.broadcasted_iota(jnp.int32, (LANE, s), 0) < NOPE
            for hh in range(g):
                kvt_sc[hh] = jnp.where(ones_rows, 1.0, kv_ref[hh].astype(F32).T).astype(BF16)

        qt = [(q_ref[hh].astype(F32) * (ATT_SCALE * LOG2E)).T.astype(BF16) for hh in range(g)]
        kpos = lax.broadcasted_iota(jnp.int32, (t, tq), 0)
        qpos = lax.broadcasted_iota(jnp.int32, (t, tq), 1) + i * tq

        def chunk(j, carry, diagonal):
            start = pl.multiple_of(j * t, t)
            scs = [_dot(k_ref[hh, pl.ds(start, t), :], qt[hh]) for hh in range(g)]
            soft = []
            for hh in range(g):
                sc = scs[hh]
                if diagonal:
                    sc = jnp.where(qpos >= kpos + start, sc, NEG)
                m_new = jnp.maximum(carry[hh][0], jnp.max(sc, axis=0, keepdims=True))
                soft.append((m_new, jnp.exp2(carry[hh][0] - m_new), jnp.exp2(sc - m_new).astype(BF16)))
            pvs = [_dot(kvt_sc[hh, :, pl.ds(start, t)], soft[hh][2]) for hh in range(g)]
            return tuple((soft[hh][0], soft[hh][1] * carry[hh][1] + pvs[hh]) for hh in range(g))

        init = tuple((jnp.full((1, tq), NEG, F32), jnp.zeros((LANE, tq), F32)) for _ in range(g))
        carry = lax.fori_loop(0, ATT_QW * i, lambda j, c: chunk(j, c, False), init)
        for d in range(ATT_QW):
            carry = chunk(ATT_QW * i + d, carry, True)
        for hh in range(g):
            m, acc = carry[hh]
            l = acc[0:1, :]
            o_ref[:, hh * LANE:(hh + 1) * LANE] = (acc / l).T.astype(BF16)
            lse_ref[hh] = m + jnp.log2(l)

    whole = pl.BlockSpec((g, s, LANE), lambda h, i: (h, 0, 0))
    res = pl.pallas_call(
        body, name="flash_fwd_gather" if nx else "flash_fwd", grid=(HEADS // g, n),
        out_shape=[jax.ShapeDtypeStruct((s, HEADS * LANE), BF16), jax.ShapeDtypeStruct((HEADS, 1, s), F32)]
        + [jax.ShapeDtypeStruct((NDEV,) + a.shape, a.dtype) for a in (prefetch or [])],
        in_specs=[pl.BlockSpec((g, tq, LANE), lambda h, i: (h, i, 0)), whole, whole] + [ANY] * nx,
        out_specs=[pl.BlockSpec((tq, g * LANE), lambda h, i: (i, h)), pl.BlockSpec((g, 1, tq), lambda h, i: (h, 0, i))] + [ANY] * nx,
        scratch_shapes=[pltpu.VMEM((g, LANE, s), BF16)] + (_comm_scratch(nx) if nx else []),
        compiler_params=_cp("arbitrary", "arbitrary"),
    )(q, k, kv, *(prefetch or []))
    return res[0], res[1], list(res[2:])


def _flash_bwd(q, k, kv, cat, dcat, lse, pending=None):
    s = q.shape[1]
    t = min(BWD_TILE, s)
    tq = BWD_QW * t
    n = s // t
    g = BWD_HEADS
    nx = len(pending) if pending else 0

    def body(*refs):
        ins, outs, scr = refs[:6], refs[6 + nx:9 + nx], refs[9 + 2 * nx:14 + 2 * nx]
        step = pl.program_id(0) * n + pl.program_id(1)
        if nx:
            start, finish = _exchange_plan(refs[6:6 + nx], refs[9 + nx:9 + 2 * nx], *refs[14 + 2 * nx:])
            pl.when(step == 0)(start)
        attend(*ins, *outs, *scr)
        if nx:
            pl.when(step == (HEADS // g) * n - 1)(finish)

    def attend(q_ref, k_ref, kv_ref, o_ref, do_ref, lse_ref, dq_ref, dk_ref, dv_ref, qt_sc, dot_sc, delta_sc, dqt_sc, qs_sc):
        j = pl.program_id(1)

        @pl.when(j == 0)
        def _():
            for hh in range(g):
                lanes = slice(hh * LANE, (hh + 1) * LANE)
                qf = q_ref[hh].astype(F32)
                qt_sc[hh] = (qf * (ATT_SCALE * LOG2E)).T.astype(BF16)
                qs_sc[hh] = (qf * ATT_SCALE).astype(BF16)
                dof = do_ref[:, lanes].astype(F32)
                dot_sc[hh] = dof.T.astype(BF16)
                delta_sc[hh] = _dot01(dof * o_ref[:, lanes].astype(F32), jnp.ones((8, LANE), F32), NT, x_first=False)
            dqt_sc[...] = jnp.zeros_like(dqt_sc)

        kjt = [_transpose_bf16(k_ref[hh]) for hh in range(g)]
        kpos = lax.broadcasted_iota(jnp.int32, (t, tq), 0) + j * t
        qpos = lax.broadcasted_iota(jnp.int32, (t, tq), 1)

        def chunk(i, carry, diagonal):
            start = pl.multiple_of(i * tq, tq)
            cols = pl.ds(start, tq)
            scs = [_dot(k_ref[hh], qt_sc[hh, :, cols]) for hh in range(g)]
            dps = [_dot(kv_ref[hh], dot_sc[hh, :, cols]) for hh in range(g)]
            pds = []
            for hh in range(g):
                p = jnp.exp2(scs[hh] - lse_ref[hh, :, cols])
                if diagonal:
                    p = jnp.where(qpos + start >= kpos, p, 0.0)
                ds = (p * (dps[hh] - delta_sc[hh, 0:1, cols])).astype(BF16)
                pds.append((p.astype(BF16), ds))
            out = []
            for hh in range(g):
                dk, dv = carry[hh]
                dv = dv + _dot(pds[hh][0], do_ref[pl.ds(start, tq), hh * LANE:(hh + 1) * LANE])
                dk = dk + _dot(pds[hh][1], qs_sc[hh, pl.ds(start, tq), :])
                dqt_sc[hh, :, cols] += _dot(kjt[hh], pds[hh][1])
                out.append((dk, dv))
            return tuple(out)

        zero = jnp.zeros((t, LANE), F32)
        first = lax.div(j, BWD_QW)
        carry = chunk(first, tuple((zero, zero) for _ in range(g)), True)
        carry = lax.fori_loop(first + 1, s // tq, lambda i, c: chunk(i, c, False), carry)
        for hh in range(g):
            dk_ref[hh] = carry[hh][0].astype(BF16)
            dv_ref[hh] = carry[hh][1].astype(BF16)

        @pl.when(j == n - 1)
        def _():
            for hh in range(g):
                dq_ref[hh] = (dqt_sc[hh] * ATT_SCALE).T

    whole = pl.BlockSpec((g, s, LANE), lambda h, j: (h, 0, 0))
    kspec = pl.BlockSpec((g, t, LANE), lambda h, j: (h, j, 0))
    ospec = pl.BlockSpec((s, g * LANE), lambda h, j: (0, h))
    res = pl.pallas_call(
        body, name="flash_bwd_exchange" if nx else "flash_bwd", grid=(HEADS // g, n),
        out_shape=[jax.ShapeDtypeStruct((HEADS, s, LANE), F32), jax.ShapeDtypeStruct((HEADS, s, LANE), BF16),
                   jax.ShapeDtypeStruct((HEADS, s, LANE), BF16)] + [jax.ShapeDtypeStruct(a.shape, a.dtype) for a in (pending or [])],
        in_specs=[whole, kspec, kspec, ospec, ospec, pl.BlockSpec((g, 1, s), lambda h, j: (h, 0, 0))] + [ANY] * nx,
        out_specs=[whole, kspec, kspec] + [ANY] * nx,
        scratch_shapes=[pltpu.VMEM((g, LANE, s), BF16), pltpu.VMEM((g, LANE, s), BF16), pltpu.VMEM((g, 8, s), F32),
                        pltpu.VMEM((g, LANE, s), F32), pltpu.VMEM((g, s, LANE), BF16)] + (_comm_scratch(nx) if nx else []),
        compiler_params=_cp("arbitrary", "arbitrary"),
    )(q, k, kv, cat, dcat, lse, *(pending or []))
    return res[0], res[1], res[2], list(res[3:])


def _conv3(ext, w_ref, ts):
    return (w_ref[0:1, :] * ext[pl.ds(HALO - 2, ts), :] + w_ref[1:2, :] * ext[pl.ds(HALO - 1, ts), :]
            + w_ref[2:3, :] * ext[pl.ds(HALO, ts), :])


def _conv3_rows(ext, w_ref, r):
    return (w_ref[0:1, :] * ext[pl.ds(HALO - 2 + r, ROW_CHUNK), :] + w_ref[1:2, :] * ext[pl.ds(HALO - 1 + r, ROW_CHUNK), :]
            + w_ref[2:3, :] * ext[pl.ds(HALO + r, ROW_CHUNK), :])


def _conv3_t(ext2, w_ref, ts):
    return (w_ref[0:1, :] * ext2[pl.ds(2, ts), :] + w_ref[1:2, :] * ext2[pl.ds(1, ts), :] + w_ref[2:3, :] * ext2[pl.ds(0, ts), :])


def _sconv_fwd(proj, w):
    s = proj.shape[0]
    ts = min(ROW_TILE, s)

    def body(b_ref, c_ref, h_ref, hc_ref, hh_ref, w_ref, o_ref, ext):
        i = pl.program_id(0)
        ext[0:HALO, :] = hc_ref[...].astype(F32) * hh_ref[...].astype(F32) * (i > 0).astype(F32)
        ext[HALO:HALO + ts, :] = c_ref[...].astype(F32) * h_ref[...].astype(F32)
        o_ref[...] = (b_ref[...].astype(F32) * _conv3(ext, w_ref, ts)).astype(BF16)

    def col(cb):
        return pl.BlockSpec((ts, SC), lambda i: (i, cb))

    def halo(cb):
        return pl.BlockSpec((HALO, SC), lambda i: (_prev_halo(i, ts), cb))

    return pl.pallas_call(
        body, name="sconv_fwd", grid=(s // ts,), out_shape=jax.ShapeDtypeStruct((s, SC), BF16),
        in_specs=[col(2), col(3), col(4), halo(3), halo(4), pl.BlockSpec((3, SC), lambda i: (0, 0))],
        out_specs=pl.BlockSpec((ts, SC), lambda i: (i, 0)), scratch_shapes=[pltpu.VMEM((ts + HALO, SC), F32)],
        compiler_params=_cp("parallel"),
    )(proj, proj, proj, proj, proj, w)


def _sconv_bwd(proj, dcat, w):
    s = proj.shape[0]
    ts = min(ROW_TILE, s)
    n = s // ts

    def body(b_ref, c_ref, h_ref, hc_ref, hh_ref, dy_ref, ndy_ref, nb_ref, w_ref, d_ref, dw_ref, ext, ext2):
        i = pl.program_id(0)
        db_ref, dc_ref, dh_ref = d_ref.at[:, 0:SC], d_ref.at[:, SC:2 * SC], d_ref.at[:, 2 * SC:3 * SC]

        @pl.when(i == 0)
        def _():
            dw_ref[...] = jnp.zeros_like(dw_ref)

        cv, hv, bv = c_ref[...].astype(F32), h_ref[...].astype(F32), b_ref[...].astype(F32)
        ext[0:HALO, :] = hc_ref[...].astype(F32) * hh_ref[...].astype(F32) * (i > 0).astype(F32)
        ext[HALO:HALO + ts, :] = cv * hv
        dy = dy_ref[...].astype(F32)
        db_ref[...] = (dy * _conv3(ext, w_ref, ts)).astype(BF16)
        dyb = dy * bv
        ext2[0:ts, :] = dyb
        ext2[ts:ts + HALO, :] = ndy_ref[...].astype(F32) * nb_ref[...].astype(F32) * (i < n - 1).astype(F32)
        dg = _conv3_t(ext2, w_ref, ts)
        dc_ref[...] = (dg * hv).astype(BF16)
        dh_ref[...] = (dg * cv).astype(BF16)
        for kk in range(3):
            dw_ref[kk:kk + 1, :] += _rowsum(dyb * ext[pl.ds(HALO - 2 + kk, ts), :])

    def col(cb):
        return pl.BlockSpec((ts, SC), lambda i: (i, cb))

    def halo(cb):
        return pl.BlockSpec((HALO, SC), lambda i: (_prev_halo(i, ts), cb))

    def nxt(cb):
        return pl.BlockSpec((HALO, SC), lambda i: (_next_halo(i, ts, n), cb))

    out = jax.ShapeDtypeStruct((s, 3 * SC), BF16)
    o0 = pl.BlockSpec((ts, 3 * SC), lambda i: (i, 0))
    return pl.pallas_call(
        body, name="sconv_bwd", grid=(n,), out_shape=(out, jax.ShapeDtypeStruct((3, SC), F32)),
        in_specs=[col(2), col(3), col(4), halo(3), halo(4), col(4), nxt(4), nxt(2), pl.BlockSpec((3, SC), lambda i: (0, 0))],
        out_specs=(o0, pl.BlockSpec((3, SC), lambda i: (0, 0))),
        scratch_shapes=[pltpu.VMEM((ts + HALO, SC), F32), pltpu.VMEM((ts + HALO, SC), F32)], compiler_params=_cp("arbitrary"),
    )(proj, proj, proj, proj, proj, dcat, dcat, proj, w)


def _ffn_stage(ext, u_ref, halo_ref, i, ts):
    ext[0:HALO, :] = halo_ref[...].astype(F32) * (i > 0).astype(F32)
    ext[HALO:HALO + ts, :] = u_ref[...].astype(F32)


def _ffn_specs(ts):
    cur = pl.BlockSpec((2, None, ts, FB), lambda j, i: (0, j, i, 0))
    halo = pl.BlockSpec((2, None, HALO, FB), lambda j, i: (0, j, _prev_halo(i, ts), 0))
    w = pl.BlockSpec((2, None, 3, FB), lambda j, i: (0, j, 0, 0))
    b = pl.BlockSpec((2, None, 1, FB), lambda j, i: (0, j, 0, 0))
    return cur, halo, w, b


def _ffn_act(upre, fcw, fcb):
    s = upre.shape[1]
    ts = min(ROW_TILE, s)

    def body(u_ref, halo_ref, w_ref, b_ref, o_ref, ext_g, ext_u):
        i = pl.program_id(1)
        _ffn_stage(ext_g, u_ref.at[0], halo_ref.at[0], i, ts)
        _ffn_stage(ext_u, u_ref.at[1], halo_ref.at[1], i, ts)
        for r in range(0, ts, ROW_CHUNK):
            gate = b_ref[0] + _conv3_rows(ext_g, w_ref.at[0], r)
            up = b_ref[1] + _conv3_rows(ext_u, w_ref.at[1], r)
            o_ref[pl.ds(r, ROW_CHUNK), :] = (gate * _sig(gate) * up).astype(BF16)

    cur, halo, w, b = _ffn_specs(ts)
    u4 = upre.reshape(2, 4, s, FB)
    return pl.pallas_call(
        body, name="ffn_act", grid=(4, s // ts), out_shape=jax.ShapeDtypeStruct((4, s, FB), BF16),
        in_specs=[cur, halo, w, b], out_specs=pl.BlockSpec((None, ts, FB), lambda j, i: (j, i, 0)),
        scratch_shapes=[pltpu.VMEM((ts + HALO, FB), F32), pltpu.VMEM((ts + HALO, FB), F32)], compiler_params=_cp("parallel", "parallel"),
    )(u4, u4, fcw.reshape(2, 4, 3, FB), fcb.reshape(2, 4, 1, FB))


def _ffn_bwd(upre, dact, fcw, fcb):
    s = upre.shape[1]
    ts = min(ROW_TILE, s)
    n = s // ts
    te = ts + HALO

    def body(u_ref, halo_ref, nxt_ref, w_ref, b_ref, da_ref, nda_ref, dup_ref, db_ref, dw_ref, ext_g, ext_u, ext_da, du_g, du_u, acc):
        i = pl.program_id(1)

        @pl.when(i == 0)
        def _():
            db_ref[...] = jnp.zeros_like(db_ref)
            dw_ref[...] = jnp.zeros_like(dw_ref)

        more = (i < n - 1).astype(F32)
        for idx, ext in ((0, ext_g), (1, ext_u)):
            _ffn_stage(ext, u_ref.at[idx], halo_ref.at[idx], i, ts)
            ext[HALO + ts:HALO + te, :] = nxt_ref[idx].astype(F32) * more
        ext_da[0:ts, :] = da_ref[...].astype(F32)
        ext_da[ts:te, :] = nda_ref[...].astype(F32) * more
        acc[...] = jnp.zeros_like(acc)
        for r in range(0, te, ROW_CHUNK):
            rows = pl.ds(r, ROW_CHUNK)
            taps = [[ext[pl.ds(HALO - 2 + kk + r, ROW_CHUNK), :] for kk in range(3)] for ext in (ext_g, ext_u)]
            gate = b_ref[0] + w_ref[0, 0:1, :] * taps[0][0] + w_ref[0, 1:2, :] * taps[0][1] + w_ref[0, 2:3, :] * taps[0][2]
            up = b_ref[1] + w_ref[1, 0:1, :] * taps[1][0] + w_ref[1, 1:2, :] * taps[1][1] + w_ref[1, 2:3, :] * taps[1][2]
            sg = _sig(gate)
            da = ext_da[rows, :]
            dgate = da * up * sg * (1.0 + gate * (1.0 - sg))
            dup = da * gate * sg
            du_g[rows, :] = dgate
            du_u[rows, :] = dup
            if r < ts:
                acc[0] += dgate
                acc[1] += dup
                for kk in range(3):
                    acc[2 + kk] += dgate * taps[0][kk]
                    acc[5 + kk] += dup * taps[1][kk]
        for r in range(0, ts, ROW_CHUNK):
            for idx, du in ((0, du_g), (1, du_u)):
                dupre = (w_ref[idx, 2:3, :] * du[pl.ds(r, ROW_CHUNK), :] + w_ref[idx, 1:2, :] * du[pl.ds(r + 1, ROW_CHUNK), :]
                         + w_ref[idx, 0:1, :] * du[pl.ds(r + 2, ROW_CHUNK), :])
                dup_ref[idx, pl.ds(r, ROW_CHUNK), :] = dupre.astype(BF16)
        for idx in range(2):
            db_ref[idx] += _rowsum(acc[idx])
            for kk in range(3):
                dw_ref[idx, kk:kk + 1, :] += _rowsum(acc[2 + 3 * idx + kk])

    cur, halo, w, b = _ffn_specs(ts)
    nxt = pl.BlockSpec((2, None, HALO, FB), lambda j, i: (0, j, _next_halo(i, ts, n), 0))
    u4 = upre.reshape(2, 4, s, FB)
    dupre, db, dw = pl.pallas_call(
        body, name="ffn_bwd", grid=(4, n),
        out_shape=(jax.ShapeDtypeStruct((2, 4, s, FB), BF16), jax.ShapeDtypeStruct((2, 4, 1, FB), F32),
                   jax.ShapeDtypeStruct((2, 4, 3, FB), F32)),
        in_specs=[cur, halo, nxt, w, b, pl.BlockSpec((None, ts, FB), lambda j, i: (j, i, 0)),
                  pl.BlockSpec((None, HALO, FB), lambda j, i: (j, _next_halo(i, ts, n), 0))],
        out_specs=(cur, b, w),
        scratch_shapes=[pltpu.VMEM((te + HALO, FB), F32), pltpu.VMEM((te + HALO, FB), F32), pltpu.VMEM((te, FB), F32),
                        pltpu.VMEM((te, FB), F32), pltpu.VMEM((te, FB), F32), pltpu.VMEM((8, ROW_CHUNK, FB), F32)],
        compiler_params=_cp("parallel", "arbitrary"),
    )(u4, u4, u4, fcw.reshape(2, 4, 3, FB), fcb.reshape(2, 4, 1, FB), dact, dact)
    return dupre.reshape(NDEV, s, FB), db.reshape(NDEV, 1, FB), dw.reshape(NDEV, 3, FB)


def _softplus(v):
    e = jnp.exp(-jnp.abs(v))
    return jnp.maximum(v, 0.0) + jnp.where(e < 1e-4, e * (1.0 - 0.5 * e), jnp.log(1.0 + e))


def _ssd_consts():
    L = SSD_L
    r = lax.broadcasted_iota(jnp.int32, (L, L), 0)
    c = lax.broadcasted_iota(jnp.int32, (L, L), 1)
    tri = r >= c
    er = lax.broadcasted_iota(jnp.int32, (LANE, SSD_DIM), 0)
    ec = lax.broadcasted_iota(jnp.int32, (LANE, SSD_DIM), 1)
    expand = ((ec >= er * 64) & (ec < er * 64 + 64)).astype(F32)
    return tri, expand


def _ssd_conv4(ext, cw_ref, cb_ref):
    L = SSD_L
    pre = cb_ref[...] + cw_ref[0:1, :] * ext[pl.ds(HALO - 3, L), :]
    for kk in range(1, 4):
        pre = pre + cw_ref[kk:kk + 1, :] * ext[pl.ds(HALO - 3 + kk, L), :]
    return pre


def _ssd_common(xbc_ref, halo_ref, dt_ref, cw_ref, cb_ref, dtb_ref, alog_ref, ext, first):
    L = SSD_L
    tri, expand = _ssd_consts()
    ext[0:HALO, :] = halo_ref[...].astype(F32) * (1.0 - first.astype(F32))
    ext[HALO:HALO + L, :] = xbc_ref[...].astype(F32)
    pre = _ssd_conv4(ext, cw_ref, cb_ref)
    sg = _sig(pre)
    act = pre * sg
    lane = lax.broadcasted_iota(jnp.int32, (1, LANE), 1)
    m4 = lane < SSD_H
    raw = dt_ref[...].astype(F32) + dtb_ref[...]
    dtv = jnp.where(m4, _softplus(raw), 0.0)
    av = jnp.where(m4, -jnp.exp(alog_ref[...]), 0.0)
    adt = dtv * av
    acs = _dot01(adt, tri, x_first=False)
    acs_b = _dot01(acs, expand)
    dt_b = _dot01(dtv, expand)
    return dict(tri=tri, expand=expand, pre=pre, sg=sg, act=act, raw=raw, dtv=dtv, av=av, m4=m4, acs=acs, acs_b=acs_b,
                dt_b=dt_b, lane=lane)


def _head_terms(cm, h):
    L = SSD_L
    acs, tri = cm["acs"], cm["tri"]
    lane_l = lax.broadcasted_iota(jnp.int32, (L, LANE), 1)
    sub_l = lax.broadcasted_iota(jnp.int32, (LANE, L), 0)
    col = jnp.sum(jnp.where(lane_l == h, acs, 0.0), axis=1, keepdims=True)
    row = jnp.sum(jnp.where(sub_l == h, acs.T, 0.0), axis=0, keepdims=True)
    dec = jnp.where(tri, jnp.exp(jnp.where(tri, col - row, NEG)), 0.0)
    rowi = lax.broadcasted_iota(jnp.int32, (L, 1), 0)
    last = jnp.sum(jnp.where(rowi == L - 1, col, 0.0), axis=0, keepdims=True)
    dte = jnp.exp(last - col)
    return col, dec, last, dte


def _ssd_fwd(proj, cw, cb, dtb, alog, dvec, nw):
    s = proj.shape[0]
    L = SSD_L
    nc = s // L

    def body(z_ref, xbc_ref, halo_ref, dt_ref, cw_ref, cb_ref, dtb_ref, alog_ref, d_ref, nw_ref, y_ref, ypre_ref, st_ref, ext, state):
        i = pl.program_id(0)

        @pl.when(i == 0)
        def _():
            state[...] = jnp.zeros_like(state)

        cm = _ssd_common(xbc_ref, halo_ref, dt_ref, cw_ref, cb_ref, dtb_ref, alog_ref, ext, i == 0)
        act = cm["act"]
        xs = act[:, 0:256]
        bm = (act[:, 256:384], act[:, 384:512])
        cmat = (act[:, 512:640].astype(BF16), act[:, 640:768].astype(BF16))
        xdt = xs * cm["dt_b"]
        prev = state[...]
        st_ref[...] = prev
        prev_bf = prev.astype(BF16)
        gm = [_dot(cmat[g], bm[g].astype(BF16), NT) for g in range(2)]
        lane2 = lax.broadcasted_iota(jnp.int32, (1, SSD_DIM), 1)
        rows2 = lax.broadcasted_iota(jnp.int32, (SSD_DIM, 1), 0)
        ydiag = jnp.zeros((L, SSD_DIM), F32)
        contrib = jnp.zeros((SSD_DIM, LANE), F32)
        cd_rows = jnp.zeros((SSD_DIM, 1), F32)
        for h in range(SSD_H):
            g = h // 2
            col, dec, last, dte = _head_terms(cm, h)
            mh = (lane2 >= 64 * h) & (lane2 < 64 * h + 64)
            xm = jnp.where(mh, xdt, 0.0).astype(BF16)
            ydiag += _dot((gm[g] * dec).astype(BF16), xm)
            contrib += _dot(xm, (bm[g] * dte).astype(BF16), TN)
            cd_rows += jnp.where((rows2 >= 64 * h) & (rows2 < 64 * h + 64), jnp.exp(last), 0.0)
        yo = jnp.where(lane2 < 128, _dot(cmat[0], prev_bf, NT), _dot(cmat[1], prev_bf, NT))
        y = ydiag + yo * jnp.exp(cm["acs_b"]) + xs * d_ref[...]
        state[...] = prev * cd_rows + contrib
        ypre_ref[...] = y
        zz = z_ref[...].astype(F32)
        gt = y * zz * _sig(zz)
        y_ref[...] = (gt * lax.rsqrt(jnp.mean(gt * gt, axis=-1, keepdims=True) + EPS) * nw_ref[...]).astype(BF16)

    def vec(w):
        return pl.BlockSpec((1, w), lambda i: (0, 0))

    return pl.pallas_call(
        body, name="ssd_fwd", grid=(nc,),
        out_shape=(jax.ShapeDtypeStruct((s, SSD_DIM), BF16), jax.ShapeDtypeStruct((s, SSD_DIM), F32),
                   jax.ShapeDtypeStruct((nc, SSD_DIM, LANE), F32)),
        in_specs=[pl.BlockSpec((L, SSD_DIM), lambda i: (i, 5)), pl.BlockSpec((L, SSD_CONV), lambda i: (i, 2)),
                  pl.BlockSpec((HALO, SSD_CONV), lambda i: (_prev_halo(i, L), 2)), pl.BlockSpec((L, LANE), lambda i: (i, 18)),
                  pl.BlockSpec((4, SSD_CONV), lambda i: (0, 0)), vec(SSD_CONV), vec(LANE), vec(LANE), vec(SSD_DIM), vec(SSD_DIM)],
        out_specs=(pl.BlockSpec((L, SSD_DIM), lambda i: (i, 0)), pl.BlockSpec((L, SSD_DIM), lambda i: (i, 0)),
                   pl.BlockSpec((None, SSD_DIM, LANE), lambda i: (i, 0, 0))),
        scratch_shapes=[pltpu.VMEM((L + HALO, SSD_CONV), F32), pltpu.VMEM((SSD_DIM, LANE), F32)], compiler_params=_cp("arbitrary"),
    )(proj, proj, proj, proj, cw, cb, dtb, alog, dvec, nw)


def _ssd_bwd(proj, dcat, ypre, states, cw, cb, dtb, alog, dvec, nw):
    s = proj.shape[0]
    L = SSD_L
    nc = s // L

    def body(z_ref, xbc_ref, halo_ref, dt_ref, dy_ref, ypre_ref, st_ref, cw_ref, cb_ref, dtb_ref, alog_ref, d_ref, nw_ref,
             din_ref, dcw_ref, dcb_ref, ddtb_ref, dalog_ref, dd_ref, dnw_ref, ext, ext2, carry, dstate, ddl):
        i = pl.program_id(0)
        dz_ref, dxbc_ref = din_ref.at[:, 0:SSD_DIM], din_ref.at[:, SSD_DIM:SSD_DIM + SSD_CONV]
        ddt_ref = din_ref.at[:, SSD_DIM + SSD_CONV:SSD_DIM + SSD_CONV + LANE]
        r = nc - 1 - i

        @pl.when(i == 0)
        def _():
            for ref in (dcw_ref, dcb_ref, ddtb_ref, dalog_ref, dd_ref, dnw_ref, carry, dstate, ddl):
                ref[...] = jnp.zeros_like(ref)

        cm = _ssd_common(xbc_ref, halo_ref, dt_ref, cw_ref, cb_ref, dtb_ref, alog_ref, ext, r == 0)
        tri, expand, act = cm["tri"], cm["expand"], cm["act"]
        xs = act[:, 0:256]
        bm = (act[:, 256:384], act[:, 384:512])
        cmat = (act[:, 512:640], act[:, 640:768])
        bm_bf = [v.astype(BF16) for v in bm]
        cm_bf = [v.astype(BF16) for v in cmat]
        dt_b = cm["dt_b"]
        xdt = xs * dt_b
        xdt_bf = xdt.astype(BF16)
        ea_b = jnp.exp(cm["acs_b"])
        prev = st_ref[...]
        prev_bf = prev.astype(BF16)
        lane2 = lax.broadcasted_iota(jnp.int32, (1, SSD_DIM), 1)
        rows2 = lax.broadcasted_iota(jnp.int32, (SSD_DIM, 1), 0)
        lane_l = lax.broadcasted_iota(jnp.int32, (L, LANE), 1)
        rowi = lax.broadcasted_iota(jnp.int32, (L, 1), 0)

        y = ypre_ref[...]
        zz = z_ref[...].astype(F32)
        sz = _sig(zz)
        gt = y * zz * sz
        dgt, dwt = _rms_bwd_math(gt, nw_ref[...], dy_ref[...].astype(F32))
        dnw_ref[...] += _rowsum(dwt)
        dy = dgt * zz * sz
        dz_ref[...] = (dgt * y * sz * (1.0 + zz * (1.0 - sz))).astype(BF16)

        ddl[0:1, :] += _rowsum(dy * xs)
        dxs = dy * d_ref[...]

        yo = jnp.where(lane2 < 128, _dot(cm_bf[0], prev_bf, NT), _dot(cm_bf[1], prev_bf, NT))
        dacs_b = dy * yo * ea_b
        dyo = dy * ea_b
        dyo_g = (jnp.where(lane2 < 128, dyo, 0.0).astype(BF16), jnp.where(lane2 >= 128, dyo, 0.0).astype(BF16))
        dc = [_dot(dyo_g[g], prev_bf) for g in range(2)]
        dprev = _dot(dyo_g[0], cm_bf[0], TN) + _dot(dyo_g[1], cm_bf[1], TN)

        gm = [_dot(cm_bf[g], bm_bf[g], NT) for g in range(2)]
        dgm = [jnp.zeros((L, L), F32), jnp.zeros((L, L), F32)]
        db = [jnp.zeros((L, LANE), F32), jnp.zeros((L, LANE), F32)]
        dxdt = jnp.zeros((L, SSD_DIM), F32)
        dacs = jnp.zeros((L, LANE), F32)
        dlast = jnp.zeros((1, LANE), F32)
        cd_rows = jnp.zeros((SSD_DIM, 1), F32)
        dst = dstate[...]
        dst_bf = dst.astype(BF16)
        dsp = dst * prev
        ones = jnp.ones((L, LANE), F32)
        for h in range(SSD_H):
            g = h // 2
            col, dec, last, dte = _head_terms(cm, h)
            mh = (lane2 >= 64 * h) & (lane2 < 64 * h + 64)
            rh = (rows2 >= 64 * h) & (rows2 < 64 * h + 64)
            sc = gm[g] * dec
            xm = jnp.where(mh, xdt, 0.0).astype(BF16)
            dym = jnp.where(mh, dy, 0.0).astype(BF16)
            dsc = _dot(dym, xdt_bf, NT)
            dxdt += _dot(sc.astype(BF16), dym, TN)
            dgm[g] += dsc * dec
            dd = dsc * sc
            rs = jnp.sum(dd, axis=1, keepdims=True)
            cs = _dot01(dd, ones, TN)
            dacs += jnp.where(lane_l == h, rs - cs, 0.0)
            bd = (bm[g] * dte).astype(BF16)
            dxdt += jnp.where(mh, _dot(bd, dst_bf, NT), 0.0)
            dbd = _dot(xm, dst_bf)
            db[g] += dbd * dte
            tt = jnp.sum(dbd * bm[g], axis=1, keepdims=True) * dte
            dacs += jnp.where(lane_l == h, -tt, 0.0)
            cdh = jnp.exp(last)
            dcd = jnp.sum(jnp.sum(jnp.where(rh, dsp, 0.0), axis=1, keepdims=True), axis=0, keepdims=True)
            dlast += jnp.where(cm["lane"] == h, jnp.sum(tt, axis=0, keepdims=True) + dcd * cdh, 0.0)
            cd_rows += jnp.where(rh, cdh, 0.0)
        dacs += jnp.where(rowi == L - 1, dlast, 0.0)
        dacs += _dot01(dacs_b, expand, NT)
        dstate[...] = dprev + dst * cd_rows

        for g in range(2):
            dgb = dgm[g].astype(BF16)
            dc[g] += _dot(dgb, bm_bf[g])
            db[g] += _dot(dgb, cm_bf[g], TN)

        dadt = _dot01(dacs, tri, TN, x_first=False)
        ddtv = dadt * cm["av"] + _dot01(dxdt * xs, expand, NT)
        dalog_ref[...] += _rowsum(dadt * cm["dtv"]) * cm["av"]
        dxs += dxdt * dt_b
        draw = jnp.where(cm["m4"], ddtv * _sig(cm["raw"]), 0.0)
        ddtb_ref[...] += _rowsum(draw)
        ddt_ref[...] = draw.astype(BF16)

        dact = jnp.concatenate([dxs, db[0], db[1], dc[0], dc[1]], axis=1)
        sg, pre = cm["sg"], cm["pre"]
        dpre = dact * sg * (1.0 + pre * (1.0 - sg))
        dcb_ref[...] += _rowsum(dpre)
        for kk in range(4):
            dcw_ref[kk:kk + 1, :] += _rowsum(dpre * ext[pl.ds(HALO - 3 + kk, L), :])
        ext2[0:L, :] = dpre
        ext2[L:L + HALO, :] = carry[...]
        dx = cw_ref[3:4, :] * ext2[pl.ds(0, L), :]
        for kk in range(3):
            dx = dx + cw_ref[kk:kk + 1, :] * ext2[pl.ds(3 - kk, L), :]
        dxbc_ref[...] = dx.astype(BF16)
        carry[...] = dpre[0:HALO, :]

        @pl.when(i == nc - 1)
        def _():
            dd_ref[...] = _dot01(ddl[...], expand, NT)

    def vec(w):
        return pl.BlockSpec((1, w), lambda i: (0, 0))

    def rv(i):
        return nc - 1 - i

    return pl.pallas_call(
        body, name="ssd_bwd", grid=(nc,),
        out_shape=(jax.ShapeDtypeStruct((s, SSD_DIM + SSD_CONV + LANE), BF16),
                   jax.ShapeDtypeStruct((4, SSD_CONV), F32), jax.ShapeDtypeStruct((1, SSD_CONV), F32), jax.ShapeDtypeStruct((1, LANE), F32),
                   jax.ShapeDtypeStruct((1, LANE), F32), jax.ShapeDtypeStruct((8, LANE), F32), jax.ShapeDtypeStruct((1, SSD_DIM), F32)),
        in_specs=[pl.BlockSpec((L, SSD_DIM), lambda i: (rv(i), 5)), pl.BlockSpec((L, SSD_CONV), lambda i: (rv(i), 2)),
                  pl.BlockSpec((HALO, SSD_CONV), lambda i: (_prev_halo(rv(i), L), 2)), pl.BlockSpec((L, LANE), lambda i: (rv(i), 18)),
                  pl.BlockSpec((L, SSD_DIM), lambda i: (rv(i), 5)), pl.BlockSpec((L, SSD_DIM), lambda i: (rv(i), 0)),
                  pl.BlockSpec((None, SSD_DIM, LANE), lambda i: (rv(i), 0, 0)),
                  pl.BlockSpec((4, SSD_CONV), lambda i: (0, 0)), vec(SSD_CONV), vec(LANE), vec(LANE), vec(SSD_DIM), vec(SSD_DIM)],
        out_specs=(pl.BlockSpec((L, SSD_DIM + SSD_CONV + LANE), lambda i: (rv(i), 0)), pl.BlockSpec((4, SSD_CONV), lambda i: (0, 0)), vec(SSD_CONV),
                   vec(LANE), vec(LANE), pl.BlockSpec((8, LANE), lambda i: (0, 0)), vec(SSD_DIM)),
        scratch_shapes=[pltpu.VMEM((L + HALO, SSD_CONV), F32), pltpu.VMEM((L + HALO, SSD_CONV), F32), pltpu.VMEM((HALO, SSD_CONV), F32),
                        pltpu.VMEM((SSD_DIM, LANE), F32), pltpu.VMEM((8, SSD_DIM), F32)],
        compiler_params=_cp("arbitrary"),
    )(proj, proj, proj, proj, dcat, ypre, states, cw, cb, dtb, alog, dvec, nw)


def _adamw(parts, w, m, v, name):
    nl, r, c = w.shape
    tr = r
    for cand in (256, 128, 64, 32, 16, 8):
        if r % cand == 0 and (cand * c * 4) <= 2 * 1024 * 1024:
            tr = cand
            break
    c1 = 1.0 - B1 ** STEP
    c2 = 1.0 - B2 ** STEP

    nrow = r // tr

    def update(p_ref, w_ref, m_ref, v_ref, g_ref, d_ref, nm_ref, nv_ref):
        g = p_ref[0].astype(F32)
        for dev in range(1, NDEV):
            g = g + p_ref[dev].astype(F32)
        mn = B1 * m_ref[...] + (1.0 - B1) * g
        vn = B2 * v_ref[...] + (1.0 - B2) * (g * g)
        g_ref[...] = g
        nm_ref[...] = mn
        nv_ref[...] = vn
        d_ref[...] = -LR * ((mn / c1) / (jnp.sqrt(vn / c2) + AEPS) + WD * w_ref[...])

    def body(*refs):
        for k in range(nl):
            pl.when(pl.program_id(0) == k)(functools.partial(update, refs[k], *refs[nl:]))

    def part_spec(k):
        return pl.BlockSpec((NDEV, tr, c), lambda l, i: (0, jnp.where(l == k, i, jnp.where(l < k, 0, nrow - 1)), 0))

    blk = pl.BlockSpec((None, tr, c), lambda l, i: (l, i, 0))
    out = jax.ShapeDtypeStruct((nl, r, c), F32)
    return pl.pallas_call(
        body, name=name, grid=(nl, nrow), out_shape=(out, out, out, out),
        in_specs=[part_spec(k) for k in range(nl)] + [blk, blk, blk], out_specs=(blk, blk, blk, blk),
        compiler_params=_cp("arbitrary", "arbitrary"),
    )(*parts, w, m, v)


def _pad_win(w):
    z = lambda n: jnp.zeros((w.shape[0], n), w.dtype)
    return jnp.concatenate([w[:, :384], z(64), w[:, 384:416], z(32), w[:, 416:], z(124)], axis=1)


def _unpad_win(g_mla, g_sc, g_ssd):
    return jnp.concatenate([g_mla[:, :384], g_mla[:, 448:480], g_sc, g_ssd[:, :1028]], axis=1)


def _pad_wout(w):
    att = jnp.pad(w[:512].reshape(HEADS, 64, D), ((0, 0), (64, 0), (0, 0))).reshape(HEADS * LANE, D)
    return jnp.concatenate([att, w[512:]], axis=0)


def _unpad_wout(g_att, g_conv, g_ssd):
    att = g_att.reshape(HEADS, LANE, D)[:, 64:, :].reshape(512, D)
    return jnp.concatenate([att, g_conv, g_ssd], axis=0)


def _lanes(v, n=LANE):
    return jnp.pad(v, (0, n - v.shape[0])).reshape(1, n)


def _prep_ffn(big):
    return {"wout": _pad_wout(big["w_out"].reshape(1024, D)), "wup": big["ffn_w_up"], "fcw": big["ffn_conv_w"].astype(F32),
            "wdown": big["ffn_w_down"].reshape(4, FB, D)}


def _prep_layer(big, small, l):
    p = _prep_ffn(big) if "w_out" in big else {}
    p["win"] = _pad_win(big["w_in"].reshape(D, 2212))
    p["wq"] = jnp.pad(big["mla_w_q_up"], ((0, 0), (0, 0), (0, LANE - 96)))
    p["wkv"] = big["mla_w_kv_up"]
    p["scw"] = big["sc_conv_w"].astype(F32).transpose(1, 0, 2).reshape(3, SC)
    p["ssdcw"] = big["ssd_conv_w"].astype(F32).transpose(1, 0, 2).reshape(4, SSD_CONV)
    for nm in ("norm_mix_pre", "norm_mix_post", "norm_ffn_pre", "norm_ffn_post", "mla_q_norm", "mla_kv_norm", "ssd_conv_b", "ssd_norm"):
        p[nm] = small[nm][l].reshape(1, -1)
    p["dtb"] = _lanes(small["ssd_dt_bias"][l])
    p["alog"] = _lanes(small["ssd_a_log"][l])
    p["dvec"] = jnp.repeat(small["ssd_d"][l], 64).reshape(1, SSD_DIM)
    p["fcb"] = small["ffn_conv_b"][l].reshape(NDEV, 1, FB)
    return p


def _rope_tables(positions):
    inv_freq = 1.0 / (ROPE_THETA ** (jnp.arange(0, ROPE, 2, dtype=F32) / ROPE))
    ang = positions.astype(F32)[:, None] * inv_freq
    cos, sin = jnp.cos(ang), jnp.sin(ang)
    s = positions.shape[0]
    z = lambda n: jnp.zeros((s, n), F32)
    tc = jnp.concatenate([jnp.ones((s, 64), F32), cos, cos, z(32)], axis=1)
    ta = jnp.concatenate([z(64), -sin, z(48)], axis=1)
    tb = jnp.concatenate([z(80), sin, z(32)], axis=1)
    return tc, ta, tb


def _layer_fwd(xv, p, tabs, prefetch=None, prep_rest=None, h=None, next_norm=None):
    if h is None:
        h = _rms(xv, p["norm_mix_pre"], BF16, "rms_pre")
    proj = _mm_rows("in_proj", h, p["win"], BF16, NN)
    q, k, kv = _mla_prep(proj, tabs, p["mla_q_norm"], p["mla_kv_norm"], p["wq"], p["wkv"])
    o, lse, gathered = _flash_fwd(q, k, kv, prefetch)
    if prep_rest is not None:
        p = {**p, **prep_rest(gathered)}
    yconv = _sconv_fwd(proj, p["scw"])
    yssd, ypre, states = _ssd_fwd(proj, p["ssdcw"], p["ssd_conv_b"], p["dtb"], p["alog"], p["dvec"], p["ssd_norm"])
    mixed = _mm_sum("out_proj", [o, yconv, yssd], [p["wout"][:HEADS * LANE], p["wout"][HEADS * LANE:HEADS * LANE + SC],
                                                  p["wout"][HEADS * LANE + SC:]], NN)
    x1, h2 = _add_rms(xv, mixed, p["norm_mix_post"], "add_rms", p["norm_ffn_pre"])
    upre = _mm_up(h2, p["wup"])
    act = _ffn_act(upre, p["fcw"], p["fcb"])
    f = _mm_down(act, p["wdown"])
    x2 = _add_rms(x1, f, p["norm_ffn_post"], "add_rms", next_norm)
    saved = dict(x=xv, h=h, proj=proj, q=q, k=k, kv=kv, lse=lse, ypre=ypre, states=states, o=o, yconv=yconv, yssd=yssd, mixed=mixed, x1=x1, h2=h2,
                 upre=upre, act=act, f=f)
    return x2, saved, p, gathered


def _pack_grads(grads, group_ids):
    return [_group_pack(GROUPS[gi], lambda n: grads[n].reshape((NDEV,) + _rows2(n, True)), (NDEV,)) for gi in group_ids]


def _layer_bwd(dx2, sv, p, tabs, exchange=False, pending=None, head=None, below=None):
    df, g_nfpo = head if head is not None else _rms_bwd(sv["f"], p["norm_ffn_post"], dx2, None, BF16, "rms_bwd_post")
    dact = _mm_dact(df, p["wdown"])
    g_wdown = _mm_dwdown(sv["act"], df)
    dupre, g_fcb, g_fcw = _ffn_bwd(sv["upre"], dact, p["fcw"], p["fcb"])
    dh2 = _mm_dh2(dupre, p["wup"])
    g_wup = _mm_dwup(sv["h2"], dupre)
    dx1, dmixed, g_nfp, g_nmpo = _rms_bwd2(sv["x1"], p["norm_ffn_pre"], dh2, dx2, sv["mixed"], p["norm_mix_post"])
    dcat = _mm_rows("dcat", dmixed, p["wout"], BF16, NT)
    g_wout = _mm_dwout([sv["o"], sv["yconv"], sv["yssd"]], dmixed)
    big = {
        "w_out": _unpad_wout(*g_wout).reshape(NDEV, 128, D),
        "ffn_w_up": g_wup,
        "ffn_conv_w": g_fcw.astype(BF16),
        "ffn_w_down": g_wdown.reshape(NDEV, 352, D),
    }
    outgoing = _pack_grads(big, FFN_SIDE) + (pending or []) if exchange else None
    dq, dk, dv, received = _flash_bwd(sv["q"], sv["k"], sv["kv"], sv["o"], dcat, sv["lse"], outgoing)
    d_mla, g_wq, g_wkv, g_qn, g_kvn = _mla_prep_bwd(sv["proj"], tabs, p["mla_q_norm"], p["mla_kv_norm"], p["wq"], p["wkv"], dq, dk, dv)
    d_sc, g_scw = _sconv_bwd(sv["proj"], dcat, p["scw"])
    d_ssd, g_cw, g_cb, g_dtb, g_alog, g_d, g_nw = _ssd_bwd(
        sv["proj"], dcat, sv["ypre"], sv["states"], p["ssdcw"], p["ssd_conv_b"], p["dtb"], p["alog"], p["dvec"], p["ssd_norm"])
    dparts = [d_mla, d_sc, d_ssd]
    dh = _mm_sum("dh", dparts, [p["win"][:, 0:512], p["win"][:, 512:1280], p["win"][:, 1280:PW]], NT)
    g_win = _mm_dwin(sv["h"], dparts)
    big.update({
        "w_in": _unpad_win(*g_win).reshape(NDEV, 128, 2212),
        "mla_w_q_up": g_wq[:, :, :96].astype(BF16),
        "mla_w_kv_up": g_wkv.astype(BF16),
        "sc_conv_w": g_scw.reshape(3, NDEV, 32).transpose(1, 0, 2).astype(BF16),
        "ssd_conv_w": g_cw.reshape(4, NDEV, 96).transpose(1, 0, 2).astype(BF16),
    })
    head_below, last_received = None, None
    if below is not None:
        dx, df_below, g_nmp, g_below = _rms_bwd2(sv["x"], p["norm_mix_pre"], dh, dx1, *below)
        head_below = (df_below, g_below)
    elif exchange:
        dx, g_nmp, last_received = _rms_bwd(sv["x"], p["norm_mix_pre"], dh, dx1, F32, "rms_bwd_pre", _pack_grads(big, ATT_SIDE))
    else:
        dx, g_nmp = _rms_bwd(sv["x"], p["norm_mix_pre"], dh, dx1, F32, "rms_bwd_pre")
    small = {
        "norm_mix_pre": g_nmp[0], "norm_mix_post": g_nmpo[0], "norm_ffn_pre": g_nfp[0], "norm_ffn_post": g_nfpo[0],
        "mla_q_norm": g_qn[0], "mla_kv_norm": g_kvn[0], "ssd_conv_b": g_cb[0], "ssd_dt_bias": g_dtb[0, :SSD_H],
        "ssd_a_log": g_alog[0, :SSD_H], "ssd_d": g_d[0, :SSD_H], "ssd_norm": g_nw[0], "ffn_conv_b": g_fcb.reshape(-1),
    }
    return dx, big, small, received, head_below, last_received


def _local_step(xv, positions, target, layers):
    tabs = _rope_tables(positions)
    saved = []
    for p in layers:
        xv, sv, _, _ = _layer_fwd(xv, p, tabs)
        saved.append(sv)
    loss, dx = _loss_head(xv, target)
    bigs, smalls = [None] * DEPTH, [None] * DEPTH
    head = None
    for l in reversed(range(len(layers))):
        below = (saved[l - 1]["f"], layers[l - 1]["norm_ffn_post"]) if l > 0 else None
        dx, bigs[l], smalls[l], _, head, _ = _layer_bwd(dx, saved[l], layers[l], tabs, head=head, below=below)
    return loss[0, 0], dx, bigs, smalls


def _rows2(n, layer=False):
    shape = SHAPES[n][1:] if layer else SHAPES[n]
    return (math.prod(shape[:-1]), shape[-1])


def _group_pack(group, get, lead):
    width, names = group
    pieces = []
    for n in names:
        rows, cols = _rows2(n, True)
        pad = [(0, 0)] * len(lead) + [(0, -rows % 16), (0, width - cols)]
        pieces.append(jnp.pad(get(n), pad))
    return pieces[0] if len(pieces) == 1 else jnp.concatenate(pieces, axis=len(lead))


def _group_unpack(group, buf):
    _, names = group
    res, off = {}, 0
    for n in names:
        rows, cols = _rows2(n, True)
        res[n] = buf[:, off:off + rows, :cols]
        off += rows + (-rows % 16)
    return res


def kernel(x, positions, norm_mix_pre, norm_mix_post, norm_ffn_pre, norm_ffn_post, w_in, mla_q_norm, mla_w_q_up, mla_kv_norm, mla_w_kv_up, sc_conv_w, ssd_conv_w, ssd_conv_b, ssd_dt_bias, ssd_a_log, ssd_d, ssd_norm, w_out, ffn_w_up, ffn_conv_w, ffn_conv_b, ffn_w_down, loss_target, m_norm_mix_pre, m_norm_mix_post, m_norm_ffn_pre, m_norm_ffn_post, m_w_in, m_mla_q_norm, m_mla_w_q_up, m_mla_kv_norm, m_mla_w_kv_up, m_sc_conv_w, m_ssd_conv_w, m_ssd_conv_b, m_ssd_dt_bias, m_ssd_a_log, m_ssd_d, m_ssd_norm, m_w_out, m_ffn_w_up, m_ffn_conv_w, m_ffn_conv_b, m_ffn_w_down, v_norm_mix_pre, v_norm_mix_post, v_norm_ffn_pre, v_norm_ffn_post, v_w_in, v_mla_q_norm, v_mla_w_q_up, v_mla_kv_norm, v_mla_w_kv_up, v_sc_conv_w, v_ssd_conv_w, v_ssd_conv_b, v_ssd_dt_bias, v_ssd_a_log, v_ssd_d, v_ssd_norm, v_w_out, v_ffn_w_up, v_ffn_conv_w, v_ffn_conv_b, v_ffn_w_down):
    given = dict(locals())
    w = {n: given[n] for n in WEIGHTS}
    m = {n: given["m_" + n] for n in WEIGHTS}
    v = {n: given["v_" + n] for n in WEIGHTS}

    def shards(l, group_ids):
        return [_group_pack(GROUPS[gi], lambda n: w[n][l].astype(BF16).reshape(_rows2(n, True)), ()) for gi in group_ids]

    def unpacked(bufs, group_ids):
        big = {}
        for gi, buf in zip(group_ids, bufs):
            for n, piece in _group_unpack(GROUPS[gi], buf).items():
                big[n] = piece.reshape((NDEV,) + SHAPES[n][1:])
        return big

    small_w = {n: w[n] for n, _ in SMALL}
    tabs = _rope_tables(positions[0])
    xv, h, layers, saved = x[0], None, [], []
    att = _all_gather(shards(0, ATT_SIDE), "gather_weights")
    for l in range(DEPTH):
        prefetch = shards(l, FFN_SIDE) + (shards(l + 1, ATT_SIDE) if l + 1 < DEPTH else [])
        nxt = w["norm_mix_pre"][l + 1].reshape(1, D) if l + 1 < DEPTH else None
        xv, sv, p, gathered = _layer_fwd(xv, _prep_layer(unpacked(att, ATT_SIDE), small_w, l), tabs, prefetch,
                                         lambda got: _prep_ffn(unpacked(got[:len(FFN_SIDE)], FFN_SIDE)), h, nxt)
        xv, h = xv if nxt is not None else (xv, None)
        att = gathered[len(FFN_SIDE):]
        layers.append(p)
        saved.append(sv)
    loss, dx = _loss_head(xv, loss_target[0])
    loss = lax.psum(loss[0, 0], ("x", "y", "c"))

    smalls, pending, head = [None] * DEPTH, None, None
    recvs = [[None] * len(GROUPS) for _ in range(DEPTH)]
    for l in reversed(range(DEPTH)):
        below = (saved[l - 1]["f"], layers[l - 1]["norm_ffn_post"]) if l > 0 else None
        dx, grads, smalls[l], received, head, last = _layer_bwd(dx, saved[l], layers[l], tabs, True, pending, head, below)
        for pos, gi in enumerate(FFN_SIDE):
            recvs[l][gi] = received[pos]
        if pending is not None:
            for pos, gi in enumerate(ATT_SIDE):
                recvs[l + 1][gi] = received[len(FFN_SIDE) + pos]
        pending = _pack_grads(grads, ATT_SIDE) if l > 0 else None
    for gi, buf in zip(ATT_SIDE, last):
        recvs[0][gi] = buf
    out = {}
    for gi, g in enumerate(GROUPS):
        per_layer = [_group_unpack(g, recvs[l][gi]) for l in range(DEPTH)]
        for n in g[1]:
            out[n] = _adamw([per_layer[l][n] for l in range(DEPTH)], w[n], m[n], v[n], "adamw_" + n)

    total = sum(width for _, width in SMALL)
    padded = -(-total // (8 * LANE)) * 8 * LANE
    pk = lambda a: jnp.pad(a, ((0, 0), (0, padded - total))).reshape(1, DEPTH * padded // LANE, LANE)
    sflat = jnp.stack([jnp.concatenate([smalls[l][n] for n, _ in SMALL]) for l in range(DEPTH)])
    sparts = _all_gather([pk(sflat)[0]], "gather_small_grads")[0]
    pw = lambda d: pk(jnp.concatenate([d[n] for n, _ in SMALL], axis=1))
    res = _adamw([sparts], pw(w), pw(m), pw(v), "adamw_small")
    off = 0
    for n, width in SMALL:
        out[n] = [a.reshape(DEPTH, padded)[:, off:off + width] for a in res]
        off += width

    return (loss, dx[None], *[out[n][0] for n in WEIGHTS], *[out[n][1] for n in WEIGHTS],
            *[out[n][2] for n in WEIGHTS], *[out[n][3] for n in WEIGHTS])
```
